```python
import math
import jax, jax.numpy as jnp
from jax import lax
import numpy as np

D_MODEL = 1024
BATCH = 4
SEQ = 4096
DEPTH = 1

GM_WIDTH = D_MODEL // 2
GM_GROUPS = 8
GM_CHUNK = 128
DIL_PAIRS = ((128, 1), (512, 4), (2048, 16))
N_DIL = len(DIL_PAIRS)
HEADS_PER_GROUP = 8
HEAD_DIM = 64
ATT_WIDTH = HEADS_PER_GROUP * HEAD_DIM
N_ATT_HEADS = N_DIL * HEADS_PER_GROUP
ATT_BLOCK = 128
NEG_INF = -1e30
REL_BUCKETS = 32
REL_MAX_EXACT = 16
REL_MAX_DIST = 2048
N_EXPERTS = 32
TOP_K = 4
D_EXPERT = D_MODEL
SWIGLU_LIMIT = 7.0
SWIGLU_ALPHA = 1.702
MOE_BLOCK = 128
DN_ALPHA = (2 * DEPTH) ** 0.25
DN_BETA = (8 * DEPTH) ** -0.25
LN_EPS = 1e-5
UV_COLS = 2 * GM_WIDTH
QKV_COLS = N_DIL * 3 * ATT_WIDTH
GATE_COLS = 2 * D_MODEL
IN_COLS = UV_COLS + QKV_COLS + GATE_COLS

kernel_name = "hybrid_gmlp_dilated_attn_moe_deepnorm"


def layer_norm(x, g=None, b=None):
    xf = x.astype(jnp.float32)
    mu = xf.mean(-1, keepdims=True)
    var = jnp.square(xf - mu).mean(-1, keepdims=True)
    y = (xf - mu) * lax.rsqrt(var + LN_EPS)
    if g is not None:
        y = y * g + b
    return y.astype(x.dtype)


def t5_bucket(dist):
    d = dist.astype(jnp.float32)
    large = REL_MAX_EXACT + jnp.log(jnp.maximum(d, float(REL_MAX_EXACT)) / REL_MAX_EXACT) / math.log(REL_MAX_DIST / REL_MAX_EXACT) * (REL_BUCKETS - REL_MAX_EXACT)
    large = jnp.minimum(large.astype(jnp.int32), REL_BUCKETS - 1)
    return jnp.where(dist < REL_MAX_EXACT, dist, large)


def chunked_spatial_gating(u, v, ln_g, ln_b, w_s, b_s):
    B, S, _ = u.shape
    nc = S // GM_CHUNK
    v = layer_norm(v, ln_g, ln_b)
    vc = v.reshape(B, nc, GM_CHUNK, GM_GROUPS, GM_WIDTH // GM_GROUPS)
    causal = jnp.tril(jnp.ones((GM_CHUNK, GM_CHUNK), dtype=bool))
    ws = jnp.where(causal[None], w_s, 0)
    s = jnp.einsum('gts,bnsgc->bntgc', ws, vc) + b_s.T[:, :, None]
    return u * s.reshape(B, S, GM_WIDTH)


def dilated_group_attention(q, k, v, bias_table, window, dilation):
    B, S, H, Dh = q.shape
    span = window // dilation
    L = S // dilation
    nb = -(-L // ATT_BLOCK)
    Lp = nb * ATT_BLOCK

    def to_sub(t):
        return t.reshape(B, L, dilation, H, Dh).transpose(0, 2, 1, 3, 4)

    qs, ks, vs = to_sub(q), to_sub(k), to_sub(v)
    qb = jnp.pad(qs, ((0, 0), (0, 0), (0, Lp - L), (0, 0), (0, 0))).reshape(B, dilation, nb, ATT_BLOCK, H, Dh)

    def key_blocks(t):
        tp = jnp.pad(t, ((0, 0), (0, 0), (ATT_BLOCK, Lp - L), (0, 0), (0, 0)))
        prev = tp[:, :, :Lp].reshape(B, dilation, nb, ATT_BLOCK, H, Dh)
        cur = tp[:, :, ATT_BLOCK:].reshape(B, dilation, nb, ATT_BLOCK, H, Dh)
        return jnp.concatenate([prev, cur], axis=3)

    kb, vb = key_blocks(ks), key_blocks(vs)
    qi = jnp.arange(ATT_BLOCK)[:, None]
    ki = jnp.arange(2 * ATT_BLOCK)[None, :]
    didx = qi + ATT_BLOCK - ki
    band = (didx >= 0) & (didx <= span)
    first = (jnp.arange(nb) == 0)[:, None, None]
    valid = band[None] & ~(first & (ki < ATT_BLOCK)[None])
    bucket = t5_bucket(jnp.clip(didx, 0, None) * dilation)
    bias = bias_table[bucket].astype(jnp.float32).transpose(2, 0, 1)

    logits = jnp.einsum('brnqhd,brnkhd->brnhqk', qb, kb, preferred_element_type=jnp.float32) * (Dh ** -0.5) + bias
    logits = jnp.where(valid[None, None, :, None], logits, NEG_INF)
    m = logits.max(-1, keepdims=True)
    p = jnp.exp(logits - m)
    den = p.sum(-1, keepdims=True)
    o = jnp.einsum('brnhqk,brnkhd->brnqhd', (p / den).astype(v.dtype), vb)
    lse = (m + jnp.log(den))[..., 0]
    o = o.reshape(B, dilation, Lp, H, Dh)[:, :, :L].transpose(0, 2, 1, 3, 4).reshape(B, S, H, Dh)
    lse = lse.transpose(0, 1, 2, 4, 3).reshape(B, dilation, Lp, H)[:, :, :L].transpose(0, 2, 1, 3).reshape(B, S, H)
    return o, lse


def moe_ffn(h, w_router, b_router, w_gate, b_gate, w_up, b_up, w_down, b_down):
    B, S, D = h.shape
    T = B * S
    xt = h.reshape(T, D)
    logits = (xt @ w_router).astype(jnp.float32) + b_router
    top_v, top_e = lax.top_k(logits, TOP_K)
    top_w = jax.nn.softmax(top_v, axis=-1)
    A = T * TOP_K
    e_flat = top_e.reshape(A)
    tok_flat = jnp.arange(A, dtype=jnp.int32) // TOP_K
    w_flat = top_w.reshape(A)
    order = jnp.argsort(e_flat)
    e_sorted = e_flat[order]
    counts = jnp.bincount(e_flat, length=N_EXPERTS)
    starts = jnp.cumsum(counts) - counts
    pcounts = (counts + MOE_BLOCK - 1) // MOE_BLOCK * MOE_BLOCK
    pends = jnp.cumsum(pcounts)
    pstarts = pends - pcounts
    dest = pstarts[e_sorted] + jnp.arange(A, dtype=jnp.int32) - starts[e_sorted]
    nblk = A // MOE_BLOCK + N_EXPERTS
    R = nblk * MOE_BLOCK
    row_tok = jnp.zeros((R,), jnp.int32).at[dest].set(tok_flat[order])
    row_w = jnp.zeros((R,), jnp.float32).at[dest].set(w_flat[order])
    blk_e = jnp.minimum(jnp.searchsorted(pends, jnp.arange(nblk, dtype=jnp.int32) * MOE_BLOCK, side='right'), N_EXPERTS - 1)

    def expert_block(args):
        tok, e = args
        xb = xt[tok]
        g = xb @ w_gate[e] + b_gate[e]
        u = xb @ w_up[e] + b_up[e]
        g = jnp.minimum(g, SWIGLU_LIMIT)
        u = jnp.clip(u, -SWIGLU_LIMIT, SWIGLU_LIMIT)
        act = (u + 1) * (g * jax.nn.sigmoid(SWIGLU_ALPHA * g))
        return act @ w_down[e] + b_down[e]

    yb = lax.map(expert_block, (row_tok.reshape(nblk, MOE_BLOCK), blk_e))
    y = jnp.zeros((T, D), jnp.float32).at[row_tok].add(yb.reshape(R, D).astype(jnp.float32) * row_w[:, None])
    return y.astype(h.dtype).reshape(B, S, D)


def setup_inputs(seed: int = 0) -> dict:
    key = jax.random.key(seed)
    ks = jax.random.split(key, 32)
    nrm = jax.random.normal
    f32 = jnp.float32
    inp = {}
    inp['x'] = nrm(ks[0], (BATCH, SEQ, D_MODEL), f32)
    inp['c'] = nrm(ks[1], (BATCH, D_MODEL), f32)
    inp['w_ada'] = nrm(ks[2], (DEPTH, D_MODEL, 6 * D_MODEL), f32) * (0.5 * D_MODEL ** -0.5)
    inp['b_ada'] = nrm(ks[3], (DEPTH, 6 * D_MODEL), f32) * 0.02
    inp['w_in'] = nrm(ks[4], (DEPTH, D_MODEL, IN_COLS), f32) * D_MODEL ** -0.5
    inp['gm_ln_g'] = 1.0 + 0.02 * nrm(ks[5], (DEPTH, GM_WIDTH), f32)
    inp['gm_ln_b'] = 0.02 * nrm(ks[6], (DEPTH, GM_WIDTH), f32)
    inp['gm_w_s'] = nrm(ks[7], (DEPTH, GM_GROUPS, GM_CHUNK, GM_CHUNK), f32) * (0.5 * GM_CHUNK ** -0.5)
    inp['gm_b_s'] = 1.0 + 0.02 * nrm(ks[8], (DEPTH, GM_GROUPS, GM_CHUNK), f32)
    inp['w_branch_a'] = nrm(ks[9], (DEPTH, GM_WIDTH, D_MODEL), f32) * GM_WIDTH ** -0.5
    inp['w_branch_b'] = nrm(ks[10], (DEPTH, ATT_WIDTH, D_MODEL), f32) * ATT_WIDTH ** -0.5
    inp['w_out'] = nrm(ks[11], (DEPTH, D_MODEL, D_MODEL), f32) * (DN_BETA * D_MODEL ** -0.5)
    inp['rel_bias'] = 0.5 * nrm(ks[12], (REL_BUCKETS, N_ATT_HEADS), f32)
    inp['ln1_g'] = 1.0 + 0.02 * nrm(ks[13], (DEPTH, D_MODEL), f32)
    inp['ln1_b'] = 0.02 * nrm(ks[14], (DEPTH, D_MODEL), f32)
    inp['w_router'] = nrm(ks[15], (DEPTH, D_MODEL, N_EXPERTS), f32) * D_MODEL ** -0.5
    inp['b_router'] = 0.01 * nrm(ks[16], (DEPTH, N_EXPERTS), f32)
    inp['w_gate'] = nrm(ks[17], (DEPTH, N_EXPERTS, D_MODEL, D_EXPERT), f32) * D_MODEL ** -0.5
    inp['b_gate'] = 0.02 * nrm(ks[18], (DEPTH, N_EXPERTS, D_EXPERT), f32)
    inp['w_up'] = nrm(ks[19], (DEPTH, N_EXPERTS, D_MODEL, D_EXPERT), f32) * D_MODEL ** -0.5
    inp['b_up'] = 0.02 * nrm(ks[20], (DEPTH, N_EXPERTS, D_EXPERT), f32)
    inp['w_down'] = nrm(ks[21], (DEPTH, N_EXPERTS, D_EXPERT, D_MODEL), f32) * (DN_BETA * D_EXPERT ** -0.5)
    inp['b_down'] = 0.02 * nrm(ks[22], (DEPTH, N_EXPERTS, D_MODEL), f32)
    inp['ln2_g'] = 1.0 + 0.02 * nrm(ks[23], (DEPTH, D_MODEL), f32)
    inp['ln2_b'] = 0.02 * nrm(ks[24], (DEPTH, D_MODEL), f32)
    return inp


def reference(x, c, w_ada, b_ada, w_in, gm_ln_g, gm_ln_b, gm_w_s, gm_b_s, w_branch_a, w_branch_b, w_out, rel_bias, ln1_g, ln1_b, w_router, b_router, w_gate, b_gate, w_up, b_up, w_down, b_down, ln2_g, ln2_b):
    B, S, _ = x.shape
    for l in range(DEPTH):
        mod = jax.nn.silu(c) @ w_ada[l] + b_ada[l]
        sh1, sc1, g1, sh2, sc2, g2 = jnp.split(mod[:, None, :], 6, axis=-1)

        h = layer_norm(x) * (1 + sc1) + sh1
        proj = h @ w_in[l]
        uv, qkv, gates = jnp.split(proj, [UV_COLS, UV_COLS + QKV_COLS], axis=-1)
        u, v = jnp.split(jax.nn.gelu(uv), 2, axis=-1)
        y_a = chunked_spatial_gating(u, v, gm_ln_g[l], gm_ln_b[l], gm_w_s[l], gm_b_s[l])
        qkv = qkv.reshape(B, S, N_DIL, 3, HEADS_PER_GROUP, HEAD_DIM)
        outs, lses = [], []
        for g, (win, dil) in enumerate(DIL_PAIRS):
            o, lse = dilated_group_attention(qkv[:, :, g, 0], qkv[:, :, g, 1], qkv[:, :, g, 2],
                                             rel_bias[:, g * HEADS_PER_GROUP:(g + 1) * HEADS_PER_GROUP], win, dil)
            outs.append(o)
            lses.append(lse)
        wts = jax.nn.softmax(jnp.stack(lses, axis=0), axis=0)
        y_b = jnp.sum(wts[..., None] * jnp.stack(outs, axis=0).astype(jnp.float32), axis=0)
        y_b = y_b.astype(x.dtype).reshape(B, S, ATT_WIDTH)
        gate_a, gate_b = jnp.split(jax.nn.sigmoid(gates), 2, axis=-1)
        merged = gate_a * (y_a @ w_branch_a[l]) + gate_b * (y_b @ w_branch_b[l])
        mix = merged @ w_out[l]
        x = layer_norm(DN_ALPHA * x + g1 * mix, ln1_g[l], ln1_b[l])

        h = layer_norm(x) * (1 + sc2) + sh2
        ffn = moe_ffn(h, w_router[l], b_router[l], w_gate[l], b_gate[l], w_up[l], b_up[l], w_down[l], b_down[l])
        x = layer_norm(DN_ALPHA * x + g2 * ffn, ln2_g[l], ln2_b[l])
    return x
```

```python
import functools
import math

import jax
import jax.numpy as jnp
from jax import lax
from jax.experimental import pallas as pl
from jax.experimental.pallas import tpu as pltpu

f32 = jnp.float32
bf16 = jnp.bfloat16
i32 = jnp.int32

D_MODEL = 1024
GM_WIDTH = 512
GM_GROUPS = 8
GM_CHUNK = 128
DIL_PAIRS = ((128, 1), (512, 4), (2048, 16))
N_DIL = 3
HEADS_PER_GROUP = 8
HEAD_DIM = 64
ATT_WIDTH = 512
ATT_BLOCK = 128
NEG_INF = -1e30
REL_BUCKETS = 32
REL_MAX_EXACT = 16
REL_MAX_DIST = 2048
N_EXPERTS = 32
TOP_K = 4
SWIGLU_LIMIT = 7.0
SWIGLU_ALPHA = 1.702
MOE_BLOCK = 128
DEPTH = 1
DN_ALPHA = (2 * DEPTH) ** 0.25
LN_EPS = 1e-5
UV_COLS = 2 * GM_WIDTH
QKV_COLS = N_DIL * 3 * ATT_WIDTH
GATE_COLS = 2 * D_MODEL
IN_COLS = UV_COLS + QKV_COLS + GATE_COLS

LANES = 128
VMEM_LIMIT = 56 * 1024 * 1024


def _ln(x):
    mu = jnp.mean(x, axis=-1, keepdims=True)
    xc = x - mu
    var = jnp.mean(xc * xc, axis=-1, keepdims=True)
    return xc * lax.rsqrt(var + LN_EPS)


def _params(sem, vmem=VMEM_LIMIT):
    return pltpu.CompilerParams(dimension_semantics=sem, vmem_limit_bytes=vmem)


def _adaln_kernel(c_ref, w_ref, b_ref, o_ref):
    c = c_ref[...]
    s = c * jax.nn.sigmoid(c)
    o_ref[...] = jnp.dot(s, w_ref[...], preferred_element_type=f32,
                         precision=lax.Precision.HIGHEST) + b_ref[...]


def _adaln(c8, w_ada, b_ada):
    n = w_ada.shape[1] // D_MODEL
    return pl.pallas_call(
        _adaln_kernel,
        grid=(n,),
        in_specs=[pl.BlockSpec((8, D_MODEL), lambda j: (0, 0)),
                  pl.BlockSpec((D_MODEL, D_MODEL), lambda j: (0, j)),
                  pl.BlockSpec((1, D_MODEL), lambda j: (0, j))],
        out_specs=pl.BlockSpec((8, D_MODEL), lambda j: (0, j)),
        out_shape=jax.ShapeDtypeStruct((8, w_ada.shape[1]), f32),
        compiler_params=_params(("arbitrary",)),
        name="adaln",
    )(c8, w_ada, b_ada)


IN_TM = 256
IN_CW = 512


def _inproj_kernel(x_ref, sc_ref, sh_ref, w_ref, uv_ref, qkv_ref, gt_ref):
    xn = _ln(x_ref[...])
    h = (xn * (1.0 + sc_ref[0]) + sh_ref[0]).astype(bf16)
    for c0 in range(0, IN_COLS, IN_CW):
        acc = jnp.dot(h, w_ref[:, c0:c0 + IN_CW], preferred_element_type=f32)
        if c0 < UV_COLS:
            uv_ref[:, c0:c0 + IN_CW] = jax.nn.gelu(acc).astype(bf16)
        elif c0 < UV_COLS + QKV_COLS:
            q0 = c0 - UV_COLS
            qkv_ref[:, q0:q0 + IN_CW] = acc.astype(bf16)
        else:
            g0 = c0 - UV_COLS - QKV_COLS
            gt_ref[:, g0:g0 + IN_CW] = jax.nn.sigmoid(acc).astype(bf16)


def _inproj(x2, sc1, sh1, w_in_bf, seq):
    t = x2.shape[0]
    per_b = seq // IN_TM
    return pl.pallas_call(
        _inproj_kernel,
        grid=(t // IN_TM,),
        in_specs=[pl.BlockSpec((IN_TM, D_MODEL), lambda i: (i, 0)),
                  pl.BlockSpec((1, 1, D_MODEL), lambda i: (i // per_b, 0, 0)),
                  pl.BlockSpec((1, 1, D_MODEL), lambda i: (i // per_b, 0, 0)),
                  pl.BlockSpec((D_MODEL, IN_COLS), lambda i: (0, 0))],
        out_specs=[pl.BlockSpec((IN_TM, UV_COLS), lambda i: (i, 0)),
                   pl.BlockSpec((IN_TM, QKV_COLS), lambda i: (i, 0)),
                   pl.BlockSpec((IN_TM, GATE_COLS), lambda i: (i, 0))],
        out_shape=[jax.ShapeDtypeStruct((t, UV_COLS), bf16),
                   jax.ShapeDtypeStruct((t, QKV_COLS), bf16),
                   jax.ShapeDtypeStruct((t, GATE_COLS), bf16)],
        compiler_params=_params(("arbitrary",)),
        name="inproj",
    )(x2, sc1, sh1, w_in_bf)


GM_TM = 512


def _gmlp_kernel(u_ref, v_ref, g_ref, b_ref, ws_ref, bs_ref, ya_ref):
    row = lax.broadcasted_iota(i32, (GM_CHUNK, GM_CHUNK), 0)
    col = lax.broadcasted_iota(i32, (GM_CHUNK, GM_CHUNK), 1)
    causal = col <= row
    first_half = lax.broadcasted_iota(i32, (GM_CHUNK, LANES), 1) < (GM_WIDTH // GM_GROUPS)
    ws = [jnp.where(causal, ws_ref[g], 0.0).astype(bf16) for g in range(GM_GROUPS)]
    for ch in range(GM_TM // GM_CHUNK):
        r0 = ch * GM_CHUNK
        vn = _ln(v_ref[r0:r0 + GM_CHUNK, :].astype(f32)) * g_ref[...] + b_ref[...]
        vn = vn.astype(bf16)
        for j in range(GM_WIDTH // LANES):
            slab = vn[:, j * LANES:(j + 1) * LANES]
            s_lo = jnp.dot(ws[2 * j], slab, preferred_element_type=f32)
            s_hi = jnp.dot(ws[2 * j + 1], slab, preferred_element_type=f32)
            s = jnp.where(first_half, s_lo, s_hi) + bs_ref[:, j * LANES:(j + 1) * LANES]
            u = u_ref[r0:r0 + GM_CHUNK, j * LANES:(j + 1) * LANES].astype(f32)
            ya_ref[r0:r0 + GM_CHUNK, j * LANES:(j + 1) * LANES] = (u * s).astype(bf16)


def _gmlp(uv, ln_g, ln_b, w_s, bs_full):
    t = uv.shape[0]
    return pl.pallas_call(
        _gmlp_kernel,
        grid=(t // GM_TM,),
        in_specs=[pl.BlockSpec((GM_TM, GM_WIDTH), lambda i: (i, 0)),
                  pl.BlockSpec((GM_TM, GM_WIDTH), lambda i: (i, 1)),
                  pl.BlockSpec((1, GM_WIDTH), lambda i: (0, 0)),
                  pl.BlockSpec((1, GM_WIDTH), lambda i: (0, 0)),
                  pl.BlockSpec((GM_GROUPS, GM_CHUNK, GM_CHUNK), lambda i: (0, 0, 0)),
                  pl.BlockSpec((GM_CHUNK, GM_WIDTH), lambda i: (0, 0))],
        out_specs=pl.BlockSpec((GM_TM, GM_WIDTH), lambda i: (i, 0)),
        out_shape=jax.ShapeDtypeStruct((t, GM_WIDTH), bf16),
        compiler_params=_params(("arbitrary",)),
        name="gmlp",
    )(uv, uv, ln_g, ln_b, w_s, bs_full)


def _relbias_kernel(tab_ref, bucket_ref, band_ref, out_ref):
    g = pl.program_id(0)
    bk = bucket_ref[0]
    band = band_ref[0] > 0
    for h in range(HEADS_PER_GROUP):
        acc = jnp.zeros((ATT_BLOCK, 2 * ATT_BLOCK), f32)
        for b in range(REL_BUCKETS):
            acc = jnp.where(bk == b, tab_ref[b, g * HEADS_PER_GROUP + h], acc)
        out_ref[0, h] = jnp.where(band, acc, NEG_INF)


def _relbias(rel_bias, bucket, band):
    return pl.pallas_call(
        _relbias_kernel,
        grid=(N_DIL,),
        in_specs=[pl.BlockSpec(memory_space=pltpu.SMEM),
                  pl.BlockSpec((1, ATT_BLOCK, 2 * ATT_BLOCK), lambda g: (g, 0, 0)),
                  pl.BlockSpec((1, ATT_BLOCK, 2 * ATT_BLOCK), lambda g: (g, 0, 0))],
        out_specs=pl.BlockSpec((1, HEADS_PER_GROUP, ATT_BLOCK, 2 * ATT_BLOCK),
                               lambda g: (g, 0, 0, 0)),
        out_shape=jax.ShapeDtypeStruct((N_DIL, HEADS_PER_GROUP, ATT_BLOCK, 2 * ATT_BLOCK), f32),
        compiler_params=_params(("arbitrary",)),
        name="relbias",
    )(rel_bias, bucket, band)


def _attn_kernel(q_ref, kp_ref, kc_ref, vp_ref, vc_ref, bias_ref, o_ref, lse_ref):
    first = pl.program_id(2) == 0
    lo_half = lax.broadcasted_iota(i32, (ATT_BLOCK, LANES), 1) < HEAD_DIM
    nt = (((1,), (1,)), ((), ()))
    for j in range(ATT_WIDTH // LANES):
        sl = slice(j * LANES, (j + 1) * LANES)
        q = q_ref[0, :, sl] * (HEAD_DIM ** -0.5)
        kp, kc = kp_ref[0, :, sl], kc_ref[0, :, sl]
        vp, vc = vp_ref[0, :, sl], vc_ref[0, :, sl]
        outs, lses = [], []
        for hh in range(2):
            h = 2 * j + hh
            qm = jnp.where(lo_half if hh == 0 else jnp.logical_not(lo_half), q, 0.0).astype(bf16)
            lp = lax.dot_general(qm, kp, nt, preferred_element_type=f32) + bias_ref[0, h, :, :ATT_BLOCK]
            lc = lax.dot_general(qm, kc, nt, preferred_element_type=f32) + bias_ref[0, h, :, ATT_BLOCK:]
            lp = jnp.where(first, NEG_INF, lp)
            m = jnp.maximum(jnp.max(lp, axis=-1, keepdims=True), jnp.max(lc, axis=-1, keepdims=True))
            pp = jnp.exp(lp - m)
            pc = jnp.exp(lc - m)
            den = jnp.sum(pp, axis=-1, keepdims=True) + jnp.sum(pc, axis=-1, keepdims=True)
            inv = 1.0 / den
            o = (jnp.dot((pp * inv).astype(bf16), vp, preferred_element_type=f32)
                 + jnp.dot((pc * inv).astype(bf16), vc, preferred_element_type=f32))
            outs.append(o)
            lses.append(jnp.broadcast_to(m + jnp.log(den), (ATT_BLOCK, LANES)))
        o_ref[0, :, sl] = jnp.where(lo_half, outs[0], outs[1])
        lse_ref[0, :, sl] = jnp.where(lo_half, lses[0], lses[1])


def _attn_group(qkv, bias, g, dil, batch, seq):
    l = seq // dil
    nb = l // ATT_BLOCK
    ncb = QKV_COLS // ATT_WIDTH
    qkv3 = qkv.reshape(batch, l, dil * QKV_COLS)
    cq, ck, cv = 3 * g, 3 * g + 1, 3 * g + 2

    def spec(cb, prev):
        if prev:
            return pl.BlockSpec((1, ATT_BLOCK, ATT_WIDTH),
                                lambda b, r, n: (b, jnp.maximum(n - 1, 0), r * ncb + cb))
        return pl.BlockSpec((1, ATT_BLOCK, ATT_WIDTH), lambda b, r, n: (b, n, r * ncb + cb))

    out_spec = pl.BlockSpec((1, ATT_BLOCK, ATT_WIDTH), lambda b, r, n: (b, n, r))
    o, lse = pl.pallas_call(
        _attn_kernel,
        grid=(batch, dil, nb),
        in_specs=[spec(cq, False), spec(ck, True), spec(ck, False), spec(cv, True), spec(cv, False),
                  pl.BlockSpec((1, HEADS_PER_GROUP, ATT_BLOCK, 2 * ATT_BLOCK),
                               lambda b, r, n: (g, 0, 0, 0))],
        out_specs=[out_spec, out_spec],
        out_shape=[jax.ShapeDtypeStruct((batch, l, dil * ATT_WIDTH), f32),
                   jax.ShapeDtypeStruct((batch, l, dil * ATT_WIDTH), f32)],
        compiler_params=_params(("arbitrary", "arbitrary", "arbitrary")),
        name=f"attn_g{g}",
    )(qkv3, qkv3, qkv3, qkv3, qkv3, bias)
    return o.reshape(batch * seq, ATT_WIDTH), lse.reshape(batch * seq, ATT_WIDTH)


MIX_TM = 256


def _mix_kernel(o0_ref, o1_ref, o2_ref, l0_ref, l1_ref, l2_ref, ya_ref, gt_ref, x_ref,
                g1_ref, sc2_ref, sh2_ref, wa_ref, wb_ref, wo_ref, ln1g_ref, ln1b_ref,
                wr_ref, br_ref, tri_ref,
                x1_ref, h2_ref, route_ref, rw_ref, cnt_ref, run_ref):
    @pl.when(pl.program_id(0) == 0)
    def _():
        run_ref[...] = jnp.zeros_like(run_ref)

    l0, l1, l2 = l0_ref[...], l1_ref[...], l2_ref[...]
    lm = jnp.maximum(jnp.maximum(l0, l1), l2)
    e0, e1, e2 = jnp.exp(l0 - lm), jnp.exp(l1 - lm), jnp.exp(l2 - lm)
    yb = (e0 * o0_ref[...] + e1 * o1_ref[...] + e2 * o2_ref[...]) / (e0 + e1 + e2)
    a = jnp.dot(ya_ref[...], wa_ref[...], preferred_element_type=f32)
    b = jnp.dot(yb.astype(bf16), wb_ref[...], preferred_element_type=f32)
    merged = gt_ref[:, :D_MODEL].astype(f32) * a + gt_ref[:, D_MODEL:].astype(f32) * b
    mix = jnp.dot(merged.astype(bf16), wo_ref[...], preferred_element_type=f32)
    x1 = _ln(DN_ALPHA * x_ref[...] + g1_ref[0] * mix) * ln1g_ref[...] + ln1b_ref[...]
    x1_ref[...] = x1
    h2 = _ln(x1) * (1.0 + sc2_ref[0]) + sh2_ref[0]
    h2_ref[...] = h2

    lane = lax.broadcasted_iota(i32, (MIX_TM, LANES), 1)
    logits = jnp.dot(h2, wr_ref[...], preferred_element_type=f32,
                     precision=lax.Precision.HIGHEST) + br_ref[...]
    logits = jnp.where(lane < N_EXPERTS, logits, -jnp.inf)
    lane_f = lane.astype(f32)
    vals, idxs = [], []
    for _k in range(TOP_K):
        m = jnp.max(logits, axis=-1, keepdims=True)
        idx = jnp.min(jnp.where(logits == m, lane_f, float(LANES)), axis=-1, keepdims=True).astype(i32)
        vals.append(m)
        idxs.append(idx)
        logits = jnp.where(lane == idx, -jnp.inf, logits)
    exps = [jnp.exp(v - vals[0]) for v in vals]
    den = exps[0] + exps[1] + exps[2] + exps[3]

    hits = [lane == idx for idx in idxs]
    onehot = jnp.zeros((MIX_TM, LANES), f32)
    for hit in hits:
        onehot = onehot + jnp.where(hit, 1.0, 0.0)
    prefix = jnp.dot(tri_ref[...], onehot.astype(bf16), preferred_element_type=f32) + run_ref[...]
    route = jnp.zeros((MIX_TM, LANES), i32)
    rw = jnp.zeros((MIX_TM, LANES), f32)
    for k in range(TOP_K):
        rank = jnp.sum(jnp.where(hits[k], prefix, 0.0), axis=-1, keepdims=True).astype(i32)
        route = jnp.where(lane == k, idxs[k], route)
        route = jnp.where(lane == TOP_K + k, rank, route)
        rw = jnp.where(lane == k, exps[k] / den, rw)
    route_ref[...] = route
    rw_ref[...] = rw
    run = run_ref[...] + jnp.sum(onehot, axis=0, keepdims=True)
    run_ref[...] = run
    cnt_ref[...] = jnp.broadcast_to(run, cnt_ref.shape)


def _mix(os_, ls_, ya, gates, x2, g1, sc2, sh2, wa, wb, wo, ln1g, ln1b, wr, br, tri, seq):
    t = x2.shape[0]
    per_b = seq // MIX_TM
    row = lambda w: pl.BlockSpec((MIX_TM, w), lambda i: (i, 0))
    const = lambda s: pl.BlockSpec(s, lambda i: tuple(0 for _ in s))
    modb = pl.BlockSpec((1, 1, D_MODEL), lambda i: (i // per_b, 0, 0))
    return pl.pallas_call(
        _mix_kernel,
        grid=(t // MIX_TM,),
        in_specs=[row(ATT_WIDTH)] * 6 + [row(GM_WIDTH), row(GATE_COLS), row(D_MODEL),
                  modb, modb, modb,
                  const((GM_WIDTH, D_MODEL)), const((ATT_WIDTH, D_MODEL)), const((D_MODEL, D_MODEL)),
                  const((1, D_MODEL)), const((1, D_MODEL)),
                  const((D_MODEL, LANES)), const((1, LANES)), const((MIX_TM, MIX_TM))],
        out_specs=[row(D_MODEL), row(D_MODEL), row(LANES), row(LANES), const((8, LANES))],
        out_shape=[jax.ShapeDtypeStruct((t, D_MODEL), f32),
                   jax.ShapeDtypeStruct((t, D_MODEL), f32),
                   jax.ShapeDtypeStruct((t, LANES), i32),
                   jax.ShapeDtypeStruct((t, LANES), f32),
                   jax.ShapeDtypeStruct((8, LANES), f32)],
        scratch_shapes=[pltpu.VMEM((1, LANES), f32)],
        compiler_params=_params(("arbitrary",)),
        name="mix",
    )(*os_, *ls_, ya, gates, x2, g1, sc2, sh2, wa, wb, wo, ln1g, ln1b, wr, br, tri)


def _moe_kernel(blk_e_ref, tok_ref, h2_hbm, rww_ref, wg_ref, bg_ref, wu_ref, bu_ref, wd_ref, bd_ref,
                out_ref, xbuf, wgb, wub, wdb, sem):
    j = pl.program_id(0)
    e = blk_e_ref[j]
    prev_e = blk_e_ref[jnp.maximum(j - 1, 0)]

    def row_copy(i):
        tok = tok_ref[0, 0, i]
        return pltpu.make_async_copy(h2_hbm.at[pl.ds(tok, 1)], xbuf.at[pl.ds(i, 1)], sem)

    def start(i, c):
        row_copy(i).start()
        return c

    lax.fori_loop(0, MOE_BLOCK, start, 0, unroll=8)

    @pl.when(jnp.logical_or(j == 0, e != prev_e))
    def _():
        wgb[...] = wg_ref[0].astype(bf16)
        wub[...] = wu_ref[0].astype(bf16)
        wdb[...] = wd_ref[0].astype(bf16)

    def wait(i, c):
        row_copy(i).wait()
        return c

    lax.fori_loop(0, MOE_BLOCK, wait, 0, unroll=8)

    xb = xbuf[...].astype(bf16)
    g = jnp.dot(xb, wgb[...], preferred_element_type=f32) + bg_ref[0]
    u = jnp.dot(xb, wub[...], preferred_element_type=f32) + bu_ref[0]
    g = jnp.minimum(g, SWIGLU_LIMIT)
    u = jnp.clip(u, -SWIGLU_LIMIT, SWIGLU_LIMIT)
    act = (u + 1.0) * (g * jax.nn.sigmoid(SWIGLU_ALPHA * g))
    y = jnp.dot(act.astype(bf16), wdb[...], preferred_element_type=f32) + bd_ref[0]
    out_ref[...] = y * rww_ref[...]


def _moe(blk_e, row_tok3, h2, row_w, w_gate, b_gate, w_up, b_up, w_down, b_down):
    nblk = blk_e.shape[0]
    wspec = pl.BlockSpec((1, D_MODEL, D_MODEL), lambda j, be: (be[j], 0, 0))
    bspec = pl.BlockSpec((1, 1, D_MODEL), lambda j, be: (be[j], 0, 0))
    grid_spec = pltpu.PrefetchScalarGridSpec(
        num_scalar_prefetch=1,
        grid=(nblk,),
        in_specs=[pl.BlockSpec((1, 1, MOE_BLOCK), lambda j, be: (j, 0, 0), memory_space=pltpu.SMEM),
                  pl.BlockSpec(memory_space=pl.ANY),
                  pl.BlockSpec((MOE_BLOCK, 1), lambda j, be: (j, 0)),
                  wspec, bspec, wspec, bspec, wspec, bspec],
        out_specs=pl.BlockSpec((MOE_BLOCK, D_MODEL), lambda j, be: (j, 0)),
        scratch_shapes=[pltpu.VMEM((MOE_BLOCK, D_MODEL), f32),
                        pltpu.VMEM((D_MODEL, D_MODEL), bf16),
                        pltpu.VMEM((D_MODEL, D_MODEL), bf16),
                        pltpu.VMEM((D_MODEL, D_MODEL), bf16),
                        pltpu.SemaphoreType.DMA(())],
    )
    return pl.pallas_call(
        _moe_kernel,
        grid_spec=grid_spec,
        out_shape=jax.ShapeDtypeStruct((nblk * MOE_BLOCK, D_MODEL), f32),
        compiler_params=_params(("arbitrary",)),
        name="moe",
    )(blk_e, row_tok3, h2, row_w, w_gate, b_gate, w_up, b_up, w_down, b_down)


CB_TM = 128


def _combine_kernel(dest_ref, yb_hbm, x1_ref, g2_ref, lng_ref, lnb_ref, out_ref, ybuf, sem):
    def row_copy(k, i):
        d = dest_ref[0, k, i]
        return pltpu.make_async_copy(yb_hbm.at[pl.ds(d, 1)], ybuf.at[k, pl.ds(i, 1)], sem)

    for k in range(TOP_K):
        def start(i, c, k=k):
            row_copy(k, i).start()
            return c
        lax.fori_loop(0, CB_TM, start, 0, unroll=8)
    for k in range(TOP_K):
        def wait(i, c, k=k):
            row_copy(k, i).wait()
            return c
        lax.fori_loop(0, CB_TM, wait, 0, unroll=8)

    y = (ybuf[0] + ybuf[1]) + (ybuf[2] + ybuf[3])
    out_ref[...] = _ln(DN_ALPHA * x1_ref[...] + g2_ref[0] * y) * lng_ref[...] + lnb_ref[...]


def _combine(dest3, yb, x1, g2, ln2g, ln2b, seq):
    t = x1.shape[0]
    per_b = seq // CB_TM
    return pl.pallas_call(
        _combine_kernel,
        grid=(t // CB_TM,),
        in_specs=[pl.BlockSpec((1, TOP_K, CB_TM), lambda i: (i, 0, 0), memory_space=pltpu.SMEM),
                  pl.BlockSpec(memory_space=pl.ANY),
                  pl.BlockSpec((CB_TM, D_MODEL), lambda i: (i, 0)),
                  pl.BlockSpec((1, 1, D_MODEL), lambda i: (i // per_b, 0, 0)),
                  pl.BlockSpec((1, D_MODEL), lambda i: (0, 0)),
                  pl.BlockSpec((1, D_MODEL), lambda i: (0, 0))],
        out_specs=pl.BlockSpec((CB_TM, D_MODEL), lambda i: (i, 0)),
        out_shape=jax.ShapeDtypeStruct((t, D_MODEL), f32),
        scratch_shapes=[pltpu.VMEM((TOP_K, CB_TM, D_MODEL), f32),
                        pltpu.SemaphoreType.DMA(())],
        compiler_params=_params(("arbitrary",)),
        name="combine",
    )(dest3, yb, x1, g2, ln2g, ln2b)


def _t5_bucket(dist):
    d = dist.astype(f32)
    large = REL_MAX_EXACT + jnp.log(jnp.maximum(d, float(REL_MAX_EXACT)) / REL_MAX_EXACT) / math.log(
        REL_MAX_DIST / REL_MAX_EXACT) * (REL_BUCKETS - REL_MAX_EXACT)
    large = jnp.minimum(large.astype(i32), REL_BUCKETS - 1)
    return jnp.where(dist < REL_MAX_EXACT, dist, large)


def _bias_indices():
    qi = jnp.arange(ATT_BLOCK)[:, None]
    ki = jnp.arange(2 * ATT_BLOCK)[None, :]
    didx = qi + ATT_BLOCK - ki
    buckets, bands = [], []
    for win, dil in DIL_PAIRS:
        buckets.append(_t5_bucket(jnp.clip(didx, 0, None) * dil))
        bands.append(((didx >= 0) & (didx <= win // dil)).astype(i32))
    return jnp.stack(buckets).astype(i32), jnp.stack(bands)


def kernel(x, c, w_ada, b_ada, w_in, gm_ln_g, gm_ln_b, gm_w_s, gm_b_s, w_branch_a, w_branch_b, w_out,
           rel_bias, ln1_g, ln1_b, w_router, b_router, w_gate, b_gate, w_up, b_up, w_down, b_down,
           ln2_g, ln2_b):
    batch, seq, _ = x.shape
    t = batch * seq
    l = 0
    x2 = x.reshape(t, D_MODEL)

    c8 = jnp.pad(c, ((0, 8 - batch), (0, 0)))
    mod = _adaln(c8, w_ada[l], b_ada[l][None, :])[:batch]
    sh1, sc1, g1, sh2, sc2, g2 = [m[:, None, :] for m in jnp.split(mod, 6, axis=-1)]

    uv, qkv, gates = _inproj(x2, sc1, sh1, w_in[l].astype(bf16), seq)

    bs_full = jnp.repeat(gm_b_s[l].T, GM_WIDTH // GM_GROUPS, axis=1)
    ya = _gmlp(uv, gm_ln_g[l][None, :], gm_ln_b[l][None, :], gm_w_s[l], bs_full)

    bucket, band = _bias_indices()
    bias = _relbias(rel_bias, bucket, band)
    os_, ls_ = [], []
    for g, (_win, dil) in enumerate(DIL_PAIRS):
        o, lse = _attn_group(qkv, bias, g, dil, batch, seq)
        os_.append(o)
        ls_.append(lse)

    wr = jnp.pad(w_router[l], ((0, 0), (0, LANES - N_EXPERTS)))
    br = jnp.pad(b_router[l], (0, LANES - N_EXPERTS))[None, :]
    tri = (jnp.arange(MIX_TM)[None, :] < jnp.arange(MIX_TM)[:, None]).astype(bf16)
    x1, h2, route, rw, cnt = _mix(
        os_, ls_, ya, gates, x2, g1, sc2, sh2,
        w_branch_a[l].astype(bf16), w_branch_b[l].astype(bf16), w_out[l].astype(bf16),
        ln1_g[l][None, :], ln1_b[l][None, :], wr, br, tri, seq)

    top_e = route[:, :TOP_K]
    rank = route[:, TOP_K:2 * TOP_K]
    top_w = rw[:, :TOP_K]
    counts = cnt[0, :N_EXPERTS].astype(i32)
    pcounts = (counts + MOE_BLOCK - 1) // MOE_BLOCK * MOE_BLOCK
    pends = jnp.cumsum(pcounts)
    pstarts = pends - pcounts
    dest = pstarts[top_e] + rank
    a_total = t * TOP_K
    nblk = a_total // MOE_BLOCK + N_EXPERTS
    r_total = nblk * MOE_BLOCK
    inv = jnp.full((r_total,), -1, i32).at[dest.reshape(-1)].set(jnp.arange(a_total, dtype=i32))
    valid = inv >= 0
    safe = jnp.maximum(inv, 0)
    row_tok = jnp.where(valid, safe // TOP_K, 0)
    row_w = jnp.where(valid, top_w.reshape(-1)[safe], 0.0)
    blk_e = jnp.minimum(
        jnp.searchsorted(pends, jnp.arange(nblk, dtype=i32) * MOE_BLOCK, side='right'),
        N_EXPERTS - 1).astype(i32)

    yb = _moe(blk_e, row_tok.reshape(nblk, 1, MOE_BLOCK), h2, row_w[:, None],
              w_gate[l], b_gate[l][:, None, :], w_up[l], b_up[l][:, None, :],
              w_down[l], b_down[l][:, None, :])

    dest3 = dest.reshape(t // CB_TM, CB_TM, TOP_K).transpose(0, 2, 1)
    out = _combine(dest3, yb, x1, g2, ln2_g[l][None, :], ln2_b[l][None, :], seq)
    return out.reshape(batch, seq, D_MODEL)
```

```python
import functools
import math

import jax
import jax.numpy as jnp
from jax import lax
from jax.experimental import pallas as pl
from jax.experimental.pallas import tpu as pltpu

f32 = jnp.float32
bf16 = jnp.bfloat16
i32 = jnp.int32

D_MODEL = 1024
GM_WIDTH = 512
GM_GROUPS = 8
GM_CHUNK = 128
DIL_PAIRS = ((128, 1), (512, 4), (2048, 16))
N_DIL = 3
HEADS_PER_GROUP = 8
HEAD_DIM = 64
ATT_WIDTH = 512
ATT_BLOCK = 128
NEG_INF = -1e30
REL_BUCKETS = 32
REL_MAX_EXACT = 16
REL_MAX_DIST = 2048
N_EXPERTS = 32
TOP_K = 4
SWIGLU_LIMIT = 7.0
SWIGLU_ALPHA = 1.702
MOE_BLOCK = 128
DEPTH = 1
DN_ALPHA = (2 * DEPTH) ** 0.25
LN_EPS = 1e-5
UV_COLS = 2 * GM_WIDTH
QKV_COLS = N_DIL * 3 * ATT_WIDTH
GATE_COLS = 2 * D_MODEL
IN_COLS = UV_COLS + QKV_COLS + GATE_COLS

LANES = 128
VMEM_LIMIT = 56 * 1024 * 1024


def _ln(x):
    mu = jnp.mean(x, axis=-1, keepdims=True)
    xc = x - mu
    var = jnp.mean(xc * xc, axis=-1, keepdims=True)
    return xc * lax.rsqrt(var + LN_EPS)


def _params(sem, vmem=VMEM_LIMIT):
    return pltpu.CompilerParams(dimension_semantics=sem, vmem_limit_bytes=vmem)


def _adaln_kernel(c_ref, w_ref, b_ref, o_ref):
    c = c_ref[...]
    s = c * jax.nn.sigmoid(c)
    o_ref[...] = jnp.dot(s, w_ref[...], preferred_element_type=f32,
                         precision=lax.Precision.HIGHEST) + b_ref[...]


def _adaln(c8, w_ada, b_ada):
    n = w_ada.shape[1] // D_MODEL
    return pl.pallas_call(
        _adaln_kernel,
        grid=(n,),
        in_specs=[pl.BlockSpec((8, D_MODEL), lambda j: (0, 0)),
                  pl.BlockSpec((D_MODEL, D_MODEL), lambda j: (0, j)),
                  pl.BlockSpec((1, D_MODEL), lambda j: (0, j))],
        out_specs=pl.BlockSpec((8, D_MODEL), lambda j: (0, j)),
        out_shape=jax.ShapeDtypeStruct((8, w_ada.shape[1]), f32),
        compiler_params=_params(("arbitrary",)),
        name="adaln",
    )(c8, w_ada, b_ada)


IN_TM = 256
IN_CW = 512
GRP_COLS = 3 * ATT_WIDTH


def _inproj_kernel(x_ref, sc_ref, sh_ref, w_ref, p1_ref, p2_ref,
                   uv_ref, gt_ref, qkv0_ref, qkv1_ref, qkv2_ref):
    xn = _ln(x_ref[...])
    h = (xn * (1.0 + sc_ref[0]) + sh_ref[0]).astype(bf16)
    hp = [h,
          jnp.dot(p1_ref[...], h, preferred_element_type=f32).astype(bf16),
          jnp.dot(p2_ref[...], h, preferred_element_type=f32).astype(bf16)]
    for c0 in range(0, UV_COLS, IN_CW):
        acc = jnp.dot(h, w_ref[:, c0:c0 + IN_CW], preferred_element_type=f32)
        uv_ref[:, c0:c0 + IN_CW] = jax.nn.gelu(acc).astype(bf16)
    for g, (qref, (_win, dil)) in enumerate(zip((qkv0_ref, qkv1_ref, qkv2_ref), DIL_PAIRS)):
        n = IN_TM // dil
        for q0 in range(0, GRP_COLS, IN_CW):
            c0 = UV_COLS + g * GRP_COLS + q0
            acc = jnp.dot(hp[g], w_ref[:, c0:c0 + IN_CW], preferred_element_type=f32).astype(bf16)
            for rho in range(dil):
                qref[0, rho, :, q0:q0 + IN_CW] = acc[rho * n:(rho + 1) * n, :]
    for g0 in range(0, GATE_COLS, IN_CW):
        c0 = UV_COLS + QKV_COLS + g0
        acc = jnp.dot(h, w_ref[:, c0:c0 + IN_CW], preferred_element_type=f32)
        gt_ref[:, g0:g0 + IN_CW] = jax.nn.sigmoid(acc).astype(bf16)


def _inproj(x2, sc1, sh1, w_in_bf, perms, batch, seq):
    t = x2.shape[0]
    per_b = seq // IN_TM
    qkv_specs, qkv_shapes = [], []
    for _win, dil in DIL_PAIRS:
        n = IN_TM // dil
        qkv_specs.append(pl.BlockSpec((1, dil, n, GRP_COLS), lambda i: (i // per_b, 0, i % per_b, 0)))
        qkv_shapes.append(jax.ShapeDtypeStruct((batch, dil, seq // dil, GRP_COLS), bf16))
    return pl.pallas_call(
        _inproj_kernel,
        grid=(t // IN_TM,),
        in_specs=[pl.BlockSpec((IN_TM, D_MODEL), lambda i: (i, 0)),
                  pl.BlockSpec((1, 1, D_MODEL), lambda i: (i // per_b, 0, 0)),
                  pl.BlockSpec((1, 1, D_MODEL), lambda i: (i // per_b, 0, 0)),
                  pl.BlockSpec((D_MODEL, IN_COLS), lambda i: (0, 0)),
                  pl.BlockSpec((IN_TM, IN_TM), lambda i: (0, 0)),
                  pl.BlockSpec((IN_TM, IN_TM), lambda i: (0, 0))],
        out_specs=[pl.BlockSpec((IN_TM, UV_COLS), lambda i: (i, 0)),
                   pl.BlockSpec((IN_TM, GATE_COLS), lambda i: (i, 0))] + qkv_specs,
        out_shape=[jax.ShapeDtypeStruct((t, UV_COLS), bf16),
                   jax.ShapeDtypeStruct((t, GATE_COLS), bf16)] + qkv_shapes,
        compiler_params=_params(("arbitrary",)),
        name="inproj",
    )(x2, sc1, sh1, w_in_bf, perms[1], perms[2])


GM_TM = 512


def _gmlp_kernel(u_ref, v_ref, g_ref, b_ref, ws_ref, bs_ref, ya_ref):
    row = lax.broadcasted_iota(i32, (GM_CHUNK, GM_CHUNK), 0)
    col = lax.broadcasted_iota(i32, (GM_CHUNK, GM_CHUNK), 1)
    causal = col <= row
    first_half = lax.broadcasted_iota(i32, (GM_CHUNK, LANES), 1) < (GM_WIDTH // GM_GROUPS)
    ws = [jnp.where(causal, ws_ref[g], 0.0).astype(bf16) for g in range(GM_GROUPS)]
    for ch in range(GM_TM // GM_CHUNK):
        r0 = ch * GM_CHUNK
        vn = _ln(v_ref[r0:r0 + GM_CHUNK, :].astype(f32)) * g_ref[...] + b_ref[...]
        vn = vn.astype(bf16)
        for j in range(GM_WIDTH // LANES):
            slab = vn[:, j * LANES:(j + 1) * LANES]
            s_lo = jnp.dot(ws[2 * j], slab, preferred_element_type=f32)
            s_hi = jnp.dot(ws[2 * j + 1], slab, preferred_element_type=f32)
            s = jnp.where(first_half, s_lo, s_hi) + bs_ref[:, j * LANES:(j + 1) * LANES]
            u = u_ref[r0:r0 + GM_CHUNK, j * LANES:(j + 1) * LANES].astype(f32)
            ya_ref[r0:r0 + GM_CHUNK, j * LANES:(j + 1) * LANES] = (u * s).astype(bf16)


def _gmlp(uv, ln_g, ln_b, w_s, bs_full):
    t = uv.shape[0]
    return pl.pallas_call(
        _gmlp_kernel,
        grid=(t // GM_TM,),
        in_specs=[pl.BlockSpec((GM_TM, GM_WIDTH), lambda i: (i, 0)),
                  pl.BlockSpec((GM_TM, GM_WIDTH), lambda i: (i, 1)),
                  pl.BlockSpec((1, GM_WIDTH), lambda i: (0, 0)),
                  pl.BlockSpec((1, GM_WIDTH), lambda i: (0, 0)),
                  pl.BlockSpec((GM_GROUPS, GM_CHUNK, GM_CHUNK), lambda i: (0, 0, 0)),
                  pl.BlockSpec((GM_CHUNK, GM_WIDTH), lambda i: (0, 0))],
        out_specs=pl.BlockSpec((GM_TM, GM_WIDTH), lambda i: (i, 0)),
        out_shape=jax.ShapeDtypeStruct((t, GM_WIDTH), bf16),
        compiler_params=_params(("arbitrary",)),
        name="gmlp",
    )(uv, uv, ln_g, ln_b, w_s, bs_full)


def _relbias_kernel(tab_ref, bucket_ref, band_ref, out_ref):
    g = pl.program_id(0)
    bk = bucket_ref[0]
    band = band_ref[0] > 0
    for h in range(HEADS_PER_GROUP):
        acc = jnp.zeros((ATT_BLOCK, 2 * ATT_BLOCK), f32)
        for b in range(REL_BUCKETS):
            acc = jnp.where(bk == b, tab_ref[b, g * HEADS_PER_GROUP + h], acc)
        out_ref[0, h] = jnp.where(band, acc, NEG_INF)


def _relbias(rel_bias, bucket, band):
    return pl.pallas_call(
        _relbias_kernel,
        grid=(N_DIL,),
        in_specs=[pl.BlockSpec(memory_space=pltpu.SMEM),
                  pl.BlockSpec((1, ATT_BLOCK, 2 * ATT_BLOCK), lambda g: (g, 0, 0)),
                  pl.BlockSpec((1, ATT_BLOCK, 2 * ATT_BLOCK), lambda g: (g, 0, 0))],
        out_specs=pl.BlockSpec((1, HEADS_PER_GROUP, ATT_BLOCK, 2 * ATT_BLOCK),
                               lambda g: (g, 0, 0, 0)),
        out_shape=jax.ShapeDtypeStruct((N_DIL, HEADS_PER_GROUP, ATT_BLOCK, 2 * ATT_BLOCK), f32),
        compiler_params=_params(("arbitrary",)),
        name="relbias",
    )(rel_bias, bucket, band)


def _attn_kernel(q_ref, kp_ref, kc_ref, vp_ref, vc_ref, bias_ref, o_ref, lse_ref):
    first = pl.program_id(2) == 0
    lane = lax.broadcasted_iota(i32, (ATT_BLOCK, LANES), 1)
    lo_half = lane < HEAD_DIM
    nt = (((1,), (1,)), ((), ()))
    ones = jnp.ones((2 * ATT_BLOCK, LANES), bf16)
    n_slab = ATT_WIDTH // LANES
    logits, v_ext = [], []
    for j in range(n_slab):
        sl = slice(j * LANES, (j + 1) * LANES)
        q = q_ref[0, 0, :, sl] * (HEAD_DIM ** -0.5)
        k_cat = jnp.concatenate([kp_ref[0, 0, :, sl], kc_ref[0, 0, :, sl]], axis=0)
        v_cat = jnp.concatenate([vp_ref[0, 0, :, sl], vc_ref[0, 0, :, sl]], axis=0)
        v_ext.append(jnp.concatenate([v_cat, ones], axis=1))
        for hh in range(2):
            qm = jnp.where(lo_half if hh == 0 else jnp.logical_not(lo_half), q, 0.0).astype(bf16)
            logits.append(lax.dot_general(qm, k_cat, nt, preferred_element_type=f32))
    lg = jnp.concatenate(logits, axis=0) + bias_ref[0].reshape(HEADS_PER_GROUP * ATT_BLOCK, 2 * ATT_BLOCK)
    prev_cols = lax.broadcasted_iota(i32, lg.shape, 1) < ATT_BLOCK
    lg = jnp.where(jnp.logical_and(first, prev_cols), NEG_INF, lg)
    m = jnp.max(lg, axis=-1, keepdims=True)
    p = jnp.exp(lg - m).astype(bf16)
    lse_tile = jnp.zeros((ATT_BLOCK, LANES), f32)
    for j in range(n_slab):
        outs = []
        for hh in range(2):
            h = 2 * j + hh
            r = jnp.dot(p[h * ATT_BLOCK:(h + 1) * ATT_BLOCK], v_ext[j], preferred_element_type=f32)
            den = r[:, LANES:]
            outs.append(r[:, :LANES] * (1.0 / den))
            lse_h = m[h * ATT_BLOCK:(h + 1) * ATT_BLOCK] + jnp.log(den)
            lse_tile = jnp.where(lane == h, lse_h, lse_tile)
        o_ref[0, 0, :, j * LANES:(j + 1) * LANES] = jnp.where(lo_half, outs[0], outs[1]).astype(bf16)
    lse_ref[0, 0] = lse_tile


def _attn_group(qkv_g, bias, g, dil, batch, seq):
    l = seq // dil
    nb = l // ATT_BLOCK

    def spec(cb, prev):
        if prev:
            return pl.BlockSpec((1, 1, ATT_BLOCK, ATT_WIDTH),
                                lambda b, r, n: (b, r, jnp.maximum(n - 1, 0), cb))
        return pl.BlockSpec((1, 1, ATT_BLOCK, ATT_WIDTH), lambda b, r, n: (b, r, n, cb))

    return pl.pallas_call(
        _attn_kernel,
        grid=(batch, dil, nb),
        in_specs=[spec(0, False), spec(1, True), spec(1, False), spec(2, True), spec(2, False),
                  pl.BlockSpec((1, HEADS_PER_GROUP, ATT_BLOCK, 2 * ATT_BLOCK),
                               lambda b, r, n: (g, 0, 0, 0))],
        out_specs=[pl.BlockSpec((1, 1, ATT_BLOCK, ATT_WIDTH), lambda b, r, n: (b, r, n, 0)),
                   pl.BlockSpec((1, 1, ATT_BLOCK, LANES), lambda b, r, n: (b, r, n, 0))],
        out_shape=[jax.ShapeDtypeStruct((batch, dil, l, ATT_WIDTH), bf16),
                   jax.ShapeDtypeStruct((batch, dil, l, LANES), f32)],
        compiler_params=_params(("arbitrary", "arbitrary", "arbitrary")),
        name=f"attn_g{g}",
    )(qkv_g, qkv_g, qkv_g, qkv_g, qkv_g, bias)


MIX_TM = 256


def _split_bf16(x, parts):
    out = []
    for _ in range(parts):
        hi = x.astype(bf16)
        out.append(hi)
        x = x - hi.astype(f32)
    return out


def _mix_kernel(o0_ref, o1_ref, o2_ref, l0_ref, l1_ref, l2_ref, pt1_ref, pt2_ref, ex_ref,
                ya_ref, gt_ref, x_ref,
                g1_ref, sc2_ref, sh2_ref, wa_ref, wb_ref, wo_ref, ln1g_ref, ln1b_ref,
                wr_ref, br_ref, tri_ref,
                x1_ref, h2_ref, route_ref, rw_ref, cnt_ref, run_ref):
    @pl.when(pl.program_id(0) == 0)
    def _():
        run_ref[...] = jnp.zeros_like(run_ref)

    os_, ls_ = [], []
    for o_ref, l_ref, pt_ref in ((o0_ref, l0_ref, None), (o1_ref, l1_ref, pt1_ref), (o2_ref, l2_ref, pt2_ref)):
        o = o_ref[0].reshape(MIX_TM, ATT_WIDTH)
        lse = l_ref[0].reshape(MIX_TM, LANES)
        if pt_ref is None:
            os_.append(o.astype(f32))
            ls_.append(lse)
        else:
            pt = pt_ref[...]
            os_.append(jnp.dot(pt, o, preferred_element_type=f32))
            parts = [jnp.dot(pt, part, preferred_element_type=f32) for part in _split_bf16(lse, 3)]
            ls_.append((parts[0] + parts[1]) + parts[2])
    lm = jnp.maximum(jnp.maximum(ls_[0], ls_[1]), ls_[2])
    es = [jnp.exp(lse - lm) for lse in ls_]
    inv = 1.0 / (es[0] + es[1] + es[2])
    yb = jnp.zeros((MIX_TM, ATT_WIDTH), f32)
    for e, o in zip(es, os_):
        w_hi, w_lo = _split_bf16(e * inv, 2)
        w_full = (jnp.dot(w_hi, ex_ref[...], preferred_element_type=f32)
                  + jnp.dot(w_lo, ex_ref[...], preferred_element_type=f32))
        yb = yb + w_full * o
    a = jnp.dot(ya_ref[...], wa_ref[...], preferred_element_type=f32)
    b = jnp.dot(yb.astype(bf16), wb_ref[...], preferred_element_type=f32)
    merged = gt_ref[:, :D_MODEL].astype(f32) * a + gt_ref[:, D_MODEL:].astype(f32) * b
    mix = jnp.dot(merged.astype(bf16), wo_ref[...], preferred_element_type=f32)
    x1 = _ln(DN_ALPHA * x_ref[...] + g1_ref[0] * mix) * ln1g_ref[...] + ln1b_ref[...]
    x1_ref[...] = x1
    h2 = _ln(x1) * (1.0 + sc2_ref[0]) + sh2_ref[0]
    h2_ref[...] = h2

    lane = lax.broadcasted_iota(i32, (MIX_TM, LANES), 1)
    logits = jnp.dot(h2, wr_ref[...], preferred_element_type=f32,
                     precision=lax.Precision.HIGHEST) + br_ref[...]
    logits = jnp.where(lane < N_EXPERTS, logits, -jnp.inf)
    lane_f = lane.astype(f32)
    vals, idxs = [], []
    for _k in range(TOP_K):
        m = jnp.max(logits, axis=-1, keepdims=True)
        idx = jnp.min(jnp.where(logits == m, lane_f, float(LANES)), axis=-1, keepdims=True).astype(i32)
        vals.append(m)
        idxs.append(idx)
        logits = jnp.where(lane == idx, -jnp.inf, logits)
    exps = [jnp.exp(v - vals[0]) for v in vals]
    den = exps[0] + exps[1] + exps[2] + exps[3]

    hits = [lane == idx for idx in idxs]
    onehot = jnp.zeros((MIX_TM, LANES), f32)
    for hit in hits:
        onehot = onehot + jnp.where(hit, 1.0, 0.0)
    prefix = jnp.dot(tri_ref[...], onehot.astype(bf16), preferred_element_type=f32) + run_ref[...]
    route = jnp.zeros((MIX_TM, LANES), i32)
    rw = jnp.zeros((MIX_TM, LANES), f32)
    for k in range(TOP_K):
        rank = jnp.sum(jnp.where(hits[k], prefix, 0.0), axis=-1, keepdims=True).astype(i32)
        route = jnp.where(lane == k, idxs[k], route)
        route = jnp.where(lane == TOP_K + k, rank, route)
        rw = jnp.where(lane == k, exps[k] / den, rw)
    route_ref[...] = route
    rw_ref[...] = rw
    run = run_ref[...] + jnp.sum(onehot, axis=0, keepdims=True)
    run_ref[...] = run
    cnt_ref[...] = jnp.broadcast_to(run, cnt_ref.shape)


def _mix(os_, ls_, perms_t, expand, ya, gates, x2, g1, sc2, sh2, wa, wb, wo, ln1g, ln1b, wr, br, tri, seq):
    t = x2.shape[0]
    per_b = seq // MIX_TM
    row = lambda w: pl.BlockSpec((MIX_TM, w), lambda i: (i, 0))
    const = lambda s: pl.BlockSpec(s, lambda i: tuple(0 for _ in s))
    modb = pl.BlockSpec((1, 1, D_MODEL), lambda i: (i // per_b, 0, 0))
    grp = lambda w: [pl.BlockSpec((1, dil, MIX_TM // dil, w), lambda i: (i // per_b, 0, i % per_b, 0))
                     for _win, dil in DIL_PAIRS]
    return pl.pallas_call(
        _mix_kernel,
        grid=(t // MIX_TM,),
        in_specs=grp(ATT_WIDTH) + grp(LANES) + [
                  const((MIX_TM, MIX_TM)), const((MIX_TM, MIX_TM)), const((LANES, ATT_WIDTH)),
                  row(GM_WIDTH), row(GATE_COLS), row(D_MODEL),
                  modb, modb, modb,
                  const((GM_WIDTH, D_MODEL)), const((ATT_WIDTH, D_MODEL)), const((D_MODEL, D_MODEL)),
                  const((1, D_MODEL)), const((1, D_MODEL)),
                  const((D_MODEL, LANES)), const((1, LANES)), const((MIX_TM, MIX_TM))],
        out_specs=[row(D_MODEL), row(D_MODEL), row(LANES), row(LANES), const((8, LANES))],
        out_shape=[jax.ShapeDtypeStruct((t, D_MODEL), f32),
                   jax.ShapeDtypeStruct((t, D_MODEL), f32),
                   jax.ShapeDtypeStruct((t, LANES), i32),
                   jax.ShapeDtypeStruct((t, LANES), f32),
                   jax.ShapeDtypeStruct((8, LANES), f32)],
        scratch_shapes=[pltpu.VMEM((1, LANES), f32)],
        compiler_params=_params(("arbitrary",)),
        name="mix",
    )(*os_, *ls_, perms_t[1], perms_t[2], expand, ya, gates, x2, g1, sc2, sh2, wa, wb, wo,
      ln1g, ln1b, wr, br, tri)


def _moe_kernel(blk_e_ref, tok_ref, h2_hbm, rww_ref, wg_ref, bg_ref, wu_ref, bu_ref, wd_ref, bd_ref,
                out_ref, xbuf, wgb, wub, wdb, sem):
    j = pl.program_id(0)
    e = blk_e_ref[j]
    prev_e = blk_e_ref[jnp.maximum(j - 1, 0)]

    def row_copy(i):
        tok = tok_ref[0, 0, i]
        return pltpu.make_async_copy(h2_hbm.at[pl.ds(tok, 1)], xbuf.at[pl.ds(i, 1)], sem)

    def start(i, c):
        row_copy(i).start()
        return c

    lax.fori_loop(0, MOE_BLOCK, start, 0, unroll=8)

    @pl.when(jnp.logical_or(j == 0, e != prev_e))
    def _():
        wgb[...] = wg_ref[0].astype(bf16)
        wub[...] = wu_ref[0].astype(bf16)
        wdb[...] = wd_ref[0].astype(bf16)

    def wait(i, c):
        row_copy(i).wait()
        return c

    lax.fori_loop(0, MOE_BLOCK, wait, 0, unroll=8)

    xb = xbuf[...].astype(bf16)
    g = jnp.dot(xb, wgb[...], preferred_element_type=f32) + bg_ref[0]
    u = jnp.dot(xb, wub[...], preferred_element_type=f32) + bu_ref[0]
    g = jnp.minimum(g, SWIGLU_LIMIT)
    u = jnp.clip(u, -SWIGLU_LIMIT, SWIGLU_LIMIT)
    act = (u + 1.0) * (g * jax.nn.sigmoid(SWIGLU_ALPHA * g))
    y = jnp.dot(act.astype(bf16), wdb[...], preferred_element_type=f32) + bd_ref[0]
    out_ref[...] = y * rww_ref[...]


def _moe(blk_e, row_tok3, h2, row_w, w_gate, b_gate, w_up, b_up, w_down, b_down):
    nblk = blk_e.shape[0]
    wspec = pl.BlockSpec((1, D_MODEL, D_MODEL), lambda j, be: (be[j], 0, 0))
    bspec = pl.BlockSpec((1, 1, D_MODEL), lambda j, be: (be[j], 0, 0))
    grid_spec = pltpu.PrefetchScalarGridSpec(
        num_scalar_prefetch=1,
        grid=(nblk,),
        in_specs=[pl.BlockSpec((1, 1, MOE_BLOCK), lambda j, be: (j, 0, 0), memory_space=pltpu.SMEM),
                  pl.BlockSpec(memory_space=pl.ANY),
                  pl.BlockSpec((MOE_BLOCK, 1), lambda j, be: (j, 0)),
                  wspec, bspec, wspec, bspec, wspec, bspec],
        out_specs=pl.BlockSpec((MOE_BLOCK, D_MODEL), lambda j, be: (j, 0)),
        scratch_shapes=[pltpu.VMEM((MOE_BLOCK, D_MODEL), f32),
                        pltpu.VMEM((D_MODEL, D_MODEL), bf16),
                        pltpu.VMEM((D_MODEL, D_MODEL), bf16),
                        pltpu.VMEM((D_MODEL, D_MODEL), bf16),
                        pltpu.SemaphoreType.DMA(())],
    )
    return pl.pallas_call(
        _moe_kernel,
        grid_spec=grid_spec,
        out_shape=jax.ShapeDtypeStruct((nblk * MOE_BLOCK, D_MODEL), f32),
        compiler_params=_params(("arbitrary",)),
        name="moe",
    )(blk_e, row_tok3, h2, row_w, w_gate, b_gate, w_up, b_up, w_down, b_down)


CB_TM = 128


def _combine_kernel(dest_ref, yb_hbm, x1_ref, g2_ref, lng_ref, lnb_ref, out_ref, ybuf, sem):
    def row_copy(k, i):
        d = dest_ref[0, k, i]
        return pltpu.make_async_copy(yb_hbm.at[pl.ds(d, 1)], ybuf.at[k, pl.ds(i, 1)], sem)

    for k in range(TOP_K):
        def start(i, c, k=k):
            row_copy(k, i).start()
            return c
        lax.fori_loop(0, CB_TM, start, 0, unroll=8)
    for k in range(TOP_K):
        def wait(i, c, k=k):
            row_copy(k, i).wait()
            return c
        lax.fori_loop(0, CB_TM, wait, 0, unroll=8)

    y = (ybuf[0] + ybuf[1]) + (ybuf[2] + ybuf[3])
    out_ref[...] = _ln(DN_ALPHA * x1_ref[...] + g2_ref[0] * y) * lng_ref[...] + lnb_ref[...]


def _combine(dest3, yb, x1, g2, ln2g, ln2b, seq):
    t = x1.shape[0]
    per_b = seq // CB_TM
    return pl.pallas_call(
        _combine_kernel,
        grid=(t // CB_TM,),
        in_specs=[pl.BlockSpec((1, TOP_K, CB_TM), lambda i: (i, 0, 0), memory_space=pltpu.SMEM),
                  pl.BlockSpec(memory_space=pl.ANY),
                  pl.BlockSpec((CB_TM, D_MODEL), lambda i: (i, 0)),
                  pl.BlockSpec((1, 1, D_MODEL), lambda i: (i // per_b, 0, 0)),
                  pl.BlockSpec((1, D_MODEL), lambda i: (0, 0)),
                  pl.BlockSpec((1, D_MODEL), lambda i: (0, 0))],
        out_specs=pl.BlockSpec((CB_TM, D_MODEL), lambda i: (i, 0)),
        out_shape=jax.ShapeDtypeStruct((t, D_MODEL), f32),
        scratch_shapes=[pltpu.VMEM((TOP_K, CB_TM, D_MODEL), f32),
                        pltpu.SemaphoreType.DMA(())],
        compiler_params=_params(("arbitrary",)),
        name="combine",
    )(dest3, yb, x1, g2, ln2g, ln2b)


def _t5_bucket(dist):
    d = dist.astype(f32)
    large = REL_MAX_EXACT + jnp.log(jnp.maximum(d, float(REL_MAX_EXACT)) / REL_MAX_EXACT) / math.log(
        REL_MAX_DIST / REL_MAX_EXACT) * (REL_BUCKETS - REL_MAX_EXACT)
    large = jnp.minimum(large.astype(i32), REL_BUCKETS - 1)
    return jnp.where(dist < REL_MAX_EXACT, dist, large)


def _bias_indices():
    qi = jnp.arange(ATT_BLOCK)[:, None]
    ki = jnp.arange(2 * ATT_BLOCK)[None, :]
    didx = qi + ATT_BLOCK - ki
    buckets, bands = [], []
    for win, dil in DIL_PAIRS:
        buckets.append(_t5_bucket(jnp.clip(didx, 0, None) * dil))
        bands.append(((didx >= 0) & (didx <= win // dil)).astype(i32))
    return jnp.stack(buckets).astype(i32), jnp.stack(bands)


def _residue_perm(tm, dil):
    n = tm // dil
    dst = jnp.arange(tm)
    src = (dst % n) * dil + dst // n
    return (src[:, None] == jnp.arange(tm)[None, :]).astype(bf16)


def kernel(x, c, w_ada, b_ada, w_in, gm_ln_g, gm_ln_b, gm_w_s, gm_b_s, w_branch_a, w_branch_b, w_out,
           rel_bias, ln1_g, ln1_b, w_router, b_router, w_gate, b_gate, w_up, b_up, w_down, b_down,
           ln2_g, ln2_b):
    batch, seq, _ = x.shape
    t = batch * seq
    l = 0
    x2 = x.reshape(t, D_MODEL)

    c8 = jnp.pad(c, ((0, 8 - batch), (0, 0)))
    mod = _adaln(c8, w_ada[l], b_ada[l][None, :])[:batch]
    sh1, sc1, g1, sh2, sc2, g2 = [m[:, None, :] for m in jnp.split(mod, 6, axis=-1)]

    perms = [_residue_perm(IN_TM, dil) for _win, dil in DIL_PAIRS]
    uv, gates, *qkvs = _inproj(x2, sc1, sh1, w_in[l].astype(bf16), perms, batch, seq)

    bs_full = jnp.repeat(gm_b_s[l].T, GM_WIDTH // GM_GROUPS, axis=1)
    ya = _gmlp(uv, gm_ln_g[l][None, :], gm_ln_b[l][None, :], gm_w_s[l], bs_full)

    bucket, band = _bias_indices()
    bias = _relbias(rel_bias, bucket, band)
    os_, ls_ = [], []
    for g, (_win, dil) in enumerate(DIL_PAIRS):
        o, lse = _attn_group(qkvs[g], bias, g, dil, batch, seq)
        os_.append(o)
        ls_.append(lse)

    wr = jnp.pad(w_router[l], ((0, 0), (0, LANES - N_EXPERTS)))
    br = jnp.pad(b_router[l], (0, LANES - N_EXPERTS))[None, :]
    tri = (jnp.arange(MIX_TM)[None, :] < jnp.arange(MIX_TM)[:, None]).astype(bf16)
    perms_t = [_residue_perm(MIX_TM, dil).T for _win, dil in DIL_PAIRS]
    expand = (jnp.arange(LANES)[:, None] == jnp.arange(ATT_WIDTH)[None, :] // HEAD_DIM).astype(bf16)
    x1, h2, route, rw, cnt = _mix(
        os_, ls_, perms_t, expand, ya, gates, x2, g1, sc2, sh2,
        w_branch_a[l].astype(bf16), w_branch_b[l].astype(bf16), w_out[l].astype(bf16),
        ln1_g[l][None, :], ln1_b[l][None, :], wr, br, tri, seq)

    top_e = route[:, :TOP_K]
    rank = route[:, TOP_K:2 * TOP_K]
    top_w = rw[:, :TOP_K]
    counts = cnt[0, :N_EXPERTS].astype(i32)
    pcounts = (counts + MOE_BLOCK - 1) // MOE_BLOCK * MOE_BLOCK
    pends = jnp.cumsum(pcounts)
    pstarts = pends - pcounts
    dest = pstarts[top_e] + rank
    a_total = t * TOP_K
    nblk = a_total // MOE_BLOCK + N_EXPERTS
    r_total = nblk * MOE_BLOCK
    inv = jnp.full((r_total,), -1, i32).at[dest.reshape(-1)].set(
        jnp.arange(a_total, dtype=i32), unique_indices=True, mode='promise_in_bounds')
    valid = inv >= 0
    safe = jnp.maximum(inv, 0)
    row_tok = jnp.where(valid, safe // TOP_K, 0)
    row_w = jnp.where(valid, top_w.reshape(-1)[safe], 0.0)
    blk_start = jnp.arange(nblk, dtype=i32) * MOE_BLOCK
    blk_e = jnp.minimum(jnp.sum((pends[None, :] <= blk_start[:, None]).astype(i32), axis=1),
                        N_EXPERTS - 1)

    yb = _moe(blk_e, row_tok.reshape(nblk, 1, MOE_BLOCK), h2, row_w[:, None],
              w_gate[l], b_gate[l][:, None, :], w_up[l], b_up[l][:, None, :],
              w_down[l], b_down[l][:, None, :])

    dest3 = dest.reshape(t // CB_TM, CB_TM, TOP_K).transpose(0, 2, 1)
    out = _combine(dest3, yb, x1, g2, ln2_g[l][None, :], ln2_b[l][None, :], seq)
    return out.reshape(batch, seq, D_MODEL)
```

```python
import functools
import math

import jax
import jax.numpy as jnp
from jax import lax
from jax.experimental import pallas as pl
from jax.experimental.pallas import tpu as pltpu

f32 = jnp.float32
bf16 = jnp.bfloat16
i32 = jnp.int32

D_MODEL = 1024
GM_WIDTH = 512
GM_GROUPS = 8
GM_CHUNK = 128
DIL_PAIRS = ((128, 1), (512, 4), (2048, 16))
N_DIL = 3
HEADS_PER_GROUP = 8
HEAD_DIM = 64
ATT_WIDTH = 512
ATT_BLOCK = 128
NEG_INF = -1e30
REL_BUCKETS = 32
REL_MAX_EXACT = 16
REL_MAX_DIST = 2048
N_EXPERTS = 32
TOP_K = 4
SWIGLU_LIMIT = 7.0
SWIGLU_ALPHA = 1.702
MOE_BLOCK = 128
DEPTH = 1
DN_ALPHA = (2 * DEPTH) ** 0.25
LN_EPS = 1e-5
UV_COLS = 2 * GM_WIDTH
QKV_COLS = N_DIL * 3 * ATT_WIDTH
GATE_COLS = 2 * D_MODEL
IN_COLS = UV_COLS + QKV_COLS + GATE_COLS

LANES = 128
SUBLANES = 8
VMEM_LIMIT = 56 * 1024 * 1024


def _ln(x):
    mu = jnp.mean(x, axis=-1, keepdims=True)
    xc = x - mu
    var = jnp.mean(xc * xc, axis=-1, keepdims=True)
    return xc * lax.rsqrt(var + LN_EPS)


def _params(sem, vmem=VMEM_LIMIT):
    return pltpu.CompilerParams(dimension_semantics=sem, vmem_limit_bytes=vmem)


def _adaln_kernel(c_ref, w_ref, b_ref, o_ref):
    c = c_ref[...]
    s = c * jax.nn.sigmoid(c)
    o_ref[...] = jnp.dot(s, w_ref[...], preferred_element_type=f32,
                         precision=lax.Precision.HIGHEST) + b_ref[...]


def _adaln(c8, w_ada, b_ada):
    n = w_ada.shape[1] // D_MODEL
    return pl.pallas_call(
        _adaln_kernel,
        grid=(n,),
        in_specs=[pl.BlockSpec((8, D_MODEL), lambda j: (0, 0)),
                  pl.BlockSpec((D_MODEL, D_MODEL), lambda j: (0, j)),
                  pl.BlockSpec((1, D_MODEL), lambda j: (0, j))],
        out_specs=pl.BlockSpec((8, D_MODEL), lambda j: (0, j)),
        out_shape=jax.ShapeDtypeStruct((8, w_ada.shape[1]), f32),
        compiler_params=_params(("arbitrary",)),
        name="adaln",
    )(c8, w_ada, b_ada)


IN_TM = 256
IN_CW = 512
GRP_COLS = 3 * ATT_WIDTH


def _inproj_kernel(x_ref, sc_ref, sh_ref, w_ref, p1_ref, p2_ref,
                   uv_ref, gt_ref, qkv0_ref, qkv1_ref, qkv2_ref):
    xn = _ln(x_ref[...])
    h = (xn * (1.0 + sc_ref[0]) + sh_ref[0]).astype(bf16)
    hp = [h,
          jnp.dot(p1_ref[...], h, preferred_element_type=f32).astype(bf16),
          jnp.dot(p2_ref[...], h, preferred_element_type=f32).astype(bf16)]
    for c0 in range(0, UV_COLS, IN_CW):
        acc = jnp.dot(h, w_ref[:, c0:c0 + IN_CW], preferred_element_type=f32)
        uv_ref[:, c0:c0 + IN_CW] = jax.nn.gelu(acc).astype(bf16)
    for g, (qref, (_win, dil)) in enumerate(zip((qkv0_ref, qkv1_ref, qkv2_ref), DIL_PAIRS)):
        n = IN_TM // dil
        for q0 in range(0, GRP_COLS, IN_CW):
            c0 = UV_COLS + g * GRP_COLS + q0
            acc = jnp.dot(hp[g], w_ref[:, c0:c0 + IN_CW], preferred_element_type=f32).astype(bf16)
            for rho in range(dil):
                qref[0, rho, :, q0:q0 + IN_CW] = acc[rho * n:(rho + 1) * n, :]
    for g0 in range(0, GATE_COLS, IN_CW):
        c0 = UV_COLS + QKV_COLS + g0
        acc = jnp.dot(h, w_ref[:, c0:c0 + IN_CW], preferred_element_type=f32)
        gt_ref[:, g0:g0 + IN_CW] = jax.nn.sigmoid(acc).astype(bf16)


def _inproj(x2, sc1, sh1, w_in_bf, perms, batch, seq):
    t = x2.shape[0]
    per_b = seq // IN_TM
    qkv_specs, qkv_shapes = [], []
    for _win, dil in DIL_PAIRS:
        n = IN_TM // dil
        qkv_specs.append(pl.BlockSpec((1, dil, n, GRP_COLS), lambda i: (i // per_b, 0, i % per_b, 0)))
        qkv_shapes.append(jax.ShapeDtypeStruct((batch, dil, seq // dil, GRP_COLS), bf16))
    return pl.pallas_call(
        _inproj_kernel,
        grid=(t // IN_TM,),
        in_specs=[pl.BlockSpec((IN_TM, D_MODEL), lambda i: (i, 0)),
                  pl.BlockSpec((1, 1, D_MODEL), lambda i: (i // per_b, 0, 0)),
                  pl.BlockSpec((1, 1, D_MODEL), lambda i: (i // per_b, 0, 0)),
                  pl.BlockSpec((D_MODEL, IN_COLS), lambda i: (0, 0)),
                  pl.BlockSpec((IN_TM, IN_TM), lambda i: (0, 0)),
                  pl.BlockSpec((IN_TM, IN_TM), lambda i: (0, 0))],
        out_specs=[pl.BlockSpec((IN_TM, UV_COLS), lambda i: (i, 0)),
                   pl.BlockSpec((IN_TM, GATE_COLS), lambda i: (i, 0))] + qkv_specs,
        out_shape=[jax.ShapeDtypeStruct((t, UV_COLS), bf16),
                   jax.ShapeDtypeStruct((t, GATE_COLS), bf16)] + qkv_shapes,
        compiler_params=_params(("arbitrary",)),
        name="inproj",
    )(x2, sc1, sh1, w_in_bf, perms[1], perms[2])


GM_TM = 512


def _gmlp_kernel(u_ref, v_ref, g_ref, b_ref, ws_ref, bs_ref, ya_ref):
    row = lax.broadcasted_iota(i32, (GM_CHUNK, GM_CHUNK), 0)
    col = lax.broadcasted_iota(i32, (GM_CHUNK, GM_CHUNK), 1)
    causal = col <= row
    first_half = lax.broadcasted_iota(i32, (GM_CHUNK, LANES), 1) < (GM_WIDTH // GM_GROUPS)
    ws = [jnp.where(causal, ws_ref[g], 0.0).astype(bf16) for g in range(GM_GROUPS)]
    for ch in range(GM_TM // GM_CHUNK):
        r0 = ch * GM_CHUNK
        vn = _ln(v_ref[r0:r0 + GM_CHUNK, :].astype(f32)) * g_ref[...] + b_ref[...]
        vn = vn.astype(bf16)
        for j in range(GM_WIDTH // LANES):
            slab = vn[:, j * LANES:(j + 1) * LANES]
            s_lo = jnp.dot(ws[2 * j], slab, preferred_element_type=f32)
            s_hi = jnp.dot(ws[2 * j + 1], slab, preferred_element_type=f32)
            s = jnp.where(first_half, s_lo, s_hi) + bs_ref[:, j * LANES:(j + 1) * LANES]
            u = u_ref[r0:r0 + GM_CHUNK, j * LANES:(j + 1) * LANES].astype(f32)
            ya_ref[r0:r0 + GM_CHUNK, j * LANES:(j + 1) * LANES] = (u * s).astype(bf16)


def _gmlp(uv, ln_g, ln_b, w_s, bs_full):
    t = uv.shape[0]
    return pl.pallas_call(
        _gmlp_kernel,
        grid=(t // GM_TM,),
        in_specs=[pl.BlockSpec((GM_TM, GM_WIDTH), lambda i: (i, 0)),
                  pl.BlockSpec((GM_TM, GM_WIDTH), lambda i: (i, 1)),
                  pl.BlockSpec((1, GM_WIDTH), lambda i: (0, 0)),
                  pl.BlockSpec((1, GM_WIDTH), lambda i: (0, 0)),
                  pl.BlockSpec((GM_GROUPS, GM_CHUNK, GM_CHUNK), lambda i: (0, 0, 0)),
                  pl.BlockSpec((GM_CHUNK, GM_WIDTH), lambda i: (0, 0))],
        out_specs=pl.BlockSpec((GM_TM, GM_WIDTH), lambda i: (i, 0)),
        out_shape=jax.ShapeDtypeStruct((t, GM_WIDTH), bf16),
        compiler_params=_params(("arbitrary",)),
        name="gmlp",
    )(uv, uv, ln_g, ln_b, w_s, bs_full)


def _relbias_kernel(tab_ref, bucket_ref, band_ref, out_ref):
    g = pl.program_id(0)
    bk = bucket_ref[0]
    band = band_ref[0] > 0
    for h in range(HEADS_PER_GROUP):
        acc = jnp.zeros((ATT_BLOCK, 2 * ATT_BLOCK), f32)
        for b in range(REL_BUCKETS):
            acc = jnp.where(bk == b, tab_ref[b, g * HEADS_PER_GROUP + h], acc)
        out_ref[0, h] = jnp.where(band, acc, NEG_INF)


def _relbias(rel_bias, bucket, band):
    return pl.pallas_call(
        _relbias_kernel,
        grid=(N_DIL,),
        in_specs=[pl.BlockSpec(memory_space=pltpu.SMEM),
                  pl.BlockSpec((1, ATT_BLOCK, 2 * ATT_BLOCK), lambda g: (g, 0, 0)),
                  pl.BlockSpec((1, ATT_BLOCK, 2 * ATT_BLOCK), lambda g: (g, 0, 0))],
        out_specs=pl.BlockSpec((1, HEADS_PER_GROUP, ATT_BLOCK, 2 * ATT_BLOCK),
                               lambda g: (g, 0, 0, 0)),
        out_shape=jax.ShapeDtypeStruct((N_DIL, HEADS_PER_GROUP, ATT_BLOCK, 2 * ATT_BLOCK), f32),
        compiler_params=_params(("arbitrary",)),
        name="relbias",
    )(rel_bias, bucket, band)


def _attn_kernel(q_ref, kp_ref, kc_ref, vp_ref, vc_ref, bias_ref, o_ref, lse_ref):
    first = pl.program_id(2) == 0
    lane = lax.broadcasted_iota(i32, (ATT_BLOCK, LANES), 1)
    lo_half = lane < HEAD_DIM
    nt = (((1,), (1,)), ((), ()))
    ones = jnp.ones((2 * ATT_BLOCK, LANES), bf16)
    n_slab = ATT_WIDTH // LANES
    logits, v_ext = [], []
    for j in range(n_slab):
        sl = slice(j * LANES, (j + 1) * LANES)
        q = q_ref[0, 0, :, sl] * (HEAD_DIM ** -0.5)
        k_cat = jnp.concatenate([kp_ref[0, 0, :, sl], kc_ref[0, 0, :, sl]], axis=0)
        v_cat = jnp.concatenate([vp_ref[0, 0, :, sl], vc_ref[0, 0, :, sl]], axis=0)
        v_ext.append(jnp.concatenate([v_cat, ones], axis=1))
        for hh in range(2):
            qm = jnp.where(lo_half if hh == 0 else jnp.logical_not(lo_half), q, 0.0).astype(bf16)
            logits.append(lax.dot_general(qm, k_cat, nt, preferred_element_type=f32))
    lg = jnp.concatenate(logits, axis=0) + bias_ref[0].reshape(HEADS_PER_GROUP * ATT_BLOCK, 2 * ATT_BLOCK)
    prev_cols = lax.broadcasted_iota(i32, lg.shape, 1) < ATT_BLOCK
    lg = jnp.where(jnp.logical_and(first, prev_cols), NEG_INF, lg)
    m = jnp.max(lg, axis=-1, keepdims=True)
    p = jnp.exp(lg - m).astype(bf16)
    lse_tile = jnp.zeros((ATT_BLOCK, LANES), f32)
    for j in range(n_slab):
        outs = []
        for hh in range(2):
            h = 2 * j + hh
            r = jnp.dot(p[h * ATT_BLOCK:(h + 1) * ATT_BLOCK], v_ext[j], preferred_element_type=f32)
            den = r[:, LANES:]
            outs.append(r[:, :LANES] * (1.0 / den))
            lse_h = m[h * ATT_BLOCK:(h + 1) * ATT_BLOCK] + jnp.log(den)
            lse_tile = jnp.where(lane == h, lse_h, lse_tile)
        o_ref[0, 0, :, j * LANES:(j + 1) * LANES] = jnp.where(lo_half, outs[0], outs[1]).astype(bf16)
    lse_ref[0, 0] = lse_tile


def _attn_group(qkv_g, bias, g, dil, batch, seq):
    l = seq // dil
    nb = l // ATT_BLOCK

    def spec(cb, prev):
        if prev:
            return pl.BlockSpec((1, 1, ATT_BLOCK, ATT_WIDTH),
                                lambda b, r, n: (b, r, jnp.maximum(n - 1, 0), cb))
        return pl.BlockSpec((1, 1, ATT_BLOCK, ATT_WIDTH), lambda b, r, n: (b, r, n, cb))

    return pl.pallas_call(
        _attn_kernel,
        grid=(batch, dil, nb),
        in_specs=[spec(0, False), spec(1, True), spec(1, False), spec(2, True), spec(2, False),
                  pl.BlockSpec((1, HEADS_PER_GROUP, ATT_BLOCK, 2 * ATT_BLOCK),
                               lambda b, r, n: (g, 0, 0, 0))],
        out_specs=[pl.BlockSpec((1, 1, ATT_BLOCK, ATT_WIDTH), lambda b, r, n: (b, r, n, 0)),
                   pl.BlockSpec((1, 1, ATT_BLOCK, LANES), lambda b, r, n: (b, r, n, 0))],
        out_shape=[jax.ShapeDtypeStruct((batch, dil, l, ATT_WIDTH), bf16),
                   jax.ShapeDtypeStruct((batch, dil, l, LANES), f32)],
        compiler_params=_params(("arbitrary", "arbitrary", "arbitrary")),
        name=f"attn_g{g}",
    )(qkv_g, qkv_g, qkv_g, qkv_g, qkv_g, bias)


MIX_TM = 256


def _split_bf16(x, parts):
    out = []
    for _ in range(parts):
        hi = x.astype(bf16)
        out.append(hi)
        x = x - hi.astype(f32)
    return out


def _mix_kernel(o0_ref, o1_ref, o2_ref, l0_ref, l1_ref, l2_ref, pt1_ref, pt2_ref, ex_ref,
                ya_ref, gt_ref, x_ref,
                g1_ref, sc2_ref, sh2_ref, wa_ref, wb_ref, wo_ref, ln1g_ref, ln1b_ref,
                wr_ref, br_ref, tri_ref,
                x1_ref, h2_ref, route_ref, rw_ref, cnt_ref, run_ref):
    @pl.when(pl.program_id(0) == 0)
    def _():
        run_ref[...] = jnp.zeros_like(run_ref)

    os_, ls_ = [], []
    for o_ref, l_ref, pt_ref in ((o0_ref, l0_ref, None), (o1_ref, l1_ref, pt1_ref), (o2_ref, l2_ref, pt2_ref)):
        o = o_ref[0].reshape(MIX_TM, ATT_WIDTH)
        lse = l_ref[0].reshape(MIX_TM, LANES)
        if pt_ref is None:
            os_.append(o.astype(f32))
            ls_.append(lse)
        else:
            pt = pt_ref[...]
            os_.append(jnp.dot(pt, o, preferred_element_type=f32))
            parts = [jnp.dot(pt, part, preferred_element_type=f32) for part in _split_bf16(lse, 3)]
            ls_.append((parts[0] + parts[1]) + parts[2])
    lm = jnp.maximum(jnp.maximum(ls_[0], ls_[1]), ls_[2])
    es = [jnp.exp(lse - lm) for lse in ls_]
    inv = 1.0 / (es[0] + es[1] + es[2])
    yb = jnp.zeros((MIX_TM, ATT_WIDTH), f32)
    for e, o in zip(es, os_):
        w_hi, w_lo = _split_bf16(e * inv, 2)
        w_full = (jnp.dot(w_hi, ex_ref[...], preferred_element_type=f32)
                  + jnp.dot(w_lo, ex_ref[...], preferred_element_type=f32))
        yb = yb + w_full * o
    a = jnp.dot(ya_ref[...], wa_ref[...], preferred_element_type=f32)
    b = jnp.dot(yb.astype(bf16), wb_ref[...], preferred_element_type=f32)
    merged = gt_ref[:, :D_MODEL].astype(f32) * a + gt_ref[:, D_MODEL:].astype(f32) * b
    mix = jnp.dot(merged.astype(bf16), wo_ref[...], preferred_element_type=f32)
    x1 = _ln(DN_ALPHA * x_ref[...] + g1_ref[0] * mix) * ln1g_ref[...] + ln1b_ref[...]
    x1_ref[...] = x1
    h2 = _ln(x1) * (1.0 + sc2_ref[0]) + sh2_ref[0]
    for r in range(SUBLANES):
        h2_ref[pl.ds(r, MIX_TM, stride=SUBLANES), :] = h2[:, r * LANES:(r + 1) * LANES]

    lane = lax.broadcasted_iota(i32, (MIX_TM, LANES), 1)
    logits = jnp.dot(h2, wr_ref[...], preferred_element_type=f32,
                     precision=lax.Precision.HIGHEST) + br_ref[...]
    logits = jnp.where(lane < N_EXPERTS, logits, -jnp.inf)
    lane_f = lane.astype(f32)
    vals, idxs = [], []
    for _k in range(TOP_K):
        m = jnp.max(logits, axis=-1, keepdims=True)
        idx = jnp.min(jnp.where(logits == m, lane_f, float(LANES)), axis=-1, keepdims=True).astype(i32)
        vals.append(m)
        idxs.append(idx)
        logits = jnp.where(lane == idx, -jnp.inf, logits)
    exps = [jnp.exp(v - vals[0]) for v in vals]
    den = exps[0] + exps[1] + exps[2] + exps[3]

    hits = [lane == idx for idx in idxs]
    onehot = jnp.zeros((MIX_TM, LANES), f32)
    for hit in hits:
        onehot = onehot + jnp.where(hit, 1.0, 0.0)
    prefix = jnp.dot(tri_ref[...], onehot.astype(bf16), preferred_element_type=f32) + run_ref[...]
    route = jnp.zeros((MIX_TM, LANES), i32)
    rw = jnp.zeros((MIX_TM, LANES), f32)
    for k in range(TOP_K):
        rank = jnp.sum(jnp.where(hits[k], prefix, 0.0), axis=-1, keepdims=True).astype(i32)
        route = jnp.where(lane == k, idxs[k], route)
        route = jnp.where(lane == TOP_K + k, rank, route)
        rw = jnp.where(lane == k, exps[k] / den, rw)
    route_ref[...] = route
    rw_ref[...] = rw
    run = run_ref[...] + jnp.sum(onehot, axis=0, keepdims=True)
    run_ref[...] = run
    cnt_ref[...] = jnp.broadcast_to(run, cnt_ref.shape)


def _mix(os_, ls_, perms_t, expand, ya, gates, x2, g1, sc2, sh2, wa, wb, wo, ln1g, ln1b, wr, br, tri, seq):
    t = x2.shape[0]
    per_b = seq // MIX_TM
    row = lambda w: pl.BlockSpec((MIX_TM, w), lambda i: (i, 0))
    const = lambda s: pl.BlockSpec(s, lambda i: tuple(0 for _ in s))
    modb = pl.BlockSpec((1, 1, D_MODEL), lambda i: (i // per_b, 0, 0))
    grp = lambda w: [pl.BlockSpec((1, dil, MIX_TM // dil, w), lambda i: (i // per_b, 0, i % per_b, 0))
                     for _win, dil in DIL_PAIRS]
    return pl.pallas_call(
        _mix_kernel,
        grid=(t // MIX_TM,),
        in_specs=grp(ATT_WIDTH) + grp(LANES) + [
                  const((MIX_TM, MIX_TM)), const((MIX_TM, MIX_TM)), const((LANES, ATT_WIDTH)),
                  row(GM_WIDTH), row(GATE_COLS), row(D_MODEL),
                  modb, modb, modb,
                  const((GM_WIDTH, D_MODEL)), const((ATT_WIDTH, D_MODEL)), const((D_MODEL, D_MODEL)),
                  const((1, D_MODEL)), const((1, D_MODEL)),
                  const((D_MODEL, LANES)), const((1, LANES)), const((MIX_TM, MIX_TM))],
        out_specs=[row(D_MODEL), pl.BlockSpec((MIX_TM * SUBLANES, LANES), lambda i: (i, 0)),
                   row(LANES), row(LANES), const((8, LANES))],
        out_shape=[jax.ShapeDtypeStruct((t, D_MODEL), f32),
                   jax.ShapeDtypeStruct((t * SUBLANES, LANES), f32),
                   jax.ShapeDtypeStruct((t, LANES), i32),
                   jax.ShapeDtypeStruct((t, LANES), f32),
                   jax.ShapeDtypeStruct((8, LANES), f32)],
        scratch_shapes=[pltpu.VMEM((1, LANES), f32)],
        compiler_params=_params(("arbitrary",)),
        name="mix",
    )(*os_, *ls_, perms_t[1], perms_t[2], expand, ya, gates, x2, g1, sc2, sh2, wa, wb, wo,
      ln1g, ln1b, wr, br, tri)


MOE_TM = 256


def _moe_kernel(te_ref, first_ref, nexte_ref, wslot_ref, nused_ref,
                tok_cur_ref, tok_nxt_ref, h2_hbm, rww_ref, wg_hbm, wu_hbm, wd_hbm,
                bg_ref, bu_ref, bd_ref,
                out_ref, xbuf0, xbuf1, wbuf, wgb, wub, wdb, sem_x, sem_w):
    j = pl.program_id(0)
    last = pl.num_programs(0) - 1
    xbufs = (xbuf0, xbuf1)

    def row_copy(tok, i, s):
        return pltpu.make_async_copy(h2_hbm.at[pl.ds(pl.multiple_of(tok * SUBLANES, SUBLANES), SUBLANES)],
                                     xbufs[s].at[pl.ds(pl.multiple_of(i * SUBLANES, SUBLANES), SUBLANES)],
                                     sem_x.at[s])

    def rows_wait(s):
        pltpu.make_async_copy(h2_hbm.at[pl.ds(0, MOE_TM * SUBLANES)], xbufs[s], sem_x.at[s]).wait()

    def weight_copies(e, ws):
        return [pltpu.make_async_copy(w.at[e], wbuf.at[ws, k], sem_w.at[ws])
                for k, w in enumerate((wg_hbm, wu_hbm, wd_hbm))]

    def gather_loop(tok_ref, s):
        def body(i, c):
            row_copy(tok_ref[0, 0, i], i, s).start()
            return c
        lax.fori_loop(0, MOE_TM, body, 0, unroll=8)

    @pl.when(j == 0)
    def _():
        gather_loop(tok_cur_ref, 0)
        for cp in weight_copies(te_ref[0], wslot_ref[0]):
            cp.start(priority=1)

    def load_weights():
        ws = wslot_ref[j]
        for cp in weight_copies(te_ref[j], ws):
            cp.wait()
        wgb[...] = wbuf[ws, 0].astype(bf16)
        wub[...] = wbuf[ws, 1].astype(bf16)
        wdb[...] = wbuf[ws, 2].astype(bf16)
        ne = nexte_ref[j]

        @pl.when(ne >= 0)
        def _():
            for cp in weight_copies(ne, 1 - ws):
                cp.start(priority=1)

    def expert_mlp(s):
        for i in range(MOE_TM):
            row_copy(tok_nxt_ref[0, 0, i], i, 1 - s).start()
        xb = jnp.concatenate([xbufs[s][pl.ds(r, MOE_TM, stride=SUBLANES), :] for r in range(SUBLANES)],
                             axis=1).astype(bf16)
        g = jnp.dot(xb, wgb[...], preferred_element_type=f32) + bg_ref[0]
        u = jnp.dot(xb, wub[...], preferred_element_type=f32) + bu_ref[0]
        g = jnp.minimum(g, SWIGLU_LIMIT)
        u = jnp.clip(u, -SWIGLU_LIMIT, SWIGLU_LIMIT)
        act = (u + 1.0) * (g * jax.nn.sigmoid(SWIGLU_ALPHA * g))
        y = (jnp.dot(act.astype(bf16), wdb[...], preferred_element_type=f32) + bd_ref[0]) * rww_ref[...]
        for r in range(SUBLANES):
            out_ref[pl.ds(r, MOE_TM, stride=SUBLANES), :] = y[:, r * LANES:(r + 1) * LANES]

    def idle_tile(s):
        gather_loop(tok_nxt_ref, s)
        out_ref[...] = jnp.zeros_like(out_ref)

    used = j < nused_ref[0]
    for s in range(2):
        @pl.when(j % 2 == s)
        def _(s=s):
            rows_wait(s)
            pl.when(first_ref[j] == 1)(load_weights)
            pl.when(used)(functools.partial(expert_mlp, s))
            pl.when(jnp.logical_not(used))(functools.partial(idle_tile, 1 - s))
            pl.when(j == last)(functools.partial(rows_wait, 1 - s))


def _moe(tile_e, tile_first, next_e, wslot, n_used, row_tok3, h2, row_w,
         w_gate, b_gate, w_up, b_up, w_down, b_down):
    ntile = tile_e.shape[0]
    bspec = pl.BlockSpec((1, 1, D_MODEL), lambda j, te, *_: (te[j], 0, 0))
    hbm = pl.BlockSpec(memory_space=pl.ANY)
    grid_spec = pltpu.PrefetchScalarGridSpec(
        num_scalar_prefetch=5,
        grid=(ntile,),
        in_specs=[pl.BlockSpec((1, 1, MOE_TM), lambda j, *_: (j, 0, 0), memory_space=pltpu.SMEM),
                  pl.BlockSpec((1, 1, MOE_TM), lambda j, *_: (jnp.minimum(j + 1, ntile - 1), 0, 0),
                               memory_space=pltpu.SMEM),
                  hbm,
                  pl.BlockSpec((MOE_TM, 1), lambda j, *_: (j, 0)),
                  hbm, hbm, hbm, bspec, bspec, bspec],
        out_specs=pl.BlockSpec((MOE_TM * SUBLANES, LANES), lambda j, *_: (j, 0)),
        scratch_shapes=[pltpu.VMEM((MOE_TM * SUBLANES, LANES), f32),
                        pltpu.VMEM((MOE_TM * SUBLANES, LANES), f32),
                        pltpu.VMEM((2, 3, D_MODEL, D_MODEL), f32),
                        pltpu.VMEM((D_MODEL, D_MODEL), bf16),
                        pltpu.VMEM((D_MODEL, D_MODEL), bf16),
                        pltpu.VMEM((D_MODEL, D_MODEL), bf16),
                        pltpu.SemaphoreType.DMA((2,)),
                        pltpu.SemaphoreType.DMA((2,))],
    )
    return pl.pallas_call(
        _moe_kernel,
        grid_spec=grid_spec,
        out_shape=jax.ShapeDtypeStruct((ntile * MOE_TM * SUBLANES, LANES), f32),
        compiler_params=_params(("arbitrary",)),
        name="moe",
    )(tile_e, tile_first, next_e, wslot, n_used, row_tok3, row_tok3, h2, row_w,
      w_gate, w_up, w_down, b_gate, b_up, b_down)


CB_TM = 128


def _combine_kernel(dcur_ref, dnxt_ref, yb_hbm, x1_ref, g2_ref, lng_ref, lnb_ref, out_ref,
                    ybuf0, ybuf1, sem):
    i = pl.program_id(0)
    last = pl.num_programs(0) - 1
    ybufs = (ybuf0, ybuf1)

    def row_copy(d, k, r, s):
        return pltpu.make_async_copy(
            yb_hbm.at[pl.ds(pl.multiple_of(d * SUBLANES, SUBLANES), SUBLANES)],
            ybufs[s].at[pl.ds(pl.multiple_of((k * CB_TM + r) * SUBLANES, SUBLANES), SUBLANES)],
            sem.at[s])

    @pl.when(i == 0)
    def _():
        for k in range(TOP_K):
            def body(r, c, k=k):
                row_copy(dcur_ref[0, k, r], k, r, 0).start()
                return c
            lax.fori_loop(0, CB_TM, body, 0, unroll=8)

    for s in range(2):
        @pl.when(i % 2 == s)
        def _(s=s):
            pltpu.make_async_copy(yb_hbm.at[pl.ds(0, TOP_K * CB_TM * SUBLANES)], ybufs[s], sem.at[s]).wait()

            @pl.when(i < last)
            def _():
                for k in range(TOP_K):
                    for r in range(CB_TM):
                        row_copy(dnxt_ref[0, k, r], k, r, 1 - s).start(priority=r % 2)

            cols = []
            for c in range(SUBLANES):
                parts = [ybufs[s][pl.ds(k * CB_TM * SUBLANES + c, CB_TM, stride=SUBLANES), :]
                         for k in range(TOP_K)]
                cols.append((parts[0] + parts[1]) + (parts[2] + parts[3]))
            y = jnp.concatenate(cols, axis=1)
            out_ref[...] = _ln(DN_ALPHA * x1_ref[...] + g2_ref[0] * y) * lng_ref[...] + lnb_ref[...]


def _combine(dest3, yb, x1, g2, ln2g, ln2b, seq):
    t = x1.shape[0]
    nb = t // CB_TM
    per_b = seq // CB_TM
    return pl.pallas_call(
        _combine_kernel,
        grid=(nb,),
        in_specs=[pl.BlockSpec((1, TOP_K, CB_TM), lambda i: (i, 0, 0), memory_space=pltpu.SMEM),
                  pl.BlockSpec((1, TOP_K, CB_TM), lambda i: (jnp.minimum(i + 1, nb - 1), 0, 0),
                               memory_space=pltpu.SMEM),
                  pl.BlockSpec(memory_space=pl.ANY),
                  pl.BlockSpec((CB_TM, D_MODEL), lambda i: (i, 0)),
                  pl.BlockSpec((1, 1, D_MODEL), lambda i: (i // per_b, 0, 0)),
                  pl.BlockSpec((1, D_MODEL), lambda i: (0, 0)),
                  pl.BlockSpec((1, D_MODEL), lambda i: (0, 0))],
        out_specs=pl.BlockSpec((CB_TM, D_MODEL), lambda i: (i, 0)),
        out_shape=jax.ShapeDtypeStruct((t, D_MODEL), f32),
        scratch_shapes=[pltpu.VMEM((TOP_K * CB_TM * SUBLANES, LANES), f32),
                        pltpu.VMEM((TOP_K * CB_TM * SUBLANES, LANES), f32),
                        pltpu.SemaphoreType.DMA((2,))],
        compiler_params=_params(("arbitrary",)),
        name="combine",
    )(dest3, dest3, yb, x1, g2, ln2g, ln2b)


def _t5_bucket(dist):
    d = dist.astype(f32)
    large = REL_MAX_EXACT + jnp.log(jnp.maximum(d, float(REL_MAX_EXACT)) / REL_MAX_EXACT) / math.log(
        REL_MAX_DIST / REL_MAX_EXACT) * (REL_BUCKETS - REL_MAX_EXACT)
    large = jnp.minimum(large.astype(i32), REL_BUCKETS - 1)
    return jnp.where(dist < REL_MAX_EXACT, dist, large)


def _bias_indices():
    qi = jnp.arange(ATT_BLOCK)[:, None]
    ki = jnp.arange(2 * ATT_BLOCK)[None, :]
    didx = qi + ATT_BLOCK - ki
    buckets, bands = [], []
    for win, dil in DIL_PAIRS:
        buckets.append(_t5_bucket(jnp.clip(didx, 0, None) * dil))
        bands.append(((didx >= 0) & (didx <= win // dil)).astype(i32))
    return jnp.stack(buckets).astype(i32), jnp.stack(bands)


def _residue_perm(tm, dil):
    n = tm // dil
    dst = jnp.arange(tm)
    src = (dst % n) * dil + dst // n
    return (src[:, None] == jnp.arange(tm)[None, :]).astype(bf16)


def kernel(x, c, w_ada, b_ada, w_in, gm_ln_g, gm_ln_b, gm_w_s, gm_b_s, w_branch_a, w_branch_b, w_out,
           rel_bias, ln1_g, ln1_b, w_router, b_router, w_gate, b_gate, w_up, b_up, w_down, b_down,
           ln2_g, ln2_b):
    batch, seq, _ = x.shape
    t = batch * seq
    l = 0
    x2 = x.reshape(t, D_MODEL)

    c8 = jnp.pad(c, ((0, 8 - batch), (0, 0)))
    mod = _adaln(c8, w_ada[l], b_ada[l][None, :])[:batch]
    sh1, sc1, g1, sh2, sc2, g2 = [m[:, None, :] for m in jnp.split(mod, 6, axis=-1)]

    perms = [_residue_perm(IN_TM, dil) for _win, dil in DIL_PAIRS]
    uv, gates, *qkvs = _inproj(x2, sc1, sh1, w_in[l].astype(bf16), perms, batch, seq)

    bs_full = jnp.repeat(gm_b_s[l].T, GM_WIDTH // GM_GROUPS, axis=1)
    ya = _gmlp(uv, gm_ln_g[l][None, :], gm_ln_b[l][None, :], gm_w_s[l], bs_full)

    bucket, band = _bias_indices()
    bias = _relbias(rel_bias, bucket, band)
    os_, ls_ = [], []
    for g, (_win, dil) in enumerate(DIL_PAIRS):
        o, lse = _attn_group(qkvs[g], bias, g, dil, batch, seq)
        os_.append(o)
        ls_.append(lse)

    wr = jnp.pad(w_router[l], ((0, 0), (0, LANES - N_EXPERTS)))
    br = jnp.pad(b_router[l], (0, LANES - N_EXPERTS))[None, :]
    tri = (jnp.arange(MIX_TM)[None, :] < jnp.arange(MIX_TM)[:, None]).astype(bf16)
    perms_t = [_residue_perm(MIX_TM, dil).T for _win, dil in DIL_PAIRS]
    expand = (jnp.arange(LANES)[:, None] == jnp.arange(ATT_WIDTH)[None, :] // HEAD_DIM).astype(bf16)
    x1, h2, route, rw, cnt = _mix(
        os_, ls_, perms_t, expand, ya, gates, x2, g1, sc2, sh2,
        w_branch_a[l].astype(bf16), w_branch_b[l].astype(bf16), w_out[l].astype(bf16),
        ln1_g[l][None, :], ln1_b[l][None, :], wr, br, tri, seq)

    top_e = route[:, :TOP_K]
    rank = route[:, TOP_K:2 * TOP_K]
    top_w = rw[:, :TOP_K]
    counts = cnt[0, :N_EXPERTS].astype(i32)
    pcounts = (counts + MOE_TM - 1) // MOE_TM * MOE_TM
    pends = jnp.cumsum(pcounts)
    pstarts = pends - pcounts
    dest = pstarts[top_e] + rank
    a_total = t * TOP_K
    ntile = a_total // MOE_TM + N_EXPERTS
    r_total = ntile * MOE_TM
    inv = jnp.full((r_total,), -1, i32).at[dest.reshape(-1)].set(
        jnp.arange(a_total, dtype=i32), unique_indices=True, mode='promise_in_bounds')
    valid = inv >= 0
    safe = jnp.maximum(inv, 0)
    row_tok = jnp.where(valid, safe // TOP_K, 0)
    row_w = jnp.where(valid, top_w.reshape(-1)[safe], 0.0)
    n_used = pends[-1] // MOE_TM
    tile_idx = jnp.minimum(jnp.arange(ntile, dtype=i32), n_used - 1)
    tile_e = jnp.sum((pends[None, :] <= (tile_idx * MOE_TM)[:, None]).astype(i32), axis=1)
    tile_first = jnp.concatenate([jnp.ones((1,), i32), (tile_e[1:] != tile_e[:-1]).astype(i32)])
    experts = jnp.arange(N_EXPERTS, dtype=i32)
    nonempty = counts > 0
    later = lax.cummin(jnp.where(nonempty, experts, N_EXPERTS), reverse=True)
    next_nonempty = jnp.concatenate([later[1:], jnp.full((1,), N_EXPERTS, i32)])
    next_nonempty = jnp.where(next_nonempty >= N_EXPERTS, -1, next_nonempty)
    expert_slot = (jnp.cumsum(nonempty.astype(i32)) - 1) % 2

    yb = _moe(tile_e, tile_first, next_nonempty[tile_e], expert_slot[tile_e], n_used.reshape(1),
              row_tok.reshape(ntile, 1, MOE_TM), h2, row_w[:, None],
              w_gate[l], b_gate[l][:, None, :], w_up[l], b_up[l][:, None, :],
              w_down[l], b_down[l][:, None, :])

    dest3 = dest.reshape(t // CB_TM, CB_TM, TOP_K).transpose(0, 2, 1)
    out = _combine(dest3, yb, x1, g2, ln2_g[l][None, :], ln2_b[l][None, :], seq)
    return out.reshape(batch, seq, D_MODEL)
```

```python
import functools
import math

import jax
import jax.numpy as jnp
from jax import lax
from jax.experimental import pallas as pl
from jax.experimental.pallas import tpu as pltpu

f32 = jnp.float32
bf16 = jnp.bfloat16
i32 = jnp.int32

D_MODEL = 1024
GM_WIDTH = 512
GM_GROUPS = 8
GM_CHUNK = 128
DIL_PAIRS = ((128, 1), (512, 4), (2048, 16))
N_DIL = 3
HEADS_PER_GROUP = 8
HEAD_DIM = 64
ATT_WIDTH = 512
ATT_BLOCK = 128
NEG_INF = -1e30
REL_BUCKETS = 32
REL_MAX_EXACT = 16
REL_MAX_DIST = 2048
N_EXPERTS = 32
TOP_K = 4
SWIGLU_LIMIT = 7.0
SWIGLU_ALPHA = 1.702
MOE_BLOCK = 128
DEPTH = 1
DN_ALPHA = (2 * DEPTH) ** 0.25
LN_EPS = 1e-5
UV_COLS = 2 * GM_WIDTH
QKV_COLS = N_DIL * 3 * ATT_WIDTH
GATE_COLS = 2 * D_MODEL
IN_COLS = UV_COLS + QKV_COLS + GATE_COLS

LANES = 128
SUBLANES = 8
VMEM_LIMIT = 56 * 1024 * 1024


def _ln(x):
    mu = jnp.mean(x, axis=-1, keepdims=True)
    xc = x - mu
    var = jnp.mean(xc * xc, axis=-1, keepdims=True)
    return xc * lax.rsqrt(var + LN_EPS)


def _params(sem, vmem=VMEM_LIMIT):
    return pltpu.CompilerParams(dimension_semantics=sem, vmem_limit_bytes=vmem)


def _adaln_kernel(c_ref, w_ref, b_ref, o_ref):
    c = c_ref[...]
    s = c * jax.nn.sigmoid(c)
    o_ref[...] = jnp.dot(s, w_ref[...], preferred_element_type=f32,
                         precision=lax.Precision.HIGHEST) + b_ref[...]


def _adaln(c8, w_ada, b_ada):
    n = w_ada.shape[1] // D_MODEL
    return pl.pallas_call(
        _adaln_kernel,
        grid=(n,),
        in_specs=[pl.BlockSpec((8, D_MODEL), lambda j: (0, 0)),
                  pl.BlockSpec((D_MODEL, D_MODEL), lambda j: (0, j)),
                  pl.BlockSpec((1, D_MODEL), lambda j: (0, j))],
        out_specs=pl.BlockSpec((8, D_MODEL), lambda j: (0, j)),
        out_shape=jax.ShapeDtypeStruct((8, w_ada.shape[1]), f32),
        compiler_params=_params(("arbitrary",)),
        name="adaln",
    )(c8, w_ada, b_ada)


IN_TM = 256
IN_CW = 512
GRP_COLS = 3 * ATT_WIDTH


def _inproj_kernel(x_ref, sc_ref, sh_ref, w_ref, p1_ref, p2_ref,
                   uv_ref, gt_ref, qkv0_ref, qkv1_ref, qkv2_ref):
    xn = _ln(x_ref[...])
    h = (xn * (1.0 + sc_ref[0]) + sh_ref[0]).astype(bf16)
    hp = [h,
          jnp.dot(p1_ref[...], h, preferred_element_type=f32).astype(bf16),
          jnp.dot(p2_ref[...], h, preferred_element_type=f32).astype(bf16)]
    for c0 in range(0, UV_COLS, IN_CW):
        acc = jnp.dot(h, w_ref[:, c0:c0 + IN_CW], preferred_element_type=f32)
        uv_ref[:, c0:c0 + IN_CW] = jax.nn.gelu(acc).astype(bf16)
    for g, (qref, (_win, dil)) in enumerate(zip((qkv0_ref, qkv1_ref, qkv2_ref), DIL_PAIRS)):
        n = IN_TM // dil
        for q0 in range(0, GRP_COLS, IN_CW):
            c0 = UV_COLS + g * GRP_COLS + q0
            acc = jnp.dot(hp[g], w_ref[:, c0:c0 + IN_CW], preferred_element_type=f32).astype(bf16)
            for rho in range(dil):
                qref[0, rho, :, q0:q0 + IN_CW] = acc[rho * n:(rho + 1) * n, :]
    for g0 in range(0, GATE_COLS, IN_CW):
        c0 = UV_COLS + QKV_COLS + g0
        acc = jnp.dot(h, w_ref[:, c0:c0 + IN_CW], preferred_element_type=f32)
        gt_ref[:, g0:g0 + IN_CW] = jax.nn.sigmoid(acc).astype(bf16)


def _inproj(x2, sc1, sh1, w_in_bf, perms, batch, seq):
    t = x2.shape[0]
    per_b = seq // IN_TM
    qkv_specs, qkv_shapes = [], []
    for _win, dil in DIL_PAIRS:
        n = IN_TM // dil
        qkv_specs.append(pl.BlockSpec((1, dil, n, GRP_COLS), lambda i: (i // per_b, 0, i % per_b, 0)))
        qkv_shapes.append(jax.ShapeDtypeStruct((batch, dil, seq // dil, GRP_COLS), bf16))
    return pl.pallas_call(
        _inproj_kernel,
        grid=(t // IN_TM,),
        in_specs=[pl.BlockSpec((IN_TM, D_MODEL), lambda i: (i, 0)),
                  pl.BlockSpec((1, 1, D_MODEL), lambda i: (i // per_b, 0, 0)),
                  pl.BlockSpec((1, 1, D_MODEL), lambda i: (i // per_b, 0, 0)),
                  pl.BlockSpec((D_MODEL, IN_COLS), lambda i: (0, 0)),
                  pl.BlockSpec((IN_TM, IN_TM), lambda i: (0, 0)),
                  pl.BlockSpec((IN_TM, IN_TM), lambda i: (0, 0))],
        out_specs=[pl.BlockSpec((IN_TM, UV_COLS), lambda i: (i, 0)),
                   pl.BlockSpec((IN_TM, GATE_COLS), lambda i: (i, 0))] + qkv_specs,
        out_shape=[jax.ShapeDtypeStruct((t, UV_COLS), bf16),
                   jax.ShapeDtypeStruct((t, GATE_COLS), bf16)] + qkv_shapes,
        compiler_params=_params(("arbitrary",)),
        name="inproj",
    )(x2, sc1, sh1, w_in_bf, perms[1], perms[2])


GM_TM = 512


def _gmlp_kernel(u_ref, v_ref, g_ref, b_ref, ws_ref, bs_ref, ya_ref):
    row = lax.broadcasted_iota(i32, (GM_CHUNK, GM_CHUNK), 0)
    col = lax.broadcasted_iota(i32, (GM_CHUNK, GM_CHUNK), 1)
    causal = col <= row
    first_half = lax.broadcasted_iota(i32, (GM_CHUNK, LANES), 1) < (GM_WIDTH // GM_GROUPS)
    ws = [jnp.where(causal, ws_ref[g], 0.0).astype(bf16) for g in range(GM_GROUPS)]
    for ch in range(GM_TM // GM_CHUNK):
        r0 = ch * GM_CHUNK
        vn = _ln(v_ref[r0:r0 + GM_CHUNK, :].astype(f32)) * g_ref[...] + b_ref[...]
        vn = vn.astype(bf16)
        for j in range(GM_WIDTH // LANES):
            slab = vn[:, j * LANES:(j + 1) * LANES]
            s_lo = jnp.dot(ws[2 * j], slab, preferred_element_type=f32)
            s_hi = jnp.dot(ws[2 * j + 1], slab, preferred_element_type=f32)
            s = jnp.where(first_half, s_lo, s_hi) + bs_ref[:, j * LANES:(j + 1) * LANES]
            u = u_ref[r0:r0 + GM_CHUNK, j * LANES:(j + 1) * LANES].astype(f32)
            ya_ref[r0:r0 + GM_CHUNK, j * LANES:(j + 1) * LANES] = (u * s).astype(bf16)


def _gmlp(uv, ln_g, ln_b, w_s, bs_full):
    t = uv.shape[0]
    return pl.pallas_call(
        _gmlp_kernel,
        grid=(t // GM_TM,),
        in_specs=[pl.BlockSpec((GM_TM, GM_WIDTH), lambda i: (i, 0)),
                  pl.BlockSpec((GM_TM, GM_WIDTH), lambda i: (i, 1)),
                  pl.BlockSpec((1, GM_WIDTH), lambda i: (0, 0)),
                  pl.BlockSpec((1, GM_WIDTH), lambda i: (0, 0)),
                  pl.BlockSpec((GM_GROUPS, GM_CHUNK, GM_CHUNK), lambda i: (0, 0, 0)),
                  pl.BlockSpec((GM_CHUNK, GM_WIDTH), lambda i: (0, 0))],
        out_specs=pl.BlockSpec((GM_TM, GM_WIDTH), lambda i: (i, 0)),
        out_shape=jax.ShapeDtypeStruct((t, GM_WIDTH), bf16),
        compiler_params=_params(("arbitrary",)),
        name="gmlp",
    )(uv, uv, ln_g, ln_b, w_s, bs_full)


def _relbias_kernel(tab_ref, bucket_ref, band_ref, out_ref):
    g = pl.program_id(0)
    bk = bucket_ref[0]
    band = band_ref[0] > 0
    for h in range(HEADS_PER_GROUP):
        acc = jnp.zeros((ATT_BLOCK, 2 * ATT_BLOCK), f32)
        for b in range(REL_BUCKETS):
            acc = jnp.where(bk == b, tab_ref[b, g * HEADS_PER_GROUP + h], acc)
        out_ref[0, h] = jnp.where(band, acc, NEG_INF)


def _relbias(rel_bias, bucket, band):
    return pl.pallas_call(
        _relbias_kernel,
        grid=(N_DIL,),
        in_specs=[pl.BlockSpec(memory_space=pltpu.SMEM),
                  pl.BlockSpec((1, ATT_BLOCK, 2 * ATT_BLOCK), lambda g: (g, 0, 0)),
                  pl.BlockSpec((1, ATT_BLOCK, 2 * ATT_BLOCK), lambda g: (g, 0, 0))],
        out_specs=pl.BlockSpec((1, HEADS_PER_GROUP, ATT_BLOCK, 2 * ATT_BLOCK),
                               lambda g: (g, 0, 0, 0)),
        out_shape=jax.ShapeDtypeStruct((N_DIL, HEADS_PER_GROUP, ATT_BLOCK, 2 * ATT_BLOCK), f32),
        compiler_params=_params(("arbitrary",)),
        name="relbias",
    )(rel_bias, bucket, band)


def _attn_kernel(q_ref, kp_ref, kc_ref, vp_ref, vc_ref, bias_ref, o_ref, lse_ref):
    first = pl.program_id(2) == 0
    lane = lax.broadcasted_iota(i32, (ATT_BLOCK, LANES), 1)
    lo_half = lane < HEAD_DIM
    nt = (((1,), (1,)), ((), ()))
    ones = jnp.ones((2 * ATT_BLOCK, LANES), bf16)
    n_slab = ATT_WIDTH // LANES
    logits, v_ext = [], []
    for j in range(n_slab):
        sl = slice(j * LANES, (j + 1) * LANES)
        q = q_ref[0, 0, :, sl] * (HEAD_DIM ** -0.5)
        k_cat = jnp.concatenate([kp_ref[0, 0, :, sl], kc_ref[0, 0, :, sl]], axis=0)
        v_cat = jnp.concatenate([vp_ref[0, 0, :, sl], vc_ref[0, 0, :, sl]], axis=0)
        v_ext.append(jnp.concatenate([v_cat, ones], axis=1))
        for hh in range(2):
            qm = jnp.where(lo_half if hh == 0 else jnp.logical_not(lo_half), q, 0.0).astype(bf16)
            logits.append(lax.dot_general(qm, k_cat, nt, preferred_element_type=f32))
    lg = jnp.concatenate(logits, axis=0) + bias_ref[0].reshape(HEADS_PER_GROUP * ATT_BLOCK, 2 * ATT_BLOCK)
    prev_cols = lax.broadcasted_iota(i32, lg.shape, 1) < ATT_BLOCK
    lg = jnp.where(jnp.logical_and(first, prev_cols), NEG_INF, lg)
    m = jnp.max(lg, axis=-1, keepdims=True)
    p = jnp.exp(lg - m).astype(bf16)
    lse_tile = jnp.zeros((ATT_BLOCK, LANES), f32)
    for j in range(n_slab):
        outs = []
        for hh in range(2):
            h = 2 * j + hh
            r = jnp.dot(p[h * ATT_BLOCK:(h + 1) * ATT_BLOCK], v_ext[j], preferred_element_type=f32)
            den = r[:, LANES:]
            outs.append(r[:, :LANES] * (1.0 / den))
            lse_h = m[h * ATT_BLOCK:(h + 1) * ATT_BLOCK] + jnp.log(den)
            lse_tile = jnp.where(lane == h, lse_h, lse_tile)
        o_ref[0, 0, :, j * LANES:(j + 1) * LANES] = jnp.where(lo_half, outs[0], outs[1]).astype(bf16)
    lse_ref[0, 0] = lse_tile


def _attn_group(qkv_g, bias, g, dil, batch, seq):
    l = seq // dil
    nb = l // ATT_BLOCK

    def spec(cb, prev):
        if prev:
            return pl.BlockSpec((1, 1, ATT_BLOCK, ATT_WIDTH),
                                lambda b, r, n: (b, r, jnp.maximum(n - 1, 0), cb))
        return pl.BlockSpec((1, 1, ATT_BLOCK, ATT_WIDTH), lambda b, r, n: (b, r, n, cb))

    return pl.pallas_call(
        _attn_kernel,
        grid=(batch, dil, nb),
        in_specs=[spec(0, False), spec(1, True), spec(1, False), spec(2, True), spec(2, False),
                  pl.BlockSpec((1, HEADS_PER_GROUP, ATT_BLOCK, 2 * ATT_BLOCK),
                               lambda b, r, n: (g, 0, 0, 0))],
        out_specs=[pl.BlockSpec((1, 1, ATT_BLOCK, ATT_WIDTH), lambda b, r, n: (b, r, n, 0)),
                   pl.BlockSpec((1, 1, ATT_BLOCK, LANES), lambda b, r, n: (b, r, n, 0))],
        out_shape=[jax.ShapeDtypeStruct((batch, dil, l, ATT_WIDTH), bf16),
                   jax.ShapeDtypeStruct((batch, dil, l, LANES), f32)],
        compiler_params=_params(("arbitrary", "arbitrary", "arbitrary")),
        name=f"attn_g{g}",
    )(qkv_g, qkv_g, qkv_g, qkv_g, qkv_g, bias)


ROW_WORDS = D_MODEL // 2
ROW_SUB = ROW_WORDS // LANES
HI_MASK = -65536


def _pack_rows(x):
    bits = lax.bitcast_convert_type(x.astype(bf16).astype(f32), i32)
    return lax.shift_right_logical(bits[:, :ROW_WORDS], 16) | (bits[:, ROW_WORDS:] & HI_MASK)


def _unpack_rows(words):
    lo = lax.bitcast_convert_type(lax.shift_left(words, 16), f32)
    hi = lax.bitcast_convert_type(words & HI_MASK, f32)
    return jnp.concatenate([lo, hi], axis=1)


def _store_packed(ref, words, n):
    for r in range(ROW_SUB):
        ref[pl.ds(r, n, stride=ROW_SUB), :] = words[:, r * LANES:(r + 1) * LANES]


def _load_packed(ref, first_row, n):
    return jnp.concatenate([ref[pl.ds(first_row * ROW_SUB + r, n, stride=ROW_SUB), :] for r in range(ROW_SUB)],
                           axis=1)


MIX_TM = 256


def _split_bf16(x, parts):
    out = []
    for _ in range(parts):
        hi = x.astype(bf16)
        out.append(hi)
        x = x - hi.astype(f32)
    return out


def _mix_kernel(o0_ref, o1_ref, o2_ref, l0_ref, l1_ref, l2_ref, pt1_ref, pt2_ref, ex_ref,
                ya_ref, gt_ref, x_ref,
                g1_ref, sc2_ref, sh2_ref, wa_ref, wb_ref, wo_ref, ln1g_ref, ln1b_ref,
                wr_ref, br_ref, tri_ref,
                x1_ref, h2_ref, route_ref, rw_ref, cnt_ref, run_ref):
    @pl.when(pl.program_id(0) == 0)
    def _():
        run_ref[...] = jnp.zeros_like(run_ref)

    os_, ls_ = [], []
    for o_ref, l_ref, pt_ref in ((o0_ref, l0_ref, None), (o1_ref, l1_ref, pt1_ref), (o2_ref, l2_ref, pt2_ref)):
        o = o_ref[0].reshape(MIX_TM, ATT_WIDTH)
        lse = l_ref[0].reshape(MIX_TM, LANES)
        if pt_ref is None:
            os_.append(o.astype(f32))
            ls_.append(lse)
        else:
            pt = pt_ref[...]
            os_.append(jnp.dot(pt, o, preferred_element_type=f32))
            parts = [jnp.dot(pt, part, preferred_element_type=f32) for part in _split_bf16(lse, 3)]
            ls_.append((parts[0] + parts[1]) + parts[2])
    lm = jnp.maximum(jnp.maximum(ls_[0], ls_[1]), ls_[2])
    es = [jnp.exp(lse - lm) for lse in ls_]
    inv = 1.0 / (es[0] + es[1] + es[2])
    yb = jnp.zeros((MIX_TM, ATT_WIDTH), f32)
    for e, o in zip(es, os_):
        w_hi, w_lo = _split_bf16(e * inv, 2)
        w_full = (jnp.dot(w_hi, ex_ref[...], preferred_element_type=f32)
                  + jnp.dot(w_lo, ex_ref[...], preferred_element_type=f32))
        yb = yb + w_full * o
    a = jnp.dot(ya_ref[...], wa_ref[...], preferred_element_type=f32)
    b = jnp.dot(yb.astype(bf16), wb_ref[...], preferred_element_type=f32)
    merged = gt_ref[:, :D_MODEL].astype(f32) * a + gt_ref[:, D_MODEL:].astype(f32) * b
    mix = jnp.dot(merged.astype(bf16), wo_ref[...], preferred_element_type=f32)
    x1 = _ln(DN_ALPHA * x_ref[...] + g1_ref[0] * mix) * ln1g_ref[...] + ln1b_ref[...]
    x1_ref[...] = x1
    h2 = _ln(x1) * (1.0 + sc2_ref[0]) + sh2_ref[0]
    _store_packed(h2_ref, _pack_rows(h2), MIX_TM)

    lane = lax.broadcasted_iota(i32, (MIX_TM, LANES), 1)
    logits = jnp.dot(h2, wr_ref[...], preferred_element_type=f32,
                     precision=lax.Precision.HIGHEST) + br_ref[...]
    logits = jnp.where(lane < N_EXPERTS, logits, -jnp.inf)
    lane_f = lane.astype(f32)
    vals, idxs = [], []
    for _k in range(TOP_K):
        m = jnp.max(logits, axis=-1, keepdims=True)
        idx = jnp.min(jnp.where(logits == m, lane_f, float(LANES)), axis=-1, keepdims=True).astype(i32)
        vals.append(m)
        idxs.append(idx)
        logits = jnp.where(lane == idx, -jnp.inf, logits)
    exps = [jnp.exp(v - vals[0]) for v in vals]
    den = exps[0] + exps[1] + exps[2] + exps[3]

    hits = [lane == idx for idx in idxs]
    onehot = jnp.zeros((MIX_TM, LANES), f32)
    for hit in hits:
        onehot = onehot + jnp.where(hit, 1.0, 0.0)
    prefix = jnp.dot(tri_ref[...], onehot.astype(bf16), preferred_element_type=f32) + run_ref[...]
    route = jnp.zeros((MIX_TM, LANES), i32)
    rw = jnp.zeros((MIX_TM, LANES), f32)
    for k in range(TOP_K):
        rank = jnp.sum(jnp.where(hits[k], prefix, 0.0), axis=-1, keepdims=True).astype(i32)
        route = jnp.where(lane == k, idxs[k], route)
        route = jnp.where(lane == TOP_K + k, rank, route)
        rw = jnp.where(lane == k, exps[k] / den, rw)
    route_ref[...] = route
    rw_ref[...] = rw
    run = run_ref[...] + jnp.sum(onehot, axis=0, keepdims=True)
    run_ref[...] = run
    cnt_ref[...] = jnp.broadcast_to(run, cnt_ref.shape)


def _mix(os_, ls_, perms_t, expand, ya, gates, x2, g1, sc2, sh2, wa, wb, wo, ln1g, ln1b, wr, br, tri, seq):
    t = x2.shape[0]
    per_b = seq // MIX_TM
    row = lambda w: pl.BlockSpec((MIX_TM, w), lambda i: (i, 0))
    const = lambda s: pl.BlockSpec(s, lambda i: tuple(0 for _ in s))
    modb = pl.BlockSpec((1, 1, D_MODEL), lambda i: (i // per_b, 0, 0))
    grp = lambda w: [pl.BlockSpec((1, dil, MIX_TM // dil, w), lambda i: (i // per_b, 0, i % per_b, 0))
                     for _win, dil in DIL_PAIRS]
    return pl.pallas_call(
        _mix_kernel,
        grid=(t // MIX_TM,),
        in_specs=grp(ATT_WIDTH) + grp(LANES) + [
                  const((MIX_TM, MIX_TM)), const((MIX_TM, MIX_TM)), const((LANES, ATT_WIDTH)),
                  row(GM_WIDTH), row(GATE_COLS), row(D_MODEL),
                  modb, modb, modb,
                  const((GM_WIDTH, D_MODEL)), const((ATT_WIDTH, D_MODEL)), const((D_MODEL, D_MODEL)),
                  const((1, D_MODEL)), const((1, D_MODEL)),
                  const((D_MODEL, LANES)), const((1, LANES)), const((MIX_TM, MIX_TM))],
        out_specs=[row(D_MODEL), pl.BlockSpec((MIX_TM * ROW_SUB, LANES), lambda i: (i, 0)),
                   row(LANES), row(LANES), const((8, LANES))],
        out_shape=[jax.ShapeDtypeStruct((t, D_MODEL), f32),
                   jax.ShapeDtypeStruct((t * ROW_SUB, LANES), i32),
                   jax.ShapeDtypeStruct((t, LANES), i32),
                   jax.ShapeDtypeStruct((t, LANES), f32),
                   jax.ShapeDtypeStruct((8, LANES), f32)],
        scratch_shapes=[pltpu.VMEM((1, LANES), f32)],
        compiler_params=_params(("arbitrary",)),
        name="mix",
    )(*os_, *ls_, perms_t[1], perms_t[2], expand, ya, gates, x2, g1, sc2, sh2, wa, wb, wo,
      ln1g, ln1b, wr, br, tri)


MOE_TM = 256


def _moe_kernel(te_ref, first_ref, nexte_ref, wslot_ref, nused_ref,
                tok_cur_ref, tok_nxt_ref, h2_hbm, rww_ref, wg_hbm, wu_hbm, wd_hbm,
                bg_ref, bu_ref, bd_ref,
                out_ref, xbuf0, xbuf1, wbuf, wgb, wub, wdb, sem_x, sem_w):
    j = pl.program_id(0)
    last = pl.num_programs(0) - 1
    xbufs = (xbuf0, xbuf1)

    def row_copy(tok, i, s):
        return pltpu.make_async_copy(h2_hbm.at[pl.ds(pl.multiple_of(tok * ROW_SUB, ROW_SUB), ROW_SUB)],
                                     xbufs[s].at[pl.ds(pl.multiple_of(i * ROW_SUB, ROW_SUB), ROW_SUB)],
                                     sem_x.at[s])

    def rows_wait(s):
        pltpu.make_async_copy(h2_hbm.at[pl.ds(0, MOE_TM * ROW_SUB)], xbufs[s], sem_x.at[s]).wait()

    def weight_copies(e, ws):
        return [pltpu.make_async_copy(w.at[e], wbuf.at[ws, k], sem_w.at[ws])
                for k, w in enumerate((wg_hbm, wu_hbm, wd_hbm))]

    def gather_loop(tok_ref, s):
        def body(i, c):
            row_copy(tok_ref[0, 0, i], i, s).start()
            return c
        lax.fori_loop(0, MOE_TM, body, 0, unroll=8)

    @pl.when(j == 0)
    def _():
        gather_loop(tok_cur_ref, 0)
        for cp in weight_copies(te_ref[0], wslot_ref[0]):
            cp.start(priority=1)

    def load_weights():
        ws = wslot_ref[j]
        for cp in weight_copies(te_ref[j], ws):
            cp.wait()
        wgb[...] = wbuf[ws, 0].astype(bf16)
        wub[...] = wbuf[ws, 1].astype(bf16)
        wdb[...] = wbuf[ws, 2].astype(bf16)
        ne = nexte_ref[j]

        @pl.when(ne >= 0)
        def _():
            for cp in weight_copies(ne, 1 - ws):
                cp.start(priority=1)

    def expert_mlp(s):
        for i in range(MOE_TM):
            row_copy(tok_nxt_ref[0, 0, i], i, 1 - s).start(priority=i % 2)
        xb = _unpack_rows(_load_packed(xbufs[s], 0, MOE_TM)).astype(bf16)
        g = jnp.dot(xb, wgb[...], preferred_element_type=f32) + bg_ref[0]
        u = jnp.dot(xb, wub[...], preferred_element_type=f32) + bu_ref[0]
        g = jnp.minimum(g, SWIGLU_LIMIT)
        u = jnp.clip(u, -SWIGLU_LIMIT, SWIGLU_LIMIT)
        act = (u + 1.0) * (g * jax.nn.sigmoid(SWIGLU_ALPHA * g))
        y = (jnp.dot(act.astype(bf16), wdb[...], preferred_element_type=f32) + bd_ref[0]) * rww_ref[...]
        _store_packed(out_ref, _pack_rows(y), MOE_TM)

    def idle_tile(s):
        gather_loop(tok_nxt_ref, s)
        out_ref[...] = jnp.zeros_like(out_ref)

    used = j < nused_ref[0]
    for s in range(2):
        @pl.when(j % 2 == s)
        def _(s=s):
            rows_wait(s)
            pl.when(first_ref[j] == 1)(load_weights)
            pl.when(used)(functools.partial(expert_mlp, s))
            pl.when(jnp.logical_not(used))(functools.partial(idle_tile, 1 - s))
            pl.when(j == last)(functools.partial(rows_wait, 1 - s))


def _moe(tile_e, tile_first, next_e, wslot, n_used, row_tok3, h2p, row_w,
         w_gate, b_gate, w_up, b_up, w_down, b_down):
    ntile = tile_e.shape[0]
    bspec = pl.BlockSpec((1, 1, D_MODEL), lambda j, te, *_: (te[j], 0, 0))
    hbm = pl.BlockSpec(memory_space=pl.ANY)
    grid_spec = pltpu.PrefetchScalarGridSpec(
        num_scalar_prefetch=5,
        grid=(ntile,),
        in_specs=[pl.BlockSpec((1, 1, MOE_TM), lambda j, *_: (j, 0, 0), memory_space=pltpu.SMEM),
                  pl.BlockSpec((1, 1, MOE_TM), lambda j, *_: (jnp.minimum(j + 1, ntile - 1), 0, 0),
                               memory_space=pltpu.SMEM),
                  hbm,
                  pl.BlockSpec((MOE_TM, 1), lambda j, *_: (j, 0)),
                  hbm, hbm, hbm, bspec, bspec, bspec],
        out_specs=pl.BlockSpec((MOE_TM * ROW_SUB, LANES), lambda j, *_: (j, 0)),
        scratch_shapes=[pltpu.VMEM((MOE_TM * ROW_SUB, LANES), i32),
                        pltpu.VMEM((MOE_TM * ROW_SUB, LANES), i32),
                        pltpu.VMEM((2, 3, D_MODEL, D_MODEL), f32),
                        pltpu.VMEM((D_MODEL, D_MODEL), bf16),
                        pltpu.VMEM((D_MODEL, D_MODEL), bf16),
                        pltpu.VMEM((D_MODEL, D_MODEL), bf16),
                        pltpu.SemaphoreType.DMA((2,)),
                        pltpu.SemaphoreType.DMA((2,))],
    )
    return pl.pallas_call(
        _moe_kernel,
        grid_spec=grid_spec,
        out_shape=jax.ShapeDtypeStruct((ntile * MOE_TM * ROW_SUB, LANES), i32),
        compiler_params=_params(("arbitrary",)),
        name="moe",
    )(tile_e, tile_first, next_e, wslot, n_used, row_tok3, row_tok3, h2p, row_w,
      w_gate, w_up, w_down, b_gate, b_up, b_down)


CB_TM = 128


def _combine_kernel(dcur_ref, dnxt_ref, yb_hbm, x1_ref, g2_ref, lng_ref, lnb_ref, out_ref,
                    ybuf0, ybuf1, sem):
    i = pl.program_id(0)
    last = pl.num_programs(0) - 1
    ybufs = (ybuf0, ybuf1)

    def row_copy(d, k, r, s):
        return pltpu.make_async_copy(
            yb_hbm.at[pl.ds(pl.multiple_of(d * ROW_SUB, ROW_SUB), ROW_SUB)],
            ybufs[s].at[pl.ds(pl.multiple_of((k * CB_TM + r) * ROW_SUB, ROW_SUB), ROW_SUB)],
            sem.at[s])

    @pl.when(i == 0)
    def _():
        for k in range(TOP_K):
            def body(r, c, k=k):
                row_copy(dcur_ref[0, k, r], k, r, 0).start()
                return c
            lax.fori_loop(0, CB_TM, body, 0, unroll=8)

    for s in range(2):
        @pl.when(i % 2 == s)
        def _(s=s):
            pltpu.make_async_copy(yb_hbm.at[pl.ds(0, TOP_K * CB_TM * ROW_SUB)], ybufs[s], sem.at[s]).wait()

            @pl.when(i < last)
            def _():
                for k in range(TOP_K):
                    for r in range(CB_TM):
                        row_copy(dnxt_ref[0, k, r], k, r, 1 - s).start(priority=r % 2)

            parts = [_unpack_rows(_load_packed(ybufs[s], k * CB_TM, CB_TM)) for k in range(TOP_K)]
            y = (parts[0] + parts[1]) + (parts[2] + parts[3])
            out_ref[...] = _ln(DN_ALPHA * x1_ref[...] + g2_ref[0] * y) * lng_ref[...] + lnb_ref[...]


def _combine(dest3, yb, x1, g2, ln2g, ln2b, seq):
    t = x1.shape[0]
    nb = t // CB_TM
    per_b = seq // CB_TM
    return pl.pallas_call(
        _combine_kernel,
        grid=(nb,),
        in_specs=[pl.BlockSpec((1, TOP_K, CB_TM), lambda i: (i, 0, 0), memory_space=pltpu.SMEM),
                  pl.BlockSpec((1, TOP_K, CB_TM), lambda i: (jnp.minimum(i + 1, nb - 1), 0, 0),
                               memory_space=pltpu.SMEM),
                  pl.BlockSpec(memory_space=pl.ANY),
                  pl.BlockSpec((CB_TM, D_MODEL), lambda i: (i, 0)),
                  pl.BlockSpec((1, 1, D_MODEL), lambda i: (i // per_b, 0, 0)),
                  pl.BlockSpec((1, D_MODEL), lambda i: (0, 0)),
                  pl.BlockSpec((1, D_MODEL), lambda i: (0, 0))],
        out_specs=pl.BlockSpec((CB_TM, D_MODEL), lambda i: (i, 0)),
        out_shape=jax.ShapeDtypeStruct((t, D_MODEL), f32),
        scratch_shapes=[pltpu.VMEM((TOP_K * CB_TM * ROW_SUB, LANES), i32),
                        pltpu.VMEM((TOP_K * CB_TM * ROW_SUB, LANES), i32),
                        pltpu.SemaphoreType.DMA((2,))],
        compiler_params=_params(("arbitrary",)),
        name="combine",
    )(dest3, dest3, yb, x1, g2, ln2g, ln2b)


def _t5_bucket(dist):
    d = dist.astype(f32)
    large = REL_MAX_EXACT + jnp.log(jnp.maximum(d, float(REL_MAX_EXACT)) / REL_MAX_EXACT) / math.log(
        REL_MAX_DIST / REL_MAX_EXACT) * (REL_BUCKETS - REL_MAX_EXACT)
    large = jnp.minimum(large.astype(i32), REL_BUCKETS - 1)
    return jnp.where(dist < REL_MAX_EXACT, dist, large)


def _bias_indices():
    qi = jnp.arange(ATT_BLOCK)[:, None]
    ki = jnp.arange(2 * ATT_BLOCK)[None, :]
    didx = qi + ATT_BLOCK - ki
    buckets, bands = [], []
    for win, dil in DIL_PAIRS:
        buckets.append(_t5_bucket(jnp.clip(didx, 0, None) * dil))
        bands.append(((didx >= 0) & (didx <= win // dil)).astype(i32))
    return jnp.stack(buckets).astype(i32), jnp.stack(bands)


def _residue_perm(tm, dil):
    n = tm // dil
    dst = jnp.arange(tm)
    src = (dst % n) * dil + dst // n
    return (src[:, None] == jnp.arange(tm)[None, :]).astype(bf16)


def kernel(x, c, w_ada, b_ada, w_in, gm_ln_g, gm_ln_b, gm_w_s, gm_b_s, w_branch_a, w_branch_b, w_out,
           rel_bias, ln1_g, ln1_b, w_router, b_router, w_gate, b_gate, w_up, b_up, w_down, b_down,
           ln2_g, ln2_b):
    batch, seq, _ = x.shape
    t = batch * seq
    l = 0
    x2 = x.reshape(t, D_MODEL)

    c8 = jnp.pad(c, ((0, 8 - batch), (0, 0)))
    mod = _adaln(c8, w_ada[l], b_ada[l][None, :])[:batch]
    sh1, sc1, g1, sh2, sc2, g2 = [m[:, None, :] for m in jnp.split(mod, 6, axis=-1)]

    perms = [_residue_perm(IN_TM, dil) for _win, dil in DIL_PAIRS]
    uv, gates, *qkvs = _inproj(x2, sc1, sh1, w_in[l].astype(bf16), perms, batch, seq)

    bs_full = jnp.repeat(gm_b_s[l].T, GM_WIDTH // GM_GROUPS, axis=1)
    ya = _gmlp(uv, gm_ln_g[l][None, :], gm_ln_b[l][None, :], gm_w_s[l], bs_full)

    bucket, band = _bias_indices()
    bias = _relbias(rel_bias, bucket, band)
    os_, ls_ = [], []
    for g, (_win, dil) in enumerate(DIL_PAIRS):
        o, lse = _attn_group(qkvs[g], bias, g, dil, batch, seq)
        os_.append(o)
        ls_.append(lse)

    wr = jnp.pad(w_router[l], ((0, 0), (0, LANES - N_EXPERTS)))
    br = jnp.pad(b_router[l], (0, LANES - N_EXPERTS))[None, :]
    tri = (jnp.arange(MIX_TM)[None, :] < jnp.arange(MIX_TM)[:, None]).astype(bf16)
    perms_t = [_residue_perm(MIX_TM, dil).T for _win, dil in DIL_PAIRS]
    expand = (jnp.arange(LANES)[:, None] == jnp.arange(ATT_WIDTH)[None, :] // HEAD_DIM).astype(bf16)
    x1, h2, route, rw, cnt = _mix(
        os_, ls_, perms_t, expand, ya, gates, x2, g1, sc2, sh2,
        w_branch_a[l].astype(bf16), w_branch_b[l].astype(bf16), w_out[l].astype(bf16),
        ln1_g[l][None, :], ln1_b[l][None, :], wr, br, tri, seq)

    top_e = route[:, :TOP_K]
    rank = route[:, TOP_K:2 * TOP_K]
    top_w = rw[:, :TOP_K]
    counts = cnt[0, :N_EXPERTS].astype(i32)
    pcounts = (counts + MOE_TM - 1) // MOE_TM * MOE_TM
    pends = jnp.cumsum(pcounts)
    pstarts = pends - pcounts
    dest = pstarts[top_e] + rank
    a_total = t * TOP_K
    ntile = a_total // MOE_TM + N_EXPERTS
    r_total = ntile * MOE_TM
    inv = jnp.full((r_total,), -1, i32).at[dest.reshape(-1)].set(
        jnp.arange(a_total, dtype=i32), unique_indices=True, mode='promise_in_bounds')
    valid = inv >= 0
    safe = jnp.maximum(inv, 0)
    row_tok = jnp.where(valid, safe // TOP_K, 0)
    row_w = jnp.where(valid, top_w.reshape(-1)[safe], 0.0)
    n_used = pends[-1] // MOE_TM
    tile_idx = jnp.minimum(jnp.arange(ntile, dtype=i32), n_used - 1)
    tile_e = jnp.sum((pends[None, :] <= (tile_idx * MOE_TM)[:, None]).astype(i32), axis=1)
    tile_first = jnp.concatenate([jnp.ones((1,), i32), (tile_e[1:] != tile_e[:-1]).astype(i32)])
    experts = jnp.arange(N_EXPERTS, dtype=i32)
    nonempty = counts > 0
    later = lax.cummin(jnp.where(nonempty, experts, N_EXPERTS), reverse=True)
    next_nonempty = jnp.concatenate([later[1:], jnp.full((1,), N_EXPERTS, i32)])
    next_nonempty = jnp.where(next_nonempty >= N_EXPERTS, -1, next_nonempty)
    expert_slot = (jnp.cumsum(nonempty.astype(i32)) - 1) % 2

    yb = _moe(tile_e, tile_first, next_nonempty[tile_e], expert_slot[tile_e], n_used.reshape(1),
              row_tok.reshape(ntile, 1, MOE_TM), h2, row_w[:, None],
              w_gate[l], b_gate[l][:, None, :], w_up[l], b_up[l][:, None, :],
              w_down[l], b_down[l][:, None, :])

    dest3 = dest.reshape(t // CB_TM, CB_TM, TOP_K).transpose(0, 2, 1)
    out = _combine(dest3, yb, x1, g2, ln2_g[l][None, :], ln2_b[l][None, :], seq)
    return out.reshape(batch, seq, D_MODEL)
```

```python
import functools
import math

import jax
import jax.numpy as jnp
from jax import lax
from jax.experimental import pallas as pl
from jax.experimental.pallas import tpu as pltpu

f32 = jnp.float32
bf16 = jnp.bfloat16
i32 = jnp.int32

D_MODEL = 1024
GM_WIDTH = 512
GM_GROUPS = 8
GM_CHUNK = 128
DIL_PAIRS = ((128, 1), (512, 4), (2048, 16))
N_DIL = 3
HEADS_PER_GROUP = 8
HEAD_DIM = 64
ATT_WIDTH = 512
ATT_BLOCK = 128
NEG_INF = -1e30
REL_BUCKETS = 32
REL_MAX_EXACT = 16
REL_MAX_DIST = 2048
N_EXPERTS = 32
TOP_K = 4
SWIGLU_LIMIT = 7.0
SWIGLU_ALPHA = 1.702
MOE_BLOCK = 128
DEPTH = 1
DN_ALPHA = (2 * DEPTH) ** 0.25
LN_EPS = 1e-5
UV_COLS = 2 * GM_WIDTH
QKV_COLS = N_DIL * 3 * ATT_WIDTH
GATE_COLS = 2 * D_MODEL
IN_COLS = UV_COLS + QKV_COLS + GATE_COLS

LANES = 128
SUBLANES = 8
VMEM_LIMIT = 56 * 1024 * 1024


def _ln(x):
    mu = jnp.mean(x, axis=-1, keepdims=True)
    xc = x - mu
    var = jnp.mean(xc * xc, axis=-1, keepdims=True)
    return xc * lax.rsqrt(var + LN_EPS)


def _params(sem, vmem=VMEM_LIMIT):
    return pltpu.CompilerParams(dimension_semantics=sem, vmem_limit_bytes=vmem)


def _adaln_kernel(c_ref, w_ref, b_ref, o_ref):
    c = c_ref[...]
    s = c * jax.nn.sigmoid(c)
    o_ref[...] = jnp.dot(s, w_ref[...], preferred_element_type=f32,
                         precision=lax.Precision.HIGHEST) + b_ref[...]


def _adaln(c8, w_ada, b_ada):
    n = w_ada.shape[1] // D_MODEL
    return pl.pallas_call(
        _adaln_kernel,
        grid=(n,),
        in_specs=[pl.BlockSpec((8, D_MODEL), lambda j: (0, 0)),
                  pl.BlockSpec((D_MODEL, D_MODEL), lambda j: (0, j)),
                  pl.BlockSpec((1, D_MODEL), lambda j: (0, j))],
        out_specs=pl.BlockSpec((8, D_MODEL), lambda j: (0, j)),
        out_shape=jax.ShapeDtypeStruct((8, w_ada.shape[1]), f32),
        compiler_params=_params(("arbitrary",)),
        name="adaln",
    )(c8, w_ada, b_ada)


IN_TM = 256
IN_CW = 512
GRP_COLS = 3 * ATT_WIDTH


def _inproj_kernel(x_ref, sc_ref, sh_ref, w_ref, p1_ref, p2_ref,
                   uv_ref, gt_ref, qkv0_ref, qkv1_ref, qkv2_ref):
    xn = _ln(x_ref[...])
    h = (xn * (1.0 + sc_ref[0]) + sh_ref[0]).astype(bf16)
    hp = [h,
          jnp.dot(p1_ref[...], h, preferred_element_type=f32).astype(bf16),
          jnp.dot(p2_ref[...], h, preferred_element_type=f32).astype(bf16)]
    for c0 in range(0, UV_COLS, IN_CW):
        acc = jnp.dot(h, w_ref[:, c0:c0 + IN_CW], preferred_element_type=f32)
        uv_ref[:, c0:c0 + IN_CW] = jax.nn.gelu(acc).astype(bf16)
    for g, (qref, (_win, dil)) in enumerate(zip((qkv0_ref, qkv1_ref, qkv2_ref), DIL_PAIRS)):
        n = IN_TM // dil
        for q0 in range(0, GRP_COLS, IN_CW):
            c0 = UV_COLS + g * GRP_COLS + q0
            acc = jnp.dot(hp[g], w_ref[:, c0:c0 + IN_CW], preferred_element_type=f32).astype(bf16)
            for rho in range(dil):
                qref[0, rho, :, q0:q0 + IN_CW] = acc[rho * n:(rho + 1) * n, :]
    for g0 in range(0, GATE_COLS, IN_CW):
        c0 = UV_COLS + QKV_COLS + g0
        acc = jnp.dot(h, w_ref[:, c0:c0 + IN_CW], preferred_element_type=f32)
        gt_ref[:, g0:g0 + IN_CW] = jax.nn.sigmoid(acc).astype(bf16)


def _inproj(x2, sc1, sh1, w_in_bf, perms, batch, seq):
    t = x2.shape[0]
    per_b = seq // IN_TM
    qkv_specs, qkv_shapes = [], []
    for _win, dil in DIL_PAIRS:
        n = IN_TM // dil
        qkv_specs.append(pl.BlockSpec((1, dil, n, GRP_COLS), lambda i: (i // per_b, 0, i % per_b, 0)))
        qkv_shapes.append(jax.ShapeDtypeStruct((batch, dil, seq // dil, GRP_COLS), bf16))
    return pl.pallas_call(
        _inproj_kernel,
        grid=(t // IN_TM,),
        in_specs=[pl.BlockSpec((IN_TM, D_MODEL), lambda i: (i, 0)),
                  pl.BlockSpec((1, 1, D_MODEL), lambda i: (i // per_b, 0, 0)),
                  pl.BlockSpec((1, 1, D_MODEL), lambda i: (i // per_b, 0, 0)),
                  pl.BlockSpec((D_MODEL, IN_COLS), lambda i: (0, 0)),
                  pl.BlockSpec((IN_TM, IN_TM), lambda i: (0, 0)),
                  pl.BlockSpec((IN_TM, IN_TM), lambda i: (0, 0))],
        out_specs=[pl.BlockSpec((IN_TM, UV_COLS), lambda i: (i, 0)),
                   pl.BlockSpec((IN_TM, GATE_COLS), lambda i: (i, 0))] + qkv_specs,
        out_shape=[jax.ShapeDtypeStruct((t, UV_COLS), bf16),
                   jax.ShapeDtypeStruct((t, GATE_COLS), bf16)] + qkv_shapes,
        compiler_params=_params(("arbitrary",)),
        name="inproj",
    )(x2, sc1, sh1, w_in_bf, perms[1], perms[2])


GM_TM = 512


def _gmlp_kernel(u_ref, v_ref, g_ref, b_ref, ws_ref, bs_ref, ya_ref):
    row = lax.broadcasted_iota(i32, (GM_CHUNK, GM_CHUNK), 0)
    col = lax.broadcasted_iota(i32, (GM_CHUNK, GM_CHUNK), 1)
    causal = col <= row
    first_half = lax.broadcasted_iota(i32, (GM_CHUNK, LANES), 1) < (GM_WIDTH // GM_GROUPS)
    ws = [jnp.where(causal, ws_ref[g], 0.0).astype(bf16) for g in range(GM_GROUPS)]
    for ch in range(GM_TM // GM_CHUNK):
        r0 = ch * GM_CHUNK
        vn = _ln(v_ref[r0:r0 + GM_CHUNK, :].astype(f32)) * g_ref[...] + b_ref[...]
        vn = vn.astype(bf16)
        for j in range(GM_WIDTH // LANES):
            slab = vn[:, j * LANES:(j + 1) * LANES]
            s_lo = jnp.dot(ws[2 * j], slab, preferred_element_type=f32)
            s_hi = jnp.dot(ws[2 * j + 1], slab, preferred_element_type=f32)
            s = jnp.where(first_half, s_lo, s_hi) + bs_ref[:, j * LANES:(j + 1) * LANES]
            u = u_ref[r0:r0 + GM_CHUNK, j * LANES:(j + 1) * LANES].astype(f32)
            ya_ref[r0:r0 + GM_CHUNK, j * LANES:(j + 1) * LANES] = (u * s).astype(bf16)


def _gmlp(uv, ln_g, ln_b, w_s, bs_full):
    t = uv.shape[0]
    return pl.pallas_call(
        _gmlp_kernel,
        grid=(t // GM_TM,),
        in_specs=[pl.BlockSpec((GM_TM, GM_WIDTH), lambda i: (i, 0)),
                  pl.BlockSpec((GM_TM, GM_WIDTH), lambda i: (i, 1)),
                  pl.BlockSpec((1, GM_WIDTH), lambda i: (0, 0)),
                  pl.BlockSpec((1, GM_WIDTH), lambda i: (0, 0)),
                  pl.BlockSpec((GM_GROUPS, GM_CHUNK, GM_CHUNK), lambda i: (0, 0, 0)),
                  pl.BlockSpec((GM_CHUNK, GM_WIDTH), lambda i: (0, 0))],
        out_specs=pl.BlockSpec((GM_TM, GM_WIDTH), lambda i: (i, 0)),
        out_shape=jax.ShapeDtypeStruct((t, GM_WIDTH), bf16),
        compiler_params=_params(("arbitrary",)),
        name="gmlp",
    )(uv, uv, ln_g, ln_b, w_s, bs_full)


def _relbias_kernel(tab_ref, bucket_ref, band_ref, out_ref):
    g = pl.program_id(0)
    bk = bucket_ref[0]
    band = band_ref[0] > 0
    for h in range(HEADS_PER_GROUP):
        acc = jnp.zeros((ATT_BLOCK, 2 * ATT_BLOCK), f32)
        for b in range(REL_BUCKETS):
            acc = jnp.where(bk == b, tab_ref[b, g * HEADS_PER_GROUP + h], acc)
        out_ref[0, h] = jnp.where(band, acc, NEG_INF)


def _relbias(rel_bias, bucket, band):
    return pl.pallas_call(
        _relbias_kernel,
        grid=(N_DIL,),
        in_specs=[pl.BlockSpec(memory_space=pltpu.SMEM),
                  pl.BlockSpec((1, ATT_BLOCK, 2 * ATT_BLOCK), lambda g: (g, 0, 0)),
                  pl.BlockSpec((1, ATT_BLOCK, 2 * ATT_BLOCK), lambda g: (g, 0, 0))],
        out_specs=pl.BlockSpec((1, HEADS_PER_GROUP, ATT_BLOCK, 2 * ATT_BLOCK),
                               lambda g: (g, 0, 0, 0)),
        out_shape=jax.ShapeDtypeStruct((N_DIL, HEADS_PER_GROUP, ATT_BLOCK, 2 * ATT_BLOCK), f32),
        compiler_params=_params(("arbitrary",)),
        name="relbias",
    )(rel_bias, bucket, band)


def _attn_kernel(q_ref, kp_ref, kc_ref, vp_ref, vc_ref, bias_ref, o_ref, lse_ref):
    first = pl.program_id(2) == 0
    lane = lax.broadcasted_iota(i32, (ATT_BLOCK, LANES), 1)
    lo_half = lane < HEAD_DIM
    nt = (((1,), (1,)), ((), ()))
    ones = jnp.ones((2 * ATT_BLOCK, LANES), bf16)
    n_slab = ATT_WIDTH // LANES
    logits, v_ext = [], []
    for j in range(n_slab):
        sl = slice(j * LANES, (j + 1) * LANES)
        q = q_ref[0, 0, :, sl] * (HEAD_DIM ** -0.5)
        k_cat = jnp.concatenate([kp_ref[0, 0, :, sl], kc_ref[0, 0, :, sl]], axis=0)
        v_cat = jnp.concatenate([vp_ref[0, 0, :, sl], vc_ref[0, 0, :, sl]], axis=0)
        v_ext.append(jnp.concatenate([v_cat, ones], axis=1))
        for hh in range(2):
            qm = jnp.where(lo_half if hh == 0 else jnp.logical_not(lo_half), q, 0.0).astype(bf16)
            logits.append(lax.dot_general(qm, k_cat, nt, preferred_element_type=f32))
    lg = jnp.concatenate(logits, axis=0) + bias_ref[0].reshape(HEADS_PER_GROUP * ATT_BLOCK, 2 * ATT_BLOCK)
    prev_cols = lax.broadcasted_iota(i32, lg.shape, 1) < ATT_BLOCK
    lg = jnp.where(jnp.logical_and(first, prev_cols), NEG_INF, lg)
    m = jnp.max(lg, axis=-1, keepdims=True)
    p = jnp.exp(lg - m).astype(bf16)
    lse_tile = jnp.zeros((ATT_BLOCK, LANES), f32)
    for j in range(n_slab):
        outs = []
        for hh in range(2):
            h = 2 * j + hh
            r = jnp.dot(p[h * ATT_BLOCK:(h + 1) * ATT_BLOCK], v_ext[j], preferred_element_type=f32)
            den = r[:, LANES:]
            outs.append(r[:, :LANES] * (1.0 / den))
            lse_h = m[h * ATT_BLOCK:(h + 1) * ATT_BLOCK] + jnp.log(den)
            lse_tile = jnp.where(lane == h, lse_h, lse_tile)
        o_ref[0, 0, :, j * LANES:(j + 1) * LANES] = jnp.where(lo_half, outs[0], outs[1]).astype(bf16)
    lse_ref[0, 0] = lse_tile


def _attn_group(qkv_g, bias, g, dil, batch, seq):
    l = seq // dil
    nb = l // ATT_BLOCK

    def spec(cb, prev):
        if prev:
            return pl.BlockSpec((1, 1, ATT_BLOCK, ATT_WIDTH),
                                lambda b, r, n: (b, r, jnp.maximum(n - 1, 0), cb))
        return pl.BlockSpec((1, 1, ATT_BLOCK, ATT_WIDTH), lambda b, r, n: (b, r, n, cb))

    return pl.pallas_call(
        _attn_kernel,
        grid=(batch, dil, nb),
        in_specs=[spec(0, False), spec(1, True), spec(1, False), spec(2, True), spec(2, False),
                  pl.BlockSpec((1, HEADS_PER_GROUP, ATT_BLOCK, 2 * ATT_BLOCK),
                               lambda b, r, n: (g, 0, 0, 0))],
        out_specs=[pl.BlockSpec((1, 1, ATT_BLOCK, ATT_WIDTH), lambda b, r, n: (b, r, n, 0)),
                   pl.BlockSpec((1, 1, ATT_BLOCK, LANES), lambda b, r, n: (b, r, n, 0))],
        out_shape=[jax.ShapeDtypeStruct((batch, dil, l, ATT_WIDTH), bf16),
                   jax.ShapeDtypeStruct((batch, dil, l, LANES), f32)],
        compiler_params=_params(("arbitrary", "arbitrary", "arbitrary")),
        name=f"attn_g{g}",
    )(qkv_g, qkv_g, qkv_g, qkv_g, qkv_g, bias)


ROW_WORDS = D_MODEL // 2
ROW_SUB = ROW_WORDS // LANES
HI_MASK = -65536


def _pack_rows(x):
    bits = lax.bitcast_convert_type(x.astype(bf16).astype(f32), i32)
    return lax.shift_right_logical(bits[:, :ROW_WORDS], 16) | (bits[:, ROW_WORDS:] & HI_MASK)


def _unpack_rows(words):
    lo = lax.bitcast_convert_type(lax.shift_left(words, 16), f32)
    hi = lax.bitcast_convert_type(words & HI_MASK, f32)
    return jnp.concatenate([lo, hi], axis=1)


def _store_packed(ref, words, n, first_row=0):
    for r in range(ROW_SUB):
        ref[pl.ds(first_row * ROW_SUB + r, n, stride=ROW_SUB), :] = words[:, r * LANES:(r + 1) * LANES]


def _load_packed(ref, first_row, n):
    return jnp.concatenate([ref[pl.ds(first_row * ROW_SUB + r, n, stride=ROW_SUB), :] for r in range(ROW_SUB)],
                           axis=1)


MIX_TM = 256
MIX_SUB = 128


def _split_bf16(x, parts):
    out = []
    for _ in range(parts):
        hi = x.astype(bf16)
        out.append(hi)
        x = x - hi.astype(f32)
    return out


def _mix_kernel(o0_ref, o1_ref, o2_ref, l0_ref, l1_ref, l2_ref, pt1_ref, pt2_ref, ex_ref,
                ya_ref, gt_ref, x_ref,
                g1_ref, sc2_ref, sh2_ref, wa_ref, wb_ref, wo_ref, ln1g_ref, ln1b_ref,
                wrh_ref, wrl_ref, br_ref, tri_ref,
                x1_ref, h2_ref, route_ref, rw_ref, cnt_ref, run_ref):
    @pl.when(pl.program_id(0) == 0)
    def _():
        run_ref[...] = jnp.zeros_like(run_ref)

    lane = lax.broadcasted_iota(i32, (MIX_SUB, LANES), 1)
    lane_f = lane.astype(f32)

    def token_rows(r0):
        rows = slice(r0, r0 + MIX_SUB)
        os_, ls_ = [o0_ref[0, 0, rows, :].astype(f32)], [l0_ref[0, 0, rows, :]]
        for o_ref, l_ref, pt_ref in ((o1_ref, l1_ref, pt1_ref), (o2_ref, l2_ref, pt2_ref)):
            pt = pt_ref[rows, :]
            os_.append(jnp.dot(pt, o_ref[0].reshape(MIX_TM, ATT_WIDTH), preferred_element_type=f32))
            parts = [jnp.dot(pt, part, preferred_element_type=f32)
                     for part in _split_bf16(l_ref[0].reshape(MIX_TM, LANES), 3)]
            ls_.append((parts[0] + parts[1]) + parts[2])
        lm = jnp.maximum(jnp.maximum(ls_[0], ls_[1]), ls_[2])
        es = [jnp.exp(lse - lm) for lse in ls_]
        inv = 1.0 / (es[0] + es[1] + es[2])
        yb = jnp.zeros((MIX_SUB, ATT_WIDTH), f32)
        for e, o in zip(es, os_):
            w_hi, w_lo = _split_bf16(e * inv, 2)
            w_full = (jnp.dot(w_hi, ex_ref[...], preferred_element_type=f32)
                      + jnp.dot(w_lo, ex_ref[...], preferred_element_type=f32))
            yb = yb + w_full * o
        a = jnp.dot(ya_ref[rows, :], wa_ref[...], preferred_element_type=f32)
        b = jnp.dot(yb.astype(bf16), wb_ref[...], preferred_element_type=f32)
        merged = gt_ref[rows, :D_MODEL].astype(f32) * a + gt_ref[rows, D_MODEL:].astype(f32) * b
        mix = jnp.dot(merged.astype(bf16), wo_ref[...], preferred_element_type=f32)
        x1 = _ln(DN_ALPHA * x_ref[rows, :] + g1_ref[0] * mix) * ln1g_ref[...] + ln1b_ref[...]
        x1_ref[rows, :] = x1
        h2 = _ln(x1) * (1.0 + sc2_ref[0]) + sh2_ref[0]
        _store_packed(h2_ref, _pack_rows(h2), MIX_SUB, r0)

        h_hi, h_lo = _split_bf16(h2, 2)
        logits = (jnp.dot(h_hi, wrh_ref[...], preferred_element_type=f32)
                  + (jnp.dot(h_hi, wrl_ref[...], preferred_element_type=f32)
                     + jnp.dot(h_lo, wrh_ref[...], preferred_element_type=f32))) + br_ref[...]
        logits = jnp.where(lane < N_EXPERTS, logits, -jnp.inf)
        vals, idxs = [], []
        for _k in range(TOP_K):
            m = jnp.max(logits, axis=-1, keepdims=True)
            idx = jnp.min(jnp.where(logits == m, lane_f, float(LANES)), axis=-1, keepdims=True).astype(i32)
            vals.append(m)
            idxs.append(idx)
            logits = jnp.where(lane == idx, -jnp.inf, logits)
        exps = [jnp.exp(v - vals[0]) for v in vals]
        den = exps[0] + exps[1] + exps[2] + exps[3]
        return idxs, [e / den for e in exps]

    subs = [token_rows(r0) for r0 in range(0, MIX_TM, MIX_SUB)]
    idxs = [jnp.concatenate([sub[0][k] for sub in subs], axis=0) for k in range(TOP_K)]
    wts = [jnp.concatenate([sub[1][k] for sub in subs], axis=0) for k in range(TOP_K)]

    lane = lax.broadcasted_iota(i32, (MIX_TM, LANES), 1)
    hits = [lane == idx for idx in idxs]
    onehot = jnp.zeros((MIX_TM, LANES), f32)
    for hit in hits:
        onehot = onehot + jnp.where(hit, 1.0, 0.0)
    prefix = jnp.dot(tri_ref[...], onehot.astype(bf16), preferred_element_type=f32) + run_ref[...]
    route = jnp.zeros((MIX_TM, LANES), i32)
    rw = jnp.zeros((MIX_TM, LANES), f32)
    for k in range(TOP_K):
        rank = jnp.sum(jnp.where(hits[k], prefix, 0.0), axis=-1, keepdims=True).astype(i32)
        route = jnp.where(lane == k, idxs[k], route)
        route = jnp.where(lane == TOP_K + k, rank, route)
        rw = jnp.where(lane == k, wts[k], rw)
    route_ref[...] = route
    rw_ref[...] = rw
    run = run_ref[...] + jnp.sum(onehot, axis=0, keepdims=True)
    run_ref[...] = run
    cnt_ref[...] = jnp.broadcast_to(run, cnt_ref.shape)


def _mix(os_, ls_, perms_t, expand, ya, gates, x2, g1, sc2, sh2, wa, wb, wo, ln1g, ln1b, wr_hi, wr_lo, br, tri,
         seq):
    t = x2.shape[0]
    per_b = seq // MIX_TM
    row = lambda w: pl.BlockSpec((MIX_TM, w), lambda i: (i, 0))
    const = lambda s: pl.BlockSpec(s, lambda i: tuple(0 for _ in s))
    modb = pl.BlockSpec((1, 1, D_MODEL), lambda i: (i // per_b, 0, 0))
    grp = lambda w: [pl.BlockSpec((1, dil, MIX_TM // dil, w), lambda i: (i // per_b, 0, i % per_b, 0))
                     for _win, dil in DIL_PAIRS]
    return pl.pallas_call(
        _mix_kernel,
        grid=(t // MIX_TM,),
        in_specs=grp(ATT_WIDTH) + grp(LANES) + [
                  const((MIX_TM, MIX_TM)), const((MIX_TM, MIX_TM)), const((LANES, ATT_WIDTH)),
                  row(GM_WIDTH), row(GATE_COLS), row(D_MODEL),
                  modb, modb, modb,
                  const((GM_WIDTH, D_MODEL)), const((ATT_WIDTH, D_MODEL)), const((D_MODEL, D_MODEL)),
                  const((1, D_MODEL)), const((1, D_MODEL)),
                  const((D_MODEL, LANES)), const((D_MODEL, LANES)), const((1, LANES)),
                  const((MIX_TM, MIX_TM))],
        out_specs=[row(D_MODEL), pl.BlockSpec((MIX_TM * ROW_SUB, LANES), lambda i: (i, 0)),
                   row(LANES), row(LANES), const((8, LANES))],
        out_shape=[jax.ShapeDtypeStruct((t, D_MODEL), f32),
                   jax.ShapeDtypeStruct((t * ROW_SUB, LANES), i32),
                   jax.ShapeDtypeStruct((t, LANES), i32),
                   jax.ShapeDtypeStruct((t, LANES), f32),
                   jax.ShapeDtypeStruct((8, LANES), f32)],
        scratch_shapes=[pltpu.VMEM((1, LANES), f32)],
        compiler_params=_params(("arbitrary",)),
        name="mix",
    )(*os_, *ls_, perms_t[1], perms_t[2], expand, ya, gates, x2, g1, sc2, sh2, wa, wb, wo,
      ln1g, ln1b, wr_hi, wr_lo, br, tri)


MOE_TM = 256


def _moe_kernel(te_ref, first_ref, nexte_ref, wslot_ref, nused_ref,
                tok_cur_ref, tok_nxt_ref, h2_hbm, rww_ref, wg_hbm, wu_hbm, wd_hbm,
                bg_ref, bu_ref, bd_ref,
                out_ref, xbuf0, xbuf1, wbuf, wgb, wub, wdb, sem_x, sem_w):
    j = pl.program_id(0)
    last = pl.num_programs(0) - 1
    xbufs = (xbuf0, xbuf1)

    def row_copy(tok, i, s):
        return pltpu.make_async_copy(h2_hbm.at[pl.ds(pl.multiple_of(tok * ROW_SUB, ROW_SUB), ROW_SUB)],
                                     xbufs[s].at[pl.ds(pl.multiple_of(i * ROW_SUB, ROW_SUB), ROW_SUB)],
                                     sem_x.at[s])

    def rows_wait(s):
        pltpu.make_async_copy(h2_hbm.at[pl.ds(0, MOE_TM * ROW_SUB)], xbufs[s], sem_x.at[s]).wait()

    def weight_copies(e, ws):
        return [pltpu.make_async_copy(w.at[e], wbuf.at[ws, k], sem_w.at[ws])
                for k, w in enumerate((wg_hbm, wu_hbm, wd_hbm))]

    def gather_loop(tok_ref, s):
        def body(i, c):
            for p in range(2):
                row_copy(tok_ref[0, 0, 2 * i + p], 2 * i + p, s).start(priority=p)
            return c
        lax.fori_loop(0, MOE_TM // 2, body, 0, unroll=4)

    @pl.when(j == 0)
    def _():
        gather_loop(tok_cur_ref, 0)
        for cp in weight_copies(te_ref[0], wslot_ref[0]):
            cp.start(priority=1)

    def load_weights():
        ws = wslot_ref[j]
        for cp in weight_copies(te_ref[j], ws):
            cp.wait()
        wgb[...] = wbuf[ws, 0].astype(bf16)
        wub[...] = wbuf[ws, 1].astype(bf16)
        wdb[...] = wbuf[ws, 2].astype(bf16)
        ne = nexte_ref[j]

        @pl.when(ne >= 0)
        def _():
            for cp in weight_copies(ne, 1 - ws):
                cp.start(priority=1)

    def expert_mlp(s):
        gather_loop(tok_nxt_ref, 1 - s)
        xb =_unpack_rows(_load_packed(xbufs[s], 0, MOE_TM)).astype(bf16)
        g = jnp.dot(xb, wgb[...], preferred_element_type=f32) + bg_ref[0]
        u = jnp.dot(xb, wub[...], preferred_element_type=f32) + bu_ref[0]
        g = jnp.minimum(g, SWIGLU_LIMIT)
        u = jnp.clip(u, -SWIGLU_LIMIT, SWIGLU_LIMIT)
        act = (u + 1.0) * (g * jax.nn.sigmoid(SWIGLU_ALPHA * g))
        y = (jnp.dot(act.astype(bf16), wdb[...], preferred_element_type=f32) + bd_ref[0]) * rww_ref[...]
        _store_packed(out_ref, _pack_rows(y), MOE_TM)

    def idle_tile(s):
        gather_loop(tok_nxt_ref, s)
        out_ref[...] = jnp.zeros_like(out_ref)

    used = j < nused_ref[0]
    for s in range(2):
        @pl.when(j % 2 == s)
        def _(s=s):
            rows_wait(s)
            pl.when(first_ref[j] == 1)(load_weights)
            pl.when(used)(functools.partial(expert_mlp, s))
            pl.when(jnp.logical_not(used))(functools.partial(idle_tile, 1 - s))
            pl.when(j == last)(functools.partial(rows_wait, 1 - s))


def _moe(tile_e, tile_first, next_e, wslot, n_used, row_tok3, h2p, row_w,
         w_gate, b_gate, w_up, b_up, w_down, b_down):
    ntile = tile_e.shape[0]
    bspec = pl.BlockSpec((1, 1, D_MODEL), lambda j, te, *_: (te[j], 0, 0))
    hbm = pl.BlockSpec(memory_space=pl.ANY)
    grid_spec = pltpu.PrefetchScalarGridSpec(
        num_scalar_prefetch=5,
        grid=(ntile,),
        in_specs=[pl.BlockSpec((1, 1, MOE_TM), lambda j, *_: (j, 0, 0), memory_space=pltpu.SMEM),
                  pl.BlockSpec((1, 1, MOE_TM), lambda j, *_: (jnp.minimum(j + 1, ntile - 1), 0, 0),
                               memory_space=pltpu.SMEM),
                  hbm,
                  pl.BlockSpec((MOE_TM, 1), lambda j, *_: (j, 0)),
                  hbm, hbm, hbm, bspec, bspec, bspec],
        out_specs=pl.BlockSpec((MOE_TM * ROW_SUB, LANES), lambda j, *_: (j, 0)),
        scratch_shapes=[pltpu.VMEM((MOE_TM * ROW_SUB, LANES), i32),
                        pltpu.VMEM((MOE_TM * ROW_SUB, LANES), i32),
                        pltpu.VMEM((2, 3, D_MODEL, D_MODEL), f32),
                        pltpu.VMEM((D_MODEL, D_MODEL), bf16),
                        pltpu.VMEM((D_MODEL, D_MODEL), bf16),
                        pltpu.VMEM((D_MODEL, D_MODEL), bf16),
                        pltpu.SemaphoreType.DMA((2,)),
                        pltpu.SemaphoreType.DMA((2,))],
    )
    return pl.pallas_call(
        _moe_kernel,
        grid_spec=grid_spec,
        out_shape=jax.ShapeDtypeStruct((ntile * MOE_TM * ROW_SUB, LANES), i32),
        compiler_params=_params(("arbitrary",)),
        name="moe",
    )(tile_e, tile_first, next_e, wslot, n_used, row_tok3, row_tok3, h2p, row_w,
      w_gate, w_up, w_down, b_gate, b_up, b_down)


CB_TM = 128


def _combine_kernel(dcur_ref, dnxt_ref, yb_hbm, x1_ref, g2_ref, lng_ref, lnb_ref, out_ref,
                    ybuf0, ybuf1, sem):
    i = pl.program_id(0)
    last = pl.num_programs(0) - 1
    ybufs = (ybuf0, ybuf1)

    def row_copy(d, k, r, s):
        return pltpu.make_async_copy(
            yb_hbm.at[pl.ds(pl.multiple_of(d * ROW_SUB, ROW_SUB), ROW_SUB)],
            ybufs[s].at[pl.ds(pl.multiple_of((k * CB_TM + r) * ROW_SUB, ROW_SUB), ROW_SUB)],
            sem.at[s])

    @pl.when(i == 0)
    def _():
        for k in range(TOP_K):
            def body(r, c, k=k):
                row_copy(dcur_ref[0, k, r], k, r, 0).start()
                return c
            lax.fori_loop(0, CB_TM, body, 0, unroll=8)

    for s in range(2):
        @pl.when(i % 2 == s)
        def _(s=s):
            pltpu.make_async_copy(yb_hbm.at[pl.ds(0, TOP_K * CB_TM * ROW_SUB)], ybufs[s], sem.at[s]).wait()

            @pl.when(i < last)
            def _():
                for k in range(TOP_K):
                    for r in range(CB_TM):
                        row_copy(dnxt_ref[0, k, r], k, r, 1 - s).start(priority=r % 2)

            parts = [_unpack_rows(_load_packed(ybufs[s], k * CB_TM, CB_TM)) for k in range(TOP_K)]
            y = (parts[0] + parts[1]) + (parts[2] + parts[3])
            out_ref[...] = _ln(DN_ALPHA * x1_ref[...] + g2_ref[0] * y) * lng_ref[...] + lnb_ref[...]


def _combine(dest3, yb, x1, g2, ln2g, ln2b, seq):
    t = x1.shape[0]
    nb = t // CB_TM
    per_b = seq // CB_TM
    return pl.pallas_call(
        _combine_kernel,
        grid=(nb,),
        in_specs=[pl.BlockSpec((1, TOP_K, CB_TM), lambda i: (i, 0, 0), memory_space=pltpu.SMEM),
                  pl.BlockSpec((1, TOP_K, CB_TM), lambda i: (jnp.minimum(i + 1, nb - 1), 0, 0),
                               memory_space=pltpu.SMEM),
                  pl.BlockSpec(memory_space=pl.ANY),
                  pl.BlockSpec((CB_TM, D_MODEL), lambda i: (i, 0)),
                  pl.BlockSpec((1, 1, D_MODEL), lambda i: (i // per_b, 0, 0)),
                  pl.BlockSpec((1, D_MODEL), lambda i: (0, 0)),
                  pl.BlockSpec((1, D_MODEL), lambda i: (0, 0))],
        out_specs=pl.BlockSpec((CB_TM, D_MODEL), lambda i: (i, 0)),
        out_shape=jax.ShapeDtypeStruct((t, D_MODEL), f32),
        scratch_shapes=[pltpu.VMEM((TOP_K * CB_TM * ROW_SUB, LANES), i32),
                        pltpu.VMEM((TOP_K * CB_TM * ROW_SUB, LANES), i32),
                        pltpu.SemaphoreType.DMA((2,))],
        compiler_params=_params(("arbitrary",)),
        name="combine",
    )(dest3, dest3, yb, x1, g2, ln2g, ln2b)


def _t5_bucket(dist):
    d = dist.astype(f32)
    large = REL_MAX_EXACT + jnp.log(jnp.maximum(d, float(REL_MAX_EXACT)) / REL_MAX_EXACT) / math.log(
        REL_MAX_DIST / REL_MAX_EXACT) * (REL_BUCKETS - REL_MAX_EXACT)
    large = jnp.minimum(large.astype(i32), REL_BUCKETS - 1)
    return jnp.where(dist < REL_MAX_EXACT, dist, large)


def _bias_indices():
    qi = jnp.arange(ATT_BLOCK)[:, None]
    ki = jnp.arange(2 * ATT_BLOCK)[None, :]
    didx = qi + ATT_BLOCK - ki
    buckets, bands = [], []
    for win, dil in DIL_PAIRS:
        buckets.append(_t5_bucket(jnp.clip(didx, 0, None) * dil))
        bands.append(((didx >= 0) & (didx <= win // dil)).astype(i32))
    return jnp.stack(buckets).astype(i32), jnp.stack(bands)


def _residue_perm(tm, dil):
    n = tm // dil
    dst = jnp.arange(tm)
    src = (dst % n) * dil + dst // n
    return (src[:, None] == jnp.arange(tm)[None, :]).astype(bf16)


def kernel(x, c, w_ada, b_ada, w_in, gm_ln_g, gm_ln_b, gm_w_s, gm_b_s, w_branch_a, w_branch_b, w_out,
           rel_bias, ln1_g, ln1_b, w_router, b_router, w_gate, b_gate, w_up, b_up, w_down, b_down,
           ln2_g, ln2_b):
    batch, seq, _ = x.shape
    t = batch * seq
    l = 0
    x2 = x.reshape(t, D_MODEL)

    c8 = jnp.pad(c, ((0, 8 - batch), (0, 0)))
    mod = _adaln(c8, w_ada[l], b_ada[l][None, :])[:batch]
    sh1, sc1, g1, sh2, sc2, g2 = [m[:, None, :] for m in jnp.split(mod, 6, axis=-1)]

    perms = [_residue_perm(IN_TM, dil) for _win, dil in DIL_PAIRS]
    uv, gates, *qkvs = _inproj(x2, sc1, sh1, w_in[l].astype(bf16), perms, batch, seq)

    bs_full = jnp.repeat(gm_b_s[l].T, GM_WIDTH // GM_GROUPS, axis=1)
    ya = _gmlp(uv, gm_ln_g[l][None, :], gm_ln_b[l][None, :], gm_w_s[l], bs_full)

    bucket, band = _bias_indices()
    bias = _relbias(rel_bias, bucket, band)
    os_, ls_ = [], []
    for g, (_win, dil) in enumerate(DIL_PAIRS):
        o, lse = _attn_group(qkvs[g], bias, g, dil, batch, seq)
        os_.append(o)
        ls_.append(lse)

    wr = jnp.pad(w_router[l], ((0, 0), (0, LANES - N_EXPERTS)))
    wr_hi = wr.astype(bf16)
    wr_lo = (wr - wr_hi.astype(f32)).astype(bf16)
    br = jnp.pad(b_router[l], (0, LANES - N_EXPERTS))[None, :]
    tri = (jnp.arange(MIX_TM)[None, :] < jnp.arange(MIX_TM)[:, None]).astype(bf16)
    perms_t = [_residue_perm(MIX_TM, dil).T for _win, dil in DIL_PAIRS]
    expand = (jnp.arange(LANES)[:, None] == jnp.arange(ATT_WIDTH)[None, :] // HEAD_DIM).astype(bf16)
    x1, h2, route, rw, cnt = _mix(
        os_, ls_, perms_t, expand, ya, gates, x2, g1, sc2, sh2,
        w_branch_a[l].astype(bf16), w_branch_b[l].astype(bf16), w_out[l].astype(bf16),
        ln1_g[l][None, :], ln1_b[l][None, :], wr_hi, wr_lo, br, tri, seq)

    top_e = route[:, :TOP_K]
    rank = route[:, TOP_K:2 * TOP_K]
    top_w = rw[:, :TOP_K]
    counts = cnt[0, :N_EXPERTS].astype(i32)
    pcounts = (counts + MOE_TM - 1) // MOE_TM * MOE_TM
    pends = jnp.cumsum(pcounts)
    pstarts = pends - pcounts
    dest = pstarts[top_e] + rank
    a_total = t * TOP_K
    ntile = a_total // MOE_TM + N_EXPERTS
    r_total = ntile * MOE_TM
    inv = jnp.full((r_total,), -1, i32).at[dest.reshape(-1)].set(
        jnp.arange(a_total, dtype=i32), unique_indices=True, mode='promise_in_bounds')
    valid = inv >= 0
    safe = jnp.maximum(inv, 0)
    row_tok = jnp.where(valid, safe // TOP_K, 0)
    row_w = jnp.where(valid, top_w.reshape(-1)[safe], 0.0)
    n_used = pends[-1] // MOE_TM
    tile_idx = jnp.minimum(jnp.arange(ntile, dtype=i32), n_used - 1)
    tile_e = jnp.sum((pends[None, :] <= (tile_idx * MOE_TM)[:, None]).astype(i32), axis=1)
    tile_first = jnp.concatenate([jnp.ones((1,), i32), (tile_e[1:] != tile_e[:-1]).astype(i32)])
    experts = jnp.arange(N_EXPERTS, dtype=i32)
    nonempty = counts > 0
    later = lax.cummin(jnp.where(nonempty, experts, N_EXPERTS), reverse=True)
    next_nonempty = jnp.concatenate([later[1:], jnp.full((1,), N_EXPERTS, i32)])
    next_nonempty = jnp.where(next_nonempty >= N_EXPERTS, -1, next_nonempty)
    expert_slot = (jnp.cumsum(nonempty.astype(i32)) - 1) % 2

    yb = _moe(tile_e, tile_first, next_nonempty[tile_e], expert_slot[tile_e], n_used.reshape(1),
              row_tok.reshape(ntile, 1, MOE_TM), h2, row_w[:, None],
              w_gate[l], b_gate[l][:, None, :], w_up[l], b_up[l][:, None, :],
              w_down[l], b_down[l][:, None, :])

    dest3 = dest.reshape(t // CB_TM, CB_TM, TOP_K).transpose(0, 2, 1)
    out = _combine(dest3, yb, x1, g2, ln2_g[l][None, :], ln2_b[l][None, :], seq)
    return out.reshape(batch, seq, D_MODEL)
```

```python
import functools
import math

import jax
import jax.numpy as jnp
from jax import lax
from jax.experimental import pallas as pl
from jax.experimental.pallas import tpu as pltpu

f32 = jnp.float32
bf16 = jnp.bfloat16
i32 = jnp.int32

D_MODEL = 1024
GM_WIDTH = 512
GM_GROUPS = 8
GM_CHUNK = 128
DIL_PAIRS = ((128, 1), (512, 4), (2048, 16))
N_DIL = 3
HEADS_PER_GROUP = 8
HEAD_DIM = 64
ATT_WIDTH = 512
ATT_BLOCK = 128
NEG_INF = -1e30
REL_BUCKETS = 32
REL_MAX_EXACT = 16
REL_MAX_DIST = 2048
N_EXPERTS = 32
TOP_K = 4
SWIGLU_LIMIT = 7.0
SWIGLU_ALPHA = 1.702
MOE_BLOCK = 128
DEPTH = 1
DN_ALPHA = (2 * DEPTH) ** 0.25
LN_EPS = 1e-5
UV_COLS = 2 * GM_WIDTH
QKV_COLS = N_DIL * 3 * ATT_WIDTH
GATE_COLS = 2 * D_MODEL
IN_COLS = UV_COLS + QKV_COLS + GATE_COLS

LANES = 128
SUBLANES = 8
VMEM_LIMIT = 56 * 1024 * 1024


def _ln(x):
    mu = jnp.mean(x, axis=-1, keepdims=True)
    xc = x - mu
    var = jnp.mean(xc * xc, axis=-1, keepdims=True)
    return xc * lax.rsqrt(var + LN_EPS)


def _params(sem, vmem=VMEM_LIMIT):
    return pltpu.CompilerParams(dimension_semantics=sem, vmem_limit_bytes=vmem)


def _adaln_kernel(c_ref, w_ref, b_ref, o_ref):
    c = c_ref[...]
    s = c * jax.nn.sigmoid(c)
    o_ref[...] = jnp.dot(s, w_ref[...], preferred_element_type=f32,
                         precision=lax.Precision.HIGHEST) + b_ref[...]


def _adaln(c8, w_ada, b_ada):
    n = w_ada.shape[1] // D_MODEL
    return pl.pallas_call(
        _adaln_kernel,
        grid=(n,),
        in_specs=[pl.BlockSpec((8, D_MODEL), lambda j: (0, 0)),
                  pl.BlockSpec((D_MODEL, D_MODEL), lambda j: (0, j)),
                  pl.BlockSpec((1, D_MODEL), lambda j: (0, j))],
        out_specs=pl.BlockSpec((8, D_MODEL), lambda j: (0, j)),
        out_shape=jax.ShapeDtypeStruct((8, w_ada.shape[1]), f32),
        compiler_params=_params(("arbitrary",)),
        name="adaln",
    )(c8, w_ada, b_ada)


IN_TM = 256
IN_CW = 512
GRP_COLS = 3 * ATT_WIDTH


def _inproj_kernel(x_ref, sc_ref, sh_ref, w_ref, p1_ref, p2_ref,
                   uv_ref, gt_ref, qkv0_ref, qkv1_ref, qkv2_ref):
    xn = _ln(x_ref[...])
    h = (xn * (1.0 + sc_ref[0]) + sh_ref[0]).astype(bf16)
    hp = [h,
          jnp.dot(p1_ref[...], h, preferred_element_type=f32).astype(bf16),
          jnp.dot(p2_ref[...], h, preferred_element_type=f32).astype(bf16)]
    for c0 in range(0, UV_COLS, IN_CW):
        acc = jnp.dot(h, w_ref[:, c0:c0 + IN_CW], preferred_element_type=f32)
        uv_ref[:, c0:c0 + IN_CW] = jax.nn.gelu(acc).astype(bf16)
    for g, (qref, (_win, dil)) in enumerate(zip((qkv0_ref, qkv1_ref, qkv2_ref), DIL_PAIRS)):
        n = IN_TM // dil
        for q0 in range(0, GRP_COLS, IN_CW):
            c0 = UV_COLS + g * GRP_COLS + q0
            acc = jnp.dot(hp[g], w_ref[:, c0:c0 + IN_CW], preferred_element_type=f32).astype(bf16)
            for rho in range(dil):
                qref[0, rho, :, q0:q0 + IN_CW] = acc[rho * n:(rho + 1) * n, :]
    for g0 in range(0, GATE_COLS, IN_CW):
        c0 = UV_COLS + QKV_COLS + g0
        acc = jnp.dot(h, w_ref[:, c0:c0 + IN_CW], preferred_element_type=f32)
        gt_ref[:, g0:g0 + IN_CW] = jax.nn.sigmoid(acc).astype(bf16)


def _inproj(x2, sc1, sh1, w_in_bf, perms, batch, seq):
    t = x2.shape[0]
    per_b = seq // IN_TM
    qkv_specs, qkv_shapes = [], []
    for _win, dil in DIL_PAIRS:
        n = IN_TM // dil
        qkv_specs.append(pl.BlockSpec((1, dil, n, GRP_COLS), lambda i: (i // per_b, 0, i % per_b, 0)))
        qkv_shapes.append(jax.ShapeDtypeStruct((batch, dil, seq // dil, GRP_COLS), bf16))
    return pl.pallas_call(
        _inproj_kernel,
        grid=(t // IN_TM,),
        in_specs=[pl.BlockSpec((IN_TM, D_MODEL), lambda i: (i, 0)),
                  pl.BlockSpec((1, 1, D_MODEL), lambda i: (i // per_b, 0, 0)),
                  pl.BlockSpec((1, 1, D_MODEL), lambda i: (i // per_b, 0, 0)),
                  pl.BlockSpec((D_MODEL, IN_COLS), lambda i: (0, 0)),
                  pl.BlockSpec((IN_TM, IN_TM), lambda i: (0, 0)),
                  pl.BlockSpec((IN_TM, IN_TM), lambda i: (0, 0))],
        out_specs=[pl.BlockSpec((IN_TM, UV_COLS), lambda i: (i, 0)),
                   pl.BlockSpec((IN_TM, GATE_COLS), lambda i: (i, 0))] + qkv_specs,
        out_shape=[jax.ShapeDtypeStruct((t, UV_COLS), bf16),
                   jax.ShapeDtypeStruct((t, GATE_COLS), bf16)] + qkv_shapes,
        compiler_params=_params(("arbitrary",)),
        name="inproj",
    )(x2, sc1, sh1, w_in_bf, perms[1], perms[2])


GM_TM = 512


def _gmlp_kernel(u_ref, v_ref, g_ref, b_ref, ws_ref, bs_ref, ya_ref):
    row = lax.broadcasted_iota(i32, (GM_CHUNK, GM_CHUNK), 0)
    col = lax.broadcasted_iota(i32, (GM_CHUNK, GM_CHUNK), 1)
    causal = col <= row
    first_half = lax.broadcasted_iota(i32, (GM_CHUNK, LANES), 1) < (GM_WIDTH // GM_GROUPS)
    ws = [jnp.where(causal, ws_ref[g], 0.0).astype(bf16) for g in range(GM_GROUPS)]
    for ch in range(GM_TM // GM_CHUNK):
        r0 = ch * GM_CHUNK
        vn = _ln(v_ref[r0:r0 + GM_CHUNK, :].astype(f32)) * g_ref[...] + b_ref[...]
        vn = vn.astype(bf16)
        for j in range(GM_WIDTH // LANES):
            slab = vn[:, j * LANES:(j + 1) * LANES]
            s_lo = jnp.dot(ws[2 * j], slab, preferred_element_type=f32)
            s_hi = jnp.dot(ws[2 * j + 1], slab, preferred_element_type=f32)
            s = jnp.where(first_half, s_lo, s_hi) + bs_ref[:, j * LANES:(j + 1) * LANES]
            u = u_ref[r0:r0 + GM_CHUNK, j * LANES:(j + 1) * LANES].astype(f32)
            ya_ref[r0:r0 + GM_CHUNK, j * LANES:(j + 1) * LANES] = (u * s).astype(bf16)


def _gmlp(uv, ln_g, ln_b, w_s, bs_full):
    t = uv.shape[0]
    return pl.pallas_call(
        _gmlp_kernel,
        grid=(t // GM_TM,),
        in_specs=[pl.BlockSpec((GM_TM, GM_WIDTH), lambda i: (i, 0)),
                  pl.BlockSpec((GM_TM, GM_WIDTH), lambda i: (i, 1)),
                  pl.BlockSpec((1, GM_WIDTH), lambda i: (0, 0)),
                  pl.BlockSpec((1, GM_WIDTH), lambda i: (0, 0)),
                  pl.BlockSpec((GM_GROUPS, GM_CHUNK, GM_CHUNK), lambda i: (0, 0, 0)),
                  pl.BlockSpec((GM_CHUNK, GM_WIDTH), lambda i: (0, 0))],
        out_specs=pl.BlockSpec((GM_TM, GM_WIDTH), lambda i: (i, 0)),
        out_shape=jax.ShapeDtypeStruct((t, GM_WIDTH), bf16),
        compiler_params=_params(("arbitrary",)),
        name="gmlp",
    )(uv, uv, ln_g, ln_b, w_s, bs_full)


def _relbias_kernel(tab_ref, bucket_ref, band_ref, out_ref):
    g = pl.program_id(0)
    bk = bucket_ref[0]
    band = band_ref[0] > 0
    for h in range(HEADS_PER_GROUP):
        acc = jnp.zeros((ATT_BLOCK, 2 * ATT_BLOCK), f32)
        for b in range(REL_BUCKETS):
            acc = jnp.where(bk == b, tab_ref[b, g * HEADS_PER_GROUP + h], acc)
        out_ref[0, h] = jnp.where(band, acc, NEG_INF)


def _relbias(rel_bias, bucket, band):
    return pl.pallas_call(
        _relbias_kernel,
        grid=(N_DIL,),
        in_specs=[pl.BlockSpec(memory_space=pltpu.SMEM),
                  pl.BlockSpec((1, ATT_BLOCK, 2 * ATT_BLOCK), lambda g: (g, 0, 0)),
                  pl.BlockSpec((1, ATT_BLOCK, 2 * ATT_BLOCK), lambda g: (g, 0, 0))],
        out_specs=pl.BlockSpec((1, HEADS_PER_GROUP, ATT_BLOCK, 2 * ATT_BLOCK),
                               lambda g: (g, 0, 0, 0)),
        out_shape=jax.ShapeDtypeStruct((N_DIL, HEADS_PER_GROUP, ATT_BLOCK, 2 * ATT_BLOCK), f32),
        compiler_params=_params(("arbitrary",)),
        name="relbias",
    )(rel_bias, bucket, band)


def _attn_kernel(q_ref, kp_ref, kc_ref, vp_ref, vc_ref, bias_ref, o_ref, lse_ref):
    first = pl.program_id(2) == 0
    lane = lax.broadcasted_iota(i32, (ATT_BLOCK, LANES), 1)
    lo_half = lane < HEAD_DIM
    nt = (((1,), (1,)), ((), ()))
    ones = jnp.ones((2 * ATT_BLOCK, LANES), bf16)
    n_slab = ATT_WIDTH // LANES
    logits, v_ext = [], []
    for j in range(n_slab):
        sl = slice(j * LANES, (j + 1) * LANES)
        q = q_ref[0, 0, :, sl] * (HEAD_DIM ** -0.5)
        k_cat = jnp.concatenate([kp_ref[0, 0, :, sl], kc_ref[0, 0, :, sl]], axis=0)
        v_cat = jnp.concatenate([vp_ref[0, 0, :, sl], vc_ref[0, 0, :, sl]], axis=0)
        v_ext.append(jnp.concatenate([v_cat, ones], axis=1))
        for hh in range(2):
            qm = jnp.where(lo_half if hh == 0 else jnp.logical_not(lo_half), q, 0.0).astype(bf16)
            logits.append(lax.dot_general(qm, k_cat, nt, preferred_element_type=f32))
    lg = jnp.concatenate(logits, axis=0) + bias_ref[0].reshape(HEADS_PER_GROUP * ATT_BLOCK, 2 * ATT_BLOCK)
    prev_cols = lax.broadcasted_iota(i32, lg.shape, 1) < ATT_BLOCK
    lg = jnp.where(jnp.logical_and(first, prev_cols), NEG_INF, lg)
    m = jnp.max(lg, axis=-1, keepdims=True)
    p = jnp.exp(lg - m).astype(bf16)
    lse_tile = jnp.zeros((ATT_BLOCK, LANES), f32)
    for j in range(n_slab):
        outs = []
        for hh in range(2):
            h = 2 * j + hh
            r = jnp.dot(p[h * ATT_BLOCK:(h + 1) * ATT_BLOCK], v_ext[j], preferred_element_type=f32)
            den = r[:, LANES:]
            outs.append(r[:, :LANES] * (1.0 / den))
            lse_h = m[h * ATT_BLOCK:(h + 1) * ATT_BLOCK] + jnp.log(den)
            lse_tile = jnp.where(lane == h, lse_h, lse_tile)
        o_ref[0, 0, :, j * LANES:(j + 1) * LANES] = jnp.where(lo_half, outs[0], outs[1]).astype(bf16)
    lse_ref[0, 0] = lse_tile


def _attn_group(qkv_g, bias, g, dil, batch, seq):
    l = seq // dil
    nb = l // ATT_BLOCK

    def spec(cb, prev):
        if prev:
            return pl.BlockSpec((1, 1, ATT_BLOCK, ATT_WIDTH),
                                lambda b, r, n: (b, r, jnp.maximum(n - 1, 0), cb))
        return pl.BlockSpec((1, 1, ATT_BLOCK, ATT_WIDTH), lambda b, r, n: (b, r, n, cb))

    return pl.pallas_call(
        _attn_kernel,
        grid=(batch, dil, nb),
        in_specs=[spec(0, False), spec(1, True), spec(1, False), spec(2, True), spec(2, False),
                  pl.BlockSpec((1, HEADS_PER_GROUP, ATT_BLOCK, 2 * ATT_BLOCK),
                               lambda b, r, n: (g, 0, 0, 0))],
        out_specs=[pl.BlockSpec((1, 1, ATT_BLOCK, ATT_WIDTH), lambda b, r, n: (b, r, n, 0)),
                   pl.BlockSpec((1, 1, ATT_BLOCK, LANES), lambda b, r, n: (b, r, n, 0))],
        out_shape=[jax.ShapeDtypeStruct((batch, dil, l, ATT_WIDTH), bf16),
                   jax.ShapeDtypeStruct((batch, dil, l, LANES), f32)],
        compiler_params=_params(("arbitrary", "arbitrary", "arbitrary")),
        name=f"attn_g{g}",
    )(qkv_g, qkv_g, qkv_g, qkv_g, qkv_g, bias)


ROW_WORDS = D_MODEL // 2
ROW_SUB = ROW_WORDS // LANES
HI_MASK = -65536


def _pack_rows(x):
    bits = lax.bitcast_convert_type(x.astype(bf16).astype(f32), i32)
    return lax.shift_right_logical(bits[:, :ROW_WORDS], 16) | (bits[:, ROW_WORDS:] & HI_MASK)


def _unpack_rows(words):
    lo = lax.bitcast_convert_type(lax.shift_left(words, 16), f32)
    hi = lax.bitcast_convert_type(words & HI_MASK, f32)
    return jnp.concatenate([lo, hi], axis=1)


def _store_packed(ref, words, n, first_row=0):
    for r in range(ROW_SUB):
        ref[pl.ds(first_row * ROW_SUB + r, n, stride=ROW_SUB), :] = words[:, r * LANES:(r + 1) * LANES]


def _load_packed(ref, first_row, n):
    return jnp.concatenate([ref[pl.ds(first_row * ROW_SUB + r, n, stride=ROW_SUB), :] for r in range(ROW_SUB)],
                           axis=1)


MIX_TM = 256
MIX_SUB = 128


def _split_bf16(x, parts):
    out = []
    for _ in range(parts):
        hi = x.astype(bf16)
        out.append(hi)
        x = x - hi.astype(f32)
    return out


def _mix_kernel(o0_ref, o1_ref, o2_ref, l0_ref, l1_ref, l2_ref, pt1_ref, pt2_ref, ex_ref,
                ya_ref, gt_ref, x_ref,
                g1_ref, sc2_ref, sh2_ref, wa_ref, wb_ref, wo_ref, ln1g_ref, ln1b_ref,
                wrh_ref, wrl_ref, br_ref, tri_ref,
                x1_ref, h2_ref, route_ref, rw_ref, cnt_ref, run_ref):
    @pl.when(pl.program_id(0) == 0)
    def _():
        run_ref[...] = jnp.zeros_like(run_ref)

    lane = lax.broadcasted_iota(i32, (MIX_SUB, LANES), 1)
    lane_f = lane.astype(f32)

    def token_rows(r0):
        rows = slice(r0, r0 + MIX_SUB)
        os_, ls_ = [o0_ref[0, 0, rows, :].astype(f32)], [l0_ref[0, 0, rows, :]]
        for o_ref, l_ref, pt_ref in ((o1_ref, l1_ref, pt1_ref), (o2_ref, l2_ref, pt2_ref)):
            pt = pt_ref[rows, :]
            os_.append(jnp.dot(pt, o_ref[0].reshape(MIX_TM, ATT_WIDTH), preferred_element_type=f32))
            parts = [jnp.dot(pt, part, preferred_element_type=f32)
                     for part in _split_bf16(l_ref[0].reshape(MIX_TM, LANES), 3)]
            ls_.append((parts[0] + parts[1]) + parts[2])
        lm = jnp.maximum(jnp.maximum(ls_[0], ls_[1]), ls_[2])
        es = [jnp.exp(lse - lm) for lse in ls_]
        inv = 1.0 / (es[0] + es[1] + es[2])
        yb = jnp.zeros((MIX_SUB, ATT_WIDTH), f32)
        for e, o in zip(es, os_):
            w_hi, w_lo = _split_bf16(e * inv, 2)
            w_full = (jnp.dot(w_hi, ex_ref[...], preferred_element_type=f32)
                      + jnp.dot(w_lo, ex_ref[...], preferred_element_type=f32))
            yb = yb + w_full * o
        a = jnp.dot(ya_ref[rows, :], wa_ref[...], preferred_element_type=f32)
        b = jnp.dot(yb.astype(bf16), wb_ref[...], preferred_element_type=f32)
        merged = gt_ref[rows, :D_MODEL].astype(f32) * a + gt_ref[rows, D_MODEL:].astype(f32) * b
        mix = jnp.dot(merged.astype(bf16), wo_ref[...], preferred_element_type=f32)
        x1 = _ln(DN_ALPHA * x_ref[rows, :] + g1_ref[0] * mix) * ln1g_ref[...] + ln1b_ref[...]
        x1_ref[rows, :] = x1
        h2 = _ln(x1) * (1.0 + sc2_ref[0]) + sh2_ref[0]
        _store_packed(h2_ref, _pack_rows(h2), MIX_SUB, r0)

        h_hi, h_lo = _split_bf16(h2, 2)
        logits = (jnp.dot(h_hi, wrh_ref[...], preferred_element_type=f32)
                  + (jnp.dot(h_hi, wrl_ref[...], preferred_element_type=f32)
                     + jnp.dot(h_lo, wrh_ref[...], preferred_element_type=f32))) + br_ref[...]
        logits = jnp.where(lane < N_EXPERTS, logits, -jnp.inf)
        vals, idxs = [], []
        for _k in range(TOP_K):
            m = jnp.max(logits, axis=-1, keepdims=True)
            idx = jnp.min(jnp.where(logits == m, lane_f, float(LANES)), axis=-1, keepdims=True).astype(i32)
            vals.append(m)
            idxs.append(idx)
            logits = jnp.where(lane == idx, -jnp.inf, logits)
        exps = [jnp.exp(v - vals[0]) for v in vals]
        den = exps[0] + exps[1] + exps[2] + exps[3]
        return idxs, [e / den for e in exps]

    subs = [token_rows(r0) for r0 in range(0, MIX_TM, MIX_SUB)]
    idxs = [jnp.concatenate([sub[0][k] for sub in subs], axis=0) for k in range(TOP_K)]
    wts = [jnp.concatenate([sub[1][k] for sub in subs], axis=0) for k in range(TOP_K)]

    lane = lax.broadcasted_iota(i32, (MIX_TM, LANES), 1)
    hits = [lane == idx for idx in idxs]
    onehot = jnp.zeros((MIX_TM, LANES), f32)
    for hit in hits:
        onehot = onehot + jnp.where(hit, 1.0, 0.0)
    prefix = jnp.dot(tri_ref[...], onehot.astype(bf16), preferred_element_type=f32) + run_ref[...]
    route = jnp.zeros((MIX_TM, LANES), i32)
    rw = jnp.zeros((MIX_TM, LANES), f32)
    for k in range(TOP_K):
        rank = jnp.sum(jnp.where(hits[k], prefix, 0.0), axis=-1, keepdims=True).astype(i32)
        route = jnp.where(lane == k, idxs[k], route)
        route = jnp.where(lane == TOP_K + k, rank, route)
        rw = jnp.where(lane == k, wts[k], rw)
    route_ref[...] = route
    rw_ref[...] = rw
    run = run_ref[...] + jnp.sum(onehot, axis=0, keepdims=True)
    run_ref[...] = run
    cnt_ref[...] = jnp.broadcast_to(run, cnt_ref.shape)


def _mix(os_, ls_, perms_t, expand, ya, gates, x2, g1, sc2, sh2, wa, wb, wo, ln1g, ln1b, wr_hi, wr_lo, br, tri,
         seq):
    t = x2.shape[0]
    per_b = seq // MIX_TM
    row = lambda w: pl.BlockSpec((MIX_TM, w), lambda i: (i, 0))
    const = lambda s: pl.BlockSpec(s, lambda i: tuple(0 for _ in s))
    modb = pl.BlockSpec((1, 1, D_MODEL), lambda i: (i // per_b, 0, 0))
    grp = lambda w: [pl.BlockSpec((1, dil, MIX_TM // dil, w), lambda i: (i // per_b, 0, i % per_b, 0))
                     for _win, dil in DIL_PAIRS]
    return pl.pallas_call(
        _mix_kernel,
        grid=(t // MIX_TM,),
        in_specs=grp(ATT_WIDTH) + grp(LANES) + [
                  const((MIX_TM, MIX_TM)), const((MIX_TM, MIX_TM)), const((LANES, ATT_WIDTH)),
                  row(GM_WIDTH), row(GATE_COLS), row(D_MODEL),
                  modb, modb, modb,
                  const((GM_WIDTH, D_MODEL)), const((ATT_WIDTH, D_MODEL)), const((D_MODEL, D_MODEL)),
                  const((1, D_MODEL)), const((1, D_MODEL)),
                  const((D_MODEL, LANES)), const((D_MODEL, LANES)), const((1, LANES)),
                  const((MIX_TM, MIX_TM))],
        out_specs=[row(D_MODEL), pl.BlockSpec((MIX_TM * ROW_SUB, LANES), lambda i: (i, 0)),
                   row(LANES), row(LANES), const((8, LANES))],
        out_shape=[jax.ShapeDtypeStruct((t, D_MODEL), f32),
                   jax.ShapeDtypeStruct((t * ROW_SUB, LANES), i32),
                   jax.ShapeDtypeStruct((t, LANES), i32),
                   jax.ShapeDtypeStruct((t, LANES), f32),
                   jax.ShapeDtypeStruct((8, LANES), f32)],
        scratch_shapes=[pltpu.VMEM((1, LANES), f32)],
        compiler_params=_params(("arbitrary",)),
        name="mix",
    )(*os_, *ls_, perms_t[1], perms_t[2], expand, ya, gates, x2, g1, sc2, sh2, wa, wb, wo,
      ln1g, ln1b, wr_hi, wr_lo, br, tri)


MOE_TM = 256


def _moe_kernel(te_ref, first_ref, nexte_ref, wslot_ref, nused_ref,
                tok_cur_ref, tok_nxt_ref, h2_hbm, rww_ref, wg_hbm, wu_hbm, wd_hbm,
                bg_ref, bu_ref, bd_ref,
                out_ref, xbuf0, xbuf1, wbuf, wgb, wub, wdb, sem_x, sem_w):
    j = pl.program_id(0)
    last = pl.num_programs(0) - 1
    xbufs = (xbuf0, xbuf1)

    def row_copy(tok, i, s):
        return pltpu.make_async_copy(h2_hbm.at[pl.ds(pl.multiple_of(tok * ROW_SUB, ROW_SUB), ROW_SUB)],
                                     xbufs[s].at[pl.ds(pl.multiple_of(i * ROW_SUB, ROW_SUB), ROW_SUB)],
                                     sem_x.at[s])

    def rows_wait(s):
        pltpu.make_async_copy(h2_hbm.at[pl.ds(0, MOE_TM * ROW_SUB)], xbufs[s], sem_x.at[s]).wait()

    def weight_copies(e, ws):
        return [pltpu.make_async_copy(w.at[e], wbuf.at[ws, k], sem_w.at[ws])
                for k, w in enumerate((wg_hbm, wu_hbm, wd_hbm))]

    def gather_loop(tok_ref, s):
        def body(i, c):
            for p in range(2):
                row_copy(tok_ref[0, 0, 2 * i + p], 2 * i + p, s).start(priority=p)
            return c
        lax.fori_loop(0, MOE_TM // 2, body, 0, unroll=4)

    @pl.when(j == 0)
    def _():
        gather_loop(tok_cur_ref, 0)
        for cp in weight_copies(te_ref[0], wslot_ref[0]):
            cp.start(priority=1)

    def load_weights():
        ws = wslot_ref[j]
        for cp in weight_copies(te_ref[j], ws):
            cp.wait()
        wgb[...] = wbuf[ws, 0].astype(bf16)
        wub[...] = wbuf[ws, 1].astype(bf16)
        wdb[...] = wbuf[ws, 2].astype(bf16)
        ne = nexte_ref[j]

        @pl.when(ne >= 0)
        def _():
            for cp in weight_copies(ne, 1 - ws):
                cp.start(priority=1)

    def expert_mlp(s):
        gather_loop(tok_nxt_ref, 1 - s)
        xb =_unpack_rows(_load_packed(xbufs[s], 0, MOE_TM)).astype(bf16)
        g = jnp.dot(xb, wgb[...], preferred_element_type=f32) + bg_ref[0]
        u = jnp.dot(xb, wub[...], preferred_element_type=f32) + bu_ref[0]
        g = jnp.minimum(g, SWIGLU_LIMIT)
        u = jnp.clip(u, -SWIGLU_LIMIT, SWIGLU_LIMIT)
        act = (u + 1.0) * (g * jax.nn.sigmoid(SWIGLU_ALPHA * g))
        y = (jnp.dot(act.astype(bf16), wdb[...], preferred_element_type=f32) + bd_ref[0]) * rww_ref[...]
        _store_packed(out_ref, _pack_rows(y), MOE_TM)

    def idle_tile(s):
        gather_loop(tok_nxt_ref, s)
        out_ref[...] = jnp.zeros_like(out_ref)

    used = j < nused_ref[0]
    for s in range(2):
        @pl.when(j % 2 == s)
        def _(s=s):
            rows_wait(s)
            pl.when(first_ref[j] == 1)(load_weights)
            pl.when(used)(functools.partial(expert_mlp, s))
            pl.when(jnp.logical_not(used))(functools.partial(idle_tile, 1 - s))
            pl.when(j == last)(functools.partial(rows_wait, 1 - s))


def _moe(tile_e, tile_first, next_e, wslot, n_used, row_tok3, h2p, row_w,
         w_gate, b_gate, w_up, b_up, w_down, b_down):
    ntile = tile_e.shape[0]
    bspec = pl.BlockSpec((1, 1, D_MODEL), lambda j, te, *_: (te[j], 0, 0))
    hbm = pl.BlockSpec(memory_space=pl.ANY)
    grid_spec = pltpu.PrefetchScalarGridSpec(
        num_scalar_prefetch=5,
        grid=(ntile,),
        in_specs=[pl.BlockSpec((1, 1, MOE_TM), lambda j, *_: (j, 0, 0), memory_space=pltpu.SMEM),
                  pl.BlockSpec((1, 1, MOE_TM), lambda j, *_: (jnp.minimum(j + 1, ntile - 1), 0, 0),
                               memory_space=pltpu.SMEM),
                  hbm,
                  pl.BlockSpec((MOE_TM, 1), lambda j, *_: (j, 0)),
                  hbm, hbm, hbm, bspec, bspec, bspec],
        out_specs=pl.BlockSpec((MOE_TM * ROW_SUB, LANES), lambda j, *_: (j, 0)),
        scratch_shapes=[pltpu.VMEM((MOE_TM * ROW_SUB, LANES), i32),
                        pltpu.VMEM((MOE_TM * ROW_SUB, LANES), i32),
                        pltpu.VMEM((2, 3, D_MODEL, D_MODEL), f32),
                        pltpu.VMEM((D_MODEL, D_MODEL), bf16),
                        pltpu.VMEM((D_MODEL, D_MODEL), bf16),
                        pltpu.VMEM((D_MODEL, D_MODEL), bf16),
                        pltpu.SemaphoreType.DMA((2,)),
                        pltpu.SemaphoreType.DMA((2,))],
    )
    return pl.pallas_call(
        _moe_kernel,
        grid_spec=grid_spec,
        out_shape=jax.ShapeDtypeStruct((ntile * MOE_TM * ROW_SUB, LANES), i32),
        compiler_params=_params(("arbitrary",)),
        name="moe",
    )(tile_e, tile_first, next_e, wslot, n_used, row_tok3, row_tok3, h2p, row_w,
      w_gate, w_up, w_down, b_gate, b_up, b_down)


CB_TM = 128


def _combine_kernel(dcur_ref, dnxt_ref, yb_hbm, x1_ref, g2_ref, lng_ref, lnb_ref, out_ref,
                    ybuf0, ybuf1, sem):
    i = pl.program_id(0)
    last = pl.num_programs(0) - 1
    ybufs = (ybuf0, ybuf1)

    def row_copy(d, k, r, s):
        return pltpu.make_async_copy(
            yb_hbm.at[pl.ds(pl.multiple_of(d * ROW_SUB, ROW_SUB), ROW_SUB)],
            ybufs[s].at[pl.ds(pl.multiple_of((k * CB_TM + r) * ROW_SUB, ROW_SUB), ROW_SUB)],
            sem.at[s])

    @pl.when(i == 0)
    def _():
        for k in range(TOP_K):
            def body(r, c, k=k):
                row_copy(dcur_ref[0, k, r], k, r, 0).start()
                return c
            lax.fori_loop(0, CB_TM, body, 0, unroll=8)

    for s in range(2):
        @pl.when(i % 2 == s)
        def _(s=s):
            pltpu.make_async_copy(yb_hbm.at[pl.ds(0, TOP_K * CB_TM * ROW_SUB)], ybufs[s], sem.at[s]).wait()

            @pl.when(i < last)
            def _():
                for k in range(TOP_K):
                    for r in range(CB_TM):
                        row_copy(dnxt_ref[0, k, r], k, r, 1 - s).start(priority=r % 2)

            parts = [_unpack_rows(_load_packed(ybufs[s], k * CB_TM, CB_TM)) for k in range(TOP_K)]
            y = (parts[0] + parts[1]) + (parts[2] + parts[3])
            out_ref[...] = _ln(DN_ALPHA * x1_ref[...] + g2_ref[0] * y) * lng_ref[...] + lnb_ref[...]


def _combine(dest3, yb, x1, g2, ln2g, ln2b, seq):
    t = x1.shape[0]
    nb = t // CB_TM
    per_b = seq // CB_TM
    return pl.pallas_call(
        _combine_kernel,
        grid=(nb,),
        in_specs=[pl.BlockSpec((1, TOP_K, CB_TM), lambda i: (i, 0, 0), memory_space=pltpu.SMEM),
                  pl.BlockSpec((1, TOP_K, CB_TM), lambda i: (jnp.minimum(i + 1, nb - 1), 0, 0),
                               memory_space=pltpu.SMEM),
                  pl.BlockSpec(memory_space=pl.ANY),
                  pl.BlockSpec((CB_TM, D_MODEL), lambda i: (i, 0)),
                  pl.BlockSpec((1, 1, D_MODEL), lambda i: (i // per_b, 0, 0)),
                  pl.BlockSpec((1, D_MODEL), lambda i: (0, 0)),
                  pl.BlockSpec((1, D_MODEL), lambda i: (0, 0))],
        out_specs=pl.BlockSpec((CB_TM, D_MODEL), lambda i: (i, 0)),
        out_shape=jax.ShapeDtypeStruct((t, D_MODEL), f32),
        scratch_shapes=[pltpu.VMEM((TOP_K * CB_TM * ROW_SUB, LANES), i32),
                        pltpu.VMEM((TOP_K * CB_TM * ROW_SUB, LANES), i32),
                        pltpu.SemaphoreType.DMA((2,))],
        compiler_params=_params(("arbitrary",)),
        name="combine",
    )(dest3, dest3, yb, x1, g2, ln2g, ln2b)


def _t5_bucket(dist):
    d = dist.astype(f32)
    large = REL_MAX_EXACT + jnp.log(jnp.maximum(d, float(REL_MAX_EXACT)) / REL_MAX_EXACT) / math.log(
        REL_MAX_DIST / REL_MAX_EXACT) * (REL_BUCKETS - REL_MAX_EXACT)
    large = jnp.minimum(large.astype(i32), REL_BUCKETS - 1)
    return jnp.where(dist < REL_MAX_EXACT, dist, large)


def _bias_indices():
    qi = jnp.arange(ATT_BLOCK)[:, None]
    ki = jnp.arange(2 * ATT_BLOCK)[None, :]
    didx = qi + ATT_BLOCK - ki
    buckets, bands = [], []
    for win, dil in DIL_PAIRS:
        buckets.append(_t5_bucket(jnp.clip(didx, 0, None) * dil))
        bands.append(((didx >= 0) & (didx <= win // dil)).astype(i32))
    return jnp.stack(buckets).astype(i32), jnp.stack(bands)


def _residue_perm(tm, dil):
    n = tm // dil
    dst = jnp.arange(tm)
    src = (dst % n) * dil + dst // n
    return (src[:, None] == jnp.arange(tm)[None, :]).astype(bf16)


def kernel(x, c, w_ada, b_ada, w_in, gm_ln_g, gm_ln_b, gm_w_s, gm_b_s, w_branch_a, w_branch_b, w_out,
           rel_bias, ln1_g, ln1_b, w_router, b_router, w_gate, b_gate, w_up, b_up, w_down, b_down,
           ln2_g, ln2_b):
    batch, seq, _ = x.shape
    t = batch * seq
    l = 0
    x2 = x.reshape(t, D_MODEL)

    c8 = jnp.pad(c, ((0, 8 - batch), (0, 0)))
    mod = _adaln(c8, w_ada[l], b_ada[l][None, :])[:batch]
    sh1, sc1, g1, sh2, sc2, g2 = [m[:, None, :] for m in jnp.split(mod, 6, axis=-1)]

    perms = [_residue_perm(IN_TM, dil) for _win, dil in DIL_PAIRS]
    uv, gates, *qkvs = _inproj(x2, sc1, sh1, w_in[l].astype(bf16), perms, batch, seq)

    bs_full = jnp.repeat(gm_b_s[l].T, GM_WIDTH // GM_GROUPS, axis=1)
    ya = _gmlp(uv, gm_ln_g[l][None, :], gm_ln_b[l][None, :], gm_w_s[l], bs_full)

    bucket, band = _bias_indices()
    bias = _relbias(rel_bias, bucket, band)
    os_, ls_ = [], []
    for g, (_win, dil) in enumerate(DIL_PAIRS):
        o, lse = _attn_group(qkvs[g], bias, g, dil, batch, seq)
        os_.append(o)
        ls_.append(lse)

    wr = jnp.pad(w_router[l], ((0, 0), (0, LANES - N_EXPERTS)))
    wr_hi = wr.astype(bf16)
    wr_lo = (wr - wr_hi.astype(f32)).astype(bf16)
    br = jnp.pad(b_router[l], (0, LANES - N_EXPERTS))[None, :]
    tri = (jnp.arange(MIX_TM)[None, :] < jnp.arange(MIX_TM)[:, None]).astype(bf16)
    perms_t = [_residue_perm(MIX_TM, dil).T for _win, dil in DIL_PAIRS]
    expand = (jnp.arange(LANES)[:, None] == jnp.arange(ATT_WIDTH)[None, :] // HEAD_DIM).astype(bf16)
    x1, h2, route, rw, cnt = _mix(
        os_, ls_, perms_t, expand, ya, gates, x2, g1, sc2, sh2,
        w_branch_a[l].astype(bf16), w_branch_b[l].astype(bf16), w_out[l].astype(bf16),
        ln1_g[l][None, :], ln1_b[l][None, :], wr_hi, wr_lo, br, tri, seq)

    top_e = route[:, :TOP_K]
    rank = route[:, TOP_K:2 * TOP_K]
    top_w = rw[:, :TOP_K]
    counts = cnt[0, :N_EXPERTS].astype(i32)
    pcounts = (counts + MOE_TM - 1) // MOE_TM * MOE_TM
    pends = jnp.cumsum(pcounts)
    pstarts = pends - pcounts
    dest = pstarts[top_e] + rank
    a_total = t * TOP_K
    ntile = a_total // MOE_TM + N_EXPERTS
    r_total = ntile * MOE_TM
    inv = jnp.full((r_total,), -1, i32).at[dest.reshape(-1)].set(
        jnp.arange(a_total, dtype=i32), unique_indices=True, mode='promise_in_bounds')
    valid = inv >= 0
    safe = jnp.maximum(inv, 0)
    row_tok = jnp.where(valid, safe // TOP_K, jnp.arange(r_total, dtype=i32) % t)
    row_w = jnp.where(valid, top_w.reshape(-1)[safe], 0.0)
    n_used = pends[-1] // MOE_TM
    tile_idx = jnp.minimum(jnp.arange(ntile, dtype=i32), n_used - 1)
    tile_e = jnp.sum((pends[None, :] <= (tile_idx * MOE_TM)[:, None]).astype(i32), axis=1)
    tile_first = jnp.concatenate([jnp.ones((1,), i32), (tile_e[1:] != tile_e[:-1]).astype(i32)])
    experts = jnp.arange(N_EXPERTS, dtype=i32)
    nonempty = counts > 0
    later = lax.cummin(jnp.where(nonempty, experts, N_EXPERTS), reverse=True)
    next_nonempty = jnp.concatenate([later[1:], jnp.full((1,), N_EXPERTS, i32)])
    next_nonempty = jnp.where(next_nonempty >= N_EXPERTS, -1, next_nonempty)
    expert_slot = (jnp.cumsum(nonempty.astype(i32)) - 1) % 2

    yb = _moe(tile_e, tile_first, next_nonempty[tile_e], expert_slot[tile_e], n_used.reshape(1),
              row_tok.reshape(ntile, 1, MOE_TM), h2, row_w[:, None],
              w_gate[l], b_gate[l][:, None, :], w_up[l], b_up[l][:, None, :],
              w_down[l], b_down[l][:, None, :])

    dest3 = dest.reshape(t // CB_TM, CB_TM, TOP_K).transpose(0, 2, 1)
    out = _combine(dest3, yb, x1, g2, ln2_g[l][None, :], ln2_b[l][None, :], seq)
    return out.reshape(batch, seq, D_MODEL)
```

```python
import functools
import math

import jax
import jax.numpy as jnp
from jax import lax
from jax.experimental import pallas as pl
from jax.experimental.pallas import tpu as pltpu

f32 = jnp.float32
bf16 = jnp.bfloat16
i32 = jnp.int32

D_MODEL = 1024
GM_WIDTH = 512
GM_GROUPS = 8
GM_CHUNK = 128
DIL_PAIRS = ((128, 1), (512, 4), (2048, 16))
N_DIL = 3
HEADS_PER_GROUP = 8
HEAD_DIM = 64
ATT_WIDTH = 512
ATT_BLOCK = 128
NEG_INF = -1e30
REL_BUCKETS = 32
REL_MAX_EXACT = 16
REL_MAX_DIST = 2048
N_EXPERTS = 32
TOP_K = 4
SWIGLU_LIMIT = 7.0
SWIGLU_ALPHA = 1.702
MOE_BLOCK = 128
DEPTH = 1
DN_ALPHA = (2 * DEPTH) ** 0.25
LN_EPS = 1e-5
UV_COLS = 2 * GM_WIDTH
QKV_COLS = N_DIL * 3 * ATT_WIDTH
GATE_COLS = 2 * D_MODEL
IN_COLS = UV_COLS + QKV_COLS + GATE_COLS

LANES = 128
SUBLANES = 8
VMEM_LIMIT = 56 * 1024 * 1024


def _ln(x):
    mu = jnp.mean(x, axis=-1, keepdims=True)
    xc = x - mu
    var = jnp.mean(xc * xc, axis=-1, keepdims=True)
    return xc * lax.rsqrt(var + LN_EPS)


def _params(sem, vmem=VMEM_LIMIT):
    return pltpu.CompilerParams(dimension_semantics=sem, vmem_limit_bytes=vmem)


def _adaln_kernel(c_ref, w_ref, b_ref, o_ref):
    c = c_ref[...]
    s = c * jax.nn.sigmoid(c)
    o_ref[...] = jnp.dot(s, w_ref[...], preferred_element_type=f32,
                         precision=lax.Precision.HIGHEST) + b_ref[...]


def _adaln(c8, w_ada, b_ada):
    n = w_ada.shape[1] // D_MODEL
    return pl.pallas_call(
        _adaln_kernel,
        grid=(n,),
        in_specs=[pl.BlockSpec((8, D_MODEL), lambda j: (0, 0)),
                  pl.BlockSpec((D_MODEL, D_MODEL), lambda j: (0, j)),
                  pl.BlockSpec((1, D_MODEL), lambda j: (0, j))],
        out_specs=pl.BlockSpec((8, D_MODEL), lambda j: (0, j)),
        out_shape=jax.ShapeDtypeStruct((8, w_ada.shape[1]), f32),
        compiler_params=_params(("arbitrary",)),
        name="adaln",
    )(c8, w_ada, b_ada)


IN_TM = 256
IN_CW = 512
GRP_COLS = 3 * ATT_WIDTH


def _inproj_kernel(x_ref, sc_ref, sh_ref, w_ref, p1_ref, p2_ref,
                   uv_ref, gt_ref, qkv0_ref, qkv1_ref, qkv2_ref):
    xn = _ln(x_ref[...])
    h = (xn * (1.0 + sc_ref[0]) + sh_ref[0]).astype(bf16)
    hp = [h,
          jnp.dot(p1_ref[...], h, preferred_element_type=f32).astype(bf16),
          jnp.dot(p2_ref[...], h, preferred_element_type=f32).astype(bf16)]
    for c0 in range(0, UV_COLS, IN_CW):
        acc = jnp.dot(h, w_ref[:, c0:c0 + IN_CW], preferred_element_type=f32)
        uv_ref[:, c0:c0 + IN_CW] = jax.nn.gelu(acc).astype(bf16)
    for g, (qref, (_win, dil)) in enumerate(zip((qkv0_ref, qkv1_ref, qkv2_ref), DIL_PAIRS)):
        n = IN_TM // dil
        for q0 in range(0, GRP_COLS, IN_CW):
            c0 = UV_COLS + g * GRP_COLS + q0
            acc = jnp.dot(hp[g], w_ref[:, c0:c0 + IN_CW], preferred_element_type=f32).astype(bf16)
            for rho in range(dil):
                qref[0, rho, :, q0:q0 + IN_CW] = acc[rho * n:(rho + 1) * n, :]
    for g0 in range(0, GATE_COLS, IN_CW):
        c0 = UV_COLS + QKV_COLS + g0
        acc = jnp.dot(h, w_ref[:, c0:c0 + IN_CW], preferred_element_type=f32)
        gt_ref[:, g0:g0 + IN_CW] = jax.nn.sigmoid(acc).astype(bf16)


def _inproj(x2, sc1, sh1, w_in_bf, perms, batch, seq):
    t = x2.shape[0]
    per_b = seq // IN_TM
    qkv_specs, qkv_shapes = [], []
    for _win, dil in DIL_PAIRS:
        n = IN_TM // dil
        qkv_specs.append(pl.BlockSpec((1, dil, n, GRP_COLS), lambda i: (i // per_b, 0, i % per_b, 0)))
        qkv_shapes.append(jax.ShapeDtypeStruct((batch, dil, seq // dil, GRP_COLS), bf16))
    return pl.pallas_call(
        _inproj_kernel,
        grid=(t // IN_TM,),
        in_specs=[pl.BlockSpec((IN_TM, D_MODEL), lambda i: (i, 0)),
                  pl.BlockSpec((1, 1, D_MODEL), lambda i: (i // per_b, 0, 0)),
                  pl.BlockSpec((1, 1, D_MODEL), lambda i: (i // per_b, 0, 0)),
                  pl.BlockSpec((D_MODEL, IN_COLS), lambda i: (0, 0)),
                  pl.BlockSpec((IN_TM, IN_TM), lambda i: (0, 0)),
                  pl.BlockSpec((IN_TM, IN_TM), lambda i: (0, 0))],
        out_specs=[pl.BlockSpec((IN_TM, UV_COLS), lambda i: (i, 0)),
                   pl.BlockSpec((IN_TM, GATE_COLS), lambda i: (i, 0))] + qkv_specs,
        out_shape=[jax.ShapeDtypeStruct((t, UV_COLS), bf16),
                   jax.ShapeDtypeStruct((t, GATE_COLS), bf16)] + qkv_shapes,
        compiler_params=_params(("arbitrary",)),
        name="inproj",
    )(x2, sc1, sh1, w_in_bf, perms[1], perms[2])


GM_TM = 512


def _gmlp_kernel(u_ref, v_ref, g_ref, b_ref, ws_ref, bs_ref, ya_ref):
    row = lax.broadcasted_iota(i32, (GM_CHUNK, GM_CHUNK), 0)
    col = lax.broadcasted_iota(i32, (GM_CHUNK, GM_CHUNK), 1)
    causal = col <= row
    first_half = lax.broadcasted_iota(i32, (GM_CHUNK, LANES), 1) < (GM_WIDTH // GM_GROUPS)
    ws = [jnp.where(causal, ws_ref[g], 0.0).astype(bf16) for g in range(GM_GROUPS)]
    for ch in range(GM_TM // GM_CHUNK):
        r0 = ch * GM_CHUNK
        vn = _ln(v_ref[r0:r0 + GM_CHUNK, :].astype(f32)) * g_ref[...] + b_ref[...]
        vn = vn.astype(bf16)
        for j in range(GM_WIDTH // LANES):
            slab = vn[:, j * LANES:(j + 1) * LANES]
            s_lo = jnp.dot(ws[2 * j], slab, preferred_element_type=f32)
            s_hi = jnp.dot(ws[2 * j + 1], slab, preferred_element_type=f32)
            s = jnp.where(first_half, s_lo, s_hi) + bs_ref[:, j * LANES:(j + 1) * LANES]
            u = u_ref[r0:r0 + GM_CHUNK, j * LANES:(j + 1) * LANES].astype(f32)
            ya_ref[r0:r0 + GM_CHUNK, j * LANES:(j + 1) * LANES] = (u * s).astype(bf16)


def _gmlp(uv, ln_g, ln_b, w_s, bs_full):
    t = uv.shape[0]
    return pl.pallas_call(
        _gmlp_kernel,
        grid=(t // GM_TM,),
        in_specs=[pl.BlockSpec((GM_TM, GM_WIDTH), lambda i: (i, 0)),
                  pl.BlockSpec((GM_TM, GM_WIDTH), lambda i: (i, 1)),
                  pl.BlockSpec((1, GM_WIDTH), lambda i: (0, 0)),
                  pl.BlockSpec((1, GM_WIDTH), lambda i: (0, 0)),
                  pl.BlockSpec((GM_GROUPS, GM_CHUNK, GM_CHUNK), lambda i: (0, 0, 0)),
                  pl.BlockSpec((GM_CHUNK, GM_WIDTH), lambda i: (0, 0))],
        out_specs=pl.BlockSpec((GM_TM, GM_WIDTH), lambda i: (i, 0)),
        out_shape=jax.ShapeDtypeStruct((t, GM_WIDTH), bf16),
        compiler_params=_params(("arbitrary",)),
        name="gmlp",
    )(uv, uv, ln_g, ln_b, w_s, bs_full)


def _relbias_kernel(tab_ref, bucket_ref, band_ref, out_ref):
    g = pl.program_id(0)
    bk = bucket_ref[0]
    band = band_ref[0] > 0
    for h in range(HEADS_PER_GROUP):
        acc = jnp.zeros((ATT_BLOCK, 2 * ATT_BLOCK), f32)
        for b in range(REL_BUCKETS):
            acc = jnp.where(bk == b, tab_ref[b, g * HEADS_PER_GROUP + h], acc)
        out_ref[0, h] = jnp.where(band, acc, NEG_INF)


def _relbias(rel_bias, bucket, band):
    return pl.pallas_call(
        _relbias_kernel,
        grid=(N_DIL,),
        in_specs=[pl.BlockSpec(memory_space=pltpu.SMEM),
                  pl.BlockSpec((1, ATT_BLOCK, 2 * ATT_BLOCK), lambda g: (g, 0, 0)),
                  pl.BlockSpec((1, ATT_BLOCK, 2 * ATT_BLOCK), lambda g: (g, 0, 0))],
        out_specs=pl.BlockSpec((1, HEADS_PER_GROUP, ATT_BLOCK, 2 * ATT_BLOCK),
                               lambda g: (g, 0, 0, 0)),
        out_shape=jax.ShapeDtypeStruct((N_DIL, HEADS_PER_GROUP, ATT_BLOCK, 2 * ATT_BLOCK), f32),
        compiler_params=_params(("arbitrary",)),
        name="relbias",
    )(rel_bias, bucket, band)


def _attn_kernel(q_ref, kp_ref, kc_ref, vp_ref, vc_ref, bias_ref, o_ref, lse_ref):
    first = pl.program_id(2) == 0
    lane = lax.broadcasted_iota(i32, (ATT_BLOCK, LANES), 1)
    lo_half = lane < HEAD_DIM
    nt = (((1,), (1,)), ((), ()))
    ones = jnp.ones((2 * ATT_BLOCK, LANES), bf16)
    n_slab = ATT_WIDTH // LANES
    logits, v_ext = [], []
    for j in range(n_slab):
        sl = slice(j * LANES, (j + 1) * LANES)
        q = q_ref[0, 0, :, sl] * (HEAD_DIM ** -0.5)
        k_cat = jnp.concatenate([kp_ref[0, 0, :, sl], kc_ref[0, 0, :, sl]], axis=0)
        v_cat = jnp.concatenate([vp_ref[0, 0, :, sl], vc_ref[0, 0, :, sl]], axis=0)
        v_ext.append(jnp.concatenate([v_cat, ones], axis=1))
        for hh in range(2):
            qm = jnp.where(lo_half if hh == 0 else jnp.logical_not(lo_half), q, 0.0).astype(bf16)
            logits.append(lax.dot_general(qm, k_cat, nt, preferred_element_type=f32))
    lg = jnp.concatenate(logits, axis=0) + bias_ref[0].reshape(HEADS_PER_GROUP * ATT_BLOCK, 2 * ATT_BLOCK)
    prev_cols = lax.broadcasted_iota(i32, lg.shape, 1) < ATT_BLOCK
    lg = jnp.where(jnp.logical_and(first, prev_cols), NEG_INF, lg)
    m = jnp.max(lg, axis=-1, keepdims=True)
    p = jnp.exp(lg - m).astype(bf16)
    lse_tile = jnp.zeros((ATT_BLOCK, LANES), f32)
    for j in range(n_slab):
        outs = []
        for hh in range(2):
            h = 2 * j + hh
            r = jnp.dot(p[h * ATT_BLOCK:(h + 1) * ATT_BLOCK], v_ext[j], preferred_element_type=f32)
            den = r[:, LANES:]
            outs.append(r[:, :LANES] * (1.0 / den))
            lse_h = m[h * ATT_BLOCK:(h + 1) * ATT_BLOCK] + jnp.log(den)
            lse_tile = jnp.where(lane == h, lse_h, lse_tile)
        o_ref[0, 0, :, j * LANES:(j + 1) * LANES] = jnp.where(lo_half, outs[0], outs[1]).astype(bf16)
    lse_ref[0, 0] = lse_tile


def _attn_group(qkv_g, bias, g, dil, batch, seq):
    l = seq // dil
    nb = l // ATT_BLOCK

    def spec(cb, prev):
        if prev:
            return pl.BlockSpec((1, 1, ATT_BLOCK, ATT_WIDTH),
                                lambda b, r, n: (b, r, jnp.maximum(n - 1, 0), cb))
        return pl.BlockSpec((1, 1, ATT_BLOCK, ATT_WIDTH), lambda b, r, n: (b, r, n, cb))

    return pl.pallas_call(
        _attn_kernel,
        grid=(batch, dil, nb),
        in_specs=[spec(0, False), spec(1, True), spec(1, False), spec(2, True), spec(2, False),
                  pl.BlockSpec((1, HEADS_PER_GROUP, ATT_BLOCK, 2 * ATT_BLOCK),
                               lambda b, r, n: (g, 0, 0, 0))],
        out_specs=[pl.BlockSpec((1, 1, ATT_BLOCK, ATT_WIDTH), lambda b, r, n: (b, r, n, 0)),
                   pl.BlockSpec((1, 1, ATT_BLOCK, LANES), lambda b, r, n: (b, r, n, 0))],
        out_shape=[jax.ShapeDtypeStruct((batch, dil, l, ATT_WIDTH), bf16),
                   jax.ShapeDtypeStruct((batch, dil, l, LANES), f32)],
        compiler_params=_params(("arbitrary", "arbitrary", "arbitrary")),
        name=f"attn_g{g}",
    )(qkv_g, qkv_g, qkv_g, qkv_g, qkv_g, bias)


ROW_WORDS = D_MODEL // 2
ROW_SUB = ROW_WORDS // LANES
HI_MASK = -65536


def _pack_rows(x):
    bits = lax.bitcast_convert_type(x.astype(bf16).astype(f32), i32)
    return lax.shift_right_logical(bits[:, :ROW_WORDS], 16) | (bits[:, ROW_WORDS:] & HI_MASK)


def _unpack_rows(words):
    lo = lax.bitcast_convert_type(lax.shift_left(words, 16), f32)
    hi = lax.bitcast_convert_type(words & HI_MASK, f32)
    return jnp.concatenate([lo, hi], axis=1)


def _store_packed(ref, words, n, first_row=0):
    for r in range(ROW_SUB):
        ref[pl.ds(first_row * ROW_SUB + r, n, stride=ROW_SUB), :] = words[:, r * LANES:(r + 1) * LANES]


def _load_packed(ref, first_row, n):
    return jnp.concatenate([ref[pl.ds(first_row * ROW_SUB + r, n, stride=ROW_SUB), :] for r in range(ROW_SUB)],
                           axis=1)


MIX_TM = 256
MIX_SUB = 128


def _split_bf16(x, parts):
    out = []
    for _ in range(parts):
        hi = x.astype(bf16)
        out.append(hi)
        x = x - hi.astype(f32)
    return out


def _mix_kernel(o0_ref, o1_ref, o2_ref, l0_ref, l1_ref, l2_ref, pt1_ref, pt2_ref, ex_ref,
                ya_ref, gt_ref, x_ref,
                g1_ref, sc2_ref, sh2_ref, wa_ref, wb_ref, wo_ref, ln1g_ref, ln1b_ref,
                wrh_ref, wrl_ref, br_ref, tri_ref,
                x1_ref, h2_ref, route_ref, rw_ref, cnt_ref, run_ref):
    @pl.when(pl.program_id(0) == 0)
    def _():
        run_ref[...] = jnp.zeros_like(run_ref)

    lane = lax.broadcasted_iota(i32, (MIX_SUB, LANES), 1)
    lane_f = lane.astype(f32)

    def token_rows(r0):
        rows = slice(r0, r0 + MIX_SUB)
        os_, ls_ = [o0_ref[0, 0, rows, :].astype(f32)], [l0_ref[0, 0, rows, :]]
        for o_ref, l_ref, pt_ref in ((o1_ref, l1_ref, pt1_ref), (o2_ref, l2_ref, pt2_ref)):
            pt = pt_ref[rows, :]
            os_.append(jnp.dot(pt, o_ref[0].reshape(MIX_TM, ATT_WIDTH), preferred_element_type=f32))
            parts = [jnp.dot(pt, part, preferred_element_type=f32)
                     for part in _split_bf16(l_ref[0].reshape(MIX_TM, LANES), 3)]
            ls_.append((parts[0] + parts[1]) + parts[2])
        lm = jnp.maximum(jnp.maximum(ls_[0], ls_[1]), ls_[2])
        es = [jnp.exp(lse - lm) for lse in ls_]
        inv = 1.0 / (es[0] + es[1] + es[2])
        yb = jnp.zeros((MIX_SUB, ATT_WIDTH), f32)
        for e, o in zip(es, os_):
            w_hi, w_lo = _split_bf16(e * inv, 2)
            w_full = (jnp.dot(w_hi, ex_ref[...], preferred_element_type=f32)
                      + jnp.dot(w_lo, ex_ref[...], preferred_element_type=f32))
            yb = yb + w_full * o
        a = jnp.dot(ya_ref[rows, :], wa_ref[...], preferred_element_type=f32)
        b = jnp.dot(yb.astype(bf16), wb_ref[...], preferred_element_type=f32)
        merged = gt_ref[rows, :D_MODEL].astype(f32) * a + gt_ref[rows, D_MODEL:].astype(f32) * b
        mix = jnp.dot(merged.astype(bf16), wo_ref[...], preferred_element_type=f32)
        x1 = _ln(DN_ALPHA * x_ref[rows, :] + g1_ref[0] * mix) * ln1g_ref[...] + ln1b_ref[...]
        x1_ref[rows, :] = x1
        h2 = _ln(x1) * (1.0 + sc2_ref[0]) + sh2_ref[0]
        _store_packed(h2_ref, _pack_rows(h2), MIX_SUB, r0)

        h_hi, h_lo = _split_bf16(h2, 2)
        logits = (jnp.dot(h_hi, wrh_ref[...], preferred_element_type=f32)
                  + (jnp.dot(h_hi, wrl_ref[...], preferred_element_type=f32)
                     + jnp.dot(h_lo, wrh_ref[...], preferred_element_type=f32))) + br_ref[...]
        logits = jnp.where(lane < N_EXPERTS, logits, -jnp.inf)
        vals, idxs = [], []
        for _k in range(TOP_K):
            m = jnp.max(logits, axis=-1, keepdims=True)
            idx = jnp.min(jnp.where(logits == m, lane_f, float(LANES)), axis=-1, keepdims=True).astype(i32)
            vals.append(m)
            idxs.append(idx)
            logits = jnp.where(lane == idx, -jnp.inf, logits)
        exps = [jnp.exp(v - vals[0]) for v in vals]
        den = exps[0] + exps[1] + exps[2] + exps[3]
        return idxs, [e / den for e in exps]

    subs = [token_rows(r0) for r0 in range(0, MIX_TM, MIX_SUB)]
    idxs = [jnp.concatenate([sub[0][k] for sub in subs], axis=0) for k in range(TOP_K)]
    wts = [jnp.concatenate([sub[1][k] for sub in subs], axis=0) for k in range(TOP_K)]

    lane = lax.broadcasted_iota(i32, (MIX_TM, LANES), 1)
    hits = [lane == idx for idx in idxs]
    onehot = jnp.zeros((MIX_TM, LANES), f32)
    for hit in hits:
        onehot = onehot + jnp.where(hit, 1.0, 0.0)
    prefix = jnp.dot(tri_ref[...], onehot.astype(bf16), preferred_element_type=f32) + run_ref[...]
    route = jnp.zeros((MIX_TM, LANES), i32)
    rw = jnp.zeros((MIX_TM, LANES), f32)
    for k in range(TOP_K):
        rank = jnp.sum(jnp.where(hits[k], prefix, 0.0), axis=-1, keepdims=True).astype(i32)
        route = jnp.where(lane == k, idxs[k], route)
        route = jnp.where(lane == TOP_K + k, rank, route)
        rw = jnp.where(lane == k, wts[k], rw)
    route_ref[...] = route
    rw_ref[...] = rw
    run = run_ref[...] + jnp.sum(onehot, axis=0, keepdims=True)
    run_ref[...] = run
    cnt_ref[...] = jnp.broadcast_to(run, cnt_ref.shape)


def _mix(os_, ls_, perms_t, expand, ya, gates, x2, g1, sc2, sh2, wa, wb, wo, ln1g, ln1b, wr_hi, wr_lo, br, tri,
         seq):
    t = x2.shape[0]
    per_b = seq // MIX_TM
    row = lambda w: pl.BlockSpec((MIX_TM, w), lambda i: (i, 0))
    const = lambda s: pl.BlockSpec(s, lambda i: tuple(0 for _ in s))
    modb = pl.BlockSpec((1, 1, D_MODEL), lambda i: (i // per_b, 0, 0))
    grp = lambda w: [pl.BlockSpec((1, dil, MIX_TM // dil, w), lambda i: (i // per_b, 0, i % per_b, 0))
                     for _win, dil in DIL_PAIRS]
    return pl.pallas_call(
        _mix_kernel,
        grid=(t // MIX_TM,),
        in_specs=grp(ATT_WIDTH) + grp(LANES) + [
                  const((MIX_TM, MIX_TM)), const((MIX_TM, MIX_TM)), const((LANES, ATT_WIDTH)),
                  row(GM_WIDTH), row(GATE_COLS), row(D_MODEL),
                  modb, modb, modb,
                  const((GM_WIDTH, D_MODEL)), const((ATT_WIDTH, D_MODEL)), const((D_MODEL, D_MODEL)),
                  const((1, D_MODEL)), const((1, D_MODEL)),
                  const((D_MODEL, LANES)), const((D_MODEL, LANES)), const((1, LANES)),
                  const((MIX_TM, MIX_TM))],
        out_specs=[row(D_MODEL), pl.BlockSpec((MIX_TM * ROW_SUB, LANES), lambda i: (i, 0)),
                   row(LANES), row(LANES), const((8, LANES))],
        out_shape=[jax.ShapeDtypeStruct((t, D_MODEL), f32),
                   jax.ShapeDtypeStruct((t * ROW_SUB, LANES), i32),
                   jax.ShapeDtypeStruct((t, LANES), i32),
                   jax.ShapeDtypeStruct((t, LANES), f32),
                   jax.ShapeDtypeStruct((8, LANES), f32)],
        scratch_shapes=[pltpu.VMEM((1, LANES), f32)],
        compiler_params=_params(("arbitrary",)),
        name="mix",
    )(*os_, *ls_, perms_t[1], perms_t[2], expand, ya, gates, x2, g1, sc2, sh2, wa, wb, wo,
      ln1g, ln1b, wr_hi, wr_lo, br, tri)


MOE_TM = 256


def _moe_kernel(te_ref, first_ref, nexte_ref, wslot_ref, nused_ref,
                tok_cur_ref, tok_nxt_ref, h2_hbm, rww_ref, wg_hbm, wu_hbm, wd_hbm,
                bg_ref, bu_ref, bd_ref,
                out_ref, xbuf0, xbuf1, wbuf, wgb, wub, wdb, sem_x, sem_w):
    j = pl.program_id(0)
    last = pl.num_programs(0) - 1
    xbufs = (xbuf0, xbuf1)

    def row_copy(tok, i, s):
        return pltpu.make_async_copy(h2_hbm.at[pl.ds(pl.multiple_of(tok * ROW_SUB, ROW_SUB), ROW_SUB)],
                                     xbufs[s].at[pl.ds(pl.multiple_of(i * ROW_SUB, ROW_SUB), ROW_SUB)],
                                     sem_x.at[s])

    def rows_wait(s):
        pltpu.make_async_copy(h2_hbm.at[pl.ds(0, MOE_TM * ROW_SUB)], xbufs[s], sem_x.at[s]).wait()

    def weight_copies(e, ws):
        return [pltpu.make_async_copy(w.at[e], wbuf.at[ws, k], sem_w.at[ws])
                for k, w in enumerate((wg_hbm, wu_hbm, wd_hbm))]

    def gather_loop(tok_ref, s):
        def body(i, c):
            for p in range(2):
                row_copy(tok_ref[0, 0, 2 * i + p], 2 * i + p, s).start(priority=p)
            return c
        lax.fori_loop(0, MOE_TM // 2, body, 0, unroll=4)

    @pl.when(j == 0)
    def _():
        gather_loop(tok_cur_ref, 0)
        for cp in weight_copies(te_ref[0], wslot_ref[0]):
            cp.start(priority=1)

    def load_weights():
        ws = wslot_ref[j]
        for cp in weight_copies(te_ref[j], ws):
            cp.wait()
        wgb[...] = wbuf[ws, 0].astype(bf16)
        wub[...] = wbuf[ws, 1].astype(bf16)
        wdb[...] = wbuf[ws, 2].astype(bf16)
        ne = nexte_ref[j]

        @pl.when(ne >= 0)
        def _():
            for cp in weight_copies(ne, 1 - ws):
                cp.start(priority=1)

    def expert_mlp(s):
        for i in range(MOE_TM):
            row_copy(tok_nxt_ref[0, 0, i], i, 1 - s).start(priority=i % 2)
        xb =_unpack_rows(_load_packed(xbufs[s], 0, MOE_TM)).astype(bf16)
        g = jnp.dot(xb, wgb[...], preferred_element_type=f32) + bg_ref[0]
        u = jnp.dot(xb, wub[...], preferred_element_type=f32) + bu_ref[0]
        g = jnp.minimum(g, SWIGLU_LIMIT)
        u = jnp.clip(u, -SWIGLU_LIMIT, SWIGLU_LIMIT)
        act = (u + 1.0) * (g * jax.nn.sigmoid(SWIGLU_ALPHA * g))
        y = (jnp.dot(act.astype(bf16), wdb[...], preferred_element_type=f32) + bd_ref[0]) * rww_ref[...]
        _store_packed(out_ref, _pack_rows(y), MOE_TM)

    def idle_tile(s):
        gather_loop(tok_nxt_ref, s)
        out_ref[...] = jnp.zeros_like(out_ref)

    used = j < nused_ref[0]
    for s in range(2):
        @pl.when(j % 2 == s)
        def _(s=s):
            rows_wait(s)
            pl.when(first_ref[j] == 1)(load_weights)
            pl.when(used)(functools.partial(expert_mlp, s))
            pl.when(jnp.logical_not(used))(functools.partial(idle_tile, 1 - s))
            pl.when(j == last)(functools.partial(rows_wait, 1 - s))


def _moe(tile_e, tile_first, next_e, wslot, n_used, row_tok3, h2p, row_w,
         w_gate, b_gate, w_up, b_up, w_down, b_down):
    ntile = tile_e.shape[0]
    bspec = pl.BlockSpec((1, 1, D_MODEL), lambda j, te, *_: (te[j], 0, 0))
    hbm = pl.BlockSpec(memory_space=pl.ANY)
    grid_spec = pltpu.PrefetchScalarGridSpec(
        num_scalar_prefetch=5,
        grid=(ntile,),
        in_specs=[pl.BlockSpec((1, 1, MOE_TM), lambda j, *_: (j, 0, 0), memory_space=pltpu.SMEM),
                  pl.BlockSpec((1, 1, MOE_TM), lambda j, *_: (jnp.minimum(j + 1, ntile - 1), 0, 0),
                               memory_space=pltpu.SMEM),
                  hbm,
                  pl.BlockSpec((MOE_TM, 1), lambda j, *_: (j, 0)),
                  hbm, hbm, hbm, bspec, bspec, bspec],
        out_specs=pl.BlockSpec((MOE_TM * ROW_SUB, LANES), lambda j, *_: (j, 0)),
        scratch_shapes=[pltpu.VMEM((MOE_TM * ROW_SUB, LANES), i32),
                        pltpu.VMEM((MOE_TM * ROW_SUB, LANES), i32),
                        pltpu.VMEM((2, 3, D_MODEL, D_MODEL), f32),
                        pltpu.VMEM((D_MODEL, D_MODEL), bf16),
                        pltpu.VMEM((D_MODEL, D_MODEL), bf16),
                        pltpu.VMEM((D_MODEL, D_MODEL), bf16),
                        pltpu.SemaphoreType.DMA((2,)),
                        pltpu.SemaphoreType.DMA((2,))],
    )
    return pl.pallas_call(
        _moe_kernel,
        grid_spec=grid_spec,
        out_shape=jax.ShapeDtypeStruct((ntile * MOE_TM * ROW_SUB, LANES), i32),
        compiler_params=_params(("arbitrary",)),
        name="moe",
    )(tile_e, tile_first, next_e, wslot, n_used, row_tok3, row_tok3, h2p, row_w,
      w_gate, w_up, w_down, b_gate, b_up, b_down)


CB_TM = 128


def _combine_kernel(dcur_ref, dnxt_ref, yb_hbm, x1_ref, g2_ref, lng_ref, lnb_ref, out_ref,
                    ybuf0, ybuf1, sem):
    i = pl.program_id(0)
    last = pl.num_programs(0) - 1
    ybufs = (ybuf0, ybuf1)

    def row_copy(d, k, r, s):
        return pltpu.make_async_copy(
            yb_hbm.at[pl.ds(pl.multiple_of(d * ROW_SUB, ROW_SUB), ROW_SUB)],
            ybufs[s].at[pl.ds(pl.multiple_of((k * CB_TM + r) * ROW_SUB, ROW_SUB), ROW_SUB)],
            sem.at[s])

    @pl.when(i == 0)
    def _():
        for k in range(TOP_K):
            def body(r, c, k=k):
                row_copy(dcur_ref[0, k, r], k, r, 0).start()
                return c
            lax.fori_loop(0, CB_TM, body, 0, unroll=8)

    for s in range(2):
        @pl.when(i % 2 == s)
        def _(s=s):
            pltpu.make_async_copy(yb_hbm.at[pl.ds(0, TOP_K * CB_TM * ROW_SUB)], ybufs[s], sem.at[s]).wait()

            @pl.when(i < last)
            def _():
                for k in range(TOP_K):
                    for r in range(CB_TM):
                        row_copy(dnxt_ref[0, k, r], k, r, 1 - s).start(priority=r % 2)

            parts = [_unpack_rows(_load_packed(ybufs[s], k * CB_TM, CB_TM)) for k in range(TOP_K)]
            y = (parts[0] + parts[1]) + (parts[2] + parts[3])
            out_ref[...] = _ln(DN_ALPHA * x1_ref[...] + g2_ref[0] * y) * lng_ref[...] + lnb_ref[...]


def _combine(dest3, yb, x1, g2, ln2g, ln2b, seq):
    t = x1.shape[0]
    nb = t // CB_TM
    per_b = seq // CB_TM
    return pl.pallas_call(
        _combine_kernel,
        grid=(nb,),
        in_specs=[pl.BlockSpec((1, TOP_K, CB_TM), lambda i: (i, 0, 0), memory_space=pltpu.SMEM),
                  pl.BlockSpec((1, TOP_K, CB_TM), lambda i: (jnp.minimum(i + 1, nb - 1), 0, 0),
                               memory_space=pltpu.SMEM),
                  pl.BlockSpec(memory_space=pl.ANY),
                  pl.BlockSpec((CB_TM, D_MODEL), lambda i: (i, 0)),
                  pl.BlockSpec((1, 1, D_MODEL), lambda i: (i // per_b, 0, 0)),
                  pl.BlockSpec((1, D_MODEL), lambda i: (0, 0)),
                  pl.BlockSpec((1, D_MODEL), lambda i: (0, 0))],
        out_specs=pl.BlockSpec((CB_TM, D_MODEL), lambda i: (i, 0)),
        out_shape=jax.ShapeDtypeStruct((t, D_MODEL), f32),
        scratch_shapes=[pltpu.VMEM((TOP_K * CB_TM * ROW_SUB, LANES), i32),
                        pltpu.VMEM((TOP_K * CB_TM * ROW_SUB, LANES), i32),
                        pltpu.SemaphoreType.DMA((2,))],
        compiler_params=_params(("arbitrary",)),
        name="combine",
    )(dest3, dest3, yb, x1, g2, ln2g, ln2b)


def _t5_bucket(dist):
    d = dist.astype(f32)
    large = REL_MAX_EXACT + jnp.log(jnp.maximum(d, float(REL_MAX_EXACT)) / REL_MAX_EXACT) / math.log(
        REL_MAX_DIST / REL_MAX_EXACT) * (REL_BUCKETS - REL_MAX_EXACT)
    large = jnp.minimum(large.astype(i32), REL_BUCKETS - 1)
    return jnp.where(dist < REL_MAX_EXACT, dist, large)


def _bias_indices():
    qi = jnp.arange(ATT_BLOCK)[:, None]
    ki = jnp.arange(2 * ATT_BLOCK)[None, :]
    didx = qi + ATT_BLOCK - ki
    buckets, bands = [], []
    for win, dil in DIL_PAIRS:
        buckets.append(_t5_bucket(jnp.clip(didx, 0, None) * dil))
        bands.append(((didx >= 0) & (didx <= win // dil)).astype(i32))
    return jnp.stack(buckets).astype(i32), jnp.stack(bands)


def _residue_perm(tm, dil):
    n = tm // dil
    dst = jnp.arange(tm)
    src = (dst % n) * dil + dst // n
    return (src[:, None] == jnp.arange(tm)[None, :]).astype(bf16)


def kernel(x, c, w_ada, b_ada, w_in, gm_ln_g, gm_ln_b, gm_w_s, gm_b_s, w_branch_a, w_branch_b, w_out,
           rel_bias, ln1_g, ln1_b, w_router, b_router, w_gate, b_gate, w_up, b_up, w_down, b_down,
           ln2_g, ln2_b):
    batch, seq, _ = x.shape
    t = batch * seq
    l = 0
    x2 = x.reshape(t, D_MODEL)

    c8 = jnp.pad(c, ((0, 8 - batch), (0, 0)))
    mod = _adaln(c8, w_ada[l], b_ada[l][None, :])[:batch]
    sh1, sc1, g1, sh2, sc2, g2 = [m[:, None, :] for m in jnp.split(mod, 6, axis=-1)]

    perms = [_residue_perm(IN_TM, dil) for _win, dil in DIL_PAIRS]
    uv, gates, *qkvs = _inproj(x2, sc1, sh1, w_in[l].astype(bf16), perms, batch, seq)

    bs_full = jnp.repeat(gm_b_s[l].T, GM_WIDTH // GM_GROUPS, axis=1)
    ya = _gmlp(uv, gm_ln_g[l][None, :], gm_ln_b[l][None, :], gm_w_s[l], bs_full)

    bucket, band = _bias_indices()
    bias = _relbias(rel_bias, bucket, band)
    os_, ls_ = [], []
    for g, (_win, dil) in enumerate(DIL_PAIRS):
        o, lse = _attn_group(qkvs[g], bias, g, dil, batch, seq)
        os_.append(o)
        ls_.append(lse)

    wr = jnp.pad(w_router[l], ((0, 0), (0, LANES - N_EXPERTS)))
    wr_hi = wr.astype(bf16)
    wr_lo = (wr - wr_hi.astype(f32)).astype(bf16)
    br = jnp.pad(b_router[l], (0, LANES - N_EXPERTS))[None, :]
    tri = (jnp.arange(MIX_TM)[None, :] < jnp.arange(MIX_TM)[:, None]).astype(bf16)
    perms_t = [_residue_perm(MIX_TM, dil).T for _win, dil in DIL_PAIRS]
    expand = (jnp.arange(LANES)[:, None] == jnp.arange(ATT_WIDTH)[None, :] // HEAD_DIM).astype(bf16)
    x1, h2, route, rw, cnt = _mix(
        os_, ls_, perms_t, expand, ya, gates, x2, g1, sc2, sh2,
        w_branch_a[l].astype(bf16), w_branch_b[l].astype(bf16), w_out[l].astype(bf16),
        ln1_g[l][None, :], ln1_b[l][None, :], wr_hi, wr_lo, br, tri, seq)

    top_e = route[:, :TOP_K]
    rank = route[:, TOP_K:2 * TOP_K]
    top_w = rw[:, :TOP_K]
    counts = cnt[0, :N_EXPERTS].astype(i32)
    pcounts = (counts + MOE_TM - 1) // MOE_TM * MOE_TM
    pends = jnp.cumsum(pcounts)
    pstarts = pends - pcounts
    dest = pstarts[top_e] + rank
    a_total = t * TOP_K
    ntile = a_total // MOE_TM + N_EXPERTS
    r_total = ntile * MOE_TM
    inv = jnp.full((r_total,), -1, i32).at[dest.reshape(-1)].set(
        jnp.arange(a_total, dtype=i32), unique_indices=True, mode='promise_in_bounds')
    valid = inv >= 0
    safe = jnp.maximum(inv, 0)
    row_tok = jnp.where(valid, safe // TOP_K, jnp.arange(r_total, dtype=i32) % t)
    row_w = jnp.where(valid, top_w.reshape(-1)[safe], 0.0)
    n_used = pends[-1] // MOE_TM
    tile_idx = jnp.minimum(jnp.arange(ntile, dtype=i32), n_used - 1)
    tile_e = jnp.sum((pends[None, :] <= (tile_idx * MOE_TM)[:, None]).astype(i32), axis=1)
    tile_first = jnp.concatenate([jnp.ones((1,), i32), (tile_e[1:] != tile_e[:-1]).astype(i32)])
    experts = jnp.arange(N_EXPERTS, dtype=i32)
    nonempty = counts > 0
    later = lax.cummin(jnp.where(nonempty, experts, N_EXPERTS), reverse=True)
    next_nonempty = jnp.concatenate([later[1:], jnp.full((1,), N_EXPERTS, i32)])
    next_nonempty = jnp.where(next_nonempty >= N_EXPERTS, -1, next_nonempty)
    expert_slot = (jnp.cumsum(nonempty.astype(i32)) - 1) % 2

    yb = _moe(tile_e, tile_first, next_nonempty[tile_e], expert_slot[tile_e], n_used.reshape(1),
              row_tok.reshape(ntile, 1, MOE_TM), h2, row_w[:, None],
              w_gate[l], b_gate[l][:, None, :], w_up[l], b_up[l][:, None, :],
              w_down[l], b_down[l][:, None, :])

    dest3 = dest.reshape(t // CB_TM, CB_TM, TOP_K).transpose(0, 2, 1)
    out = _combine(dest3, yb, x1, g2, ln2_g[l][None, :], ln2_b[l][None, :], seq)
    return out.reshape(batch, seq, D_MODEL)
```

```python
import functools
import math

import jax
import jax.numpy as jnp
from jax import lax
from jax.experimental import pallas as pl
from jax.experimental.pallas import tpu as pltpu

f32 = jnp.float32
bf16 = jnp.bfloat16
i32 = jnp.int32

D_MODEL = 1024
GM_WIDTH = 512
GM_GROUPS = 8
GM_CHUNK = 128
DIL_PAIRS = ((128, 1), (512, 4), (2048, 16))
N_DIL = 3
HEADS_PER_GROUP = 8
HEAD_DIM = 64
ATT_WIDTH = 512
ATT_BLOCK = 128
NEG_INF = -1e30
REL_BUCKETS = 32
REL_MAX_EXACT = 16
REL_MAX_DIST = 2048
N_EXPERTS = 32
TOP_K = 4
SWIGLU_LIMIT = 7.0
SWIGLU_ALPHA = 1.702
MOE_BLOCK = 128
DEPTH = 1
DN_ALPHA = (2 * DEPTH) ** 0.25
LN_EPS = 1e-5
UV_COLS = 2 * GM_WIDTH
QKV_COLS = N_DIL * 3 * ATT_WIDTH
GATE_COLS = 2 * D_MODEL
IN_COLS = UV_COLS + QKV_COLS + GATE_COLS

LANES = 128
SUBLANES = 8
VMEM_LIMIT = 56 * 1024 * 1024


def _ln(x):
    mu = jnp.mean(x, axis=-1, keepdims=True)
    xc = x - mu
    var = jnp.mean(xc * xc, axis=-1, keepdims=True)
    return xc * lax.rsqrt(var + LN_EPS)


def _params(sem, vmem=VMEM_LIMIT):
    return pltpu.CompilerParams(dimension_semantics=sem, vmem_limit_bytes=vmem)


def _adaln_kernel(c_ref, w_ref, b_ref, o_ref):
    c = c_ref[...]
    s = c * jax.nn.sigmoid(c)
    o_ref[...] = jnp.dot(s, w_ref[...], preferred_element_type=f32,
                         precision=lax.Precision.HIGHEST) + b_ref[...]


def _adaln(c8, w_ada, b_ada):
    n = w_ada.shape[1] // D_MODEL
    return pl.pallas_call(
        _adaln_kernel,
        grid=(n,),
        in_specs=[pl.BlockSpec((8, D_MODEL), lambda j: (0, 0)),
                  pl.BlockSpec((D_MODEL, D_MODEL), lambda j: (0, j)),
                  pl.BlockSpec((1, D_MODEL), lambda j: (0, j))],
        out_specs=pl.BlockSpec((8, D_MODEL), lambda j: (0, j)),
        out_shape=jax.ShapeDtypeStruct((8, w_ada.shape[1]), f32),
        compiler_params=_params(("arbitrary",)),
        name="adaln",
    )(c8, w_ada, b_ada)


IN_TM = 256
IN_CW = 512
GRP_COLS = 3 * ATT_WIDTH


def _inproj_kernel(x_ref, sc_ref, sh_ref, w_ref, p1_ref, p2_ref,
                   uv_ref, gt_ref, qkv0_ref, qkv1_ref, qkv2_ref):
    xn = _ln(x_ref[...])
    h = (xn * (1.0 + sc_ref[0]) + sh_ref[0]).astype(bf16)
    hp = [h,
          jnp.dot(p1_ref[...], h, preferred_element_type=f32).astype(bf16),
          jnp.dot(p2_ref[...], h, preferred_element_type=f32).astype(bf16)]
    for c0 in range(0, UV_COLS, IN_CW):
        acc = jnp.dot(h, w_ref[:, c0:c0 + IN_CW], preferred_element_type=f32)
        uv_ref[:, c0:c0 + IN_CW] = jax.nn.gelu(acc).astype(bf16)
    for g, (qref, (_win, dil)) in enumerate(zip((qkv0_ref, qkv1_ref, qkv2_ref), DIL_PAIRS)):
        n = IN_TM // dil
        for q0 in range(0, GRP_COLS, IN_CW):
            c0 = UV_COLS + g * GRP_COLS + q0
            acc = jnp.dot(hp[g], w_ref[:, c0:c0 + IN_CW], preferred_element_type=f32).astype(bf16)
            for rho in range(dil):
                qref[0, rho, :, q0:q0 + IN_CW] = acc[rho * n:(rho + 1) * n, :]
    for g0 in range(0, GATE_COLS, IN_CW):
        c0 = UV_COLS + QKV_COLS + g0
        acc = jnp.dot(h, w_ref[:, c0:c0 + IN_CW], preferred_element_type=f32)
        gt_ref[:, g0:g0 + IN_CW] = jax.nn.sigmoid(acc).astype(bf16)


def _inproj(x2, sc1, sh1, w_in_bf, perms, batch, seq):
    t = x2.shape[0]
    per_b = seq // IN_TM
    qkv_specs, qkv_shapes = [], []
    for _win, dil in DIL_PAIRS:
        n = IN_TM // dil
        qkv_specs.append(pl.BlockSpec((1, dil, n, GRP_COLS), lambda i: (i // per_b, 0, i % per_b, 0)))
        qkv_shapes.append(jax.ShapeDtypeStruct((batch, dil, seq // dil, GRP_COLS), bf16))
    return pl.pallas_call(
        _inproj_kernel,
        grid=(t // IN_TM,),
        in_specs=[pl.BlockSpec((IN_TM, D_MODEL), lambda i: (i, 0)),
                  pl.BlockSpec((1, 1, D_MODEL), lambda i: (i // per_b, 0, 0)),
                  pl.BlockSpec((1, 1, D_MODEL), lambda i: (i // per_b, 0, 0)),
                  pl.BlockSpec((D_MODEL, IN_COLS), lambda i: (0, 0)),
                  pl.BlockSpec((IN_TM, IN_TM), lambda i: (0, 0)),
                  pl.BlockSpec((IN_TM, IN_TM), lambda i: (0, 0))],
        out_specs=[pl.BlockSpec((IN_TM, UV_COLS), lambda i: (i, 0)),
                   pl.BlockSpec((IN_TM, GATE_COLS), lambda i: (i, 0))] + qkv_specs,
        out_shape=[jax.ShapeDtypeStruct((t, UV_COLS), bf16),
                   jax.ShapeDtypeStruct((t, GATE_COLS), bf16)] + qkv_shapes,
        compiler_params=_params(("arbitrary",)),
        name="inproj",
    )(x2, sc1, sh1, w_in_bf, perms[1], perms[2])


GM_TM = 512


def _gmlp_kernel(u_ref, v_ref, g_ref, b_ref, ws_ref, bs_ref, ya_ref):
    row = lax.broadcasted_iota(i32, (GM_CHUNK, GM_CHUNK), 0)
    col = lax.broadcasted_iota(i32, (GM_CHUNK, GM_CHUNK), 1)
    causal = col <= row
    first_half = lax.broadcasted_iota(i32, (GM_CHUNK, LANES), 1) < (GM_WIDTH // GM_GROUPS)
    ws = [jnp.where(causal, ws_ref[g], 0.0).astype(bf16) for g in range(GM_GROUPS)]
    for ch in range(GM_TM // GM_CHUNK):
        r0 = ch * GM_CHUNK
        vn = _ln(v_ref[r0:r0 + GM_CHUNK, :].astype(f32)) * g_ref[...] + b_ref[...]
        vn = vn.astype(bf16)
        for j in range(GM_WIDTH // LANES):
            slab = vn[:, j * LANES:(j + 1) * LANES]
            s_lo = jnp.dot(ws[2 * j], slab, preferred_element_type=f32)
            s_hi = jnp.dot(ws[2 * j + 1], slab, preferred_element_type=f32)
            s = jnp.where(first_half, s_lo, s_hi) + bs_ref[:, j * LANES:(j + 1) * LANES]
            u = u_ref[r0:r0 + GM_CHUNK, j * LANES:(j + 1) * LANES].astype(f32)
            ya_ref[r0:r0 + GM_CHUNK, j * LANES:(j + 1) * LANES] = (u * s).astype(bf16)


def _gmlp(uv, ln_g, ln_b, w_s, bs_full):
    t = uv.shape[0]
    return pl.pallas_call(
        _gmlp_kernel,
        grid=(t // GM_TM,),
        in_specs=[pl.BlockSpec((GM_TM, GM_WIDTH), lambda i: (i, 0)),
                  pl.BlockSpec((GM_TM, GM_WIDTH), lambda i: (i, 1)),
                  pl.BlockSpec((1, GM_WIDTH), lambda i: (0, 0)),
                  pl.BlockSpec((1, GM_WIDTH), lambda i: (0, 0)),
                  pl.BlockSpec((GM_GROUPS, GM_CHUNK, GM_CHUNK), lambda i: (0, 0, 0)),
                  pl.BlockSpec((GM_CHUNK, GM_WIDTH), lambda i: (0, 0))],
        out_specs=pl.BlockSpec((GM_TM, GM_WIDTH), lambda i: (i, 0)),
        out_shape=jax.ShapeDtypeStruct((t, GM_WIDTH), bf16),
        compiler_params=_params(("arbitrary",)),
        name="gmlp",
    )(uv, uv, ln_g, ln_b, w_s, bs_full)


def _relbias_kernel(tab_ref, bucket_ref, band_ref, out_ref):
    g = pl.program_id(0)
    bk = bucket_ref[0]
    band = band_ref[0] > 0
    for h in range(HEADS_PER_GROUP):
        acc = jnp.zeros((ATT_BLOCK, 2 * ATT_BLOCK), f32)
        for b in range(REL_BUCKETS):
            acc = jnp.where(bk == b, tab_ref[b, g * HEADS_PER_GROUP + h], acc)
        out_ref[0, h] = jnp.where(band, acc, NEG_INF)


def _relbias(rel_bias, bucket, band):
    return pl.pallas_call(
        _relbias_kernel,
        grid=(N_DIL,),
        in_specs=[pl.BlockSpec(memory_space=pltpu.SMEM),
                  pl.BlockSpec((1, ATT_BLOCK, 2 * ATT_BLOCK), lambda g: (g, 0, 0)),
                  pl.BlockSpec((1, ATT_BLOCK, 2 * ATT_BLOCK), lambda g: (g, 0, 0))],
        out_specs=pl.BlockSpec((1, HEADS_PER_GROUP, ATT_BLOCK, 2 * ATT_BLOCK),
                               lambda g: (g, 0, 0, 0)),
        out_shape=jax.ShapeDtypeStruct((N_DIL, HEADS_PER_GROUP, ATT_BLOCK, 2 * ATT_BLOCK), f32),
        compiler_params=_params(("arbitrary",)),
        name="relbias",
    )(rel_bias, bucket, band)


ATT_STEP_BLOCKS = 2
ATT_TM = ATT_STEP_BLOCKS * ATT_BLOCK


def _attn_kernel(q_ref, kp_ref, kc_ref, vp_ref, vc_ref, bias_ref, o_ref, lse_ref):
    first = pl.program_id(2) == 0
    lane = lax.broadcasted_iota(i32, (ATT_BLOCK, LANES), 1)
    lo_half = lane < HEAD_DIM
    nt = (((1,), (1,)), ((), ()))
    ones = jnp.ones((2 * ATT_BLOCK, LANES), bf16)
    n_slab = ATT_WIDTH // LANES
    logits, v_ext = [], []
    for i in range(ATT_STEP_BLOCKS):
        cur = slice(i * ATT_BLOCK, (i + 1) * ATT_BLOCK)
        prv = slice((i - 1) * ATT_BLOCK, i * ATT_BLOCK)
        for j in range(n_slab):
            sl = slice(j * LANES, (j + 1) * LANES)
            q = q_ref[0, 0, cur, sl] * (HEAD_DIM ** -0.5)
            k_prev = kp_ref[0, 0, :, sl] if i == 0 else kc_ref[0, 0, prv, sl]
            v_prev = vp_ref[0, 0, :, sl] if i == 0 else vc_ref[0, 0, prv, sl]
            k_cat = jnp.concatenate([k_prev, kc_ref[0, 0, cur, sl]], axis=0)
            v_cat = jnp.concatenate([v_prev, vc_ref[0, 0, cur, sl]], axis=0)
            v_ext.append(jnp.concatenate([v_cat, ones], axis=1))
            for hh in range(2):
                qm = jnp.where(lo_half if hh == 0 else jnp.logical_not(lo_half), q, 0.0).astype(bf16)
                logits.append(lax.dot_general(qm, k_cat, nt, preferred_element_type=f32))
    bias = bias_ref[0].reshape(HEADS_PER_GROUP * ATT_BLOCK, 2 * ATT_BLOCK)
    rows_per_block = HEADS_PER_GROUP * ATT_BLOCK
    lg = jnp.concatenate(logits, axis=0) + jnp.concatenate([bias] * ATT_STEP_BLOCKS, axis=0)
    row = lax.broadcasted_iota(i32, lg.shape, 0)
    col = lax.broadcasted_iota(i32, lg.shape, 1)
    no_prev = jnp.logical_and(first, jnp.logical_and(row < rows_per_block, col < ATT_BLOCK))
    lg = jnp.where(no_prev, NEG_INF, lg)
    m = jnp.max(lg, axis=-1, keepdims=True)
    p = jnp.exp(lg - m).astype(bf16)
    for i in range(ATT_STEP_BLOCKS):
        cur = slice(i * ATT_BLOCK, (i + 1) * ATT_BLOCK)
        lse_tile = jnp.zeros((ATT_BLOCK, LANES), f32)
        for j in range(n_slab):
            outs = []
            for hh in range(2):
                h = 2 * j + hh
                r0 = i * rows_per_block + h * ATT_BLOCK
                r = jnp.dot(p[r0:r0 + ATT_BLOCK], v_ext[i * n_slab + j], preferred_element_type=f32)
                den = r[:, LANES:]
                outs.append(r[:, :LANES] * (1.0 / den))
                lse_h = m[r0:r0 + ATT_BLOCK] + jnp.log(den)
                lse_tile = jnp.where(lane == h, lse_h, lse_tile)
            o_ref[0, 0, cur, j * LANES:(j + 1) * LANES] = jnp.where(lo_half, outs[0], outs[1]).astype(bf16)
        lse_ref[0, 0, cur, :] = lse_tile


def _attn_group(qkv_g, bias, g, dil, batch, seq):
    l = seq // dil
    nsteps = l // ATT_TM

    def cur(cb):
        return pl.BlockSpec((1, 1, ATT_TM, ATT_WIDTH), lambda b, r, n: (b, r, n, cb))

    def prev(cb):
        return pl.BlockSpec((1, 1, ATT_BLOCK, ATT_WIDTH),
                            lambda b, r, n: (b, r, jnp.maximum(n * ATT_STEP_BLOCKS - 1, 0), cb))

    return pl.pallas_call(
        _attn_kernel,
        grid=(batch, dil, nsteps),
        in_specs=[cur(0), prev(1), cur(1), prev(2), cur(2),
                  pl.BlockSpec((1, HEADS_PER_GROUP, ATT_BLOCK, 2 * ATT_BLOCK),
                               lambda b, r, n: (g, 0, 0, 0))],
        out_specs=[pl.BlockSpec((1, 1, ATT_TM, ATT_WIDTH), lambda b, r, n: (b, r, n, 0)),
                   pl.BlockSpec((1, 1, ATT_TM, LANES), lambda b, r, n: (b, r, n, 0))],
        out_shape=[jax.ShapeDtypeStruct((batch, dil, l, ATT_WIDTH), bf16),
                   jax.ShapeDtypeStruct((batch, dil, l, LANES), f32)],
        compiler_params=_params(("arbitrary", "arbitrary", "arbitrary")),
        name=f"attn_g{g}",
    )(qkv_g, qkv_g, qkv_g, qkv_g, qkv_g, bias)


ROW_WORDS = D_MODEL // 2
ROW_SUB = ROW_WORDS // LANES
HI_MASK = -65536


def _pack_rows(x):
    bits = lax.bitcast_convert_type(x.astype(bf16).astype(f32), i32)
    return lax.shift_right_logical(bits[:, :ROW_WORDS], 16) | (bits[:, ROW_WORDS:] & HI_MASK)


def _unpack_rows(words):
    lo = lax.bitcast_convert_type(lax.shift_left(words, 16), f32)
    hi = lax.bitcast_convert_type(words & HI_MASK, f32)
    return jnp.concatenate([lo, hi], axis=1)


def _store_packed(ref, words, n, first_row=0):
    for r in range(ROW_SUB):
        ref[pl.ds(first_row * ROW_SUB + r, n, stride=ROW_SUB), :] = words[:, r * LANES:(r + 1) * LANES]


def _load_packed(ref, first_row, n):
    return jnp.concatenate([ref[pl.ds(first_row * ROW_SUB + r, n, stride=ROW_SUB), :] for r in range(ROW_SUB)],
                           axis=1)


MIX_TM = 256
MIX_SUB = 128


def _split_bf16(x, parts):
    out = []
    for _ in range(parts):
        hi = x.astype(bf16)
        out.append(hi)
        x = x - hi.astype(f32)
    return out


def _mix_kernel(o0_ref, o1_ref, o2_ref, l0_ref, l1_ref, l2_ref, pt1_ref, pt2_ref, ex_ref,
                ya_ref, gt_ref, x_ref,
                g1_ref, sc2_ref, sh2_ref, wa_ref, wb_ref, wo_ref, ln1g_ref, ln1b_ref,
                wrh_ref, wrl_ref, br_ref, tri_ref,
                x1_ref, h2_ref, route_ref, rw_ref, cnt_ref, run_ref):
    @pl.when(pl.program_id(0) == 0)
    def _():
        run_ref[...] = jnp.zeros_like(run_ref)

    lane = lax.broadcasted_iota(i32, (MIX_SUB, LANES), 1)
    lane_f = lane.astype(f32)

    def token_rows(r0):
        rows = slice(r0, r0 + MIX_SUB)
        os_, ls_ = [o0_ref[0, 0, rows, :].astype(f32)], [l0_ref[0, 0, rows, :]]
        for o_ref, l_ref, pt_ref in ((o1_ref, l1_ref, pt1_ref), (o2_ref, l2_ref, pt2_ref)):
            pt = pt_ref[rows, :]
            os_.append(jnp.dot(pt, o_ref[0].reshape(MIX_TM, ATT_WIDTH), preferred_element_type=f32))
            parts = [jnp.dot(pt, part, preferred_element_type=f32)
                     for part in _split_bf16(l_ref[0].reshape(MIX_TM, LANES), 3)]
            ls_.append((parts[0] + parts[1]) + parts[2])
        lm = jnp.maximum(jnp.maximum(ls_[0], ls_[1]), ls_[2])
        es = [jnp.exp(lse - lm) for lse in ls_]
        inv = 1.0 / (es[0] + es[1] + es[2])
        yb = jnp.zeros((MIX_SUB, ATT_WIDTH), f32)
        for e, o in zip(es, os_):
            w_hi, w_lo = _split_bf16(e * inv, 2)
            w_full = (jnp.dot(w_hi, ex_ref[...], preferred_element_type=f32)
                      + jnp.dot(w_lo, ex_ref[...], preferred_element_type=f32))
            yb = yb + w_full * o
        a = jnp.dot(ya_ref[rows, :], wa_ref[...], preferred_element_type=f32)
        b = jnp.dot(yb.astype(bf16), wb_ref[...], preferred_element_type=f32)
        merged = gt_ref[rows, :D_MODEL].astype(f32) * a + gt_ref[rows, D_MODEL:].astype(f32) * b
        mix = jnp.dot(merged.astype(bf16), wo_ref[...], preferred_element_type=f32)
        x1 = _ln(DN_ALPHA * x_ref[rows, :] + g1_ref[0] * mix) * ln1g_ref[...] + ln1b_ref[...]
        x1_ref[rows, :] = x1
        h2 = _ln(x1) * (1.0 + sc2_ref[0]) + sh2_ref[0]
        _store_packed(h2_ref, _pack_rows(h2), MIX_SUB, r0)

        h_hi, h_lo = _split_bf16(h2, 2)
        logits = (jnp.dot(h_hi, wrh_ref[...], preferred_element_type=f32)
                  + (jnp.dot(h_hi, wrl_ref[...], preferred_element_type=f32)
                     + jnp.dot(h_lo, wrh_ref[...], preferred_element_type=f32))) + br_ref[...]
        logits = jnp.where(lane < N_EXPERTS, logits, -jnp.inf)
        vals, idxs = [], []
        for _k in range(TOP_K):
            m = jnp.max(logits, axis=-1, keepdims=True)
            idx = jnp.min(jnp.where(logits == m, lane_f, float(LANES)), axis=-1, keepdims=True).astype(i32)
            vals.append(m)
            idxs.append(idx)
            logits = jnp.where(lane == idx, -jnp.inf, logits)
        exps = [jnp.exp(v - vals[0]) for v in vals]
        den = exps[0] + exps[1] + exps[2] + exps[3]
        return idxs, [e / den for e in exps]

    subs = [token_rows(r0) for r0 in range(0, MIX_TM, MIX_SUB)]
    idxs = [jnp.concatenate([sub[0][k] for sub in subs], axis=0) for k in range(TOP_K)]
    wts = [jnp.concatenate([sub[1][k] for sub in subs], axis=0) for k in range(TOP_K)]

    lane = lax.broadcasted_iota(i32, (MIX_TM, LANES), 1)
    hits = [lane == idx for idx in idxs]
    onehot = jnp.zeros((MIX_TM, LANES), f32)
    for hit in hits:
        onehot = onehot + jnp.where(hit, 1.0, 0.0)
    prefix = jnp.dot(tri_ref[...], onehot.astype(bf16), preferred_element_type=f32) + run_ref[...]
    route = jnp.zeros((MIX_TM, LANES), i32)
    rw = jnp.zeros((MIX_TM, LANES), f32)
    for k in range(TOP_K):
        rank = jnp.sum(jnp.where(hits[k], prefix, 0.0), axis=-1, keepdims=True).astype(i32)
        route = jnp.where(lane == k, idxs[k], route)
        route = jnp.where(lane == TOP_K + k, rank, route)
        rw = jnp.where(lane == k, wts[k], rw)
    route_ref[...] = route
    rw_ref[...] = rw
    run = run_ref[...] + jnp.sum(onehot, axis=0, keepdims=True)
    run_ref[...] = run
    cnt_ref[...] = jnp.broadcast_to(run, cnt_ref.shape)


def _mix(os_, ls_, perms_t, expand, ya, gates, x2, g1, sc2, sh2, wa, wb, wo, ln1g, ln1b, wr_hi, wr_lo, br, tri,
         seq):
    t = x2.shape[0]
    per_b = seq // MIX_TM
    row = lambda w: pl.BlockSpec((MIX_TM, w), lambda i: (i, 0))
    const = lambda s: pl.BlockSpec(s, lambda i: tuple(0 for _ in s))
    modb = pl.BlockSpec((1, 1, D_MODEL), lambda i: (i // per_b, 0, 0))
    grp = lambda w: [pl.BlockSpec((1, dil, MIX_TM // dil, w), lambda i: (i // per_b, 0, i % per_b, 0))
                     for _win, dil in DIL_PAIRS]
    return pl.pallas_call(
        _mix_kernel,
        grid=(t // MIX_TM,),
        in_specs=grp(ATT_WIDTH) + grp(LANES) + [
                  const((MIX_TM, MIX_TM)), const((MIX_TM, MIX_TM)), const((LANES, ATT_WIDTH)),
                  row(GM_WIDTH), row(GATE_COLS), row(D_MODEL),
                  modb, modb, modb,
                  const((GM_WIDTH, D_MODEL)), const((ATT_WIDTH, D_MODEL)), const((D_MODEL, D_MODEL)),
                  const((1, D_MODEL)), const((1, D_MODEL)),
                  const((D_MODEL, LANES)), const((D_MODEL, LANES)), const((1, LANES)),
                  const((MIX_TM, MIX_TM))],
        out_specs=[row(D_MODEL), pl.BlockSpec((MIX_TM * ROW_SUB, LANES), lambda i: (i, 0)),
                   row(LANES), row(LANES), const((8, LANES))],
        out_shape=[jax.ShapeDtypeStruct((t, D_MODEL), f32),
                   jax.ShapeDtypeStruct((t * ROW_SUB, LANES), i32),
                   jax.ShapeDtypeStruct((t, LANES), i32),
                   jax.ShapeDtypeStruct((t, LANES), f32),
                   jax.ShapeDtypeStruct((8, LANES), f32)],
        scratch_shapes=[pltpu.VMEM((1, LANES), f32)],
        compiler_params=_params(("arbitrary",)),
        name="mix",
    )(*os_, *ls_, perms_t[1], perms_t[2], expand, ya, gates, x2, g1, sc2, sh2, wa, wb, wo,
      ln1g, ln1b, wr_hi, wr_lo, br, tri)


MOE_TM = 256


def _moe_kernel(te_ref, first_ref, nexte_ref, wslot_ref, nused_ref,
                tok_cur_ref, tok_nxt_ref, h2_hbm, rww_ref, wg_hbm, wu_hbm, wd_hbm,
                bg_ref, bu_ref, bd_ref,
                out_ref, xbuf0, xbuf1, wbuf, wgb, wub, wdb, sem_x, sem_w):
    j = pl.program_id(0)
    last = pl.num_programs(0) - 1
    xbufs = (xbuf0, xbuf1)

    def row_copy(tok, i, s):
        return pltpu.make_async_copy(h2_hbm.at[pl.ds(pl.multiple_of(tok * ROW_SUB, ROW_SUB), ROW_SUB)],
                                     xbufs[s].at[pl.ds(pl.multiple_of(i * ROW_SUB, ROW_SUB), ROW_SUB)],
                                     sem_x.at[s])

    def rows_wait(s):
        pltpu.make_async_copy(h2_hbm.at[pl.ds(0, MOE_TM * ROW_SUB)], xbufs[s], sem_x.at[s]).wait()

    def weight_copies(e, ws):
        return [pltpu.make_async_copy(w.at[e], wbuf.at[ws, k], sem_w.at[ws])
                for k, w in enumerate((wg_hbm, wu_hbm, wd_hbm))]

    def gather_loop(tok_ref, s):
        def body(i, c):
            for p in range(2):
                row_copy(tok_ref[0, 0, 2 * i + p], 2 * i + p, s).start(priority=p)
            return c
        lax.fori_loop(0, MOE_TM // 2, body, 0, unroll=4)

    @pl.when(j == 0)
    def _():
        gather_loop(tok_cur_ref, 0)
        for cp in weight_copies(te_ref[0], wslot_ref[0]):
            cp.start(priority=1)

    def load_weights():
        ws = wslot_ref[j]
        for cp in weight_copies(te_ref[j], ws):
            cp.wait()
        wgb[...] = wbuf[ws, 0].astype(bf16)
        wub[...] = wbuf[ws, 1].astype(bf16)
        wdb[...] = wbuf[ws, 2].astype(bf16)
        ne = nexte_ref[j]

        @pl.when(ne >= 0)
        def _():
            for cp in weight_copies(ne, 1 - ws):
                cp.start(priority=1)

    def expert_mlp(s):
        gather_loop(tok_nxt_ref, 1 - s)
        xb =_unpack_rows(_load_packed(xbufs[s], 0, MOE_TM)).astype(bf16)
        g = jnp.dot(xb, wgb[...], preferred_element_type=f32) + bg_ref[0]
        u = jnp.dot(xb, wub[...], preferred_element_type=f32) + bu_ref[0]
        g = jnp.minimum(g, SWIGLU_LIMIT)
        u = jnp.clip(u, -SWIGLU_LIMIT, SWIGLU_LIMIT)
        act = (u + 1.0) * (g * jax.nn.sigmoid(SWIGLU_ALPHA * g))
        y = (jnp.dot(act.astype(bf16), wdb[...], preferred_element_type=f32) + bd_ref[0]) * rww_ref[...]
        _store_packed(out_ref, _pack_rows(y), MOE_TM)

    def idle_tile(s):
        gather_loop(tok_nxt_ref, s)
        out_ref[...] = jnp.zeros_like(out_ref)

    used = j < nused_ref[0]
    for s in range(2):
        @pl.when(j % 2 == s)
        def _(s=s):
            rows_wait(s)
            pl.when(first_ref[j] == 1)(load_weights)
            pl.when(used)(functools.partial(expert_mlp, s))
            pl.when(jnp.logical_not(used))(functools.partial(idle_tile, 1 - s))
            pl.when(j == last)(functools.partial(rows_wait, 1 - s))


def _moe(tile_e, tile_first, next_e, wslot, n_used, row_tok3, h2p, row_w,
         w_gate, b_gate, w_up, b_up, w_down, b_down):
    ntile = tile_e.shape[0]
    bspec = pl.BlockSpec((1, 1, D_MODEL), lambda j, te, *_: (te[j], 0, 0))
    hbm = pl.BlockSpec(memory_space=pl.ANY)
    grid_spec = pltpu.PrefetchScalarGridSpec(
        num_scalar_prefetch=5,
        grid=(ntile,),
        in_specs=[pl.BlockSpec((1, 1, MOE_TM), lambda j, *_: (j, 0, 0), memory_space=pltpu.SMEM),
                  pl.BlockSpec((1, 1, MOE_TM), lambda j, *_: (jnp.minimum(j + 1, ntile - 1), 0, 0),
                               memory_space=pltpu.SMEM),
                  hbm,
                  pl.BlockSpec((MOE_TM, 1), lambda j, *_: (j, 0)),
                  hbm, hbm, hbm, bspec, bspec, bspec],
        out_specs=pl.BlockSpec((MOE_TM * ROW_SUB, LANES), lambda j, *_: (j, 0)),
        scratch_shapes=[pltpu.VMEM((MOE_TM * ROW_SUB, LANES), i32),
                        pltpu.VMEM((MOE_TM * ROW_SUB, LANES), i32),
                        pltpu.VMEM((2, 3, D_MODEL, D_MODEL), f32),
                        pltpu.VMEM((D_MODEL, D_MODEL), bf16),
                        pltpu.VMEM((D_MODEL, D_MODEL), bf16),
                        pltpu.VMEM((D_MODEL, D_MODEL), bf16),
                        pltpu.SemaphoreType.DMA((2,)),
                        pltpu.SemaphoreType.DMA((2,))],
    )
    return pl.pallas_call(
        _moe_kernel,
        grid_spec=grid_spec,
        out_shape=jax.ShapeDtypeStruct((ntile * MOE_TM * ROW_SUB, LANES), i32),
        compiler_params=_params(("arbitrary",)),
        name="moe",
    )(tile_e, tile_first, next_e, wslot, n_used, row_tok3, row_tok3, h2p, row_w,
      w_gate, w_up, w_down, b_gate, b_up, b_down)


CB_TM = 128


def _combine_kernel(dcur_ref, dnxt_ref, yb_hbm, x1_ref, g2_ref, lng_ref, lnb_ref, out_ref,
                    ybuf0, ybuf1, sem):
    i = pl.program_id(0)
    last = pl.num_programs(0) - 1
    ybufs = (ybuf0, ybuf1)

    def row_copy(d, k, r, s):
        return pltpu.make_async_copy(
            yb_hbm.at[pl.ds(pl.multiple_of(d * ROW_SUB, ROW_SUB), ROW_SUB)],
            ybufs[s].at[pl.ds(pl.multiple_of((k * CB_TM + r) * ROW_SUB, ROW_SUB), ROW_SUB)],
            sem.at[s])

    @pl.when(i == 0)
    def _():
        for k in range(TOP_K):
            def body(r, c, k=k):
                row_copy(dcur_ref[0, k, r], k, r, 0).start()
                return c
            lax.fori_loop(0, CB_TM, body, 0, unroll=8)

    for s in range(2):
        @pl.when(i % 2 == s)
        def _(s=s):
            pltpu.make_async_copy(yb_hbm.at[pl.ds(0, TOP_K * CB_TM * ROW_SUB)], ybufs[s], sem.at[s]).wait()

            @pl.when(i < last)
            def _():
                for k in range(TOP_K):
                    for r in range(CB_TM):
                        row_copy(dnxt_ref[0, k, r], k, r, 1 - s).start(priority=r % 2)

            parts = [_unpack_rows(_load_packed(ybufs[s], k * CB_TM, CB_TM)) for k in range(TOP_K)]
            y = (parts[0] + parts[1]) + (parts[2] + parts[3])
            out_ref[...] = _ln(DN_ALPHA * x1_ref[...] + g2_ref[0] * y) * lng_ref[...] + lnb_ref[...]


def _combine(dest3, yb, x1, g2, ln2g, ln2b, seq):
    t = x1.shape[0]
    nb = t // CB_TM
    per_b = seq // CB_TM
    return pl.pallas_call(
        _combine_kernel,
        grid=(nb,),
        in_specs=[pl.BlockSpec((1, TOP_K, CB_TM), lambda i: (i, 0, 0), memory_space=pltpu.SMEM),
                  pl.BlockSpec((1, TOP_K, CB_TM), lambda i: (jnp.minimum(i + 1, nb - 1), 0, 0),
                               memory_space=pltpu.SMEM),
                  pl.BlockSpec(memory_space=pl.ANY),
                  pl.BlockSpec((CB_TM, D_MODEL), lambda i: (i, 0)),
                  pl.BlockSpec((1, 1, D_MODEL), lambda i: (i // per_b, 0, 0)),
                  pl.BlockSpec((1, D_MODEL), lambda i: (0, 0)),
                  pl.BlockSpec((1, D_MODEL), lambda i: (0, 0))],
        out_specs=pl.BlockSpec((CB_TM, D_MODEL), lambda i: (i, 0)),
        out_shape=jax.ShapeDtypeStruct((t, D_MODEL), f32),
        scratch_shapes=[pltpu.VMEM((TOP_K * CB_TM * ROW_SUB, LANES), i32),
                        pltpu.VMEM((TOP_K * CB_TM * ROW_SUB, LANES), i32),
                        pltpu.SemaphoreType.DMA((2,))],
        compiler_params=_params(("arbitrary",)),
        name="combine",
    )(dest3, dest3, yb, x1, g2, ln2g, ln2b)


def _t5_bucket(dist):
    d = dist.astype(f32)
    large = REL_MAX_EXACT + jnp.log(jnp.maximum(d, float(REL_MAX_EXACT)) / REL_MAX_EXACT) / math.log(
        REL_MAX_DIST / REL_MAX_EXACT) * (REL_BUCKETS - REL_MAX_EXACT)
    large = jnp.minimum(large.astype(i32), REL_BUCKETS - 1)
    return jnp.where(dist < REL_MAX_EXACT, dist, large)


def _bias_indices():
    qi = jnp.arange(ATT_BLOCK)[:, None]
    ki = jnp.arange(2 * ATT_BLOCK)[None, :]
    didx = qi + ATT_BLOCK - ki
    buckets, bands = [], []
    for win, dil in DIL_PAIRS:
        buckets.append(_t5_bucket(jnp.clip(didx, 0, None) * dil))
        bands.append(((didx >= 0) & (didx <= win // dil)).astype(i32))
    return jnp.stack(buckets).astype(i32), jnp.stack(bands)


def _residue_perm(tm, dil):
    n = tm // dil
    dst = jnp.arange(tm)
    src = (dst % n) * dil + dst // n
    return (src[:, None] == jnp.arange(tm)[None, :]).astype(bf16)


def kernel(x, c, w_ada, b_ada, w_in, gm_ln_g, gm_ln_b, gm_w_s, gm_b_s, w_branch_a, w_branch_b, w_out,
           rel_bias, ln1_g, ln1_b, w_router, b_router, w_gate, b_gate, w_up, b_up, w_down, b_down,
           ln2_g, ln2_b):
    batch, seq, _ = x.shape
    t = batch * seq
    l = 0
    x2 = x.reshape(t, D_MODEL)

    c8 = jnp.pad(c, ((0, 8 - batch), (0, 0)))
    mod = _adaln(c8, w_ada[l], b_ada[l][None, :])[:batch]
    sh1, sc1, g1, sh2, sc2, g2 = [m[:, None, :] for m in jnp.split(mod, 6, axis=-1)]

    perms = [_residue_perm(IN_TM, dil) for _win, dil in DIL_PAIRS]
    uv, gates, *qkvs = _inproj(x2, sc1, sh1, w_in[l].astype(bf16), perms, batch, seq)

    bs_full = jnp.repeat(gm_b_s[l].T, GM_WIDTH // GM_GROUPS, axis=1)
    ya = _gmlp(uv, gm_ln_g[l][None, :], gm_ln_b[l][None, :], gm_w_s[l], bs_full)

    bucket, band = _bias_indices()
    bias = _relbias(rel_bias, bucket, band)
    os_, ls_ = [], []
    for g, (_win, dil) in enumerate(DIL_PAIRS):
        o, lse = _attn_group(qkvs[g], bias, g, dil, batch, seq)
        os_.append(o)
        ls_.append(lse)

    wr = jnp.pad(w_router[l], ((0, 0), (0, LANES - N_EXPERTS)))
    wr_hi = wr.astype(bf16)
    wr_lo = (wr - wr_hi.astype(f32)).astype(bf16)
    br = jnp.pad(b_router[l], (0, LANES - N_EXPERTS))[None, :]
    tri = (jnp.arange(MIX_TM)[None, :] < jnp.arange(MIX_TM)[:, None]).astype(bf16)
    perms_t = [_residue_perm(MIX_TM, dil).T for _win, dil in DIL_PAIRS]
    expand = (jnp.arange(LANES)[:, None] == jnp.arange(ATT_WIDTH)[None, :] // HEAD_DIM).astype(bf16)
    x1, h2, route, rw, cnt = _mix(
        os_, ls_, perms_t, expand, ya, gates, x2, g1, sc2, sh2,
        w_branch_a[l].astype(bf16), w_branch_b[l].astype(bf16), w_out[l].astype(bf16),
        ln1_g[l][None, :], ln1_b[l][None, :], wr_hi, wr_lo, br, tri, seq)

    top_e = route[:, :TOP_K]
    rank = route[:, TOP_K:2 * TOP_K]
    top_w = rw[:, :TOP_K]
    counts = cnt[0, :N_EXPERTS].astype(i32)
    pcounts = (counts + MOE_TM - 1) // MOE_TM * MOE_TM
    pends = jnp.cumsum(pcounts)
    pstarts = pends - pcounts
    dest = pstarts[top_e] + rank
    a_total = t * TOP_K
    ntile = a_total // MOE_TM + N_EXPERTS
    r_total = ntile * MOE_TM
    experts = jnp.arange(N_EXPERTS, dtype=i32)
    pad_cnt = pcounts - counts
    pad_end = jnp.cumsum(pad_cnt)
    n_pad = r_total - a_total
    pad_i = jnp.arange(n_pad, dtype=i32)
    pad_e = jnp.minimum(jnp.sum((pad_end[None, :] <= pad_i[:, None]).astype(i32), axis=1), N_EXPERTS - 1)
    in_expert = pad_i < pad_end[-1]
    pad_row = jnp.where(in_expert,
                        (pstarts + counts)[pad_e] + pad_i - (pad_end - pad_cnt)[pad_e],
                        pends[-1] + pad_i - pad_end[-1])
    keys = jnp.concatenate([dest.reshape(-1), pad_row])
    vals = jnp.concatenate([jnp.arange(a_total, dtype=i32), jnp.full((n_pad,), -1, i32)])
    _, inv = lax.sort_key_val(keys, vals)
    valid = inv >= 0
    safe = jnp.maximum(inv, 0)
    row_tok = jnp.where(valid, safe // TOP_K, jnp.arange(r_total, dtype=i32) % t)
    row_w = jnp.where(valid, top_w.reshape(-1)[safe], 0.0)
    n_used = pends[-1] // MOE_TM
    tile_idx = jnp.minimum(jnp.arange(ntile, dtype=i32), n_used - 1)
    tile_e = jnp.sum((pends[None, :] <= (tile_idx * MOE_TM)[:, None]).astype(i32), axis=1)
    tile_first = jnp.concatenate([jnp.ones((1,), i32), (tile_e[1:] != tile_e[:-1]).astype(i32)])
    nonempty = counts > 0
    later = lax.cummin(jnp.where(nonempty, experts, N_EXPERTS), reverse=True)
    next_nonempty = jnp.concatenate([later[1:], jnp.full((1,), N_EXPERTS, i32)])
    next_nonempty = jnp.where(next_nonempty >= N_EXPERTS, -1, next_nonempty)
    expert_slot = (jnp.cumsum(nonempty.astype(i32)) - 1) % 2

    yb = _moe(tile_e, tile_first, next_nonempty[tile_e], expert_slot[tile_e], n_used.reshape(1),
              row_tok.reshape(ntile, 1, MOE_TM), h2, row_w[:, None],
              w_gate[l], b_gate[l][:, None, :], w_up[l], b_up[l][:, None, :],
              w_down[l], b_down[l][:, None, :])

    dest3 = dest.reshape(t // CB_TM, CB_TM, TOP_K).transpose(0, 2, 1)
    out = _combine(dest3, yb, x1, g2, ln2_g[l][None, :], ln2_b[l][None, :], seq)
    return out.reshape(batch, seq, D_MODEL)
```

```python
import functools
import math

import jax
import jax.numpy as jnp
from jax import lax
from jax.experimental import pallas as pl
from jax.experimental.pallas import tpu as pltpu

f32 = jnp.float32
bf16 = jnp.bfloat16
i32 = jnp.int32

D_MODEL = 1024
GM_WIDTH = 512
GM_GROUPS = 8
GM_CHUNK = 128
DIL_PAIRS = ((128, 1), (512, 4), (2048, 16))
N_DIL = 3
HEADS_PER_GROUP = 8
HEAD_DIM = 64
ATT_WIDTH = 512
ATT_BLOCK = 128
NEG_INF = -1e30
REL_BUCKETS = 32
REL_MAX_EXACT = 16
REL_MAX_DIST = 2048
N_EXPERTS = 32
TOP_K = 4
SWIGLU_LIMIT = 7.0
SWIGLU_ALPHA = 1.702
MOE_BLOCK = 128
DEPTH = 1
DN_ALPHA = (2 * DEPTH) ** 0.25
LN_EPS = 1e-5
UV_COLS = 2 * GM_WIDTH
QKV_COLS = N_DIL * 3 * ATT_WIDTH
GATE_COLS = 2 * D_MODEL
IN_COLS = UV_COLS + QKV_COLS + GATE_COLS

LANES = 128
SUBLANES = 8
VMEM_LIMIT = 56 * 1024 * 1024


def _ln(x):
    mu = jnp.mean(x, axis=-1, keepdims=True)
    xc = x - mu
    var = jnp.mean(xc * xc, axis=-1, keepdims=True)
    return xc * lax.rsqrt(var + LN_EPS)


def _params(sem, vmem=VMEM_LIMIT):
    return pltpu.CompilerParams(dimension_semantics=sem, vmem_limit_bytes=vmem)


def _adaln_kernel(c_ref, w_ref, b_ref, o_ref):
    c = c_ref[...]
    s = c * jax.nn.sigmoid(c)
    o_ref[...] = jnp.dot(s, w_ref[...], preferred_element_type=f32,
                         precision=lax.Precision.HIGHEST) + b_ref[...]


def _adaln(c8, w_ada, b_ada):
    n = w_ada.shape[1] // D_MODEL
    return pl.pallas_call(
        _adaln_kernel,
        grid=(n,),
        in_specs=[pl.BlockSpec((8, D_MODEL), lambda j: (0, 0)),
                  pl.BlockSpec((D_MODEL, D_MODEL), lambda j: (0, j)),
                  pl.BlockSpec((1, D_MODEL), lambda j: (0, j))],
        out_specs=pl.BlockSpec((8, D_MODEL), lambda j: (0, j)),
        out_shape=jax.ShapeDtypeStruct((8, w_ada.shape[1]), f32),
        compiler_params=_params(("arbitrary",)),
        name="adaln",
    )(c8, w_ada, b_ada)


IN_TM = 256
IN_CW = 512
GRP_COLS = 3 * ATT_WIDTH


def _inproj_kernel(x_ref, sc_ref, sh_ref, w_ref, p1_ref, p2_ref,
                   uv_ref, gt_ref, qkv0_ref, qkv1_ref, qkv2_ref):
    xn = _ln(x_ref[...])
    h = (xn * (1.0 + sc_ref[0]) + sh_ref[0]).astype(bf16)
    hp = [h,
          jnp.dot(p1_ref[...], h, preferred_element_type=f32).astype(bf16),
          jnp.dot(p2_ref[...], h, preferred_element_type=f32).astype(bf16)]
    for c0 in range(0, UV_COLS, IN_CW):
        acc = jnp.dot(h, w_ref[:, c0:c0 + IN_CW], preferred_element_type=f32)
        uv_ref[:, c0:c0 + IN_CW] = jax.nn.gelu(acc).astype(bf16)
    for g, (qref, (_win, dil)) in enumerate(zip((qkv0_ref, qkv1_ref, qkv2_ref), DIL_PAIRS)):
        n = IN_TM // dil
        for q0 in range(0, GRP_COLS, IN_CW):
            c0 = UV_COLS + g * GRP_COLS + q0
            acc = jnp.dot(hp[g], w_ref[:, c0:c0 + IN_CW], preferred_element_type=f32).astype(bf16)
            for rho in range(dil):
                qref[0, rho, :, q0:q0 + IN_CW] = acc[rho * n:(rho + 1) * n, :]
    for g0 in range(0, GATE_COLS, IN_CW):
        c0 = UV_COLS + QKV_COLS + g0
        acc = jnp.dot(h, w_ref[:, c0:c0 + IN_CW], preferred_element_type=f32)
        gt_ref[:, g0:g0 + IN_CW] = jax.nn.sigmoid(acc).astype(bf16)


def _inproj(x2, sc1, sh1, w_in_bf, perms, batch, seq):
    t = x2.shape[0]
    per_b = seq // IN_TM
    qkv_specs, qkv_shapes = [], []
    for _win, dil in DIL_PAIRS:
        n = IN_TM // dil
        qkv_specs.append(pl.BlockSpec((1, dil, n, GRP_COLS), lambda i: (i // per_b, 0, i % per_b, 0)))
        qkv_shapes.append(jax.ShapeDtypeStruct((batch, dil, seq // dil, GRP_COLS), bf16))
    return pl.pallas_call(
        _inproj_kernel,
        grid=(t // IN_TM,),
        in_specs=[pl.BlockSpec((IN_TM, D_MODEL), lambda i: (i, 0)),
                  pl.BlockSpec((1, 1, D_MODEL), lambda i: (i // per_b, 0, 0)),
                  pl.BlockSpec((1, 1, D_MODEL), lambda i: (i // per_b, 0, 0)),
                  pl.BlockSpec((D_MODEL, IN_COLS), lambda i: (0, 0)),
                  pl.BlockSpec((IN_TM, IN_TM), lambda i: (0, 0)),
                  pl.BlockSpec((IN_TM, IN_TM), lambda i: (0, 0))],
        out_specs=[pl.BlockSpec((IN_TM, UV_COLS), lambda i: (i, 0)),
                   pl.BlockSpec((IN_TM, GATE_COLS), lambda i: (i, 0))] + qkv_specs,
        out_shape=[jax.ShapeDtypeStruct((t, UV_COLS), bf16),
                   jax.ShapeDtypeStruct((t, GATE_COLS), bf16)] + qkv_shapes,
        compiler_params=_params(("arbitrary",)),
        name="inproj",
    )(x2, sc1, sh1, w_in_bf, perms[1], perms[2])


GM_TM = 512


def _gmlp_kernel(u_ref, v_ref, g_ref, b_ref, ws_ref, bs_ref, ya_ref):
    row = lax.broadcasted_iota(i32, (GM_CHUNK, GM_CHUNK), 0)
    col = lax.broadcasted_iota(i32, (GM_CHUNK, GM_CHUNK), 1)
    causal = col <= row
    first_half = lax.broadcasted_iota(i32, (GM_CHUNK, LANES), 1) < (GM_WIDTH // GM_GROUPS)
    ws = [jnp.where(causal, ws_ref[g], 0.0).astype(bf16) for g in range(GM_GROUPS)]
    for ch in range(GM_TM // GM_CHUNK):
        r0 = ch * GM_CHUNK
        vn = _ln(v_ref[r0:r0 + GM_CHUNK, :].astype(f32)) * g_ref[...] + b_ref[...]
        vn = vn.astype(bf16)
        for j in range(GM_WIDTH // LANES):
            slab = vn[:, j * LANES:(j + 1) * LANES]
            s_lo = jnp.dot(ws[2 * j], slab, preferred_element_type=f32)
            s_hi = jnp.dot(ws[2 * j + 1], slab, preferred_element_type=f32)
            s = jnp.where(first_half, s_lo, s_hi) + bs_ref[:, j * LANES:(j + 1) * LANES]
            u = u_ref[r0:r0 + GM_CHUNK, j * LANES:(j + 1) * LANES].astype(f32)
            ya_ref[r0:r0 + GM_CHUNK, j * LANES:(j + 1) * LANES] = (u * s).astype(bf16)


def _gmlp(uv, ln_g, ln_b, w_s, bs_full):
    t = uv.shape[0]
    return pl.pallas_call(
        _gmlp_kernel,
        grid=(t // GM_TM,),
        in_specs=[pl.BlockSpec((GM_TM, GM_WIDTH), lambda i: (i, 0)),
                  pl.BlockSpec((GM_TM, GM_WIDTH), lambda i: (i, 1)),
                  pl.BlockSpec((1, GM_WIDTH), lambda i: (0, 0)),
                  pl.BlockSpec((1, GM_WIDTH), lambda i: (0, 0)),
                  pl.BlockSpec((GM_GROUPS, GM_CHUNK, GM_CHUNK), lambda i: (0, 0, 0)),
                  pl.BlockSpec((GM_CHUNK, GM_WIDTH), lambda i: (0, 0))],
        out_specs=pl.BlockSpec((GM_TM, GM_WIDTH), lambda i: (i, 0)),
        out_shape=jax.ShapeDtypeStruct((t, GM_WIDTH), bf16),
        compiler_params=_params(("arbitrary",)),
        name="gmlp",
    )(uv, uv, ln_g, ln_b, w_s, bs_full)


def _relbias_kernel(tab_ref, bucket_ref, band_ref, out_ref):
    g = pl.program_id(0)
    bk = bucket_ref[0]
    band = band_ref[0] > 0
    for h in range(HEADS_PER_GROUP):
        acc = jnp.zeros((ATT_BLOCK, 2 * ATT_BLOCK), f32)
        for b in range(REL_BUCKETS):
            acc = jnp.where(bk == b, tab_ref[b, g * HEADS_PER_GROUP + h], acc)
        out_ref[0, h] = jnp.where(band, acc, NEG_INF)


def _relbias(rel_bias, bucket, band):
    return pl.pallas_call(
        _relbias_kernel,
        grid=(N_DIL,),
        in_specs=[pl.BlockSpec(memory_space=pltpu.SMEM),
                  pl.BlockSpec((1, ATT_BLOCK, 2 * ATT_BLOCK), lambda g: (g, 0, 0)),
                  pl.BlockSpec((1, ATT_BLOCK, 2 * ATT_BLOCK), lambda g: (g, 0, 0))],
        out_specs=pl.BlockSpec((1, HEADS_PER_GROUP, ATT_BLOCK, 2 * ATT_BLOCK),
                               lambda g: (g, 0, 0, 0)),
        out_shape=jax.ShapeDtypeStruct((N_DIL, HEADS_PER_GROUP, ATT_BLOCK, 2 * ATT_BLOCK), f32),
        compiler_params=_params(("arbitrary",)),
        name="relbias",
    )(rel_bias, bucket, band)


ATT_MAX_STEP_BLOCKS = 4


def _attn_kernel(nblk, q_ref, kp_ref, kc_ref, vp_ref, vc_ref, bias_ref, o_ref, lse_ref):
    first = pl.program_id(2) == 0
    lane = lax.broadcasted_iota(i32, (ATT_BLOCK, LANES), 1)
    lo_half = lane < HEAD_DIM
    nt = (((1,), (1,)), ((), ()))
    ones = jnp.ones((2 * ATT_BLOCK, LANES), bf16)
    n_slab = ATT_WIDTH // LANES
    logits, v_ext = [], []
    for i in range(nblk):
        cur = slice(i * ATT_BLOCK, (i + 1) * ATT_BLOCK)
        prv = slice((i - 1) * ATT_BLOCK, i * ATT_BLOCK)
        for j in range(n_slab):
            sl = slice(j * LANES, (j + 1) * LANES)
            q = q_ref[0, 0, cur, sl] * (HEAD_DIM ** -0.5)
            k_prev = kp_ref[0, 0, :, sl] if i == 0 else kc_ref[0, 0, prv, sl]
            v_prev = vp_ref[0, 0, :, sl] if i == 0 else vc_ref[0, 0, prv, sl]
            k_cat = jnp.concatenate([k_prev, kc_ref[0, 0, cur, sl]], axis=0)
            v_cat = jnp.concatenate([v_prev, vc_ref[0, 0, cur, sl]], axis=0)
            v_ext.append(jnp.concatenate([v_cat, ones], axis=1))
            for hh in range(2):
                qm = jnp.where(lo_half if hh == 0 else jnp.logical_not(lo_half), q, 0.0).astype(bf16)
                logits.append(lax.dot_general(qm, k_cat, nt, preferred_element_type=f32))
    bias = bias_ref[0].reshape(HEADS_PER_GROUP * ATT_BLOCK, 2 * ATT_BLOCK)
    rows_per_block = HEADS_PER_GROUP * ATT_BLOCK
    lg = jnp.concatenate(logits, axis=0) + jnp.concatenate([bias] * nblk, axis=0)
    row = lax.broadcasted_iota(i32, lg.shape, 0)
    col = lax.broadcasted_iota(i32, lg.shape, 1)
    no_prev = jnp.logical_and(first, jnp.logical_and(row < rows_per_block, col < ATT_BLOCK))
    lg = jnp.where(no_prev, NEG_INF, lg)
    m = jnp.max(lg, axis=-1, keepdims=True)
    p = jnp.exp(lg - m).astype(bf16)
    for i in range(nblk):
        cur = slice(i * ATT_BLOCK, (i + 1) * ATT_BLOCK)
        lse_tile = jnp.zeros((ATT_BLOCK, LANES), f32)
        for j in range(n_slab):
            outs = []
            for hh in range(2):
                h = 2 * j + hh
                r0 = i * rows_per_block + h * ATT_BLOCK
                r = jnp.dot(p[r0:r0 + ATT_BLOCK], v_ext[i * n_slab + j], preferred_element_type=f32)
                den = r[:, LANES:]
                outs.append(r[:, :LANES] * (1.0 / den))
                lse_h = m[r0:r0 + ATT_BLOCK] + jnp.log(den)
                lse_tile = jnp.where(lane == h, lse_h, lse_tile)
            o_ref[0, 0, cur, j * LANES:(j + 1) * LANES] = jnp.where(lo_half, outs[0], outs[1]).astype(bf16)
        lse_ref[0, 0, cur, :] = lse_tile


def _attn_group(qkv_g, bias, g, dil, batch, seq):
    l = seq // dil
    nblk = min(ATT_MAX_STEP_BLOCKS, l // ATT_BLOCK)
    tm = nblk * ATT_BLOCK
    nsteps = l // tm

    def cur(cb):
        return pl.BlockSpec((1, 1, tm, ATT_WIDTH), lambda b, r, n: (b, r, n, cb))

    def prev(cb):
        return pl.BlockSpec((1, 1, ATT_BLOCK, ATT_WIDTH),
                            lambda b, r, n: (b, r, jnp.maximum(n * nblk - 1, 0), cb))

    return pl.pallas_call(
        functools.partial(_attn_kernel, nblk),
        grid=(batch, dil, nsteps),
        in_specs=[cur(0), prev(1), cur(1), prev(2), cur(2),
                  pl.BlockSpec((1, HEADS_PER_GROUP, ATT_BLOCK, 2 * ATT_BLOCK),
                               lambda b, r, n: (g, 0, 0, 0))],
        out_specs=[pl.BlockSpec((1, 1, tm, ATT_WIDTH), lambda b, r, n: (b, r, n, 0)),
                   pl.BlockSpec((1, 1, tm, LANES), lambda b, r, n: (b, r, n, 0))],
        out_shape=[jax.ShapeDtypeStruct((batch, dil, l, ATT_WIDTH), bf16),
                   jax.ShapeDtypeStruct((batch, dil, l, LANES), f32)],
        compiler_params=_params(("arbitrary", "arbitrary", "arbitrary")),
        name=f"attn_g{g}",
    )(qkv_g, qkv_g, qkv_g, qkv_g, qkv_g, bias)


ROW_WORDS = D_MODEL // 2
ROW_SUB = ROW_WORDS // LANES
HI_MASK = -65536


def _pack_rows(x):
    bits = lax.bitcast_convert_type(x.astype(bf16).astype(f32), i32)
    return lax.shift_right_logical(bits[:, :ROW_WORDS], 16) | (bits[:, ROW_WORDS:] & HI_MASK)


def _unpack_rows(words):
    lo = lax.bitcast_convert_type(lax.shift_left(words, 16), f32)
    hi = lax.bitcast_convert_type(words & HI_MASK, f32)
    return jnp.concatenate([lo, hi], axis=1)


def _store_packed(ref, words, n, first_row=0):
    for r in range(ROW_SUB):
        ref[pl.ds(first_row * ROW_SUB + r, n, stride=ROW_SUB), :] = words[:, r * LANES:(r + 1) * LANES]


def _load_packed(ref, first_row, n):
    return jnp.concatenate([ref[pl.ds(first_row * ROW_SUB + r, n, stride=ROW_SUB), :] for r in range(ROW_SUB)],
                           axis=1)


MIX_TM = 256
MIX_SUB = 128


def _split_bf16(x, parts):
    out = []
    for _ in range(parts):
        hi = x.astype(bf16)
        out.append(hi)
        x = x - hi.astype(f32)
    return out


def _mix_kernel(o0_ref, o1_ref, o2_ref, l0_ref, l1_ref, l2_ref, pt1_ref, pt2_ref, ex_ref,
                ya_ref, gt_ref, x_ref,
                g1_ref, sc2_ref, sh2_ref, wa_ref, wb_ref, wo_ref, ln1g_ref, ln1b_ref,
                wrh_ref, wrl_ref, br_ref, tri_ref,
                x1_ref, h2_ref, route_ref, rw_ref, cnt_ref, run_ref):
    @pl.when(pl.program_id(0) == 0)
    def _():
        run_ref[...] = jnp.zeros_like(run_ref)

    lane = lax.broadcasted_iota(i32, (MIX_SUB, LANES), 1)
    lane_f = lane.astype(f32)

    def token_rows(r0):
        rows = slice(r0, r0 + MIX_SUB)
        os_, ls_ = [o0_ref[0, 0, rows, :].astype(f32)], [l0_ref[0, 0, rows, :]]
        for o_ref, l_ref, pt_ref in ((o1_ref, l1_ref, pt1_ref), (o2_ref, l2_ref, pt2_ref)):
            pt = pt_ref[rows, :]
            os_.append(jnp.dot(pt, o_ref[0].reshape(MIX_TM, ATT_WIDTH), preferred_element_type=f32))
            parts = [jnp.dot(pt, part, preferred_element_type=f32)
                     for part in _split_bf16(l_ref[0].reshape(MIX_TM, LANES), 3)]
            ls_.append((parts[0] + parts[1]) + parts[2])
        lm = jnp.maximum(jnp.maximum(ls_[0], ls_[1]), ls_[2])
        es = [jnp.exp(lse - lm) for lse in ls_]
        inv = 1.0 / (es[0] + es[1] + es[2])
        yb = jnp.zeros((MIX_SUB, ATT_WIDTH), f32)
        for e, o in zip(es, os_):
            w_hi, w_lo = _split_bf16(e * inv, 2)
            w_full = (jnp.dot(w_hi, ex_ref[...], preferred_element_type=f32)
                      + jnp.dot(w_lo, ex_ref[...], preferred_element_type=f32))
            yb = yb + w_full * o
        a = jnp.dot(ya_ref[rows, :], wa_ref[...], preferred_element_type=f32)
        b = jnp.dot(yb.astype(bf16), wb_ref[...], preferred_element_type=f32)
        merged = gt_ref[rows, :D_MODEL].astype(f32) * a + gt_ref[rows, D_MODEL:].astype(f32) * b
        mix = jnp.dot(merged.astype(bf16), wo_ref[...], preferred_element_type=f32)
        x1 = _ln(DN_ALPHA * x_ref[rows, :] + g1_ref[0] * mix) * ln1g_ref[...] + ln1b_ref[...]
        x1_ref[rows, :] = x1
        h2 = _ln(x1) * (1.0 + sc2_ref[0]) + sh2_ref[0]
        _store_packed(h2_ref, _pack_rows(h2), MIX_SUB, r0)

        h_hi, h_lo = _split_bf16(h2, 2)
        logits = (jnp.dot(h_hi, wrh_ref[...], preferred_element_type=f32)
                  + (jnp.dot(h_hi, wrl_ref[...], preferred_element_type=f32)
                     + jnp.dot(h_lo, wrh_ref[...], preferred_element_type=f32))) + br_ref[...]
        logits = jnp.where(lane < N_EXPERTS, logits, -jnp.inf)
        vals, idxs = [], []
        for _k in range(TOP_K):
            m = jnp.max(logits, axis=-1, keepdims=True)
            idx = jnp.min(jnp.where(logits == m, lane_f, float(LANES)), axis=-1, keepdims=True).astype(i32)
            vals.append(m)
            idxs.append(idx)
            logits = jnp.where(lane == idx, -jnp.inf, logits)
        exps = [jnp.exp(v - vals[0]) for v in vals]
        den = exps[0] + exps[1] + exps[2] + exps[3]
        return idxs, [e / den for e in exps]

    subs = [token_rows(r0) for r0 in range(0, MIX_TM, MIX_SUB)]
    idxs = [jnp.concatenate([sub[0][k] for sub in subs], axis=0) for k in range(TOP_K)]
    wts = [jnp.concatenate([sub[1][k] for sub in subs], axis=0) for k in range(TOP_K)]

    lane = lax.broadcasted_iota(i32, (MIX_TM, LANES), 1)
    hits = [lane == idx for idx in idxs]
    onehot = jnp.zeros((MIX_TM, LANES), f32)
    for hit in hits:
        onehot = onehot + jnp.where(hit, 1.0, 0.0)
    prefix = jnp.dot(tri_ref[...], onehot.astype(bf16), preferred_element_type=f32) + run_ref[...]
    route = jnp.zeros((MIX_TM, LANES), i32)
    rw = jnp.zeros((MIX_TM, LANES), f32)
    for k in range(TOP_K):
        rank = jnp.sum(jnp.where(hits[k], prefix, 0.0), axis=-1, keepdims=True).astype(i32)
        route = jnp.where(lane == k, idxs[k], route)
        route = jnp.where(lane == TOP_K + k, rank, route)
        rw = jnp.where(lane == k, wts[k], rw)
    route_ref[...] = route
    rw_ref[...] = rw
    run = run_ref[...] + jnp.sum(onehot, axis=0, keepdims=True)
    run_ref[...] = run
    cnt_ref[...] = jnp.broadcast_to(run, cnt_ref.shape)


def _mix(os_, ls_, perms_t, expand, ya, gates, x2, g1, sc2, sh2, wa, wb, wo, ln1g, ln1b, wr_hi, wr_lo, br, tri,
         seq):
    t = x2.shape[0]
    per_b = seq // MIX_TM
    row = lambda w: pl.BlockSpec((MIX_TM, w), lambda i: (i, 0))
    const = lambda s: pl.BlockSpec(s, lambda i: tuple(0 for _ in s))
    modb = pl.BlockSpec((1, 1, D_MODEL), lambda i: (i // per_b, 0, 0))
    grp = lambda w: [pl.BlockSpec((1, dil, MIX_TM // dil, w), lambda i: (i // per_b, 0, i % per_b, 0))
                     for _win, dil in DIL_PAIRS]
    return pl.pallas_call(
        _mix_kernel,
        grid=(t // MIX_TM,),
        in_specs=grp(ATT_WIDTH) + grp(LANES) + [
                  const((MIX_TM, MIX_TM)), const((MIX_TM, MIX_TM)), const((LANES, ATT_WIDTH)),
                  row(GM_WIDTH), row(GATE_COLS), row(D_MODEL),
                  modb, modb, modb,
                  const((GM_WIDTH, D_MODEL)), const((ATT_WIDTH, D_MODEL)), const((D_MODEL, D_MODEL)),
                  const((1, D_MODEL)), const((1, D_MODEL)),
                  const((D_MODEL, LANES)), const((D_MODEL, LANES)), const((1, LANES)),
                  const((MIX_TM, MIX_TM))],
        out_specs=[row(D_MODEL), pl.BlockSpec((MIX_TM * ROW_SUB, LANES), lambda i: (i, 0)),
                   row(LANES), row(LANES), const((8, LANES))],
        out_shape=[jax.ShapeDtypeStruct((t, D_MODEL), f32),
                   jax.ShapeDtypeStruct((t * ROW_SUB, LANES), i32),
                   jax.ShapeDtypeStruct((t, LANES), i32),
                   jax.ShapeDtypeStruct((t, LANES), f32),
                   jax.ShapeDtypeStruct((8, LANES), f32)],
        scratch_shapes=[pltpu.VMEM((1, LANES), f32)],
        compiler_params=_params(("arbitrary",)),
        name="mix",
    )(*os_, *ls_, perms_t[1], perms_t[2], expand, ya, gates, x2, g1, sc2, sh2, wa, wb, wo,
      ln1g, ln1b, wr_hi, wr_lo, br, tri)


MOE_TM = 256


def _moe_kernel(te_ref, first_ref, nexte_ref, wslot_ref, nused_ref,
                tok_cur_ref, tok_nxt_ref, h2_hbm, rww_ref, wg_hbm, wu_hbm, wd_hbm,
                bg_ref, bu_ref, bd_ref,
                out_ref, xbuf0, xbuf1, wbuf, wgb, wub, wdb, sem_x, sem_w):
    j = pl.program_id(0)
    last = pl.num_programs(0) - 1
    xbufs = (xbuf0, xbuf1)

    def row_copy(tok, i, s):
        return pltpu.make_async_copy(h2_hbm.at[pl.ds(pl.multiple_of(tok * ROW_SUB, ROW_SUB), ROW_SUB)],
                                     xbufs[s].at[pl.ds(pl.multiple_of(i * ROW_SUB, ROW_SUB), ROW_SUB)],
                                     sem_x.at[s])

    def rows_wait(s):
        pltpu.make_async_copy(h2_hbm.at[pl.ds(0, MOE_TM * ROW_SUB)], xbufs[s], sem_x.at[s]).wait()

    def weight_copies(e, ws):
        return [pltpu.make_async_copy(w.at[e], wbuf.at[ws, k], sem_w.at[ws])
                for k, w in enumerate((wg_hbm, wu_hbm, wd_hbm))]

    def gather_loop(tok_ref, s):
        def body(i, c):
            for p in range(2):
                row_copy(tok_ref[0, 0, 2 * i + p], 2 * i + p, s).start(priority=p)
            return c
        lax.fori_loop(0, MOE_TM // 2, body, 0, unroll=4)

    @pl.when(j == 0)
    def _():
        gather_loop(tok_cur_ref, 0)
        for cp in weight_copies(te_ref[0], wslot_ref[0]):
            cp.start(priority=1)

    def load_weights():
        ws = wslot_ref[j]
        for cp in weight_copies(te_ref[j], ws):
            cp.wait()
        wgb[...] = wbuf[ws, 0].astype(bf16)
        wub[...] = wbuf[ws, 1].astype(bf16)
        wdb[...] = wbuf[ws, 2].astype(bf16)
        ne = nexte_ref[j]

        @pl.when(ne >= 0)
        def _():
            for cp in weight_copies(ne, 1 - ws):
                cp.start(priority=1)

    def expert_mlp(s):
        gather_loop(tok_nxt_ref, 1 - s)
        xb =_unpack_rows(_load_packed(xbufs[s], 0, MOE_TM)).astype(bf16)
        g = jnp.dot(xb, wgb[...], preferred_element_type=f32) + bg_ref[0]
        u = jnp.dot(xb, wub[...], preferred_element_type=f32) + bu_ref[0]
        g = jnp.minimum(g, SWIGLU_LIMIT)
        u = jnp.clip(u, -SWIGLU_LIMIT, SWIGLU_LIMIT)
        act = (u + 1.0) * (g * jax.nn.sigmoid(SWIGLU_ALPHA * g))
        y = (jnp.dot(act.astype(bf16), wdb[...], preferred_element_type=f32) + bd_ref[0]) * rww_ref[...]
        _store_packed(out_ref, _pack_rows(y), MOE_TM)

    def idle_tile(s):
        gather_loop(tok_nxt_ref, s)
        out_ref[...] = jnp.zeros_like(out_ref)

    used = j < nused_ref[0]
    for s in range(2):
        @pl.when(j % 2 == s)
        def _(s=s):
            rows_wait(s)
            pl.when(first_ref[j] == 1)(load_weights)
            pl.when(used)(functools.partial(expert_mlp, s))
            pl.when(jnp.logical_not(used))(functools.partial(idle_tile, 1 - s))
            pl.when(j == last)(functools.partial(rows_wait, 1 - s))


def _moe(tile_e, tile_first, next_e, wslot, n_used, row_tok3, h2p, row_w,
         w_gate, b_gate, w_up, b_up, w_down, b_down):
    ntile = tile_e.shape[0]
    bspec = pl.BlockSpec((1, 1, D_MODEL), lambda j, te, *_: (te[j], 0, 0))
    hbm = pl.BlockSpec(memory_space=pl.ANY)
    grid_spec = pltpu.PrefetchScalarGridSpec(
        num_scalar_prefetch=5,
        grid=(ntile,),
        in_specs=[pl.BlockSpec((1, 1, MOE_TM), lambda j, *_: (j, 0, 0), memory_space=pltpu.SMEM),
                  pl.BlockSpec((1, 1, MOE_TM), lambda j, *_: (jnp.minimum(j + 1, ntile - 1), 0, 0),
                               memory_space=pltpu.SMEM),
                  hbm,
                  pl.BlockSpec((MOE_TM, 1), lambda j, *_: (j, 0)),
                  hbm, hbm, hbm, bspec, bspec, bspec],
        out_specs=pl.BlockSpec((MOE_TM * ROW_SUB, LANES), lambda j, *_: (j, 0)),
        scratch_shapes=[pltpu.VMEM((MOE_TM * ROW_SUB, LANES), i32),
                        pltpu.VMEM((MOE_TM * ROW_SUB, LANES), i32),
                        pltpu.VMEM((2, 3, D_MODEL, D_MODEL), f32),
                        pltpu.VMEM((D_MODEL, D_MODEL), bf16),
                        pltpu.VMEM((D_MODEL, D_MODEL), bf16),
                        pltpu.VMEM((D_MODEL, D_MODEL), bf16),
                        pltpu.SemaphoreType.DMA((2,)),
                        pltpu.SemaphoreType.DMA((2,))],
    )
    return pl.pallas_call(
        _moe_kernel,
        grid_spec=grid_spec,
        out_shape=jax.ShapeDtypeStruct((ntile * MOE_TM * ROW_SUB, LANES), i32),
        compiler_params=_params(("arbitrary",)),
        name="moe",
    )(tile_e, tile_first, next_e, wslot, n_used, row_tok3, row_tok3, h2p, row_w,
      w_gate, w_up, w_down, b_gate, b_up, b_down)


CB_TM = 256


def _combine_kernel(dcur_ref, dnxt_ref, yb_hbm, x1_ref, g2_ref, lng_ref, lnb_ref, out_ref,
                    ybuf0, ybuf1, sem):
    i = pl.program_id(0)
    last = pl.num_programs(0) - 1
    ybufs = (ybuf0, ybuf1)

    def row_copy(d, k, r, s):
        return pltpu.make_async_copy(
            yb_hbm.at[pl.ds(pl.multiple_of(d * ROW_SUB, ROW_SUB), ROW_SUB)],
            ybufs[s].at[pl.ds(pl.multiple_of((k * CB_TM + r) * ROW_SUB, ROW_SUB), ROW_SUB)],
            sem.at[s])

    @pl.when(i == 0)
    def _():
        for k in range(TOP_K):
            def body(r, c, k=k):
                row_copy(dcur_ref[0, k, r], k, r, 0).start()
                return c
            lax.fori_loop(0, CB_TM, body, 0, unroll=8)

    for s in range(2):
        @pl.when(i % 2 == s)
        def _(s=s):
            pltpu.make_async_copy(yb_hbm.at[pl.ds(0, TOP_K * CB_TM * ROW_SUB)], ybufs[s], sem.at[s]).wait()

            @pl.when(i < last)
            def _():
                for k in range(TOP_K):
                    for r in range(CB_TM):
                        row_copy(dnxt_ref[0, k, r], k, r, 1 - s).start(priority=r % 2)

            parts = [_unpack_rows(_load_packed(ybufs[s], k * CB_TM, CB_TM)) for k in range(TOP_K)]
            y = (parts[0] + parts[1]) + (parts[2] + parts[3])
            out_ref[...] = _ln(DN_ALPHA * x1_ref[...] + g2_ref[0] * y) * lng_ref[...] + lnb_ref[...]


def _combine(dest3, yb, x1, g2, ln2g, ln2b, seq):
    t = x1.shape[0]
    nb = t // CB_TM
    per_b = seq // CB_TM
    return pl.pallas_call(
        _combine_kernel,
        grid=(nb,),
        in_specs=[pl.BlockSpec((1, TOP_K, CB_TM), lambda i: (i, 0, 0), memory_space=pltpu.SMEM),
                  pl.BlockSpec((1, TOP_K, CB_TM), lambda i: (jnp.minimum(i + 1, nb - 1), 0, 0),
                               memory_space=pltpu.SMEM),
                  pl.BlockSpec(memory_space=pl.ANY),
                  pl.BlockSpec((CB_TM, D_MODEL), lambda i: (i, 0)),
                  pl.BlockSpec((1, 1, D_MODEL), lambda i: (i // per_b, 0, 0)),
                  pl.BlockSpec((1, D_MODEL), lambda i: (0, 0)),
                  pl.BlockSpec((1, D_MODEL), lambda i: (0, 0))],
        out_specs=pl.BlockSpec((CB_TM, D_MODEL), lambda i: (i, 0)),
        out_shape=jax.ShapeDtypeStruct((t, D_MODEL), f32),
        scratch_shapes=[pltpu.VMEM((TOP_K * CB_TM * ROW_SUB, LANES), i32),
                        pltpu.VMEM((TOP_K * CB_TM * ROW_SUB, LANES), i32),
                        pltpu.SemaphoreType.DMA((2,))],
        compiler_params=_params(("arbitrary",)),
        name="combine",
    )(dest3, dest3, yb, x1, g2, ln2g, ln2b)


def _t5_bucket(dist):
    d = dist.astype(f32)
    large = REL_MAX_EXACT + jnp.log(jnp.maximum(d, float(REL_MAX_EXACT)) / REL_MAX_EXACT) / math.log(
        REL_MAX_DIST / REL_MAX_EXACT) * (REL_BUCKETS - REL_MAX_EXACT)
    large = jnp.minimum(large.astype(i32), REL_BUCKETS - 1)
    return jnp.where(dist < REL_MAX_EXACT, dist, large)


def _bias_indices():
    qi = jnp.arange(ATT_BLOCK)[:, None]
    ki = jnp.arange(2 * ATT_BLOCK)[None, :]
    didx = qi + ATT_BLOCK - ki
    buckets, bands = [], []
    for win, dil in DIL_PAIRS:
        buckets.append(_t5_bucket(jnp.clip(didx, 0, None) * dil))
        bands.append(((didx >= 0) & (didx <= win // dil)).astype(i32))
    return jnp.stack(buckets).astype(i32), jnp.stack(bands)


def _residue_perm(tm, dil):
    n = tm // dil
    dst = jnp.arange(tm)
    src = (dst % n) * dil + dst // n
    return (src[:, None] == jnp.arange(tm)[None, :]).astype(bf16)


def kernel(x, c, w_ada, b_ada, w_in, gm_ln_g, gm_ln_b, gm_w_s, gm_b_s, w_branch_a, w_branch_b, w_out,
           rel_bias, ln1_g, ln1_b, w_router, b_router, w_gate, b_gate, w_up, b_up, w_down, b_down,
           ln2_g, ln2_b):
    batch, seq, _ = x.shape
    t = batch * seq
    l = 0
    x2 = x.reshape(t, D_MODEL)

    c8 = jnp.pad(c, ((0, 8 - batch), (0, 0)))
    mod = _adaln(c8, w_ada[l], b_ada[l][None, :])[:batch]
    sh1, sc1, g1, sh2, sc2, g2 = [m[:, None, :] for m in jnp.split(mod, 6, axis=-1)]

    perms = [_residue_perm(IN_TM, dil) for _win, dil in DIL_PAIRS]
    uv, gates, *qkvs = _inproj(x2, sc1, sh1, w_in[l].astype(bf16), perms, batch, seq)

    bs_full = jnp.repeat(gm_b_s[l].T, GM_WIDTH // GM_GROUPS, axis=1)
    ya = _gmlp(uv, gm_ln_g[l][None, :], gm_ln_b[l][None, :], gm_w_s[l], bs_full)

    bucket, band = _bias_indices()
    bias = _relbias(rel_bias, bucket, band)
    os_, ls_ = [], []
    for g, (_win, dil) in enumerate(DIL_PAIRS):
        o, lse = _attn_group(qkvs[g], bias, g, dil, batch, seq)
        os_.append(o)
        ls_.append(lse)

    wr = jnp.pad(w_router[l], ((0, 0), (0, LANES - N_EXPERTS)))
    wr_hi = wr.astype(bf16)
    wr_lo = (wr - wr_hi.astype(f32)).astype(bf16)
    br = jnp.pad(b_router[l], (0, LANES - N_EXPERTS))[None, :]
    tri = (jnp.arange(MIX_TM)[None, :] < jnp.arange(MIX_TM)[:, None]).astype(bf16)
    perms_t = [_residue_perm(MIX_TM, dil).T for _win, dil in DIL_PAIRS]
    expand = (jnp.arange(LANES)[:, None] == jnp.arange(ATT_WIDTH)[None, :] // HEAD_DIM).astype(bf16)
    x1, h2, route, rw, cnt = _mix(
        os_, ls_, perms_t, expand, ya, gates, x2, g1, sc2, sh2,
        w_branch_a[l].astype(bf16), w_branch_b[l].astype(bf16), w_out[l].astype(bf16),
        ln1_g[l][None, :], ln1_b[l][None, :], wr_hi, wr_lo, br, tri, seq)

    top_e = route[:, :TOP_K]
    rank = route[:, TOP_K:2 * TOP_K]
    top_w = rw[:, :TOP_K]
    counts = cnt[0, :N_EXPERTS].astype(i32)
    pcounts = (counts + MOE_TM - 1) // MOE_TM * MOE_TM
    pends = jnp.cumsum(pcounts)
    pstarts = pends - pcounts
    experts = jnp.arange(N_EXPERTS, dtype=i32)
    dest = jnp.sum(jnp.where(top_e[:, :, None] == experts, pstarts, 0), axis=-1) + rank
    a_total = t * TOP_K
    ntile = a_total // MOE_TM + N_EXPERTS
    r_total = ntile * MOE_TM
    pad_cnt = pcounts - counts
    pad_end = jnp.cumsum(pad_cnt)
    n_pad = r_total - a_total
    pad_i = jnp.arange(n_pad, dtype=i32)
    pad_e = jnp.minimum(jnp.sum((pad_end[None, :] <= pad_i[:, None]).astype(i32), axis=1), N_EXPERTS - 1)
    in_expert = pad_i < pad_end[-1]
    pad_row = jnp.where(in_expert,
                        (pstarts + counts)[pad_e] + pad_i - (pad_end - pad_cnt)[pad_e],
                        pends[-1] + pad_i - pad_end[-1])
    keys = jnp.concatenate([dest.reshape(-1), pad_row])
    vals = jnp.concatenate([jnp.arange(a_total, dtype=i32), jnp.full((n_pad,), -1, i32)])
    wts = jnp.concatenate([top_w.reshape(-1), jnp.zeros((n_pad,), f32)])
    _, inv, row_w = lax.sort((keys, vals, wts), num_keys=1)
    row_tok = jnp.where(inv >= 0, inv // TOP_K, jnp.arange(r_total, dtype=i32) % t)
    n_used = pends[-1] // MOE_TM
    tile_idx = jnp.minimum(jnp.arange(ntile, dtype=i32), n_used - 1)
    tile_e = jnp.sum((pends[None, :] <= (tile_idx * MOE_TM)[:, None]).astype(i32), axis=1)
    tile_first = jnp.concatenate([jnp.ones((1,), i32), (tile_e[1:] != tile_e[:-1]).astype(i32)])
    nonempty = counts > 0
    later = lax.cummin(jnp.where(nonempty, experts, N_EXPERTS), reverse=True)
    next_nonempty = jnp.concatenate([later[1:], jnp.full((1,), N_EXPERTS, i32)])
    next_nonempty = jnp.where(next_nonempty >= N_EXPERTS, -1, next_nonempty)
    expert_slot = (jnp.cumsum(nonempty.astype(i32)) - 1) % 2

    yb = _moe(tile_e, tile_first, next_nonempty[tile_e], expert_slot[tile_e], n_used.reshape(1),
              row_tok.reshape(ntile, 1, MOE_TM), h2, row_w[:, None],
              w_gate[l], b_gate[l][:, None, :], w_up[l], b_up[l][:, None, :],
              w_down[l], b_down[l][:, None, :])

    dest3 = dest.reshape(t // CB_TM, CB_TM, TOP_K).transpose(0, 2, 1)
    out = _combine(dest3, yb, x1, g2, ln2_g[l][None, :], ln2_b[l][None, :], seq)
    return out.reshape(batch, seq, D_MODEL)
```

```python
import functools
import math

import jax
import jax.numpy as jnp
from jax import lax
from jax.experimental import pallas as pl
from jax.experimental.pallas import tpu as pltpu

f32 = jnp.float32
bf16 = jnp.bfloat16
i32 = jnp.int32

D_MODEL = 1024
GM_WIDTH = 512
GM_GROUPS = 8
GM_CHUNK = 128
DIL_PAIRS = ((128, 1), (512, 4), (2048, 16))
N_DIL = 3
HEADS_PER_GROUP = 8
HEAD_DIM = 64
ATT_WIDTH = 512
ATT_BLOCK = 128
NEG_INF = -1e30
REL_BUCKETS = 32
REL_MAX_EXACT = 16
REL_MAX_DIST = 2048
N_EXPERTS = 32
TOP_K = 4
SWIGLU_LIMIT = 7.0
SWIGLU_ALPHA = 1.702
MOE_BLOCK = 128
DEPTH = 1
DN_ALPHA = (2 * DEPTH) ** 0.25
LN_EPS = 1e-5
UV_COLS = 2 * GM_WIDTH
QKV_COLS = N_DIL * 3 * ATT_WIDTH
GATE_COLS = 2 * D_MODEL
IN_COLS = UV_COLS + QKV_COLS + GATE_COLS

LANES = 128
SUBLANES = 8
VMEM_LIMIT = 56 * 1024 * 1024


def _ln(x):
    mu = jnp.mean(x, axis=-1, keepdims=True)
    xc = x - mu
    var = jnp.mean(xc * xc, axis=-1, keepdims=True)
    return xc * lax.rsqrt(var + LN_EPS)


def _params(sem, vmem=VMEM_LIMIT):
    return pltpu.CompilerParams(dimension_semantics=sem, vmem_limit_bytes=vmem)


def _adaln_kernel(c_ref, w_ref, b_ref, o_ref):
    c = c_ref[...]
    s = c * jax.nn.sigmoid(c)
    o_ref[...] = jnp.dot(s, w_ref[...], preferred_element_type=f32,
                         precision=lax.Precision.HIGHEST) + b_ref[...]


def _adaln(c8, w_ada, b_ada):
    n = w_ada.shape[1] // D_MODEL
    return pl.pallas_call(
        _adaln_kernel,
        grid=(n,),
        in_specs=[pl.BlockSpec((8, D_MODEL), lambda j: (0, 0)),
                  pl.BlockSpec((D_MODEL, D_MODEL), lambda j: (0, j)),
                  pl.BlockSpec((1, D_MODEL), lambda j: (0, j))],
        out_specs=pl.BlockSpec((8, D_MODEL), lambda j: (0, j)),
        out_shape=jax.ShapeDtypeStruct((8, w_ada.shape[1]), f32),
        compiler_params=_params(("arbitrary",)),
        name="adaln",
    )(c8, w_ada, b_ada)


IN_TM = 256
IN_CW = 512
GRP_COLS = 3 * ATT_WIDTH


def _inproj_kernel(x_ref, sc_ref, sh_ref, w_ref, p1_ref, p2_ref,
                   uv_ref, gt_ref, qkv0_ref, qkv1_ref, qkv2_ref):
    xn = _ln(x_ref[...])
    h = (xn * (1.0 + sc_ref[0]) + sh_ref[0]).astype(bf16)
    hp = [h,
          jnp.dot(p1_ref[...], h, preferred_element_type=f32).astype(bf16),
          jnp.dot(p2_ref[...], h, preferred_element_type=f32).astype(bf16)]
    for c0 in range(0, UV_COLS, IN_CW):
        acc = jnp.dot(h, w_ref[:, c0:c0 + IN_CW], preferred_element_type=f32)
        uv_ref[:, c0:c0 + IN_CW] = jax.nn.gelu(acc).astype(bf16)
    for g, (qref, (_win, dil)) in enumerate(zip((qkv0_ref, qkv1_ref, qkv2_ref), DIL_PAIRS)):
        n = IN_TM // dil
        for q0 in range(0, GRP_COLS, IN_CW):
            c0 = UV_COLS + g * GRP_COLS + q0
            acc = jnp.dot(hp[g], w_ref[:, c0:c0 + IN_CW], preferred_element_type=f32).astype(bf16)
            for rho in range(dil):
                qref[0, rho, :, q0:q0 + IN_CW] = acc[rho * n:(rho + 1) * n, :]
    for g0 in range(0, GATE_COLS, IN_CW):
        c0 = UV_COLS + QKV_COLS + g0
        acc = jnp.dot(h, w_ref[:, c0:c0 + IN_CW], preferred_element_type=f32)
        gt_ref[:, g0:g0 + IN_CW] = jax.nn.sigmoid(acc).astype(bf16)


def _inproj(x2, sc1, sh1, w_in_bf, perms, batch, seq):
    t = x2.shape[0]
    per_b = seq // IN_TM
    qkv_specs, qkv_shapes = [], []
    for _win, dil in DIL_PAIRS:
        n = IN_TM // dil
        qkv_specs.append(pl.BlockSpec((1, dil, n, GRP_COLS), lambda i: (i // per_b, 0, i % per_b, 0)))
        qkv_shapes.append(jax.ShapeDtypeStruct((batch, dil, seq // dil, GRP_COLS), bf16))
    return pl.pallas_call(
        _inproj_kernel,
        grid=(t // IN_TM,),
        in_specs=[pl.BlockSpec((IN_TM, D_MODEL), lambda i: (i, 0)),
                  pl.BlockSpec((1, 1, D_MODEL), lambda i: (i // per_b, 0, 0)),
                  pl.BlockSpec((1, 1, D_MODEL), lambda i: (i // per_b, 0, 0)),
                  pl.BlockSpec((D_MODEL, IN_COLS), lambda i: (0, 0)),
                  pl.BlockSpec((IN_TM, IN_TM), lambda i: (0, 0)),
                  pl.BlockSpec((IN_TM, IN_TM), lambda i: (0, 0))],
        out_specs=[pl.BlockSpec((IN_TM, UV_COLS), lambda i: (i, 0)),
                   pl.BlockSpec((IN_TM, GATE_COLS), lambda i: (i, 0))] + qkv_specs,
        out_shape=[jax.ShapeDtypeStruct((t, UV_COLS), bf16),
                   jax.ShapeDtypeStruct((t, GATE_COLS), bf16)] + qkv_shapes,
        compiler_params=_params(("arbitrary",)),
        name="inproj",
    )(x2, sc1, sh1, w_in_bf, perms[1], perms[2])


GM_TM = 512


def _gmlp_kernel(u_ref, v_ref, g_ref, b_ref, ws_ref, bs_ref, ya_ref):
    row = lax.broadcasted_iota(i32, (GM_CHUNK, GM_CHUNK), 0)
    col = lax.broadcasted_iota(i32, (GM_CHUNK, GM_CHUNK), 1)
    causal = col <= row
    first_half = lax.broadcasted_iota(i32, (GM_CHUNK, LANES), 1) < (GM_WIDTH // GM_GROUPS)
    ws = [jnp.where(causal, ws_ref[g], 0.0).astype(bf16) for g in range(GM_GROUPS)]
    for ch in range(GM_TM // GM_CHUNK):
        r0 = ch * GM_CHUNK
        vn = _ln(v_ref[r0:r0 + GM_CHUNK, :].astype(f32)) * g_ref[...] + b_ref[...]
        vn = vn.astype(bf16)
        for j in range(GM_WIDTH // LANES):
            slab = vn[:, j * LANES:(j + 1) * LANES]
            s_lo = jnp.dot(ws[2 * j], slab, preferred_element_type=f32)
            s_hi = jnp.dot(ws[2 * j + 1], slab, preferred_element_type=f32)
            s = jnp.where(first_half, s_lo, s_hi) + bs_ref[:, j * LANES:(j + 1) * LANES]
            u = u_ref[r0:r0 + GM_CHUNK, j * LANES:(j + 1) * LANES].astype(f32)
            ya_ref[r0:r0 + GM_CHUNK, j * LANES:(j + 1) * LANES] = (u * s).astype(bf16)


def _gmlp(uv, ln_g, ln_b, w_s, bs_full):
    t = uv.shape[0]
    return pl.pallas_call(
        _gmlp_kernel,
        grid=(t // GM_TM,),
        in_specs=[pl.BlockSpec((GM_TM, GM_WIDTH), lambda i: (i, 0)),
                  pl.BlockSpec((GM_TM, GM_WIDTH), lambda i: (i, 1)),
                  pl.BlockSpec((1, GM_WIDTH), lambda i: (0, 0)),
                  pl.BlockSpec((1, GM_WIDTH), lambda i: (0, 0)),
                  pl.BlockSpec((GM_GROUPS, GM_CHUNK, GM_CHUNK), lambda i: (0, 0, 0)),
                  pl.BlockSpec((GM_CHUNK, GM_WIDTH), lambda i: (0, 0))],
        out_specs=pl.BlockSpec((GM_TM, GM_WIDTH), lambda i: (i, 0)),
        out_shape=jax.ShapeDtypeStruct((t, GM_WIDTH), bf16),
        compiler_params=_params(("arbitrary",)),
        name="gmlp",
    )(uv, uv, ln_g, ln_b, w_s, bs_full)


def _relbias_kernel(tab_ref, bucket_ref, band_ref, out_ref):
    g = pl.program_id(0)
    bk = bucket_ref[0]
    band = band_ref[0] > 0
    for h in range(HEADS_PER_GROUP):
        acc = jnp.zeros((ATT_BLOCK, 2 * ATT_BLOCK), f32)
        for b in range(REL_BUCKETS):
            acc = jnp.where(bk == b, tab_ref[b, g * HEADS_PER_GROUP + h], acc)
        out_ref[0, h] = jnp.where(band, acc, NEG_INF)


def _relbias(rel_bias, bucket, band):
    return pl.pallas_call(
        _relbias_kernel,
        grid=(N_DIL,),
        in_specs=[pl.BlockSpec(memory_space=pltpu.SMEM),
                  pl.BlockSpec((1, ATT_BLOCK, 2 * ATT_BLOCK), lambda g: (g, 0, 0)),
                  pl.BlockSpec((1, ATT_BLOCK, 2 * ATT_BLOCK), lambda g: (g, 0, 0))],
        out_specs=pl.BlockSpec((1, HEADS_PER_GROUP, ATT_BLOCK, 2 * ATT_BLOCK),
                               lambda g: (g, 0, 0, 0)),
        out_shape=jax.ShapeDtypeStruct((N_DIL, HEADS_PER_GROUP, ATT_BLOCK, 2 * ATT_BLOCK), f32),
        compiler_params=_params(("arbitrary",)),
        name="relbias",
    )(rel_bias, bucket, band)


ATT_MAX_STEP_BLOCKS = 4


def _attn_kernel(nblk, q_ref, kp_ref, kc_ref, vp_ref, vc_ref, bias_ref, o_ref, lse_ref):
    first = pl.program_id(2) == 0
    lane = lax.broadcasted_iota(i32, (ATT_BLOCK, LANES), 1)
    lo_half = lane < HEAD_DIM
    nt = (((1,), (1,)), ((), ()))
    ones = jnp.ones((2 * ATT_BLOCK, LANES), bf16)
    n_slab = ATT_WIDTH // LANES
    logits, v_ext = [], []
    for i in range(nblk):
        cur = slice(i * ATT_BLOCK, (i + 1) * ATT_BLOCK)
        prv = slice((i - 1) * ATT_BLOCK, i * ATT_BLOCK)
        for j in range(n_slab):
            sl = slice(j * LANES, (j + 1) * LANES)
            q = q_ref[0, 0, cur, sl] * (HEAD_DIM ** -0.5)
            k_prev = kp_ref[0, 0, :, sl] if i == 0 else kc_ref[0, 0, prv, sl]
            v_prev = vp_ref[0, 0, :, sl] if i == 0 else vc_ref[0, 0, prv, sl]
            k_cat = jnp.concatenate([k_prev, kc_ref[0, 0, cur, sl]], axis=0)
            v_cat = jnp.concatenate([v_prev, vc_ref[0, 0, cur, sl]], axis=0)
            v_ext.append(jnp.concatenate([v_cat, ones], axis=1))
            for hh in range(2):
                qm = jnp.where(lo_half if hh == 0 else jnp.logical_not(lo_half), q, 0.0).astype(bf16)
                logits.append(lax.dot_general(qm, k_cat, nt, preferred_element_type=f32))
    bias = bias_ref[0].reshape(HEADS_PER_GROUP * ATT_BLOCK, 2 * ATT_BLOCK)
    rows_per_block = HEADS_PER_GROUP * ATT_BLOCK
    lg = jnp.concatenate(logits, axis=0) + jnp.concatenate([bias] * nblk, axis=0)
    row = lax.broadcasted_iota(i32, lg.shape, 0)
    col = lax.broadcasted_iota(i32, lg.shape, 1)
    no_prev = jnp.logical_and(first, jnp.logical_and(row < rows_per_block, col < ATT_BLOCK))
    lg = jnp.where(no_prev, NEG_INF, lg)
    m = jnp.max(lg, axis=-1, keepdims=True)
    p = jnp.exp(lg - m).astype(bf16)
    for i in range(nblk):
        cur = slice(i * ATT_BLOCK, (i + 1) * ATT_BLOCK)
        lse_tile = jnp.zeros((ATT_BLOCK, LANES), f32)
        for j in range(n_slab):
            outs = []
            for hh in range(2):
                h = 2 * j + hh
                r0 = i * rows_per_block + h * ATT_BLOCK
                r = jnp.dot(p[r0:r0 + ATT_BLOCK], v_ext[i * n_slab + j], preferred_element_type=f32)
                den = r[:, LANES:]
                outs.append(r[:, :LANES] * (1.0 / den))
                lse_h = m[r0:r0 + ATT_BLOCK] + jnp.log(den)
                lse_tile = jnp.where(lane == h, lse_h, lse_tile)
            o_ref[0, 0, cur, j * LANES:(j + 1) * LANES] = jnp.where(lo_half, outs[0], outs[1]).astype(bf16)
        lse_ref[0, 0, cur, :] = lse_tile


def _attn_group(qkv_g, bias, g, dil, batch, seq):
    l = seq // dil
    nblk = min(ATT_MAX_STEP_BLOCKS, l // ATT_BLOCK)
    tm = nblk * ATT_BLOCK
    nsteps = l // tm

    def cur(cb):
        return pl.BlockSpec((1, 1, tm, ATT_WIDTH), lambda b, r, n: (b, r, n, cb))

    def prev(cb):
        return pl.BlockSpec((1, 1, ATT_BLOCK, ATT_WIDTH),
                            lambda b, r, n: (b, r, jnp.maximum(n * nblk - 1, 0), cb))

    return pl.pallas_call(
        functools.partial(_attn_kernel, nblk),
        grid=(batch, dil, nsteps),
        in_specs=[cur(0), prev(1), cur(1), prev(2), cur(2),
                  pl.BlockSpec((1, HEADS_PER_GROUP, ATT_BLOCK, 2 * ATT_BLOCK),
                               lambda b, r, n: (g, 0, 0, 0))],
        out_specs=[pl.BlockSpec((1, 1, tm, ATT_WIDTH), lambda b, r, n: (b, r, n, 0)),
                   pl.BlockSpec((1, 1, tm, LANES), lambda b, r, n: (b, r, n, 0))],
        out_shape=[jax.ShapeDtypeStruct((batch, dil, l, ATT_WIDTH), bf16),
                   jax.ShapeDtypeStruct((batch, dil, l, LANES), f32)],
        compiler_params=_params(("arbitrary", "arbitrary", "arbitrary")),
        name=f"attn_g{g}",
    )(qkv_g, qkv_g, qkv_g, qkv_g, qkv_g, bias)


ROW_WORDS = D_MODEL // 2
ROW_SUB = ROW_WORDS // LANES
HI_MASK = -65536


def _pack_rows(x):
    bits = lax.bitcast_convert_type(x.astype(bf16).astype(f32), i32)
    return lax.shift_right_logical(bits[:, :ROW_WORDS], 16) | (bits[:, ROW_WORDS:] & HI_MASK)


def _unpack_rows(words):
    lo = lax.bitcast_convert_type(lax.shift_left(words, 16), f32)
    hi = lax.bitcast_convert_type(words & HI_MASK, f32)
    return jnp.concatenate([lo, hi], axis=1)


def _store_packed(ref, words, n, first_row=0):
    for r in range(ROW_SUB):
        ref[pl.ds(first_row * ROW_SUB + r, n, stride=ROW_SUB), :] = words[:, r * LANES:(r + 1) * LANES]


def _load_packed(ref, first_row, n):
    return jnp.concatenate([ref[pl.ds(first_row * ROW_SUB + r, n, stride=ROW_SUB), :] for r in range(ROW_SUB)],
                           axis=1)


MIX_TM = 256
MIX_SUB = 128


def _split_bf16(x, parts):
    out = []
    for _ in range(parts):
        hi = x.astype(bf16)
        out.append(hi)
        x = x - hi.astype(f32)
    return out


def _mix_kernel(o0_ref, o1_ref, o2_ref, l0_ref, l1_ref, l2_ref, pt1_ref, pt2_ref, ex_ref,
                ya_ref, gt_ref, x_ref,
                g1_ref, sc2_ref, sh2_ref, wa_ref, wb_ref, wo_ref, ln1g_ref, ln1b_ref,
                wrh_ref, wrl_ref, br_ref, tri_ref,
                x1_ref, h2_ref, route_ref, rw_ref, cnt_ref, run_ref):
    @pl.when(pl.program_id(0) == 0)
    def _():
        run_ref[...] = jnp.zeros_like(run_ref)

    lane = lax.broadcasted_iota(i32, (MIX_SUB, LANES), 1)
    lane_f = lane.astype(f32)

    def token_rows(r0):
        rows = slice(r0, r0 + MIX_SUB)
        os_, ls_ = [o0_ref[0, 0, rows, :].astype(f32)], [l0_ref[0, 0, rows, :]]
        for o_ref, l_ref, pt_ref in ((o1_ref, l1_ref, pt1_ref), (o2_ref, l2_ref, pt2_ref)):
            pt = pt_ref[rows, :]
            os_.append(jnp.dot(pt, o_ref[0].reshape(MIX_TM, ATT_WIDTH), preferred_element_type=f32))
            parts = [jnp.dot(pt, part, preferred_element_type=f32)
                     for part in _split_bf16(l_ref[0].reshape(MIX_TM, LANES), 3)]
            ls_.append((parts[0] + parts[1]) + parts[2])
        lm = jnp.maximum(jnp.maximum(ls_[0], ls_[1]), ls_[2])
        es = [jnp.exp(lse - lm) for lse in ls_]
        inv = 1.0 / (es[0] + es[1] + es[2])
        yb = jnp.zeros((MIX_SUB, ATT_WIDTH), f32)
        for e, o in zip(es, os_):
            w_hi, w_lo = _split_bf16(e * inv, 2)
            w_full = (jnp.dot(w_hi, ex_ref[...], preferred_element_type=f32)
                      + jnp.dot(w_lo, ex_ref[...], preferred_element_type=f32))
            yb = yb + w_full * o
        a = jnp.dot(ya_ref[rows, :], wa_ref[...], preferred_element_type=f32)
        b = jnp.dot(yb.astype(bf16), wb_ref[...], preferred_element_type=f32)
        merged = gt_ref[rows, :D_MODEL].astype(f32) * a + gt_ref[rows, D_MODEL:].astype(f32) * b
        mix = jnp.dot(merged.astype(bf16), wo_ref[...], preferred_element_type=f32)
        x1 = _ln(DN_ALPHA * x_ref[rows, :] + g1_ref[0] * mix) * ln1g_ref[...] + ln1b_ref[...]
        x1_ref[rows, :] = x1
        h2 = _ln(x1) * (1.0 + sc2_ref[0]) + sh2_ref[0]
        _store_packed(h2_ref, _pack_rows(h2), MIX_SUB, r0)

        h_hi, h_lo = _split_bf16(h2, 2)
        logits = (jnp.dot(h_hi, wrh_ref[...], preferred_element_type=f32)
                  + (jnp.dot(h_hi, wrl_ref[...], preferred_element_type=f32)
                     + jnp.dot(h_lo, wrh_ref[...], preferred_element_type=f32))) + br_ref[...]
        logits = jnp.where(lane < N_EXPERTS, logits, -jnp.inf)
        vals, idxs = [], []
        for _k in range(TOP_K):
            m = jnp.max(logits, axis=-1, keepdims=True)
            idx = jnp.min(jnp.where(logits == m, lane_f, float(LANES)), axis=-1, keepdims=True).astype(i32)
            vals.append(m)
            idxs.append(idx)
            logits = jnp.where(lane == idx, -jnp.inf, logits)
        exps = [jnp.exp(v - vals[0]) for v in vals]
        den = exps[0] + exps[1] + exps[2] + exps[3]
        return idxs, [e / den for e in exps]

    subs = [token_rows(r0) for r0 in range(0, MIX_TM, MIX_SUB)]
    idxs = [jnp.concatenate([sub[0][k] for sub in subs], axis=0) for k in range(TOP_K)]
    wts = [jnp.concatenate([sub[1][k] for sub in subs], axis=0) for k in range(TOP_K)]

    lane = lax.broadcasted_iota(i32, (MIX_TM, LANES), 1)
    hits = [lane == idx for idx in idxs]
    onehot = jnp.zeros((MIX_TM, LANES), f32)
    for hit in hits:
        onehot = onehot + jnp.where(hit, 1.0, 0.0)
    prefix = jnp.dot(tri_ref[...], onehot.astype(bf16), preferred_element_type=f32) + run_ref[...]
    route = jnp.zeros((MIX_TM, LANES), i32)
    rw = jnp.zeros((MIX_TM, LANES), f32)
    for k in range(TOP_K):
        rank = jnp.sum(jnp.where(hits[k], prefix, 0.0), axis=-1, keepdims=True).astype(i32)
        route = jnp.where(lane == k, idxs[k], route)
        route = jnp.where(lane == TOP_K + k, rank, route)
        rw = jnp.where(lane == k, wts[k], rw)
    route_ref[...] = route
    rw_ref[...] = rw
    run = run_ref[...] + jnp.sum(onehot, axis=0, keepdims=True)
    run_ref[...] = run
    cnt_ref[...] = jnp.broadcast_to(run, cnt_ref.shape)


def _mix(os_, ls_, perms_t, expand, ya, gates, x2, g1, sc2, sh2, wa, wb, wo, ln1g, ln1b, wr_hi, wr_lo, br, tri,
         seq):
    t = x2.shape[0]
    per_b = seq // MIX_TM
    row = lambda w: pl.BlockSpec((MIX_TM, w), lambda i: (i, 0))
    const = lambda s: pl.BlockSpec(s, lambda i: tuple(0 for _ in s))
    modb = pl.BlockSpec((1, 1, D_MODEL), lambda i: (i // per_b, 0, 0))
    grp = lambda w: [pl.BlockSpec((1, dil, MIX_TM // dil, w), lambda i: (i // per_b, 0, i % per_b, 0))
                     for _win, dil in DIL_PAIRS]
    return pl.pallas_call(
        _mix_kernel,
        grid=(t // MIX_TM,),
        in_specs=grp(ATT_WIDTH) + grp(LANES) + [
                  const((MIX_TM, MIX_TM)), const((MIX_TM, MIX_TM)), const((LANES, ATT_WIDTH)),
                  row(GM_WIDTH), row(GATE_COLS), row(D_MODEL),
                  modb, modb, modb,
                  const((GM_WIDTH, D_MODEL)), const((ATT_WIDTH, D_MODEL)), const((D_MODEL, D_MODEL)),
                  const((1, D_MODEL)), const((1, D_MODEL)),
                  const((D_MODEL, LANES)), const((D_MODEL, LANES)), const((1, LANES)),
                  const((MIX_TM, MIX_TM))],
        out_specs=[row(D_MODEL), pl.BlockSpec((MIX_TM * ROW_SUB, LANES), lambda i: (i, 0)),
                   row(LANES), row(LANES), const((8, LANES))],
        out_shape=[jax.ShapeDtypeStruct((t, D_MODEL), f32),
                   jax.ShapeDtypeStruct((t * ROW_SUB, LANES), i32),
                   jax.ShapeDtypeStruct((t, LANES), i32),
                   jax.ShapeDtypeStruct((t, LANES), f32),
                   jax.ShapeDtypeStruct((8, LANES), f32)],
        scratch_shapes=[pltpu.VMEM((1, LANES), f32)],
        compiler_params=_params(("arbitrary",)),
        name="mix",
    )(*os_, *ls_, perms_t[1], perms_t[2], expand, ya, gates, x2, g1, sc2, sh2, wa, wb, wo,
      ln1g, ln1b, wr_hi, wr_lo, br, tri)


DISP_TM = 256
MOE_TM = 256


def _dispatch_kernel(pends_ref, pcnt_ref, nused_ref, dest_ref, h2p_ref, xs_hbm, zbuf, sem, zsem):
    i = pl.program_id(0)
    ntile = xs_hbm.shape[0] // (MOE_TM * ROW_SUB)

    def zero_tile(first_row):
        return pltpu.make_async_copy(
            zbuf, xs_hbm.at[pl.ds(pl.multiple_of(first_row * ROW_SUB, MOE_TM * ROW_SUB), MOE_TM * ROW_SUB)], zsem)

    def for_each_zero_tile(fn):
        for e in range(N_EXPERTS):
            pl.when(pcnt_ref[e] > 0)(functools.partial(fn, lambda e=e: zero_tile(pends_ref[e] - MOE_TM)))
        for k in range(N_EXPERTS):
            tile = nused_ref[0] + k
            pl.when(tile < ntile)(functools.partial(fn, lambda tile=tile: zero_tile(tile * MOE_TM)))

    @pl.when(i == 0)
    def _():
        zbuf[...] = jnp.zeros_like(zbuf)
        for_each_zero_tile(lambda mk: mk().start())
        for_each_zero_tile(lambda mk: mk().wait())

    def row_copy(k, r):
        d = dest_ref[0, k, r]
        return pltpu.make_async_copy(h2p_ref.at[pl.ds(r * ROW_SUB, ROW_SUB)],
                                     xs_hbm.at[pl.ds(pl.multiple_of(d * ROW_SUB, ROW_SUB), ROW_SUB)], sem)

    for r in range(DISP_TM):
        for k in range(TOP_K):
            row_copy(k, r).start(priority=k % 2)
    for k in range(TOP_K):
        pltpu.make_async_copy(h2p_ref, xs_hbm.at[pl.ds(0, DISP_TM * ROW_SUB)], sem).wait()


def _dispatch(pends, pcounts, n_used, dest3, h2p, ntile):
    t = h2p.shape[0] // ROW_SUB
    grid_spec = pltpu.PrefetchScalarGridSpec(
        num_scalar_prefetch=3,
        grid=(t // DISP_TM,),
        in_specs=[pl.BlockSpec((1, TOP_K, DISP_TM), lambda i, *_: (i, 0, 0), memory_space=pltpu.SMEM),
                  pl.BlockSpec((DISP_TM * ROW_SUB, LANES), lambda i, *_: (i, 0))],
        out_specs=pl.BlockSpec(memory_space=pl.ANY),
        scratch_shapes=[pltpu.VMEM((MOE_TM * ROW_SUB, LANES), i32),
                        pltpu.SemaphoreType.DMA(()),
                        pltpu.SemaphoreType.DMA(())],
    )
    return pl.pallas_call(
        _dispatch_kernel,
        grid_spec=grid_spec,
        out_shape=jax.ShapeDtypeStruct((ntile * MOE_TM * ROW_SUB, LANES), i32),
        compiler_params=_params(("arbitrary",)),
        name="dispatch",
    )(pends, pcounts, n_used, dest3, h2p)


def _moe_kernel(te_ref, first_ref, nexte_ref, wslot_ref, nused_ref,
                xs_ref, wg_hbm, wu_hbm, wd_hbm, bg_ref, bu_ref, bd_ref,
                out_ref, wbuf, wgb, wub, wdb, sem_w):
    j = pl.program_id(0)

    def weight_copies(e, ws):
        return [pltpu.make_async_copy(w.at[e], wbuf.at[ws, k], sem_w.at[ws])
                for k, w in enumerate((wg_hbm, wu_hbm, wd_hbm))]

    @pl.when(j == 0)
    def _():
        for cp in weight_copies(te_ref[0], wslot_ref[0]):
            cp.start()

    @pl.when(first_ref[j] == 1)
    def _():
        ws = wslot_ref[j]
        for cp in weight_copies(te_ref[j], ws):
            cp.wait()
        wgb[...] = wbuf[ws, 0].astype(bf16)
        wub[...] = wbuf[ws, 1].astype(bf16)
        wdb[...] = wbuf[ws, 2].astype(bf16)
        ne = nexte_ref[j]

        @pl.when(ne >= 0)
        def _():
            for cp in weight_copies(ne, 1 - ws):
                cp.start()

    used = j < nused_ref[0]

    @pl.when(used)
    def _():
        xb = _unpack_rows(_load_packed(xs_ref, 0, MOE_TM)).astype(bf16)
        g = jnp.dot(xb, wgb[...], preferred_element_type=f32) + bg_ref[0]
        u = jnp.dot(xb, wub[...], preferred_element_type=f32) + bu_ref[0]
        g = jnp.minimum(g, SWIGLU_LIMIT)
        u = jnp.clip(u, -SWIGLU_LIMIT, SWIGLU_LIMIT)
        act = (u + 1.0) * (g * jax.nn.sigmoid(SWIGLU_ALPHA * g))
        y = jnp.dot(act.astype(bf16), wdb[...], preferred_element_type=f32) + bd_ref[0]
        _store_packed(out_ref, _pack_rows(y), MOE_TM)

    @pl.when(jnp.logical_not(used))
    def _():
        out_ref[...] = jnp.zeros_like(out_ref)


def _moe(tile_e, tile_first, next_e, wslot, n_used, xs, w_gate, b_gate, w_up, b_up, w_down, b_down):
    ntile = tile_e.shape[0]
    bspec = pl.BlockSpec((1, 1, D_MODEL), lambda j, te, *_: (te[j], 0, 0))
    hbm = pl.BlockSpec(memory_space=pl.ANY)
    grid_spec = pltpu.PrefetchScalarGridSpec(
        num_scalar_prefetch=5,
        grid=(ntile,),
        in_specs=[pl.BlockSpec((MOE_TM * ROW_SUB, LANES),
                               lambda j, te, fi, ne, ws, nu: (jnp.minimum(j, nu[0] - 1), 0)),
                  hbm, hbm, hbm, bspec, bspec, bspec],
        out_specs=pl.BlockSpec((MOE_TM * ROW_SUB, LANES), lambda j, *_: (j, 0)),
        scratch_shapes=[pltpu.VMEM((2, 3, D_MODEL, D_MODEL), f32),
                        pltpu.VMEM((D_MODEL, D_MODEL), bf16),
                        pltpu.VMEM((D_MODEL, D_MODEL), bf16),
                        pltpu.VMEM((D_MODEL, D_MODEL), bf16),
                        pltpu.SemaphoreType.DMA((2,))],
    )
    return pl.pallas_call(
        _moe_kernel,
        grid_spec=grid_spec,
        out_shape=jax.ShapeDtypeStruct((ntile * MOE_TM * ROW_SUB, LANES), i32),
        compiler_params=_params(("arbitrary",)),
        name="moe",
    )(tile_e, tile_first, next_e, wslot, n_used, xs, w_gate, w_up, w_down, b_gate, b_up, b_down)


CB_TM = 256


def _combine_kernel(dcur_ref, dnxt_ref, yb_hbm, rw_ref, x1_ref, g2_ref, lng_ref, lnb_ref, out_ref,
                    ybuf0, ybuf1, sem):
    i = pl.program_id(0)
    last = pl.num_programs(0) - 1
    ybufs = (ybuf0, ybuf1)

    def row_copy(d, k, r, s):
        return pltpu.make_async_copy(
            yb_hbm.at[pl.ds(pl.multiple_of(d * ROW_SUB, ROW_SUB), ROW_SUB)],
            ybufs[s].at[pl.ds(pl.multiple_of((k * CB_TM + r) * ROW_SUB, ROW_SUB), ROW_SUB)],
            sem.at[s])

    @pl.when(i == 0)
    def _():
        for k in range(TOP_K):
            def body(r, c, k=k):
                row_copy(dcur_ref[0, k, r], k, r, 0).start()
                return c
            lax.fori_loop(0, CB_TM, body, 0, unroll=8)

    for s in range(2):
        @pl.when(i % 2 == s)
        def _(s=s):
            pltpu.make_async_copy(yb_hbm.at[pl.ds(0, TOP_K * CB_TM * ROW_SUB)], ybufs[s], sem.at[s]).wait()

            @pl.when(i < last)
            def _():
                for k in range(TOP_K):
                    for r in range(CB_TM):
                        row_copy(dnxt_ref[0, k, r], k, r, 1 - s).start(priority=r % 2)

            parts = [_unpack_rows(_load_packed(ybufs[s], k * CB_TM, CB_TM)) * rw_ref[:, k:k + 1]
                     for k in range(TOP_K)]
            y = (parts[0] + parts[1]) + (parts[2] + parts[3])
            out_ref[...] = _ln(DN_ALPHA * x1_ref[...] + g2_ref[0] * y) * lng_ref[...] + lnb_ref[...]


def _combine(dest3, yb, rw, x1, g2, ln2g, ln2b, seq):
    t = x1.shape[0]
    nb = t // CB_TM
    per_b = seq // CB_TM
    return pl.pallas_call(
        _combine_kernel,
        grid=(nb,),
        in_specs=[pl.BlockSpec((1, TOP_K, CB_TM), lambda i: (i, 0, 0), memory_space=pltpu.SMEM),
                  pl.BlockSpec((1, TOP_K, CB_TM), lambda i: (jnp.minimum(i + 1, nb - 1), 0, 0),
                               memory_space=pltpu.SMEM),
                  pl.BlockSpec(memory_space=pl.ANY),
                  pl.BlockSpec((CB_TM, LANES), lambda i: (i, 0)),
                  pl.BlockSpec((CB_TM, D_MODEL), lambda i: (i, 0)),
                  pl.BlockSpec((1, 1, D_MODEL), lambda i: (i // per_b, 0, 0)),
                  pl.BlockSpec((1, D_MODEL), lambda i: (0, 0)),
                  pl.BlockSpec((1, D_MODEL), lambda i: (0, 0))],
        out_specs=pl.BlockSpec((CB_TM, D_MODEL), lambda i: (i, 0)),
        out_shape=jax.ShapeDtypeStruct((t, D_MODEL), f32),
        scratch_shapes=[pltpu.VMEM((TOP_K * CB_TM * ROW_SUB, LANES), i32),
                        pltpu.VMEM((TOP_K * CB_TM * ROW_SUB, LANES), i32),
                        pltpu.SemaphoreType.DMA((2,))],
        compiler_params=_params(("arbitrary",)),
        name="combine",
    )(dest3, dest3, yb, rw, x1, g2, ln2g, ln2b)


def _t5_bucket(dist):
    d = dist.astype(f32)
    large = REL_MAX_EXACT + jnp.log(jnp.maximum(d, float(REL_MAX_EXACT)) / REL_MAX_EXACT) / math.log(
        REL_MAX_DIST / REL_MAX_EXACT) * (REL_BUCKETS - REL_MAX_EXACT)
    large = jnp.minimum(large.astype(i32), REL_BUCKETS - 1)
    return jnp.where(dist < REL_MAX_EXACT, dist, large)


def _bias_indices():
    qi = jnp.arange(ATT_BLOCK)[:, None]
    ki = jnp.arange(2 * ATT_BLOCK)[None, :]
    didx = qi + ATT_BLOCK - ki
    buckets, bands = [], []
    for win, dil in DIL_PAIRS:
        buckets.append(_t5_bucket(jnp.clip(didx, 0, None) * dil))
        bands.append(((didx >= 0) & (didx <= win // dil)).astype(i32))
    return jnp.stack(buckets).astype(i32), jnp.stack(bands)


def _residue_perm(tm, dil):
    n = tm // dil
    dst = jnp.arange(tm)
    src = (dst % n) * dil + dst // n
    return (src[:, None] == jnp.arange(tm)[None, :]).astype(bf16)


def kernel(x, c, w_ada, b_ada, w_in, gm_ln_g, gm_ln_b, gm_w_s, gm_b_s, w_branch_a, w_branch_b, w_out,
           rel_bias, ln1_g, ln1_b, w_router, b_router, w_gate, b_gate, w_up, b_up, w_down, b_down,
           ln2_g, ln2_b):
    batch, seq, _ = x.shape
    t = batch * seq
    l = 0
    x2 = x.reshape(t, D_MODEL)

    c8 = jnp.pad(c, ((0, 8 - batch), (0, 0)))
    mod = _adaln(c8, w_ada[l], b_ada[l][None, :])[:batch]
    sh1, sc1, g1, sh2, sc2, g2 = [m[:, None, :] for m in jnp.split(mod, 6, axis=-1)]

    perms = [_residue_perm(IN_TM, dil) for _win, dil in DIL_PAIRS]
    uv, gates, *qkvs = _inproj(x2, sc1, sh1, w_in[l].astype(bf16), perms, batch, seq)

    bs_full = jnp.repeat(gm_b_s[l].T, GM_WIDTH // GM_GROUPS, axis=1)
    ya = _gmlp(uv, gm_ln_g[l][None, :], gm_ln_b[l][None, :], gm_w_s[l], bs_full)

    bucket, band = _bias_indices()
    bias = _relbias(rel_bias, bucket, band)
    os_, ls_ = [], []
    for g, (_win, dil) in enumerate(DIL_PAIRS):
        o, lse = _attn_group(qkvs[g], bias, g, dil, batch, seq)
        os_.append(o)
        ls_.append(lse)

    wr = jnp.pad(w_router[l], ((0, 0), (0, LANES - N_EXPERTS)))
    wr_hi = wr.astype(bf16)
    wr_lo = (wr - wr_hi.astype(f32)).astype(bf16)
    br = jnp.pad(b_router[l], (0, LANES - N_EXPERTS))[None, :]
    tri = (jnp.arange(MIX_TM)[None, :] < jnp.arange(MIX_TM)[:, None]).astype(bf16)
    perms_t = [_residue_perm(MIX_TM, dil).T for _win, dil in DIL_PAIRS]
    expand = (jnp.arange(LANES)[:, None] == jnp.arange(ATT_WIDTH)[None, :] // HEAD_DIM).astype(bf16)
    x1, h2, route, rw, cnt = _mix(
        os_, ls_, perms_t, expand, ya, gates, x2, g1, sc2, sh2,
        w_branch_a[l].astype(bf16), w_branch_b[l].astype(bf16), w_out[l].astype(bf16),
        ln1_g[l][None, :], ln1_b[l][None, :], wr_hi, wr_lo, br, tri, seq)

    top_e = route[:, :TOP_K]
    rank = route[:, TOP_K:2 * TOP_K]
    counts = cnt[0, :N_EXPERTS].astype(i32)
    pcounts = (counts + MOE_TM - 1) // MOE_TM * MOE_TM
    pends = jnp.cumsum(pcounts)
    pstarts = pends - pcounts
    experts = jnp.arange(N_EXPERTS, dtype=i32)
    dest = jnp.sum(jnp.where(top_e[:, :, None] == experts, pstarts, 0), axis=-1) + rank
    ntile = t * TOP_K // MOE_TM + N_EXPERTS
    n_used = (pends[-1] // MOE_TM).reshape(1)
    tile_idx = jnp.minimum(jnp.arange(ntile, dtype=i32), n_used - 1)
    tile_e = jnp.sum((pends[None, :] <= (tile_idx * MOE_TM)[:, None]).astype(i32), axis=1)
    tile_first = jnp.concatenate([jnp.ones((1,), i32), (tile_e[1:] != tile_e[:-1]).astype(i32)])
    nonempty = counts > 0
    later = lax.cummin(jnp.where(nonempty, experts, N_EXPERTS), reverse=True)
    next_nonempty = jnp.concatenate([later[1:], jnp.full((1,), N_EXPERTS, i32)])
    next_nonempty = jnp.where(next_nonempty >= N_EXPERTS, -1, next_nonempty)
    expert_slot = (jnp.cumsum(nonempty.astype(i32)) - 1) % 2

    dest3 = dest.reshape(t // DISP_TM, DISP_TM, TOP_K).transpose(0, 2, 1)
    xs = _dispatch(pends, pcounts, n_used, dest3, h2, ntile)
    yb = _moe(tile_e, tile_first, next_nonempty[tile_e], expert_slot[tile_e], n_used, xs,
              w_gate[l], b_gate[l][:, None, :], w_up[l], b_up[l][:, None, :],
              w_down[l], b_down[l][:, None, :])
    out = _combine(dest3, yb, rw, x1, g2, ln2_g[l][None, :], ln2_b[l][None, :], seq)
    return out.reshape(batch, seq, D_MODEL)
```

```python
import functools
import math

import jax
import jax.numpy as jnp
from jax import lax
from jax.experimental import pallas as pl
from jax.experimental.pallas import tpu as pltpu

f32 = jnp.float32
bf16 = jnp.bfloat16
i32 = jnp.int32

D_MODEL = 1024
GM_WIDTH = 512
GM_GROUPS = 8
GM_CHUNK = 128
DIL_PAIRS = ((128, 1), (512, 4), (2048, 16))
N_DIL = 3
HEADS_PER_GROUP = 8
HEAD_DIM = 64
ATT_WIDTH = 512
ATT_BLOCK = 128
NEG_INF = -1e30
REL_BUCKETS = 32
REL_MAX_EXACT = 16
REL_MAX_DIST = 2048
N_EXPERTS = 32
TOP_K = 4
SWIGLU_LIMIT = 7.0
SWIGLU_ALPHA = 1.702
MOE_BLOCK = 128
DEPTH = 1
DN_ALPHA = (2 * DEPTH) ** 0.25
LN_EPS = 1e-5
UV_COLS = 2 * GM_WIDTH
QKV_COLS = N_DIL * 3 * ATT_WIDTH
GATE_COLS = 2 * D_MODEL
IN_COLS = UV_COLS + QKV_COLS + GATE_COLS

LANES = 128
SUBLANES = 8
VMEM_LIMIT = 56 * 1024 * 1024


def _ln(x):
    mu = jnp.mean(x, axis=-1, keepdims=True)
    xc = x - mu
    var = jnp.mean(xc * xc, axis=-1, keepdims=True)
    return xc * lax.rsqrt(var + LN_EPS)


def _params(sem, vmem=VMEM_LIMIT):
    return pltpu.CompilerParams(dimension_semantics=sem, vmem_limit_bytes=vmem)


def _adaln_kernel(c_ref, w_ref, b_ref, o_ref):
    c = c_ref[...]
    s = c * jax.nn.sigmoid(c)
    o_ref[...] = jnp.dot(s, w_ref[...], preferred_element_type=f32,
                         precision=lax.Precision.HIGHEST) + b_ref[...]


def _adaln(c8, w_ada, b_ada):
    n = w_ada.shape[1] // D_MODEL
    return pl.pallas_call(
        _adaln_kernel,
        grid=(n,),
        in_specs=[pl.BlockSpec((8, D_MODEL), lambda j: (0, 0)),
                  pl.BlockSpec((D_MODEL, D_MODEL), lambda j: (0, j)),
                  pl.BlockSpec((1, D_MODEL), lambda j: (0, j))],
        out_specs=pl.BlockSpec((8, D_MODEL), lambda j: (0, j)),
        out_shape=jax.ShapeDtypeStruct((8, w_ada.shape[1]), f32),
        compiler_params=_params(("arbitrary",)),
        name="adaln",
    )(c8, w_ada, b_ada)


IN_TM = 256
IN_CW = 512
GRP_COLS = 3 * ATT_WIDTH


def _inproj_kernel(x_ref, sc_ref, sh_ref, w_ref, p1_ref, p2_ref,
                   uv_ref, gt_ref, qkv0_ref, qkv1_ref, qkv2_ref):
    xn = _ln(x_ref[...])
    h = (xn * (1.0 + sc_ref[0]) + sh_ref[0]).astype(bf16)
    hp = [h,
          jnp.dot(p1_ref[...], h, preferred_element_type=f32).astype(bf16),
          jnp.dot(p2_ref[...], h, preferred_element_type=f32).astype(bf16)]
    for c0 in range(0, UV_COLS, IN_CW):
        acc = jnp.dot(h, w_ref[:, c0:c0 + IN_CW], preferred_element_type=f32)
        uv_ref[:, c0:c0 + IN_CW] = jax.nn.gelu(acc).astype(bf16)
    for g, (qref, (_win, dil)) in enumerate(zip((qkv0_ref, qkv1_ref, qkv2_ref), DIL_PAIRS)):
        n = IN_TM // dil
        for q0 in range(0, GRP_COLS, IN_CW):
            c0 = UV_COLS + g * GRP_COLS + q0
            acc = jnp.dot(hp[g], w_ref[:, c0:c0 + IN_CW], preferred_element_type=f32).astype(bf16)
            for rho in range(dil):
                qref[0, rho, :, q0:q0 + IN_CW] = acc[rho * n:(rho + 1) * n, :]
    for g0 in range(0, GATE_COLS, IN_CW):
        c0 = UV_COLS + QKV_COLS + g0
        acc = jnp.dot(h, w_ref[:, c0:c0 + IN_CW], preferred_element_type=f32)
        gt_ref[:, g0:g0 + IN_CW] = jax.nn.sigmoid(acc).astype(bf16)


def _inproj(x2, sc1, sh1, w_in_bf, perms, batch, seq):
    t = x2.shape[0]
    per_b = seq // IN_TM
    qkv_specs, qkv_shapes = [], []
    for _win, dil in DIL_PAIRS:
        n = IN_TM // dil
        qkv_specs.append(pl.BlockSpec((1, dil, n, GRP_COLS), lambda i: (i // per_b, 0, i % per_b, 0)))
        qkv_shapes.append(jax.ShapeDtypeStruct((batch, dil, seq // dil, GRP_COLS), bf16))
    return pl.pallas_call(
        _inproj_kernel,
        grid=(t // IN_TM,),
        in_specs=[pl.BlockSpec((IN_TM, D_MODEL), lambda i: (i, 0)),
                  pl.BlockSpec((1, 1, D_MODEL), lambda i: (i // per_b, 0, 0)),
                  pl.BlockSpec((1, 1, D_MODEL), lambda i: (i // per_b, 0, 0)),
                  pl.BlockSpec((D_MODEL, IN_COLS), lambda i: (0, 0)),
                  pl.BlockSpec((IN_TM, IN_TM), lambda i: (0, 0)),
                  pl.BlockSpec((IN_TM, IN_TM), lambda i: (0, 0))],
        out_specs=[pl.BlockSpec((IN_TM, UV_COLS), lambda i: (i, 0)),
                   pl.BlockSpec((IN_TM, GATE_COLS), lambda i: (i, 0))] + qkv_specs,
        out_shape=[jax.ShapeDtypeStruct((t, UV_COLS), bf16),
                   jax.ShapeDtypeStruct((t, GATE_COLS), bf16)] + qkv_shapes,
        compiler_params=_params(("arbitrary",)),
        name="inproj",
    )(x2, sc1, sh1, w_in_bf, perms[1], perms[2])


GM_TM = 512


def _gmlp_kernel(u_ref, v_ref, g_ref, b_ref, ws_ref, bs_ref, ya_ref):
    row = lax.broadcasted_iota(i32, (GM_CHUNK, GM_CHUNK), 0)
    col = lax.broadcasted_iota(i32, (GM_CHUNK, GM_CHUNK), 1)
    causal = col <= row
    first_half = lax.broadcasted_iota(i32, (GM_CHUNK, LANES), 1) < (GM_WIDTH // GM_GROUPS)
    ws = [jnp.where(causal, ws_ref[g], 0.0).astype(bf16) for g in range(GM_GROUPS)]
    for ch in range(GM_TM // GM_CHUNK):
        r0 = ch * GM_CHUNK
        vn = _ln(v_ref[r0:r0 + GM_CHUNK, :].astype(f32)) * g_ref[...] + b_ref[...]
        vn = vn.astype(bf16)
        for j in range(GM_WIDTH // LANES):
            slab = vn[:, j * LANES:(j + 1) * LANES]
            s_lo = jnp.dot(ws[2 * j], slab, preferred_element_type=f32)
            s_hi = jnp.dot(ws[2 * j + 1], slab, preferred_element_type=f32)
            s = jnp.where(first_half, s_lo, s_hi) + bs_ref[:, j * LANES:(j + 1) * LANES]
            u = u_ref[r0:r0 + GM_CHUNK, j * LANES:(j + 1) * LANES].astype(f32)
            ya_ref[r0:r0 + GM_CHUNK, j * LANES:(j + 1) * LANES] = (u * s).astype(bf16)


def _gmlp(uv, ln_g, ln_b, w_s, bs_full):
    t = uv.shape[0]
    return pl.pallas_call(
        _gmlp_kernel,
        grid=(t // GM_TM,),
        in_specs=[pl.BlockSpec((GM_TM, GM_WIDTH), lambda i: (i, 0)),
                  pl.BlockSpec((GM_TM, GM_WIDTH), lambda i: (i, 1)),
                  pl.BlockSpec((1, GM_WIDTH), lambda i: (0, 0)),
                  pl.BlockSpec((1, GM_WIDTH), lambda i: (0, 0)),
                  pl.BlockSpec((GM_GROUPS, GM_CHUNK, GM_CHUNK), lambda i: (0, 0, 0)),
                  pl.BlockSpec((GM_CHUNK, GM_WIDTH), lambda i: (0, 0))],
        out_specs=pl.BlockSpec((GM_TM, GM_WIDTH), lambda i: (i, 0)),
        out_shape=jax.ShapeDtypeStruct((t, GM_WIDTH), bf16),
        compiler_params=_params(("arbitrary",)),
        name="gmlp",
    )(uv, uv, ln_g, ln_b, w_s, bs_full)


def _relbias_kernel(tab_ref, bucket_ref, band_ref, out_ref):
    g = pl.program_id(0)
    bk = bucket_ref[0]
    band = band_ref[0] > 0
    for h in range(HEADS_PER_GROUP):
        acc = jnp.zeros((ATT_BLOCK, 2 * ATT_BLOCK), f32)
        for b in range(REL_BUCKETS):
            acc = jnp.where(bk == b, tab_ref[b, g * HEADS_PER_GROUP + h], acc)
        out_ref[0, h] = jnp.where(band, acc, NEG_INF)


def _relbias(rel_bias, bucket, band):
    return pl.pallas_call(
        _relbias_kernel,
        grid=(N_DIL,),
        in_specs=[pl.BlockSpec(memory_space=pltpu.SMEM),
                  pl.BlockSpec((1, ATT_BLOCK, 2 * ATT_BLOCK), lambda g: (g, 0, 0)),
                  pl.BlockSpec((1, ATT_BLOCK, 2 * ATT_BLOCK), lambda g: (g, 0, 0))],
        out_specs=pl.BlockSpec((1, HEADS_PER_GROUP, ATT_BLOCK, 2 * ATT_BLOCK),
                               lambda g: (g, 0, 0, 0)),
        out_shape=jax.ShapeDtypeStruct((N_DIL, HEADS_PER_GROUP, ATT_BLOCK, 2 * ATT_BLOCK), f32),
        compiler_params=_params(("arbitrary",)),
        name="relbias",
    )(rel_bias, bucket, band)


ATT_MAX_STEP_BLOCKS = 4


def _attn_kernel(nblk, q_ref, kp_ref, kc_ref, vp_ref, vc_ref, bias_ref, o_ref, lse_ref):
    first = pl.program_id(2) == 0
    lane = lax.broadcasted_iota(i32, (ATT_BLOCK, LANES), 1)
    lo_half = lane < HEAD_DIM
    nt = (((1,), (1,)), ((), ()))
    ones = jnp.ones((2 * ATT_BLOCK, LANES), bf16)
    n_slab = ATT_WIDTH // LANES
    logits, v_ext = [], []
    for i in range(nblk):
        cur = slice(i * ATT_BLOCK, (i + 1) * ATT_BLOCK)
        prv = slice((i - 1) * ATT_BLOCK, i * ATT_BLOCK)
        for j in range(n_slab):
            sl = slice(j * LANES, (j + 1) * LANES)
            q = q_ref[0, 0, cur, sl] * (HEAD_DIM ** -0.5)
            k_prev = kp_ref[0, 0, :, sl] if i == 0 else kc_ref[0, 0, prv, sl]
            v_prev = vp_ref[0, 0, :, sl] if i == 0 else vc_ref[0, 0, prv, sl]
            k_cat = jnp.concatenate([k_prev, kc_ref[0, 0, cur, sl]], axis=0)
            v_cat = jnp.concatenate([v_prev, vc_ref[0, 0, cur, sl]], axis=0)
            v_ext.append(jnp.concatenate([v_cat, ones], axis=1))
            for hh in range(2):
                qm = jnp.where(lo_half if hh == 0 else jnp.logical_not(lo_half), q, 0.0).astype(bf16)
                logits.append(lax.dot_general(qm, k_cat, nt, preferred_element_type=f32))
    bias = bias_ref[0].reshape(HEADS_PER_GROUP * ATT_BLOCK, 2 * ATT_BLOCK)
    rows_per_block = HEADS_PER_GROUP * ATT_BLOCK
    lg = jnp.concatenate(logits, axis=0) + jnp.concatenate([bias] * nblk, axis=0)
    row = lax.broadcasted_iota(i32, lg.shape, 0)
    col = lax.broadcasted_iota(i32, lg.shape, 1)
    no_prev = jnp.logical_and(first, jnp.logical_and(row < rows_per_block, col < ATT_BLOCK))
    lg = jnp.where(no_prev, NEG_INF, lg)
    m = jnp.max(lg, axis=-1, keepdims=True)
    p = jnp.exp(lg - m).astype(bf16)
    for i in range(nblk):
        cur = slice(i * ATT_BLOCK, (i + 1) * ATT_BLOCK)
        lse_tile = jnp.zeros((ATT_BLOCK, LANES), f32)
        for j in range(n_slab):
            outs = []
            for hh in range(2):
                h = 2 * j + hh
                r0 = i * rows_per_block + h * ATT_BLOCK
                r = jnp.dot(p[r0:r0 + ATT_BLOCK], v_ext[i * n_slab + j], preferred_element_type=f32)
                den = r[:, LANES:]
                outs.append(r[:, :LANES] * (1.0 / den))
                lse_h = m[r0:r0 + ATT_BLOCK] + jnp.log(den)
                lse_tile = jnp.where(lane == h, lse_h, lse_tile)
            o_ref[0, 0, cur, j * LANES:(j + 1) * LANES] = jnp.where(lo_half, outs[0], outs[1]).astype(bf16)
        lse_ref[0, 0, cur, :] = lse_tile


def _attn_group(qkv_g, bias, g, dil, batch, seq):
    l = seq // dil
    nblk = min(ATT_MAX_STEP_BLOCKS, l // ATT_BLOCK)
    tm = nblk * ATT_BLOCK
    nsteps = l // tm

    def cur(cb):
        return pl.BlockSpec((1, 1, tm, ATT_WIDTH), lambda b, r, n: (b, r, n, cb))

    def prev(cb):
        return pl.BlockSpec((1, 1, ATT_BLOCK, ATT_WIDTH),
                            lambda b, r, n: (b, r, jnp.maximum(n * nblk - 1, 0), cb))

    return pl.pallas_call(
        functools.partial(_attn_kernel, nblk),
        grid=(batch, dil, nsteps),
        in_specs=[cur(0), prev(1), cur(1), prev(2), cur(2),
                  pl.BlockSpec((1, HEADS_PER_GROUP, ATT_BLOCK, 2 * ATT_BLOCK),
                               lambda b, r, n: (g, 0, 0, 0))],
        out_specs=[pl.BlockSpec((1, 1, tm, ATT_WIDTH), lambda b, r, n: (b, r, n, 0)),
                   pl.BlockSpec((1, 1, tm, LANES), lambda b, r, n: (b, r, n, 0))],
        out_shape=[jax.ShapeDtypeStruct((batch, dil, l, ATT_WIDTH), bf16),
                   jax.ShapeDtypeStruct((batch, dil, l, LANES), f32)],
        compiler_params=_params(("arbitrary", "arbitrary", "arbitrary")),
        name=f"attn_g{g}",
    )(qkv_g, qkv_g, qkv_g, qkv_g, qkv_g, bias)


ROW_WORDS = D_MODEL // 2
ROW_SUB = ROW_WORDS // LANES
HI_MASK = -65536


def _pack_rows(x):
    bits = lax.bitcast_convert_type(x.astype(bf16).astype(f32), i32)
    return lax.shift_right_logical(bits[:, :ROW_WORDS], 16) | (bits[:, ROW_WORDS:] & HI_MASK)


def _unpack_rows(words):
    lo = lax.bitcast_convert_type(lax.shift_left(words, 16), f32)
    hi = lax.bitcast_convert_type(words & HI_MASK, f32)
    return jnp.concatenate([lo, hi], axis=1)


def _store_packed(ref, words, n, first_row=0):
    for r in range(ROW_SUB):
        ref[pl.ds(first_row * ROW_SUB + r, n, stride=ROW_SUB), :] = words[:, r * LANES:(r + 1) * LANES]


def _load_packed(ref, first_row, n):
    return jnp.concatenate([ref[pl.ds(first_row * ROW_SUB + r, n, stride=ROW_SUB), :] for r in range(ROW_SUB)],
                           axis=1)


MIX_TM = 256
MIX_SUB = 128


def _split_bf16(x, parts):
    out = []
    for _ in range(parts):
        hi = x.astype(bf16)
        out.append(hi)
        x = x - hi.astype(f32)
    return out


def _mix_kernel(o0_ref, o1_ref, o2_ref, l0_ref, l1_ref, l2_ref, pt1_ref, pt2_ref, ex_ref,
                ya_ref, gt_ref, x_ref,
                g1_ref, sc2_ref, sh2_ref, wa_ref, wb_ref, wo_ref, ln1g_ref, ln1b_ref,
                wrc_ref, br_ref, tri_ref,
                x1_ref, h2_ref, route_ref, rw_ref, cnt_ref, run_ref):
    @pl.when(pl.program_id(0) == 0)
    def _():
        run_ref[...] = jnp.zeros_like(run_ref)

    def token_rows(r0):
        rows = slice(r0, r0 + MIX_SUB)
        os_, ls_ = [o0_ref[0, 0, rows, :].astype(f32)], [l0_ref[0, 0, rows, :]]
        for o_ref, l_ref, pt_ref in ((o1_ref, l1_ref, pt1_ref), (o2_ref, l2_ref, pt2_ref)):
            pt = pt_ref[rows, :]
            os_.append(jnp.dot(pt, o_ref[0].reshape(MIX_TM, ATT_WIDTH), preferred_element_type=f32))
            parts = [jnp.dot(pt, part, preferred_element_type=f32)
                     for part in _split_bf16(l_ref[0].reshape(MIX_TM, LANES), 3)]
            ls_.append((parts[0] + parts[1]) + parts[2])
        lm = jnp.maximum(jnp.maximum(ls_[0], ls_[1]), ls_[2])
        es = [jnp.exp(lse - lm) for lse in ls_]
        inv = 1.0 / (es[0] + es[1] + es[2])
        yb = jnp.zeros((MIX_SUB, ATT_WIDTH), f32)
        for e, o in zip(es, os_):
            w_parts = jnp.concatenate(_split_bf16(e * inv, 2), axis=1)
            yb = yb + jnp.dot(w_parts, ex_ref[...], preferred_element_type=f32) * o
        a = jnp.dot(ya_ref[rows, :], wa_ref[...], preferred_element_type=f32)
        b = jnp.dot(yb.astype(bf16), wb_ref[...], preferred_element_type=f32)
        merged = gt_ref[rows, :D_MODEL].astype(f32) * a + gt_ref[rows, D_MODEL:].astype(f32) * b
        mix = jnp.dot(merged.astype(bf16), wo_ref[...], preferred_element_type=f32)
        x1 = _ln(DN_ALPHA * x_ref[rows, :] + g1_ref[0] * mix) * ln1g_ref[...] + ln1b_ref[...]
        x1_ref[rows, :] = x1
        h2 = _ln(x1) * (1.0 + sc2_ref[0]) + sh2_ref[0]
        _store_packed(h2_ref, _pack_rows(h2), MIX_SUB, r0)

        h_hi, h_lo = _split_bf16(h2, 2)
        hi_both = jnp.dot(h_hi, wrc_ref[...], preferred_element_type=f32)
        return (hi_both[:, :LANES]
                + (hi_both[:, LANES:] + jnp.dot(h_lo, wrc_ref[:, :LANES], preferred_element_type=f32))
                ) + br_ref[...]

    logits = jnp.concatenate([token_rows(r0) for r0 in range(0, MIX_TM, MIX_SUB)], axis=0)

    lane = lax.broadcasted_iota(i32, (MIX_TM, LANES), 1)
    logits = jnp.where(lane < N_EXPERTS, logits, -jnp.inf)
    vals, idxs = [], []
    for _k in range(TOP_K):
        vals.append(jnp.max(logits, axis=-1, keepdims=True))
        idxs.append(jnp.argmax(logits, axis=-1, keepdims=True).astype(i32))
        logits = jnp.where(lane == idxs[-1], -jnp.inf, logits)
    exps = [jnp.exp(v - vals[0]) for v in vals]
    den = exps[0] + exps[1] + exps[2] + exps[3]
    wts = [e / den for e in exps]
    hits = [lane == idx for idx in idxs]
    onehot = jnp.zeros((MIX_TM, LANES), f32)
    for hit in hits:
        onehot = onehot + jnp.where(hit, 1.0, 0.0)
    prefix = jnp.dot(tri_ref[...], onehot.astype(bf16), preferred_element_type=f32) + run_ref[...]
    route = jnp.zeros((MIX_TM, LANES), i32)
    rw = jnp.zeros((MIX_TM, LANES), f32)
    for k in range(TOP_K):
        rank = jnp.sum(jnp.where(hits[k], prefix, 0.0), axis=-1, keepdims=True).astype(i32)
        route = jnp.where(lane == k, idxs[k], route)
        route = jnp.where(lane == TOP_K + k, rank, route)
        rw = jnp.where(lane == k, wts[k], rw)
    route_ref[...] = route
    rw_ref[...] = rw
    run = run_ref[...] + jnp.sum(onehot, axis=0, keepdims=True)
    run_ref[...] = run
    cnt_ref[...] = jnp.broadcast_to(run, cnt_ref.shape)


def _mix(os_, ls_, perms_t, expand, ya, gates, x2, g1, sc2, sh2, wa, wb, wo, ln1g, ln1b, wr_parts, br, tri, seq):
    t = x2.shape[0]
    per_b = seq // MIX_TM
    row = lambda w: pl.BlockSpec((MIX_TM, w), lambda i: (i, 0))
    const = lambda s: pl.BlockSpec(s, lambda i: tuple(0 for _ in s))
    modb = pl.BlockSpec((1, 1, D_MODEL), lambda i: (i // per_b, 0, 0))
    grp = lambda w: [pl.BlockSpec((1, dil, MIX_TM // dil, w), lambda i: (i // per_b, 0, i % per_b, 0))
                     for _win, dil in DIL_PAIRS]
    return pl.pallas_call(
        _mix_kernel,
        grid=(t // MIX_TM,),
        in_specs=grp(ATT_WIDTH) + grp(LANES) + [
                  const((MIX_TM, MIX_TM)), const((MIX_TM, MIX_TM)), const((2 * LANES, ATT_WIDTH)),
                  row(GM_WIDTH), row(GATE_COLS), row(D_MODEL),
                  modb, modb, modb,
                  const((GM_WIDTH, D_MODEL)), const((ATT_WIDTH, D_MODEL)), const((D_MODEL, D_MODEL)),
                  const((1, D_MODEL)), const((1, D_MODEL)),
                  const((D_MODEL, 2 * LANES)), const((1, LANES)), const((MIX_TM, MIX_TM))],
        out_specs=[row(D_MODEL), pl.BlockSpec((MIX_TM * ROW_SUB, LANES), lambda i: (i, 0)),
                   row(LANES), row(LANES), const((8, LANES))],
        out_shape=[jax.ShapeDtypeStruct((t, D_MODEL), f32),
                   jax.ShapeDtypeStruct((t * ROW_SUB, LANES), i32),
                   jax.ShapeDtypeStruct((t, LANES), i32),
                   jax.ShapeDtypeStruct((t, LANES), f32),
                   jax.ShapeDtypeStruct((8, LANES), f32)],
        scratch_shapes=[pltpu.VMEM((1, LANES), f32)],
        compiler_params=_params(("arbitrary",)),
        name="mix",
    )(*os_, *ls_, perms_t[1], perms_t[2], expand, ya, gates, x2, g1, sc2, sh2, wa, wb, wo,
      ln1g, ln1b, wr_parts, br, tri)


DISP_TM = 256
MOE_TM = 256


def _dispatch_kernel(pends_ref, pcnt_ref, nused_ref, dest_ref, h2p_ref, xs_hbm, zbuf, sem, zsem):
    i = pl.program_id(0)
    ntile = xs_hbm.shape[0] // (MOE_TM * ROW_SUB)

    def zero_tile(first_row):
        return pltpu.make_async_copy(
            zbuf, xs_hbm.at[pl.ds(pl.multiple_of(first_row * ROW_SUB, MOE_TM * ROW_SUB), MOE_TM * ROW_SUB)], zsem)

    def for_each_zero_tile(fn):
        for e in range(N_EXPERTS):
            pl.when(pcnt_ref[e] > 0)(functools.partial(fn, lambda e=e: zero_tile(pends_ref[e] - MOE_TM)))
        for k in range(N_EXPERTS):
            tile = nused_ref[0] + k
            pl.when(tile < ntile)(functools.partial(fn, lambda tile=tile: zero_tile(tile * MOE_TM)))

    @pl.when(i == 0)
    def _():
        zbuf[...] = jnp.zeros_like(zbuf)
        for_each_zero_tile(lambda mk: mk().start())
        for_each_zero_tile(lambda mk: mk().wait())

    def row_copy(k, r):
        d = dest_ref[0, k, r]
        return pltpu.make_async_copy(h2p_ref.at[pl.ds(r * ROW_SUB, ROW_SUB)],
                                     xs_hbm.at[pl.ds(pl.multiple_of(d * ROW_SUB, ROW_SUB), ROW_SUB)], sem)

    for r in range(DISP_TM):
        for k in range(TOP_K):
            row_copy(k, r).start(priority=k % 2)
    for k in range(TOP_K):
        pltpu.make_async_copy(h2p_ref, xs_hbm.at[pl.ds(0, DISP_TM * ROW_SUB)], sem).wait()


def _dispatch(pends, pcounts, n_used, dest3, h2p, ntile):
    t = h2p.shape[0] // ROW_SUB
    grid_spec = pltpu.PrefetchScalarGridSpec(
        num_scalar_prefetch=3,
        grid=(t // DISP_TM,),
        in_specs=[pl.BlockSpec((1, TOP_K, DISP_TM), lambda i, *_: (i, 0, 0), memory_space=pltpu.SMEM),
                  pl.BlockSpec((DISP_TM * ROW_SUB, LANES), lambda i, *_: (i, 0))],
        out_specs=pl.BlockSpec(memory_space=pl.ANY),
        scratch_shapes=[pltpu.VMEM((MOE_TM * ROW_SUB, LANES), i32),
                        pltpu.SemaphoreType.DMA(()),
                        pltpu.SemaphoreType.DMA(())],
    )
    return pl.pallas_call(
        _dispatch_kernel,
        grid_spec=grid_spec,
        out_shape=jax.ShapeDtypeStruct((ntile * MOE_TM * ROW_SUB, LANES), i32),
        compiler_params=_params(("arbitrary",)),
        name="dispatch",
    )(pends, pcounts, n_used, dest3, h2p)


def _moe_kernel(te_ref, first_ref, nexte_ref, wslot_ref, nused_ref,
                xs_ref, wg_hbm, wu_hbm, wd_hbm, bg_ref, bu_ref, bd_ref,
                out_ref, wbuf, wgb, wub, wdb, sem_w):
    j = pl.program_id(0)

    def weight_copies(e, ws):
        return [pltpu.make_async_copy(w.at[e], wbuf.at[ws, k], sem_w.at[ws])
                for k, w in enumerate((wg_hbm, wu_hbm, wd_hbm))]

    @pl.when(j == 0)
    def _():
        for cp in weight_copies(te_ref[0], wslot_ref[0]):
            cp.start()

    @pl.when(first_ref[j] == 1)
    def _():
        ws = wslot_ref[j]
        for cp in weight_copies(te_ref[j], ws):
            cp.wait()
        wgb[...] = wbuf[ws, 0].astype(bf16)
        wub[...] = wbuf[ws, 1].astype(bf16)
        wdb[...] = wbuf[ws, 2].astype(bf16)
        ne = nexte_ref[j]

        @pl.when(ne >= 0)
        def _():
            for cp in weight_copies(ne, 1 - ws):
                cp.start()

    used = j < nused_ref[0]

    @pl.when(used)
    def _():
        xb = _unpack_rows(_load_packed(xs_ref, 0, MOE_TM)).astype(bf16)
        g = jnp.dot(xb, wgb[...], preferred_element_type=f32) + bg_ref[0]
        u = jnp.dot(xb, wub[...], preferred_element_type=f32) + bu_ref[0]
        g = jnp.minimum(g, SWIGLU_LIMIT)
        u = jnp.clip(u, -SWIGLU_LIMIT, SWIGLU_LIMIT)
        act = (u + 1.0) * (g * jax.nn.sigmoid(SWIGLU_ALPHA * g))
        y = jnp.dot(act.astype(bf16), wdb[...], preferred_element_type=f32) + bd_ref[0]
        _store_packed(out_ref, _pack_rows(y), MOE_TM)

    @pl.when(jnp.logical_not(used))
    def _():
        out_ref[...] = jnp.zeros_like(out_ref)


def _moe(tile_e, tile_first, next_e, wslot, n_used, xs, w_gate, b_gate, w_up, b_up, w_down, b_down):
    ntile = tile_e.shape[0]
    bspec = pl.BlockSpec((1, 1, D_MODEL), lambda j, te, *_: (te[j], 0, 0))
    hbm = pl.BlockSpec(memory_space=pl.ANY)
    grid_spec = pltpu.PrefetchScalarGridSpec(
        num_scalar_prefetch=5,
        grid=(ntile,),
        in_specs=[pl.BlockSpec((MOE_TM * ROW_SUB, LANES),
                               lambda j, te, fi, ne, ws, nu: (jnp.minimum(j, nu[0] - 1), 0)),
                  hbm, hbm, hbm, bspec, bspec, bspec],
        out_specs=pl.BlockSpec((MOE_TM * ROW_SUB, LANES), lambda j, *_: (j, 0)),
        scratch_shapes=[pltpu.VMEM((2, 3, D_MODEL, D_MODEL), f32),
                        pltpu.VMEM((D_MODEL, D_MODEL), bf16),
                        pltpu.VMEM((D_MODEL, D_MODEL), bf16),
                        pltpu.VMEM((D_MODEL, D_MODEL), bf16),
                        pltpu.SemaphoreType.DMA((2,))],
    )
    return pl.pallas_call(
        _moe_kernel,
        grid_spec=grid_spec,
        out_shape=jax.ShapeDtypeStruct((ntile * MOE_TM * ROW_SUB, LANES), i32),
        compiler_params=_params(("arbitrary",)),
        name="moe",
    )(tile_e, tile_first, next_e, wslot, n_used, xs, w_gate, w_up, w_down, b_gate, b_up, b_down)


CB_TM = 256


def _combine_kernel(dcur_ref, dnxt_ref, yb_hbm, rw_ref, x1_ref, g2_ref, lng_ref, lnb_ref, out_ref,
                    ybuf0, ybuf1, sem):
    i = pl.program_id(0)
    last = pl.num_programs(0) - 1
    ybufs = (ybuf0, ybuf1)

    def row_copy(d, k, r, s):
        return pltpu.make_async_copy(
            yb_hbm.at[pl.ds(pl.multiple_of(d * ROW_SUB, ROW_SUB), ROW_SUB)],
            ybufs[s].at[pl.ds(pl.multiple_of((k * CB_TM + r) * ROW_SUB, ROW_SUB), ROW_SUB)],
            sem.at[s])

    @pl.when(i == 0)
    def _():
        for k in range(TOP_K):
            def body(r, c, k=k):
                row_copy(dcur_ref[0, k, r], k, r, 0).start()
                return c
            lax.fori_loop(0, CB_TM, body, 0, unroll=8)

    for s in range(2):
        @pl.when(i % 2 == s)
        def _(s=s):
            pltpu.make_async_copy(yb_hbm.at[pl.ds(0, TOP_K * CB_TM * ROW_SUB)], ybufs[s], sem.at[s]).wait()

            @pl.when(i < last)
            def _():
                for k in range(TOP_K):
                    for r in range(CB_TM):
                        row_copy(dnxt_ref[0, k, r], k, r, 1 - s).start(priority=r % 2)

            parts = [_unpack_rows(_load_packed(ybufs[s], k * CB_TM, CB_TM)) * rw_ref[:, k:k + 1]
                     for k in range(TOP_K)]
            y = (parts[0] + parts[1]) + (parts[2] + parts[3])
            out_ref[...] = _ln(DN_ALPHA * x1_ref[...] + g2_ref[0] * y) * lng_ref[...] + lnb_ref[...]


def _combine(dest3, yb, rw, x1, g2, ln2g, ln2b, seq):
    t = x1.shape[0]
    nb = t // CB_TM
    per_b = seq // CB_TM
    return pl.pallas_call(
        _combine_kernel,
        grid=(nb,),
        in_specs=[pl.BlockSpec((1, TOP_K, CB_TM), lambda i: (i, 0, 0), memory_space=pltpu.SMEM),
                  pl.BlockSpec((1, TOP_K, CB_TM), lambda i: (jnp.minimum(i + 1, nb - 1), 0, 0),
                               memory_space=pltpu.SMEM),
                  pl.BlockSpec(memory_space=pl.ANY),
                  pl.BlockSpec((CB_TM, LANES), lambda i: (i, 0)),
                  pl.BlockSpec((CB_TM, D_MODEL), lambda i: (i, 0)),
                  pl.BlockSpec((1, 1, D_MODEL), lambda i: (i // per_b, 0, 0)),
                  pl.BlockSpec((1, D_MODEL), lambda i: (0, 0)),
                  pl.BlockSpec((1, D_MODEL), lambda i: (0, 0))],
        out_specs=pl.BlockSpec((CB_TM, D_MODEL), lambda i: (i, 0)),
        out_shape=jax.ShapeDtypeStruct((t, D_MODEL), f32),
        scratch_shapes=[pltpu.VMEM((TOP_K * CB_TM * ROW_SUB, LANES), i32),
                        pltpu.VMEM((TOP_K * CB_TM * ROW_SUB, LANES), i32),
                        pltpu.SemaphoreType.DMA((2,))],
        compiler_params=_params(("arbitrary",)),
        name="combine",
    )(dest3, dest3, yb, rw, x1, g2, ln2g, ln2b)


def _t5_bucket(dist):
    d = dist.astype(f32)
    large = REL_MAX_EXACT + jnp.log(jnp.maximum(d, float(REL_MAX_EXACT)) / REL_MAX_EXACT) / math.log(
        REL_MAX_DIST / REL_MAX_EXACT) * (REL_BUCKETS - REL_MAX_EXACT)
    large = jnp.minimum(large.astype(i32), REL_BUCKETS - 1)
    return jnp.where(dist < REL_MAX_EXACT, dist, large)


def _bias_indices():
    qi = jnp.arange(ATT_BLOCK)[:, None]
    ki = jnp.arange(2 * ATT_BLOCK)[None, :]
    didx = qi + ATT_BLOCK - ki
    buckets, bands = [], []
    for win, dil in DIL_PAIRS:
        buckets.append(_t5_bucket(jnp.clip(didx, 0, None) * dil))
        bands.append(((didx >= 0) & (didx <= win // dil)).astype(i32))
    return jnp.stack(buckets).astype(i32), jnp.stack(bands)


def _residue_perm(tm, dil):
    n = tm // dil
    dst = jnp.arange(tm)
    src = (dst % n) * dil + dst // n
    return (src[:, None] == jnp.arange(tm)[None, :]).astype(bf16)


def kernel(x, c, w_ada, b_ada, w_in, gm_ln_g, gm_ln_b, gm_w_s, gm_b_s, w_branch_a, w_branch_b, w_out,
           rel_bias, ln1_g, ln1_b, w_router, b_router, w_gate, b_gate, w_up, b_up, w_down, b_down,
           ln2_g, ln2_b):
    batch, seq, _ = x.shape
    t = batch * seq
    l = 0
    x2 = x.reshape(t, D_MODEL)

    c8 = jnp.pad(c, ((0, 8 - batch), (0, 0)))
    mod = _adaln(c8, w_ada[l], b_ada[l][None, :])[:batch]
    sh1, sc1, g1, sh2, sc2, g2 = [m[:, None, :] for m in jnp.split(mod, 6, axis=-1)]

    perms = [_residue_perm(IN_TM, dil) for _win, dil in DIL_PAIRS]
    uv, gates, *qkvs = _inproj(x2, sc1, sh1, w_in[l].astype(bf16), perms, batch, seq)

    bs_full = jnp.repeat(gm_b_s[l].T, GM_WIDTH // GM_GROUPS, axis=1)
    ya = _gmlp(uv, gm_ln_g[l][None, :], gm_ln_b[l][None, :], gm_w_s[l], bs_full)

    bucket, band = _bias_indices()
    bias = _relbias(rel_bias, bucket, band)
    os_, ls_ = [], []
    for g, (_win, dil) in enumerate(DIL_PAIRS):
        o, lse = _attn_group(qkvs[g], bias, g, dil, batch, seq)
        os_.append(o)
        ls_.append(lse)

    wr = jnp.pad(w_router[l], ((0, 0), (0, LANES - N_EXPERTS)))
    wr_hi = wr.astype(bf16)
    wr_parts = jnp.concatenate([wr_hi, (wr - wr_hi.astype(f32)).astype(bf16)], axis=1)
    br = jnp.pad(b_router[l], (0, LANES - N_EXPERTS))[None, :]
    tri = (jnp.arange(MIX_TM)[None, :] < jnp.arange(MIX_TM)[:, None]).astype(bf16)
    perms_t = [_residue_perm(MIX_TM, dil).T for _win, dil in DIL_PAIRS]
    expand = (jnp.arange(LANES)[:, None] == jnp.arange(ATT_WIDTH)[None, :] // HEAD_DIM).astype(bf16)
    expand = jnp.concatenate([expand, expand], axis=0)
    x1, h2, route, rw, cnt = _mix(
        os_, ls_, perms_t, expand, ya, gates, x2, g1, sc2, sh2,
        w_branch_a[l].astype(bf16), w_branch_b[l].astype(bf16), w_out[l].astype(bf16),
        ln1_g[l][None, :], ln1_b[l][None, :], wr_parts, br, tri, seq)

    top_e = route[:, :TOP_K]
    rank = route[:, TOP_K:2 * TOP_K]
    counts = cnt[0, :N_EXPERTS].astype(i32)
    pcounts = (counts + MOE_TM - 1) // MOE_TM * MOE_TM
    pends = jnp.cumsum(pcounts)
    pstarts = pends - pcounts
    experts = jnp.arange(N_EXPERTS, dtype=i32)
    dest = jnp.sum(jnp.where(top_e[:, :, None] == experts, pstarts, 0), axis=-1) + rank
    ntile = t * TOP_K // MOE_TM + N_EXPERTS
    n_used = (pends[-1] // MOE_TM).reshape(1)
    tile_idx = jnp.minimum(jnp.arange(ntile, dtype=i32), n_used - 1)
    tile_e = jnp.sum((pends[None, :] <= (tile_idx * MOE_TM)[:, None]).astype(i32), axis=1)
    tile_first = jnp.concatenate([jnp.ones((1,), i32), (tile_e[1:] != tile_e[:-1]).astype(i32)])
    nonempty = counts > 0
    later = lax.cummin(jnp.where(nonempty, experts, N_EXPERTS), reverse=True)
    next_nonempty = jnp.concatenate([later[1:], jnp.full((1,), N_EXPERTS, i32)])
    next_nonempty = jnp.where(next_nonempty >= N_EXPERTS, -1, next_nonempty)
    expert_slot = (jnp.cumsum(nonempty.astype(i32)) - 1) % 2

    dest3 = dest.reshape(t // DISP_TM, DISP_TM, TOP_K).transpose(0, 2, 1)
    xs = _dispatch(pends, pcounts, n_used, dest3, h2, ntile)
    yb = _moe(tile_e, tile_first, next_nonempty[tile_e], expert_slot[tile_e], n_used, xs,
              w_gate[l], b_gate[l][:, None, :], w_up[l], b_up[l][:, None, :],
              w_down[l], b_down[l][:, None, :])
    out = _combine(dest3, yb, rw, x1, g2, ln2_g[l][None, :], ln2_b[l][None, :], seq)
    return out.reshape(batch, seq, D_MODEL)
```

```python
import functools
import math

import jax
import jax.numpy as jnp
from jax import lax
from jax.experimental import pallas as pl
from jax.experimental.pallas import tpu as pltpu

f32 = jnp.float32
bf16 = jnp.bfloat16
i32 = jnp.int32

D_MODEL = 1024
GM_WIDTH = 512
GM_GROUPS = 8
GM_CHUNK = 128
DIL_PAIRS = ((128, 1), (512, 4), (2048, 16))
N_DIL = 3
HEADS_PER_GROUP = 8
HEAD_DIM = 64
ATT_WIDTH = 512
ATT_BLOCK = 128
NEG_INF = -1e30
REL_BUCKETS = 32
REL_MAX_EXACT = 16
REL_MAX_DIST = 2048
N_EXPERTS = 32
TOP_K = 4
SWIGLU_LIMIT = 7.0
SWIGLU_ALPHA = 1.702
MOE_BLOCK = 128
DEPTH = 1
DN_ALPHA = (2 * DEPTH) ** 0.25
LN_EPS = 1e-5
UV_COLS = 2 * GM_WIDTH
QKV_COLS = N_DIL * 3 * ATT_WIDTH
GATE_COLS = 2 * D_MODEL
IN_COLS = UV_COLS + QKV_COLS + GATE_COLS

LANES = 128
SUBLANES = 8
VMEM_LIMIT = 56 * 1024 * 1024


def _ln(x):
    mu = jnp.mean(x, axis=-1, keepdims=True)
    xc = x - mu
    var = jnp.mean(xc * xc, axis=-1, keepdims=True)
    return xc * lax.rsqrt(var + LN_EPS)


def _params(sem, vmem=VMEM_LIMIT):
    return pltpu.CompilerParams(dimension_semantics=sem, vmem_limit_bytes=vmem)


def _adaln_kernel(c_ref, w_ref, b_ref, o_ref):
    c = c_ref[...]
    s = c * jax.nn.sigmoid(c)
    o_ref[...] = jnp.dot(s, w_ref[...], preferred_element_type=f32,
                         precision=lax.Precision.HIGHEST) + b_ref[...]


def _adaln(c8, w_ada, b_ada):
    n = w_ada.shape[1] // D_MODEL
    return pl.pallas_call(
        _adaln_kernel,
        grid=(n,),
        in_specs=[pl.BlockSpec((8, D_MODEL), lambda j: (0, 0)),
                  pl.BlockSpec((D_MODEL, D_MODEL), lambda j: (0, j)),
                  pl.BlockSpec((1, D_MODEL), lambda j: (0, j))],
        out_specs=pl.BlockSpec((8, D_MODEL), lambda j: (0, j)),
        out_shape=jax.ShapeDtypeStruct((8, w_ada.shape[1]), f32),
        compiler_params=_params(("arbitrary",)),
        name="adaln",
    )(c8, w_ada, b_ada)


IN_TM = 256
IN_CW = 512
GRP_COLS = 3 * ATT_WIDTH


def _inproj_kernel(x_ref, sc_ref, sh_ref, w_ref, p1_ref, p2_ref,
                   uv_ref, gt_ref, qkv0_ref, qkv1_ref, qkv2_ref):
    xn = _ln(x_ref[...])
    h = (xn * (1.0 + sc_ref[0]) + sh_ref[0]).astype(bf16)
    hp = [h,
          jnp.dot(p1_ref[...], h, preferred_element_type=f32).astype(bf16),
          jnp.dot(p2_ref[...], h, preferred_element_type=f32).astype(bf16)]
    for c0 in range(0, UV_COLS, IN_CW):
        acc = jnp.dot(h, w_ref[:, c0:c0 + IN_CW], preferred_element_type=f32)
        uv_ref[:, c0:c0 + IN_CW] = jax.nn.gelu(acc).astype(bf16)
    for g, (qref, (_win, dil)) in enumerate(zip((qkv0_ref, qkv1_ref, qkv2_ref), DIL_PAIRS)):
        n = IN_TM // dil
        for q0 in range(0, GRP_COLS, IN_CW):
            c0 = UV_COLS + g * GRP_COLS + q0
            acc = jnp.dot(hp[g], w_ref[:, c0:c0 + IN_CW], preferred_element_type=f32).astype(bf16)
            for rho in range(dil):
                qref[0, rho, :, q0:q0 + IN_CW] = acc[rho * n:(rho + 1) * n, :]
    for g0 in range(0, GATE_COLS, IN_CW):
        c0 = UV_COLS + QKV_COLS + g0
        acc = jnp.dot(h, w_ref[:, c0:c0 + IN_CW], preferred_element_type=f32)
        gt_ref[:, g0:g0 + IN_CW] = jax.nn.sigmoid(acc).astype(bf16)


def _inproj(x2, sc1, sh1, w_in_bf, perms, batch, seq):
    t = x2.shape[0]
    per_b = seq // IN_TM
    qkv_specs, qkv_shapes = [], []
    for _win, dil in DIL_PAIRS:
        n = IN_TM // dil
        qkv_specs.append(pl.BlockSpec((1, dil, n, GRP_COLS), lambda i: (i // per_b, 0, i % per_b, 0)))
        qkv_shapes.append(jax.ShapeDtypeStruct((batch, dil, seq // dil, GRP_COLS), bf16))
    return pl.pallas_call(
        _inproj_kernel,
        grid=(t // IN_TM,),
        in_specs=[pl.BlockSpec((IN_TM, D_MODEL), lambda i: (i, 0)),
                  pl.BlockSpec((1, 1, D_MODEL), lambda i: (i // per_b, 0, 0)),
                  pl.BlockSpec((1, 1, D_MODEL), lambda i: (i // per_b, 0, 0)),
                  pl.BlockSpec((D_MODEL, IN_COLS), lambda i: (0, 0)),
                  pl.BlockSpec((IN_TM, IN_TM), lambda i: (0, 0)),
                  pl.BlockSpec((IN_TM, IN_TM), lambda i: (0, 0))],
        out_specs=[pl.BlockSpec((IN_TM, UV_COLS), lambda i: (i, 0)),
                   pl.BlockSpec((IN_TM, GATE_COLS), lambda i: (i, 0))] + qkv_specs,
        out_shape=[jax.ShapeDtypeStruct((t, UV_COLS), bf16),
                   jax.ShapeDtypeStruct((t, GATE_COLS), bf16)] + qkv_shapes,
        compiler_params=_params(("arbitrary",)),
        name="inproj",
    )(x2, sc1, sh1, w_in_bf, perms[1], perms[2])


GM_TM = 512


def _gmlp_kernel(u_ref, v_ref, g_ref, b_ref, ws_ref, bs_ref, ya_ref):
    row = lax.broadcasted_iota(i32, (GM_CHUNK, GM_CHUNK), 0)
    col = lax.broadcasted_iota(i32, (GM_CHUNK, GM_CHUNK), 1)
    causal = col <= row
    first_half = lax.broadcasted_iota(i32, (GM_CHUNK, LANES), 1) < (GM_WIDTH // GM_GROUPS)
    ws = [jnp.where(causal, ws_ref[g], 0.0).astype(bf16) for g in range(GM_GROUPS)]
    for ch in range(GM_TM // GM_CHUNK):
        r0 = ch * GM_CHUNK
        vn = _ln(v_ref[r0:r0 + GM_CHUNK, :].astype(f32)) * g_ref[...] + b_ref[...]
        vn = vn.astype(bf16)
        for j in range(GM_WIDTH // LANES):
            slab = vn[:, j * LANES:(j + 1) * LANES]
            s_lo = jnp.dot(ws[2 * j], slab, preferred_element_type=f32)
            s_hi = jnp.dot(ws[2 * j + 1], slab, preferred_element_type=f32)
            s = jnp.where(first_half, s_lo, s_hi) + bs_ref[:, j * LANES:(j + 1) * LANES]
            u = u_ref[r0:r0 + GM_CHUNK, j * LANES:(j + 1) * LANES].astype(f32)
            ya_ref[r0:r0 + GM_CHUNK, j * LANES:(j + 1) * LANES] = (u * s).astype(bf16)


def _gmlp(uv, ln_g, ln_b, w_s, bs_full):
    t = uv.shape[0]
    return pl.pallas_call(
        _gmlp_kernel,
        grid=(t // GM_TM,),
        in_specs=[pl.BlockSpec((GM_TM, GM_WIDTH), lambda i: (i, 0)),
                  pl.BlockSpec((GM_TM, GM_WIDTH), lambda i: (i, 1)),
                  pl.BlockSpec((1, GM_WIDTH), lambda i: (0, 0)),
                  pl.BlockSpec((1, GM_WIDTH), lambda i: (0, 0)),
                  pl.BlockSpec((GM_GROUPS, GM_CHUNK, GM_CHUNK), lambda i: (0, 0, 0)),
                  pl.BlockSpec((GM_CHUNK, GM_WIDTH), lambda i: (0, 0))],
        out_specs=pl.BlockSpec((GM_TM, GM_WIDTH), lambda i: (i, 0)),
        out_shape=jax.ShapeDtypeStruct((t, GM_WIDTH), bf16),
        compiler_params=_params(("arbitrary",)),
        name="gmlp",
    )(uv, uv, ln_g, ln_b, w_s, bs_full)


def _relbias_kernel(tab_ref, bucket_ref, band_ref, out_ref):
    g = pl.program_id(0)
    bk = bucket_ref[0]
    band = band_ref[0] > 0
    for h in range(HEADS_PER_GROUP):
        acc = jnp.zeros((ATT_BLOCK, 2 * ATT_BLOCK), f32)
        for b in range(REL_BUCKETS):
            acc = jnp.where(bk == b, tab_ref[b, g * HEADS_PER_GROUP + h], acc)
        out_ref[0, h] = jnp.where(band, acc, NEG_INF)


def _relbias(rel_bias, bucket, band):
    return pl.pallas_call(
        _relbias_kernel,
        grid=(N_DIL,),
        in_specs=[pl.BlockSpec(memory_space=pltpu.SMEM),
                  pl.BlockSpec((1, ATT_BLOCK, 2 * ATT_BLOCK), lambda g: (g, 0, 0)),
                  pl.BlockSpec((1, ATT_BLOCK, 2 * ATT_BLOCK), lambda g: (g, 0, 0))],
        out_specs=pl.BlockSpec((1, HEADS_PER_GROUP, ATT_BLOCK, 2 * ATT_BLOCK),
                               lambda g: (g, 0, 0, 0)),
        out_shape=jax.ShapeDtypeStruct((N_DIL, HEADS_PER_GROUP, ATT_BLOCK, 2 * ATT_BLOCK), f32),
        compiler_params=_params(("arbitrary",)),
        name="relbias",
    )(rel_bias, bucket, band)


ATT_MAX_STEP_BLOCKS = 4


def _attn_kernel(nblk, q_ref, kp_ref, kc_ref, vp_ref, vc_ref, bias_ref, o_ref, lse_ref):
    first = pl.program_id(2) == 0
    lane = lax.broadcasted_iota(i32, (ATT_BLOCK, LANES), 1)
    lo_half = lane < HEAD_DIM
    nt = (((1,), (1,)), ((), ()))
    ones = jnp.ones((2 * ATT_BLOCK, LANES), bf16)
    n_slab = ATT_WIDTH // LANES
    logits, v_ext = [], []
    for i in range(nblk):
        cur = slice(i * ATT_BLOCK, (i + 1) * ATT_BLOCK)
        prv = slice((i - 1) * ATT_BLOCK, i * ATT_BLOCK)
        for j in range(n_slab):
            sl = slice(j * LANES, (j + 1) * LANES)
            q = q_ref[0, 0, cur, sl] * (HEAD_DIM ** -0.5)
            k_prev = kp_ref[0, 0, :, sl] if i == 0 else kc_ref[0, 0, prv, sl]
            v_prev = vp_ref[0, 0, :, sl] if i == 0 else vc_ref[0, 0, prv, sl]
            k_cat = jnp.concatenate([k_prev, kc_ref[0, 0, cur, sl]], axis=0)
            v_cat = jnp.concatenate([v_prev, vc_ref[0, 0, cur, sl]], axis=0)
            v_ext.append(jnp.concatenate([v_cat, ones], axis=1))
            for hh in range(2):
                qm = jnp.where(lo_half if hh == 0 else jnp.logical_not(lo_half), q, 0.0).astype(bf16)
                logits.append(lax.dot_general(qm, k_cat, nt, preferred_element_type=f32))
    bias = bias_ref[0].reshape(HEADS_PER_GROUP * ATT_BLOCK, 2 * ATT_BLOCK)
    rows_per_block = HEADS_PER_GROUP * ATT_BLOCK
    lg = jnp.concatenate(logits, axis=0) + jnp.concatenate([bias] * nblk, axis=0)
    row = lax.broadcasted_iota(i32, lg.shape, 0)
    col = lax.broadcasted_iota(i32, lg.shape, 1)
    no_prev = jnp.logical_and(first, jnp.logical_and(row < rows_per_block, col < ATT_BLOCK))
    lg = jnp.where(no_prev, NEG_INF, lg)
    m = jnp.max(lg, axis=-1, keepdims=True)
    p = jnp.exp(lg - m).astype(bf16)
    for i in range(nblk):
        cur = slice(i * ATT_BLOCK, (i + 1) * ATT_BLOCK)
        lse_tile = jnp.zeros((ATT_BLOCK, LANES), f32)
        for j in range(n_slab):
            outs = []
            for hh in range(2):
                h = 2 * j + hh
                r0 = i * rows_per_block + h * ATT_BLOCK
                r = jnp.dot(p[r0:r0 + ATT_BLOCK], v_ext[i * n_slab + j], preferred_element_type=f32)
                den = r[:, LANES:]
                outs.append(r[:, :LANES] * (1.0 / den))
                lse_h = m[r0:r0 + ATT_BLOCK] + jnp.log(den)
                lse_tile = jnp.where(lane == h, lse_h, lse_tile)
            o_ref[0, 0, cur, j * LANES:(j + 1) * LANES] = jnp.where(lo_half, outs[0], outs[1]).astype(bf16)
        lse_ref[0, 0, cur, :] = lse_tile


def _attn_group(qkv_g, bias, g, dil, batch, seq):
    l = seq // dil
    nblk = min(ATT_MAX_STEP_BLOCKS, l // ATT_BLOCK)
    tm = nblk * ATT_BLOCK
    nsteps = l // tm

    def cur(cb):
        return pl.BlockSpec((1, 1, tm, ATT_WIDTH), lambda b, r, n: (b, r, n, cb))

    def prev(cb):
        return pl.BlockSpec((1, 1, ATT_BLOCK, ATT_WIDTH),
                            lambda b, r, n: (b, r, jnp.maximum(n * nblk - 1, 0), cb))

    return pl.pallas_call(
        functools.partial(_attn_kernel, nblk),
        grid=(batch, dil, nsteps),
        in_specs=[cur(0), prev(1), cur(1), prev(2), cur(2),
                  pl.BlockSpec((1, HEADS_PER_GROUP, ATT_BLOCK, 2 * ATT_BLOCK),
                               lambda b, r, n: (g, 0, 0, 0))],
        out_specs=[pl.BlockSpec((1, 1, tm, ATT_WIDTH), lambda b, r, n: (b, r, n, 0)),
                   pl.BlockSpec((1, 1, tm, LANES), lambda b, r, n: (b, r, n, 0))],
        out_shape=[jax.ShapeDtypeStruct((batch, dil, l, ATT_WIDTH), bf16),
                   jax.ShapeDtypeStruct((batch, dil, l, LANES), f32)],
        compiler_params=_params(("arbitrary", "arbitrary", "arbitrary")),
        name=f"attn_g{g}",
    )(qkv_g, qkv_g, qkv_g, qkv_g, qkv_g, bias)


ROW_WORDS = D_MODEL // 2
ROW_SUB = ROW_WORDS // LANES
HI_MASK = -65536


def _pack_rows(x):
    bits = lax.bitcast_convert_type(x.astype(bf16).astype(f32), i32)
    return lax.shift_right_logical(bits[:, :ROW_WORDS], 16) | (bits[:, ROW_WORDS:] & HI_MASK)


def _unpack_rows(words):
    lo = lax.bitcast_convert_type(lax.shift_left(words, 16), f32)
    hi = lax.bitcast_convert_type(words & HI_MASK, f32)
    return jnp.concatenate([lo, hi], axis=1)


def _store_packed(ref, words, n, first_row=0):
    for r in range(ROW_SUB):
        ref[pl.ds(first_row * ROW_SUB + r, n, stride=ROW_SUB), :] = words[:, r * LANES:(r + 1) * LANES]


def _load_packed(ref, first_row, n):
    return jnp.concatenate([ref[pl.ds(first_row * ROW_SUB + r, n, stride=ROW_SUB), :] for r in range(ROW_SUB)],
                           axis=1)


MIX_TM = 256
MIX_SUB = 128


def _split_bf16(x, parts):
    out = []
    for _ in range(parts):
        hi = x.astype(bf16)
        out.append(hi)
        x = x - hi.astype(f32)
    return out


def _mix_kernel(o0_ref, o1_ref, o2_ref, l0_ref, l1_ref, l2_ref, pt1_ref, pt2_ref, ex_ref,
                ya_ref, gt_ref, x_ref,
                g1_ref, sc2_ref, sh2_ref, wa_ref, wb_ref, wo_ref, ln1g_ref, ln1b_ref,
                wrc_ref, br_ref, tri_ref,
                x1_ref, h2_ref, route_ref, rw_ref, cnt_ref, run_ref):
    @pl.when(pl.program_id(0) == 0)
    def _():
        run_ref[...] = jnp.zeros_like(run_ref)

    def token_rows(r0):
        rows = slice(r0, r0 + MIX_SUB)
        os_, ls_ = [o0_ref[0, 0, rows, :].astype(f32)], [l0_ref[0, 0, rows, :]]
        for o_ref, l_ref, pt_ref in ((o1_ref, l1_ref, pt1_ref), (o2_ref, l2_ref, pt2_ref)):
            pt = pt_ref[rows, :]
            os_.append(jnp.dot(pt, o_ref[0].reshape(MIX_TM, ATT_WIDTH), preferred_element_type=f32))
            parts = [jnp.dot(pt, part, preferred_element_type=f32)
                     for part in _split_bf16(l_ref[0].reshape(MIX_TM, LANES), 3)]
            ls_.append((parts[0] + parts[1]) + parts[2])
        lm = jnp.maximum(jnp.maximum(ls_[0], ls_[1]), ls_[2])
        es = [jnp.exp(lse - lm) for lse in ls_]
        inv = 1.0 / (es[0] + es[1] + es[2])
        yb = jnp.zeros((MIX_SUB, ATT_WIDTH), f32)
        for e, o in zip(es, os_):
            w_parts = jnp.concatenate(_split_bf16(e * inv, 2), axis=1)
            yb = yb + jnp.dot(w_parts, ex_ref[...], preferred_element_type=f32) * o
        a = jnp.dot(ya_ref[rows, :], wa_ref[...], preferred_element_type=f32)
        b = jnp.dot(yb.astype(bf16), wb_ref[...], preferred_element_type=f32)
        merged = gt_ref[rows, :D_MODEL].astype(f32) * a + gt_ref[rows, D_MODEL:].astype(f32) * b
        mix = jnp.dot(merged.astype(bf16), wo_ref[...], preferred_element_type=f32)
        x1 = _ln(DN_ALPHA * x_ref[rows, :] + g1_ref[0] * mix) * ln1g_ref[...] + ln1b_ref[...]
        x1_ref[rows, :] = x1
        h2 = _ln(x1) * (1.0 + sc2_ref[0]) + sh2_ref[0]
        _store_packed(h2_ref, _pack_rows(h2), MIX_SUB, r0)

        h_hi, h_lo = _split_bf16(h2, 2)
        hi_both = jnp.dot(h_hi, wrc_ref[...], preferred_element_type=f32)
        return (hi_both[:, :LANES]
                + (hi_both[:, LANES:] + jnp.dot(h_lo, wrc_ref[:, :LANES], preferred_element_type=f32))
                ) + br_ref[...]

    logits = jnp.concatenate([token_rows(r0) for r0 in range(0, MIX_TM, MIX_SUB)], axis=0)

    lane = lax.broadcasted_iota(i32, (MIX_TM, LANES), 1)
    logits = jnp.where(lane < N_EXPERTS, logits, -jnp.inf)
    lane_f = lane.astype(f32)
    vals, idxs = [], []
    for _k in range(TOP_K):
        m = jnp.max(logits, axis=-1, keepdims=True)
        vals.append(m)
        idxs.append(jnp.min(jnp.where(logits == m, lane_f, float(LANES)), axis=-1, keepdims=True).astype(i32))
        logits = jnp.where(lane == idxs[-1], -jnp.inf, logits)
    exps = [jnp.exp(v - vals[0]) for v in vals]
    den = exps[0] + exps[1] + exps[2] + exps[3]
    wts = [e / den for e in exps]
    hits = [lane == idx for idx in idxs]
    onehot = jnp.zeros((MIX_TM, LANES), f32)
    for hit in hits:
        onehot = onehot + jnp.where(hit, 1.0, 0.0)
    prefix = jnp.dot(tri_ref[...], onehot.astype(bf16), preferred_element_type=f32) + run_ref[...]
    route = jnp.zeros((MIX_TM, LANES), i32)
    rw = jnp.zeros((MIX_TM, LANES), f32)
    for k in range(TOP_K):
        rank = jnp.sum(jnp.where(hits[k], prefix, 0.0), axis=-1, keepdims=True).astype(i32)
        route = jnp.where(lane == k, idxs[k], route)
        route = jnp.where(lane == TOP_K + k, rank, route)
        rw = jnp.where(lane == k, wts[k], rw)
    route_ref[...] = route
    rw_ref[...] = rw
    run = run_ref[...] + jnp.sum(onehot, axis=0, keepdims=True)
    run_ref[...] = run
    cnt_ref[...] = jnp.broadcast_to(run, cnt_ref.shape)


def _mix(os_, ls_, perms_t, expand, ya, gates, x2, g1, sc2, sh2, wa, wb, wo, ln1g, ln1b, wr_parts, br, tri, seq):
    t = x2.shape[0]
    per_b = seq // MIX_TM
    row = lambda w: pl.BlockSpec((MIX_TM, w), lambda i: (i, 0))
    const = lambda s: pl.BlockSpec(s, lambda i: tuple(0 for _ in s))
    modb = pl.BlockSpec((1, 1, D_MODEL), lambda i: (i // per_b, 0, 0))
    grp = lambda w: [pl.BlockSpec((1, dil, MIX_TM // dil, w), lambda i: (i // per_b, 0, i % per_b, 0))
                     for _win, dil in DIL_PAIRS]
    return pl.pallas_call(
        _mix_kernel,
        grid=(t // MIX_TM,),
        in_specs=grp(ATT_WIDTH) + grp(LANES) + [
                  const((MIX_TM, MIX_TM)), const((MIX_TM, MIX_TM)), const((2 * LANES, ATT_WIDTH)),
                  row(GM_WIDTH), row(GATE_COLS), row(D_MODEL),
                  modb, modb, modb,
                  const((GM_WIDTH, D_MODEL)), const((ATT_WIDTH, D_MODEL)), const((D_MODEL, D_MODEL)),
                  const((1, D_MODEL)), const((1, D_MODEL)),
                  const((D_MODEL, 2 * LANES)), const((1, LANES)), const((MIX_TM, MIX_TM))],
        out_specs=[row(D_MODEL), pl.BlockSpec((MIX_TM * ROW_SUB, LANES), lambda i: (i, 0)),
                   row(LANES), row(LANES), const((8, LANES))],
        out_shape=[jax.ShapeDtypeStruct((t, D_MODEL), f32),
                   jax.ShapeDtypeStruct((t * ROW_SUB, LANES), i32),
                   jax.ShapeDtypeStruct((t, LANES), i32),
                   jax.ShapeDtypeStruct((t, LANES), f32),
                   jax.ShapeDtypeStruct((8, LANES), f32)],
        scratch_shapes=[pltpu.VMEM((1, LANES), f32)],
        compiler_params=_params(("arbitrary",)),
        name="mix",
    )(*os_, *ls_, perms_t[1], perms_t[2], expand, ya, gates, x2, g1, sc2, sh2, wa, wb, wo,
      ln1g, ln1b, wr_parts, br, tri)


DISP_TM = 256
MOE_TM = 256


def _dispatch_kernel(pends_ref, pcnt_ref, nused_ref, dest_ref, h2p_ref, xs_hbm, zbuf, sem, zsem):
    i = pl.program_id(0)
    ntile = xs_hbm.shape[0] // (MOE_TM * ROW_SUB)

    def zero_tile(first_row):
        return pltpu.make_async_copy(
            zbuf, xs_hbm.at[pl.ds(pl.multiple_of(first_row * ROW_SUB, MOE_TM * ROW_SUB), MOE_TM * ROW_SUB)], zsem)

    def for_each_zero_tile(fn):
        for e in range(N_EXPERTS):
            pl.when(pcnt_ref[e] > 0)(functools.partial(fn, lambda e=e: zero_tile(pends_ref[e] - MOE_TM)))
        for k in range(N_EXPERTS):
            tile = nused_ref[0] + k
            pl.when(tile < ntile)(functools.partial(fn, lambda tile=tile: zero_tile(tile * MOE_TM)))

    @pl.when(i == 0)
    def _():
        zbuf[...] = jnp.zeros_like(zbuf)
        for_each_zero_tile(lambda mk: mk().start())
        for_each_zero_tile(lambda mk: mk().wait())

    def row_copy(k, r):
        d = dest_ref[0, k, r]
        return pltpu.make_async_copy(h2p_ref.at[pl.ds(r * ROW_SUB, ROW_SUB)],
                                     xs_hbm.at[pl.ds(pl.multiple_of(d * ROW_SUB, ROW_SUB), ROW_SUB)], sem)

    for r in range(DISP_TM):
        for k in range(TOP_K):
            row_copy(k, r).start(priority=k % 2)
    for k in range(TOP_K):
        pltpu.make_async_copy(h2p_ref, xs_hbm.at[pl.ds(0, DISP_TM * ROW_SUB)], sem).wait()


def _dispatch(pends, pcounts, n_used, dest3, h2p, ntile):
    t = h2p.shape[0] // ROW_SUB
    grid_spec = pltpu.PrefetchScalarGridSpec(
        num_scalar_prefetch=3,
        grid=(t // DISP_TM,),
        in_specs=[pl.BlockSpec((1, TOP_K, DISP_TM), lambda i, *_: (i, 0, 0), memory_space=pltpu.SMEM),
                  pl.BlockSpec((DISP_TM * ROW_SUB, LANES), lambda i, *_: (i, 0))],
        out_specs=pl.BlockSpec(memory_space=pl.ANY),
        scratch_shapes=[pltpu.VMEM((MOE_TM * ROW_SUB, LANES), i32),
                        pltpu.SemaphoreType.DMA(()),
                        pltpu.SemaphoreType.DMA(())],
    )
    return pl.pallas_call(
        _dispatch_kernel,
        grid_spec=grid_spec,
        out_shape=jax.ShapeDtypeStruct((ntile * MOE_TM * ROW_SUB, LANES), i32),
        compiler_params=_params(("arbitrary",)),
        name="dispatch",
    )(pends, pcounts, n_used, dest3, h2p)


def _moe_kernel(te_ref, first_ref, nexte_ref, wslot_ref, nused_ref,
                xs_ref, wg_hbm, wu_hbm, wd_hbm, bg_ref, bu_ref, bd_ref,
                out_ref, wbuf, wgb, wub, wdb, sem_w):
    j = pl.program_id(0)

    def weight_copies(e, ws):
        return [pltpu.make_async_copy(w.at[e], wbuf.at[ws, k], sem_w.at[ws])
                for k, w in enumerate((wg_hbm, wu_hbm, wd_hbm))]

    @pl.when(j == 0)
    def _():
        for cp in weight_copies(te_ref[0], wslot_ref[0]):
            cp.start()

    @pl.when(first_ref[j] == 1)
    def _():
        ws = wslot_ref[j]
        for cp in weight_copies(te_ref[j], ws):
            cp.wait()
        wgb[...] = wbuf[ws, 0].astype(bf16)
        wub[...] = wbuf[ws, 1].astype(bf16)
        wdb[...] = wbuf[ws, 2].astype(bf16)
        ne = nexte_ref[j]

        @pl.when(ne >= 0)
        def _():
            for cp in weight_copies(ne, 1 - ws):
                cp.start()

    used = j < nused_ref[0]

    @pl.when(used)
    def _():
        xb = _unpack_rows(_load_packed(xs_ref, 0, MOE_TM)).astype(bf16)
        g = jnp.dot(xb, wgb[...], preferred_element_type=f32) + bg_ref[0]
        u = jnp.dot(xb, wub[...], preferred_element_type=f32) + bu_ref[0]
        g = jnp.minimum(g, SWIGLU_LIMIT)
        u = jnp.clip(u, -SWIGLU_LIMIT, SWIGLU_LIMIT)
        act = (u + 1.0) * (g * jax.nn.sigmoid(SWIGLU_ALPHA * g))
        y = jnp.dot(act.astype(bf16), wdb[...], preferred_element_type=f32) + bd_ref[0]
        _store_packed(out_ref, _pack_rows(y), MOE_TM)

    @pl.when(jnp.logical_not(used))
    def _():
        out_ref[...] = jnp.zeros_like(out_ref)


def _moe(tile_e, tile_first, next_e, wslot, n_used, xs, w_gate, b_gate, w_up, b_up, w_down, b_down):
    ntile = tile_e.shape[0]
    bspec = pl.BlockSpec((1, 1, D_MODEL), lambda j, te, *_: (te[j], 0, 0))
    hbm = pl.BlockSpec(memory_space=pl.ANY)
    grid_spec = pltpu.PrefetchScalarGridSpec(
        num_scalar_prefetch=5,
        grid=(ntile,),
        in_specs=[pl.BlockSpec((MOE_TM * ROW_SUB, LANES),
                               lambda j, te, fi, ne, ws, nu: (jnp.minimum(j, nu[0] - 1), 0)),
                  hbm, hbm, hbm, bspec, bspec, bspec],
        out_specs=pl.BlockSpec((MOE_TM * ROW_SUB, LANES), lambda j, *_: (j, 0)),
        scratch_shapes=[pltpu.VMEM((2, 3, D_MODEL, D_MODEL), f32),
                        pltpu.VMEM((D_MODEL, D_MODEL), bf16),
                        pltpu.VMEM((D_MODEL, D_MODEL), bf16),
                        pltpu.VMEM((D_MODEL, D_MODEL), bf16),
                        pltpu.SemaphoreType.DMA((2,))],
    )
    return pl.pallas_call(
        _moe_kernel,
        grid_spec=grid_spec,
        out_shape=jax.ShapeDtypeStruct((ntile * MOE_TM * ROW_SUB, LANES), i32),
        compiler_params=_params(("arbitrary",)),
        name="moe",
    )(tile_e, tile_first, next_e, wslot, n_used, xs, w_gate, w_up, w_down, b_gate, b_up, b_down)


CB_TM = 256


def _combine_kernel(dcur_ref, dnxt_ref, yb_hbm, rw_ref, x1_ref, g2_ref, lng_ref, lnb_ref, out_ref,
                    ybuf0, ybuf1, sem):
    i = pl.program_id(0)
    last = pl.num_programs(0) - 1
    ybufs = (ybuf0, ybuf1)

    def row_copy(d, k, r, s):
        return pltpu.make_async_copy(
            yb_hbm.at[pl.ds(pl.multiple_of(d * ROW_SUB, ROW_SUB), ROW_SUB)],
            ybufs[s].at[pl.ds(pl.multiple_of((k * CB_TM + r) * ROW_SUB, ROW_SUB), ROW_SUB)],
            sem.at[s])

    @pl.when(i == 0)
    def _():
        for k in range(TOP_K):
            def body(r, c, k=k):
                row_copy(dcur_ref[0, k, r], k, r, 0).start()
                return c
            lax.fori_loop(0, CB_TM, body, 0, unroll=8)

    for s in range(2):
        @pl.when(i % 2 == s)
        def _(s=s):
            pltpu.make_async_copy(yb_hbm.at[pl.ds(0, TOP_K * CB_TM * ROW_SUB)], ybufs[s], sem.at[s]).wait()

            @pl.when(i < last)
            def _():
                for k in range(TOP_K):
                    for r in range(CB_TM):
                        row_copy(dnxt_ref[0, k, r], k, r, 1 - s).start(priority=r % 2)

            parts = [_unpack_rows(_load_packed(ybufs[s], k * CB_TM, CB_TM)) * rw_ref[:, k:k + 1]
                     for k in range(TOP_K)]
            y = (parts[0] + parts[1]) + (parts[2] + parts[3])
            out_ref[...] = _ln(DN_ALPHA * x1_ref[...] + g2_ref[0] * y) * lng_ref[...] + lnb_ref[...]


def _combine(dest3, yb, rw, x1, g2, ln2g, ln2b, seq):
    t = x1.shape[0]
    nb = t // CB_TM
    per_b = seq // CB_TM
    return pl.pallas_call(
        _combine_kernel,
        grid=(nb,),
        in_specs=[pl.BlockSpec((1, TOP_K, CB_TM), lambda i: (i, 0, 0), memory_space=pltpu.SMEM),
                  pl.BlockSpec((1, TOP_K, CB_TM), lambda i: (jnp.minimum(i + 1, nb - 1), 0, 0),
                               memory_space=pltpu.SMEM),
                  pl.BlockSpec(memory_space=pl.ANY),
                  pl.BlockSpec((CB_TM, LANES), lambda i: (i, 0)),
                  pl.BlockSpec((CB_TM, D_MODEL), lambda i: (i, 0)),
                  pl.BlockSpec((1, 1, D_MODEL), lambda i: (i // per_b, 0, 0)),
                  pl.BlockSpec((1, D_MODEL), lambda i: (0, 0)),
                  pl.BlockSpec((1, D_MODEL), lambda i: (0, 0))],
        out_specs=pl.BlockSpec((CB_TM, D_MODEL), lambda i: (i, 0)),
        out_shape=jax.ShapeDtypeStruct((t, D_MODEL), f32),
        scratch_shapes=[pltpu.VMEM((TOP_K * CB_TM * ROW_SUB, LANES), i32),
                        pltpu.VMEM((TOP_K * CB_TM * ROW_SUB, LANES), i32),
                        pltpu.SemaphoreType.DMA((2,))],
        compiler_params=_params(("arbitrary",)),
        name="combine",
    )(dest3, dest3, yb, rw, x1, g2, ln2g, ln2b)


def _t5_bucket(dist):
    d = dist.astype(f32)
    large = REL_MAX_EXACT + jnp.log(jnp.maximum(d, float(REL_MAX_EXACT)) / REL_MAX_EXACT) / math.log(
        REL_MAX_DIST / REL_MAX_EXACT) * (REL_BUCKETS - REL_MAX_EXACT)
    large = jnp.minimum(large.astype(i32), REL_BUCKETS - 1)
    return jnp.where(dist < REL_MAX_EXACT, dist, large)


def _bias_indices():
    qi = jnp.arange(ATT_BLOCK)[:, None]
    ki = jnp.arange(2 * ATT_BLOCK)[None, :]
    didx = qi + ATT_BLOCK - ki
    buckets, bands = [], []
    for win, dil in DIL_PAIRS:
        buckets.append(_t5_bucket(jnp.clip(didx, 0, None) * dil))
        bands.append(((didx >= 0) & (didx <= win // dil)).astype(i32))
    return jnp.stack(buckets).astype(i32), jnp.stack(bands)


def _residue_perm(tm, dil):
    n = tm // dil
    dst = jnp.arange(tm)
    src = (dst % n) * dil + dst // n
    return (src[:, None] == jnp.arange(tm)[None, :]).astype(bf16)


def kernel(x, c, w_ada, b_ada, w_in, gm_ln_g, gm_ln_b, gm_w_s, gm_b_s, w_branch_a, w_branch_b, w_out,
           rel_bias, ln1_g, ln1_b, w_router, b_router, w_gate, b_gate, w_up, b_up, w_down, b_down,
           ln2_g, ln2_b):
    batch, seq, _ = x.shape
    t = batch * seq
    l = 0
    x2 = x.reshape(t, D_MODEL)

    c8 = jnp.pad(c, ((0, 8 - batch), (0, 0)))
    mod = _adaln(c8, w_ada[l], b_ada[l][None, :])[:batch]
    sh1, sc1, g1, sh2, sc2, g2 = [m[:, None, :] for m in jnp.split(mod, 6, axis=-1)]

    perms = [_residue_perm(IN_TM, dil) for _win, dil in DIL_PAIRS]
    uv, gates, *qkvs = _inproj(x2, sc1, sh1, w_in[l].astype(bf16), perms, batch, seq)

    bs_full = jnp.repeat(gm_b_s[l].T, GM_WIDTH // GM_GROUPS, axis=1)
    ya = _gmlp(uv, gm_ln_g[l][None, :], gm_ln_b[l][None, :], gm_w_s[l], bs_full)

    bucket, band = _bias_indices()
    bias = _relbias(rel_bias, bucket, band)
    os_, ls_ = [], []
    for g, (_win, dil) in enumerate(DIL_PAIRS):
        o, lse = _attn_group(qkvs[g], bias, g, dil, batch, seq)
        os_.append(o)
        ls_.append(lse)

    wr = jnp.pad(w_router[l], ((0, 0), (0, LANES - N_EXPERTS)))
    wr_hi = wr.astype(bf16)
    wr_parts = jnp.concatenate([wr_hi, (wr - wr_hi.astype(f32)).astype(bf16)], axis=1)
    br = jnp.pad(b_router[l], (0, LANES - N_EXPERTS))[None, :]
    tri = (jnp.arange(MIX_TM)[None, :] < jnp.arange(MIX_TM)[:, None]).astype(bf16)
    perms_t = [_residue_perm(MIX_TM, dil).T for _win, dil in DIL_PAIRS]
    expand = (jnp.arange(LANES)[:, None] == jnp.arange(ATT_WIDTH)[None, :] // HEAD_DIM).astype(bf16)
    expand = jnp.concatenate([expand, expand], axis=0)
    x1, h2, route, rw, cnt = _mix(
        os_, ls_, perms_t, expand, ya, gates, x2, g1, sc2, sh2,
        w_branch_a[l].astype(bf16), w_branch_b[l].astype(bf16), w_out[l].astype(bf16),
        ln1_g[l][None, :], ln1_b[l][None, :], wr_parts, br, tri, seq)

    top_e = route[:, :TOP_K]
    rank = route[:, TOP_K:2 * TOP_K]
    counts = cnt[0, :N_EXPERTS].astype(i32)
    pcounts = (counts + MOE_TM - 1) // MOE_TM * MOE_TM
    pends = jnp.cumsum(pcounts)
    pstarts = pends - pcounts
    experts = jnp.arange(N_EXPERTS, dtype=i32)
    dest = jnp.sum(jnp.where(top_e[:, :, None] == experts, pstarts, 0), axis=-1) + rank
    ntile = t * TOP_K // MOE_TM + N_EXPERTS
    n_used = (pends[-1] // MOE_TM).reshape(1)
    tile_idx = jnp.minimum(jnp.arange(ntile, dtype=i32), n_used - 1)
    tile_e = jnp.sum((pends[None, :] <= (tile_idx * MOE_TM)[:, None]).astype(i32), axis=1)
    tile_first = jnp.concatenate([jnp.ones((1,), i32), (tile_e[1:] != tile_e[:-1]).astype(i32)])
    nonempty = counts > 0
    later = lax.cummin(jnp.where(nonempty, experts, N_EXPERTS), reverse=True)
    next_nonempty = jnp.concatenate([later[1:], jnp.full((1,), N_EXPERTS, i32)])
    next_nonempty = jnp.where(next_nonempty >= N_EXPERTS, -1, next_nonempty)
    expert_slot = (jnp.cumsum(nonempty.astype(i32)) - 1) % 2

    dest3 = dest.reshape(t // DISP_TM, DISP_TM, TOP_K).transpose(0, 2, 1)
    xs = _dispatch(pends, pcounts, n_used, dest3, h2, ntile)
    yb = _moe(tile_e, tile_first, next_nonempty[tile_e], expert_slot[tile_e], n_used, xs,
              w_gate[l], b_gate[l][:, None, :], w_up[l], b_up[l][:, None, :],
              w_down[l], b_down[l][:, None, :])
    out = _combine(dest3, yb, rw, x1, g2, ln2_g[l][None, :], ln2_b[l][None, :], seq)
    return out.reshape(batch, seq, D_MODEL)
```

```python
import functools
import math

import jax
import jax.numpy as jnp
from jax import lax
from jax.experimental import pallas as pl
from jax.experimental.pallas import tpu as pltpu

f32 = jnp.float32
bf16 = jnp.bfloat16
i32 = jnp.int32

D_MODEL = 1024
GM_WIDTH = 512
GM_GROUPS = 8
GM_CHUNK = 128
DIL_PAIRS = ((128, 1), (512, 4), (2048, 16))
N_DIL = 3
HEADS_PER_GROUP = 8
HEAD_DIM = 64
ATT_WIDTH = 512
ATT_BLOCK = 128
NEG_INF = -1e30
REL_BUCKETS = 32
REL_MAX_EXACT = 16
REL_MAX_DIST = 2048
N_EXPERTS = 32
TOP_K = 4
SWIGLU_LIMIT = 7.0
SWIGLU_ALPHA = 1.702
MOE_BLOCK = 128
DEPTH = 1
DN_ALPHA = (2 * DEPTH) ** 0.25
LN_EPS = 1e-5
UV_COLS = 2 * GM_WIDTH
QKV_COLS = N_DIL * 3 * ATT_WIDTH
GATE_COLS = 2 * D_MODEL
IN_COLS = UV_COLS + QKV_COLS + GATE_COLS

LANES = 128
SUBLANES = 8
VMEM_LIMIT = 56 * 1024 * 1024


def _ln(x):
    mu = jnp.mean(x, axis=-1, keepdims=True)
    xc = x - mu
    var = jnp.mean(xc * xc, axis=-1, keepdims=True)
    return xc * lax.rsqrt(var + LN_EPS)


def _params(sem, vmem=VMEM_LIMIT):
    return pltpu.CompilerParams(dimension_semantics=sem, vmem_limit_bytes=vmem)


def _adaln_kernel(c_ref, w_ref, b_ref, o_ref):
    c = c_ref[...]
    s = c * jax.nn.sigmoid(c)
    o_ref[...] = jnp.dot(s, w_ref[...], preferred_element_type=f32,
                         precision=lax.Precision.HIGHEST) + b_ref[...]


def _adaln(c8, w_ada, b_ada):
    n = w_ada.shape[1] // D_MODEL
    return pl.pallas_call(
        _adaln_kernel,
        grid=(n,),
        in_specs=[pl.BlockSpec((8, D_MODEL), lambda j: (0, 0)),
                  pl.BlockSpec((D_MODEL, D_MODEL), lambda j: (0, j)),
                  pl.BlockSpec((1, D_MODEL), lambda j: (0, j))],
        out_specs=pl.BlockSpec((8, D_MODEL), lambda j: (0, j)),
        out_shape=jax.ShapeDtypeStruct((8, w_ada.shape[1]), f32),
        compiler_params=_params(("arbitrary",)),
        name="adaln",
    )(c8, w_ada, b_ada)


IN_TM = 256
IN_CW = 512
GRP_COLS = 3 * ATT_WIDTH


def _inproj_kernel(x_ref, sc_ref, sh_ref, w_ref, p1_ref, p2_ref,
                   uv_ref, gt_ref, qkv0_ref, qkv1_ref, qkv2_ref):
    xn = _ln(x_ref[...])
    h = (xn * (1.0 + sc_ref[0]) + sh_ref[0]).astype(bf16)
    hp = [h,
          jnp.dot(p1_ref[...], h, preferred_element_type=f32).astype(bf16),
          jnp.dot(p2_ref[...], h, preferred_element_type=f32).astype(bf16)]
    for c0 in range(0, UV_COLS, IN_CW):
        acc = jnp.dot(h, w_ref[:, c0:c0 + IN_CW], preferred_element_type=f32)
        uv_ref[:, c0:c0 + IN_CW] = jax.nn.gelu(acc).astype(bf16)
    for g, (qref, (_win, dil)) in enumerate(zip((qkv0_ref, qkv1_ref, qkv2_ref), DIL_PAIRS)):
        n = IN_TM // dil
        for q0 in range(0, GRP_COLS, IN_CW):
            c0 = UV_COLS + g * GRP_COLS + q0
            acc = jnp.dot(hp[g], w_ref[:, c0:c0 + IN_CW], preferred_element_type=f32).astype(bf16)
            for rho in range(dil):
                qref[0, rho, :, q0:q0 + IN_CW] = acc[rho * n:(rho + 1) * n, :]
    for g0 in range(0, GATE_COLS, IN_CW):
        c0 = UV_COLS + QKV_COLS + g0
        acc = jnp.dot(h, w_ref[:, c0:c0 + IN_CW], preferred_element_type=f32)
        gt_ref[:, g0:g0 + IN_CW] = jax.nn.sigmoid(acc).astype(bf16)


def _inproj(x2, sc1, sh1, w_in_bf, perms, batch, seq):
    t = x2.shape[0]
    per_b = seq // IN_TM
    qkv_specs, qkv_shapes = [], []
    for _win, dil in DIL_PAIRS:
        n = IN_TM // dil
        qkv_specs.append(pl.BlockSpec((1, dil, n, GRP_COLS), lambda i: (i // per_b, 0, i % per_b, 0)))
        qkv_shapes.append(jax.ShapeDtypeStruct((batch, dil, seq // dil, GRP_COLS), bf16))
    return pl.pallas_call(
        _inproj_kernel,
        grid=(t // IN_TM,),
        in_specs=[pl.BlockSpec((IN_TM, D_MODEL), lambda i: (i, 0)),
                  pl.BlockSpec((1, 1, D_MODEL), lambda i: (i // per_b, 0, 0)),
                  pl.BlockSpec((1, 1, D_MODEL), lambda i: (i // per_b, 0, 0)),
                  pl.BlockSpec((D_MODEL, IN_COLS), lambda i: (0, 0)),
                  pl.BlockSpec((IN_TM, IN_TM), lambda i: (0, 0)),
                  pl.BlockSpec((IN_TM, IN_TM), lambda i: (0, 0))],
        out_specs=[pl.BlockSpec((IN_TM, UV_COLS), lambda i: (i, 0)),
                   pl.BlockSpec((IN_TM, GATE_COLS), lambda i: (i, 0))] + qkv_specs,
        out_shape=[jax.ShapeDtypeStruct((t, UV_COLS), bf16),
                   jax.ShapeDtypeStruct((t, GATE_COLS), bf16)] + qkv_shapes,
        compiler_params=_params(("arbitrary",)),
        name="inproj",
    )(x2, sc1, sh1, w_in_bf, perms[1], perms[2])


GM_TM = 512


def _gmlp_kernel(u_ref, v_ref, g_ref, b_ref, ws_ref, bs_ref, ya_ref):
    row = lax.broadcasted_iota(i32, (GM_CHUNK, GM_CHUNK), 0)
    col = lax.broadcasted_iota(i32, (GM_CHUNK, GM_CHUNK), 1)
    causal = col <= row
    first_half = lax.broadcasted_iota(i32, (GM_CHUNK, LANES), 1) < (GM_WIDTH // GM_GROUPS)
    ws = [jnp.where(causal, ws_ref[g], 0.0).astype(bf16) for g in range(GM_GROUPS)]
    for ch in range(GM_TM // GM_CHUNK):
        r0 = ch * GM_CHUNK
        vn = _ln(v_ref[r0:r0 + GM_CHUNK, :].astype(f32)) * g_ref[...] + b_ref[...]
        vn = vn.astype(bf16)
        for j in range(GM_WIDTH // LANES):
            slab = vn[:, j * LANES:(j + 1) * LANES]
            s_lo = jnp.dot(ws[2 * j], slab, preferred_element_type=f32)
            s_hi = jnp.dot(ws[2 * j + 1], slab, preferred_element_type=f32)
            s = jnp.where(first_half, s_lo, s_hi) + bs_ref[:, j * LANES:(j + 1) * LANES]
            u = u_ref[r0:r0 + GM_CHUNK, j * LANES:(j + 1) * LANES].astype(f32)
            ya_ref[r0:r0 + GM_CHUNK, j * LANES:(j + 1) * LANES] = (u * s).astype(bf16)


def _gmlp(uv, ln_g, ln_b, w_s, bs_full):
    t = uv.shape[0]
    return pl.pallas_call(
        _gmlp_kernel,
        grid=(t // GM_TM,),
        in_specs=[pl.BlockSpec((GM_TM, GM_WIDTH), lambda i: (i, 0)),
                  pl.BlockSpec((GM_TM, GM_WIDTH), lambda i: (i, 1)),
                  pl.BlockSpec((1, GM_WIDTH), lambda i: (0, 0)),
                  pl.BlockSpec((1, GM_WIDTH), lambda i: (0, 0)),
                  pl.BlockSpec((GM_GROUPS, GM_CHUNK, GM_CHUNK), lambda i: (0, 0, 0)),
                  pl.BlockSpec((GM_CHUNK, GM_WIDTH), lambda i: (0, 0))],
        out_specs=pl.BlockSpec((GM_TM, GM_WIDTH), lambda i: (i, 0)),
        out_shape=jax.ShapeDtypeStruct((t, GM_WIDTH), bf16),
        compiler_params=_params(("arbitrary",)),
        name="gmlp",
    )(uv, uv, ln_g, ln_b, w_s, bs_full)


def _relbias_kernel(tab_ref, bucket_ref, band_ref, out_ref):
    g = pl.program_id(0)
    bk = bucket_ref[0]
    band = band_ref[0] > 0
    for h in range(HEADS_PER_GROUP):
        acc = jnp.zeros((ATT_BLOCK, 2 * ATT_BLOCK), f32)
        for b in range(REL_BUCKETS):
            acc = jnp.where(bk == b, tab_ref[b, g * HEADS_PER_GROUP + h], acc)
        out_ref[0, h] = jnp.where(band, acc, NEG_INF)


def _relbias(rel_bias, bucket, band):
    return pl.pallas_call(
        _relbias_kernel,
        grid=(N_DIL,),
        in_specs=[pl.BlockSpec(memory_space=pltpu.SMEM),
                  pl.BlockSpec((1, ATT_BLOCK, 2 * ATT_BLOCK), lambda g: (g, 0, 0)),
                  pl.BlockSpec((1, ATT_BLOCK, 2 * ATT_BLOCK), lambda g: (g, 0, 0))],
        out_specs=pl.BlockSpec((1, HEADS_PER_GROUP, ATT_BLOCK, 2 * ATT_BLOCK),
                               lambda g: (g, 0, 0, 0)),
        out_shape=jax.ShapeDtypeStruct((N_DIL, HEADS_PER_GROUP, ATT_BLOCK, 2 * ATT_BLOCK), f32),
        compiler_params=_params(("arbitrary",)),
        name="relbias",
    )(rel_bias, bucket, band)


ATT_MAX_STEP_BLOCKS = 4


def _attn_kernel(nblk, q_ref, kp_ref, kc_ref, vp_ref, vc_ref, bias_ref, o_ref, lse_ref):
    first = pl.program_id(2) == 0
    lane = lax.broadcasted_iota(i32, (ATT_BLOCK, LANES), 1)
    lo_half = lane < HEAD_DIM
    nt = (((1,), (1,)), ((), ()))
    ones = jnp.ones((2 * ATT_BLOCK, LANES), bf16)
    n_slab = ATT_WIDTH // LANES
    logits, v_ext = [], []
    for i in range(nblk):
        cur = slice(i * ATT_BLOCK, (i + 1) * ATT_BLOCK)
        prv = slice((i - 1) * ATT_BLOCK, i * ATT_BLOCK)
        for j in range(n_slab):
            sl = slice(j * LANES, (j + 1) * LANES)
            q = q_ref[0, 0, cur, sl] * (HEAD_DIM ** -0.5)
            k_prev = kp_ref[0, 0, :, sl] if i == 0 else kc_ref[0, 0, prv, sl]
            v_prev = vp_ref[0, 0, :, sl] if i == 0 else vc_ref[0, 0, prv, sl]
            k_cat = jnp.concatenate([k_prev, kc_ref[0, 0, cur, sl]], axis=0)
            v_cat = jnp.concatenate([v_prev, vc_ref[0, 0, cur, sl]], axis=0)
            v_ext.append(jnp.concatenate([v_cat, ones], axis=1))
            for hh in range(2):
                qm = jnp.where(lo_half if hh == 0 else jnp.logical_not(lo_half), q, 0.0).astype(bf16)
                logits.append(lax.dot_general(qm, k_cat, nt, preferred_element_type=f32))
    bias = bias_ref[0].reshape(HEADS_PER_GROUP * ATT_BLOCK, 2 * ATT_BLOCK)
    rows_per_block = HEADS_PER_GROUP * ATT_BLOCK
    lg = jnp.concatenate(logits, axis=0) + jnp.concatenate([bias] * nblk, axis=0)
    row = lax.broadcasted_iota(i32, lg.shape, 0)
    col = lax.broadcasted_iota(i32, lg.shape, 1)
    no_prev = jnp.logical_and(first, jnp.logical_and(row < rows_per_block, col < ATT_BLOCK))
    lg = jnp.where(no_prev, NEG_INF, lg)
    m = jnp.max(lg, axis=-1, keepdims=True)
    p = jnp.exp(lg - m).astype(bf16)
    for i in range(nblk):
        cur = slice(i * ATT_BLOCK, (i + 1) * ATT_BLOCK)
        lse_tile = jnp.zeros((ATT_BLOCK, LANES), f32)
        for j in range(n_slab):
            outs = []
            for hh in range(2):
                h = 2 * j + hh
                r0 = i * rows_per_block + h * ATT_BLOCK
                r = jnp.dot(p[r0:r0 + ATT_BLOCK], v_ext[i * n_slab + j], preferred_element_type=f32)
                den = r[:, LANES:]
                outs.append(r[:, :LANES] * (1.0 / den))
                lse_h = m[r0:r0 + ATT_BLOCK] + jnp.log(den)
                lse_tile = jnp.where(lane == h, lse_h, lse_tile)
            o_ref[0, 0, cur, j * LANES:(j + 1) * LANES] = jnp.where(lo_half, outs[0], outs[1]).astype(bf16)
        lse_ref[0, 0, cur, :] = lse_tile


def _attn_group(qkv_g, bias, g, dil, batch, seq):
    l = seq // dil
    nblk = min(ATT_MAX_STEP_BLOCKS, l // ATT_BLOCK)
    tm = nblk * ATT_BLOCK
    nsteps = l // tm

    def cur(cb):
        return pl.BlockSpec((1, 1, tm, ATT_WIDTH), lambda b, r, n: (b, r, n, cb))

    def prev(cb):
        return pl.BlockSpec((1, 1, ATT_BLOCK, ATT_WIDTH),
                            lambda b, r, n: (b, r, jnp.maximum(n * nblk - 1, 0), cb))

    return pl.pallas_call(
        functools.partial(_attn_kernel, nblk),
        grid=(batch, dil, nsteps),
        in_specs=[cur(0), prev(1), cur(1), prev(2), cur(2),
                  pl.BlockSpec((1, HEADS_PER_GROUP, ATT_BLOCK, 2 * ATT_BLOCK),
                               lambda b, r, n: (g, 0, 0, 0))],
        out_specs=[pl.BlockSpec((1, 1, tm, ATT_WIDTH), lambda b, r, n: (b, r, n, 0)),
                   pl.BlockSpec((1, 1, tm, LANES), lambda b, r, n: (b, r, n, 0))],
        out_shape=[jax.ShapeDtypeStruct((batch, dil, l, ATT_WIDTH), bf16),
                   jax.ShapeDtypeStruct((batch, dil, l, LANES), f32)],
        compiler_params=_params(("arbitrary", "arbitrary", "arbitrary")),
        name=f"attn_g{g}",
    )(qkv_g, qkv_g, qkv_g, qkv_g, qkv_g, bias)


ROW_WORDS = D_MODEL // 2
ROW_SUB = ROW_WORDS // LANES
HI_MASK = -65536


def _pack_rows(x):
    bits = lax.bitcast_convert_type(x.astype(bf16).astype(f32), i32)
    return lax.shift_right_logical(bits[:, :ROW_WORDS], 16) | (bits[:, ROW_WORDS:] & HI_MASK)


def _unpack_rows(words):
    lo = lax.bitcast_convert_type(lax.shift_left(words, 16), f32)
    hi = lax.bitcast_convert_type(words & HI_MASK, f32)
    return jnp.concatenate([lo, hi], axis=1)


def _store_packed(ref, words, n, first_row=0):
    for r in range(ROW_SUB):
        ref[pl.ds(first_row * ROW_SUB + r, n, stride=ROW_SUB), :] = words[:, r * LANES:(r + 1) * LANES]


def _load_packed(ref, first_row, n):
    return jnp.concatenate([ref[pl.ds(first_row * ROW_SUB + r, n, stride=ROW_SUB), :] for r in range(ROW_SUB)],
                           axis=1)


MIX_TM = 256
MIX_SUB = 128


def _split_bf16(x, parts):
    out = []
    for _ in range(parts):
        hi = x.astype(bf16)
        out.append(hi)
        x = x - hi.astype(f32)
    return out


def _mix_kernel(o0_ref, o1_ref, o2_ref, l0_ref, l1_ref, l2_ref, pt1_ref, pt2_ref, ex_ref,
                ya_ref, gt_ref, x_ref,
                g1_ref, sc2_ref, sh2_ref, wa_ref, wb_ref, wo_ref, ln1g_ref, ln1b_ref,
                wrc_ref, br_ref, tri_ref,
                x1_ref, h2_ref, route_ref, rw_ref, cnt_ref, run_ref, xr_ref):
    step = pl.program_id(0)

    @pl.when(step == 0)
    def _():
        run_ref[...] = jnp.zeros_like(run_ref)
        xr_ref[...] = jnp.zeros_like(xr_ref)

    def back_rows(r0):
        rows = slice(r0, r0 + MIX_SUB)
        x1 = _ln(xr_ref[rows, :]) * ln1g_ref[...] + ln1b_ref[...]
        x1_ref[rows, :] = x1
        h2 = _ln(x1) * (1.0 + sc2_ref[0]) + sh2_ref[0]
        _store_packed(h2_ref, _pack_rows(h2), MIX_SUB, r0)
        h_hi, h_lo = _split_bf16(h2, 2)
        hi_both = jnp.dot(h_hi, wrc_ref[...], preferred_element_type=f32)
        return (hi_both[:, :LANES]
                + (hi_both[:, LANES:] + jnp.dot(h_lo, wrc_ref[:, :LANES], preferred_element_type=f32))
                ) + br_ref[...]

    def front_rows(r0):
        rows = slice(r0, r0 + MIX_SUB)
        os_, ls_ = [o0_ref[0, 0, rows, :].astype(f32)], [l0_ref[0, 0, rows, :]]
        for o_ref, l_ref, pt_ref in ((o1_ref, l1_ref, pt1_ref), (o2_ref, l2_ref, pt2_ref)):
            pt = pt_ref[rows, :]
            os_.append(jnp.dot(pt, o_ref[0].reshape(MIX_TM, ATT_WIDTH), preferred_element_type=f32))
            parts = [jnp.dot(pt, part, preferred_element_type=f32)
                     for part in _split_bf16(l_ref[0].reshape(MIX_TM, LANES), 3)]
            ls_.append((parts[0] + parts[1]) + parts[2])
        lm = jnp.maximum(jnp.maximum(ls_[0], ls_[1]), ls_[2])
        es = [jnp.exp(lse - lm) for lse in ls_]
        inv = 1.0 / (es[0] + es[1] + es[2])
        yb = jnp.zeros((MIX_SUB, ATT_WIDTH), f32)
        for e, o in zip(es, os_):
            w_parts = jnp.concatenate(_split_bf16(e * inv, 2), axis=1)
            yb = yb + jnp.dot(w_parts, ex_ref[...], preferred_element_type=f32) * o
        a = jnp.dot(ya_ref[rows, :], wa_ref[...], preferred_element_type=f32)
        b = jnp.dot(yb.astype(bf16), wb_ref[...], preferred_element_type=f32)
        merged = gt_ref[rows, :D_MODEL].astype(f32) * a + gt_ref[rows, D_MODEL:].astype(f32) * b
        mix = jnp.dot(merged.astype(bf16), wo_ref[...], preferred_element_type=f32)
        xr_ref[rows, :] = DN_ALPHA * x_ref[rows, :] + g1_ref[0] * mix

    logit_parts = []
    for r0 in range(0, MIX_TM, MIX_SUB):
        logit_parts.append(back_rows(r0))
        front_rows(r0)
    logits = jnp.concatenate(logit_parts, axis=0)
    lane = lax.broadcasted_iota(i32, (MIX_TM, LANES), 1)
    logits = jnp.where(lane < N_EXPERTS, logits, -jnp.inf)
    lane_f = lane.astype(f32)
    vals, idxs = [], []
    for _k in range(TOP_K):
        m = jnp.max(logits, axis=-1, keepdims=True)
        vals.append(m)
        idxs.append(jnp.min(jnp.where(logits == m, lane_f, float(LANES)), axis=-1, keepdims=True).astype(i32))
        logits = jnp.where(lane == idxs[-1], -jnp.inf, logits)
    exps = [jnp.exp(v - vals[0]) for v in vals]
    den = exps[0] + exps[1] + exps[2] + exps[3]
    wts = [e / den for e in exps]
    hits = [lane == idx for idx in idxs]
    counted = jnp.where(step > 0, 1.0, 0.0)
    onehot = jnp.zeros((MIX_TM, LANES), f32)
    for hit in hits:
        onehot = onehot + jnp.where(hit, counted, 0.0)
    prefix = jnp.dot(tri_ref[...], onehot.astype(bf16), preferred_element_type=f32) + run_ref[...]
    route = jnp.zeros((MIX_TM, LANES), i32)
    rw = jnp.zeros((MIX_TM, LANES), f32)
    for k in range(TOP_K):
        rank = jnp.sum(jnp.where(hits[k], prefix, 0.0), axis=-1, keepdims=True).astype(i32)
        route = jnp.where(lane == k, idxs[k], route)
        route = jnp.where(lane == TOP_K + k, rank, route)
        rw = jnp.where(lane == k, wts[k], rw)
    route_ref[...] = route
    rw_ref[...] = rw
    run = run_ref[...] + jnp.sum(onehot, axis=0, keepdims=True)
    run_ref[...] = run
    cnt_ref[...] = jnp.broadcast_to(run, cnt_ref.shape)


def _mix(os_, ls_, perms_t, expand, ya, gates, x2, g1, sc2, sh2, wa, wb, wo, ln1g, ln1b, wr_parts, br, tri, seq):
    t = x2.shape[0]
    nb = t // MIX_TM
    per_b = seq // MIX_TM
    cur = lambda i: jnp.minimum(i, nb - 1)
    prv = lambda i: jnp.maximum(i - 1, 0)
    row = lambda w: pl.BlockSpec((MIX_TM, w), lambda i: (cur(i), 0))
    out_row = lambda w: pl.BlockSpec((MIX_TM, w), lambda i: (prv(i), 0))
    const = lambda s: pl.BlockSpec(s, lambda i: tuple(0 for _ in s))
    mod_cur = pl.BlockSpec((1, 1, D_MODEL), lambda i: (cur(i) // per_b, 0, 0))
    mod_prv = pl.BlockSpec((1, 1, D_MODEL), lambda i: (prv(i) // per_b, 0, 0))
    grp = lambda w: [pl.BlockSpec((1, dil, MIX_TM // dil, w), lambda i: (cur(i) // per_b, 0, cur(i) % per_b, 0))
                     for _win, dil in DIL_PAIRS]
    return pl.pallas_call(
        _mix_kernel,
        grid=(nb + 1,),
        in_specs=grp(ATT_WIDTH) + grp(LANES) + [
                  const((MIX_TM, MIX_TM)), const((MIX_TM, MIX_TM)), const((2 * LANES, ATT_WIDTH)),
                  row(GM_WIDTH), row(GATE_COLS), row(D_MODEL),
                  mod_cur, mod_prv, mod_prv,
                  const((GM_WIDTH, D_MODEL)), const((ATT_WIDTH, D_MODEL)), const((D_MODEL, D_MODEL)),
                  const((1, D_MODEL)), const((1, D_MODEL)),
                  const((D_MODEL, 2 * LANES)), const((1, LANES)), const((MIX_TM, MIX_TM))],
        out_specs=[out_row(D_MODEL), pl.BlockSpec((MIX_TM * ROW_SUB, LANES), lambda i: (prv(i), 0)),
                   out_row(LANES), out_row(LANES), const((8, LANES))],
        out_shape=[jax.ShapeDtypeStruct((t, D_MODEL), f32),
                   jax.ShapeDtypeStruct((t * ROW_SUB, LANES), i32),
                   jax.ShapeDtypeStruct((t, LANES), i32),
                   jax.ShapeDtypeStruct((t, LANES), f32),
                   jax.ShapeDtypeStruct((8, LANES), f32)],
        scratch_shapes=[pltpu.VMEM((1, LANES), f32), pltpu.VMEM((MIX_TM, D_MODEL), f32)],
        compiler_params=_params(("arbitrary",)),
        name="mix",
    )(*os_, *ls_, perms_t[1], perms_t[2], expand, ya, gates, x2, g1, sc2, sh2, wa, wb, wo,
      ln1g, ln1b, wr_parts, br, tri)


DISP_TM = 256
MOE_TM = 256


def _dispatch_kernel(pends_ref, pcnt_ref, nused_ref, dest_ref, h2p_ref, xs_hbm, zbuf, sem, zsem):
    i = pl.program_id(0)
    ntile = xs_hbm.shape[0] // (MOE_TM * ROW_SUB)

    def zero_tile(first_row):
        return pltpu.make_async_copy(
            zbuf, xs_hbm.at[pl.ds(pl.multiple_of(first_row * ROW_SUB, MOE_TM * ROW_SUB), MOE_TM * ROW_SUB)], zsem)

    def for_each_zero_tile(fn):
        for e in range(N_EXPERTS):
            pl.when(pcnt_ref[e] > 0)(functools.partial(fn, lambda e=e: zero_tile(pends_ref[e] - MOE_TM)))
        for k in range(N_EXPERTS):
            tile = nused_ref[0] + k
            pl.when(tile < ntile)(functools.partial(fn, lambda tile=tile: zero_tile(tile * MOE_TM)))

    @pl.when(i == 0)
    def _():
        zbuf[...] = jnp.zeros_like(zbuf)
        for_each_zero_tile(lambda mk: mk().start())
        for_each_zero_tile(lambda mk: mk().wait())

    def row_copy(k, r):
        d = dest_ref[0, k, r]
        return pltpu.make_async_copy(h2p_ref.at[pl.ds(r * ROW_SUB, ROW_SUB)],
                                     xs_hbm.at[pl.ds(pl.multiple_of(d * ROW_SUB, ROW_SUB), ROW_SUB)], sem)

    for r in range(DISP_TM):
        for k in range(TOP_K):
            row_copy(k, r).start(priority=k % 2)
    for k in range(TOP_K):
        pltpu.make_async_copy(h2p_ref, xs_hbm.at[pl.ds(0, DISP_TM * ROW_SUB)], sem).wait()


def _dispatch(pends, pcounts, n_used, dest3, h2p, ntile):
    t = h2p.shape[0] // ROW_SUB
    grid_spec = pltpu.PrefetchScalarGridSpec(
        num_scalar_prefetch=3,
        grid=(t // DISP_TM,),
        in_specs=[pl.BlockSpec((1, TOP_K, DISP_TM), lambda i, *_: (i, 0, 0), memory_space=pltpu.SMEM),
                  pl.BlockSpec((DISP_TM * ROW_SUB, LANES), lambda i, *_: (i, 0))],
        out_specs=pl.BlockSpec(memory_space=pl.ANY),
        scratch_shapes=[pltpu.VMEM((MOE_TM * ROW_SUB, LANES), i32),
                        pltpu.SemaphoreType.DMA(()),
                        pltpu.SemaphoreType.DMA(())],
    )
    return pl.pallas_call(
        _dispatch_kernel,
        grid_spec=grid_spec,
        out_shape=jax.ShapeDtypeStruct((ntile * MOE_TM * ROW_SUB, LANES), i32),
        compiler_params=_params(("arbitrary",)),
        name="dispatch",
    )(pends, pcounts, n_used, dest3, h2p)


def _moe_kernel(te_ref, first_ref, nexte_ref, wslot_ref, nused_ref,
                xs_ref, wg_hbm, wu_hbm, wd_hbm, bg_ref, bu_ref, bd_ref,
                out_ref, wbuf, wgb, wub, wdb, sem_w):
    j = pl.program_id(0)

    def weight_copies(e, ws):
        return [pltpu.make_async_copy(w.at[e], wbuf.at[ws, k], sem_w.at[ws])
                for k, w in enumerate((wg_hbm, wu_hbm, wd_hbm))]

    @pl.when(j == 0)
    def _():
        for cp in weight_copies(te_ref[0], wslot_ref[0]):
            cp.start()

    @pl.when(first_ref[j] == 1)
    def _():
        ws = wslot_ref[j]
        for cp in weight_copies(te_ref[j], ws):
            cp.wait()
        wgb[...] = wbuf[ws, 0].astype(bf16)
        wub[...] = wbuf[ws, 1].astype(bf16)
        wdb[...] = wbuf[ws, 2].astype(bf16)
        ne = nexte_ref[j]

        @pl.when(ne >= 0)
        def _():
            for cp in weight_copies(ne, 1 - ws):
                cp.start()

    used = j < nused_ref[0]

    @pl.when(used)
    def _():
        xb = _unpack_rows(_load_packed(xs_ref, 0, MOE_TM)).astype(bf16)
        g = jnp.dot(xb, wgb[...], preferred_element_type=f32) + bg_ref[0]
        u = jnp.dot(xb, wub[...], preferred_element_type=f32) + bu_ref[0]
        g = jnp.minimum(g, SWIGLU_LIMIT)
        u = jnp.clip(u, -SWIGLU_LIMIT, SWIGLU_LIMIT)
        act = (u + 1.0) * (g * jax.nn.sigmoid(SWIGLU_ALPHA * g))
        y = jnp.dot(act.astype(bf16), wdb[...], preferred_element_type=f32) + bd_ref[0]
        _store_packed(out_ref, _pack_rows(y), MOE_TM)

    @pl.when(jnp.logical_not(used))
    def _():
        out_ref[...] = jnp.zeros_like(out_ref)


def _moe(tile_e, tile_first, next_e, wslot, n_used, xs, w_gate, b_gate, w_up, b_up, w_down, b_down):
    ntile = tile_e.shape[0]
    bspec = pl.BlockSpec((1, 1, D_MODEL), lambda j, te, *_: (te[j], 0, 0))
    hbm = pl.BlockSpec(memory_space=pl.ANY)
    grid_spec = pltpu.PrefetchScalarGridSpec(
        num_scalar_prefetch=5,
        grid=(ntile,),
        in_specs=[pl.BlockSpec((MOE_TM * ROW_SUB, LANES),
                               lambda j, te, fi, ne, ws, nu: (jnp.minimum(j, nu[0] - 1), 0)),
                  hbm, hbm, hbm, bspec, bspec, bspec],
        out_specs=pl.BlockSpec((MOE_TM * ROW_SUB, LANES), lambda j, *_: (j, 0)),
        scratch_shapes=[pltpu.VMEM((2, 3, D_MODEL, D_MODEL), f32),
                        pltpu.VMEM((D_MODEL, D_MODEL), bf16),
                        pltpu.VMEM((D_MODEL, D_MODEL), bf16),
                        pltpu.VMEM((D_MODEL, D_MODEL), bf16),
                        pltpu.SemaphoreType.DMA((2,))],
    )
    return pl.pallas_call(
        _moe_kernel,
        grid_spec=grid_spec,
        out_shape=jax.ShapeDtypeStruct((ntile * MOE_TM * ROW_SUB, LANES), i32),
        compiler_params=_params(("arbitrary",)),
        name="moe",
    )(tile_e, tile_first, next_e, wslot, n_used, xs, w_gate, w_up, w_down, b_gate, b_up, b_down)


CB_TM = 256


def _combine_kernel(dcur_ref, dnxt_ref, yb_hbm, rw_ref, x1_ref, g2_ref, lng_ref, lnb_ref, out_ref,
                    ybuf0, ybuf1, sem):
    i = pl.program_id(0)
    last = pl.num_programs(0) - 1
    ybufs = (ybuf0, ybuf1)

    def row_copy(d, k, r, s):
        return pltpu.make_async_copy(
            yb_hbm.at[pl.ds(pl.multiple_of(d * ROW_SUB, ROW_SUB), ROW_SUB)],
            ybufs[s].at[pl.ds(pl.multiple_of((k * CB_TM + r) * ROW_SUB, ROW_SUB), ROW_SUB)],
            sem.at[s])

    @pl.when(i == 0)
    def _():
        for k in range(TOP_K):
            def body(r, c, k=k):
                row_copy(dcur_ref[0, k, r], k, r, 0).start()
                return c
            lax.fori_loop(0, CB_TM, body, 0, unroll=8)

    for s in range(2):
        @pl.when(i % 2 == s)
        def _(s=s):
            pltpu.make_async_copy(yb_hbm.at[pl.ds(0, TOP_K * CB_TM * ROW_SUB)], ybufs[s], sem.at[s]).wait()

            @pl.when(i < last)
            def _():
                for k in range(TOP_K):
                    for r in range(CB_TM):
                        row_copy(dnxt_ref[0, k, r], k, r, 1 - s).start(priority=r % 2)

            parts = [_unpack_rows(_load_packed(ybufs[s], k * CB_TM, CB_TM)) * rw_ref[:, k:k + 1]
                     for k in range(TOP_K)]
            y = (parts[0] + parts[1]) + (parts[2] + parts[3])
            out_ref[...] = _ln(DN_ALPHA * x1_ref[...] + g2_ref[0] * y) * lng_ref[...] + lnb_ref[...]


def _combine(dest3, yb, rw, x1, g2, ln2g, ln2b, seq):
    t = x1.shape[0]
    nb = t // CB_TM
    per_b = seq // CB_TM
    return pl.pallas_call(
        _combine_kernel,
        grid=(nb,),
        in_specs=[pl.BlockSpec((1, TOP_K, CB_TM), lambda i: (i, 0, 0), memory_space=pltpu.SMEM),
                  pl.BlockSpec((1, TOP_K, CB_TM), lambda i: (jnp.minimum(i + 1, nb - 1), 0, 0),
                               memory_space=pltpu.SMEM),
                  pl.BlockSpec(memory_space=pl.ANY),
                  pl.BlockSpec((CB_TM, LANES), lambda i: (i, 0)),
                  pl.BlockSpec((CB_TM, D_MODEL), lambda i: (i, 0)),
                  pl.BlockSpec((1, 1, D_MODEL), lambda i: (i // per_b, 0, 0)),
                  pl.BlockSpec((1, D_MODEL), lambda i: (0, 0)),
                  pl.BlockSpec((1, D_MODEL), lambda i: (0, 0))],
        out_specs=pl.BlockSpec((CB_TM, D_MODEL), lambda i: (i, 0)),
        out_shape=jax.ShapeDtypeStruct((t, D_MODEL), f32),
        scratch_shapes=[pltpu.VMEM((TOP_K * CB_TM * ROW_SUB, LANES), i32),
                        pltpu.VMEM((TOP_K * CB_TM * ROW_SUB, LANES), i32),
                        pltpu.SemaphoreType.DMA((2,))],
        compiler_params=_params(("arbitrary",)),
        name="combine",
    )(dest3, dest3, yb, rw, x1, g2, ln2g, ln2b)


def _t5_bucket(dist):
    d = dist.astype(f32)
    large = REL_MAX_EXACT + jnp.log(jnp.maximum(d, float(REL_MAX_EXACT)) / REL_MAX_EXACT) / math.log(
        REL_MAX_DIST / REL_MAX_EXACT) * (REL_BUCKETS - REL_MAX_EXACT)
    large = jnp.minimum(large.astype(i32), REL_BUCKETS - 1)
    return jnp.where(dist < REL_MAX_EXACT, dist, large)


def _bias_indices():
    qi = jnp.arange(ATT_BLOCK)[:, None]
    ki = jnp.arange(2 * ATT_BLOCK)[None, :]
    didx = qi + ATT_BLOCK - ki
    buckets, bands = [], []
    for win, dil in DIL_PAIRS:
        buckets.append(_t5_bucket(jnp.clip(didx, 0, None) * dil))
        bands.append(((didx >= 0) & (didx <= win // dil)).astype(i32))
    return jnp.stack(buckets).astype(i32), jnp.stack(bands)


def _residue_perm(tm, dil):
    n = tm // dil
    dst = jnp.arange(tm)
    src = (dst % n) * dil + dst // n
    return (src[:, None] == jnp.arange(tm)[None, :]).astype(bf16)


def kernel(x, c, w_ada, b_ada, w_in, gm_ln_g, gm_ln_b, gm_w_s, gm_b_s, w_branch_a, w_branch_b, w_out,
           rel_bias, ln1_g, ln1_b, w_router, b_router, w_gate, b_gate, w_up, b_up, w_down, b_down,
           ln2_g, ln2_b):
    batch, seq, _ = x.shape
    t = batch * seq
    l = 0
    x2 = x.reshape(t, D_MODEL)

    c8 = jnp.pad(c, ((0, 8 - batch), (0, 0)))
    mod = _adaln(c8, w_ada[l], b_ada[l][None, :])[:batch]
    sh1, sc1, g1, sh2, sc2, g2 = [m[:, None, :] for m in jnp.split(mod, 6, axis=-1)]

    perms = [_residue_perm(IN_TM, dil) for _win, dil in DIL_PAIRS]
    uv, gates, *qkvs = _inproj(x2, sc1, sh1, w_in[l].astype(bf16), perms, batch, seq)

    bs_full = jnp.repeat(gm_b_s[l].T, GM_WIDTH // GM_GROUPS, axis=1)
    ya = _gmlp(uv, gm_ln_g[l][None, :], gm_ln_b[l][None, :], gm_w_s[l], bs_full)

    bucket, band = _bias_indices()
    bias = _relbias(rel_bias, bucket, band)
    os_, ls_ = [], []
    for g, (_win, dil) in enumerate(DIL_PAIRS):
        o, lse = _attn_group(qkvs[g], bias, g, dil, batch, seq)
        os_.append(o)
        ls_.append(lse)

    wr = jnp.pad(w_router[l], ((0, 0), (0, LANES - N_EXPERTS)))
    wr_hi = wr.astype(bf16)
    wr_parts = jnp.concatenate([wr_hi, (wr - wr_hi.astype(f32)).astype(bf16)], axis=1)
    br = jnp.pad(b_router[l], (0, LANES - N_EXPERTS))[None, :]
    tri = (jnp.arange(MIX_TM)[None, :] < jnp.arange(MIX_TM)[:, None]).astype(bf16)
    perms_t = [_residue_perm(MIX_TM, dil).T for _win, dil in DIL_PAIRS]
    expand = (jnp.arange(LANES)[:, None] == jnp.arange(ATT_WIDTH)[None, :] // HEAD_DIM).astype(bf16)
    expand = jnp.concatenate([expand, expand], axis=0)
    x1, h2, route, rw, cnt = _mix(
        os_, ls_, perms_t, expand, ya, gates, x2, g1, sc2, sh2,
        w_branch_a[l].astype(bf16), w_branch_b[l].astype(bf16), w_out[l].astype(bf16),
        ln1_g[l][None, :], ln1_b[l][None, :], wr_parts, br, tri, seq)

    top_e = route[:, :TOP_K]
    rank = route[:, TOP_K:2 * TOP_K]
    counts = cnt[0, :N_EXPERTS].astype(i32)
    pcounts = (counts + MOE_TM - 1) // MOE_TM * MOE_TM
    pends = jnp.cumsum(pcounts)
    pstarts = pends - pcounts
    experts = jnp.arange(N_EXPERTS, dtype=i32)
    dest = jnp.sum(jnp.where(top_e[:, :, None] == experts, pstarts, 0), axis=-1) + rank
    ntile = t * TOP_K // MOE_TM + N_EXPERTS
    n_used = (pends[-1] // MOE_TM).reshape(1)
    tile_idx = jnp.minimum(jnp.arange(ntile, dtype=i32), n_used - 1)
    tile_e = jnp.sum((pends[None, :] <= (tile_idx * MOE_TM)[:, None]).astype(i32), axis=1)
    tile_first = jnp.concatenate([jnp.ones((1,), i32), (tile_e[1:] != tile_e[:-1]).astype(i32)])
    nonempty = counts > 0
    later = lax.cummin(jnp.where(nonempty, experts, N_EXPERTS), reverse=True)
    next_nonempty = jnp.concatenate([later[1:], jnp.full((1,), N_EXPERTS, i32)])
    next_nonempty = jnp.where(next_nonempty >= N_EXPERTS, -1, next_nonempty)
    expert_slot = (jnp.cumsum(nonempty.astype(i32)) - 1) % 2

    dest3 = dest.reshape(t // DISP_TM, DISP_TM, TOP_K).transpose(0, 2, 1)
    xs = _dispatch(pends, pcounts, n_used, dest3, h2, ntile)
    yb = _moe(tile_e, tile_first, next_nonempty[tile_e], expert_slot[tile_e], n_used, xs,
              w_gate[l], b_gate[l][:, None, :], w_up[l], b_up[l][:, None, :],
              w_down[l], b_down[l][:, None, :])
    out = _combine(dest3, yb, rw, x1, g2, ln2_g[l][None, :], ln2_b[l][None, :], seq)
    return out.reshape(batch, seq, D_MODEL)
```

```python
import functools
import math

import jax
import jax.numpy as jnp
from jax import lax
from jax.experimental import pallas as pl
from jax.experimental.pallas import tpu as pltpu

f32 = jnp.float32
bf16 = jnp.bfloat16
i32 = jnp.int32

D_MODEL = 1024
GM_WIDTH = 512
GM_GROUPS = 8
GM_CHUNK = 128
DIL_PAIRS = ((128, 1), (512, 4), (2048, 16))
N_DIL = 3
HEADS_PER_GROUP = 8
HEAD_DIM = 64
ATT_WIDTH = 512
ATT_BLOCK = 128
NEG_INF = -1e30
REL_BUCKETS = 32
REL_MAX_EXACT = 16
REL_MAX_DIST = 2048
N_EXPERTS = 32
TOP_K = 4
SWIGLU_LIMIT = 7.0
SWIGLU_ALPHA = 1.702
MOE_BLOCK = 128
DEPTH = 1
DN_ALPHA = (2 * DEPTH) ** 0.25
LN_EPS = 1e-5
UV_COLS = 2 * GM_WIDTH
QKV_COLS = N_DIL * 3 * ATT_WIDTH
GATE_COLS = 2 * D_MODEL
IN_COLS = UV_COLS + QKV_COLS + GATE_COLS

LANES = 128
SUBLANES = 8
VMEM_LIMIT = 56 * 1024 * 1024


def _ln(x):
    mu = jnp.mean(x, axis=-1, keepdims=True)
    xc = x - mu
    var = jnp.mean(xc * xc, axis=-1, keepdims=True)
    return xc * lax.rsqrt(var + LN_EPS)


def _params(sem, vmem=VMEM_LIMIT):
    return pltpu.CompilerParams(dimension_semantics=sem, vmem_limit_bytes=vmem)


def _adaln_kernel(c_ref, w_ref, b_ref, o_ref):
    c = c_ref[...]
    s = c * jax.nn.sigmoid(c)
    o_ref[...] = jnp.dot(s, w_ref[...], preferred_element_type=f32,
                         precision=lax.Precision.HIGHEST) + b_ref[...]


def _adaln(c8, w_ada, b_ada):
    n = w_ada.shape[1] // D_MODEL
    return pl.pallas_call(
        _adaln_kernel,
        grid=(n,),
        in_specs=[pl.BlockSpec((8, D_MODEL), lambda j: (0, 0)),
                  pl.BlockSpec((D_MODEL, D_MODEL), lambda j: (0, j)),
                  pl.BlockSpec((1, D_MODEL), lambda j: (0, j))],
        out_specs=pl.BlockSpec((8, D_MODEL), lambda j: (0, j)),
        out_shape=jax.ShapeDtypeStruct((8, w_ada.shape[1]), f32),
        compiler_params=_params(("arbitrary",)),
        name="adaln",
    )(c8, w_ada, b_ada)


IN_TM = 256
IN_CW = 512
GRP_COLS = 3 * ATT_WIDTH


IN_PREP_ROWS = 32


def _inproj_kernel(x0_ref, xn_ref, sc0_ref, sh0_ref, scn_ref, shn_ref, w_ref, p1_ref, p2_ref,
                   uv_ref, gt_ref, qkv0_ref, qkv1_ref, qkv2_ref, h_ref, hp1_ref, hp2_ref, hn_ref):
    def modulated(x, sc_ref, sh_ref):
        return (_ln(x) * (1.0 + sc_ref[0]) + sh_ref[0]).astype(bf16)

    def regroup(h):
        hp1_ref[...] = jnp.dot(p1_ref[...], h, preferred_element_type=f32).astype(bf16)
        hp2_ref[...] = jnp.dot(p2_ref[...], h, preferred_element_type=f32).astype(bf16)

    @pl.when(pl.program_id(0) == 0)
    def _():
        h0 = modulated(x0_ref[...], sc0_ref, sh0_ref)
        h_ref[...] = h0
        regroup(h0)

    def uv_chunk(c0):
        acc = jnp.dot(h_ref[...], w_ref[:, c0:c0 + IN_CW], preferred_element_type=f32)
        uv_ref[:, c0:c0 + IN_CW] = jax.nn.gelu(acc).astype(bf16)

    def qkv_chunk(g, q0):
        operand = (h_ref, hp1_ref, hp2_ref)[g]
        qref, dil = (qkv0_ref, qkv1_ref, qkv2_ref)[g], DIL_PAIRS[g][1]
        n = IN_TM // dil
        c0 = UV_COLS + g * GRP_COLS + q0
        acc = jnp.dot(operand[...], w_ref[:, c0:c0 + IN_CW], preferred_element_type=f32).astype(bf16)
        for rho in range(dil):
            qref[0, rho, :, q0:q0 + IN_CW] = acc[rho * n:(rho + 1) * n, :]

    def gate_chunk(g0):
        c0 = UV_COLS + QKV_COLS + g0
        acc = jnp.dot(h_ref[...], w_ref[:, c0:c0 + IN_CW], preferred_element_type=f32)
        gt_ref[:, g0:g0 + IN_CW] = jax.nn.sigmoid(acc).astype(bf16)

    chunks = ([functools.partial(uv_chunk, c0) for c0 in range(0, UV_COLS, IN_CW)]
              + [functools.partial(qkv_chunk, g, q0) for g in range(N_DIL) for q0 in range(0, GRP_COLS, IN_CW)]
              + [functools.partial(gate_chunk, g0) for g0 in range(0, GATE_COLS, IN_CW)])
    for k, chunk in enumerate(chunks):
        chunk()
        r0 = k * IN_PREP_ROWS
        if r0 < IN_TM:
            hn_ref[r0:r0 + IN_PREP_ROWS, :] = modulated(xn_ref[r0:r0 + IN_PREP_ROWS, :], scn_ref, shn_ref)
    hn = hn_ref[...]
    h_ref[...] = hn
    regroup(hn)


def _inproj(x2, sc1, sh1, w_in_bf, perms, batch, seq):
    t = x2.shape[0]
    nb = t // IN_TM
    per_b = seq // IN_TM
    nxt = lambda i: jnp.minimum(i + 1, nb - 1)
    qkv_specs, qkv_shapes = [], []
    for _win, dil in DIL_PAIRS:
        n = IN_TM // dil
        qkv_specs.append(pl.BlockSpec((1, dil, n, GRP_COLS), lambda i: (i // per_b, 0, i % per_b, 0)))
        qkv_shapes.append(jax.ShapeDtypeStruct((batch, dil, seq // dil, GRP_COLS), bf16))
    mod_cur = pl.BlockSpec((1, 1, D_MODEL), lambda i: (i // per_b, 0, 0))
    mod_nxt = pl.BlockSpec((1, 1, D_MODEL), lambda i: (nxt(i) // per_b, 0, 0))
    return pl.pallas_call(
        _inproj_kernel,
        grid=(nb,),
        in_specs=[pl.BlockSpec((IN_TM, D_MODEL), lambda i: (i, 0)),
                  pl.BlockSpec((IN_TM, D_MODEL), lambda i: (nxt(i), 0)),
                  mod_cur, mod_cur, mod_nxt, mod_nxt,
                  pl.BlockSpec((D_MODEL, IN_COLS), lambda i: (0, 0)),
                  pl.BlockSpec((IN_TM, IN_TM), lambda i: (0, 0)),
                  pl.BlockSpec((IN_TM, IN_TM), lambda i: (0, 0))],
        out_specs=[pl.BlockSpec((IN_TM, UV_COLS), lambda i: (i, 0)),
                   pl.BlockSpec((IN_TM, GATE_COLS), lambda i: (i, 0))] + qkv_specs,
        out_shape=[jax.ShapeDtypeStruct((t, UV_COLS), bf16),
                   jax.ShapeDtypeStruct((t, GATE_COLS), bf16)] + qkv_shapes,
        scratch_shapes=[pltpu.VMEM((IN_TM, D_MODEL), bf16)] * 4,
        compiler_params=_params(("arbitrary",)),
        name="inproj",
    )(x2, x2, sc1, sh1, sc1, sh1, w_in_bf, perms[1], perms[2])


GM_TM = 512


def _gmlp_kernel(u_ref, v_ref, g_ref, b_ref, ws_ref, bs_ref, ya_ref):
    row = lax.broadcasted_iota(i32, (GM_CHUNK, GM_CHUNK), 0)
    col = lax.broadcasted_iota(i32, (GM_CHUNK, GM_CHUNK), 1)
    causal = col <= row
    first_half = lax.broadcasted_iota(i32, (GM_CHUNK, LANES), 1) < (GM_WIDTH // GM_GROUPS)
    ws = [jnp.where(causal, ws_ref[g], 0.0).astype(bf16) for g in range(GM_GROUPS)]
    for ch in range(GM_TM // GM_CHUNK):
        r0 = ch * GM_CHUNK
        vn = _ln(v_ref[r0:r0 + GM_CHUNK, :].astype(f32)) * g_ref[...] + b_ref[...]
        vn = vn.astype(bf16)
        for j in range(GM_WIDTH // LANES):
            slab = vn[:, j * LANES:(j + 1) * LANES]
            s_lo = jnp.dot(ws[2 * j], slab, preferred_element_type=f32)
            s_hi = jnp.dot(ws[2 * j + 1], slab, preferred_element_type=f32)
            s = jnp.where(first_half, s_lo, s_hi) + bs_ref[:, j * LANES:(j + 1) * LANES]
            u = u_ref[r0:r0 + GM_CHUNK, j * LANES:(j + 1) * LANES].astype(f32)
            ya_ref[r0:r0 + GM_CHUNK, j * LANES:(j + 1) * LANES] = (u * s).astype(bf16)


def _gmlp(uv, ln_g, ln_b, w_s, bs_full):
    t = uv.shape[0]
    return pl.pallas_call(
        _gmlp_kernel,
        grid=(t // GM_TM,),
        in_specs=[pl.BlockSpec((GM_TM, GM_WIDTH), lambda i: (i, 0)),
                  pl.BlockSpec((GM_TM, GM_WIDTH), lambda i: (i, 1)),
                  pl.BlockSpec((1, GM_WIDTH), lambda i: (0, 0)),
                  pl.BlockSpec((1, GM_WIDTH), lambda i: (0, 0)),
                  pl.BlockSpec((GM_GROUPS, GM_CHUNK, GM_CHUNK), lambda i: (0, 0, 0)),
                  pl.BlockSpec((GM_CHUNK, GM_WIDTH), lambda i: (0, 0))],
        out_specs=pl.BlockSpec((GM_TM, GM_WIDTH), lambda i: (i, 0)),
        out_shape=jax.ShapeDtypeStruct((t, GM_WIDTH), bf16),
        compiler_params=_params(("arbitrary",)),
        name="gmlp",
    )(uv, uv, ln_g, ln_b, w_s, bs_full)


def _relbias_kernel(tab_ref, bucket_ref, band_ref, out_ref):
    g = pl.program_id(0)
    bk = bucket_ref[0]
    band = band_ref[0] > 0
    for h in range(HEADS_PER_GROUP):
        acc = jnp.zeros((ATT_BLOCK, 2 * ATT_BLOCK), f32)
        for b in range(REL_BUCKETS):
            acc = jnp.where(bk == b, tab_ref[b, g * HEADS_PER_GROUP + h], acc)
        out_ref[0, h] = jnp.where(band, acc, NEG_INF)


def _relbias(rel_bias, bucket, band):
    return pl.pallas_call(
        _relbias_kernel,
        grid=(N_DIL,),
        in_specs=[pl.BlockSpec(memory_space=pltpu.SMEM),
                  pl.BlockSpec((1, ATT_BLOCK, 2 * ATT_BLOCK), lambda g: (g, 0, 0)),
                  pl.BlockSpec((1, ATT_BLOCK, 2 * ATT_BLOCK), lambda g: (g, 0, 0))],
        out_specs=pl.BlockSpec((1, HEADS_PER_GROUP, ATT_BLOCK, 2 * ATT_BLOCK),
                               lambda g: (g, 0, 0, 0)),
        out_shape=jax.ShapeDtypeStruct((N_DIL, HEADS_PER_GROUP, ATT_BLOCK, 2 * ATT_BLOCK), f32),
        compiler_params=_params(("arbitrary",)),
        name="relbias",
    )(rel_bias, bucket, band)


ATT_MAX_STEP_BLOCKS = 4


def _attn_kernel(nblk, q_ref, kp_ref, kc_ref, vp_ref, vc_ref, bias_ref, o_ref, lse_ref):
    first = pl.program_id(2) == 0
    lane = lax.broadcasted_iota(i32, (ATT_BLOCK, LANES), 1)
    lo_half = lane < HEAD_DIM
    nt = (((1,), (1,)), ((), ()))
    ones = jnp.ones((2 * ATT_BLOCK, LANES), bf16)
    n_slab = ATT_WIDTH // LANES
    logits, v_ext = [], []
    for i in range(nblk):
        cur = slice(i * ATT_BLOCK, (i + 1) * ATT_BLOCK)
        prv = slice((i - 1) * ATT_BLOCK, i * ATT_BLOCK)
        for j in range(n_slab):
            sl = slice(j * LANES, (j + 1) * LANES)
            q = q_ref[0, 0, cur, sl] * (HEAD_DIM ** -0.5)
            k_prev = kp_ref[0, 0, :, sl] if i == 0 else kc_ref[0, 0, prv, sl]
            v_prev = vp_ref[0, 0, :, sl] if i == 0 else vc_ref[0, 0, prv, sl]
            k_cat = jnp.concatenate([k_prev, kc_ref[0, 0, cur, sl]], axis=0)
            v_cat = jnp.concatenate([v_prev, vc_ref[0, 0, cur, sl]], axis=0)
            v_ext.append(jnp.concatenate([v_cat, ones], axis=1))
            for hh in range(2):
                qm = jnp.where(lo_half if hh == 0 else jnp.logical_not(lo_half), q, 0.0).astype(bf16)
                logits.append(lax.dot_general(qm, k_cat, nt, preferred_element_type=f32))
    bias = bias_ref[0].reshape(HEADS_PER_GROUP * ATT_BLOCK, 2 * ATT_BLOCK)
    rows_per_block = HEADS_PER_GROUP * ATT_BLOCK
    lg = jnp.concatenate(logits, axis=0) + jnp.concatenate([bias] * nblk, axis=0)
    row = lax.broadcasted_iota(i32, lg.shape, 0)
    col = lax.broadcasted_iota(i32, lg.shape, 1)
    no_prev = jnp.logical_and(first, jnp.logical_and(row < rows_per_block, col < ATT_BLOCK))
    lg = jnp.where(no_prev, NEG_INF, lg)
    m = jnp.max(lg, axis=-1, keepdims=True)
    p = jnp.exp(lg - m).astype(bf16)
    for i in range(nblk):
        cur = slice(i * ATT_BLOCK, (i + 1) * ATT_BLOCK)
        lse_tile = jnp.zeros((ATT_BLOCK, LANES), f32)
        for j in range(n_slab):
            outs = []
            for hh in range(2):
                h = 2 * j + hh
                r0 = i * rows_per_block + h * ATT_BLOCK
                r = jnp.dot(p[r0:r0 + ATT_BLOCK], v_ext[i * n_slab + j], preferred_element_type=f32)
                den = r[:, LANES:]
                outs.append(r[:, :LANES] * (1.0 / den))
                lse_h = m[r0:r0 + ATT_BLOCK] + jnp.log(den)
                lse_tile = jnp.where(lane == h, lse_h, lse_tile)
            o_ref[0, 0, cur, j * LANES:(j + 1) * LANES] = jnp.where(lo_half, outs[0], outs[1]).astype(bf16)
        lse_ref[0, 0, cur, :] = lse_tile


def _attn_group(qkv_g, bias, g, dil, batch, seq):
    l = seq // dil
    nblk = min(ATT_MAX_STEP_BLOCKS, l // ATT_BLOCK)
    tm = nblk * ATT_BLOCK
    nsteps = l // tm

    def cur(cb):
        return pl.BlockSpec((1, 1, tm, ATT_WIDTH), lambda b, r, n: (b, r, n, cb))

    def prev(cb):
        return pl.BlockSpec((1, 1, ATT_BLOCK, ATT_WIDTH),
                            lambda b, r, n: (b, r, jnp.maximum(n * nblk - 1, 0), cb))

    return pl.pallas_call(
        functools.partial(_attn_kernel, nblk),
        grid=(batch, dil, nsteps),
        in_specs=[cur(0), prev(1), cur(1), prev(2), cur(2),
                  pl.BlockSpec((1, HEADS_PER_GROUP, ATT_BLOCK, 2 * ATT_BLOCK),
                               lambda b, r, n: (g, 0, 0, 0))],
        out_specs=[pl.BlockSpec((1, 1, tm, ATT_WIDTH), lambda b, r, n: (b, r, n, 0)),
                   pl.BlockSpec((1, 1, tm, LANES), lambda b, r, n: (b, r, n, 0))],
        out_shape=[jax.ShapeDtypeStruct((batch, dil, l, ATT_WIDTH), bf16),
                   jax.ShapeDtypeStruct((batch, dil, l, LANES), f32)],
        compiler_params=_params(("arbitrary", "arbitrary", "arbitrary")),
        name=f"attn_g{g}",
    )(qkv_g, qkv_g, qkv_g, qkv_g, qkv_g, bias)


ROW_WORDS = D_MODEL // 2
ROW_SUB = ROW_WORDS // LANES
HI_MASK = -65536


def _pack_rows(x):
    bits = lax.bitcast_convert_type(x.astype(bf16).astype(f32), i32)
    return lax.shift_right_logical(bits[:, :ROW_WORDS], 16) | (bits[:, ROW_WORDS:] & HI_MASK)


def _unpack_rows(words):
    lo = lax.bitcast_convert_type(lax.shift_left(words, 16), f32)
    hi = lax.bitcast_convert_type(words & HI_MASK, f32)
    return jnp.concatenate([lo, hi], axis=1)


def _store_packed(ref, words, n, first_row=0):
    for r in range(ROW_SUB):
        ref[pl.ds(first_row * ROW_SUB + r, n, stride=ROW_SUB), :] = words[:, r * LANES:(r + 1) * LANES]


def _load_packed(ref, first_row, n):
    return jnp.concatenate([ref[pl.ds(first_row * ROW_SUB + r, n, stride=ROW_SUB), :] for r in range(ROW_SUB)],
                           axis=1)


MIX_TM = 256
MIX_SUB = 128


def _split_bf16(x, parts):
    out = []
    for _ in range(parts):
        hi = x.astype(bf16)
        out.append(hi)
        x = x - hi.astype(f32)
    return out


def _mix_kernel(o0_ref, o1_ref, o2_ref, l0_ref, l1_ref, l2_ref, pt1_ref, pt2_ref, ex_ref,
                ya_ref, gt_ref, x_ref,
                g1_ref, sc2_ref, sh2_ref, wa_ref, wb_ref, wo_ref, ln1g_ref, ln1b_ref,
                wrc_ref, br_ref, tri_ref,
                x1_ref, h2_ref, route_ref, rw_ref, cnt_ref, run_ref, xr_ref):
    step = pl.program_id(0)

    @pl.when(step == 0)
    def _():
        run_ref[...] = jnp.zeros_like(run_ref)
        xr_ref[...] = jnp.zeros_like(xr_ref)

    def back_rows(r0):
        rows = slice(r0, r0 + MIX_SUB)
        x1 = _ln(xr_ref[rows, :]) * ln1g_ref[...] + ln1b_ref[...]
        x1_ref[rows, :] = x1
        h2 = _ln(x1) * (1.0 + sc2_ref[0]) + sh2_ref[0]
        _store_packed(h2_ref, _pack_rows(h2), MIX_SUB, r0)
        h_hi, h_lo = _split_bf16(h2, 2)
        hi_both = jnp.dot(h_hi, wrc_ref[...], preferred_element_type=f32)
        return (hi_both[:, :LANES]
                + (hi_both[:, LANES:] + jnp.dot(h_lo, wrc_ref[:, :LANES], preferred_element_type=f32))
                ) + br_ref[...]

    def front_rows(r0):
        rows = slice(r0, r0 + MIX_SUB)
        os_, ls_ = [o0_ref[0, 0, rows, :].astype(f32)], [l0_ref[0, 0, rows, :]]
        for o_ref, l_ref, pt_ref in ((o1_ref, l1_ref, pt1_ref), (o2_ref, l2_ref, pt2_ref)):
            pt = pt_ref[rows, :]
            os_.append(jnp.dot(pt, o_ref[0].reshape(MIX_TM, ATT_WIDTH), preferred_element_type=f32))
            parts = [jnp.dot(pt, part, preferred_element_type=f32)
                     for part in _split_bf16(l_ref[0].reshape(MIX_TM, LANES), 3)]
            ls_.append((parts[0] + parts[1]) + parts[2])
        lm = jnp.maximum(jnp.maximum(ls_[0], ls_[1]), ls_[2])
        es = [jnp.exp(lse - lm) for lse in ls_]
        inv = 1.0 / (es[0] + es[1] + es[2])
        yb = jnp.zeros((MIX_SUB, ATT_WIDTH), f32)
        for e, o in zip(es, os_):
            w_parts = jnp.concatenate(_split_bf16(e * inv, 2), axis=1)
            yb = yb + jnp.dot(w_parts, ex_ref[...], preferred_element_type=f32) * o
        a = jnp.dot(ya_ref[rows, :], wa_ref[...], preferred_element_type=f32)
        b = jnp.dot(yb.astype(bf16), wb_ref[...], preferred_element_type=f32)
        merged = gt_ref[rows, :D_MODEL].astype(f32) * a + gt_ref[rows, D_MODEL:].astype(f32) * b
        mix = jnp.dot(merged.astype(bf16), wo_ref[...], preferred_element_type=f32)
        xr_ref[rows, :] = DN_ALPHA * x_ref[rows, :] + g1_ref[0] * mix

    logit_parts = []
    for r0 in range(0, MIX_TM, MIX_SUB):
        logit_parts.append(back_rows(r0))
        front_rows(r0)
    logits = jnp.concatenate(logit_parts, axis=0)
    lane = lax.broadcasted_iota(i32, (MIX_TM, LANES), 1)
    logits = jnp.where(lane < N_EXPERTS, logits, -jnp.inf)
    lane_f = lane.astype(f32)
    vals, idxs = [], []
    for _k in range(TOP_K):
        m = jnp.max(logits, axis=-1, keepdims=True)
        vals.append(m)
        idxs.append(jnp.min(jnp.where(logits == m, lane_f, float(LANES)), axis=-1, keepdims=True).astype(i32))
        logits = jnp.where(lane == idxs[-1], -jnp.inf, logits)
    exps = [jnp.exp(v - vals[0]) for v in vals]
    den = exps[0] + exps[1] + exps[2] + exps[3]
    wts = [e / den for e in exps]
    hits = [lane == idx for idx in idxs]
    counted = jnp.where(step > 0, 1.0, 0.0)
    onehot = jnp.zeros((MIX_TM, LANES), f32)
    for hit in hits:
        onehot = onehot + jnp.where(hit, counted, 0.0)
    prefix = jnp.dot(tri_ref[...], onehot.astype(bf16), preferred_element_type=f32) + run_ref[...]
    route = jnp.zeros((MIX_TM, LANES), i32)
    rw = jnp.zeros((MIX_TM, LANES), f32)
    for k in range(TOP_K):
        rank = jnp.sum(jnp.where(hits[k], prefix, 0.0), axis=-1, keepdims=True).astype(i32)
        route = jnp.where(lane == k, idxs[k], route)
        route = jnp.where(lane == TOP_K + k, rank, route)
        rw = jnp.where(lane == k, wts[k], rw)
    route_ref[...] = route
    rw_ref[...] = rw
    run = run_ref[...] + jnp.sum(onehot, axis=0, keepdims=True)
    run_ref[...] = run
    cnt_ref[...] = jnp.broadcast_to(run, cnt_ref.shape)


def _mix(os_, ls_, perms_t, expand, ya, gates, x2, g1, sc2, sh2, wa, wb, wo, ln1g, ln1b, wr_parts, br, tri, seq):
    t = x2.shape[0]
    nb = t // MIX_TM
    per_b = seq // MIX_TM
    cur = lambda i: jnp.minimum(i, nb - 1)
    prv = lambda i: jnp.maximum(i - 1, 0)
    row = lambda w: pl.BlockSpec((MIX_TM, w), lambda i: (cur(i), 0))
    out_row = lambda w: pl.BlockSpec((MIX_TM, w), lambda i: (prv(i), 0))
    const = lambda s: pl.BlockSpec(s, lambda i: tuple(0 for _ in s))
    mod_cur = pl.BlockSpec((1, 1, D_MODEL), lambda i: (cur(i) // per_b, 0, 0))
    mod_prv = pl.BlockSpec((1, 1, D_MODEL), lambda i: (prv(i) // per_b, 0, 0))
    grp = lambda w: [pl.BlockSpec((1, dil, MIX_TM // dil, w), lambda i: (cur(i) // per_b, 0, cur(i) % per_b, 0))
                     for _win, dil in DIL_PAIRS]
    return pl.pallas_call(
        _mix_kernel,
        grid=(nb + 1,),
        in_specs=grp(ATT_WIDTH) + grp(LANES) + [
                  const((MIX_TM, MIX_TM)), const((MIX_TM, MIX_TM)), const((2 * LANES, ATT_WIDTH)),
                  row(GM_WIDTH), row(GATE_COLS), row(D_MODEL),
                  mod_cur, mod_prv, mod_prv,
                  const((GM_WIDTH, D_MODEL)), const((ATT_WIDTH, D_MODEL)), const((D_MODEL, D_MODEL)),
                  const((1, D_MODEL)), const((1, D_MODEL)),
                  const((D_MODEL, 2 * LANES)), const((1, LANES)), const((MIX_TM, MIX_TM))],
        out_specs=[out_row(D_MODEL), pl.BlockSpec((MIX_TM * ROW_SUB, LANES), lambda i: (prv(i), 0)),
                   out_row(LANES), out_row(LANES), const((8, LANES))],
        out_shape=[jax.ShapeDtypeStruct((t, D_MODEL), f32),
                   jax.ShapeDtypeStruct((t * ROW_SUB, LANES), i32),
                   jax.ShapeDtypeStruct((t, LANES), i32),
                   jax.ShapeDtypeStruct((t, LANES), f32),
                   jax.ShapeDtypeStruct((8, LANES), f32)],
        scratch_shapes=[pltpu.VMEM((1, LANES), f32), pltpu.VMEM((MIX_TM, D_MODEL), f32)],
        compiler_params=_params(("arbitrary",)),
        name="mix",
    )(*os_, *ls_, perms_t[1], perms_t[2], expand, ya, gates, x2, g1, sc2, sh2, wa, wb, wo,
      ln1g, ln1b, wr_parts, br, tri)


DISP_TM = 256
MOE_TM = 256


def _dispatch_kernel(pends_ref, pcnt_ref, nused_ref, dest_ref, h2p_ref, xs_hbm, zbuf, sem, zsem):
    i = pl.program_id(0)
    ntile = xs_hbm.shape[0] // (MOE_TM * ROW_SUB)

    def zero_tile(first_row):
        return pltpu.make_async_copy(
            zbuf, xs_hbm.at[pl.ds(pl.multiple_of(first_row * ROW_SUB, MOE_TM * ROW_SUB), MOE_TM * ROW_SUB)], zsem)

    def for_each_zero_tile(fn):
        for e in range(N_EXPERTS):
            pl.when(pcnt_ref[e] > 0)(functools.partial(fn, lambda e=e: zero_tile(pends_ref[e] - MOE_TM)))
        for k in range(N_EXPERTS):
            tile = nused_ref[0] + k
            pl.when(tile < ntile)(functools.partial(fn, lambda tile=tile: zero_tile(tile * MOE_TM)))

    @pl.when(i == 0)
    def _():
        zbuf[...] = jnp.zeros_like(zbuf)
        for_each_zero_tile(lambda mk: mk().start())
        for_each_zero_tile(lambda mk: mk().wait())

    def row_copy(k, r):
        d = dest_ref[0, k, r]
        return pltpu.make_async_copy(h2p_ref.at[pl.ds(r * ROW_SUB, ROW_SUB)],
                                     xs_hbm.at[pl.ds(pl.multiple_of(d, ROW_SUB), ROW_SUB)], sem)

    for r in range(DISP_TM):
        for k in range(TOP_K):
            row_copy(k, r).start(priority=k % 2)
    for k in range(TOP_K):
        pltpu.make_async_copy(h2p_ref, xs_hbm.at[pl.ds(0, DISP_TM * ROW_SUB)], sem).wait()


def _dispatch(pends, pcounts, n_used, dest3, h2p, ntile):
    t = h2p.shape[0] // ROW_SUB
    grid_spec = pltpu.PrefetchScalarGridSpec(
        num_scalar_prefetch=3,
        grid=(t // DISP_TM,),
        in_specs=[pl.BlockSpec((1, TOP_K, DISP_TM), lambda i, *_: (i, 0, 0), memory_space=pltpu.SMEM),
                  pl.BlockSpec((DISP_TM * ROW_SUB, LANES), lambda i, *_: (i, 0))],
        out_specs=pl.BlockSpec(memory_space=pl.ANY),
        scratch_shapes=[pltpu.VMEM((MOE_TM * ROW_SUB, LANES), i32),
                        pltpu.SemaphoreType.DMA(()),
                        pltpu.SemaphoreType.DMA(())],
    )
    return pl.pallas_call(
        _dispatch_kernel,
        grid_spec=grid_spec,
        out_shape=jax.ShapeDtypeStruct((ntile * MOE_TM * ROW_SUB, LANES), i32),
        compiler_params=_params(("arbitrary",)),
        name="dispatch",
    )(pends, pcounts, n_used, dest3, h2p)


def _moe_kernel(te_ref, first_ref, nexte_ref, wslot_ref, nused_ref,
                xs_ref, wg_hbm, wu_hbm, wd_hbm, bg_ref, bu_ref, bd_ref,
                out_ref, wbuf, wgb, wub, wdb, sem_w):
    j = pl.program_id(0)

    def weight_copies(e, ws):
        return [pltpu.make_async_copy(w.at[e], wbuf.at[ws, k], sem_w.at[ws])
                for k, w in enumerate((wg_hbm, wu_hbm, wd_hbm))]

    @pl.when(j == 0)
    def _():
        for cp in weight_copies(te_ref[0], wslot_ref[0]):
            cp.start()

    @pl.when(first_ref[j] == 1)
    def _():
        ws = wslot_ref[j]
        for cp in weight_copies(te_ref[j], ws):
            cp.wait()
        wgb[...] = wbuf[ws, 0].astype(bf16)
        wub[...] = wbuf[ws, 1].astype(bf16)
        wdb[...] = wbuf[ws, 2].astype(bf16)
        ne = nexte_ref[j]

        @pl.when(ne >= 0)
        def _():
            for cp in weight_copies(ne, 1 - ws):
                cp.start()

    used = j < nused_ref[0]

    @pl.when(used)
    def _():
        xb = _unpack_rows(_load_packed(xs_ref, 0, MOE_TM)).astype(bf16)
        g = jnp.dot(xb, wgb[...], preferred_element_type=f32) + bg_ref[0]
        u = jnp.dot(xb, wub[...], preferred_element_type=f32) + bu_ref[0]
        g = jnp.minimum(g, SWIGLU_LIMIT)
        u = jnp.clip(u, -SWIGLU_LIMIT, SWIGLU_LIMIT)
        act = (u + 1.0) * (g * jax.nn.sigmoid(SWIGLU_ALPHA * g))
        y = jnp.dot(act.astype(bf16), wdb[...], preferred_element_type=f32) + bd_ref[0]
        _store_packed(out_ref, _pack_rows(y), MOE_TM)

    @pl.when(jnp.logical_not(used))
    def _():
        out_ref[...] = jnp.zeros_like(out_ref)


def _moe(tile_e, tile_first, next_e, wslot, n_used, xs, w_gate, b_gate, w_up, b_up, w_down, b_down):
    ntile = tile_e.shape[0]
    bspec = pl.BlockSpec((1, 1, D_MODEL), lambda j, te, *_: (te[j], 0, 0))
    hbm = pl.BlockSpec(memory_space=pl.ANY)
    grid_spec = pltpu.PrefetchScalarGridSpec(
        num_scalar_prefetch=5,
        grid=(ntile,),
        in_specs=[pl.BlockSpec((MOE_TM * ROW_SUB, LANES),
                               lambda j, te, fi, ne, ws, nu: (jnp.minimum(j, nu[0] - 1), 0)),
                  hbm, hbm, hbm, bspec, bspec, bspec],
        out_specs=pl.BlockSpec((MOE_TM * ROW_SUB, LANES), lambda j, *_: (j, 0)),
        scratch_shapes=[pltpu.VMEM((2, 3, D_MODEL, D_MODEL), f32),
                        pltpu.VMEM((D_MODEL, D_MODEL), bf16),
                        pltpu.VMEM((D_MODEL, D_MODEL), bf16),
                        pltpu.VMEM((D_MODEL, D_MODEL), bf16),
                        pltpu.SemaphoreType.DMA((2,))],
    )
    return pl.pallas_call(
        _moe_kernel,
        grid_spec=grid_spec,
        out_shape=jax.ShapeDtypeStruct((ntile * MOE_TM * ROW_SUB, LANES), i32),
        compiler_params=_params(("arbitrary",)),
        name="moe",
    )(tile_e, tile_first, next_e, wslot, n_used, xs, w_gate, w_up, w_down, b_gate, b_up, b_down)


CB_TM = 256


def _combine_kernel(dcur_ref, dnxt_ref, yb_hbm, rw_ref, x1_ref, g2_ref, lng_ref, lnb_ref, out_ref,
                    ybuf0, ybuf1, sem):
    i = pl.program_id(0)
    last = pl.num_programs(0) - 1
    ybufs = (ybuf0, ybuf1)

    def row_copy(d, k, r, s):
        return pltpu.make_async_copy(
            yb_hbm.at[pl.ds(pl.multiple_of(d, ROW_SUB), ROW_SUB)],
            ybufs[s].at[pl.ds(pl.multiple_of((k * CB_TM + r) * ROW_SUB, ROW_SUB), ROW_SUB)],
            sem.at[s])

    @pl.when(i == 0)
    def _():
        for k in range(TOP_K):
            def body(r, c, k=k):
                row_copy(dcur_ref[0, k, r], k, r, 0).start()
                return c
            lax.fori_loop(0, CB_TM, body, 0, unroll=8)

    for s in range(2):
        @pl.when(i % 2 == s)
        def _(s=s):
            pltpu.make_async_copy(yb_hbm.at[pl.ds(0, TOP_K * CB_TM * ROW_SUB)], ybufs[s], sem.at[s]).wait()

            @pl.when(i < last)
            def _():
                for k in range(TOP_K):
                    for r in range(CB_TM):
                        row_copy(dnxt_ref[0, k, r], k, r, 1 - s).start(priority=r % 2)

            parts = [_unpack_rows(_load_packed(ybufs[s], k * CB_TM, CB_TM)) * rw_ref[:, k:k + 1]
                     for k in range(TOP_K)]
            y = (parts[0] + parts[1]) + (parts[2] + parts[3])
            out_ref[...] = _ln(DN_ALPHA * x1_ref[...] + g2_ref[0] * y) * lng_ref[...] + lnb_ref[...]


def _combine(dest3, yb, rw, x1, g2, ln2g, ln2b, seq):
    t = x1.shape[0]
    nb = t // CB_TM
    per_b = seq // CB_TM
    return pl.pallas_call(
        _combine_kernel,
        grid=(nb,),
        in_specs=[pl.BlockSpec((1, TOP_K, CB_TM), lambda i: (i, 0, 0), memory_space=pltpu.SMEM),
                  pl.BlockSpec((1, TOP_K, CB_TM), lambda i: (jnp.minimum(i + 1, nb - 1), 0, 0),
                               memory_space=pltpu.SMEM),
                  pl.BlockSpec(memory_space=pl.ANY),
                  pl.BlockSpec((CB_TM, LANES), lambda i: (i, 0)),
                  pl.BlockSpec((CB_TM, D_MODEL), lambda i: (i, 0)),
                  pl.BlockSpec((1, 1, D_MODEL), lambda i: (i // per_b, 0, 0)),
                  pl.BlockSpec((1, D_MODEL), lambda i: (0, 0)),
                  pl.BlockSpec((1, D_MODEL), lambda i: (0, 0))],
        out_specs=pl.BlockSpec((CB_TM, D_MODEL), lambda i: (i, 0)),
        out_shape=jax.ShapeDtypeStruct((t, D_MODEL), f32),
        scratch_shapes=[pltpu.VMEM((TOP_K * CB_TM * ROW_SUB, LANES), i32),
                        pltpu.VMEM((TOP_K * CB_TM * ROW_SUB, LANES), i32),
                        pltpu.SemaphoreType.DMA((2,))],
        compiler_params=_params(("arbitrary",)),
        name="combine",
    )(dest3, dest3, yb, rw, x1, g2, ln2g, ln2b)


def _t5_bucket(dist):
    d = dist.astype(f32)
    large = REL_MAX_EXACT + jnp.log(jnp.maximum(d, float(REL_MAX_EXACT)) / REL_MAX_EXACT) / math.log(
        REL_MAX_DIST / REL_MAX_EXACT) * (REL_BUCKETS - REL_MAX_EXACT)
    large = jnp.minimum(large.astype(i32), REL_BUCKETS - 1)
    return jnp.where(dist < REL_MAX_EXACT, dist, large)


def _bias_indices():
    qi = jnp.arange(ATT_BLOCK)[:, None]
    ki = jnp.arange(2 * ATT_BLOCK)[None, :]
    didx = qi + ATT_BLOCK - ki
    buckets, bands = [], []
    for win, dil in DIL_PAIRS:
        buckets.append(_t5_bucket(jnp.clip(didx, 0, None) * dil))
        bands.append(((didx >= 0) & (didx <= win // dil)).astype(i32))
    return jnp.stack(buckets).astype(i32), jnp.stack(bands)


def _residue_perm(tm, dil):
    n = tm // dil
    dst = jnp.arange(tm)
    src = (dst % n) * dil + dst // n
    return (src[:, None] == jnp.arange(tm)[None, :]).astype(bf16)


def kernel(x, c, w_ada, b_ada, w_in, gm_ln_g, gm_ln_b, gm_w_s, gm_b_s, w_branch_a, w_branch_b, w_out,
           rel_bias, ln1_g, ln1_b, w_router, b_router, w_gate, b_gate, w_up, b_up, w_down, b_down,
           ln2_g, ln2_b):
    batch, seq, _ = x.shape
    t = batch * seq
    l = 0
    x2 = x.reshape(t, D_MODEL)

    c8 = jnp.pad(c, ((0, 8 - batch), (0, 0)))
    mod = _adaln(c8, w_ada[l], b_ada[l][None, :])[:batch]
    sh1, sc1, g1, sh2, sc2, g2 = [m[:, None, :] for m in jnp.split(mod, 6, axis=-1)]

    perms = [_residue_perm(IN_TM, dil) for _win, dil in DIL_PAIRS]
    uv, gates, *qkvs = _inproj(x2, sc1, sh1, w_in[l].astype(bf16), perms, batch, seq)

    bs_full = jnp.repeat(gm_b_s[l].T, GM_WIDTH // GM_GROUPS, axis=1)
    ya = _gmlp(uv, gm_ln_g[l][None, :], gm_ln_b[l][None, :], gm_w_s[l], bs_full)

    bucket, band = _bias_indices()
    bias = _relbias(rel_bias, bucket, band)
    os_, ls_ = [], []
    for g, (_win, dil) in enumerate(DIL_PAIRS):
        o, lse = _attn_group(qkvs[g], bias, g, dil, batch, seq)
        os_.append(o)
        ls_.append(lse)

    wr = jnp.pad(w_router[l], ((0, 0), (0, LANES - N_EXPERTS)))
    wr_hi = wr.astype(bf16)
    wr_parts = jnp.concatenate([wr_hi, (wr - wr_hi.astype(f32)).astype(bf16)], axis=1)
    br = jnp.pad(b_router[l], (0, LANES - N_EXPERTS))[None, :]
    tri = (jnp.arange(MIX_TM)[None, :] < jnp.arange(MIX_TM)[:, None]).astype(bf16)
    perms_t = [_residue_perm(MIX_TM, dil).T for _win, dil in DIL_PAIRS]
    expand = (jnp.arange(LANES)[:, None] == jnp.arange(ATT_WIDTH)[None, :] // HEAD_DIM).astype(bf16)
    expand = jnp.concatenate([expand, expand], axis=0)
    x1, h2, route, rw, cnt = _mix(
        os_, ls_, perms_t, expand, ya, gates, x2, g1, sc2, sh2,
        w_branch_a[l].astype(bf16), w_branch_b[l].astype(bf16), w_out[l].astype(bf16),
        ln1_g[l][None, :], ln1_b[l][None, :], wr_parts, br, tri, seq)

    top_e = route[:, :TOP_K]
    rank = route[:, TOP_K:2 * TOP_K]
    counts = cnt[0, :N_EXPERTS].astype(i32)
    pcounts = (counts + MOE_TM - 1) // MOE_TM * MOE_TM
    pends = jnp.cumsum(pcounts)
    pstarts = pends - pcounts
    experts = jnp.arange(N_EXPERTS, dtype=i32)
    dest = jnp.sum(jnp.where(top_e[:, :, None] == experts, pstarts, 0), axis=-1) + rank
    ntile = t * TOP_K // MOE_TM + N_EXPERTS
    n_used = (pends[-1] // MOE_TM).reshape(1)
    tile_idx = jnp.minimum(jnp.arange(ntile, dtype=i32), n_used - 1)
    tile_e = jnp.sum((pends[None, :] <= (tile_idx * MOE_TM)[:, None]).astype(i32), axis=1)
    tile_first = jnp.concatenate([jnp.ones((1,), i32), (tile_e[1:] != tile_e[:-1]).astype(i32)])
    nonempty = counts > 0
    later = lax.cummin(jnp.where(nonempty, experts, N_EXPERTS), reverse=True)
    next_nonempty = jnp.concatenate([later[1:], jnp.full((1,), N_EXPERTS, i32)])
    next_nonempty = jnp.where(next_nonempty >= N_EXPERTS, -1, next_nonempty)
    expert_slot = (jnp.cumsum(nonempty.astype(i32)) - 1) % 2

    dest3 = (dest * ROW_SUB).reshape(t // DISP_TM, DISP_TM, TOP_K).transpose(0, 2, 1)
    xs = _dispatch(pends, pcounts, n_used, dest3, h2, ntile)
    yb = _moe(tile_e, tile_first, next_nonempty[tile_e], expert_slot[tile_e], n_used, xs,
              w_gate[l], b_gate[l][:, None, :], w_up[l], b_up[l][:, None, :],
              w_down[l], b_down[l][:, None, :])
    out = _combine(dest3, yb, rw, x1, g2, ln2_g[l][None, :], ln2_b[l][None, :], seq)
    return out.reshape(batch, seq, D_MODEL)
```

```python
import functools
import math

import jax
import jax.numpy as jnp
from jax import lax
from jax.experimental import pallas as pl
from jax.experimental.pallas import tpu as pltpu

f32 = jnp.float32
bf16 = jnp.bfloat16
i32 = jnp.int32

D_MODEL = 1024
GM_WIDTH = 512
GM_GROUPS = 8
GM_CHUNK = 128
DIL_PAIRS = ((128, 1), (512, 4), (2048, 16))
N_DIL = 3
HEADS_PER_GROUP = 8
HEAD_DIM = 64
ATT_WIDTH = 512
ATT_BLOCK = 128
NEG_INF = -1e30
REL_BUCKETS = 32
REL_MAX_EXACT = 16
REL_MAX_DIST = 2048
N_EXPERTS = 32
TOP_K = 4
SWIGLU_LIMIT = 7.0
SWIGLU_ALPHA = 1.702
MOE_BLOCK = 128
DEPTH = 1
DN_ALPHA = (2 * DEPTH) ** 0.25
LN_EPS = 1e-5
UV_COLS = 2 * GM_WIDTH
QKV_COLS = N_DIL * 3 * ATT_WIDTH
GATE_COLS = 2 * D_MODEL
IN_COLS = UV_COLS + QKV_COLS + GATE_COLS

LANES = 128
SUBLANES = 8
VMEM_LIMIT = 56 * 1024 * 1024


def _ln(x):
    mu = jnp.mean(x, axis=-1, keepdims=True)
    xc = x - mu
    var = jnp.mean(xc * xc, axis=-1, keepdims=True)
    return xc * lax.rsqrt(var + LN_EPS)


def _params(sem, vmem=VMEM_LIMIT):
    return pltpu.CompilerParams(dimension_semantics=sem, vmem_limit_bytes=vmem)


def _adaln_kernel(c_ref, w_ref, b_ref, o_ref):
    c = c_ref[...]
    s = c * jax.nn.sigmoid(c)
    o_ref[...] = jnp.dot(s, w_ref[...], preferred_element_type=f32,
                         precision=lax.Precision.HIGHEST) + b_ref[...]


def _adaln(c8, w_ada, b_ada):
    n = w_ada.shape[1] // D_MODEL
    return pl.pallas_call(
        _adaln_kernel,
        grid=(n,),
        in_specs=[pl.BlockSpec((8, D_MODEL), lambda j: (0, 0)),
                  pl.BlockSpec((D_MODEL, D_MODEL), lambda j: (0, j)),
                  pl.BlockSpec((1, D_MODEL), lambda j: (0, j))],
        out_specs=pl.BlockSpec((8, D_MODEL), lambda j: (0, j)),
        out_shape=jax.ShapeDtypeStruct((8, w_ada.shape[1]), f32),
        compiler_params=_params(("arbitrary",)),
        name="adaln",
    )(c8, w_ada, b_ada)


IN_TM = 256
IN_CW = 512
GRP_COLS = 3 * ATT_WIDTH


def _inproj_kernel(x_ref, sc_ref, sh_ref, w_ref, p1_ref, p2_ref,
                   uv_ref, gt_ref, qkv0_ref, qkv1_ref, qkv2_ref):
    xn = _ln(x_ref[...])
    h = (xn * (1.0 + sc_ref[0]) + sh_ref[0]).astype(bf16)
    hp = [h,
          jnp.dot(p1_ref[...], h, preferred_element_type=f32).astype(bf16),
          jnp.dot(p2_ref[...], h, preferred_element_type=f32).astype(bf16)]
    for c0 in range(0, UV_COLS, IN_CW):
        acc = jnp.dot(h, w_ref[:, c0:c0 + IN_CW], preferred_element_type=f32)
        uv_ref[:, c0:c0 + IN_CW] = jax.nn.gelu(acc).astype(bf16)
    for g, (qref, (_win, dil)) in enumerate(zip((qkv0_ref, qkv1_ref, qkv2_ref), DIL_PAIRS)):
        n = IN_TM // dil
        for q0 in range(0, GRP_COLS, IN_CW):
            c0 = UV_COLS + g * GRP_COLS + q0
            acc = jnp.dot(hp[g], w_ref[:, c0:c0 + IN_CW], preferred_element_type=f32).astype(bf16)
            for rho in range(dil):
                qref[0, rho, :, q0:q0 + IN_CW] = acc[rho * n:(rho + 1) * n, :]
    for g0 in range(0, GATE_COLS, IN_CW):
        c0 = UV_COLS + QKV_COLS + g0
        acc = jnp.dot(h, w_ref[:, c0:c0 + IN_CW], preferred_element_type=f32)
        gt_ref[:, g0:g0 + IN_CW] = jax.nn.sigmoid(acc).astype(bf16)


def _inproj(x2, sc1, sh1, w_in_bf, perms, batch, seq):
    t = x2.shape[0]
    per_b = seq // IN_TM
    qkv_specs, qkv_shapes = [], []
    for _win, dil in DIL_PAIRS:
        n = IN_TM // dil
        qkv_specs.append(pl.BlockSpec((1, dil, n, GRP_COLS), lambda i: (i // per_b, 0, i % per_b, 0)))
        qkv_shapes.append(jax.ShapeDtypeStruct((batch, dil, seq // dil, GRP_COLS), bf16))
    return pl.pallas_call(
        _inproj_kernel,
        grid=(t // IN_TM,),
        in_specs=[pl.BlockSpec((IN_TM, D_MODEL), lambda i: (i, 0)),
                  pl.BlockSpec((1, 1, D_MODEL), lambda i: (i // per_b, 0, 0)),
                  pl.BlockSpec((1, 1, D_MODEL), lambda i: (i // per_b, 0, 0)),
                  pl.BlockSpec((D_MODEL, IN_COLS), lambda i: (0, 0)),
                  pl.BlockSpec((IN_TM, IN_TM), lambda i: (0, 0)),
                  pl.BlockSpec((IN_TM, IN_TM), lambda i: (0, 0))],
        out_specs=[pl.BlockSpec((IN_TM, UV_COLS), lambda i: (i, 0)),
                   pl.BlockSpec((IN_TM, GATE_COLS), lambda i: (i, 0))] + qkv_specs,
        out_shape=[jax.ShapeDtypeStruct((t, UV_COLS), bf16),
                   jax.ShapeDtypeStruct((t, GATE_COLS), bf16)] + qkv_shapes,
        compiler_params=_params(("arbitrary",)),
        name="inproj",
    )(x2, sc1, sh1, w_in_bf, perms[1], perms[2])


GM_TM = 512


def _gmlp_kernel(u_ref, v_ref, g_ref, b_ref, ws_ref, bs_ref, ya_ref):
    row = lax.broadcasted_iota(i32, (GM_CHUNK, GM_CHUNK), 0)
    col = lax.broadcasted_iota(i32, (GM_CHUNK, GM_CHUNK), 1)
    causal = col <= row
    first_half = lax.broadcasted_iota(i32, (GM_CHUNK, LANES), 1) < (GM_WIDTH // GM_GROUPS)
    ws = [jnp.where(causal, ws_ref[g], 0.0).astype(bf16) for g in range(GM_GROUPS)]
    for ch in range(GM_TM // GM_CHUNK):
        r0 = ch * GM_CHUNK
        vn = _ln(v_ref[r0:r0 + GM_CHUNK, :].astype(f32)) * g_ref[...] + b_ref[...]
        vn = vn.astype(bf16)
        for j in range(GM_WIDTH // LANES):
            slab = vn[:, j * LANES:(j + 1) * LANES]
            s_lo = jnp.dot(ws[2 * j], slab, preferred_element_type=f32)
            s_hi = jnp.dot(ws[2 * j + 1], slab, preferred_element_type=f32)
            s = jnp.where(first_half, s_lo, s_hi) + bs_ref[:, j * LANES:(j + 1) * LANES]
            u = u_ref[r0:r0 + GM_CHUNK, j * LANES:(j + 1) * LANES].astype(f32)
            ya_ref[r0:r0 + GM_CHUNK, j * LANES:(j + 1) * LANES] = (u * s).astype(bf16)


def _gmlp(uv, ln_g, ln_b, w_s, bs_full):
    t = uv.shape[0]
    return pl.pallas_call(
        _gmlp_kernel,
        grid=(t // GM_TM,),
        in_specs=[pl.BlockSpec((GM_TM, GM_WIDTH), lambda i: (i, 0)),
                  pl.BlockSpec((GM_TM, GM_WIDTH), lambda i: (i, 1)),
                  pl.BlockSpec((1, GM_WIDTH), lambda i: (0, 0)),
                  pl.BlockSpec((1, GM_WIDTH), lambda i: (0, 0)),
                  pl.BlockSpec((GM_GROUPS, GM_CHUNK, GM_CHUNK), lambda i: (0, 0, 0)),
                  pl.BlockSpec((GM_CHUNK, GM_WIDTH), lambda i: (0, 0))],
        out_specs=pl.BlockSpec((GM_TM, GM_WIDTH), lambda i: (i, 0)),
        out_shape=jax.ShapeDtypeStruct((t, GM_WIDTH), bf16),
        compiler_params=_params(("arbitrary",)),
        name="gmlp",
    )(uv, uv, ln_g, ln_b, w_s, bs_full)


def _relbias_kernel(tab_ref, bucket_ref, band_ref, out_ref):
    g = pl.program_id(0)
    bk = bucket_ref[0]
    band = band_ref[0] > 0
    for h in range(HEADS_PER_GROUP):
        acc = jnp.zeros((ATT_BLOCK, 2 * ATT_BLOCK), f32)
        for b in range(REL_BUCKETS):
            acc = jnp.where(bk == b, tab_ref[b, g * HEADS_PER_GROUP + h], acc)
        out_ref[0, h] = jnp.where(band, acc, NEG_INF)


def _relbias(rel_bias, bucket, band):
    return pl.pallas_call(
        _relbias_kernel,
        grid=(N_DIL,),
        in_specs=[pl.BlockSpec(memory_space=pltpu.SMEM),
                  pl.BlockSpec((1, ATT_BLOCK, 2 * ATT_BLOCK), lambda g: (g, 0, 0)),
                  pl.BlockSpec((1, ATT_BLOCK, 2 * ATT_BLOCK), lambda g: (g, 0, 0))],
        out_specs=pl.BlockSpec((1, HEADS_PER_GROUP, ATT_BLOCK, 2 * ATT_BLOCK),
                               lambda g: (g, 0, 0, 0)),
        out_shape=jax.ShapeDtypeStruct((N_DIL, HEADS_PER_GROUP, ATT_BLOCK, 2 * ATT_BLOCK), f32),
        compiler_params=_params(("arbitrary",)),
        name="relbias",
    )(rel_bias, bucket, band)


ATT_MAX_STEP_BLOCKS = 4


def _attn_kernel(nblk, q_ref, kp_ref, kc_ref, vp_ref, vc_ref, bias_ref, o_ref, lse_ref):
    first = pl.program_id(2) == 0
    lane = lax.broadcasted_iota(i32, (ATT_BLOCK, LANES), 1)
    lo_half = lane < HEAD_DIM
    nt = (((1,), (1,)), ((), ()))
    ones = jnp.ones((2 * ATT_BLOCK, LANES), bf16)
    n_slab = ATT_WIDTH // LANES
    logits, v_ext = [], []
    for i in range(nblk):
        cur = slice(i * ATT_BLOCK, (i + 1) * ATT_BLOCK)
        prv = slice((i - 1) * ATT_BLOCK, i * ATT_BLOCK)
        for j in range(n_slab):
            sl = slice(j * LANES, (j + 1) * LANES)
            q = q_ref[0, 0, cur, sl] * (HEAD_DIM ** -0.5)
            k_prev = kp_ref[0, 0, :, sl] if i == 0 else kc_ref[0, 0, prv, sl]
            v_prev = vp_ref[0, 0, :, sl] if i == 0 else vc_ref[0, 0, prv, sl]
            k_cat = jnp.concatenate([k_prev, kc_ref[0, 0, cur, sl]], axis=0)
            v_cat = jnp.concatenate([v_prev, vc_ref[0, 0, cur, sl]], axis=0)
            v_ext.append(jnp.concatenate([v_cat, ones], axis=1))
            for hh in range(2):
                qm = jnp.where(lo_half if hh == 0 else jnp.logical_not(lo_half), q, 0.0).astype(bf16)
                logits.append(lax.dot_general(qm, k_cat, nt, preferred_element_type=f32))
    bias = bias_ref[0].reshape(HEADS_PER_GROUP * ATT_BLOCK, 2 * ATT_BLOCK)
    rows_per_block = HEADS_PER_GROUP * ATT_BLOCK
    lg = jnp.concatenate(logits, axis=0) + jnp.concatenate([bias] * nblk, axis=0)
    row = lax.broadcasted_iota(i32, lg.shape, 0)
    col = lax.broadcasted_iota(i32, lg.shape, 1)
    no_prev = jnp.logical_and(first, jnp.logical_and(row < rows_per_block, col < ATT_BLOCK))
    lg = jnp.where(no_prev, NEG_INF, lg)
    m = jnp.max(lg, axis=-1, keepdims=True)
    p = jnp.exp(lg - m).astype(bf16)
    for i in range(nblk):
        cur = slice(i * ATT_BLOCK, (i + 1) * ATT_BLOCK)
        lse_tile = jnp.zeros((ATT_BLOCK, LANES), f32)
        for j in range(n_slab):
            outs = []
            for hh in range(2):
                h = 2 * j + hh
                r0 = i * rows_per_block + h * ATT_BLOCK
                r = jnp.dot(p[r0:r0 + ATT_BLOCK], v_ext[i * n_slab + j], preferred_element_type=f32)
                den = r[:, LANES:]
                outs.append(r[:, :LANES] * (1.0 / den))
                lse_h = m[r0:r0 + ATT_BLOCK] + jnp.log(den)
                lse_tile = jnp.where(lane == h, lse_h, lse_tile)
            o_ref[0, 0, cur, j * LANES:(j + 1) * LANES] = jnp.where(lo_half, outs[0], outs[1]).astype(bf16)
        lse_ref[0, 0, cur, :] = lse_tile


def _attn_group(qkv_g, bias, g, dil, batch, seq):
    l = seq // dil
    nblk = min(ATT_MAX_STEP_BLOCKS, l // ATT_BLOCK)
    tm = nblk * ATT_BLOCK
    nsteps = l // tm

    def cur(cb):
        return pl.BlockSpec((1, 1, tm, ATT_WIDTH), lambda b, r, n: (b, r, n, cb))

    def prev(cb):
        return pl.BlockSpec((1, 1, ATT_BLOCK, ATT_WIDTH),
                            lambda b, r, n: (b, r, jnp.maximum(n * nblk - 1, 0), cb))

    return pl.pallas_call(
        functools.partial(_attn_kernel, nblk),
        grid=(batch, dil, nsteps),
        in_specs=[cur(0), prev(1), cur(1), prev(2), cur(2),
                  pl.BlockSpec((1, HEADS_PER_GROUP, ATT_BLOCK, 2 * ATT_BLOCK),
                               lambda b, r, n: (g, 0, 0, 0))],
        out_specs=[pl.BlockSpec((1, 1, tm, ATT_WIDTH), lambda b, r, n: (b, r, n, 0)),
                   pl.BlockSpec((1, 1, tm, LANES), lambda b, r, n: (b, r, n, 0))],
        out_shape=[jax.ShapeDtypeStruct((batch, dil, l, ATT_WIDTH), bf16),
                   jax.ShapeDtypeStruct((batch, dil, l, LANES), f32)],
        compiler_params=_params(("arbitrary", "arbitrary", "arbitrary")),
        name=f"attn_g{g}",
    )(qkv_g, qkv_g, qkv_g, qkv_g, qkv_g, bias)


ROW_WORDS = D_MODEL // 2
ROW_SUB = ROW_WORDS // LANES
HI_MASK = -65536


def _pack_rows(x):
    bits = lax.bitcast_convert_type(x.astype(bf16).astype(f32), i32)
    return lax.shift_right_logical(bits[:, :ROW_WORDS], 16) | (bits[:, ROW_WORDS:] & HI_MASK)


def _unpack_rows(words):
    lo = lax.bitcast_convert_type(lax.shift_left(words, 16), f32)
    hi = lax.bitcast_convert_type(words & HI_MASK, f32)
    return jnp.concatenate([lo, hi], axis=1)


def _store_packed(ref, words, n, first_row=0):
    for r in range(ROW_SUB):
        ref[pl.ds(first_row * ROW_SUB + r, n, stride=ROW_SUB), :] = words[:, r * LANES:(r + 1) * LANES]


def _load_packed(ref, first_row, n):
    return jnp.concatenate([ref[pl.ds(first_row * ROW_SUB + r, n, stride=ROW_SUB), :] for r in range(ROW_SUB)],
                           axis=1)


MIX_TM = 256
MIX_SUB = 128


def _split_bf16(x, parts):
    out = []
    for _ in range(parts):
        hi = x.astype(bf16)
        out.append(hi)
        x = x - hi.astype(f32)
    return out


def _mix_kernel(o0_ref, o1_ref, o2_ref, l0_ref, l1_ref, l2_ref, pt1_ref, pt2_ref, ex_ref,
                ya_ref, gt_ref, x_ref,
                g1_ref, sc2_ref, sh2_ref, wa_ref, wb_ref, wo_ref, ln1g_ref, ln1b_ref,
                wrc_ref, br_ref, tri_ref,
                x1_ref, h2_ref, route_ref, rw_ref, cnt_ref, run_ref, xr_ref):
    step = pl.program_id(0)

    @pl.when(step == 0)
    def _():
        run_ref[...] = jnp.zeros_like(run_ref)
        xr_ref[...] = jnp.zeros_like(xr_ref)

    def back_rows(r0):
        rows = slice(r0, r0 + MIX_SUB)
        x1 = _ln(xr_ref[rows, :]) * ln1g_ref[...] + ln1b_ref[...]
        x1_ref[rows, :] = x1
        h2 = _ln(x1) * (1.0 + sc2_ref[0]) + sh2_ref[0]
        _store_packed(h2_ref, _pack_rows(h2), MIX_SUB, r0)
        h_hi, h_lo = _split_bf16(h2, 2)
        hi_both = jnp.dot(h_hi, wrc_ref[...], preferred_element_type=f32)
        return (hi_both[:, :LANES]
                + (hi_both[:, LANES:] + jnp.dot(h_lo, wrc_ref[:, :LANES], preferred_element_type=f32))
                ) + br_ref[...]

    def front_rows(r0):
        rows = slice(r0, r0 + MIX_SUB)
        os_, ls_ = [o0_ref[0, 0, rows, :].astype(f32)], [l0_ref[0, 0, rows, :]]
        for o_ref, l_ref, pt_ref in ((o1_ref, l1_ref, pt1_ref), (o2_ref, l2_ref, pt2_ref)):
            pt = pt_ref[rows, :]
            os_.append(jnp.dot(pt, o_ref[0].reshape(MIX_TM, ATT_WIDTH), preferred_element_type=f32))
            parts = [jnp.dot(pt, part, preferred_element_type=f32)
                     for part in _split_bf16(l_ref[0].reshape(MIX_TM, LANES), 3)]
            ls_.append((parts[0] + parts[1]) + parts[2])
        lm = jnp.maximum(jnp.maximum(ls_[0], ls_[1]), ls_[2])
        es = [jnp.exp(lse - lm) for lse in ls_]
        inv = 1.0 / (es[0] + es[1] + es[2])
        yb = jnp.zeros((MIX_SUB, ATT_WIDTH), f32)
        for e, o in zip(es, os_):
            w_parts = jnp.concatenate(_split_bf16(e * inv, 2), axis=1)
            yb = yb + jnp.dot(w_parts, ex_ref[...], preferred_element_type=f32) * o
        a = jnp.dot(ya_ref[rows, :], wa_ref[...], preferred_element_type=f32)
        b = jnp.dot(yb.astype(bf16), wb_ref[...], preferred_element_type=f32)
        merged = gt_ref[rows, :D_MODEL].astype(f32) * a + gt_ref[rows, D_MODEL:].astype(f32) * b
        mix = jnp.dot(merged.astype(bf16), wo_ref[...], preferred_element_type=f32)
        xr_ref[rows, :] = DN_ALPHA * x_ref[rows, :] + g1_ref[0] * mix

    logit_parts = []
    for r0 in range(0, MIX_TM, MIX_SUB):
        logit_parts.append(back_rows(r0))
        front_rows(r0)
    logits = jnp.concatenate(logit_parts, axis=0)
    lane = lax.broadcasted_iota(i32, (MIX_TM, LANES), 1)
    logits = jnp.where(lane < N_EXPERTS, logits, -jnp.inf)
    lane_f = lane.astype(f32)
    vals, idxs = [], []
    for _k in range(TOP_K):
        m = jnp.max(logits, axis=-1, keepdims=True)
        vals.append(m)
        idxs.append(jnp.min(jnp.where(logits == m, lane_f, float(LANES)), axis=-1, keepdims=True).astype(i32))
        logits = jnp.where(lane == idxs[-1], -jnp.inf, logits)
    exps = [jnp.exp(v - vals[0]) for v in vals]
    den = exps[0] + exps[1] + exps[2] + exps[3]
    wts = [e / den for e in exps]
    hits = [lane == idx for idx in idxs]
    counted = jnp.where(step > 0, 1.0, 0.0)
    onehot = jnp.zeros((MIX_TM, LANES), f32)
    for hit in hits:
        onehot = onehot + jnp.where(hit, counted, 0.0)
    prefix = jnp.dot(tri_ref[...], onehot.astype(bf16), preferred_element_type=f32) + run_ref[...]
    route = jnp.zeros((MIX_TM, LANES), i32)
    rw = jnp.zeros((MIX_TM, LANES), f32)
    for k in range(TOP_K):
        rank = jnp.sum(jnp.where(hits[k], prefix, 0.0), axis=-1, keepdims=True).astype(i32)
        route = jnp.where(lane == k, idxs[k], route)
        route = jnp.where(lane == TOP_K + k, rank, route)
        rw = jnp.where(lane == k, wts[k], rw)
    route_ref[...] = route
    rw_ref[...] = rw
    run = run_ref[...] + jnp.sum(onehot, axis=0, keepdims=True)
    run_ref[...] = run
    cnt_ref[...] = jnp.broadcast_to(run, cnt_ref.shape)


def _mix(os_, ls_, perms_t, expand, ya, gates, x2, g1, sc2, sh2, wa, wb, wo, ln1g, ln1b, wr_parts, br, tri, seq):
    t = x2.shape[0]
    nb = t // MIX_TM
    per_b = seq // MIX_TM
    cur = lambda i: jnp.minimum(i, nb - 1)
    prv = lambda i: jnp.maximum(i - 1, 0)
    row = lambda w: pl.BlockSpec((MIX_TM, w), lambda i: (cur(i), 0))
    out_row = lambda w: pl.BlockSpec((MIX_TM, w), lambda i: (prv(i), 0))
    const = lambda s: pl.BlockSpec(s, lambda i: tuple(0 for _ in s))
    mod_cur = pl.BlockSpec((1, 1, D_MODEL), lambda i: (cur(i) // per_b, 0, 0))
    mod_prv = pl.BlockSpec((1, 1, D_MODEL), lambda i: (prv(i) // per_b, 0, 0))
    grp = lambda w: [pl.BlockSpec((1, dil, MIX_TM // dil, w), lambda i: (cur(i) // per_b, 0, cur(i) % per_b, 0))
                     for _win, dil in DIL_PAIRS]
    return pl.pallas_call(
        _mix_kernel,
        grid=(nb + 1,),
        in_specs=grp(ATT_WIDTH) + grp(LANES) + [
                  const((MIX_TM, MIX_TM)), const((MIX_TM, MIX_TM)), const((2 * LANES, ATT_WIDTH)),
                  row(GM_WIDTH), row(GATE_COLS), row(D_MODEL),
                  mod_cur, mod_prv, mod_prv,
                  const((GM_WIDTH, D_MODEL)), const((ATT_WIDTH, D_MODEL)), const((D_MODEL, D_MODEL)),
                  const((1, D_MODEL)), const((1, D_MODEL)),
                  const((D_MODEL, 2 * LANES)), const((1, LANES)), const((MIX_TM, MIX_TM))],
        out_specs=[out_row(D_MODEL), pl.BlockSpec((MIX_TM * ROW_SUB, LANES), lambda i: (prv(i), 0)),
                   out_row(LANES), out_row(LANES), const((8, LANES))],
        out_shape=[jax.ShapeDtypeStruct((t, D_MODEL), f32),
                   jax.ShapeDtypeStruct((t * ROW_SUB, LANES), i32),
                   jax.ShapeDtypeStruct((t, LANES), i32),
                   jax.ShapeDtypeStruct((t, LANES), f32),
                   jax.ShapeDtypeStruct((8, LANES), f32)],
        scratch_shapes=[pltpu.VMEM((1, LANES), f32), pltpu.VMEM((MIX_TM, D_MODEL), f32)],
        compiler_params=_params(("arbitrary",)),
        name="mix",
    )(*os_, *ls_, perms_t[1], perms_t[2], expand, ya, gates, x2, g1, sc2, sh2, wa, wb, wo,
      ln1g, ln1b, wr_parts, br, tri)


DISP_TM = 256
MOE_TM = 512


def _dispatch_kernel(pends_ref, pcnt_ref, nused_ref, dest_ref, h2p_ref, xs_hbm, zbuf, sem, zsem):
    i = pl.program_id(0)
    ntile = xs_hbm.shape[0] // (MOE_TM * ROW_SUB)

    def zero_tile(first_row):
        return pltpu.make_async_copy(
            zbuf, xs_hbm.at[pl.ds(pl.multiple_of(first_row * ROW_SUB, MOE_TM * ROW_SUB), MOE_TM * ROW_SUB)], zsem)

    def for_each_zero_tile(fn):
        for e in range(N_EXPERTS):
            pl.when(pcnt_ref[e] > 0)(functools.partial(fn, lambda e=e: zero_tile(pends_ref[e] - MOE_TM)))
        for k in range(N_EXPERTS):
            tile = nused_ref[0] + k
            pl.when(tile < ntile)(functools.partial(fn, lambda tile=tile: zero_tile(tile * MOE_TM)))

    @pl.when(i == 0)
    def _():
        zbuf[...] = jnp.zeros_like(zbuf)
        for_each_zero_tile(lambda mk: mk().start())
        for_each_zero_tile(lambda mk: mk().wait())

    def row_copy(k, r):
        d = dest_ref[0, k, r]
        return pltpu.make_async_copy(h2p_ref.at[pl.ds(r * ROW_SUB, ROW_SUB)],
                                     xs_hbm.at[pl.ds(pl.multiple_of(d * ROW_SUB, ROW_SUB), ROW_SUB)], sem)

    for r in range(DISP_TM):
        for k in range(TOP_K):
            row_copy(k, r).start(priority=k % 2)
    for k in range(TOP_K):
        pltpu.make_async_copy(h2p_ref, xs_hbm.at[pl.ds(0, DISP_TM * ROW_SUB)], sem).wait()


def _dispatch(pends, pcounts, n_used, dest3, h2p, ntile):
    t = h2p.shape[0] // ROW_SUB
    grid_spec = pltpu.PrefetchScalarGridSpec(
        num_scalar_prefetch=3,
        grid=(t // DISP_TM,),
        in_specs=[pl.BlockSpec((1, TOP_K, DISP_TM), lambda i, *_: (i, 0, 0), memory_space=pltpu.SMEM),
                  pl.BlockSpec((DISP_TM * ROW_SUB, LANES), lambda i, *_: (i, 0))],
        out_specs=pl.BlockSpec(memory_space=pl.ANY),
        scratch_shapes=[pltpu.VMEM((MOE_TM * ROW_SUB, LANES), i32),
                        pltpu.SemaphoreType.DMA(()),
                        pltpu.SemaphoreType.DMA(())],
    )
    return pl.pallas_call(
        _dispatch_kernel,
        grid_spec=grid_spec,
        out_shape=jax.ShapeDtypeStruct((ntile * MOE_TM * ROW_SUB, LANES), i32),
        compiler_params=_params(("arbitrary",)),
        name="dispatch",
    )(pends, pcounts, n_used, dest3, h2p)


def _moe_kernel(te_ref, first_ref, nexte_ref, wslot_ref, nused_ref,
                xs_ref, wg_hbm, wu_hbm, wd_hbm, bg_ref, bu_ref, bd_ref,
                out_ref, wbuf, wgb, wub, wdb, sem_w):
    j = pl.program_id(0)

    def weight_copies(e, ws):
        return [pltpu.make_async_copy(w.at[e], wbuf.at[ws, k], sem_w.at[ws])
                for k, w in enumerate((wg_hbm, wu_hbm, wd_hbm))]

    @pl.when(j == 0)
    def _():
        for cp in weight_copies(te_ref[0], wslot_ref[0]):
            cp.start()

    @pl.when(first_ref[j] == 1)
    def _():
        ws = wslot_ref[j]
        for cp in weight_copies(te_ref[j], ws):
            cp.wait()
        wgb[...] = wbuf[ws, 0].astype(bf16)
        wub[...] = wbuf[ws, 1].astype(bf16)
        wdb[...] = wbuf[ws, 2].astype(bf16)
        ne = nexte_ref[j]

        @pl.when(ne >= 0)
        def _():
            for cp in weight_copies(ne, 1 - ws):
                cp.start()

    used = j < nused_ref[0]

    @pl.when(used)
    def _():
        xb = _unpack_rows(_load_packed(xs_ref, 0, MOE_TM)).astype(bf16)
        g = jnp.dot(xb, wgb[...], preferred_element_type=f32) + bg_ref[0]
        u = jnp.dot(xb, wub[...], preferred_element_type=f32) + bu_ref[0]
        g = jnp.minimum(g, SWIGLU_LIMIT)
        u = jnp.clip(u, -SWIGLU_LIMIT, SWIGLU_LIMIT)
        act = (u + 1.0) * (g * jax.nn.sigmoid(SWIGLU_ALPHA * g))
        y = jnp.dot(act.astype(bf16), wdb[...], preferred_element_type=f32) + bd_ref[0]
        _store_packed(out_ref, _pack_rows(y), MOE_TM)

    @pl.when(jnp.logical_not(used))
    def _():
        out_ref[...] = jnp.zeros_like(out_ref)


def _moe(tile_e, tile_first, next_e, wslot, n_used, xs, w_gate, b_gate, w_up, b_up, w_down, b_down):
    ntile = tile_e.shape[0]
    bspec = pl.BlockSpec((1, 1, D_MODEL), lambda j, te, *_: (te[j], 0, 0))
    hbm = pl.BlockSpec(memory_space=pl.ANY)
    grid_spec = pltpu.PrefetchScalarGridSpec(
        num_scalar_prefetch=5,
        grid=(ntile,),
        in_specs=[pl.BlockSpec((MOE_TM * ROW_SUB, LANES),
                               lambda j, te, fi, ne, ws, nu: (jnp.minimum(j, nu[0] - 1), 0)),
                  hbm, hbm, hbm, bspec, bspec, bspec],
        out_specs=pl.BlockSpec((MOE_TM * ROW_SUB, LANES), lambda j, *_: (j, 0)),
        scratch_shapes=[pltpu.VMEM((2, 3, D_MODEL, D_MODEL), f32),
                        pltpu.VMEM((D_MODEL, D_MODEL), bf16),
                        pltpu.VMEM((D_MODEL, D_MODEL), bf16),
                        pltpu.VMEM((D_MODEL, D_MODEL), bf16),
                        pltpu.SemaphoreType.DMA((2,))],
    )
    return pl.pallas_call(
        _moe_kernel,
        grid_spec=grid_spec,
        out_shape=jax.ShapeDtypeStruct((ntile * MOE_TM * ROW_SUB, LANES), i32),
        compiler_params=_params(("arbitrary",)),
        name="moe",
    )(tile_e, tile_first, next_e, wslot, n_used, xs, w_gate, w_up, w_down, b_gate, b_up, b_down)


CB_TM = 256


def _combine_kernel(dcur_ref, dnxt_ref, yb_hbm, rw_ref, x1_ref, g2_ref, lng_ref, lnb_ref, out_ref,
                    ybuf0, ybuf1, sem):
    i = pl.program_id(0)
    last = pl.num_programs(0) - 1
    ybufs = (ybuf0, ybuf1)

    def row_copy(d, k, r, s):
        return pltpu.make_async_copy(
            yb_hbm.at[pl.ds(pl.multiple_of(d * ROW_SUB, ROW_SUB), ROW_SUB)],
            ybufs[s].at[pl.ds(pl.multiple_of((k * CB_TM + r) * ROW_SUB, ROW_SUB), ROW_SUB)],
            sem.at[s])

    @pl.when(i == 0)
    def _():
        for k in range(TOP_K):
            def body(r, c, k=k):
                row_copy(dcur_ref[0, k, r], k, r, 0).start()
                return c
            lax.fori_loop(0, CB_TM, body, 0, unroll=8)

    for s in range(2):
        @pl.when(i % 2 == s)
        def _(s=s):
            pltpu.make_async_copy(yb_hbm.at[pl.ds(0, TOP_K * CB_TM * ROW_SUB)], ybufs[s], sem.at[s]).wait()

            @pl.when(i < last)
            def _():
                for k in range(TOP_K):
                    for r in range(CB_TM):
                        row_copy(dnxt_ref[0, k, r], k, r, 1 - s).start(priority=r % 2)

            parts = [_unpack_rows(_load_packed(ybufs[s], k * CB_TM, CB_TM)) * rw_ref[:, k:k + 1]
                     for k in range(TOP_K)]
            y = (parts[0] + parts[1]) + (parts[2] + parts[3])
            out_ref[...] = _ln(DN_ALPHA * x1_ref[...] + g2_ref[0] * y) * lng_ref[...] + lnb_ref[...]


def _combine(dest3, yb, rw, x1, g2, ln2g, ln2b, seq):
    t = x1.shape[0]
    nb = t // CB_TM
    per_b = seq // CB_TM
    return pl.pallas_call(
        _combine_kernel,
        grid=(nb,),
        in_specs=[pl.BlockSpec((1, TOP_K, CB_TM), lambda i: (i, 0, 0), memory_space=pltpu.SMEM),
                  pl.BlockSpec((1, TOP_K, CB_TM), lambda i: (jnp.minimum(i + 1, nb - 1), 0, 0),
                               memory_space=pltpu.SMEM),
                  pl.BlockSpec(memory_space=pl.ANY),
                  pl.BlockSpec((CB_TM, LANES), lambda i: (i, 0)),
                  pl.BlockSpec((CB_TM, D_MODEL), lambda i: (i, 0)),
                  pl.BlockSpec((1, 1, D_MODEL), lambda i: (i // per_b, 0, 0)),
                  pl.BlockSpec((1, D_MODEL), lambda i: (0, 0)),
                  pl.BlockSpec((1, D_MODEL), lambda i: (0, 0))],
        out_specs=pl.BlockSpec((CB_TM, D_MODEL), lambda i: (i, 0)),
        out_shape=jax.ShapeDtypeStruct((t, D_MODEL), f32),
        scratch_shapes=[pltpu.VMEM((TOP_K * CB_TM * ROW_SUB, LANES), i32),
                        pltpu.VMEM((TOP_K * CB_TM * ROW_SUB, LANES), i32),
                        pltpu.SemaphoreType.DMA((2,))],
        compiler_params=_params(("arbitrary",)),
        name="combine",
    )(dest3, dest3, yb, rw, x1, g2, ln2g, ln2b)


def _t5_bucket(dist):
    d = dist.astype(f32)
    large = REL_MAX_EXACT + jnp.log(jnp.maximum(d, float(REL_MAX_EXACT)) / REL_MAX_EXACT) / math.log(
        REL_MAX_DIST / REL_MAX_EXACT) * (REL_BUCKETS - REL_MAX_EXACT)
    large = jnp.minimum(large.astype(i32), REL_BUCKETS - 1)
    return jnp.where(dist < REL_MAX_EXACT, dist, large)


def _bias_indices():
    qi = jnp.arange(ATT_BLOCK)[:, None]
    ki = jnp.arange(2 * ATT_BLOCK)[None, :]
    didx = qi + ATT_BLOCK - ki
    buckets, bands = [], []
    for win, dil in DIL_PAIRS:
        buckets.append(_t5_bucket(jnp.clip(didx, 0, None) * dil))
        bands.append(((didx >= 0) & (didx <= win // dil)).astype(i32))
    return jnp.stack(buckets).astype(i32), jnp.stack(bands)


def _residue_perm(tm, dil):
    n = tm // dil
    dst = jnp.arange(tm)
    src = (dst % n) * dil + dst // n
    return (src[:, None] == jnp.arange(tm)[None, :]).astype(bf16)


def kernel(x, c, w_ada, b_ada, w_in, gm_ln_g, gm_ln_b, gm_w_s, gm_b_s, w_branch_a, w_branch_b, w_out,
           rel_bias, ln1_g, ln1_b, w_router, b_router, w_gate, b_gate, w_up, b_up, w_down, b_down,
           ln2_g, ln2_b):
    batch, seq, _ = x.shape
    t = batch * seq
    l = 0
    x2 = x.reshape(t, D_MODEL)

    c8 = jnp.pad(c, ((0, 8 - batch), (0, 0)))
    mod = _adaln(c8, w_ada[l], b_ada[l][None, :])[:batch]
    sh1, sc1, g1, sh2, sc2, g2 = [m[:, None, :] for m in jnp.split(mod, 6, axis=-1)]

    perms = [_residue_perm(IN_TM, dil) for _win, dil in DIL_PAIRS]
    uv, gates, *qkvs = _inproj(x2, sc1, sh1, w_in[l].astype(bf16), perms, batch, seq)

    bs_full = jnp.repeat(gm_b_s[l].T, GM_WIDTH // GM_GROUPS, axis=1)
    ya = _gmlp(uv, gm_ln_g[l][None, :], gm_ln_b[l][None, :], gm_w_s[l], bs_full)

    bucket, band = _bias_indices()
    bias = _relbias(rel_bias, bucket, band)
    os_, ls_ = [], []
    for g, (_win, dil) in enumerate(DIL_PAIRS):
        o, lse = _attn_group(qkvs[g], bias, g, dil, batch, seq)
        os_.append(o)
        ls_.append(lse)

    wr = jnp.pad(w_router[l], ((0, 0), (0, LANES - N_EXPERTS)))
    wr_hi = wr.astype(bf16)
    wr_parts = jnp.concatenate([wr_hi, (wr - wr_hi.astype(f32)).astype(bf16)], axis=1)
    br = jnp.pad(b_router[l], (0, LANES - N_EXPERTS))[None, :]
    tri = (jnp.arange(MIX_TM)[None, :] < jnp.arange(MIX_TM)[:, None]).astype(bf16)
    perms_t = [_residue_perm(MIX_TM, dil).T for _win, dil in DIL_PAIRS]
    expand = (jnp.arange(LANES)[:, None] == jnp.arange(ATT_WIDTH)[None, :] // HEAD_DIM).astype(bf16)
    expand = jnp.concatenate([expand, expand], axis=0)
    x1, h2, route, rw, cnt = _mix(
        os_, ls_, perms_t, expand, ya, gates, x2, g1, sc2, sh2,
        w_branch_a[l].astype(bf16), w_branch_b[l].astype(bf16), w_out[l].astype(bf16),
        ln1_g[l][None, :], ln1_b[l][None, :], wr_parts, br, tri, seq)

    top_e = route[:, :TOP_K]
    rank = route[:, TOP_K:2 * TOP_K]
    counts = cnt[0, :N_EXPERTS].astype(i32)
    pcounts = (counts + MOE_TM - 1) // MOE_TM * MOE_TM
    pends = jnp.cumsum(pcounts)
    pstarts = pends - pcounts
    experts = jnp.arange(N_EXPERTS, dtype=i32)
    dest = jnp.sum(jnp.where(top_e[:, :, None] == experts, pstarts, 0), axis=-1) + rank
    ntile = t * TOP_K // MOE_TM + N_EXPERTS
    n_used = (pends[-1] // MOE_TM).reshape(1)
    tile_idx = jnp.minimum(jnp.arange(ntile, dtype=i32), n_used - 1)
    tile_e = jnp.sum((pends[None, :] <= (tile_idx * MOE_TM)[:, None]).astype(i32), axis=1)
    tile_first = jnp.concatenate([jnp.ones((1,), i32), (tile_e[1:] != tile_e[:-1]).astype(i32)])
    nonempty = counts > 0
    later = lax.cummin(jnp.where(nonempty, experts, N_EXPERTS), reverse=True)
    next_nonempty = jnp.concatenate([later[1:], jnp.full((1,), N_EXPERTS, i32)])
    next_nonempty = jnp.where(next_nonempty >= N_EXPERTS, -1, next_nonempty)
    expert_slot = (jnp.cumsum(nonempty.astype(i32)) - 1) % 2

    dest3 = dest.reshape(t // DISP_TM, DISP_TM, TOP_K).transpose(0, 2, 1)
    xs = _dispatch(pends, pcounts, n_used, dest3, h2, ntile)
    yb = _moe(tile_e, tile_first, next_nonempty[tile_e], expert_slot[tile_e], n_used, xs,
              w_gate[l], b_gate[l][:, None, :], w_up[l], b_up[l][:, None, :],
              w_down[l], b_down[l][:, None, :])
    out = _combine(dest3, yb, rw, x1, g2, ln2_g[l][None, :], ln2_b[l][None, :], seq)
    return out.reshape(batch, seq, D_MODEL)
```

```python
import functools
import math

import jax
import jax.numpy as jnp
from jax import lax
from jax.experimental import pallas as pl
from jax.experimental.pallas import tpu as pltpu

f32 = jnp.float32
bf16 = jnp.bfloat16
i32 = jnp.int32

D_MODEL = 1024
GM_WIDTH = 512
GM_GROUPS = 8
GM_CHUNK = 128
DIL_PAIRS = ((128, 1), (512, 4), (2048, 16))
N_DIL = 3
HEADS_PER_GROUP = 8
HEAD_DIM = 64
ATT_WIDTH = 512
ATT_BLOCK = 128
NEG_INF = -1e30
REL_BUCKETS = 32
REL_MAX_EXACT = 16
REL_MAX_DIST = 2048
N_EXPERTS = 32
TOP_K = 4
SWIGLU_LIMIT = 7.0
SWIGLU_ALPHA = 1.702
MOE_BLOCK = 128
DEPTH = 1
DN_ALPHA = (2 * DEPTH) ** 0.25
LN_EPS = 1e-5
UV_COLS = 2 * GM_WIDTH
QKV_COLS = N_DIL * 3 * ATT_WIDTH
GATE_COLS = 2 * D_MODEL
IN_COLS = UV_COLS + QKV_COLS + GATE_COLS

LANES = 128
SUBLANES = 8
VMEM_LIMIT = 56 * 1024 * 1024


def _ln(x):
    mu = jnp.mean(x, axis=-1, keepdims=True)
    xc = x - mu
    var = jnp.mean(xc * xc, axis=-1, keepdims=True)
    return xc * lax.rsqrt(var + LN_EPS)


def _params(sem, vmem=VMEM_LIMIT):
    return pltpu.CompilerParams(dimension_semantics=sem, vmem_limit_bytes=vmem)


def _adaln_kernel(c_ref, w_ref, b_ref, o_ref):
    c = c_ref[...]
    s = c * jax.nn.sigmoid(c)
    o_ref[...] = jnp.dot(s, w_ref[...], preferred_element_type=f32,
                         precision=lax.Precision.HIGHEST) + b_ref[...]


def _adaln(c8, w_ada, b_ada):
    n = w_ada.shape[1] // D_MODEL
    return pl.pallas_call(
        _adaln_kernel,
        grid=(n,),
        in_specs=[pl.BlockSpec((8, D_MODEL), lambda j: (0, 0)),
                  pl.BlockSpec((D_MODEL, D_MODEL), lambda j: (0, j)),
                  pl.BlockSpec((1, D_MODEL), lambda j: (0, j))],
        out_specs=pl.BlockSpec((8, D_MODEL), lambda j: (0, j)),
        out_shape=jax.ShapeDtypeStruct((8, w_ada.shape[1]), f32),
        compiler_params=_params(("arbitrary",)),
        name="adaln",
    )(c8, w_ada, b_ada)


IN_TM = 256
IN_CW = 512
GRP_COLS = 3 * ATT_WIDTH


def _inproj_kernel(x_ref, sc_ref, sh_ref, w_ref, p1_ref, p2_ref,
                   uv_ref, gt_ref, qkv0_ref, qkv1_ref, qkv2_ref):
    xn = _ln(x_ref[...])
    h = (xn * (1.0 + sc_ref[0]) + sh_ref[0]).astype(bf16)
    hp = [h,
          jnp.dot(p1_ref[...], h, preferred_element_type=f32).astype(bf16),
          jnp.dot(p2_ref[...], h, preferred_element_type=f32).astype(bf16)]
    for c0 in range(0, UV_COLS, IN_CW):
        acc = jnp.dot(h, w_ref[:, c0:c0 + IN_CW], preferred_element_type=f32)
        uv_ref[:, c0:c0 + IN_CW] = jax.nn.gelu(acc).astype(bf16)
    for g, (qref, (_win, dil)) in enumerate(zip((qkv0_ref, qkv1_ref, qkv2_ref), DIL_PAIRS)):
        n = IN_TM // dil
        for q0 in range(0, GRP_COLS, IN_CW):
            c0 = UV_COLS + g * GRP_COLS + q0
            acc = jnp.dot(hp[g], w_ref[:, c0:c0 + IN_CW], preferred_element_type=f32).astype(bf16)
            for rho in range(dil):
                qref[0, rho, :, q0:q0 + IN_CW] = acc[rho * n:(rho + 1) * n, :]
    for g0 in range(0, GATE_COLS, IN_CW):
        c0 = UV_COLS + QKV_COLS + g0
        acc = jnp.dot(h, w_ref[:, c0:c0 + IN_CW], preferred_element_type=f32)
        gt_ref[:, g0:g0 + IN_CW] = jax.nn.sigmoid(acc).astype(bf16)


def _inproj(x2, sc1, sh1, w_in_bf, perms, batch, seq):
    t = x2.shape[0]
    per_b = seq // IN_TM
    qkv_specs, qkv_shapes = [], []
    for _win, dil in DIL_PAIRS:
        n = IN_TM // dil
        qkv_specs.append(pl.BlockSpec((1, dil, n, GRP_COLS), lambda i: (i // per_b, 0, i % per_b, 0)))
        qkv_shapes.append(jax.ShapeDtypeStruct((batch, dil, seq // dil, GRP_COLS), bf16))
    return pl.pallas_call(
        _inproj_kernel,
        grid=(t // IN_TM,),
        in_specs=[pl.BlockSpec((IN_TM, D_MODEL), lambda i: (i, 0)),
                  pl.BlockSpec((1, 1, D_MODEL), lambda i: (i // per_b, 0, 0)),
                  pl.BlockSpec((1, 1, D_MODEL), lambda i: (i // per_b, 0, 0)),
                  pl.BlockSpec((D_MODEL, IN_COLS), lambda i: (0, 0)),
                  pl.BlockSpec((IN_TM, IN_TM), lambda i: (0, 0)),
                  pl.BlockSpec((IN_TM, IN_TM), lambda i: (0, 0))],
        out_specs=[pl.BlockSpec((IN_TM, UV_COLS), lambda i: (i, 0)),
                   pl.BlockSpec((IN_TM, GATE_COLS), lambda i: (i, 0))] + qkv_specs,
        out_shape=[jax.ShapeDtypeStruct((t, UV_COLS), bf16),
                   jax.ShapeDtypeStruct((t, GATE_COLS), bf16)] + qkv_shapes,
        compiler_params=_params(("arbitrary",)),
        name="inproj",
    )(x2, sc1, sh1, w_in_bf, perms[1], perms[2])


GM_TM = 512


def _gmlp_kernel(u_ref, v_ref, g_ref, b_ref, ws_ref, bs_ref, ya_ref):
    row = lax.broadcasted_iota(i32, (GM_CHUNK, GM_CHUNK), 0)
    col = lax.broadcasted_iota(i32, (GM_CHUNK, GM_CHUNK), 1)
    causal = col <= row
    first_half = lax.broadcasted_iota(i32, (GM_CHUNK, LANES), 1) < (GM_WIDTH // GM_GROUPS)
    ws = [jnp.where(causal, ws_ref[g], 0.0).astype(bf16) for g in range(GM_GROUPS)]
    for ch in range(GM_TM // GM_CHUNK):
        r0 = ch * GM_CHUNK
        vn = _ln(v_ref[r0:r0 + GM_CHUNK, :].astype(f32)) * g_ref[...] + b_ref[...]
        vn = vn.astype(bf16)
        for j in range(GM_WIDTH // LANES):
            slab = vn[:, j * LANES:(j + 1) * LANES]
            s_lo = jnp.dot(ws[2 * j], slab, preferred_element_type=f32)
            s_hi = jnp.dot(ws[2 * j + 1], slab, preferred_element_type=f32)
            s = jnp.where(first_half, s_lo, s_hi) + bs_ref[:, j * LANES:(j + 1) * LANES]
            u = u_ref[r0:r0 + GM_CHUNK, j * LANES:(j + 1) * LANES].astype(f32)
            ya_ref[r0:r0 + GM_CHUNK, j * LANES:(j + 1) * LANES] = (u * s).astype(bf16)


def _gmlp(uv, ln_g, ln_b, w_s, bs_full):
    t = uv.shape[0]
    return pl.pallas_call(
        _gmlp_kernel,
        grid=(t // GM_TM,),
        in_specs=[pl.BlockSpec((GM_TM, GM_WIDTH), lambda i: (i, 0)),
                  pl.BlockSpec((GM_TM, GM_WIDTH), lambda i: (i, 1)),
                  pl.BlockSpec((1, GM_WIDTH), lambda i: (0, 0)),
                  pl.BlockSpec((1, GM_WIDTH), lambda i: (0, 0)),
                  pl.BlockSpec((GM_GROUPS, GM_CHUNK, GM_CHUNK), lambda i: (0, 0, 0)),
                  pl.BlockSpec((GM_CHUNK, GM_WIDTH), lambda i: (0, 0))],
        out_specs=pl.BlockSpec((GM_TM, GM_WIDTH), lambda i: (i, 0)),
        out_shape=jax.ShapeDtypeStruct((t, GM_WIDTH), bf16),
        compiler_params=_params(("arbitrary",)),
        name="gmlp",
    )(uv, uv, ln_g, ln_b, w_s, bs_full)


def _relbias_kernel(tab_ref, bucket_ref, band_ref, out_ref):
    g = pl.program_id(0)
    bk = bucket_ref[0]
    band = band_ref[0] > 0
    for h in range(HEADS_PER_GROUP):
        acc = jnp.zeros((ATT_BLOCK, 2 * ATT_BLOCK), f32)
        for b in range(REL_BUCKETS):
            acc = jnp.where(bk == b, tab_ref[b, g * HEADS_PER_GROUP + h], acc)
        out_ref[0, h] = jnp.where(band, acc, NEG_INF)


def _relbias(rel_bias, bucket, band):
    return pl.pallas_call(
        _relbias_kernel,
        grid=(N_DIL,),
        in_specs=[pl.BlockSpec(memory_space=pltpu.SMEM),
                  pl.BlockSpec((1, ATT_BLOCK, 2 * ATT_BLOCK), lambda g: (g, 0, 0)),
                  pl.BlockSpec((1, ATT_BLOCK, 2 * ATT_BLOCK), lambda g: (g, 0, 0))],
        out_specs=pl.BlockSpec((1, HEADS_PER_GROUP, ATT_BLOCK, 2 * ATT_BLOCK),
                               lambda g: (g, 0, 0, 0)),
        out_shape=jax.ShapeDtypeStruct((N_DIL, HEADS_PER_GROUP, ATT_BLOCK, 2 * ATT_BLOCK), f32),
        compiler_params=_params(("arbitrary",)),
        name="relbias",
    )(rel_bias, bucket, band)


ATT_MAX_STEP_BLOCKS = 4


def _attn_kernel(nblk, q_ref, kp_ref, kc_ref, vp_ref, vc_ref, bias_ref, o_ref, lse_ref):
    first = pl.program_id(2) == 0
    lane = lax.broadcasted_iota(i32, (ATT_BLOCK, LANES), 1)
    lo_half = lane < HEAD_DIM
    nt = (((1,), (1,)), ((), ()))
    ones = jnp.ones((2 * ATT_BLOCK, LANES), bf16)
    n_slab = ATT_WIDTH // LANES
    logits, v_ext = [], []
    for i in range(nblk):
        cur = slice(i * ATT_BLOCK, (i + 1) * ATT_BLOCK)
        prv = slice((i - 1) * ATT_BLOCK, i * ATT_BLOCK)
        for j in range(n_slab):
            sl = slice(j * LANES, (j + 1) * LANES)
            q = q_ref[0, 0, cur, sl] * (HEAD_DIM ** -0.5)
            k_prev = kp_ref[0, 0, :, sl] if i == 0 else kc_ref[0, 0, prv, sl]
            v_prev = vp_ref[0, 0, :, sl] if i == 0 else vc_ref[0, 0, prv, sl]
            k_cat = jnp.concatenate([k_prev, kc_ref[0, 0, cur, sl]], axis=0)
            v_cat = jnp.concatenate([v_prev, vc_ref[0, 0, cur, sl]], axis=0)
            v_ext.append(jnp.concatenate([v_cat, ones], axis=1))
            for hh in range(2):
                qm = jnp.where(lo_half if hh == 0 else jnp.logical_not(lo_half), q, 0.0).astype(bf16)
                logits.append(lax.dot_general(qm, k_cat, nt, preferred_element_type=f32))
    bias = bias_ref[0].reshape(HEADS_PER_GROUP * ATT_BLOCK, 2 * ATT_BLOCK)
    rows_per_block = HEADS_PER_GROUP * ATT_BLOCK
    lg = jnp.concatenate(logits, axis=0) + jnp.concatenate([bias] * nblk, axis=0)
    row = lax.broadcasted_iota(i32, lg.shape, 0)
    col = lax.broadcasted_iota(i32, lg.shape, 1)
    no_prev = jnp.logical_and(first, jnp.logical_and(row < rows_per_block, col < ATT_BLOCK))
    lg = jnp.where(no_prev, NEG_INF, lg)
    m = jnp.max(lg, axis=-1, keepdims=True)
    p = jnp.exp(lg - m).astype(bf16)
    for i in range(nblk):
        cur = slice(i * ATT_BLOCK, (i + 1) * ATT_BLOCK)
        lse_tile = jnp.zeros((ATT_BLOCK, LANES), f32)
        for j in range(n_slab):
            outs = []
            for hh in range(2):
                h = 2 * j + hh
                r0 = i * rows_per_block + h * ATT_BLOCK
                r = jnp.dot(p[r0:r0 + ATT_BLOCK], v_ext[i * n_slab + j], preferred_element_type=f32)
                den = r[:, LANES:]
                outs.append(r[:, :LANES] * (1.0 / den))
                lse_h = m[r0:r0 + ATT_BLOCK] + jnp.log(den)
                lse_tile = jnp.where(lane == h, lse_h, lse_tile)
            o_ref[0, 0, cur, j * LANES:(j + 1) * LANES] = jnp.where(lo_half, outs[0], outs[1]).astype(bf16)
        lse_ref[0, 0, cur, :] = lse_tile


def _attn_group(qkv_g, bias, g, dil, batch, seq):
    l = seq // dil
    nblk = min(ATT_MAX_STEP_BLOCKS, l // ATT_BLOCK)
    tm = nblk * ATT_BLOCK
    nsteps = l // tm

    def cur(cb):
        return pl.BlockSpec((1, 1, tm, ATT_WIDTH), lambda b, r, n: (b, r, n, cb))

    def prev(cb):
        return pl.BlockSpec((1, 1, ATT_BLOCK, ATT_WIDTH),
                            lambda b, r, n: (b, r, jnp.maximum(n * nblk - 1, 0), cb))

    return pl.pallas_call(
        functools.partial(_attn_kernel, nblk),
        grid=(batch, dil, nsteps),
        in_specs=[cur(0), prev(1), cur(1), prev(2), cur(2),
                  pl.BlockSpec((1, HEADS_PER_GROUP, ATT_BLOCK, 2 * ATT_BLOCK),
                               lambda b, r, n: (g, 0, 0, 0))],
        out_specs=[pl.BlockSpec((1, 1, tm, ATT_WIDTH), lambda b, r, n: (b, r, n, 0)),
                   pl.BlockSpec((1, 1, tm, LANES), lambda b, r, n: (b, r, n, 0))],
        out_shape=[jax.ShapeDtypeStruct((batch, dil, l, ATT_WIDTH), bf16),
                   jax.ShapeDtypeStruct((batch, dil, l, LANES), f32)],
        compiler_params=_params(("arbitrary", "arbitrary", "arbitrary")),
        name=f"attn_g{g}",
    )(qkv_g, qkv_g, qkv_g, qkv_g, qkv_g, bias)


ROW_WORDS = D_MODEL // 2
ROW_SUB = ROW_WORDS // LANES
HI_MASK = -65536


def _pack_rows(x):
    bits = lax.bitcast_convert_type(x.astype(bf16).astype(f32), i32)
    return lax.shift_right_logical(bits[:, :ROW_WORDS], 16) | (bits[:, ROW_WORDS:] & HI_MASK)


def _unpack_rows(words):
    lo = lax.bitcast_convert_type(lax.shift_left(words, 16), f32)
    hi = lax.bitcast_convert_type(words & HI_MASK, f32)
    return jnp.concatenate([lo, hi], axis=1)


def _store_packed(ref, words, n, first_row=0):
    for r in range(ROW_SUB):
        ref[pl.ds(first_row * ROW_SUB + r, n, stride=ROW_SUB), :] = words[:, r * LANES:(r + 1) * LANES]


def _load_packed(ref, first_row, n):
    return jnp.concatenate([ref[pl.ds(first_row * ROW_SUB + r, n, stride=ROW_SUB), :] for r in range(ROW_SUB)],
                           axis=1)


MIX_TM = 256
MIX_SUB = 128


def _split_bf16(x, parts):
    out = []
    for _ in range(parts):
        hi = x.astype(bf16)
        out.append(hi)
        x = x - hi.astype(f32)
    return out


def _mix_kernel(o0_ref, o1_ref, o2_ref, l0_ref, l1_ref, l2_ref, pt1_ref, pt2_ref, ex_ref,
                ya_ref, gt_ref, x_ref,
                g1_ref, sc2_ref, sh2_ref, wa_ref, wb_ref, wo_ref, ln1g_ref, ln1b_ref,
                wrc_ref, br_ref, tri_ref,
                x1_ref, h2_ref, route_ref, rw_ref, cnt_ref, run_ref, xr_ref):
    step = pl.program_id(0)

    @pl.when(step == 0)
    def _():
        run_ref[...] = jnp.zeros_like(run_ref)
        xr_ref[...] = jnp.zeros_like(xr_ref)

    def back_rows(r0):
        rows = slice(r0, r0 + MIX_SUB)
        x1 = _ln(xr_ref[rows, :]) * ln1g_ref[...] + ln1b_ref[...]
        x1_ref[rows, :] = x1
        h2 = _ln(x1) * (1.0 + sc2_ref[0]) + sh2_ref[0]
        _store_packed(h2_ref, _pack_rows(h2), MIX_SUB, r0)
        h_hi, h_lo = _split_bf16(h2, 2)
        hi_both = jnp.dot(h_hi, wrc_ref[...], preferred_element_type=f32)
        return (hi_both[:, :LANES]
                + (hi_both[:, LANES:] + jnp.dot(h_lo, wrc_ref[:, :LANES], preferred_element_type=f32))
                ) + br_ref[...]

    def front_rows(r0):
        rows = slice(r0, r0 + MIX_SUB)
        os_, ls_ = [o0_ref[0, 0, rows, :].astype(f32)], [l0_ref[0, 0, rows, :]]
        for o_ref, l_ref, pt_ref in ((o1_ref, l1_ref, pt1_ref), (o2_ref, l2_ref, pt2_ref)):
            pt = pt_ref[rows, :]
            os_.append(jnp.dot(pt, o_ref[0].reshape(MIX_TM, ATT_WIDTH), preferred_element_type=f32))
            parts = [jnp.dot(pt, part, preferred_element_type=f32)
                     for part in _split_bf16(l_ref[0].reshape(MIX_TM, LANES), 3)]
            ls_.append((parts[0] + parts[1]) + parts[2])
        lm = jnp.maximum(jnp.maximum(ls_[0], ls_[1]), ls_[2])
        es = [jnp.exp(lse - lm) for lse in ls_]
        inv = 1.0 / (es[0] + es[1] + es[2])
        yb = jnp.zeros((MIX_SUB, ATT_WIDTH), f32)
        for e, o in zip(es, os_):
            w_parts = jnp.concatenate(_split_bf16(e * inv, 2), axis=1)
            yb = yb + jnp.dot(w_parts, ex_ref[...], preferred_element_type=f32) * o
        a = jnp.dot(ya_ref[rows, :], wa_ref[...], preferred_element_type=f32)
        b = jnp.dot(yb.astype(bf16), wb_ref[...], preferred_element_type=f32)
        merged = gt_ref[rows, :D_MODEL].astype(f32) * a + gt_ref[rows, D_MODEL:].astype(f32) * b
        mix = jnp.dot(merged.astype(bf16), wo_ref[...], preferred_element_type=f32)
        xr_ref[rows, :] = DN_ALPHA * x_ref[rows, :] + g1_ref[0] * mix

    logit_parts = []
    for r0 in range(0, MIX_TM, MIX_SUB):
        logit_parts.append(back_rows(r0))
        front_rows(r0)
    logits = jnp.concatenate(logit_parts, axis=0)
    lane = lax.broadcasted_iota(i32, (MIX_TM, LANES), 1)
    logits = jnp.where(lane < N_EXPERTS, logits, -jnp.inf)
    lane_f = lane.astype(f32)
    vals, idxs = [], []
    for _k in range(TOP_K):
        m = jnp.max(logits, axis=-1, keepdims=True)
        vals.append(m)
        idxs.append(jnp.min(jnp.where(logits == m, lane_f, float(LANES)), axis=-1, keepdims=True).astype(i32))
        logits = jnp.where(lane == idxs[-1], -jnp.inf, logits)
    exps = [jnp.exp(v - vals[0]) for v in vals]
    den = exps[0] + exps[1] + exps[2] + exps[3]
    wts = [e / den for e in exps]
    hits = [lane == idx for idx in idxs]
    counted = jnp.where(step > 0, 1.0, 0.0)
    onehot = jnp.zeros((MIX_TM, LANES), f32)
    for hit in hits:
        onehot = onehot + jnp.where(hit, counted, 0.0)
    prefix = jnp.dot(tri_ref[...], onehot.astype(bf16), preferred_element_type=f32) + run_ref[...]
    route = jnp.zeros((MIX_TM, LANES), i32)
    rw = jnp.zeros((MIX_TM, LANES), f32)
    for k in range(TOP_K):
        rank = jnp.sum(jnp.where(hits[k], prefix, 0.0), axis=-1, keepdims=True).astype(i32)
        route = jnp.where(lane == k, idxs[k], route)
        route = jnp.where(lane == TOP_K + k, rank, route)
        rw = jnp.where(lane == k, wts[k], rw)
    route_ref[...] = route
    rw_ref[...] = rw
    run = run_ref[...] + jnp.sum(onehot, axis=0, keepdims=True)
    run_ref[...] = run
    cnt_ref[...] = jnp.broadcast_to(run, cnt_ref.shape)


def _mix(os_, ls_, perms_t, expand, ya, gates, x2, g1, sc2, sh2, wa, wb, wo, ln1g, ln1b, wr_parts, br, tri, seq):
    t = x2.shape[0]
    nb = t // MIX_TM
    per_b = seq // MIX_TM
    cur = lambda i: jnp.minimum(i, nb - 1)
    prv = lambda i: jnp.maximum(i - 1, 0)
    row = lambda w: pl.BlockSpec((MIX_TM, w), lambda i: (cur(i), 0))
    out_row = lambda w: pl.BlockSpec((MIX_TM, w), lambda i: (prv(i), 0))
    const = lambda s: pl.BlockSpec(s, lambda i: tuple(0 for _ in s))
    mod_cur = pl.BlockSpec((1, 1, D_MODEL), lambda i: (cur(i) // per_b, 0, 0))
    mod_prv = pl.BlockSpec((1, 1, D_MODEL), lambda i: (prv(i) // per_b, 0, 0))
    grp = lambda w: [pl.BlockSpec((1, dil, MIX_TM // dil, w), lambda i: (cur(i) // per_b, 0, cur(i) % per_b, 0))
                     for _win, dil in DIL_PAIRS]
    return pl.pallas_call(
        _mix_kernel,
        grid=(nb + 1,),
        in_specs=grp(ATT_WIDTH) + grp(LANES) + [
                  const((MIX_TM, MIX_TM)), const((MIX_TM, MIX_TM)), const((2 * LANES, ATT_WIDTH)),
                  row(GM_WIDTH), row(GATE_COLS), row(D_MODEL),
                  mod_cur, mod_prv, mod_prv,
                  const((GM_WIDTH, D_MODEL)), const((ATT_WIDTH, D_MODEL)), const((D_MODEL, D_MODEL)),
                  const((1, D_MODEL)), const((1, D_MODEL)),
                  const((D_MODEL, 2 * LANES)), const((1, LANES)), const((MIX_TM, MIX_TM))],
        out_specs=[out_row(D_MODEL), pl.BlockSpec((MIX_TM * ROW_SUB, LANES), lambda i: (prv(i), 0)),
                   out_row(LANES), out_row(LANES), const((8, LANES))],
        out_shape=[jax.ShapeDtypeStruct((t, D_MODEL), f32),
                   jax.ShapeDtypeStruct((t * ROW_SUB, LANES), i32),
                   jax.ShapeDtypeStruct((t, LANES), i32),
                   jax.ShapeDtypeStruct((t, LANES), f32),
                   jax.ShapeDtypeStruct((8, LANES), f32)],
        scratch_shapes=[pltpu.VMEM((1, LANES), f32), pltpu.VMEM((MIX_TM, D_MODEL), f32)],
        compiler_params=_params(("arbitrary",)),
        name="mix",
    )(*os_, *ls_, perms_t[1], perms_t[2], expand, ya, gates, x2, g1, sc2, sh2, wa, wb, wo,
      ln1g, ln1b, wr_parts, br, tri)


DISP_TM = 512
MOE_TM = 512


def _dispatch_kernel(pends_ref, pcnt_ref, nused_ref, dest_ref, h2p_ref, xs_hbm, zbuf, sem, zsem):
    i = pl.program_id(0)
    ntile = xs_hbm.shape[0] // (MOE_TM * ROW_SUB)

    def zero_tile(first_row):
        return pltpu.make_async_copy(
            zbuf, xs_hbm.at[pl.ds(pl.multiple_of(first_row * ROW_SUB, MOE_TM * ROW_SUB), MOE_TM * ROW_SUB)], zsem)

    def for_each_zero_tile(fn):
        for e in range(N_EXPERTS):
            pl.when(pcnt_ref[e] > 0)(functools.partial(fn, lambda e=e: zero_tile(pends_ref[e] - MOE_TM)))
        for k in range(N_EXPERTS):
            tile = nused_ref[0] + k
            pl.when(tile < ntile)(functools.partial(fn, lambda tile=tile: zero_tile(tile * MOE_TM)))

    @pl.when(i == 0)
    def _():
        zbuf[...] = jnp.zeros_like(zbuf)
        for_each_zero_tile(lambda mk: mk().start())
        for_each_zero_tile(lambda mk: mk().wait())

    def row_copy(k, r):
        d = dest_ref[0, k, r]
        return pltpu.make_async_copy(h2p_ref.at[pl.ds(r * ROW_SUB, ROW_SUB)],
                                     xs_hbm.at[pl.ds(pl.multiple_of(d * ROW_SUB, ROW_SUB), ROW_SUB)], sem)

    for r in range(DISP_TM):
        for k in range(TOP_K):
            row_copy(k, r).start(priority=k % 2)
    for k in range(TOP_K):
        pltpu.make_async_copy(h2p_ref, xs_hbm.at[pl.ds(0, DISP_TM * ROW_SUB)], sem).wait()


def _dispatch(pends, pcounts, n_used, dest3, h2p, ntile):
    t = h2p.shape[0] // ROW_SUB
    grid_spec = pltpu.PrefetchScalarGridSpec(
        num_scalar_prefetch=3,
        grid=(t // DISP_TM,),
        in_specs=[pl.BlockSpec((1, TOP_K, DISP_TM), lambda i, *_: (i, 0, 0), memory_space=pltpu.SMEM),
                  pl.BlockSpec((DISP_TM * ROW_SUB, LANES), lambda i, *_: (i, 0))],
        out_specs=pl.BlockSpec(memory_space=pl.ANY),
        scratch_shapes=[pltpu.VMEM((MOE_TM * ROW_SUB, LANES), i32),
                        pltpu.SemaphoreType.DMA(()),
                        pltpu.SemaphoreType.DMA(())],
    )
    return pl.pallas_call(
        _dispatch_kernel,
        grid_spec=grid_spec,
        out_shape=jax.ShapeDtypeStruct((ntile * MOE_TM * ROW_SUB, LANES), i32),
        compiler_params=_params(("arbitrary",)),
        name="dispatch",
    )(pends, pcounts, n_used, dest3, h2p)


def _moe_kernel(te_ref, first_ref, nexte_ref, wslot_ref, nused_ref,
                xs_ref, wg_hbm, wu_hbm, wd_hbm, bg_ref, bu_ref, bd_ref,
                out_ref, wbuf, wgb, wub, wdb, sem_w):
    j = pl.program_id(0)

    def weight_copies(e, ws):
        return [pltpu.make_async_copy(w.at[e], wbuf.at[ws, k], sem_w.at[ws])
                for k, w in enumerate((wg_hbm, wu_hbm, wd_hbm))]

    @pl.when(j == 0)
    def _():
        for cp in weight_copies(te_ref[0], wslot_ref[0]):
            cp.start()

    @pl.when(first_ref[j] == 1)
    def _():
        ws = wslot_ref[j]
        for cp in weight_copies(te_ref[j], ws):
            cp.wait()
        wgb[...] = wbuf[ws, 0].astype(bf16)
        wub[...] = wbuf[ws, 1].astype(bf16)
        wdb[...] = wbuf[ws, 2].astype(bf16)
        ne = nexte_ref[j]

        @pl.when(ne >= 0)
        def _():
            for cp in weight_copies(ne, 1 - ws):
                cp.start()

    used = j < nused_ref[0]

    @pl.when(used)
    def _():
        xb = _unpack_rows(_load_packed(xs_ref, 0, MOE_TM)).astype(bf16)
        g = jnp.dot(xb, wgb[...], preferred_element_type=f32) + bg_ref[0]
        u = jnp.dot(xb, wub[...], preferred_element_type=f32) + bu_ref[0]
        g = jnp.minimum(g, SWIGLU_LIMIT)
        u = jnp.clip(u, -SWIGLU_LIMIT, SWIGLU_LIMIT)
        act = (u + 1.0) * (g * jax.nn.sigmoid(SWIGLU_ALPHA * g))
        y = jnp.dot(act.astype(bf16), wdb[...], preferred_element_type=f32) + bd_ref[0]
        _store_packed(out_ref, _pack_rows(y), MOE_TM)

    @pl.when(jnp.logical_not(used))
    def _():
        out_ref[...] = jnp.zeros_like(out_ref)


def _moe(tile_e, tile_first, next_e, wslot, n_used, xs, w_gate, b_gate, w_up, b_up, w_down, b_down):
    ntile = tile_e.shape[0]
    bspec = pl.BlockSpec((1, 1, D_MODEL), lambda j, te, *_: (te[j], 0, 0))
    hbm = pl.BlockSpec(memory_space=pl.ANY)
    grid_spec = pltpu.PrefetchScalarGridSpec(
        num_scalar_prefetch=5,
        grid=(ntile,),
        in_specs=[pl.BlockSpec((MOE_TM * ROW_SUB, LANES),
                               lambda j, te, fi, ne, ws, nu: (jnp.minimum(j, nu[0] - 1), 0)),
                  hbm, hbm, hbm, bspec, bspec, bspec],
        out_specs=pl.BlockSpec((MOE_TM * ROW_SUB, LANES), lambda j, *_: (j, 0)),
        scratch_shapes=[pltpu.VMEM((2, 3, D_MODEL, D_MODEL), f32),
                        pltpu.VMEM((D_MODEL, D_MODEL), bf16),
                        pltpu.VMEM((D_MODEL, D_MODEL), bf16),
                        pltpu.VMEM((D_MODEL, D_MODEL), bf16),
                        pltpu.SemaphoreType.DMA((2,))],
    )
    return pl.pallas_call(
        _moe_kernel,
        grid_spec=grid_spec,
        out_shape=jax.ShapeDtypeStruct((ntile * MOE_TM * ROW_SUB, LANES), i32),
        compiler_params=_params(("arbitrary",)),
        name="moe",
    )(tile_e, tile_first, next_e, wslot, n_used, xs, w_gate, w_up, w_down, b_gate, b_up, b_down)


CB_TM = DISP_TM


def _combine_kernel(dcur_ref, dnxt_ref, yb_hbm, rw_ref, x1_ref, g2_ref, lng_ref, lnb_ref, out_ref,
                    ybuf0, ybuf1, sem):
    i = pl.program_id(0)
    last = pl.num_programs(0) - 1
    ybufs = (ybuf0, ybuf1)

    def row_copy(d, k, r, s):
        return pltpu.make_async_copy(
            yb_hbm.at[pl.ds(pl.multiple_of(d * ROW_SUB, ROW_SUB), ROW_SUB)],
            ybufs[s].at[pl.ds(pl.multiple_of((k * CB_TM + r) * ROW_SUB, ROW_SUB), ROW_SUB)],
            sem.at[s])

    @pl.when(i == 0)
    def _():
        for k in range(TOP_K):
            def body(r, c, k=k):
                row_copy(dcur_ref[0, k, r], k, r, 0).start()
                return c
            lax.fori_loop(0, CB_TM, body, 0, unroll=8)

    for s in range(2):
        @pl.when(i % 2 == s)
        def _(s=s):
            pltpu.make_async_copy(yb_hbm.at[pl.ds(0, TOP_K * CB_TM * ROW_SUB)], ybufs[s], sem.at[s]).wait()

            @pl.when(i < last)
            def _():
                for k in range(TOP_K):
                    for r in range(CB_TM):
                        row_copy(dnxt_ref[0, k, r], k, r, 1 - s).start(priority=r % 2)

            parts = [_unpack_rows(_load_packed(ybufs[s], k * CB_TM, CB_TM)) * rw_ref[:, k:k + 1]
                     for k in range(TOP_K)]
            y = (parts[0] + parts[1]) + (parts[2] + parts[3])
            out_ref[...] = _ln(DN_ALPHA * x1_ref[...] + g2_ref[0] * y) * lng_ref[...] + lnb_ref[...]


def _combine(dest3, yb, rw, x1, g2, ln2g, ln2b, seq):
    t = x1.shape[0]
    nb = t // CB_TM
    per_b = seq // CB_TM
    return pl.pallas_call(
        _combine_kernel,
        grid=(nb,),
        in_specs=[pl.BlockSpec((1, TOP_K, CB_TM), lambda i: (i, 0, 0), memory_space=pltpu.SMEM),
                  pl.BlockSpec((1, TOP_K, CB_TM), lambda i: (jnp.minimum(i + 1, nb - 1), 0, 0),
                               memory_space=pltpu.SMEM),
                  pl.BlockSpec(memory_space=pl.ANY),
                  pl.BlockSpec((CB_TM, LANES), lambda i: (i, 0)),
                  pl.BlockSpec((CB_TM, D_MODEL), lambda i: (i, 0)),
                  pl.BlockSpec((1, 1, D_MODEL), lambda i: (i // per_b, 0, 0)),
                  pl.BlockSpec((1, D_MODEL), lambda i: (0, 0)),
                  pl.BlockSpec((1, D_MODEL), lambda i: (0, 0))],
        out_specs=pl.BlockSpec((CB_TM, D_MODEL), lambda i: (i, 0)),
        out_shape=jax.ShapeDtypeStruct((t, D_MODEL), f32),
        scratch_shapes=[pltpu.VMEM((TOP_K * CB_TM * ROW_SUB, LANES), i32),
                        pltpu.VMEM((TOP_K * CB_TM * ROW_SUB, LANES), i32),
                        pltpu.SemaphoreType.DMA((2,))],
        compiler_params=_params(("arbitrary",)),
        name="combine",
    )(dest3, dest3, yb, rw, x1, g2, ln2g, ln2b)


def _t5_bucket(dist):
    d = dist.astype(f32)
    large = REL_MAX_EXACT + jnp.log(jnp.maximum(d, float(REL_MAX_EXACT)) / REL_MAX_EXACT) / math.log(
        REL_MAX_DIST / REL_MAX_EXACT) * (REL_BUCKETS - REL_MAX_EXACT)
    large = jnp.minimum(large.astype(i32), REL_BUCKETS - 1)
    return jnp.where(dist < REL_MAX_EXACT, dist, large)


def _bias_indices():
    qi = jnp.arange(ATT_BLOCK)[:, None]
    ki = jnp.arange(2 * ATT_BLOCK)[None, :]
    didx = qi + ATT_BLOCK - ki
    buckets, bands = [], []
    for win, dil in DIL_PAIRS:
        buckets.append(_t5_bucket(jnp.clip(didx, 0, None) * dil))
        bands.append(((didx >= 0) & (didx <= win // dil)).astype(i32))
    return jnp.stack(buckets).astype(i32), jnp.stack(bands)


def _residue_perm(tm, dil):
    n = tm // dil
    dst = jnp.arange(tm)
    src = (dst % n) * dil + dst // n
    return (src[:, None] == jnp.arange(tm)[None, :]).astype(bf16)


def kernel(x, c, w_ada, b_ada, w_in, gm_ln_g, gm_ln_b, gm_w_s, gm_b_s, w_branch_a, w_branch_b, w_out,
           rel_bias, ln1_g, ln1_b, w_router, b_router, w_gate, b_gate, w_up, b_up, w_down, b_down,
           ln2_g, ln2_b):
    batch, seq, _ = x.shape
    t = batch * seq
    l = 0
    x2 = x.reshape(t, D_MODEL)

    c8 = jnp.pad(c, ((0, 8 - batch), (0, 0)))
    mod = _adaln(c8, w_ada[l], b_ada[l][None, :])[:batch]
    sh1, sc1, g1, sh2, sc2, g2 = [m[:, None, :] for m in jnp.split(mod, 6, axis=-1)]

    perms = [_residue_perm(IN_TM, dil) for _win, dil in DIL_PAIRS]
    uv, gates, *qkvs = _inproj(x2, sc1, sh1, w_in[l].astype(bf16), perms, batch, seq)

    bs_full = jnp.repeat(gm_b_s[l].T, GM_WIDTH // GM_GROUPS, axis=1)
    ya = _gmlp(uv, gm_ln_g[l][None, :], gm_ln_b[l][None, :], gm_w_s[l], bs_full)

    bucket, band = _bias_indices()
    bias = _relbias(rel_bias, bucket, band)
    os_, ls_ = [], []
    for g, (_win, dil) in enumerate(DIL_PAIRS):
        o, lse = _attn_group(qkvs[g], bias, g, dil, batch, seq)
        os_.append(o)
        ls_.append(lse)

    wr = jnp.pad(w_router[l], ((0, 0), (0, LANES - N_EXPERTS)))
    wr_hi = wr.astype(bf16)
    wr_parts = jnp.concatenate([wr_hi, (wr - wr_hi.astype(f32)).astype(bf16)], axis=1)
    br = jnp.pad(b_router[l], (0, LANES - N_EXPERTS))[None, :]
    tri = (jnp.arange(MIX_TM)[None, :] < jnp.arange(MIX_TM)[:, None]).astype(bf16)
    perms_t = [_residue_perm(MIX_TM, dil).T for _win, dil in DIL_PAIRS]
    expand = (jnp.arange(LANES)[:, None] == jnp.arange(ATT_WIDTH)[None, :] // HEAD_DIM).astype(bf16)
    expand = jnp.concatenate([expand, expand], axis=0)
    x1, h2, route, rw, cnt = _mix(
        os_, ls_, perms_t, expand, ya, gates, x2, g1, sc2, sh2,
        w_branch_a[l].astype(bf16), w_branch_b[l].astype(bf16), w_out[l].astype(bf16),
        ln1_g[l][None, :], ln1_b[l][None, :], wr_parts, br, tri, seq)

    top_e = route[:, :TOP_K]
    rank = route[:, TOP_K:2 * TOP_K]
    counts = cnt[0, :N_EXPERTS].astype(i32)
    pcounts = (counts + MOE_TM - 1) // MOE_TM * MOE_TM
    pends = jnp.cumsum(pcounts)
    pstarts = pends - pcounts
    experts = jnp.arange(N_EXPERTS, dtype=i32)
    dest = jnp.sum(jnp.where(top_e[:, :, None] == experts, pstarts, 0), axis=-1) + rank
    ntile = t * TOP_K // MOE_TM + N_EXPERTS
    n_used = (pends[-1] // MOE_TM).reshape(1)
    tile_idx = jnp.minimum(jnp.arange(ntile, dtype=i32), n_used - 1)
    tile_e = jnp.sum((pends[None, :] <= (tile_idx * MOE_TM)[:, None]).astype(i32), axis=1)
    tile_first = jnp.concatenate([jnp.ones((1,), i32), (tile_e[1:] != tile_e[:-1]).astype(i32)])
    nonempty = counts > 0
    later = lax.cummin(jnp.where(nonempty, experts, N_EXPERTS), reverse=True)
    next_nonempty = jnp.concatenate([later[1:], jnp.full((1,), N_EXPERTS, i32)])
    next_nonempty = jnp.where(next_nonempty >= N_EXPERTS, -1, next_nonempty)
    expert_slot = (jnp.cumsum(nonempty.astype(i32)) - 1) % 2

    dest3 = dest.reshape(t // DISP_TM, DISP_TM, TOP_K).transpose(0, 2, 1)
    xs = _dispatch(pends, pcounts, n_used, dest3, h2, ntile)
    yb = _moe(tile_e, tile_first, next_nonempty[tile_e], expert_slot[tile_e], n_used, xs,
              w_gate[l], b_gate[l][:, None, :], w_up[l], b_up[l][:, None, :],
              w_down[l], b_down[l][:, None, :])
    out = _combine(dest3, yb, rw, x1, g2, ln2_g[l][None, :], ln2_b[l][None, :], seq)
    return out.reshape(batch, seq, D_MODEL)
```

```python
import functools
import math

import numpy as np
import jax
import jax.numpy as jnp
from jax import lax
from jax.experimental import pallas as pl
from jax.experimental.pallas import tpu as pltpu

f32 = jnp.float32
bf16 = jnp.bfloat16
i32 = jnp.int32

D_MODEL = 1024
GM_WIDTH = 512
GM_GROUPS = 8
GM_CHUNK = 128
DIL_PAIRS = ((128, 1), (512, 4), (2048, 16))
N_DIL = 3
HEADS_PER_GROUP = 8
HEAD_DIM = 64
ATT_WIDTH = 512
ATT_BLOCK = 128
NEG_INF = -1e30
REL_BUCKETS = 32
REL_MAX_EXACT = 16
REL_MAX_DIST = 2048
N_EXPERTS = 32
TOP_K = 4
SWIGLU_LIMIT = 7.0
SWIGLU_ALPHA = 1.702
MOE_BLOCK = 128
DEPTH = 1
DN_ALPHA = (2 * DEPTH) ** 0.25
LN_EPS = 1e-5
UV_COLS = 2 * GM_WIDTH
QKV_COLS = N_DIL * 3 * ATT_WIDTH
GATE_COLS = 2 * D_MODEL
IN_COLS = UV_COLS + QKV_COLS + GATE_COLS

LANES = 128
SUBLANES = 8
VMEM_LIMIT = 56 * 1024 * 1024


def _ln(x):
    mu = jnp.mean(x, axis=-1, keepdims=True)
    xc = x - mu
    var = jnp.mean(xc * xc, axis=-1, keepdims=True)
    return xc * lax.rsqrt(var + LN_EPS)


def _params(sem, vmem=VMEM_LIMIT):
    return pltpu.CompilerParams(dimension_semantics=sem, vmem_limit_bytes=vmem)


def _adaln_kernel(c_ref, w_ref, b_ref, o_ref):
    c = c_ref[...]
    s = c * jax.nn.sigmoid(c)
    o_ref[...] = jnp.dot(s, w_ref[...], preferred_element_type=f32,
                         precision=lax.Precision.HIGHEST) + b_ref[...]


def _adaln(c8, w_ada, b_ada):
    n = w_ada.shape[1] // D_MODEL
    return pl.pallas_call(
        _adaln_kernel,
        grid=(n,),
        in_specs=[pl.BlockSpec((8, D_MODEL), lambda j: (0, 0)),
                  pl.BlockSpec((D_MODEL, D_MODEL), lambda j: (0, j)),
                  pl.BlockSpec((1, D_MODEL), lambda j: (0, j))],
        out_specs=pl.BlockSpec((8, D_MODEL), lambda j: (0, j)),
        out_shape=jax.ShapeDtypeStruct((8, w_ada.shape[1]), f32),
        compiler_params=_params(("arbitrary",)),
        name="adaln",
    )(c8, w_ada, b_ada)


IN_TM = 256
IN_CW = 512
GRP_COLS = 3 * ATT_WIDTH


def _inproj_kernel(x_ref, sc_ref, sh_ref, w_ref, p1_ref, p2_ref,
                   uv_ref, gt_ref, qkv0_ref, qkv1_ref, qkv2_ref):
    xn = _ln(x_ref[...])
    h = (xn * (1.0 + sc_ref[0]) + sh_ref[0]).astype(bf16)
    hp = [h,
          jnp.dot(p1_ref[...], h, preferred_element_type=f32).astype(bf16),
          jnp.dot(p2_ref[...], h, preferred_element_type=f32).astype(bf16)]
    for c0 in range(0, UV_COLS, IN_CW):
        acc = jnp.dot(h, w_ref[:, c0:c0 + IN_CW], preferred_element_type=f32)
        uv_ref[:, c0:c0 + IN_CW] = jax.nn.gelu(acc).astype(bf16)
    for g, (qref, (_win, dil)) in enumerate(zip((qkv0_ref, qkv1_ref, qkv2_ref), DIL_PAIRS)):
        n = IN_TM // dil
        for q0 in range(0, GRP_COLS, IN_CW):
            c0 = UV_COLS + g * GRP_COLS + q0
            acc = jnp.dot(hp[g], w_ref[:, c0:c0 + IN_CW], preferred_element_type=f32).astype(bf16)
            for rho in range(dil):
                qref[0, rho, :, q0:q0 + IN_CW] = acc[rho * n:(rho + 1) * n, :]
    for g0 in range(0, GATE_COLS, IN_CW):
        c0 = UV_COLS + QKV_COLS + g0
        acc = jnp.dot(h, w_ref[:, c0:c0 + IN_CW], preferred_element_type=f32)
        gt_ref[:, g0:g0 + IN_CW] = jax.nn.sigmoid(acc).astype(bf16)


def _inproj(x2, sc1, sh1, w_in_bf, perms, batch, seq):
    t = x2.shape[0]
    per_b = seq // IN_TM
    qkv_specs, qkv_shapes = [], []
    for _win, dil in DIL_PAIRS:
        n = IN_TM // dil
        qkv_specs.append(pl.BlockSpec((1, dil, n, GRP_COLS), lambda i: (i // per_b, 0, i % per_b, 0)))
        qkv_shapes.append(jax.ShapeDtypeStruct((batch, dil, seq // dil, GRP_COLS), bf16))
    return pl.pallas_call(
        _inproj_kernel,
        grid=(t // IN_TM,),
        in_specs=[pl.BlockSpec((IN_TM, D_MODEL), lambda i: (i, 0)),
                  pl.BlockSpec((1, 1, D_MODEL), lambda i: (i // per_b, 0, 0)),
                  pl.BlockSpec((1, 1, D_MODEL), lambda i: (i // per_b, 0, 0)),
                  pl.BlockSpec((D_MODEL, IN_COLS), lambda i: (0, 0)),
                  pl.BlockSpec((IN_TM, IN_TM), lambda i: (0, 0)),
                  pl.BlockSpec((IN_TM, IN_TM), lambda i: (0, 0))],
        out_specs=[pl.BlockSpec((IN_TM, UV_COLS), lambda i: (i, 0)),
                   pl.BlockSpec((IN_TM, GATE_COLS), lambda i: (i, 0))] + qkv_specs,
        out_shape=[jax.ShapeDtypeStruct((t, UV_COLS), bf16),
                   jax.ShapeDtypeStruct((t, GATE_COLS), bf16)] + qkv_shapes,
        compiler_params=_params(("arbitrary",)),
        name="inproj",
    )(x2, sc1, sh1, w_in_bf, perms[1], perms[2])


GM_TM = 512


def _gmlp_kernel(u_ref, v_ref, g_ref, b_ref, ws_ref, bs_ref, ya_ref):
    row = lax.broadcasted_iota(i32, (GM_CHUNK, GM_CHUNK), 0)
    col = lax.broadcasted_iota(i32, (GM_CHUNK, GM_CHUNK), 1)
    causal = col <= row
    first_half = lax.broadcasted_iota(i32, (GM_CHUNK, LANES), 1) < (GM_WIDTH // GM_GROUPS)
    ws = [jnp.where(causal, ws_ref[g], 0.0).astype(bf16) for g in range(GM_GROUPS)]
    for ch in range(GM_TM // GM_CHUNK):
        r0 = ch * GM_CHUNK
        vn = _ln(v_ref[r0:r0 + GM_CHUNK, :].astype(f32)) * g_ref[...] + b_ref[...]
        vn = vn.astype(bf16)
        for j in range(GM_WIDTH // LANES):
            slab = vn[:, j * LANES:(j + 1) * LANES]
            s_lo = jnp.dot(ws[2 * j], slab, preferred_element_type=f32)
            s_hi = jnp.dot(ws[2 * j + 1], slab, preferred_element_type=f32)
            s = jnp.where(first_half, s_lo, s_hi) + bs_ref[:, j * LANES:(j + 1) * LANES]
            u = u_ref[r0:r0 + GM_CHUNK, j * LANES:(j + 1) * LANES].astype(f32)
            ya_ref[r0:r0 + GM_CHUNK, j * LANES:(j + 1) * LANES] = (u * s).astype(bf16)


def _gmlp(uv, ln_g, ln_b, w_s, bs_full):
    t = uv.shape[0]
    return pl.pallas_call(
        _gmlp_kernel,
        grid=(t // GM_TM,),
        in_specs=[pl.BlockSpec((GM_TM, GM_WIDTH), lambda i: (i, 0)),
                  pl.BlockSpec((GM_TM, GM_WIDTH), lambda i: (i, 1)),
                  pl.BlockSpec((1, GM_WIDTH), lambda i: (0, 0)),
                  pl.BlockSpec((1, GM_WIDTH), lambda i: (0, 0)),
                  pl.BlockSpec((GM_GROUPS, GM_CHUNK, GM_CHUNK), lambda i: (0, 0, 0)),
                  pl.BlockSpec((GM_CHUNK, GM_WIDTH), lambda i: (0, 0))],
        out_specs=pl.BlockSpec((GM_TM, GM_WIDTH), lambda i: (i, 0)),
        out_shape=jax.ShapeDtypeStruct((t, GM_WIDTH), bf16),
        compiler_params=_params(("arbitrary",)),
        name="gmlp",
    )(uv, uv, ln_g, ln_b, w_s, bs_full)


def _relbias_kernel(tab_ref, bucket_ref, band_ref, out_ref):
    g = pl.program_id(0)
    bk = bucket_ref[0]
    band = band_ref[0] > 0
    for h in range(HEADS_PER_GROUP):
        acc = jnp.zeros((ATT_BLOCK, 2 * ATT_BLOCK), f32)
        for b in range(REL_BUCKETS):
            acc = jnp.where(bk == b, tab_ref[b, g * HEADS_PER_GROUP + h], acc)
        out_ref[0, h] = jnp.where(band, acc, NEG_INF)


def _relbias(rel_bias, bucket, band):
    return pl.pallas_call(
        _relbias_kernel,
        grid=(N_DIL,),
        in_specs=[pl.BlockSpec(memory_space=pltpu.SMEM),
                  pl.BlockSpec((1, ATT_BLOCK, 2 * ATT_BLOCK), lambda g: (g, 0, 0)),
                  pl.BlockSpec((1, ATT_BLOCK, 2 * ATT_BLOCK), lambda g: (g, 0, 0))],
        out_specs=pl.BlockSpec((1, HEADS_PER_GROUP, ATT_BLOCK, 2 * ATT_BLOCK),
                               lambda g: (g, 0, 0, 0)),
        out_shape=jax.ShapeDtypeStruct((N_DIL, HEADS_PER_GROUP, ATT_BLOCK, 2 * ATT_BLOCK), f32),
        compiler_params=_params(("arbitrary",)),
        name="relbias",
    )(rel_bias, bucket, band)


ATT_MAX_STEP_BLOCKS = 4


def _attn_kernel(nblk, q_ref, kp_ref, kc_ref, vp_ref, vc_ref, bias_ref, o_ref, lse_ref):
    first = pl.program_id(2) == 0
    lane = lax.broadcasted_iota(i32, (ATT_BLOCK, LANES), 1)
    lo_half = lane < HEAD_DIM
    nt = (((1,), (1,)), ((), ()))
    ones = jnp.ones((2 * ATT_BLOCK, LANES), bf16)
    n_slab = ATT_WIDTH // LANES
    logits, v_ext = [], []
    for i in range(nblk):
        cur = slice(i * ATT_BLOCK, (i + 1) * ATT_BLOCK)
        prv = slice((i - 1) * ATT_BLOCK, i * ATT_BLOCK)
        for j in range(n_slab):
            sl = slice(j * LANES, (j + 1) * LANES)
            q = q_ref[0, 0, cur, sl] * (HEAD_DIM ** -0.5)
            k_prev = kp_ref[0, 0, :, sl] if i == 0 else kc_ref[0, 0, prv, sl]
            v_prev = vp_ref[0, 0, :, sl] if i == 0 else vc_ref[0, 0, prv, sl]
            k_cat = jnp.concatenate([k_prev, kc_ref[0, 0, cur, sl]], axis=0)
            v_cat = jnp.concatenate([v_prev, vc_ref[0, 0, cur, sl]], axis=0)
            v_ext.append(jnp.concatenate([v_cat, ones], axis=1))
            for hh in range(2):
                qm = jnp.where(lo_half if hh == 0 else jnp.logical_not(lo_half), q, 0.0).astype(bf16)
                logits.append(lax.dot_general(qm, k_cat, nt, preferred_element_type=f32))
    bias = bias_ref[0].reshape(HEADS_PER_GROUP * ATT_BLOCK, 2 * ATT_BLOCK)
    rows_per_block = HEADS_PER_GROUP * ATT_BLOCK
    lg = jnp.concatenate(logits, axis=0) + jnp.concatenate([bias] * nblk, axis=0)
    row = lax.broadcasted_iota(i32, lg.shape, 0)
    col = lax.broadcasted_iota(i32, lg.shape, 1)
    no_prev = jnp.logical_and(first, jnp.logical_and(row < rows_per_block, col < ATT_BLOCK))
    lg = jnp.where(no_prev, NEG_INF, lg)
    m = jnp.max(lg, axis=-1, keepdims=True)
    p = jnp.exp(lg - m).astype(bf16)
    for i in range(nblk):
        cur = slice(i * ATT_BLOCK, (i + 1) * ATT_BLOCK)
        lse_tile = jnp.zeros((ATT_BLOCK, LANES), f32)
        for j in range(n_slab):
            outs = []
            for hh in range(2):
                h = 2 * j + hh
                r0 = i * rows_per_block + h * ATT_BLOCK
                r = jnp.dot(p[r0:r0 + ATT_BLOCK], v_ext[i * n_slab + j], preferred_element_type=f32)
                den = r[:, LANES:]
                outs.append(r[:, :LANES] * (1.0 / den))
                lse_h = m[r0:r0 + ATT_BLOCK] + jnp.log(den)
                lse_tile = jnp.where(lane == h, lse_h, lse_tile)
            o_ref[0, 0, cur, j * LANES:(j + 1) * LANES] = jnp.where(lo_half, outs[0], outs[1]).astype(bf16)
        lse_ref[0, 0, cur, :] = lse_tile


def _attn_group(qkv_g, bias, g, dil, batch, seq):
    l = seq // dil
    nblk = min(ATT_MAX_STEP_BLOCKS, l // ATT_BLOCK)
    tm = nblk * ATT_BLOCK
    nsteps = l // tm

    def cur(cb):
        return pl.BlockSpec((1, 1, tm, ATT_WIDTH), lambda b, r, n: (b, r, n, cb))

    def prev(cb):
        return pl.BlockSpec((1, 1, ATT_BLOCK, ATT_WIDTH),
                            lambda b, r, n: (b, r, jnp.maximum(n * nblk - 1, 0), cb))

    return pl.pallas_call(
        functools.partial(_attn_kernel, nblk),
        grid=(batch, dil, nsteps),
        in_specs=[cur(0), prev(1), cur(1), prev(2), cur(2),
                  pl.BlockSpec((1, HEADS_PER_GROUP, ATT_BLOCK, 2 * ATT_BLOCK),
                               lambda b, r, n: (g, 0, 0, 0))],
        out_specs=[pl.BlockSpec((1, 1, tm, ATT_WIDTH), lambda b, r, n: (b, r, n, 0)),
                   pl.BlockSpec((1, 1, tm, LANES), lambda b, r, n: (b, r, n, 0))],
        out_shape=[jax.ShapeDtypeStruct((batch, dil, l, ATT_WIDTH), bf16),
                   jax.ShapeDtypeStruct((batch, dil, l, LANES), f32)],
        compiler_params=_params(("arbitrary", "arbitrary", "arbitrary")),
        name=f"attn_g{g}",
    )(qkv_g, qkv_g, qkv_g, qkv_g, qkv_g, bias)


ROW_WORDS = D_MODEL // 2
ROW_SUB = ROW_WORDS // LANES
HI_MASK = -65536


def _pack_rows(x):
    bits = lax.bitcast_convert_type(x.astype(bf16).astype(f32), i32)
    return lax.shift_right_logical(bits[:, :ROW_WORDS], 16) | (bits[:, ROW_WORDS:] & HI_MASK)


def _unpack_rows(words):
    lo = lax.bitcast_convert_type(lax.shift_left(words, 16), f32)
    hi = lax.bitcast_convert_type(words & HI_MASK, f32)
    return jnp.concatenate([lo, hi], axis=1)


def _store_packed(ref, words, n, first_row=0):
    for r in range(ROW_SUB):
        ref[pl.ds(first_row * ROW_SUB + r, n, stride=ROW_SUB), :] = words[:, r * LANES:(r + 1) * LANES]


def _load_packed(ref, first_row, n):
    return jnp.concatenate([ref[pl.ds(first_row * ROW_SUB + r, n, stride=ROW_SUB), :] for r in range(ROW_SUB)],
                           axis=1)


MIX_TM = 256
MIX_SUB = 128


def _split_bf16(x, parts):
    out = []
    for _ in range(parts):
        hi = x.astype(bf16)
        out.append(hi)
        x = x - hi.astype(f32)
    return out


def _mix_kernel(o0_ref, o1_ref, o2_ref, l0_ref, l1_ref, l2_ref, pt1_ref, pt2_ref, ex_ref,
                ya_ref, gt_ref, x_ref,
                g1_ref, sc2_ref, sh2_ref, wa_ref, wb_ref, wo_ref, ln1g_ref, ln1b_ref,
                wrc_ref, br_ref, tri_ref,
                x1_ref, h2_ref, route_ref, rw_ref, cnt_ref, run_ref, xr_ref):
    step = pl.program_id(0)

    @pl.when(step == 0)
    def _():
        run_ref[...] = jnp.zeros_like(run_ref)
        xr_ref[...] = jnp.zeros_like(xr_ref)

    def back_rows(r0):
        rows = slice(r0, r0 + MIX_SUB)
        x1 = _ln(xr_ref[rows, :]) * ln1g_ref[...] + ln1b_ref[...]
        x1_ref[rows, :] = x1
        h2 = _ln(x1) * (1.0 + sc2_ref[0]) + sh2_ref[0]
        _store_packed(h2_ref, _pack_rows(h2), MIX_SUB, r0)
        h_hi, h_lo = _split_bf16(h2, 2)
        hi_both = jnp.dot(h_hi, wrc_ref[...], preferred_element_type=f32)
        return (hi_both[:, :LANES]
                + (hi_both[:, LANES:] + jnp.dot(h_lo, wrc_ref[:, :LANES], preferred_element_type=f32))
                ) + br_ref[...]

    def front_rows(r0):
        rows = slice(r0, r0 + MIX_SUB)
        os_, ls_ = [o0_ref[0, 0, rows, :].astype(f32)], [l0_ref[0, 0, rows, :]]
        for o_ref, l_ref, pt_ref in ((o1_ref, l1_ref, pt1_ref), (o2_ref, l2_ref, pt2_ref)):
            pt = pt_ref[rows, :]
            os_.append(jnp.dot(pt, o_ref[0].reshape(MIX_TM, ATT_WIDTH), preferred_element_type=f32))
            parts = [jnp.dot(pt, part, preferred_element_type=f32)
                     for part in _split_bf16(l_ref[0].reshape(MIX_TM, LANES), 3)]
            ls_.append((parts[0] + parts[1]) + parts[2])
        lm = jnp.maximum(jnp.maximum(ls_[0], ls_[1]), ls_[2])
        es = [jnp.exp(lse - lm) for lse in ls_]
        inv = 1.0 / (es[0] + es[1] + es[2])
        yb = jnp.zeros((MIX_SUB, ATT_WIDTH), f32)
        for e, o in zip(es, os_):
            w_parts = jnp.concatenate(_split_bf16(e * inv, 2), axis=1)
            yb = yb + jnp.dot(w_parts, ex_ref[...], preferred_element_type=f32) * o
        a = jnp.dot(ya_ref[rows, :], wa_ref[...], preferred_element_type=f32)
        b = jnp.dot(yb.astype(bf16), wb_ref[...], preferred_element_type=f32)
        merged = gt_ref[rows, :D_MODEL].astype(f32) * a + gt_ref[rows, D_MODEL:].astype(f32) * b
        mix = jnp.dot(merged.astype(bf16), wo_ref[...], preferred_element_type=f32)
        xr_ref[rows, :] = DN_ALPHA * x_ref[rows, :] + g1_ref[0] * mix

    logit_parts = []
    for r0 in range(0, MIX_TM, MIX_SUB):
        logit_parts.append(back_rows(r0))
        front_rows(r0)
    logits = jnp.concatenate(logit_parts, axis=0)
    lane = lax.broadcasted_iota(i32, (MIX_TM, LANES), 1)
    logits = jnp.where(lane < N_EXPERTS, logits, -jnp.inf)
    lane_f = lane.astype(f32)
    vals, idxs = [], []
    for _k in range(TOP_K):
        m = jnp.max(logits, axis=-1, keepdims=True)
        vals.append(m)
        idxs.append(jnp.min(jnp.where(logits == m, lane_f, float(LANES)), axis=-1, keepdims=True).astype(i32))
        logits = jnp.where(lane == idxs[-1], -jnp.inf, logits)
    exps = [jnp.exp(v - vals[0]) for v in vals]
    den = exps[0] + exps[1] + exps[2] + exps[3]
    wts = [e / den for e in exps]
    hits = [lane == idx for idx in idxs]
    counted = jnp.where(step > 0, 1.0, 0.0)
    onehot = jnp.zeros((MIX_TM, LANES), f32)
    for hit in hits:
        onehot = onehot + jnp.where(hit, counted, 0.0)
    prefix = jnp.dot(tri_ref[...], onehot.astype(bf16), preferred_element_type=f32) + run_ref[...]
    route = jnp.zeros((MIX_TM, LANES), i32)
    rw = jnp.zeros((MIX_TM, LANES), f32)
    for k in range(TOP_K):
        rank = jnp.sum(jnp.where(hits[k], prefix, 0.0), axis=-1, keepdims=True).astype(i32)
        route = jnp.where(lane == k, idxs[k], route)
        route = jnp.where(lane == TOP_K + k, rank, route)
        rw = jnp.where(lane == k, wts[k], rw)
    route_ref[...] = route
    rw_ref[...] = rw
    run = run_ref[...] + jnp.sum(onehot, axis=0, keepdims=True)
    run_ref[...] = run
    cnt_ref[...] = jnp.broadcast_to(run, cnt_ref.shape)


def _mix(os_, ls_, perms_t, expand, ya, gates, x2, g1, sc2, sh2, wa, wb, wo, ln1g, ln1b, wr_parts, br, tri, seq):
    t = x2.shape[0]
    nb = t // MIX_TM
    per_b = seq // MIX_TM
    cur = lambda i: jnp.minimum(i, nb - 1)
    prv = lambda i: jnp.maximum(i - 1, 0)
    row = lambda w: pl.BlockSpec((MIX_TM, w), lambda i: (cur(i), 0))
    out_row = lambda w: pl.BlockSpec((MIX_TM, w), lambda i: (prv(i), 0))
    const = lambda s: pl.BlockSpec(s, lambda i: tuple(0 for _ in s))
    mod_cur = pl.BlockSpec((1, 1, D_MODEL), lambda i: (cur(i) // per_b, 0, 0))
    mod_prv = pl.BlockSpec((1, 1, D_MODEL), lambda i: (prv(i) // per_b, 0, 0))
    grp = lambda w: [pl.BlockSpec((1, dil, MIX_TM // dil, w), lambda i: (cur(i) // per_b, 0, cur(i) % per_b, 0))
                     for _win, dil in DIL_PAIRS]
    return pl.pallas_call(
        _mix_kernel,
        grid=(nb + 1,),
        in_specs=grp(ATT_WIDTH) + grp(LANES) + [
                  const((MIX_TM, MIX_TM)), const((MIX_TM, MIX_TM)), const((2 * LANES, ATT_WIDTH)),
                  row(GM_WIDTH), row(GATE_COLS), row(D_MODEL),
                  mod_cur, mod_prv, mod_prv,
                  const((GM_WIDTH, D_MODEL)), const((ATT_WIDTH, D_MODEL)), const((D_MODEL, D_MODEL)),
                  const((1, D_MODEL)), const((1, D_MODEL)),
                  const((D_MODEL, 2 * LANES)), const((1, LANES)), const((MIX_TM, MIX_TM))],
        out_specs=[out_row(D_MODEL), pl.BlockSpec((MIX_TM * ROW_SUB, LANES), lambda i: (prv(i), 0)),
                   out_row(LANES), out_row(LANES), const((8, LANES))],
        out_shape=[jax.ShapeDtypeStruct((t, D_MODEL), f32),
                   jax.ShapeDtypeStruct((t * ROW_SUB, LANES), i32),
                   jax.ShapeDtypeStruct((t, LANES), i32),
                   jax.ShapeDtypeStruct((t, LANES), f32),
                   jax.ShapeDtypeStruct((8, LANES), f32)],
        scratch_shapes=[pltpu.VMEM((1, LANES), f32), pltpu.VMEM((MIX_TM, D_MODEL), f32)],
        compiler_params=_params(("arbitrary",)),
        name="mix",
    )(*os_, *ls_, perms_t[1], perms_t[2], expand, ya, gates, x2, g1, sc2, sh2, wa, wb, wo,
      ln1g, ln1b, wr_parts, br, tri)


DISP_TM = 512
MOE_TM = 512


def _dispatch_kernel(pends_ref, pcnt_ref, nused_ref, dest_ref, h2p_ref, xs_hbm, zbuf, sem, zsem):
    i = pl.program_id(0)
    ntile = xs_hbm.shape[0] // (MOE_TM * ROW_SUB)

    def zero_tile(first_row):
        return pltpu.make_async_copy(
            zbuf, xs_hbm.at[pl.ds(pl.multiple_of(first_row * ROW_SUB, MOE_TM * ROW_SUB), MOE_TM * ROW_SUB)], zsem)

    def for_each_zero_tile(fn):
        for e in range(N_EXPERTS):
            pl.when(pcnt_ref[e] > 0)(functools.partial(fn, lambda e=e: zero_tile(pends_ref[e] - MOE_TM)))
        for k in range(N_EXPERTS):
            tile = nused_ref[0] + k
            pl.when(tile < ntile)(functools.partial(fn, lambda tile=tile: zero_tile(tile * MOE_TM)))

    @pl.when(i == 0)
    def _():
        zbuf[...] = jnp.zeros_like(zbuf)
        for_each_zero_tile(lambda mk: mk().start())
        for_each_zero_tile(lambda mk: mk().wait())

    def row_copy(k, r):
        d = dest_ref[0, k, r]
        return pltpu.make_async_copy(h2p_ref.at[pl.ds(r * ROW_SUB, ROW_SUB)],
                                     xs_hbm.at[pl.ds(pl.multiple_of(d * ROW_SUB, ROW_SUB), ROW_SUB)], sem)

    for r in range(DISP_TM):
        for k in range(TOP_K):
            row_copy(k, r).start(priority=k % 2)
    for k in range(TOP_K):
        pltpu.make_async_copy(h2p_ref, xs_hbm.at[pl.ds(0, DISP_TM * ROW_SUB)], sem).wait()


def _dispatch(pends, pcounts, n_used, dest3, h2p, ntile):
    t = h2p.shape[0] // ROW_SUB
    grid_spec = pltpu.PrefetchScalarGridSpec(
        num_scalar_prefetch=3,
        grid=(t // DISP_TM,),
        in_specs=[pl.BlockSpec((1, TOP_K, DISP_TM), lambda i, *_: (i, 0, 0), memory_space=pltpu.SMEM),
                  pl.BlockSpec((DISP_TM * ROW_SUB, LANES), lambda i, *_: (i, 0))],
        out_specs=pl.BlockSpec(memory_space=pl.ANY),
        scratch_shapes=[pltpu.VMEM((MOE_TM * ROW_SUB, LANES), i32),
                        pltpu.SemaphoreType.DMA(()),
                        pltpu.SemaphoreType.DMA(())],
    )
    return pl.pallas_call(
        _dispatch_kernel,
        grid_spec=grid_spec,
        out_shape=jax.ShapeDtypeStruct((ntile * MOE_TM * ROW_SUB, LANES), i32),
        compiler_params=_params(("arbitrary",)),
        name="dispatch",
    )(pends, pcounts, n_used, dest3, h2p)


def _moe_kernel(te_ref, first_ref, nexte_ref, wslot_ref, nused_ref,
                xs_ref, wg_hbm, wu_hbm, wd_hbm, bg_ref, bu_ref, bd_ref,
                out_ref, wbuf, wgb, wub, wdb, sem_w):
    j = pl.program_id(0)

    def weight_copies(e, ws):
        return [pltpu.make_async_copy(w.at[e], wbuf.at[ws, k], sem_w.at[ws])
                for k, w in enumerate((wg_hbm, wu_hbm, wd_hbm))]

    @pl.when(j == 0)
    def _():
        for cp in weight_copies(te_ref[0], wslot_ref[0]):
            cp.start()

    @pl.when(first_ref[j] == 1)
    def _():
        ws = wslot_ref[j]
        for cp in weight_copies(te_ref[j], ws):
            cp.wait()
        wgb[...] = wbuf[ws, 0].astype(bf16)
        wub[...] = wbuf[ws, 1].astype(bf16)
        wdb[...] = wbuf[ws, 2].astype(bf16)
        ne = nexte_ref[j]

        @pl.when(ne >= 0)
        def _():
            for cp in weight_copies(ne, 1 - ws):
                cp.start()

    used = j < nused_ref[0]

    @pl.when(used)
    def _():
        xb = _unpack_rows(_load_packed(xs_ref, 0, MOE_TM)).astype(bf16)
        g = jnp.dot(xb, wgb[...], preferred_element_type=f32) + bg_ref[0]
        u = jnp.dot(xb, wub[...], preferred_element_type=f32) + bu_ref[0]
        g = jnp.minimum(g, SWIGLU_LIMIT)
        u = jnp.clip(u, -SWIGLU_LIMIT, SWIGLU_LIMIT)
        act = (u + 1.0) * (g * jax.nn.sigmoid(SWIGLU_ALPHA * g))
        y = jnp.dot(act.astype(bf16), wdb[...], preferred_element_type=f32) + bd_ref[0]
        _store_packed(out_ref, _pack_rows(y), MOE_TM)

    @pl.when(jnp.logical_not(used))
    def _():
        out_ref[...] = jnp.zeros_like(out_ref)


def _moe(tile_e, tile_first, next_e, wslot, n_used, xs, w_gate, b_gate, w_up, b_up, w_down, b_down):
    ntile = tile_e.shape[0]
    bspec = pl.BlockSpec((1, 1, D_MODEL), lambda j, te, *_: (te[j], 0, 0))
    hbm = pl.BlockSpec(memory_space=pl.ANY)
    grid_spec = pltpu.PrefetchScalarGridSpec(
        num_scalar_prefetch=5,
        grid=(ntile,),
        in_specs=[pl.BlockSpec((MOE_TM * ROW_SUB, LANES),
                               lambda j, te, fi, ne, ws, nu: (jnp.minimum(j, nu[0] - 1), 0)),
                  hbm, hbm, hbm, bspec, bspec, bspec],
        out_specs=pl.BlockSpec((MOE_TM * ROW_SUB, LANES), lambda j, *_: (j, 0)),
        scratch_shapes=[pltpu.VMEM((2, 3, D_MODEL, D_MODEL), f32),
                        pltpu.VMEM((D_MODEL, D_MODEL), bf16),
                        pltpu.VMEM((D_MODEL, D_MODEL), bf16),
                        pltpu.VMEM((D_MODEL, D_MODEL), bf16),
                        pltpu.SemaphoreType.DMA((2,))],
    )
    return pl.pallas_call(
        _moe_kernel,
        grid_spec=grid_spec,
        out_shape=jax.ShapeDtypeStruct((ntile * MOE_TM * ROW_SUB, LANES), i32),
        compiler_params=_params(("arbitrary",)),
        name="moe",
    )(tile_e, tile_first, next_e, wslot, n_used, xs, w_gate, w_up, w_down, b_gate, b_up, b_down)


CB_TM = DISP_TM


def _combine_kernel(dcur_ref, dnxt_ref, yb_hbm, rw_ref, x1_ref, g2_ref, lng_ref, lnb_ref, out_ref,
                    ybuf0, ybuf1, sem):
    i = pl.program_id(0)
    last = pl.num_programs(0) - 1
    ybufs = (ybuf0, ybuf1)

    def row_copy(d, k, r, s):
        return pltpu.make_async_copy(
            yb_hbm.at[pl.ds(pl.multiple_of(d * ROW_SUB, ROW_SUB), ROW_SUB)],
            ybufs[s].at[pl.ds(pl.multiple_of((k * CB_TM + r) * ROW_SUB, ROW_SUB), ROW_SUB)],
            sem.at[s])

    @pl.when(i == 0)
    def _():
        for k in range(TOP_K):
            def body(r, c, k=k):
                row_copy(dcur_ref[0, k, r], k, r, 0).start()
                return c
            lax.fori_loop(0, CB_TM, body, 0, unroll=8)

    for s in range(2):
        @pl.when(i % 2 == s)
        def _(s=s):
            pltpu.make_async_copy(yb_hbm.at[pl.ds(0, TOP_K * CB_TM * ROW_SUB)], ybufs[s], sem.at[s]).wait()

            @pl.when(i < last)
            def _():
                for k in range(TOP_K):
                    for r in range(CB_TM):
                        row_copy(dnxt_ref[0, k, r], k, r, 1 - s).start(priority=r % 2)

            parts = [_unpack_rows(_load_packed(ybufs[s], k * CB_TM, CB_TM)) * rw_ref[:, k:k + 1]
                     for k in range(TOP_K)]
            y = (parts[0] + parts[1]) + (parts[2] + parts[3])
            out_ref[...] = _ln(DN_ALPHA * x1_ref[...] + g2_ref[0] * y) * lng_ref[...] + lnb_ref[...]


def _combine(dest3, yb, rw, x1, g2, ln2g, ln2b, seq):
    t = x1.shape[0]
    nb = t // CB_TM
    per_b = seq // CB_TM
    return pl.pallas_call(
        _combine_kernel,
        grid=(nb,),
        in_specs=[pl.BlockSpec((1, TOP_K, CB_TM), lambda i: (i, 0, 0), memory_space=pltpu.SMEM),
                  pl.BlockSpec((1, TOP_K, CB_TM), lambda i: (jnp.minimum(i + 1, nb - 1), 0, 0),
                               memory_space=pltpu.SMEM),
                  pl.BlockSpec(memory_space=pl.ANY),
                  pl.BlockSpec((CB_TM, LANES), lambda i: (i, 0)),
                  pl.BlockSpec((CB_TM, D_MODEL), lambda i: (i, 0)),
                  pl.BlockSpec((1, 1, D_MODEL), lambda i: (i // per_b, 0, 0)),
                  pl.BlockSpec((1, D_MODEL), lambda i: (0, 0)),
                  pl.BlockSpec((1, D_MODEL), lambda i: (0, 0))],
        out_specs=pl.BlockSpec((CB_TM, D_MODEL), lambda i: (i, 0)),
        out_shape=jax.ShapeDtypeStruct((t, D_MODEL), f32),
        scratch_shapes=[pltpu.VMEM((TOP_K * CB_TM * ROW_SUB, LANES), i32),
                        pltpu.VMEM((TOP_K * CB_TM * ROW_SUB, LANES), i32),
                        pltpu.SemaphoreType.DMA((2,))],
        compiler_params=_params(("arbitrary",)),
        name="combine",
    )(dest3, dest3, yb, rw, x1, g2, ln2g, ln2b)


def _t5_bucket(dist):
    d = dist.astype(f32)
    large = REL_MAX_EXACT + jnp.log(jnp.maximum(d, float(REL_MAX_EXACT)) / REL_MAX_EXACT) / math.log(
        REL_MAX_DIST / REL_MAX_EXACT) * (REL_BUCKETS - REL_MAX_EXACT)
    large = jnp.minimum(large.astype(i32), REL_BUCKETS - 1)
    return jnp.where(dist < REL_MAX_EXACT, dist, large)


def _bias_indices():
    qi = jnp.arange(ATT_BLOCK)[:, None]
    ki = jnp.arange(2 * ATT_BLOCK)[None, :]
    didx = qi + ATT_BLOCK - ki
    buckets, bands = [], []
    for win, dil in DIL_PAIRS:
        buckets.append(_t5_bucket(jnp.clip(didx, 0, None) * dil))
        bands.append(((didx >= 0) & (didx <= win // dil)).astype(i32))
    return jnp.stack(buckets).astype(i32), jnp.stack(bands)


def _residue_perm(tm, dil):
    n = tm // dil
    dst = np.arange(tm)
    src = (dst % n) * dil + dst // n
    return src[:, None] == np.arange(tm)[None, :]


def kernel(x, c, w_ada, b_ada, w_in, gm_ln_g, gm_ln_b, gm_w_s, gm_b_s, w_branch_a, w_branch_b, w_out,
           rel_bias, ln1_g, ln1_b, w_router, b_router, w_gate, b_gate, w_up, b_up, w_down, b_down,
           ln2_g, ln2_b):
    batch, seq, _ = x.shape
    t = batch * seq
    l = 0
    x2 = x.reshape(t, D_MODEL)

    c8 = jnp.pad(c, ((0, 8 - batch), (0, 0)))
    mod = _adaln(c8, w_ada[l], b_ada[l][None, :])[:batch]
    sh1, sc1, g1, sh2, sc2, g2 = [m[:, None, :] for m in jnp.split(mod, 6, axis=-1)]

    perms = [jnp.asarray(_residue_perm(IN_TM, dil), bf16) for _win, dil in DIL_PAIRS]
    uv, gates, *qkvs = _inproj(x2, sc1, sh1, w_in[l].astype(bf16), perms, batch, seq)

    bs_full = jnp.repeat(gm_b_s[l].T, GM_WIDTH // GM_GROUPS, axis=1)
    ya = _gmlp(uv, gm_ln_g[l][None, :], gm_ln_b[l][None, :], gm_w_s[l], bs_full)

    bucket, band = _bias_indices()
    bias = _relbias(rel_bias, bucket, band)
    os_, ls_ = [], []
    for g, (_win, dil) in enumerate(DIL_PAIRS):
        o, lse = _attn_group(qkvs[g], bias, g, dil, batch, seq)
        os_.append(o)
        ls_.append(lse)

    wr = jnp.pad(w_router[l], ((0, 0), (0, LANES - N_EXPERTS)))
    wr_hi = wr.astype(bf16)
    wr_parts = jnp.concatenate([wr_hi, (wr - wr_hi.astype(f32)).astype(bf16)], axis=1)
    br = jnp.pad(b_router[l], (0, LANES - N_EXPERTS))[None, :]
    tri = jnp.asarray(np.arange(MIX_TM)[None, :] < np.arange(MIX_TM)[:, None], bf16)
    perms_t = [jnp.asarray(_residue_perm(MIX_TM, dil).T, bf16) for _win, dil in DIL_PAIRS]
    expand = np.arange(LANES)[:, None] == np.arange(ATT_WIDTH)[None, :] // HEAD_DIM
    expand = jnp.asarray(np.concatenate([expand, expand], axis=0), bf16)
    x1, h2, route, rw, cnt = _mix(
        os_, ls_, perms_t, expand, ya, gates, x2, g1, sc2, sh2,
        w_branch_a[l].astype(bf16), w_branch_b[l].astype(bf16), w_out[l].astype(bf16),
        ln1_g[l][None, :], ln1_b[l][None, :], wr_parts, br, tri, seq)

    top_e = route[:, :TOP_K]
    rank = route[:, TOP_K:2 * TOP_K]
    counts = cnt[0, :N_EXPERTS].astype(i32)
    pcounts = (counts + MOE_TM - 1) // MOE_TM * MOE_TM
    experts = jnp.arange(N_EXPERTS, dtype=i32)
    upto = experts[None, :] <= experts[:, None]
    pends = jnp.sum(jnp.where(upto, pcounts[None, :], 0), axis=1)
    pstarts = pends - pcounts
    dest = jnp.sum(jnp.where(top_e[:, :, None] == experts, pstarts, 0), axis=-1) + rank
    ntile = t * TOP_K // MOE_TM + N_EXPERTS
    n_used = (pends[-1] // MOE_TM).reshape(1)
    tile_idx = jnp.minimum(jnp.arange(ntile, dtype=i32), n_used - 1)
    tile_e = jnp.sum((pends[None, :] <= (tile_idx * MOE_TM)[:, None]).astype(i32), axis=1)
    tile_first = jnp.concatenate([jnp.ones((1,), i32), (tile_e[1:] != tile_e[:-1]).astype(i32)])
    nonempty = counts > 0
    later = jnp.logical_and(experts[None, :] > experts[:, None], nonempty[None, :])
    next_nonempty = jnp.min(jnp.where(later, experts[None, :], N_EXPERTS), axis=1)
    next_nonempty = jnp.where(next_nonempty >= N_EXPERTS, -1, next_nonempty)
    expert_slot = (jnp.sum(jnp.logical_and(upto, nonempty[None, :]).astype(i32), axis=1) - 1) % 2
    of_tile = tile_e[:, None] == experts[None, :]
    tile_next = jnp.sum(jnp.where(of_tile, next_nonempty[None, :], 0), axis=1)
    tile_slot = jnp.sum(jnp.where(of_tile, expert_slot[None, :], 0), axis=1)

    dest3 = dest.reshape(t // DISP_TM, DISP_TM, TOP_K).transpose(0, 2, 1)
    xs = _dispatch(pends, pcounts, n_used, dest3, h2, ntile)
    yb = _moe(tile_e, tile_first, tile_next, tile_slot, n_used, xs,
              w_gate[l], b_gate[l][:, None, :], w_up[l], b_up[l][:, None, :],
              w_down[l], b_down[l][:, None, :])
    out = _combine(dest3, yb, rw, x1, g2, ln2_g[l][None, :], ln2_b[l][None, :], seq)
    return out.reshape(batch, seq, D_MODEL)
```

```python
import functools
import math

import numpy as np
import jax
import jax.numpy as jnp
from jax import lax
from jax.experimental import pallas as pl
from jax.experimental.pallas import tpu as pltpu

f32 = jnp.float32
bf16 = jnp.bfloat16
i32 = jnp.int32

D_MODEL = 1024
GM_WIDTH = 512
GM_GROUPS = 8
GM_CHUNK = 128
DIL_PAIRS = ((128, 1), (512, 4), (2048, 16))
N_DIL = 3
HEADS_PER_GROUP = 8
HEAD_DIM = 64
ATT_WIDTH = 512
ATT_BLOCK = 128
NEG_INF = -1e30
REL_BUCKETS = 32
REL_MAX_EXACT = 16
REL_MAX_DIST = 2048
N_EXPERTS = 32
TOP_K = 4
SWIGLU_LIMIT = 7.0
SWIGLU_ALPHA = 1.702
MOE_BLOCK = 128
DEPTH = 1
DN_ALPHA = (2 * DEPTH) ** 0.25
LN_EPS = 1e-5
UV_COLS = 2 * GM_WIDTH
QKV_COLS = N_DIL * 3 * ATT_WIDTH
GATE_COLS = 2 * D_MODEL
IN_COLS = UV_COLS + QKV_COLS + GATE_COLS

LANES = 128
SUBLANES = 8
VMEM_LIMIT = 56 * 1024 * 1024


def _ln(x):
    mu = jnp.mean(x, axis=-1, keepdims=True)
    xc = x - mu
    var = jnp.mean(xc * xc, axis=-1, keepdims=True)
    return xc * lax.rsqrt(var + LN_EPS)


def _params(sem, vmem=VMEM_LIMIT):
    return pltpu.CompilerParams(dimension_semantics=sem, vmem_limit_bytes=vmem)


def _adaln_kernel(c_ref, w_ref, b_ref, o_ref):
    c = c_ref[...]
    s = c * jax.nn.sigmoid(c)
    o_ref[...] = jnp.dot(s, w_ref[...], preferred_element_type=f32,
                         precision=lax.Precision.HIGHEST) + b_ref[...]


def _adaln(c8, w_ada, b_ada):
    n = w_ada.shape[1] // D_MODEL
    return pl.pallas_call(
        _adaln_kernel,
        grid=(n,),
        in_specs=[pl.BlockSpec((8, D_MODEL), lambda j: (0, 0)),
                  pl.BlockSpec((D_MODEL, D_MODEL), lambda j: (0, j)),
                  pl.BlockSpec((1, D_MODEL), lambda j: (0, j))],
        out_specs=pl.BlockSpec((8, D_MODEL), lambda j: (0, j)),
        out_shape=jax.ShapeDtypeStruct((8, w_ada.shape[1]), f32),
        compiler_params=_params(("arbitrary",)),
        name="adaln",
    )(c8, w_ada, b_ada)


IN_TM = 256
IN_CW = 512
GRP_COLS = 3 * ATT_WIDTH


def _inproj_kernel(x_ref, sc_ref, sh_ref, w_ref, p1_ref, p2_ref,
                   uv_ref, gt_ref, qkv0_ref, qkv1_ref, qkv2_ref):
    xn = _ln(x_ref[...])
    h = (xn * (1.0 + sc_ref[0]) + sh_ref[0]).astype(bf16)
    hp = [h,
          jnp.dot(p1_ref[...], h, preferred_element_type=f32).astype(bf16),
          jnp.dot(p2_ref[...], h, preferred_element_type=f32).astype(bf16)]
    for c0 in range(0, UV_COLS, IN_CW):
        acc = jnp.dot(h, w_ref[:, c0:c0 + IN_CW], preferred_element_type=f32)
        uv_ref[:, c0:c0 + IN_CW] = jax.nn.gelu(acc).astype(bf16)
    for g, (qref, (_win, dil)) in enumerate(zip((qkv0_ref, qkv1_ref, qkv2_ref), DIL_PAIRS)):
        n = IN_TM // dil
        for q0 in range(0, GRP_COLS, IN_CW):
            c0 = UV_COLS + g * GRP_COLS + q0
            acc = jnp.dot(hp[g], w_ref[:, c0:c0 + IN_CW], preferred_element_type=f32).astype(bf16)
            for rho in range(dil):
                qref[0, rho, :, q0:q0 + IN_CW] = acc[rho * n:(rho + 1) * n, :]
    for g0 in range(0, GATE_COLS, IN_CW):
        c0 = UV_COLS + QKV_COLS + g0
        acc = jnp.dot(h, w_ref[:, c0:c0 + IN_CW], preferred_element_type=f32)
        gt_ref[:, g0:g0 + IN_CW] = jax.nn.sigmoid(acc).astype(bf16)


def _inproj(x2, sc1, sh1, w_in_bf, perms, batch, seq):
    t = x2.shape[0]
    per_b = seq // IN_TM
    qkv_specs, qkv_shapes = [], []
    for _win, dil in DIL_PAIRS:
        n = IN_TM // dil
        qkv_specs.append(pl.BlockSpec((1, dil, n, GRP_COLS), lambda i: (i // per_b, 0, i % per_b, 0)))
        qkv_shapes.append(jax.ShapeDtypeStruct((batch, dil, seq // dil, GRP_COLS), bf16))
    return pl.pallas_call(
        _inproj_kernel,
        grid=(t // IN_TM,),
        in_specs=[pl.BlockSpec((IN_TM, D_MODEL), lambda i: (i, 0)),
                  pl.BlockSpec((1, 1, D_MODEL), lambda i: (i // per_b, 0, 0)),
                  pl.BlockSpec((1, 1, D_MODEL), lambda i: (i // per_b, 0, 0)),
                  pl.BlockSpec((D_MODEL, IN_COLS), lambda i: (0, 0)),
                  pl.BlockSpec((IN_TM, IN_TM), lambda i: (0, 0)),
                  pl.BlockSpec((IN_TM, IN_TM), lambda i: (0, 0))],
        out_specs=[pl.BlockSpec((IN_TM, UV_COLS), lambda i: (i, 0)),
                   pl.BlockSpec((IN_TM, GATE_COLS), lambda i: (i, 0))] + qkv_specs,
        out_shape=[jax.ShapeDtypeStruct((t, UV_COLS), bf16),
                   jax.ShapeDtypeStruct((t, GATE_COLS), bf16)] + qkv_shapes,
        compiler_params=_params(("arbitrary",)),
        name="inproj",
    )(x2, sc1, sh1, w_in_bf, perms[1], perms[2])


GM_TM = 512


def _gmlp_kernel(u_ref, v_ref, g_ref, b_ref, ws_ref, bs_ref, ya_ref):
    row = lax.broadcasted_iota(i32, (GM_CHUNK, GM_CHUNK), 0)
    col = lax.broadcasted_iota(i32, (GM_CHUNK, GM_CHUNK), 1)
    causal = col <= row
    first_half = lax.broadcasted_iota(i32, (GM_CHUNK, LANES), 1) < (GM_WIDTH // GM_GROUPS)
    ws = [jnp.where(causal, ws_ref[g], 0.0).astype(bf16) for g in range(GM_GROUPS)]
    for ch in range(GM_TM // GM_CHUNK):
        r0 = ch * GM_CHUNK
        vn = _ln(v_ref[r0:r0 + GM_CHUNK, :].astype(f32)) * g_ref[...] + b_ref[...]
        vn = vn.astype(bf16)
        for j in range(GM_WIDTH // LANES):
            slab = vn[:, j * LANES:(j + 1) * LANES]
            s_lo = jnp.dot(ws[2 * j], slab, preferred_element_type=f32)
            s_hi = jnp.dot(ws[2 * j + 1], slab, preferred_element_type=f32)
            s = jnp.where(first_half, s_lo, s_hi) + bs_ref[:, j * LANES:(j + 1) * LANES]
            u = u_ref[r0:r0 + GM_CHUNK, j * LANES:(j + 1) * LANES].astype(f32)
            ya_ref[r0:r0 + GM_CHUNK, j * LANES:(j + 1) * LANES] = (u * s).astype(bf16)


def _gmlp(uv, ln_g, ln_b, w_s, bs_full):
    t = uv.shape[0]
    return pl.pallas_call(
        _gmlp_kernel,
        grid=(t // GM_TM,),
        in_specs=[pl.BlockSpec((GM_TM, GM_WIDTH), lambda i: (i, 0)),
                  pl.BlockSpec((GM_TM, GM_WIDTH), lambda i: (i, 1)),
                  pl.BlockSpec((1, GM_WIDTH), lambda i: (0, 0)),
                  pl.BlockSpec((1, GM_WIDTH), lambda i: (0, 0)),
                  pl.BlockSpec((GM_GROUPS, GM_CHUNK, GM_CHUNK), lambda i: (0, 0, 0)),
                  pl.BlockSpec((GM_CHUNK, GM_WIDTH), lambda i: (0, 0))],
        out_specs=pl.BlockSpec((GM_TM, GM_WIDTH), lambda i: (i, 0)),
        out_shape=jax.ShapeDtypeStruct((t, GM_WIDTH), bf16),
        compiler_params=_params(("arbitrary",)),
        name="gmlp",
    )(uv, uv, ln_g, ln_b, w_s, bs_full)


def _relbias_kernel(tab_ref, bucket_ref, band_ref, out_ref):
    g = pl.program_id(0)
    bk = bucket_ref[0]
    band = band_ref[0] > 0
    for h in range(HEADS_PER_GROUP):
        acc = jnp.zeros((ATT_BLOCK, 2 * ATT_BLOCK), f32)
        for b in range(REL_BUCKETS):
            acc = jnp.where(bk == b, tab_ref[b, g * HEADS_PER_GROUP + h], acc)
        out_ref[0, h] = jnp.where(band, acc, NEG_INF)


def _relbias(rel_bias, bucket, band):
    return pl.pallas_call(
        _relbias_kernel,
        grid=(N_DIL,),
        in_specs=[pl.BlockSpec(memory_space=pltpu.SMEM),
                  pl.BlockSpec((1, ATT_BLOCK, 2 * ATT_BLOCK), lambda g: (g, 0, 0)),
                  pl.BlockSpec((1, ATT_BLOCK, 2 * ATT_BLOCK), lambda g: (g, 0, 0))],
        out_specs=pl.BlockSpec((1, HEADS_PER_GROUP, ATT_BLOCK, 2 * ATT_BLOCK),
                               lambda g: (g, 0, 0, 0)),
        out_shape=jax.ShapeDtypeStruct((N_DIL, HEADS_PER_GROUP, ATT_BLOCK, 2 * ATT_BLOCK), f32),
        compiler_params=_params(("arbitrary",)),
        name="relbias",
    )(rel_bias, bucket, band)


ATT_MAX_STEP_BLOCKS = 8


def _attn_kernel(nres, nblk, q_ref, kp_ref, kc_ref, vp_ref, vc_ref, bias_ref, o_ref, lse_ref):
    first = pl.program_id(2) == 0
    lane = lax.broadcasted_iota(i32, (ATT_BLOCK, LANES), 1)
    lo_half = lane < HEAD_DIM
    prev_cols = lax.broadcasted_iota(i32, (ATT_BLOCK, 2 * ATT_BLOCK), 1) < ATT_BLOCK
    no_prev = jnp.logical_and(first, prev_cols)
    nt = (((1,), (1,)), ((), ()))
    ones = jnp.ones((2 * ATT_BLOCK, LANES), bf16)
    n_slab = ATT_WIDTH // LANES
    blocks = [(res, i) for res in range(nres) for i in range(nblk)]
    logits, v_ext = [], []
    for res, i in blocks:
        cur = slice(i * ATT_BLOCK, (i + 1) * ATT_BLOCK)
        prv = slice((i - 1) * ATT_BLOCK, i * ATT_BLOCK)
        for j in range(n_slab):
            sl = slice(j * LANES, (j + 1) * LANES)
            q = q_ref[0, res, cur, sl] * (HEAD_DIM ** -0.5)
            k_prev = kp_ref[0, res, :, sl] if i == 0 else kc_ref[0, res, prv, sl]
            v_prev = vp_ref[0, res, :, sl] if i == 0 else vc_ref[0, res, prv, sl]
            k_cat = jnp.concatenate([k_prev, kc_ref[0, res, cur, sl]], axis=0)
            v_cat = jnp.concatenate([v_prev, vc_ref[0, res, cur, sl]], axis=0)
            v_ext.append(jnp.concatenate([v_cat, ones], axis=1))
            for hh in range(2):
                qm = jnp.where(lo_half if hh == 0 else jnp.logical_not(lo_half), q, 0.0).astype(bf16)
                lg_h = lax.dot_general(qm, k_cat, nt, preferred_element_type=f32) + bias_ref[0, 2 * j + hh]
                logits.append(jnp.where(no_prev, NEG_INF, lg_h) if i == 0 else lg_h)
    rows_per_block = HEADS_PER_GROUP * ATT_BLOCK
    lg = jnp.concatenate(logits, axis=0)
    m = jnp.max(lg, axis=-1, keepdims=True)
    p = jnp.exp(lg - m).astype(bf16)
    for b, (res, i) in enumerate(blocks):
        cur = slice(i * ATT_BLOCK, (i + 1) * ATT_BLOCK)
        lse_tile = jnp.zeros((ATT_BLOCK, LANES), f32)
        for j in range(n_slab):
            outs = []
            for hh in range(2):
                h = 2 * j + hh
                r0 = b * rows_per_block + h * ATT_BLOCK
                r = jnp.dot(p[r0:r0 + ATT_BLOCK], v_ext[b * n_slab + j], preferred_element_type=f32)
                den = r[:, LANES:]
                outs.append(r[:, :LANES] * (1.0 / den))
                lse_h = m[r0:r0 + ATT_BLOCK] + jnp.log(den)
                lse_tile = jnp.where(lane == h, lse_h, lse_tile)
            o_ref[0, res, cur, j * LANES:(j + 1) * LANES] = jnp.where(lo_half, outs[0], outs[1]).astype(bf16)
        lse_ref[0, res, cur, :] = lse_tile


def _attn_group(qkv_g, bias, g, dil, batch, seq):
    l = seq // dil
    nblk = min(ATT_MAX_STEP_BLOCKS, l // ATT_BLOCK)
    nres = min(dil, ATT_MAX_STEP_BLOCKS // nblk)
    tm = nblk * ATT_BLOCK
    nsteps = l // tm

    def cur(cb):
        return pl.BlockSpec((1, nres, tm, ATT_WIDTH), lambda b, r, n: (b, r, n, cb))

    def prev(cb):
        return pl.BlockSpec((1, nres, ATT_BLOCK, ATT_WIDTH),
                            lambda b, r, n: (b, r, jnp.maximum(n * nblk - 1, 0), cb))

    return pl.pallas_call(
        functools.partial(_attn_kernel, nres, nblk),
        grid=(batch, dil // nres, nsteps),
        in_specs=[cur(0), prev(1), cur(1), prev(2), cur(2),
                  pl.BlockSpec((1, HEADS_PER_GROUP, ATT_BLOCK, 2 * ATT_BLOCK),
                               lambda b, r, n: (g, 0, 0, 0))],
        out_specs=[pl.BlockSpec((1, nres, tm, ATT_WIDTH), lambda b, r, n: (b, r, n, 0)),
                   pl.BlockSpec((1, nres, tm, LANES), lambda b, r, n: (b, r, n, 0))],
        out_shape=[jax.ShapeDtypeStruct((batch, dil, l, ATT_WIDTH), bf16),
                   jax.ShapeDtypeStruct((batch, dil, l, LANES), f32)],
        compiler_params=_params(("arbitrary", "arbitrary", "arbitrary")),
        name=f"attn_g{g}",
    )(qkv_g, qkv_g, qkv_g, qkv_g, qkv_g, bias)


ROW_WORDS = D_MODEL // 2
ROW_SUB = ROW_WORDS // LANES
HI_MASK = -65536


def _pack_rows(x):
    bits = lax.bitcast_convert_type(x.astype(bf16).astype(f32), i32)
    return lax.shift_right_logical(bits[:, :ROW_WORDS], 16) | (bits[:, ROW_WORDS:] & HI_MASK)


def _unpack_rows(words):
    lo = lax.bitcast_convert_type(lax.shift_left(words, 16), f32)
    hi = lax.bitcast_convert_type(words & HI_MASK, f32)
    return jnp.concatenate([lo, hi], axis=1)


def _store_packed(ref, words, n, first_row=0):
    for r in range(ROW_SUB):
        ref[pl.ds(first_row * ROW_SUB + r, n, stride=ROW_SUB), :] = words[:, r * LANES:(r + 1) * LANES]


def _load_packed(ref, first_row, n):
    return jnp.concatenate([ref[pl.ds(first_row * ROW_SUB + r, n, stride=ROW_SUB), :] for r in range(ROW_SUB)],
                           axis=1)


MIX_TM = 256
MIX_SUB = 128


def _split_bf16(x, parts):
    out = []
    for _ in range(parts):
        hi = x.astype(bf16)
        out.append(hi)
        x = x - hi.astype(f32)
    return out


def _mix_kernel(o0_ref, o1_ref, o2_ref, l0_ref, l1_ref, l2_ref, pt1_ref, pt2_ref, ex_ref,
                ya_ref, gt_ref, x_ref,
                g1_ref, sc2_ref, sh2_ref, wa_ref, wb_ref, wo_ref, ln1g_ref, ln1b_ref,
                wrc_ref, br_ref, tri_ref,
                x1_ref, h2_ref, route_ref, rw_ref, cnt_ref, run_ref, xr_ref):
    step = pl.program_id(0)

    @pl.when(step == 0)
    def _():
        run_ref[...] = jnp.zeros_like(run_ref)
        xr_ref[...] = jnp.zeros_like(xr_ref)

    def back_rows(r0):
        rows = slice(r0, r0 + MIX_SUB)
        x1 = _ln(xr_ref[rows, :]) * ln1g_ref[...] + ln1b_ref[...]
        x1_ref[rows, :] = x1
        h2 = _ln(x1) * (1.0 + sc2_ref[0]) + sh2_ref[0]
        _store_packed(h2_ref, _pack_rows(h2), MIX_SUB, r0)
        h_hi, h_lo = _split_bf16(h2, 2)
        hi_both = jnp.dot(h_hi, wrc_ref[...], preferred_element_type=f32)
        return (hi_both[:, :LANES]
                + (hi_both[:, LANES:] + jnp.dot(h_lo, wrc_ref[:, :LANES], preferred_element_type=f32))
                ) + br_ref[...]

    def front_rows(r0):
        rows = slice(r0, r0 + MIX_SUB)
        os_, ls_ = [o0_ref[0, 0, rows, :].astype(f32)], [l0_ref[0, 0, rows, :]]
        for o_ref, l_ref, pt_ref in ((o1_ref, l1_ref, pt1_ref), (o2_ref, l2_ref, pt2_ref)):
            pt = pt_ref[rows, :]
            os_.append(jnp.dot(pt, o_ref[0].reshape(MIX_TM, ATT_WIDTH), preferred_element_type=f32))
            parts = [jnp.dot(pt, part, preferred_element_type=f32)
                     for part in _split_bf16(l_ref[0].reshape(MIX_TM, LANES), 3)]
            ls_.append((parts[0] + parts[1]) + parts[2])
        lm = jnp.maximum(jnp.maximum(ls_[0], ls_[1]), ls_[2])
        es = [jnp.exp(lse - lm) for lse in ls_]
        inv = 1.0 / (es[0] + es[1] + es[2])
        yb = jnp.zeros((MIX_SUB, ATT_WIDTH), f32)
        for e, o in zip(es, os_):
            w_parts = jnp.concatenate(_split_bf16(e * inv, 2), axis=1)
            yb = yb + jnp.dot(w_parts, ex_ref[...], preferred_element_type=f32) * o
        a = jnp.dot(ya_ref[rows, :], wa_ref[...], preferred_element_type=f32)
        b = jnp.dot(yb.astype(bf16), wb_ref[...], preferred_element_type=f32)
        merged = gt_ref[rows, :D_MODEL].astype(f32) * a + gt_ref[rows, D_MODEL:].astype(f32) * b
        mix = jnp.dot(merged.astype(bf16), wo_ref[...], preferred_element_type=f32)
        xr_ref[rows, :] = DN_ALPHA * x_ref[rows, :] + g1_ref[0] * mix

    logit_parts = []
    for r0 in range(0, MIX_TM, MIX_SUB):
        logit_parts.append(back_rows(r0))
        front_rows(r0)
    logits = jnp.concatenate(logit_parts, axis=0)
    lane = lax.broadcasted_iota(i32, (MIX_TM, LANES), 1)
    logits = jnp.where(lane < N_EXPERTS, logits, -jnp.inf)
    lane_f = lane.astype(f32)
    vals, idxs = [], []
    for _k in range(TOP_K):
        m = jnp.max(logits, axis=-1, keepdims=True)
        vals.append(m)
        idxs.append(jnp.min(jnp.where(logits == m, lane_f, float(LANES)), axis=-1, keepdims=True).astype(i32))
        logits = jnp.where(lane == idxs[-1], -jnp.inf, logits)
    exps = [jnp.exp(v - vals[0]) for v in vals]
    den = exps[0] + exps[1] + exps[2] + exps[3]
    wts = [e / den for e in exps]
    hits = [lane == idx for idx in idxs]
    counted = jnp.where(step > 0, 1.0, 0.0)
    onehot = jnp.zeros((MIX_TM, LANES), f32)
    for hit in hits:
        onehot = onehot + jnp.where(hit, counted, 0.0)
    prefix = jnp.dot(tri_ref[...], onehot.astype(bf16), preferred_element_type=f32) + run_ref[...]
    route = jnp.zeros((MIX_TM, LANES), i32)
    rw = jnp.zeros((MIX_TM, LANES), f32)
    for k in range(TOP_K):
        rank = jnp.sum(jnp.where(hits[k], prefix, 0.0), axis=-1, keepdims=True).astype(i32)
        route = jnp.where(lane == k, idxs[k], route)
        route = jnp.where(lane == TOP_K + k, rank, route)
        rw = jnp.where(lane == k, wts[k], rw)
    route_ref[...] = route
    rw_ref[...] = rw
    run = run_ref[...] + jnp.sum(onehot, axis=0, keepdims=True)
    run_ref[...] = run
    cnt_ref[...] = jnp.broadcast_to(run, cnt_ref.shape)


def _mix(os_, ls_, perms_t, expand, ya, gates, x2, g1, sc2, sh2, wa, wb, wo, ln1g, ln1b, wr_parts, br, tri, seq):
    t = x2.shape[0]
    nb = t // MIX_TM
    per_b = seq // MIX_TM
    cur = lambda i: jnp.minimum(i, nb - 1)
    prv = lambda i: jnp.maximum(i - 1, 0)
    row = lambda w: pl.BlockSpec((MIX_TM, w), lambda i: (cur(i), 0))
    out_row = lambda w: pl.BlockSpec((MIX_TM, w), lambda i: (prv(i), 0))
    const = lambda s: pl.BlockSpec(s, lambda i: tuple(0 for _ in s))
    mod_cur = pl.BlockSpec((1, 1, D_MODEL), lambda i: (cur(i) // per_b, 0, 0))
    mod_prv = pl.BlockSpec((1, 1, D_MODEL), lambda i: (prv(i) // per_b, 0, 0))
    grp = lambda w: [pl.BlockSpec((1, dil, MIX_TM // dil, w), lambda i: (cur(i) // per_b, 0, cur(i) % per_b, 0))
                     for _win, dil in DIL_PAIRS]
    return pl.pallas_call(
        _mix_kernel,
        grid=(nb + 1,),
        in_specs=grp(ATT_WIDTH) + grp(LANES) + [
                  const((MIX_TM, MIX_TM)), const((MIX_TM, MIX_TM)), const((2 * LANES, ATT_WIDTH)),
                  row(GM_WIDTH), row(GATE_COLS), row(D_MODEL),
                  mod_cur, mod_prv, mod_prv,
                  const((GM_WIDTH, D_MODEL)), const((ATT_WIDTH, D_MODEL)), const((D_MODEL, D_MODEL)),
                  const((1, D_MODEL)), const((1, D_MODEL)),
                  const((D_MODEL, 2 * LANES)), const((1, LANES)), const((MIX_TM, MIX_TM))],
        out_specs=[out_row(D_MODEL), pl.BlockSpec((MIX_TM * ROW_SUB, LANES), lambda i: (prv(i), 0)),
                   out_row(LANES), out_row(LANES), const((8, LANES))],
        out_shape=[jax.ShapeDtypeStruct((t, D_MODEL), f32),
                   jax.ShapeDtypeStruct((t * ROW_SUB, LANES), i32),
                   jax.ShapeDtypeStruct((t, LANES), i32),
                   jax.ShapeDtypeStruct((t, LANES), f32),
                   jax.ShapeDtypeStruct((8, LANES), f32)],
        scratch_shapes=[pltpu.VMEM((1, LANES), f32), pltpu.VMEM((MIX_TM, D_MODEL), f32)],
        compiler_params=_params(("arbitrary",)),
        name="mix",
    )(*os_, *ls_, perms_t[1], perms_t[2], expand, ya, gates, x2, g1, sc2, sh2, wa, wb, wo,
      ln1g, ln1b, wr_parts, br, tri)


DISP_TM = 512
MOE_TM = 512


def _dispatch_kernel(pends_ref, pcnt_ref, nused_ref, dest_ref, h2p_ref, xs_hbm, zbuf, sem, zsem):
    i = pl.program_id(0)
    ntile = xs_hbm.shape[0] // (MOE_TM * ROW_SUB)

    def zero_tile(first_row):
        return pltpu.make_async_copy(
            zbuf, xs_hbm.at[pl.ds(pl.multiple_of(first_row * ROW_SUB, MOE_TM * ROW_SUB), MOE_TM * ROW_SUB)], zsem)

    def for_each_zero_tile(fn):
        for e in range(N_EXPERTS):
            pl.when(pcnt_ref[e] > 0)(functools.partial(fn, lambda e=e: zero_tile(pends_ref[e] - MOE_TM)))
        for k in range(N_EXPERTS):
            tile = nused_ref[0] + k
            pl.when(tile < ntile)(functools.partial(fn, lambda tile=tile: zero_tile(tile * MOE_TM)))

    @pl.when(i == 0)
    def _():
        zbuf[...] = jnp.zeros_like(zbuf)
        for_each_zero_tile(lambda mk: mk().start())
        for_each_zero_tile(lambda mk: mk().wait())

    def row_copy(k, r):
        d = dest_ref[0, k, r]
        return pltpu.make_async_copy(h2p_ref.at[pl.ds(r * ROW_SUB, ROW_SUB)],
                                     xs_hbm.at[pl.ds(pl.multiple_of(d * ROW_SUB, ROW_SUB), ROW_SUB)], sem)

    for r in range(DISP_TM):
        for k in range(TOP_K):
            row_copy(k, r).start(priority=k % 2)
    for k in range(TOP_K):
        pltpu.make_async_copy(h2p_ref, xs_hbm.at[pl.ds(0, DISP_TM * ROW_SUB)], sem).wait()


def _dispatch(pends, pcounts, n_used, dest3, h2p, ntile):
    t = h2p.shape[0] // ROW_SUB
    grid_spec = pltpu.PrefetchScalarGridSpec(
        num_scalar_prefetch=3,
        grid=(t // DISP_TM,),
        in_specs=[pl.BlockSpec((1, TOP_K, DISP_TM), lambda i, *_: (i, 0, 0), memory_space=pltpu.SMEM),
                  pl.BlockSpec((DISP_TM * ROW_SUB, LANES), lambda i, *_: (i, 0))],
        out_specs=pl.BlockSpec(memory_space=pl.ANY),
        scratch_shapes=[pltpu.VMEM((MOE_TM * ROW_SUB, LANES), i32),
                        pltpu.SemaphoreType.DMA(()),
                        pltpu.SemaphoreType.DMA(())],
    )
    return pl.pallas_call(
        _dispatch_kernel,
        grid_spec=grid_spec,
        out_shape=jax.ShapeDtypeStruct((ntile * MOE_TM * ROW_SUB, LANES), i32),
        compiler_params=_params(("arbitrary",)),
        name="dispatch",
    )(pends, pcounts, n_used, dest3, h2p)


def _moe_kernel(te_ref, first_ref, nexte_ref, wslot_ref, nused_ref,
                xs_ref, wg_hbm, wu_hbm, wd_hbm, bg_ref, bu_ref, bd_ref,
                out_ref, wbuf, wgb, wub, wdb, sem_w):
    j = pl.program_id(0)

    def weight_copies(e, ws):
        return [pltpu.make_async_copy(w.at[e], wbuf.at[ws, k], sem_w.at[ws])
                for k, w in enumerate((wg_hbm, wu_hbm, wd_hbm))]

    @pl.when(j == 0)
    def _():
        for cp in weight_copies(te_ref[0], wslot_ref[0]):
            cp.start()

    @pl.when(first_ref[j] == 1)
    def _():
        ws = wslot_ref[j]
        for cp in weight_copies(te_ref[j], ws):
            cp.wait()
        wgb[...] = wbuf[ws, 0].astype(bf16)
        wub[...] = wbuf[ws, 1].astype(bf16)
        wdb[...] = wbuf[ws, 2].astype(bf16)
        ne = nexte_ref[j]

        @pl.when(ne >= 0)
        def _():
            for cp in weight_copies(ne, 1 - ws):
                cp.start()

    used = j < nused_ref[0]

    @pl.when(used)
    def _():
        xb = _unpack_rows(_load_packed(xs_ref, 0, MOE_TM)).astype(bf16)
        g = jnp.dot(xb, wgb[...], preferred_element_type=f32) + bg_ref[0]
        u = jnp.dot(xb, wub[...], preferred_element_type=f32) + bu_ref[0]
        g = jnp.minimum(g, SWIGLU_LIMIT)
        u = jnp.clip(u, -SWIGLU_LIMIT, SWIGLU_LIMIT)
        act = (u + 1.0) * (g * jax.nn.sigmoid(SWIGLU_ALPHA * g))
        y = jnp.dot(act.astype(bf16), wdb[...], preferred_element_type=f32) + bd_ref[0]
        _store_packed(out_ref, _pack_rows(y), MOE_TM)

    @pl.when(jnp.logical_not(used))
    def _():
        out_ref[...] = jnp.zeros_like(out_ref)


def _moe(tile_e, tile_first, next_e, wslot, n_used, xs, w_gate, b_gate, w_up, b_up, w_down, b_down):
    ntile = tile_e.shape[0]
    bspec = pl.BlockSpec((1, 1, D_MODEL), lambda j, te, *_: (te[j], 0, 0))
    hbm = pl.BlockSpec(memory_space=pl.ANY)
    grid_spec = pltpu.PrefetchScalarGridSpec(
        num_scalar_prefetch=5,
        grid=(ntile,),
        in_specs=[pl.BlockSpec((MOE_TM * ROW_SUB, LANES),
                               lambda j, te, fi, ne, ws, nu: (jnp.minimum(j, nu[0] - 1), 0)),
                  hbm, hbm, hbm, bspec, bspec, bspec],
        out_specs=pl.BlockSpec((MOE_TM * ROW_SUB, LANES), lambda j, *_: (j, 0)),
        scratch_shapes=[pltpu.VMEM((2, 3, D_MODEL, D_MODEL), f32),
                        pltpu.VMEM((D_MODEL, D_MODEL), bf16),
                        pltpu.VMEM((D_MODEL, D_MODEL), bf16),
                        pltpu.VMEM((D_MODEL, D_MODEL), bf16),
                        pltpu.SemaphoreType.DMA((2,))],
    )
    return pl.pallas_call(
        _moe_kernel,
        grid_spec=grid_spec,
        out_shape=jax.ShapeDtypeStruct((ntile * MOE_TM * ROW_SUB, LANES), i32),
        compiler_params=_params(("arbitrary",)),
        name="moe",
    )(tile_e, tile_first, next_e, wslot, n_used, xs, w_gate, w_up, w_down, b_gate, b_up, b_down)


CB_TM = DISP_TM


def _combine_kernel(dcur_ref, dnxt_ref, yb_hbm, rw_ref, x1_ref, g2_ref, lng_ref, lnb_ref, out_ref,
                    ybuf0, ybuf1, sem):
    i = pl.program_id(0)
    last = pl.num_programs(0) - 1
    ybufs = (ybuf0, ybuf1)

    def row_copy(d, k, r, s):
        return pltpu.make_async_copy(
            yb_hbm.at[pl.ds(pl.multiple_of(d * ROW_SUB, ROW_SUB), ROW_SUB)],
            ybufs[s].at[pl.ds(pl.multiple_of((k * CB_TM + r) * ROW_SUB, ROW_SUB), ROW_SUB)],
            sem.at[s])

    @pl.when(i == 0)
    def _():
        for k in range(TOP_K):
            def body(r, c, k=k):
                row_copy(dcur_ref[0, k, r], k, r, 0).start()
                return c
            lax.fori_loop(0, CB_TM, body, 0, unroll=8)

    for s in range(2):
        @pl.when(i % 2 == s)
        def _(s=s):
            pltpu.make_async_copy(yb_hbm.at[pl.ds(0, TOP_K * CB_TM * ROW_SUB)], ybufs[s], sem.at[s]).wait()

            @pl.when(i < last)
            def _():
                for k in range(TOP_K):
                    for r in range(CB_TM):
                        row_copy(dnxt_ref[0, k, r], k, r, 1 - s).start(priority=r % 2)

            parts = [_unpack_rows(_load_packed(ybufs[s], k * CB_TM, CB_TM)) * rw_ref[:, k:k + 1]
                     for k in range(TOP_K)]
            y = (parts[0] + parts[1]) + (parts[2] + parts[3])
            out_ref[...] = _ln(DN_ALPHA * x1_ref[...] + g2_ref[0] * y) * lng_ref[...] + lnb_ref[...]


def _combine(dest3, yb, rw, x1, g2, ln2g, ln2b, seq):
    t = x1.shape[0]
    nb = t // CB_TM
    per_b = seq // CB_TM
    return pl.pallas_call(
        _combine_kernel,
        grid=(nb,),
        in_specs=[pl.BlockSpec((1, TOP_K, CB_TM), lambda i: (i, 0, 0), memory_space=pltpu.SMEM),
                  pl.BlockSpec((1, TOP_K, CB_TM), lambda i: (jnp.minimum(i + 1, nb - 1), 0, 0),
                               memory_space=pltpu.SMEM),
                  pl.BlockSpec(memory_space=pl.ANY),
                  pl.BlockSpec((CB_TM, LANES), lambda i: (i, 0)),
                  pl.BlockSpec((CB_TM, D_MODEL), lambda i: (i, 0)),
                  pl.BlockSpec((1, 1, D_MODEL), lambda i: (i // per_b, 0, 0)),
                  pl.BlockSpec((1, D_MODEL), lambda i: (0, 0)),
                  pl.BlockSpec((1, D_MODEL), lambda i: (0, 0))],
        out_specs=pl.BlockSpec((CB_TM, D_MODEL), lambda i: (i, 0)),
        out_shape=jax.ShapeDtypeStruct((t, D_MODEL), f32),
        scratch_shapes=[pltpu.VMEM((TOP_K * CB_TM * ROW_SUB, LANES), i32),
                        pltpu.VMEM((TOP_K * CB_TM * ROW_SUB, LANES), i32),
                        pltpu.SemaphoreType.DMA((2,))],
        compiler_params=_params(("arbitrary",)),
        name="combine",
    )(dest3, dest3, yb, rw, x1, g2, ln2g, ln2b)


def _t5_bucket(dist):
    d = dist.astype(f32)
    large = REL_MAX_EXACT + jnp.log(jnp.maximum(d, float(REL_MAX_EXACT)) / REL_MAX_EXACT) / math.log(
        REL_MAX_DIST / REL_MAX_EXACT) * (REL_BUCKETS - REL_MAX_EXACT)
    large = jnp.minimum(large.astype(i32), REL_BUCKETS - 1)
    return jnp.where(dist < REL_MAX_EXACT, dist, large)


def _bias_indices():
    qi = jnp.arange(ATT_BLOCK)[:, None]
    ki = jnp.arange(2 * ATT_BLOCK)[None, :]
    didx = qi + ATT_BLOCK - ki
    buckets, bands = [], []
    for win, dil in DIL_PAIRS:
        buckets.append(_t5_bucket(jnp.clip(didx, 0, None) * dil))
        bands.append(((didx >= 0) & (didx <= win // dil)).astype(i32))
    return jnp.stack(buckets).astype(i32), jnp.stack(bands)


def _residue_perm(tm, dil):
    n = tm // dil
    dst = np.arange(tm)
    src = (dst % n) * dil + dst // n
    return src[:, None] == np.arange(tm)[None, :]


def kernel(x, c, w_ada, b_ada, w_in, gm_ln_g, gm_ln_b, gm_w_s, gm_b_s, w_branch_a, w_branch_b, w_out,
           rel_bias, ln1_g, ln1_b, w_router, b_router, w_gate, b_gate, w_up, b_up, w_down, b_down,
           ln2_g, ln2_b):
    batch, seq, _ = x.shape
    t = batch * seq
    l = 0
    x2 = x.reshape(t, D_MODEL)

    c8 = jnp.pad(c, ((0, 8 - batch), (0, 0)))
    mod = _adaln(c8, w_ada[l], b_ada[l][None, :])[:batch]
    sh1, sc1, g1, sh2, sc2, g2 = [m[:, None, :] for m in jnp.split(mod, 6, axis=-1)]

    perms = [jnp.asarray(_residue_perm(IN_TM, dil), bf16) for _win, dil in DIL_PAIRS]
    uv, gates, *qkvs = _inproj(x2, sc1, sh1, w_in[l].astype(bf16), perms, batch, seq)

    bs_full = jnp.repeat(gm_b_s[l].T, GM_WIDTH // GM_GROUPS, axis=1)
    ya = _gmlp(uv, gm_ln_g[l][None, :], gm_ln_b[l][None, :], gm_w_s[l], bs_full)

    bucket, band = _bias_indices()
    bias = _relbias(rel_bias, bucket, band)
    os_, ls_ = [], []
    for g, (_win, dil) in enumerate(DIL_PAIRS):
        o, lse = _attn_group(qkvs[g], bias, g, dil, batch, seq)
        os_.append(o)
        ls_.append(lse)

    wr = jnp.pad(w_router[l], ((0, 0), (0, LANES - N_EXPERTS)))
    wr_hi = wr.astype(bf16)
    wr_parts = jnp.concatenate([wr_hi, (wr - wr_hi.astype(f32)).astype(bf16)], axis=1)
    br = jnp.pad(b_router[l], (0, LANES - N_EXPERTS))[None, :]
    tri = jnp.asarray(np.arange(MIX_TM)[None, :] < np.arange(MIX_TM)[:, None], bf16)
    perms_t = [jnp.asarray(_residue_perm(MIX_TM, dil).T, bf16) for _win, dil in DIL_PAIRS]
    expand = np.arange(LANES)[:, None] == np.arange(ATT_WIDTH)[None, :] // HEAD_DIM
    expand = jnp.asarray(np.concatenate([expand, expand], axis=0), bf16)
    x1, h2, route, rw, cnt = _mix(
        os_, ls_, perms_t, expand, ya, gates, x2, g1, sc2, sh2,
        w_branch_a[l].astype(bf16), w_branch_b[l].astype(bf16), w_out[l].astype(bf16),
        ln1_g[l][None, :], ln1_b[l][None, :], wr_parts, br, tri, seq)

    top_e = route[:, :TOP_K]
    rank = route[:, TOP_K:2 * TOP_K]
    counts = cnt[0, :N_EXPERTS].astype(i32)
    pcounts = (counts + MOE_TM - 1) // MOE_TM * MOE_TM
    experts = jnp.arange(N_EXPERTS, dtype=i32)
    upto = experts[None, :] <= experts[:, None]
    pends = jnp.sum(jnp.where(upto, pcounts[None, :], 0), axis=1)
    pstarts = pends - pcounts
    dest = jnp.sum(jnp.where(top_e[:, :, None] == experts, pstarts, 0), axis=-1) + rank
    ntile = t * TOP_K // MOE_TM + N_EXPERTS
    n_used = (pends[-1] // MOE_TM).reshape(1)
    tile_idx = jnp.minimum(jnp.arange(ntile, dtype=i32), n_used - 1)
    tile_e = jnp.sum((pends[None, :] <= (tile_idx * MOE_TM)[:, None]).astype(i32), axis=1)
    tile_first = jnp.concatenate([jnp.ones((1,), i32), (tile_e[1:] != tile_e[:-1]).astype(i32)])
    nonempty = counts > 0
    later = jnp.logical_and(experts[None, :] > experts[:, None], nonempty[None, :])
    next_nonempty = jnp.min(jnp.where(later, experts[None, :], N_EXPERTS), axis=1)
    next_nonempty = jnp.where(next_nonempty >= N_EXPERTS, -1, next_nonempty)
    expert_slot = (jnp.sum(jnp.logical_and(upto, nonempty[None, :]).astype(i32), axis=1) - 1) % 2
    of_tile = tile_e[:, None] == experts[None, :]
    tile_next = jnp.sum(jnp.where(of_tile, next_nonempty[None, :], 0), axis=1)
    tile_slot = jnp.sum(jnp.where(of_tile, expert_slot[None, :], 0), axis=1)

    dest3 = dest.reshape(t // DISP_TM, DISP_TM, TOP_K).transpose(0, 2, 1)
    xs = _dispatch(pends, pcounts, n_used, dest3, h2, ntile)
    yb = _moe(tile_e, tile_first, tile_next, tile_slot, n_used, xs,
              w_gate[l], b_gate[l][:, None, :], w_up[l], b_up[l][:, None, :],
              w_down[l], b_down[l][:, None, :])
    out = _combine(dest3, yb, rw, x1, g2, ln2_g[l][None, :], ln2_b[l][None, :], seq)
    return out.reshape(batch, seq, D_MODEL)
```

```python
import functools
import math

import numpy as np
import jax
import jax.numpy as jnp
from jax import lax
from jax.experimental import pallas as pl
from jax.experimental.pallas import tpu as pltpu

f32 = jnp.float32
bf16 = jnp.bfloat16
i32 = jnp.int32

D_MODEL = 1024
GM_WIDTH = 512
GM_GROUPS = 8
GM_CHUNK = 128
DIL_PAIRS = ((128, 1), (512, 4), (2048, 16))
N_DIL = 3
HEADS_PER_GROUP = 8
HEAD_DIM = 64
ATT_WIDTH = 512
ATT_BLOCK = 128
NEG_INF = -1e30
REL_BUCKETS = 32
REL_MAX_EXACT = 16
REL_MAX_DIST = 2048
N_EXPERTS = 32
TOP_K = 4
SWIGLU_LIMIT = 7.0
SWIGLU_ALPHA = 1.702
MOE_BLOCK = 128
DEPTH = 1
DN_ALPHA = (2 * DEPTH) ** 0.25
LN_EPS = 1e-5
UV_COLS = 2 * GM_WIDTH
QKV_COLS = N_DIL * 3 * ATT_WIDTH
GATE_COLS = 2 * D_MODEL
IN_COLS = UV_COLS + QKV_COLS + GATE_COLS

LANES = 128
SUBLANES = 8
VMEM_LIMIT = 56 * 1024 * 1024


def _ln(x):
    mu = jnp.mean(x, axis=-1, keepdims=True)
    xc = x - mu
    var = jnp.mean(xc * xc, axis=-1, keepdims=True)
    return xc * lax.rsqrt(var + LN_EPS)


def _params(sem, vmem=VMEM_LIMIT):
    return pltpu.CompilerParams(dimension_semantics=sem, vmem_limit_bytes=vmem)


def _adaln_kernel(c_ref, w_ref, b_ref, o_ref):
    c = c_ref[...]
    s = c * jax.nn.sigmoid(c)
    o_ref[...] = jnp.dot(s, w_ref[...], preferred_element_type=f32,
                         precision=lax.Precision.HIGHEST) + b_ref[...]


def _adaln(c8, w_ada, b_ada):
    n = w_ada.shape[1] // D_MODEL
    return pl.pallas_call(
        _adaln_kernel,
        grid=(n,),
        in_specs=[pl.BlockSpec((8, D_MODEL), lambda j: (0, 0)),
                  pl.BlockSpec((D_MODEL, D_MODEL), lambda j: (0, j)),
                  pl.BlockSpec((1, D_MODEL), lambda j: (0, j))],
        out_specs=pl.BlockSpec((8, D_MODEL), lambda j: (0, j)),
        out_shape=jax.ShapeDtypeStruct((8, w_ada.shape[1]), f32),
        compiler_params=_params(("arbitrary",)),
        name="adaln",
    )(c8, w_ada, b_ada)


IN_TM = 256
IN_CW = 512
GRP_COLS = 3 * ATT_WIDTH


def _inproj_kernel(x_ref, sc_ref, sh_ref, w_ref, p1_ref, p2_ref, gm_g_ref, gm_b_ref, ws_ref, bs_ref,
                   ya_ref, gt_ref, qkv0_ref, qkv1_ref, qkv2_ref):
    xn = _ln(x_ref[...])
    h = (xn * (1.0 + sc_ref[0]) + sh_ref[0]).astype(bf16)
    hp = [h,
          jnp.dot(p1_ref[...], h, preferred_element_type=f32).astype(bf16),
          jnp.dot(p2_ref[...], h, preferred_element_type=f32).astype(bf16)]
    u_act, v_act = [jax.nn.gelu(jnp.dot(h, w_ref[:, c0:c0 + GM_WIDTH], preferred_element_type=f32))
                    for c0 in range(0, UV_COLS, GM_WIDTH)]
    row = lax.broadcasted_iota(i32, (GM_CHUNK, GM_CHUNK), 0)
    col = lax.broadcasted_iota(i32, (GM_CHUNK, GM_CHUNK), 1)
    first_half = lax.broadcasted_iota(i32, (GM_CHUNK, LANES), 1) < (GM_WIDTH // GM_GROUPS)
    ws = [jnp.where(col <= row, ws_ref[g], 0.0).astype(bf16) for g in range(GM_GROUPS)]

    def gate_chunk(r0):
        vn = (_ln(v_act[r0:r0 + GM_CHUNK, :]) * gm_g_ref[...] + gm_b_ref[...]).astype(bf16)
        for j in range(GM_WIDTH // LANES):
            sl = slice(j * LANES, (j + 1) * LANES)
            s_lo = jnp.dot(ws[2 * j], vn[:, sl], preferred_element_type=f32)
            s_hi = jnp.dot(ws[2 * j + 1], vn[:, sl], preferred_element_type=f32)
            s = jnp.where(first_half, s_lo, s_hi) + bs_ref[:, sl]
            ya_ref[r0:r0 + GM_CHUNK, sl] = (u_act[r0:r0 + GM_CHUNK, sl] * s).astype(bf16)

    pending = list(range(0, IN_TM, GM_CHUNK))
    for g, (qref, (_win, dil)) in enumerate(zip((qkv0_ref, qkv1_ref, qkv2_ref), DIL_PAIRS)):
        n = IN_TM // dil
        for q0 in range(0, GRP_COLS, IN_CW):
            c0 = UV_COLS + g * GRP_COLS + q0
            acc = jnp.dot(hp[g], w_ref[:, c0:c0 + IN_CW], preferred_element_type=f32).astype(bf16)
            for rho in range(dil):
                qref[0, rho, :, q0:q0 + IN_CW] = acc[rho * n:(rho + 1) * n, :]
        if pending:
            gate_chunk(pending.pop(0))
    while pending:
        gate_chunk(pending.pop(0))
    for g0 in range(0, GATE_COLS, IN_CW):
        c0 = UV_COLS + QKV_COLS + g0
        acc = jnp.dot(h, w_ref[:, c0:c0 + IN_CW], preferred_element_type=f32)
        gt_ref[:, g0:g0 + IN_CW] = jax.nn.sigmoid(acc).astype(bf16)


def _inproj(x2, sc1, sh1, w_in_bf, perms, gm_g, gm_b, w_s, bs_full, batch, seq):
    t = x2.shape[0]
    per_b = seq // IN_TM
    qkv_specs, qkv_shapes = [], []
    for _win, dil in DIL_PAIRS:
        n = IN_TM // dil
        qkv_specs.append(pl.BlockSpec((1, dil, n, GRP_COLS), lambda i: (i // per_b, 0, i % per_b, 0)))
        qkv_shapes.append(jax.ShapeDtypeStruct((batch, dil, seq // dil, GRP_COLS), bf16))
    return pl.pallas_call(
        _inproj_kernel,
        grid=(t // IN_TM,),
        in_specs=[pl.BlockSpec((IN_TM, D_MODEL), lambda i: (i, 0)),
                  pl.BlockSpec((1, 1, D_MODEL), lambda i: (i // per_b, 0, 0)),
                  pl.BlockSpec((1, 1, D_MODEL), lambda i: (i // per_b, 0, 0)),
                  pl.BlockSpec((D_MODEL, IN_COLS), lambda i: (0, 0)),
                  pl.BlockSpec((IN_TM, IN_TM), lambda i: (0, 0)),
                  pl.BlockSpec((IN_TM, IN_TM), lambda i: (0, 0)),
                  pl.BlockSpec((1, GM_WIDTH), lambda i: (0, 0)),
                  pl.BlockSpec((1, GM_WIDTH), lambda i: (0, 0)),
                  pl.BlockSpec((GM_GROUPS, GM_CHUNK, GM_CHUNK), lambda i: (0, 0, 0)),
                  pl.BlockSpec((GM_CHUNK, GM_WIDTH), lambda i: (0, 0))],
        out_specs=[pl.BlockSpec((IN_TM, GM_WIDTH), lambda i: (i, 0)),
                   pl.BlockSpec((IN_TM, GATE_COLS), lambda i: (i, 0))] + qkv_specs,
        out_shape=[jax.ShapeDtypeStruct((t, GM_WIDTH), bf16),
                   jax.ShapeDtypeStruct((t, GATE_COLS), bf16)] + qkv_shapes,
        compiler_params=_params(("arbitrary",)),
        name="inproj",
    )(x2, sc1, sh1, w_in_bf, perms[1], perms[2], gm_g, gm_b, w_s, bs_full)


def _relbias_kernel(tab_ref, bucket_ref, band_ref, out_ref):
    g = pl.program_id(0)
    bk = bucket_ref[0]
    band = band_ref[0] > 0
    for h in range(HEADS_PER_GROUP):
        acc = jnp.zeros((ATT_BLOCK, 2 * ATT_BLOCK), f32)
        for b in range(REL_BUCKETS):
            acc = jnp.where(bk == b, tab_ref[b, g * HEADS_PER_GROUP + h], acc)
        out_ref[0, h] = jnp.where(band, acc, NEG_INF)


def _relbias(rel_bias, bucket, band):
    return pl.pallas_call(
        _relbias_kernel,
        grid=(N_DIL,),
        in_specs=[pl.BlockSpec(memory_space=pltpu.SMEM),
                  pl.BlockSpec((1, ATT_BLOCK, 2 * ATT_BLOCK), lambda g: (g, 0, 0)),
                  pl.BlockSpec((1, ATT_BLOCK, 2 * ATT_BLOCK), lambda g: (g, 0, 0))],
        out_specs=pl.BlockSpec((1, HEADS_PER_GROUP, ATT_BLOCK, 2 * ATT_BLOCK),
                               lambda g: (g, 0, 0, 0)),
        out_shape=jax.ShapeDtypeStruct((N_DIL, HEADS_PER_GROUP, ATT_BLOCK, 2 * ATT_BLOCK), f32),
        compiler_params=_params(("arbitrary",)),
        name="relbias",
    )(rel_bias, bucket, band)


ATT_MAX_STEP_BLOCKS = 8


def _attn_kernel(nres, nblk, q_ref, kp_ref, kc_ref, vp_ref, vc_ref, bias_ref, o_ref, lse_ref):
    first = pl.program_id(2) == 0
    lane = lax.broadcasted_iota(i32, (ATT_BLOCK, LANES), 1)
    lo_half = lane < HEAD_DIM
    prev_cols = lax.broadcasted_iota(i32, (ATT_BLOCK, 2 * ATT_BLOCK), 1) < ATT_BLOCK
    no_prev = jnp.logical_and(first, prev_cols)
    nt = (((1,), (1,)), ((), ()))
    ones = jnp.ones((2 * ATT_BLOCK, LANES), bf16)
    n_slab = ATT_WIDTH // LANES
    blocks = [(res, i) for res in range(nres) for i in range(nblk)]
    logits, v_ext = [], []
    for res, i in blocks:
        cur = slice(i * ATT_BLOCK, (i + 1) * ATT_BLOCK)
        prv = slice((i - 1) * ATT_BLOCK, i * ATT_BLOCK)
        for j in range(n_slab):
            sl = slice(j * LANES, (j + 1) * LANES)
            q = q_ref[0, res, cur, sl] * (HEAD_DIM ** -0.5)
            k_prev = kp_ref[0, res, :, sl] if i == 0 else kc_ref[0, res, prv, sl]
            v_prev = vp_ref[0, res, :, sl] if i == 0 else vc_ref[0, res, prv, sl]
            k_cat = jnp.concatenate([k_prev, kc_ref[0, res, cur, sl]], axis=0)
            v_cat = jnp.concatenate([v_prev, vc_ref[0, res, cur, sl]], axis=0)
            v_ext.append(jnp.concatenate([v_cat, ones], axis=1))
            for hh in range(2):
                qm = jnp.where(lo_half if hh == 0 else jnp.logical_not(lo_half), q, 0.0).astype(bf16)
                lg_h = lax.dot_general(qm, k_cat, nt, preferred_element_type=f32) + bias_ref[0, 2 * j + hh]
                logits.append(jnp.where(no_prev, NEG_INF, lg_h) if i == 0 else lg_h)
    rows_per_block = HEADS_PER_GROUP * ATT_BLOCK
    lg = jnp.concatenate(logits, axis=0)
    m = jnp.max(lg, axis=-1, keepdims=True)
    p = jnp.exp(lg - m).astype(bf16)
    for b, (res, i) in enumerate(blocks):
        cur = slice(i * ATT_BLOCK, (i + 1) * ATT_BLOCK)
        lse_tile = jnp.zeros((ATT_BLOCK, LANES), f32)
        for j in range(n_slab):
            outs = []
            for hh in range(2):
                h = 2 * j + hh
                r0 = b * rows_per_block + h * ATT_BLOCK
                r = jnp.dot(p[r0:r0 + ATT_BLOCK], v_ext[b * n_slab + j], preferred_element_type=f32)
                den = r[:, LANES:]
                outs.append(r[:, :LANES] * (1.0 / den))
                lse_h = m[r0:r0 + ATT_BLOCK] + jnp.log(den)
                lse_tile = jnp.where(lane == h, lse_h, lse_tile)
            o_ref[0, res, cur, j * LANES:(j + 1) * LANES] = jnp.where(lo_half, outs[0], outs[1]).astype(bf16)
        lse_ref[0, res, cur, :] = lse_tile


def _attn_group(qkv_g, bias, g, dil, batch, seq):
    l = seq // dil
    nblk = min(ATT_MAX_STEP_BLOCKS, l // ATT_BLOCK)
    nres = min(dil, ATT_MAX_STEP_BLOCKS // nblk)
    tm = nblk * ATT_BLOCK
    nsteps = l // tm

    def cur(cb):
        return pl.BlockSpec((1, nres, tm, ATT_WIDTH), lambda b, r, n: (b, r, n, cb))

    def prev(cb):
        return pl.BlockSpec((1, nres, ATT_BLOCK, ATT_WIDTH),
                            lambda b, r, n: (b, r, jnp.maximum(n * nblk - 1, 0), cb))

    return pl.pallas_call(
        functools.partial(_attn_kernel, nres, nblk),
        grid=(batch, dil // nres, nsteps),
        in_specs=[cur(0), prev(1), cur(1), prev(2), cur(2),
                  pl.BlockSpec((1, HEADS_PER_GROUP, ATT_BLOCK, 2 * ATT_BLOCK),
                               lambda b, r, n: (g, 0, 0, 0))],
        out_specs=[pl.BlockSpec((1, nres, tm, ATT_WIDTH), lambda b, r, n: (b, r, n, 0)),
                   pl.BlockSpec((1, nres, tm, LANES), lambda b, r, n: (b, r, n, 0))],
        out_shape=[jax.ShapeDtypeStruct((batch, dil, l, ATT_WIDTH), bf16),
                   jax.ShapeDtypeStruct((batch, dil, l, LANES), f32)],
        compiler_params=_params(("arbitrary", "arbitrary", "arbitrary")),
        name=f"attn_g{g}",
    )(qkv_g, qkv_g, qkv_g, qkv_g, qkv_g, bias)


ROW_WORDS = D_MODEL // 2
ROW_SUB = ROW_WORDS // LANES
HI_MASK = -65536


def _pack_rows(x):
    bits = lax.bitcast_convert_type(x.astype(bf16).astype(f32), i32)
    return lax.shift_right_logical(bits[:, :ROW_WORDS], 16) | (bits[:, ROW_WORDS:] & HI_MASK)


def _unpack_rows(words):
    lo = lax.bitcast_convert_type(lax.shift_left(words, 16), f32)
    hi = lax.bitcast_convert_type(words & HI_MASK, f32)
    return jnp.concatenate([lo, hi], axis=1)


def _store_packed(ref, words, n, first_row=0):
    for r in range(ROW_SUB):
        ref[pl.ds(first_row * ROW_SUB + r, n, stride=ROW_SUB), :] = words[:, r * LANES:(r + 1) * LANES]


def _load_packed(ref, first_row, n):
    return jnp.concatenate([ref[pl.ds(first_row * ROW_SUB + r, n, stride=ROW_SUB), :] for r in range(ROW_SUB)],
                           axis=1)


MIX_TM = 256
MIX_SUB = 128


def _split_bf16(x, parts):
    out = []
    for _ in range(parts):
        hi = x.astype(bf16)
        out.append(hi)
        x = x - hi.astype(f32)
    return out


def _mix_kernel(o0_ref, o1_ref, o2_ref, l0_ref, l1_ref, l2_ref, pt1_ref, pt2_ref, ex_ref,
                ya_ref, gt_ref, x_ref,
                g1_ref, sc2_ref, sh2_ref, wa_ref, wb_ref, wo_ref, ln1g_ref, ln1b_ref,
                wrc_ref, br_ref, tri_ref,
                x1_ref, h2_ref, route_ref, rw_ref, cnt_ref, run_ref, xr_ref):
    step = pl.program_id(0)

    @pl.when(step == 0)
    def _():
        run_ref[...] = jnp.zeros_like(run_ref)
        xr_ref[...] = jnp.zeros_like(xr_ref)

    def back_rows(r0):
        rows = slice(r0, r0 + MIX_SUB)
        x1 = _ln(xr_ref[rows, :]) * ln1g_ref[...] + ln1b_ref[...]
        x1_ref[rows, :] = x1
        h2 = _ln(x1) * (1.0 + sc2_ref[0]) + sh2_ref[0]
        _store_packed(h2_ref, _pack_rows(h2), MIX_SUB, r0)
        h_hi, h_lo = _split_bf16(h2, 2)
        hi_both = jnp.dot(h_hi, wrc_ref[...], preferred_element_type=f32)
        return (hi_both[:, :LANES]
                + (hi_both[:, LANES:] + jnp.dot(h_lo, wrc_ref[:, :LANES], preferred_element_type=f32))
                ) + br_ref[...]

    def front_rows(r0):
        rows = slice(r0, r0 + MIX_SUB)
        os_, ls_ = [o0_ref[0, 0, rows, :].astype(f32)], [l0_ref[0, 0, rows, :]]
        for o_ref, l_ref, pt_ref in ((o1_ref, l1_ref, pt1_ref), (o2_ref, l2_ref, pt2_ref)):
            pt = pt_ref[rows, :]
            os_.append(jnp.dot(pt, o_ref[0].reshape(MIX_TM, ATT_WIDTH), preferred_element_type=f32))
            parts = [jnp.dot(pt, part, preferred_element_type=f32)
                     for part in _split_bf16(l_ref[0].reshape(MIX_TM, LANES), 3)]
            ls_.append((parts[0] + parts[1]) + parts[2])
        lm = jnp.maximum(jnp.maximum(ls_[0], ls_[1]), ls_[2])
        es = [jnp.exp(lse - lm) for lse in ls_]
        inv = 1.0 / (es[0] + es[1] + es[2])
        yb = jnp.zeros((MIX_SUB, ATT_WIDTH), f32)
        for e, o in zip(es, os_):
            w_parts = jnp.concatenate(_split_bf16(e * inv, 2), axis=1)
            yb = yb + jnp.dot(w_parts, ex_ref[...], preferred_element_type=f32) * o
        a = jnp.dot(ya_ref[rows, :], wa_ref[...], preferred_element_type=f32)
        b = jnp.dot(yb.astype(bf16), wb_ref[...], preferred_element_type=f32)
        merged = gt_ref[rows, :D_MODEL].astype(f32) * a + gt_ref[rows, D_MODEL:].astype(f32) * b
        mix = jnp.dot(merged.astype(bf16), wo_ref[...], preferred_element_type=f32)
        xr_ref[rows, :] = DN_ALPHA * x_ref[rows, :] + g1_ref[0] * mix

    logit_parts = []
    for r0 in range(0, MIX_TM, MIX_SUB):
        logit_parts.append(back_rows(r0))
        front_rows(r0)
    logits = jnp.concatenate(logit_parts, axis=0)
    lane = lax.broadcasted_iota(i32, (MIX_TM, LANES), 1)
    logits = jnp.where(lane < N_EXPERTS, logits, -jnp.inf)
    lane_f = lane.astype(f32)
    vals, idxs = [], []
    for _k in range(TOP_K):
        m = jnp.max(logits, axis=-1, keepdims=True)
        vals.append(m)
        idxs.append(jnp.min(jnp.where(logits == m, lane_f, float(LANES)), axis=-1, keepdims=True).astype(i32))
        logits = jnp.where(lane == idxs[-1], -jnp.inf, logits)
    exps = [jnp.exp(v - vals[0]) for v in vals]
    den = exps[0] + exps[1] + exps[2] + exps[3]
    wts = [e / den for e in exps]
    hits = [lane == idx for idx in idxs]
    counted = jnp.where(step > 0, 1.0, 0.0)
    onehot = jnp.zeros((MIX_TM, LANES), f32)
    for hit in hits:
        onehot = onehot + jnp.where(hit, counted, 0.0)
    prefix = jnp.dot(tri_ref[...], onehot.astype(bf16), preferred_element_type=f32) + run_ref[...]
    route = jnp.zeros((MIX_TM, LANES), i32)
    rw = jnp.zeros((MIX_TM, LANES), f32)
    for k in range(TOP_K):
        rank = jnp.sum(jnp.where(hits[k], prefix, 0.0), axis=-1, keepdims=True).astype(i32)
        route = jnp.where(lane == k, idxs[k], route)
        route = jnp.where(lane == TOP_K + k, rank, route)
        rw = jnp.where(lane == k, wts[k], rw)
    route_ref[...] = route
    rw_ref[...] = rw
    run = run_ref[...] + jnp.sum(onehot, axis=0, keepdims=True)
    run_ref[...] = run
    cnt_ref[...] = jnp.broadcast_to(run, cnt_ref.shape)


def _mix(os_, ls_, perms_t, expand, ya, gates, x2, g1, sc2, sh2, wa, wb, wo, ln1g, ln1b, wr_parts, br, tri, seq):
    t = x2.shape[0]
    nb = t // MIX_TM
    per_b = seq // MIX_TM
    cur = lambda i: jnp.minimum(i, nb - 1)
    prv = lambda i: jnp.maximum(i - 1, 0)
    row = lambda w: pl.BlockSpec((MIX_TM, w), lambda i: (cur(i), 0))
    out_row = lambda w: pl.BlockSpec((MIX_TM, w), lambda i: (prv(i), 0))
    const = lambda s: pl.BlockSpec(s, lambda i: tuple(0 for _ in s))
    mod_cur = pl.BlockSpec((1, 1, D_MODEL), lambda i: (cur(i) // per_b, 0, 0))
    mod_prv = pl.BlockSpec((1, 1, D_MODEL), lambda i: (prv(i) // per_b, 0, 0))
    grp = lambda w: [pl.BlockSpec((1, dil, MIX_TM // dil, w), lambda i: (cur(i) // per_b, 0, cur(i) % per_b, 0))
                     for _win, dil in DIL_PAIRS]
    return pl.pallas_call(
        _mix_kernel,
        grid=(nb + 1,),
        in_specs=grp(ATT_WIDTH) + grp(LANES) + [
                  const((MIX_TM, MIX_TM)), const((MIX_TM, MIX_TM)), const((2 * LANES, ATT_WIDTH)),
                  row(GM_WIDTH), row(GATE_COLS), row(D_MODEL),
                  mod_cur, mod_prv, mod_prv,
                  const((GM_WIDTH, D_MODEL)), const((ATT_WIDTH, D_MODEL)), const((D_MODEL, D_MODEL)),
                  const((1, D_MODEL)), const((1, D_MODEL)),
                  const((D_MODEL, 2 * LANES)), const((1, LANES)), const((MIX_TM, MIX_TM))],
        out_specs=[out_row(D_MODEL), pl.BlockSpec((MIX_TM * ROW_SUB, LANES), lambda i: (prv(i), 0)),
                   out_row(LANES), out_row(LANES), const((8, LANES))],
        out_shape=[jax.ShapeDtypeStruct((t, D_MODEL), f32),
                   jax.ShapeDtypeStruct((t * ROW_SUB, LANES), i32),
                   jax.ShapeDtypeStruct((t, LANES), i32),
                   jax.ShapeDtypeStruct((t, LANES), f32),
                   jax.ShapeDtypeStruct((8, LANES), f32)],
        scratch_shapes=[pltpu.VMEM((1, LANES), f32), pltpu.VMEM((MIX_TM, D_MODEL), f32)],
        compiler_params=_params(("arbitrary",)),
        name="mix",
    )(*os_, *ls_, perms_t[1], perms_t[2], expand, ya, gates, x2, g1, sc2, sh2, wa, wb, wo,
      ln1g, ln1b, wr_parts, br, tri)


DISP_TM = 512
MOE_TM = 512


def _dispatch_kernel(pends_ref, pcnt_ref, nused_ref, dest_ref, h2p_ref, xs_hbm, zbuf, sem, zsem):
    i = pl.program_id(0)
    ntile = xs_hbm.shape[0] // (MOE_TM * ROW_SUB)

    def zero_tile(first_row):
        return pltpu.make_async_copy(
            zbuf, xs_hbm.at[pl.ds(pl.multiple_of(first_row * ROW_SUB, MOE_TM * ROW_SUB), MOE_TM * ROW_SUB)], zsem)

    def for_each_zero_tile(fn):
        for e in range(N_EXPERTS):
            pl.when(pcnt_ref[e] > 0)(functools.partial(fn, lambda e=e: zero_tile(pends_ref[e] - MOE_TM)))
        for k in range(N_EXPERTS):
            tile = nused_ref[0] + k
            pl.when(tile < ntile)(functools.partial(fn, lambda tile=tile: zero_tile(tile * MOE_TM)))

    @pl.when(i == 0)
    def _():
        zbuf[...] = jnp.zeros_like(zbuf)
        for_each_zero_tile(lambda mk: mk().start())
        for_each_zero_tile(lambda mk: mk().wait())

    def row_copy(k, r):
        d = dest_ref[0, k, r]
        return pltpu.make_async_copy(h2p_ref.at[pl.ds(r * ROW_SUB, ROW_SUB)],
                                     xs_hbm.at[pl.ds(pl.multiple_of(d * ROW_SUB, ROW_SUB), ROW_SUB)], sem)

    for r in range(DISP_TM):
        for k in range(TOP_K):
            row_copy(k, r).start(priority=k % 2)
    for k in range(TOP_K):
        pltpu.make_async_copy(h2p_ref, xs_hbm.at[pl.ds(0, DISP_TM * ROW_SUB)], sem).wait()


def _dispatch(pends, pcounts, n_used, dest3, h2p, ntile):
    t = h2p.shape[0] // ROW_SUB
    grid_spec = pltpu.PrefetchScalarGridSpec(
        num_scalar_prefetch=3,
        grid=(t // DISP_TM,),
        in_specs=[pl.BlockSpec((1, TOP_K, DISP_TM), lambda i, *_: (i, 0, 0), memory_space=pltpu.SMEM),
                  pl.BlockSpec((DISP_TM * ROW_SUB, LANES), lambda i, *_: (i, 0))],
        out_specs=pl.BlockSpec(memory_space=pl.ANY),
        scratch_shapes=[pltpu.VMEM((MOE_TM * ROW_SUB, LANES), i32),
                        pltpu.SemaphoreType.DMA(()),
                        pltpu.SemaphoreType.DMA(())],
    )
    return pl.pallas_call(
        _dispatch_kernel,
        grid_spec=grid_spec,
        out_shape=jax.ShapeDtypeStruct((ntile * MOE_TM * ROW_SUB, LANES), i32),
        compiler_params=_params(("arbitrary",)),
        name="dispatch",
    )(pends, pcounts, n_used, dest3, h2p)


def _moe_kernel(te_ref, first_ref, nexte_ref, wslot_ref, nused_ref,
                xs_ref, wg_hbm, wu_hbm, wd_hbm, bg_ref, bu_ref, bd_ref,
                out_ref, wbuf, wgb, wub, wdb, sem_w):
    j = pl.program_id(0)

    def weight_copies(e, ws):
        return [pltpu.make_async_copy(w.at[e], wbuf.at[ws, k], sem_w.at[ws])
                for k, w in enumerate((wg_hbm, wu_hbm, wd_hbm))]

    @pl.when(j == 0)
    def _():
        for cp in weight_copies(te_ref[0], wslot_ref[0]):
            cp.start()

    @pl.when(first_ref[j] == 1)
    def _():
        ws = wslot_ref[j]
        for cp in weight_copies(te_ref[j], ws):
            cp.wait()
        wgb[...] = wbuf[ws, 0].astype(bf16)
        wub[...] = wbuf[ws, 1].astype(bf16)
        wdb[...] = wbuf[ws, 2].astype(bf16)
        ne = nexte_ref[j]

        @pl.when(ne >= 0)
        def _():
            for cp in weight_copies(ne, 1 - ws):
                cp.start()

    used = j < nused_ref[0]

    @pl.when(used)
    def _():
        xb = _unpack_rows(_load_packed(xs_ref, 0, MOE_TM)).astype(bf16)
        g = jnp.dot(xb, wgb[...], preferred_element_type=f32) + bg_ref[0]
        u = jnp.dot(xb, wub[...], preferred_element_type=f32) + bu_ref[0]
        g = jnp.minimum(g, SWIGLU_LIMIT)
        u = jnp.clip(u, -SWIGLU_LIMIT, SWIGLU_LIMIT)
        act = (u + 1.0) * (g * jax.nn.sigmoid(SWIGLU_ALPHA * g))
        y = jnp.dot(act.astype(bf16), wdb[...], preferred_element_type=f32) + bd_ref[0]
        _store_packed(out_ref, _pack_rows(y), MOE_TM)

    @pl.when(jnp.logical_not(used))
    def _():
        out_ref[...] = jnp.zeros_like(out_ref)


def _moe(tile_e, tile_first, next_e, wslot, n_used, xs, w_gate, b_gate, w_up, b_up, w_down, b_down):
    ntile = tile_e.shape[0]
    bspec = pl.BlockSpec((1, 1, D_MODEL), lambda j, te, *_: (te[j], 0, 0))
    hbm = pl.BlockSpec(memory_space=pl.ANY)
    grid_spec = pltpu.PrefetchScalarGridSpec(
        num_scalar_prefetch=5,
        grid=(ntile,),
        in_specs=[pl.BlockSpec((MOE_TM * ROW_SUB, LANES),
                               lambda j, te, fi, ne, ws, nu: (jnp.minimum(j, nu[0] - 1), 0)),
                  hbm, hbm, hbm, bspec, bspec, bspec],
        out_specs=pl.BlockSpec((MOE_TM * ROW_SUB, LANES), lambda j, *_: (j, 0)),
        scratch_shapes=[pltpu.VMEM((2, 3, D_MODEL, D_MODEL), f32),
                        pltpu.VMEM((D_MODEL, D_MODEL), bf16),
                        pltpu.VMEM((D_MODEL, D_MODEL), bf16),
                        pltpu.VMEM((D_MODEL, D_MODEL), bf16),
                        pltpu.SemaphoreType.DMA((2,))],
    )
    return pl.pallas_call(
        _moe_kernel,
        grid_spec=grid_spec,
        out_shape=jax.ShapeDtypeStruct((ntile * MOE_TM * ROW_SUB, LANES), i32),
        compiler_params=_params(("arbitrary",)),
        name="moe",
    )(tile_e, tile_first, next_e, wslot, n_used, xs, w_gate, w_up, w_down, b_gate, b_up, b_down)


CB_TM = DISP_TM


def _combine_kernel(dcur_ref, dnxt_ref, yb_hbm, rw_ref, x1_ref, g2_ref, lng_ref, lnb_ref, out_ref,
                    ybuf0, ybuf1, sem):
    i = pl.program_id(0)
    last = pl.num_programs(0) - 1
    ybufs = (ybuf0, ybuf1)

    def row_copy(d, k, r, s):
        return pltpu.make_async_copy(
            yb_hbm.at[pl.ds(pl.multiple_of(d * ROW_SUB, ROW_SUB), ROW_SUB)],
            ybufs[s].at[pl.ds(pl.multiple_of((k * CB_TM + r) * ROW_SUB, ROW_SUB), ROW_SUB)],
            sem.at[s])

    @pl.when(i == 0)
    def _():
        for k in range(TOP_K):
            def body(r, c, k=k):
                row_copy(dcur_ref[0, k, r], k, r, 0).start()
                return c
            lax.fori_loop(0, CB_TM, body, 0, unroll=8)

    for s in range(2):
        @pl.when(i % 2 == s)
        def _(s=s):
            pltpu.make_async_copy(yb_hbm.at[pl.ds(0, TOP_K * CB_TM * ROW_SUB)], ybufs[s], sem.at[s]).wait()

            @pl.when(i < last)
            def _():
                for k in range(TOP_K):
                    for r in range(CB_TM):
                        row_copy(dnxt_ref[0, k, r], k, r, 1 - s).start(priority=r % 2)

            parts = [_unpack_rows(_load_packed(ybufs[s], k * CB_TM, CB_TM)) * rw_ref[:, k:k + 1]
                     for k in range(TOP_K)]
            y = (parts[0] + parts[1]) + (parts[2] + parts[3])
            out_ref[...] = _ln(DN_ALPHA * x1_ref[...] + g2_ref[0] * y) * lng_ref[...] + lnb_ref[...]


def _combine(dest3, yb, rw, x1, g2, ln2g, ln2b, seq):
    t = x1.shape[0]
    nb = t // CB_TM
    per_b = seq // CB_TM
    return pl.pallas_call(
        _combine_kernel,
        grid=(nb,),
        in_specs=[pl.BlockSpec((1, TOP_K, CB_TM), lambda i: (i, 0, 0), memory_space=pltpu.SMEM),
                  pl.BlockSpec((1, TOP_K, CB_TM), lambda i: (jnp.minimum(i + 1, nb - 1), 0, 0),
                               memory_space=pltpu.SMEM),
                  pl.BlockSpec(memory_space=pl.ANY),
                  pl.BlockSpec((CB_TM, LANES), lambda i: (i, 0)),
                  pl.BlockSpec((CB_TM, D_MODEL), lambda i: (i, 0)),
                  pl.BlockSpec((1, 1, D_MODEL), lambda i: (i // per_b, 0, 0)),
                  pl.BlockSpec((1, D_MODEL), lambda i: (0, 0)),
                  pl.BlockSpec((1, D_MODEL), lambda i: (0, 0))],
        out_specs=pl.BlockSpec((CB_TM, D_MODEL), lambda i: (i, 0)),
        out_shape=jax.ShapeDtypeStruct((t, D_MODEL), f32),
        scratch_shapes=[pltpu.VMEM((TOP_K * CB_TM * ROW_SUB, LANES), i32),
                        pltpu.VMEM((TOP_K * CB_TM * ROW_SUB, LANES), i32),
                        pltpu.SemaphoreType.DMA((2,))],
        compiler_params=_params(("arbitrary",)),
        name="combine",
    )(dest3, dest3, yb, rw, x1, g2, ln2g, ln2b)


def _t5_bucket(dist):
    d = dist.astype(f32)
    large = REL_MAX_EXACT + jnp.log(jnp.maximum(d, float(REL_MAX_EXACT)) / REL_MAX_EXACT) / math.log(
        REL_MAX_DIST / REL_MAX_EXACT) * (REL_BUCKETS - REL_MAX_EXACT)
    large = jnp.minimum(large.astype(i32), REL_BUCKETS - 1)
    return jnp.where(dist < REL_MAX_EXACT, dist, large)


def _bias_indices():
    qi = jnp.arange(ATT_BLOCK)[:, None]
    ki = jnp.arange(2 * ATT_BLOCK)[None, :]
    didx = qi + ATT_BLOCK - ki
    buckets, bands = [], []
    for win, dil in DIL_PAIRS:
        buckets.append(_t5_bucket(jnp.clip(didx, 0, None) * dil))
        bands.append(((didx >= 0) & (didx <= win // dil)).astype(i32))
    return jnp.stack(buckets).astype(i32), jnp.stack(bands)


def _residue_perm(tm, dil):
    n = tm // dil
    dst = np.arange(tm)
    src = (dst % n) * dil + dst // n
    return src[:, None] == np.arange(tm)[None, :]


def kernel(x, c, w_ada, b_ada, w_in, gm_ln_g, gm_ln_b, gm_w_s, gm_b_s, w_branch_a, w_branch_b, w_out,
           rel_bias, ln1_g, ln1_b, w_router, b_router, w_gate, b_gate, w_up, b_up, w_down, b_down,
           ln2_g, ln2_b):
    batch, seq, _ = x.shape
    t = batch * seq
    l = 0
    x2 = x.reshape(t, D_MODEL)

    c8 = jnp.pad(c, ((0, 8 - batch), (0, 0)))
    mod = _adaln(c8, w_ada[l], b_ada[l][None, :])[:batch]
    sh1, sc1, g1, sh2, sc2, g2 = [m[:, None, :] for m in jnp.split(mod, 6, axis=-1)]

    perms = [jnp.asarray(_residue_perm(IN_TM, dil), bf16) for _win, dil in DIL_PAIRS]
    bs_full = jnp.repeat(gm_b_s[l].T, GM_WIDTH // GM_GROUPS, axis=1)
    ya, gates, *qkvs = _inproj(x2, sc1, sh1, w_in[l].astype(bf16), perms,
                               gm_ln_g[l][None, :], gm_ln_b[l][None, :], gm_w_s[l], bs_full, batch, seq)

    bucket, band = _bias_indices()
    bias = _relbias(rel_bias, bucket, band)
    os_, ls_ = [], []
    for g, (_win, dil) in enumerate(DIL_PAIRS):
        o, lse = _attn_group(qkvs[g], bias, g, dil, batch, seq)
        os_.append(o)
        ls_.append(lse)

    wr = jnp.pad(w_router[l], ((0, 0), (0, LANES - N_EXPERTS)))
    wr_hi = wr.astype(bf16)
    wr_parts = jnp.concatenate([wr_hi, (wr - wr_hi.astype(f32)).astype(bf16)], axis=1)
    br = jnp.pad(b_router[l], (0, LANES - N_EXPERTS))[None, :]
    tri = jnp.asarray(np.arange(MIX_TM)[None, :] < np.arange(MIX_TM)[:, None], bf16)
    perms_t = [jnp.asarray(_residue_perm(MIX_TM, dil).T, bf16) for _win, dil in DIL_PAIRS]
    expand = np.arange(LANES)[:, None] == np.arange(ATT_WIDTH)[None, :] // HEAD_DIM
    expand = jnp.asarray(np.concatenate([expand, expand], axis=0), bf16)
    x1, h2, route, rw, cnt = _mix(
        os_, ls_, perms_t, expand, ya, gates, x2, g1, sc2, sh2,
        w_branch_a[l].astype(bf16), w_branch_b[l].astype(bf16), w_out[l].astype(bf16),
        ln1_g[l][None, :], ln1_b[l][None, :], wr_parts, br, tri, seq)

    top_e = route[:, :TOP_K]
    rank = route[:, TOP_K:2 * TOP_K]
    counts = cnt[0, :N_EXPERTS].astype(i32)
    pcounts = (counts + MOE_TM - 1) // MOE_TM * MOE_TM
    experts = jnp.arange(N_EXPERTS, dtype=i32)
    upto = experts[None, :] <= experts[:, None]
    pends = jnp.sum(jnp.where(upto, pcounts[None, :], 0), axis=1)
    pstarts = pends - pcounts
    dest = jnp.sum(jnp.where(top_e[:, :, None] == experts, pstarts, 0), axis=-1) + rank
    ntile = t * TOP_K // MOE_TM + N_EXPERTS
    n_used = (pends[-1] // MOE_TM).reshape(1)
    tile_idx = jnp.minimum(jnp.arange(ntile, dtype=i32), n_used - 1)
    tile_e = jnp.sum((pends[None, :] <= (tile_idx * MOE_TM)[:, None]).astype(i32), axis=1)
    tile_first = jnp.concatenate([jnp.ones((1,), i32), (tile_e[1:] != tile_e[:-1]).astype(i32)])
    nonempty = counts > 0
    later = jnp.logical_and(experts[None, :] > experts[:, None], nonempty[None, :])
    next_nonempty = jnp.min(jnp.where(later, experts[None, :], N_EXPERTS), axis=1)
    next_nonempty = jnp.where(next_nonempty >= N_EXPERTS, -1, next_nonempty)
    expert_slot = (jnp.sum(jnp.logical_and(upto, nonempty[None, :]).astype(i32), axis=1) - 1) % 2
    of_tile = tile_e[:, None] == experts[None, :]
    tile_next = jnp.sum(jnp.where(of_tile, next_nonempty[None, :], 0), axis=1)
    tile_slot = jnp.sum(jnp.where(of_tile, expert_slot[None, :], 0), axis=1)

    dest3 = dest.reshape(t // DISP_TM, DISP_TM, TOP_K).transpose(0, 2, 1)
    xs = _dispatch(pends, pcounts, n_used, dest3, h2, ntile)
    yb = _moe(tile_e, tile_first, tile_next, tile_slot, n_used, xs,
              w_gate[l], b_gate[l][:, None, :], w_up[l], b_up[l][:, None, :],
              w_down[l], b_down[l][:, None, :])
    out = _combine(dest3, yb, rw, x1, g2, ln2_g[l][None, :], ln2_b[l][None, :], seq)
    return out.reshape(batch, seq, D_MODEL)
```

```python
import functools
import math

import numpy as np
import jax
import jax.numpy as jnp
from jax import lax
from jax.experimental import pallas as pl
from jax.experimental.pallas import tpu as pltpu

f32 = jnp.float32
bf16 = jnp.bfloat16
i32 = jnp.int32

D_MODEL = 1024
GM_WIDTH = 512
GM_GROUPS = 8
GM_CHUNK = 128
DIL_PAIRS = ((128, 1), (512, 4), (2048, 16))
N_DIL = 3
HEADS_PER_GROUP = 8
HEAD_DIM = 64
ATT_WIDTH = 512
ATT_BLOCK = 128
NEG_INF = -1e30
REL_BUCKETS = 32
REL_MAX_EXACT = 16
REL_MAX_DIST = 2048
N_EXPERTS = 32
TOP_K = 4
SWIGLU_LIMIT = 7.0
SWIGLU_ALPHA = 1.702
DEPTH = 1
DN_ALPHA = (2 * DEPTH) ** 0.25
LN_EPS = 1e-5
UV_COLS = 2 * GM_WIDTH
QKV_COLS = N_DIL * 3 * ATT_WIDTH
GATE_COLS = 2 * D_MODEL
IN_COLS = UV_COLS + QKV_COLS + GATE_COLS

LANES = 128
VMEM_LIMIT = 56 * 1024 * 1024


def _ln(x):
    mu = jnp.mean(x, axis=-1, keepdims=True)
    xc = x - mu
    var = jnp.mean(xc * xc, axis=-1, keepdims=True)
    return xc * lax.rsqrt(var + LN_EPS)


def _params(sem, vmem=VMEM_LIMIT):
    return pltpu.CompilerParams(dimension_semantics=sem, vmem_limit_bytes=vmem)


def _adaln_kernel(c_ref, w_ref, b_ref, o_ref):
    c = c_ref[...]
    s = c * jax.nn.sigmoid(c)
    o_ref[...] = jnp.dot(s, w_ref[...], preferred_element_type=f32,
                         precision=lax.Precision.HIGHEST) + b_ref[...]


def _adaln(c8, w_ada, b_ada):
    n = w_ada.shape[1] // D_MODEL
    return pl.pallas_call(
        _adaln_kernel,
        grid=(n,),
        in_specs=[pl.BlockSpec((8, D_MODEL), lambda j: (0, 0)),
                  pl.BlockSpec((D_MODEL, D_MODEL), lambda j: (0, j)),
                  pl.BlockSpec((1, D_MODEL), lambda j: (0, j))],
        out_specs=pl.BlockSpec((8, D_MODEL), lambda j: (0, j)),
        out_shape=jax.ShapeDtypeStruct((8, w_ada.shape[1]), f32),
        compiler_params=_params(("arbitrary",)),
        name="adaln",
    )(c8, w_ada, b_ada)


IN_TM = 256
IN_CW = 512
GRP_COLS = 3 * ATT_WIDTH


def _inproj_kernel(x_ref, sc_ref, sh_ref, w_ref, p1_ref, p2_ref, gm_g_ref, gm_b_ref, ws_ref, bs_ref,
                   ya_ref, gt_ref, qkv0_ref, qkv1_ref, qkv2_ref):
    xn = _ln(x_ref[...])
    h = (xn * (1.0 + sc_ref[0]) + sh_ref[0]).astype(bf16)
    hp = [h,
          jnp.dot(p1_ref[...], h, preferred_element_type=f32).astype(bf16),
          jnp.dot(p2_ref[...], h, preferred_element_type=f32).astype(bf16)]
    u_act, v_act = [jax.nn.gelu(jnp.dot(h, w_ref[:, c0:c0 + GM_WIDTH], preferred_element_type=f32))
                    for c0 in range(0, UV_COLS, GM_WIDTH)]
    row = lax.broadcasted_iota(i32, (GM_CHUNK, GM_CHUNK), 0)
    col = lax.broadcasted_iota(i32, (GM_CHUNK, GM_CHUNK), 1)
    first_half = lax.broadcasted_iota(i32, (GM_CHUNK, LANES), 1) < (GM_WIDTH // GM_GROUPS)
    ws = [jnp.where(col <= row, ws_ref[g], 0.0).astype(bf16) for g in range(GM_GROUPS)]

    def gate_chunk(r0):
        vn = (_ln(v_act[r0:r0 + GM_CHUNK, :]) * gm_g_ref[...] + gm_b_ref[...]).astype(bf16)
        for j in range(GM_WIDTH // LANES):
            sl = slice(j * LANES, (j + 1) * LANES)
            s_lo = jnp.dot(ws[2 * j], vn[:, sl], preferred_element_type=f32)
            s_hi = jnp.dot(ws[2 * j + 1], vn[:, sl], preferred_element_type=f32)
            s = jnp.where(first_half, s_lo, s_hi) + bs_ref[:, sl]
            ya_ref[r0:r0 + GM_CHUNK, sl] = (u_act[r0:r0 + GM_CHUNK, sl] * s).astype(bf16)

    pending = list(range(0, IN_TM, GM_CHUNK))
    for g, (qref, (_win, dil)) in enumerate(zip((qkv0_ref, qkv1_ref, qkv2_ref), DIL_PAIRS)):
        n = IN_TM // dil
        for q0 in range(0, GRP_COLS, IN_CW):
            c0 = UV_COLS + g * GRP_COLS + q0
            acc = jnp.dot(hp[g], w_ref[:, c0:c0 + IN_CW], preferred_element_type=f32).astype(bf16)
            for rho in range(dil):
                qref[0, rho, :, q0:q0 + IN_CW] = acc[rho * n:(rho + 1) * n, :]
        if pending:
            gate_chunk(pending.pop(0))
    while pending:
        gate_chunk(pending.pop(0))
    for g0 in range(0, GATE_COLS, IN_CW):
        c0 = UV_COLS + QKV_COLS + g0
        acc = jnp.dot(h, w_ref[:, c0:c0 + IN_CW], preferred_element_type=f32)
        gt_ref[:, g0:g0 + IN_CW] = jax.nn.sigmoid(acc).astype(bf16)


def _inproj(x2, sc1, sh1, w_in_bf, perms, gm_g, gm_b, w_s, bs_full, batch, seq):
    t = x2.shape[0]
    per_b = seq // IN_TM
    qkv_specs, qkv_shapes = [], []
    for _win, dil in DIL_PAIRS:
        n = IN_TM // dil
        qkv_specs.append(pl.BlockSpec((1, dil, n, GRP_COLS), lambda i: (i // per_b, 0, i % per_b, 0)))
        qkv_shapes.append(jax.ShapeDtypeStruct((batch, dil, seq // dil, GRP_COLS), bf16))
    return pl.pallas_call(
        _inproj_kernel,
        grid=(t // IN_TM,),
        in_specs=[pl.BlockSpec((IN_TM, D_MODEL), lambda i: (i, 0)),
                  pl.BlockSpec((1, 1, D_MODEL), lambda i: (i // per_b, 0, 0)),
                  pl.BlockSpec((1, 1, D_MODEL), lambda i: (i // per_b, 0, 0)),
                  pl.BlockSpec((D_MODEL, IN_COLS), lambda i: (0, 0)),
                  pl.BlockSpec((IN_TM, IN_TM), lambda i: (0, 0)),
                  pl.BlockSpec((IN_TM, IN_TM), lambda i: (0, 0)),
                  pl.BlockSpec((1, GM_WIDTH), lambda i: (0, 0)),
                  pl.BlockSpec((1, GM_WIDTH), lambda i: (0, 0)),
                  pl.BlockSpec((GM_GROUPS, GM_CHUNK, GM_CHUNK), lambda i: (0, 0, 0)),
                  pl.BlockSpec((GM_CHUNK, GM_WIDTH), lambda i: (0, 0))],
        out_specs=[pl.BlockSpec((IN_TM, GM_WIDTH), lambda i: (i, 0)),
                   pl.BlockSpec((IN_TM, GATE_COLS), lambda i: (i, 0))] + qkv_specs,
        out_shape=[jax.ShapeDtypeStruct((t, GM_WIDTH), bf16),
                   jax.ShapeDtypeStruct((t, GATE_COLS), bf16)] + qkv_shapes,
        compiler_params=_params(("arbitrary",)),
        name="inproj",
    )(x2, sc1, sh1, w_in_bf, perms[1], perms[2], gm_g, gm_b, w_s, bs_full)


def _relbias_kernel(tab_ref, bucket_ref, band_ref, out_ref):
    g = pl.program_id(0)
    bk = bucket_ref[0]
    band = band_ref[0] > 0
    for h in range(HEADS_PER_GROUP):
        acc = jnp.zeros((ATT_BLOCK, 2 * ATT_BLOCK), f32)
        for b in range(REL_BUCKETS):
            acc = jnp.where(bk == b, tab_ref[b, g * HEADS_PER_GROUP + h], acc)
        out_ref[0, h] = jnp.where(band, acc, NEG_INF)


def _relbias(rel_bias, bucket, band):
    return pl.pallas_call(
        _relbias_kernel,
        grid=(N_DIL,),
        in_specs=[pl.BlockSpec(memory_space=pltpu.SMEM),
                  pl.BlockSpec((1, ATT_BLOCK, 2 * ATT_BLOCK), lambda g: (g, 0, 0)),
                  pl.BlockSpec((1, ATT_BLOCK, 2 * ATT_BLOCK), lambda g: (g, 0, 0))],
        out_specs=pl.BlockSpec((1, HEADS_PER_GROUP, ATT_BLOCK, 2 * ATT_BLOCK),
                               lambda g: (g, 0, 0, 0)),
        out_shape=jax.ShapeDtypeStruct((N_DIL, HEADS_PER_GROUP, ATT_BLOCK, 2 * ATT_BLOCK), f32),
        compiler_params=_params(("arbitrary",)),
        name="relbias",
    )(rel_bias, bucket, band)


ATT_MAX_STEP_BLOCKS = 8


def _attn_kernel(nres, nblk, q_ref, kp_ref, kc_ref, vp_ref, vc_ref, bias_ref, o_ref, lse_ref):
    first = pl.program_id(2) == 0
    lane = lax.broadcasted_iota(i32, (ATT_BLOCK, LANES), 1)
    lo_half = lane < HEAD_DIM
    prev_cols = lax.broadcasted_iota(i32, (ATT_BLOCK, 2 * ATT_BLOCK), 1) < ATT_BLOCK
    no_prev = jnp.logical_and(first, prev_cols)
    nt = (((1,), (1,)), ((), ()))
    ones = jnp.ones((2 * ATT_BLOCK, LANES), bf16)
    n_slab = ATT_WIDTH // LANES
    blocks = [(res, i) for res in range(nres) for i in range(nblk)]
    logits, v_ext = [], []
    for res, i in blocks:
        cur = slice(i * ATT_BLOCK, (i + 1) * ATT_BLOCK)
        prv = slice((i - 1) * ATT_BLOCK, i * ATT_BLOCK)
        for j in range(n_slab):
            sl = slice(j * LANES, (j + 1) * LANES)
            q = q_ref[0, res, cur, sl] * (HEAD_DIM ** -0.5)
            k_prev = kp_ref[0, res, :, sl] if i == 0 else kc_ref[0, res, prv, sl]
            v_prev = vp_ref[0, res, :, sl] if i == 0 else vc_ref[0, res, prv, sl]
            k_cat = jnp.concatenate([k_prev, kc_ref[0, res, cur, sl]], axis=0)
            v_cat = jnp.concatenate([v_prev, vc_ref[0, res, cur, sl]], axis=0)
            v_ext.append(jnp.concatenate([v_cat, ones], axis=1))
            for hh in range(2):
                qm = jnp.where(lo_half if hh == 0 else jnp.logical_not(lo_half), q, 0.0).astype(bf16)
                lg_h = lax.dot_general(qm, k_cat, nt, preferred_element_type=f32) + bias_ref[0, 2 * j + hh]
                logits.append(jnp.where(no_prev, NEG_INF, lg_h) if i == 0 else lg_h)
    rows_per_block = HEADS_PER_GROUP * ATT_BLOCK
    lg = jnp.concatenate(logits, axis=0)
    m = jnp.max(lg, axis=-1, keepdims=True)
    p = jnp.exp(lg - m).astype(bf16)
    for b, (res, i) in enumerate(blocks):
        cur = slice(i * ATT_BLOCK, (i + 1) * ATT_BLOCK)
        lse_tile = jnp.zeros((ATT_BLOCK, LANES), f32)
        for j in range(n_slab):
            outs = []
            for hh in range(2):
                h = 2 * j + hh
                r0 = b * rows_per_block + h * ATT_BLOCK
                r = jnp.dot(p[r0:r0 + ATT_BLOCK], v_ext[b * n_slab + j], preferred_element_type=f32)
                den = r[:, LANES:]
                outs.append(r[:, :LANES] * (1.0 / den))
                lse_h = m[r0:r0 + ATT_BLOCK] + jnp.log(den)
                lse_tile = jnp.where(lane == h, lse_h, lse_tile)
            o_ref[0, res, cur, j * LANES:(j + 1) * LANES] = jnp.where(lo_half, outs[0], outs[1]).astype(bf16)
        lse_ref[0, res, cur, :] = lse_tile


def _attn_group(qkv_g, bias, g, dil, batch, seq):
    l = seq // dil
    nblk = min(ATT_MAX_STEP_BLOCKS, l // ATT_BLOCK)
    nres = min(dil, ATT_MAX_STEP_BLOCKS // nblk)
    tm = nblk * ATT_BLOCK
    nsteps = l // tm

    def cur(cb):
        return pl.BlockSpec((1, nres, tm, ATT_WIDTH), lambda b, r, n: (b, r, n, cb))

    def prev(cb):
        return pl.BlockSpec((1, nres, ATT_BLOCK, ATT_WIDTH),
                            lambda b, r, n: (b, r, jnp.maximum(n * nblk - 1, 0), cb))

    return pl.pallas_call(
        functools.partial(_attn_kernel, nres, nblk),
        grid=(batch, dil // nres, nsteps),
        in_specs=[cur(0), prev(1), cur(1), prev(2), cur(2),
                  pl.BlockSpec((1, HEADS_PER_GROUP, ATT_BLOCK, 2 * ATT_BLOCK),
                               lambda b, r, n: (g, 0, 0, 0))],
        out_specs=[pl.BlockSpec((1, nres, tm, ATT_WIDTH), lambda b, r, n: (b, r, n, 0)),
                   pl.BlockSpec((1, nres, tm, LANES), lambda b, r, n: (b, r, n, 0))],
        out_shape=[jax.ShapeDtypeStruct((batch, dil, l, ATT_WIDTH), bf16),
                   jax.ShapeDtypeStruct((batch, dil, l, LANES), f32)],
        compiler_params=_params(("arbitrary", "arbitrary", "arbitrary")),
        name=f"attn_g{g}",
    )(qkv_g, qkv_g, qkv_g, qkv_g, qkv_g, bias)


ROW_WORDS = D_MODEL // 2
ROW_SUB = ROW_WORDS // LANES
HI_MASK = -65536


def _pack_rows(x):
    bits = lax.bitcast_convert_type(x.astype(bf16).astype(f32), i32)
    return lax.shift_right_logical(bits[:, :ROW_WORDS], 16) | (bits[:, ROW_WORDS:] & HI_MASK)


def _unpack_rows(words):
    lo = lax.bitcast_convert_type(lax.shift_left(words, 16), f32)
    hi = lax.bitcast_convert_type(words & HI_MASK, f32)
    return jnp.concatenate([lo, hi], axis=1)


def _store_packed(ref, words, n, first_row=0):
    for r in range(ROW_SUB):
        ref[pl.ds(first_row * ROW_SUB + r, n, stride=ROW_SUB), :] = words[:, r * LANES:(r + 1) * LANES]


def _load_packed(ref, first_row, n):
    return jnp.concatenate([ref[pl.ds(first_row * ROW_SUB + r, n, stride=ROW_SUB), :] for r in range(ROW_SUB)],
                           axis=1)


MIX_TM = 256
MIX_SUB = 128


def _split_bf16(x, parts):
    out = []
    for _ in range(parts):
        hi = x.astype(bf16)
        out.append(hi)
        x = x - hi.astype(f32)
    return out


def _mix_kernel(o0_ref, o1_ref, o2_ref, l0_ref, l1_ref, l2_ref, pt1_ref, pt2_ref, ex_ref,
                ya_ref, gt_ref, x_ref,
                g1_ref, sc2_ref, sh2_ref, wa_ref, wb_ref, wo_ref, ln1g_ref, ln1b_ref,
                wrc_ref, br_ref, tri_ref,
                x1_ref, h2_ref, route_ref, rw_ref, cnt_ref, run_ref, xr_ref):
    step = pl.program_id(0)

    @pl.when(step == 0)
    def _():
        run_ref[...] = jnp.zeros_like(run_ref)
        xr_ref[...] = jnp.zeros_like(xr_ref)

    def back_rows(r0):
        rows = slice(r0, r0 + MIX_SUB)
        x1 = _ln(xr_ref[rows, :]) * ln1g_ref[...] + ln1b_ref[...]
        x1_ref[rows, :] = x1
        h2 = _ln(x1) * (1.0 + sc2_ref[0]) + sh2_ref[0]
        _store_packed(h2_ref, _pack_rows(h2), MIX_SUB, r0)
        h_hi, h_lo = _split_bf16(h2, 2)
        hi_both = jnp.dot(h_hi, wrc_ref[...], preferred_element_type=f32)
        return (hi_both[:, :LANES]
                + (hi_both[:, LANES:] + jnp.dot(h_lo, wrc_ref[:, :LANES], preferred_element_type=f32))
                ) + br_ref[...]

    def front_rows(r0):
        rows = slice(r0, r0 + MIX_SUB)
        os_, ls_ = [o0_ref[0, 0, rows, :].astype(f32)], [l0_ref[0, 0, rows, :]]
        for o_ref, l_ref, pt_ref in ((o1_ref, l1_ref, pt1_ref), (o2_ref, l2_ref, pt2_ref)):
            pt = pt_ref[rows, :]
            os_.append(jnp.dot(pt, o_ref[0].reshape(MIX_TM, ATT_WIDTH), preferred_element_type=f32))
            parts = [jnp.dot(pt, part, preferred_element_type=f32)
                     for part in _split_bf16(l_ref[0].reshape(MIX_TM, LANES), 3)]
            ls_.append((parts[0] + parts[1]) + parts[2])
        lm = jnp.maximum(jnp.maximum(ls_[0], ls_[1]), ls_[2])
        es = [jnp.exp(lse - lm) for lse in ls_]
        inv = 1.0 / (es[0] + es[1] + es[2])
        yb = jnp.zeros((MIX_SUB, ATT_WIDTH), f32)
        for e, o in zip(es, os_):
            w_parts = jnp.concatenate(_split_bf16(e * inv, 2), axis=1)
            yb = yb + jnp.dot(w_parts, ex_ref[...], preferred_element_type=f32) * o
        a = jnp.dot(ya_ref[rows, :], wa_ref[...], preferred_element_type=f32)
        b = jnp.dot(yb.astype(bf16), wb_ref[...], preferred_element_type=f32)
        merged = gt_ref[rows, :D_MODEL].astype(f32) * a + gt_ref[rows, D_MODEL:].astype(f32) * b
        mix = jnp.dot(merged.astype(bf16), wo_ref[...], preferred_element_type=f32)
        xr_ref[rows, :] = DN_ALPHA * x_ref[rows, :] + g1_ref[0] * mix

    logit_parts = []
    for r0 in range(0, MIX_TM, MIX_SUB):
        logit_parts.append(back_rows(r0))
        front_rows(r0)
    logits = jnp.concatenate(logit_parts, axis=0)
    lane = lax.broadcasted_iota(i32, (MIX_TM, LANES), 1)
    logits = jnp.where(lane < N_EXPERTS, logits, -jnp.inf)
    lane_f = lane.astype(f32)
    vals, idxs = [], []
    for _k in range(TOP_K):
        m = jnp.max(logits, axis=-1, keepdims=True)
        vals.append(m)
        idxs.append(jnp.min(jnp.where(logits == m, lane_f, float(LANES)), axis=-1, keepdims=True).astype(i32))
        logits = jnp.where(lane == idxs[-1], -jnp.inf, logits)
    exps = [jnp.exp(v - vals[0]) for v in vals]
    den = exps[0] + exps[1] + exps[2] + exps[3]
    wts = [e / den for e in exps]
    hits = [lane == idx for idx in idxs]
    counted = jnp.where(step > 0, 1.0, 0.0)
    onehot = jnp.zeros((MIX_TM, LANES), f32)
    for hit in hits:
        onehot = onehot + jnp.where(hit, counted, 0.0)
    prefix = jnp.dot(tri_ref[...], onehot.astype(bf16), preferred_element_type=f32) + run_ref[...]
    route = jnp.zeros((MIX_TM, LANES), i32)
    rw = jnp.zeros((MIX_TM, LANES), f32)
    for k in range(TOP_K):
        rank = jnp.sum(jnp.where(hits[k], prefix, 0.0), axis=-1, keepdims=True).astype(i32)
        route = jnp.where(lane == k, idxs[k], route)
        route = jnp.where(lane == TOP_K + k, rank, route)
        rw = jnp.where(lane == k, wts[k], rw)
    route_ref[...] = route
    rw_ref[...] = rw
    run = run_ref[...] + jnp.sum(onehot, axis=0, keepdims=True)
    run_ref[...] = run
    cnt_ref[...] = jnp.broadcast_to(run, cnt_ref.shape)


def _mix(os_, ls_, perms_t, expand, ya, gates, x2, g1, sc2, sh2, wa, wb, wo, ln1g, ln1b, wr_parts, br, tri, seq):
    t = x2.shape[0]
    nb = t // MIX_TM
    per_b = seq // MIX_TM
    cur = lambda i: jnp.minimum(i, nb - 1)
    prv = lambda i: jnp.maximum(i - 1, 0)
    row = lambda w: pl.BlockSpec((MIX_TM, w), lambda i: (cur(i), 0))
    out_row = lambda w: pl.BlockSpec((MIX_TM, w), lambda i: (prv(i), 0))
    const = lambda s: pl.BlockSpec(s, lambda i: tuple(0 for _ in s))
    mod_cur = pl.BlockSpec((1, 1, D_MODEL), lambda i: (cur(i) // per_b, 0, 0))
    mod_prv = pl.BlockSpec((1, 1, D_MODEL), lambda i: (prv(i) // per_b, 0, 0))
    grp = lambda w: [pl.BlockSpec((1, dil, MIX_TM // dil, w), lambda i: (cur(i) // per_b, 0, cur(i) % per_b, 0))
                     for _win, dil in DIL_PAIRS]
    return pl.pallas_call(
        _mix_kernel,
        grid=(nb + 1,),
        in_specs=grp(ATT_WIDTH) + grp(LANES) + [
                  const((MIX_TM, MIX_TM)), const((MIX_TM, MIX_TM)), const((2 * LANES, ATT_WIDTH)),
                  row(GM_WIDTH), row(GATE_COLS), row(D_MODEL),
                  mod_cur, mod_prv, mod_prv,
                  const((GM_WIDTH, D_MODEL)), const((ATT_WIDTH, D_MODEL)), const((D_MODEL, D_MODEL)),
                  const((1, D_MODEL)), const((1, D_MODEL)),
                  const((D_MODEL, 2 * LANES)), const((1, LANES)), const((MIX_TM, MIX_TM))],
        out_specs=[out_row(D_MODEL), pl.BlockSpec((MIX_TM * ROW_SUB, LANES), lambda i: (prv(i), 0)),
                   out_row(LANES), out_row(LANES), const((8, LANES))],
        out_shape=[jax.ShapeDtypeStruct((t, D_MODEL), f32),
                   jax.ShapeDtypeStruct((t * ROW_SUB, LANES), i32),
                   jax.ShapeDtypeStruct((t, LANES), i32),
                   jax.ShapeDtypeStruct((t, LANES), f32),
                   jax.ShapeDtypeStruct((8, LANES), f32)],
        scratch_shapes=[pltpu.VMEM((1, LANES), f32), pltpu.VMEM((MIX_TM, D_MODEL), f32)],
        compiler_params=_params(("arbitrary",)),
        name="mix",
    )(*os_, *ls_, perms_t[1], perms_t[2], expand, ya, gates, x2, g1, sc2, sh2, wa, wb, wo,
      ln1g, ln1b, wr_parts, br, tri)


DISP_TM = 512
MOE_TM = 512


def _dispatch_kernel(pends_ref, pcnt_ref, nused_ref, dest_ref, h2p_ref, xs_hbm, zbuf, sem, zsem):
    i = pl.program_id(0)
    ntile = xs_hbm.shape[0] // (MOE_TM * ROW_SUB)

    def zero_tile(first_row):
        return pltpu.make_async_copy(
            zbuf, xs_hbm.at[pl.ds(pl.multiple_of(first_row * ROW_SUB, MOE_TM * ROW_SUB), MOE_TM * ROW_SUB)], zsem)

    def for_each_zero_tile(fn):
        for e in range(N_EXPERTS):
            pl.when(pcnt_ref[e] > 0)(functools.partial(fn, lambda e=e: zero_tile(pends_ref[e] - MOE_TM)))
        for k in range(N_EXPERTS):
            tile = nused_ref[0] + k
            pl.when(tile < ntile)(functools.partial(fn, lambda tile=tile: zero_tile(tile * MOE_TM)))

    @pl.when(i == 0)
    def _():
        zbuf[...] = jnp.zeros_like(zbuf)
        for_each_zero_tile(lambda mk: mk().start())
        for_each_zero_tile(lambda mk: mk().wait())

    def row_copy(k, r):
        d = dest_ref[0, k, r]
        return pltpu.make_async_copy(h2p_ref.at[pl.ds(r * ROW_SUB, ROW_SUB)],
                                     xs_hbm.at[pl.ds(pl.multiple_of(d * ROW_SUB, ROW_SUB), ROW_SUB)], sem)

    for r in range(DISP_TM):
        for k in range(TOP_K):
            row_copy(k, r).start(priority=k % 2)
    for k in range(TOP_K):
        pltpu.make_async_copy(h2p_ref, xs_hbm.at[pl.ds(0, DISP_TM * ROW_SUB)], sem).wait()


def _dispatch(pends, pcounts, n_used, dest3, h2p, ntile):
    t = h2p.shape[0] // ROW_SUB
    grid_spec = pltpu.PrefetchScalarGridSpec(
        num_scalar_prefetch=3,
        grid=(t // DISP_TM,),
        in_specs=[pl.BlockSpec((1, TOP_K, DISP_TM), lambda i, *_: (i, 0, 0), memory_space=pltpu.SMEM),
                  pl.BlockSpec((DISP_TM * ROW_SUB, LANES), lambda i, *_: (i, 0))],
        out_specs=pl.BlockSpec(memory_space=pl.ANY),
        scratch_shapes=[pltpu.VMEM((MOE_TM * ROW_SUB, LANES), i32),
                        pltpu.SemaphoreType.DMA(()),
                        pltpu.SemaphoreType.DMA(())],
    )
    return pl.pallas_call(
        _dispatch_kernel,
        grid_spec=grid_spec,
        out_shape=jax.ShapeDtypeStruct((ntile * MOE_TM * ROW_SUB, LANES), i32),
        compiler_params=_params(("arbitrary",)),
        name="dispatch",
    )(pends, pcounts, n_used, dest3, h2p)


def _moe_kernel(te_ref, first_ref, nexte_ref, wslot_ref, nused_ref,
                xs_ref, wg_hbm, wu_hbm, wd_hbm, bg_ref, bu_ref, bd_ref,
                out_ref, wbuf, wgb, wub, wdb, sem_w):
    j = pl.program_id(0)

    def weight_copies(e, ws):
        return [pltpu.make_async_copy(w.at[e], wbuf.at[ws, k], sem_w.at[ws])
                for k, w in enumerate((wg_hbm, wu_hbm, wd_hbm))]

    @pl.when(j == 0)
    def _():
        for cp in weight_copies(te_ref[0], wslot_ref[0]):
            cp.start()

    @pl.when(first_ref[j] == 1)
    def _():
        ws = wslot_ref[j]
        for cp in weight_copies(te_ref[j], ws):
            cp.wait()
        wgb[...] = wbuf[ws, 0].astype(bf16)
        wub[...] = wbuf[ws, 1].astype(bf16)
        wdb[...] = wbuf[ws, 2].astype(bf16)
        ne = nexte_ref[j]

        @pl.when(ne >= 0)
        def _():
            for cp in weight_copies(ne, 1 - ws):
                cp.start()

    used = j < nused_ref[0]

    @pl.when(used)
    def _():
        xb = _unpack_rows(_load_packed(xs_ref, 0, MOE_TM)).astype(bf16)
        g = jnp.dot(xb, wgb[...], preferred_element_type=f32) + bg_ref[0]
        u = jnp.dot(xb, wub[...], preferred_element_type=f32) + bu_ref[0]
        g = jnp.minimum(g, SWIGLU_LIMIT)
        u = jnp.clip(u, -SWIGLU_LIMIT, SWIGLU_LIMIT)
        act = (u + 1.0) * (g * jax.nn.sigmoid(SWIGLU_ALPHA * g))
        y = jnp.dot(act.astype(bf16), wdb[...], preferred_element_type=f32) + bd_ref[0]
        _store_packed(out_ref, _pack_rows(y), MOE_TM)

    @pl.when(jnp.logical_not(used))
    def _():
        out_ref[...] = jnp.zeros_like(out_ref)


def _moe(tile_e, tile_first, next_e, wslot, n_used, xs, w_gate, b_gate, w_up, b_up, w_down, b_down):
    ntile = tile_e.shape[0]
    bspec = pl.BlockSpec((1, 1, D_MODEL), lambda j, te, *_: (te[j], 0, 0))
    hbm = pl.BlockSpec(memory_space=pl.ANY)
    grid_spec = pltpu.PrefetchScalarGridSpec(
        num_scalar_prefetch=5,
        grid=(ntile,),
        in_specs=[pl.BlockSpec((MOE_TM * ROW_SUB, LANES),
                               lambda j, te, fi, ne, ws, nu: (jnp.minimum(j, nu[0] - 1), 0)),
                  hbm, hbm, hbm, bspec, bspec, bspec],
        out_specs=pl.BlockSpec((MOE_TM * ROW_SUB, LANES), lambda j, *_: (j, 0)),
        scratch_shapes=[pltpu.VMEM((2, 3, D_MODEL, D_MODEL), f32),
                        pltpu.VMEM((D_MODEL, D_MODEL), bf16),
                        pltpu.VMEM((D_MODEL, D_MODEL), bf16),
                        pltpu.VMEM((D_MODEL, D_MODEL), bf16),
                        pltpu.SemaphoreType.DMA((2,))],
    )
    return pl.pallas_call(
        _moe_kernel,
        grid_spec=grid_spec,
        out_shape=jax.ShapeDtypeStruct((ntile * MOE_TM * ROW_SUB, LANES), i32),
        compiler_params=_params(("arbitrary",)),
        name="moe",
    )(tile_e, tile_first, next_e, wslot, n_used, xs, w_gate, w_up, w_down, b_gate, b_up, b_down)


CB_TM = DISP_TM


def _combine_kernel(dcur_ref, dnxt_ref, yb_hbm, rw_ref, x1_ref, g2_ref, lng_ref, lnb_ref, out_ref,
                    ybuf0, ybuf1, sem):
    i = pl.program_id(0)
    last = pl.num_programs(0) - 1
    ybufs = (ybuf0, ybuf1)

    def row_copy(d, k, r, s):
        return pltpu.make_async_copy(
            yb_hbm.at[pl.ds(pl.multiple_of(d * ROW_SUB, ROW_SUB), ROW_SUB)],
            ybufs[s].at[pl.ds(pl.multiple_of((k * CB_TM + r) * ROW_SUB, ROW_SUB), ROW_SUB)],
            sem.at[s])

    @pl.when(i == 0)
    def _():
        for k in range(TOP_K):
            def body(r, c, k=k):
                row_copy(dcur_ref[0, k, r], k, r, 0).start()
                return c
            lax.fori_loop(0, CB_TM, body, 0, unroll=8)

    for s in range(2):
        @pl.when(i % 2 == s)
        def _(s=s):
            pltpu.make_async_copy(yb_hbm.at[pl.ds(0, TOP_K * CB_TM * ROW_SUB)], ybufs[s], sem.at[s]).wait()

            @pl.when(i < last)
            def _():
                for k in range(TOP_K):
                    for r in range(CB_TM):
                        row_copy(dnxt_ref[0, k, r], k, r, 1 - s).start(priority=r % 2)

            parts = [_unpack_rows(_load_packed(ybufs[s], k * CB_TM, CB_TM)) * rw_ref[:, k:k + 1]
                     for k in range(TOP_K)]
            y = (parts[0] + parts[1]) + (parts[2] + parts[3])
            out_ref[...] = _ln(DN_ALPHA * x1_ref[...] + g2_ref[0] * y) * lng_ref[...] + lnb_ref[...]


def _combine(dest3, yb, rw, x1, g2, ln2g, ln2b, seq):
    t = x1.shape[0]
    nb = t // CB_TM
    per_b = seq // CB_TM
    return pl.pallas_call(
        _combine_kernel,
        grid=(nb,),
        in_specs=[pl.BlockSpec((1, TOP_K, CB_TM), lambda i: (i, 0, 0), memory_space=pltpu.SMEM),
                  pl.BlockSpec((1, TOP_K, CB_TM), lambda i: (jnp.minimum(i + 1, nb - 1), 0, 0),
                               memory_space=pltpu.SMEM),
                  pl.BlockSpec(memory_space=pl.ANY),
                  pl.BlockSpec((CB_TM, LANES), lambda i: (i, 0)),
                  pl.BlockSpec((CB_TM, D_MODEL), lambda i: (i, 0)),
                  pl.BlockSpec((1, 1, D_MODEL), lambda i: (i // per_b, 0, 0)),
                  pl.BlockSpec((1, D_MODEL), lambda i: (0, 0)),
                  pl.BlockSpec((1, D_MODEL), lambda i: (0, 0))],
        out_specs=pl.BlockSpec((CB_TM, D_MODEL), lambda i: (i, 0)),
        out_shape=jax.ShapeDtypeStruct((t, D_MODEL), f32),
        scratch_shapes=[pltpu.VMEM((TOP_K * CB_TM * ROW_SUB, LANES), i32),
                        pltpu.VMEM((TOP_K * CB_TM * ROW_SUB, LANES), i32),
                        pltpu.SemaphoreType.DMA((2,))],
        compiler_params=_params(("arbitrary",)),
        name="combine",
    )(dest3, dest3, yb, rw, x1, g2, ln2g, ln2b)


def _t5_bucket(dist):
    d = dist.astype(f32)
    large = REL_MAX_EXACT + jnp.log(jnp.maximum(d, float(REL_MAX_EXACT)) / REL_MAX_EXACT) / math.log(
        REL_MAX_DIST / REL_MAX_EXACT) * (REL_BUCKETS - REL_MAX_EXACT)
    large = jnp.minimum(large.astype(i32), REL_BUCKETS - 1)
    return jnp.where(dist < REL_MAX_EXACT, dist, large)


def _bias_indices():
    qi = jnp.arange(ATT_BLOCK)[:, None]
    ki = jnp.arange(2 * ATT_BLOCK)[None, :]
    didx = qi + ATT_BLOCK - ki
    buckets, bands = [], []
    for win, dil in DIL_PAIRS:
        buckets.append(_t5_bucket(jnp.clip(didx, 0, None) * dil))
        bands.append(((didx >= 0) & (didx <= win // dil)).astype(i32))
    return jnp.stack(buckets).astype(i32), jnp.stack(bands)


def _residue_perm(tm, dil):
    n = tm // dil
    dst = np.arange(tm)
    src = (dst % n) * dil + dst // n
    return src[:, None] == np.arange(tm)[None, :]


def kernel(x, c, w_ada, b_ada, w_in, gm_ln_g, gm_ln_b, gm_w_s, gm_b_s, w_branch_a, w_branch_b, w_out,
           rel_bias, ln1_g, ln1_b, w_router, b_router, w_gate, b_gate, w_up, b_up, w_down, b_down,
           ln2_g, ln2_b):
    batch, seq, _ = x.shape
    t = batch * seq
    l = 0
    x2 = x.reshape(t, D_MODEL)

    c8 = jnp.pad(c, ((0, 8 - batch), (0, 0)))
    mod = _adaln(c8, w_ada[l], b_ada[l][None, :])[:batch]
    sh1, sc1, g1, sh2, sc2, g2 = [m[:, None, :] for m in jnp.split(mod, 6, axis=-1)]

    perms = [jnp.asarray(_residue_perm(IN_TM, dil), bf16) for _win, dil in DIL_PAIRS]
    bs_full = jnp.repeat(gm_b_s[l].T, GM_WIDTH // GM_GROUPS, axis=1)
    ya, gates, *qkvs = _inproj(x2, sc1, sh1, w_in[l].astype(bf16), perms,
                               gm_ln_g[l][None, :], gm_ln_b[l][None, :], gm_w_s[l], bs_full, batch, seq)

    bucket, band = _bias_indices()
    bias = _relbias(rel_bias, bucket, band)
    os_, ls_ = [], []
    for g, (_win, dil) in enumerate(DIL_PAIRS):
        o, lse = _attn_group(qkvs[g], bias, g, dil, batch, seq)
        os_.append(o)
        ls_.append(lse)

    wr = jnp.pad(w_router[l], ((0, 0), (0, LANES - N_EXPERTS)))
    wr_hi = wr.astype(bf16)
    wr_parts = jnp.concatenate([wr_hi, (wr - wr_hi.astype(f32)).astype(bf16)], axis=1)
    br = jnp.pad(b_router[l], (0, LANES - N_EXPERTS))[None, :]
    tri = jnp.asarray(np.arange(MIX_TM)[None, :] < np.arange(MIX_TM)[:, None], bf16)
    perms_t = [jnp.asarray(_residue_perm(MIX_TM, dil).T, bf16) for _win, dil in DIL_PAIRS]
    expand = np.arange(LANES)[:, None] == np.arange(ATT_WIDTH)[None, :] // HEAD_DIM
    expand = jnp.asarray(np.concatenate([expand, expand], axis=0), bf16)
    x1, h2, route, rw, cnt = _mix(
        os_, ls_, perms_t, expand, ya, gates, x2, g1, sc2, sh2,
        w_branch_a[l].astype(bf16), w_branch_b[l].astype(bf16), w_out[l].astype(bf16),
        ln1_g[l][None, :], ln1_b[l][None, :], wr_parts, br, tri, seq)

    top_e = route[:, :TOP_K]
    rank = route[:, TOP_K:2 * TOP_K]
    counts = cnt[0, :N_EXPERTS].astype(i32)
    pcounts = (counts + MOE_TM - 1) // MOE_TM * MOE_TM
    experts = jnp.arange(N_EXPERTS, dtype=i32)
    upto = experts[None, :] <= experts[:, None]
    pends = jnp.sum(jnp.where(upto, pcounts[None, :], 0), axis=1)
    pstarts = pends - pcounts
    dest = jnp.sum(jnp.where(top_e[:, :, None] == experts, pstarts, 0), axis=-1) + rank
    ntile = t * TOP_K // MOE_TM + N_EXPERTS
    n_used = (pends[-1] // MOE_TM).reshape(1)
    tile_idx = jnp.minimum(jnp.arange(ntile, dtype=i32), n_used - 1)
    tile_e = jnp.sum((pends[None, :] <= (tile_idx * MOE_TM)[:, None]).astype(i32), axis=1)
    tile_first = jnp.concatenate([jnp.ones((1,), i32), (tile_e[1:] != tile_e[:-1]).astype(i32)])
    nonempty = counts > 0
    later = jnp.logical_and(experts[None, :] > experts[:, None], nonempty[None, :])
    next_nonempty = jnp.min(jnp.where(later, experts[None, :], N_EXPERTS), axis=1)
    next_nonempty = jnp.where(next_nonempty >= N_EXPERTS, -1, next_nonempty)
    expert_slot = (jnp.sum(jnp.logical_and(upto, nonempty[None, :]).astype(i32), axis=1) - 1) % 2
    of_tile = tile_e[:, None] == experts[None, :]
    tile_next = jnp.sum(jnp.where(of_tile, next_nonempty[None, :], 0), axis=1)
    tile_slot = jnp.sum(jnp.where(of_tile, expert_slot[None, :], 0), axis=1)

    dest3 = dest.reshape(t // DISP_TM, DISP_TM, TOP_K).transpose(0, 2, 1)
    xs = _dispatch(pends, pcounts, n_used, dest3, h2, ntile)
    yb = _moe(tile_e, tile_first, tile_next, tile_slot, n_used, xs,
              w_gate[l], b_gate[l][:, None, :], w_up[l], b_up[l][:, None, :],
              w_down[l], b_down[l][:, None, :])
    out = _combine(dest3, yb, rw, x1, g2, ln2_g[l][None, :], ln2_b[l][None, :], seq)
    return out.reshape(batch, seq, D_MODEL)
```

```python
import functools
import math

import numpy as np
import jax
import jax.numpy as jnp
from jax import lax
from jax.experimental import pallas as pl
from jax.experimental.pallas import tpu as pltpu

f32 = jnp.float32
bf16 = jnp.bfloat16
i32 = jnp.int32

D_MODEL = 1024
GM_WIDTH = 512
GM_GROUPS = 8
GM_CHUNK = 128
DIL_PAIRS = ((128, 1), (512, 4), (2048, 16))
N_DIL = 3
HEADS_PER_GROUP = 8
HEAD_DIM = 64
ATT_WIDTH = 512
ATT_BLOCK = 128
NEG_INF = -1e30
REL_BUCKETS = 32
REL_MAX_EXACT = 16
REL_MAX_DIST = 2048
N_EXPERTS = 32
TOP_K = 4
SWIGLU_LIMIT = 7.0
SWIGLU_ALPHA = 1.702
DEPTH = 1
DN_ALPHA = (2 * DEPTH) ** 0.25
LN_EPS = 1e-5
UV_COLS = 2 * GM_WIDTH
QKV_COLS = N_DIL * 3 * ATT_WIDTH
GATE_COLS = 2 * D_MODEL
IN_COLS = UV_COLS + QKV_COLS + GATE_COLS

LANES = 128
VMEM_LIMIT = 56 * 1024 * 1024


def _ln(x):
    mu = jnp.mean(x, axis=-1, keepdims=True)
    xc = x - mu
    var = jnp.mean(xc * xc, axis=-1, keepdims=True)
    return xc * lax.rsqrt(var + LN_EPS)


def _params(sem, vmem=VMEM_LIMIT):
    return pltpu.CompilerParams(dimension_semantics=sem, vmem_limit_bytes=vmem)


def _adaln_kernel(c_ref, w_ref, b_ref, o_ref):
    c = c_ref[...]
    s = c * jax.nn.sigmoid(c)
    o_ref[...] = jnp.dot(s, w_ref[...], preferred_element_type=f32,
                         precision=lax.Precision.HIGHEST) + b_ref[...]


def _adaln(c8, w_ada, b_ada):
    n = w_ada.shape[1] // D_MODEL
    return pl.pallas_call(
        _adaln_kernel,
        grid=(n,),
        in_specs=[pl.BlockSpec((8, D_MODEL), lambda j: (0, 0)),
                  pl.BlockSpec((D_MODEL, D_MODEL), lambda j: (0, j)),
                  pl.BlockSpec((1, D_MODEL), lambda j: (0, j))],
        out_specs=pl.BlockSpec((8, D_MODEL), lambda j: (0, j)),
        out_shape=jax.ShapeDtypeStruct((8, w_ada.shape[1]), f32),
        compiler_params=_params(("arbitrary",)),
        name="adaln",
    )(c8, w_ada, b_ada)


IN_TM = 256
IN_CW = 512
GRP_COLS = 3 * ATT_WIDTH


def _inproj_kernel(x_ref, sc_ref, sh_ref, w_ref, p1_ref, p2_ref, gm_g_ref, gm_b_ref, ws_ref, bs_ref,
                   ya_ref, gt_ref, qkv0_ref, qkv1_ref, qkv2_ref):
    xn = _ln(x_ref[...])
    h = (xn * (1.0 + sc_ref[0]) + sh_ref[0]).astype(bf16)
    hp = [h,
          jnp.dot(p1_ref[...], h, preferred_element_type=f32).astype(bf16),
          jnp.dot(p2_ref[...], h, preferred_element_type=f32).astype(bf16)]
    u_act, v_act = [jax.nn.gelu(jnp.dot(h, w_ref[:, c0:c0 + GM_WIDTH], preferred_element_type=f32))
                    for c0 in range(0, UV_COLS, GM_WIDTH)]
    row = lax.broadcasted_iota(i32, (GM_CHUNK, GM_CHUNK), 0)
    col = lax.broadcasted_iota(i32, (GM_CHUNK, GM_CHUNK), 1)
    first_half = lax.broadcasted_iota(i32, (GM_CHUNK, LANES), 1) < (GM_WIDTH // GM_GROUPS)
    ws = [jnp.where(col <= row, ws_ref[g], 0.0).astype(bf16) for g in range(GM_GROUPS)]

    def gate_chunk(r0):
        vn = (_ln(v_act[r0:r0 + GM_CHUNK, :]) * gm_g_ref[...] + gm_b_ref[...]).astype(bf16)
        for j in range(GM_WIDTH // LANES):
            sl = slice(j * LANES, (j + 1) * LANES)
            s_lo = jnp.dot(ws[2 * j], vn[:, sl], preferred_element_type=f32)
            s_hi = jnp.dot(ws[2 * j + 1], vn[:, sl], preferred_element_type=f32)
            s = jnp.where(first_half, s_lo, s_hi) + bs_ref[:, sl]
            ya_ref[r0:r0 + GM_CHUNK, sl] = (u_act[r0:r0 + GM_CHUNK, sl] * s).astype(bf16)

    pending = list(range(0, IN_TM, GM_CHUNK))
    for g, (qref, (_win, dil)) in enumerate(zip((qkv0_ref, qkv1_ref, qkv2_ref), DIL_PAIRS)):
        n = IN_TM // dil
        for q0 in range(0, GRP_COLS, IN_CW):
            c0 = UV_COLS + g * GRP_COLS + q0
            acc = jnp.dot(hp[g], w_ref[:, c0:c0 + IN_CW], preferred_element_type=f32).astype(bf16)
            for rho in range(dil):
                qref[0, rho, :, q0:q0 + IN_CW] = acc[rho * n:(rho + 1) * n, :]
        if pending:
            gate_chunk(pending.pop(0))
    while pending:
        gate_chunk(pending.pop(0))
    for g0 in range(0, GATE_COLS, IN_CW):
        c0 = UV_COLS + QKV_COLS + g0
        acc = jnp.dot(h, w_ref[:, c0:c0 + IN_CW], preferred_element_type=f32)
        gt_ref[:, g0:g0 + IN_CW] = jax.nn.sigmoid(acc).astype(bf16)


def _inproj(x2, sc1, sh1, w_in_bf, perms, gm_g, gm_b, w_s, bs_full, batch, seq):
    t = x2.shape[0]
    per_b = seq // IN_TM
    qkv_specs, qkv_shapes = [], []
    for _win, dil in DIL_PAIRS:
        n = IN_TM // dil
        qkv_specs.append(pl.BlockSpec((1, dil, n, GRP_COLS), lambda i: (i // per_b, 0, i % per_b, 0)))
        qkv_shapes.append(jax.ShapeDtypeStruct((batch, dil, seq // dil, GRP_COLS), bf16))
    return pl.pallas_call(
        _inproj_kernel,
        grid=(t // IN_TM,),
        in_specs=[pl.BlockSpec((IN_TM, D_MODEL), lambda i: (i, 0)),
                  pl.BlockSpec((1, 1, D_MODEL), lambda i: (i // per_b, 0, 0)),
                  pl.BlockSpec((1, 1, D_MODEL), lambda i: (i // per_b, 0, 0)),
                  pl.BlockSpec((D_MODEL, IN_COLS), lambda i: (0, 0)),
                  pl.BlockSpec((IN_TM, IN_TM), lambda i: (0, 0)),
                  pl.BlockSpec((IN_TM, IN_TM), lambda i: (0, 0)),
                  pl.BlockSpec((1, GM_WIDTH), lambda i: (0, 0)),
                  pl.BlockSpec((1, GM_WIDTH), lambda i: (0, 0)),
                  pl.BlockSpec((GM_GROUPS, GM_CHUNK, GM_CHUNK), lambda i: (0, 0, 0)),
                  pl.BlockSpec((GM_CHUNK, GM_WIDTH), lambda i: (0, 0))],
        out_specs=[pl.BlockSpec((IN_TM, GM_WIDTH), lambda i: (i, 0)),
                   pl.BlockSpec((IN_TM, GATE_COLS), lambda i: (i, 0))] + qkv_specs,
        out_shape=[jax.ShapeDtypeStruct((t, GM_WIDTH), bf16),
                   jax.ShapeDtypeStruct((t, GATE_COLS), bf16)] + qkv_shapes,
        compiler_params=_params(("arbitrary",)),
        name="inproj",
    )(x2, sc1, sh1, w_in_bf, perms[1], perms[2], gm_g, gm_b, w_s, bs_full)


def _relbias_kernel(tab_ref, bucket_ref, band_ref, out_ref):
    g = pl.program_id(0)
    bk = bucket_ref[0]
    band = band_ref[0] > 0
    for h in range(HEADS_PER_GROUP):
        acc = jnp.zeros((ATT_BLOCK, 2 * ATT_BLOCK), f32)
        for b in range(REL_BUCKETS):
            acc = jnp.where(bk == b, tab_ref[b, g * HEADS_PER_GROUP + h], acc)
        out_ref[0, h] = jnp.where(band, acc, NEG_INF)


def _relbias(rel_bias, bucket, band):
    return pl.pallas_call(
        _relbias_kernel,
        grid=(N_DIL,),
        in_specs=[pl.BlockSpec(memory_space=pltpu.SMEM),
                  pl.BlockSpec((1, ATT_BLOCK, 2 * ATT_BLOCK), lambda g: (g, 0, 0)),
                  pl.BlockSpec((1, ATT_BLOCK, 2 * ATT_BLOCK), lambda g: (g, 0, 0))],
        out_specs=pl.BlockSpec((1, HEADS_PER_GROUP, ATT_BLOCK, 2 * ATT_BLOCK),
                               lambda g: (g, 0, 0, 0)),
        out_shape=jax.ShapeDtypeStruct((N_DIL, HEADS_PER_GROUP, ATT_BLOCK, 2 * ATT_BLOCK), f32),
        compiler_params=_params(("arbitrary",)),
        name="relbias",
    )(rel_bias, bucket, band)


ATT_MAX_STEP_BLOCKS = 8


def _attn_kernel(nres, nblk, q_ref, kp_ref, kc_ref, vp_ref, vc_ref, bias_ref, o_ref, lse_ref):
    first = pl.program_id(2) == 0
    lane = lax.broadcasted_iota(i32, (ATT_BLOCK, LANES), 1)
    lo_half = lane < HEAD_DIM
    prev_cols = lax.broadcasted_iota(i32, (ATT_BLOCK, 2 * ATT_BLOCK), 1) < ATT_BLOCK
    no_prev = jnp.logical_and(first, prev_cols)
    nt = (((1,), (1,)), ((), ()))
    ones = jnp.ones((2 * ATT_BLOCK, LANES), bf16)
    n_slab = ATT_WIDTH // LANES
    blocks = [(res, i) for res in range(nres) for i in range(nblk)]
    logits, v_ext = [], []
    for res, i in blocks:
        cur = slice(i * ATT_BLOCK, (i + 1) * ATT_BLOCK)
        prv = slice((i - 1) * ATT_BLOCK, i * ATT_BLOCK)
        for j in range(n_slab):
            sl = slice(j * LANES, (j + 1) * LANES)
            q = q_ref[0, res, cur, sl] * (HEAD_DIM ** -0.5)
            k_prev = kp_ref[0, res, :, sl] if i == 0 else kc_ref[0, res, prv, sl]
            v_prev = vp_ref[0, res, :, sl] if i == 0 else vc_ref[0, res, prv, sl]
            k_cat = jnp.concatenate([k_prev, kc_ref[0, res, cur, sl]], axis=0)
            v_cat = jnp.concatenate([v_prev, vc_ref[0, res, cur, sl]], axis=0)
            v_ext.append(jnp.concatenate([v_cat, ones], axis=1))
            for hh in range(2):
                qm = jnp.where(lo_half if hh == 0 else jnp.logical_not(lo_half), q, 0.0).astype(bf16)
                lg_h = lax.dot_general(qm, k_cat, nt, preferred_element_type=f32) + bias_ref[0, 2 * j + hh]
                logits.append(jnp.where(no_prev, NEG_INF, lg_h) if i == 0 else lg_h)
    rows_per_block = HEADS_PER_GROUP * ATT_BLOCK
    lg = jnp.concatenate(logits, axis=0)
    m = jnp.max(lg, axis=-1, keepdims=True)
    p = jnp.exp(lg - m).astype(bf16)
    for b, (res, i) in enumerate(blocks):
        cur = slice(i * ATT_BLOCK, (i + 1) * ATT_BLOCK)
        lse_tile = jnp.zeros((ATT_BLOCK, LANES), f32)
        for j in range(n_slab):
            outs = []
            for hh in range(2):
                h = 2 * j + hh
                r0 = b * rows_per_block + h * ATT_BLOCK
                r = jnp.dot(p[r0:r0 + ATT_BLOCK], v_ext[b * n_slab + j], preferred_element_type=f32)
                den = r[:, LANES:]
                outs.append(r[:, :LANES] * (1.0 / den))
                lse_h = m[r0:r0 + ATT_BLOCK] + jnp.log(den)
                lse_tile = jnp.where(lane == h, lse_h, lse_tile)
            o_ref[0, res, cur, j * LANES:(j + 1) * LANES] = jnp.where(lo_half, outs[0], outs[1]).astype(bf16)
        lse_ref[0, res, cur, :] = lse_tile


def _attn_group(qkv_g, bias, g, dil, batch, seq):
    l = seq // dil
    nblk = min(ATT_MAX_STEP_BLOCKS, l // ATT_BLOCK)
    nres = min(dil, ATT_MAX_STEP_BLOCKS // nblk)
    tm = nblk * ATT_BLOCK
    nsteps = l // tm

    def cur(cb):
        return pl.BlockSpec((1, nres, tm, ATT_WIDTH), lambda b, r, n: (b, r, n, cb))

    def prev(cb):
        return pl.BlockSpec((1, nres, ATT_BLOCK, ATT_WIDTH),
                            lambda b, r, n: (b, r, jnp.maximum(n * nblk - 1, 0), cb))

    return pl.pallas_call(
        functools.partial(_attn_kernel, nres, nblk),
        grid=(batch, dil // nres, nsteps),
        in_specs=[cur(0), prev(1), cur(1), prev(2), cur(2),
                  pl.BlockSpec((1, HEADS_PER_GROUP, ATT_BLOCK, 2 * ATT_BLOCK),
                               lambda b, r, n: (g, 0, 0, 0))],
        out_specs=[pl.BlockSpec((1, nres, tm, ATT_WIDTH), lambda b, r, n: (b, r, n, 0)),
                   pl.BlockSpec((1, nres, tm, LANES), lambda b, r, n: (b, r, n, 0))],
        out_shape=[jax.ShapeDtypeStruct((batch, dil, l, ATT_WIDTH), bf16),
                   jax.ShapeDtypeStruct((batch, dil, l, LANES), f32)],
        compiler_params=_params(("arbitrary", "arbitrary", "arbitrary")),
        name=f"attn_g{g}",
    )(qkv_g, qkv_g, qkv_g, qkv_g, qkv_g, bias)


ROW_WORDS = D_MODEL // 2
ROW_SUB = ROW_WORDS // LANES
HI_MASK = -65536


def _pack_rows(x):
    bits = lax.bitcast_convert_type(x.astype(bf16).astype(f32), i32)
    return lax.shift_right_logical(bits[:, :ROW_WORDS], 16) | (bits[:, ROW_WORDS:] & HI_MASK)


def _unpack_rows(words):
    lo = lax.bitcast_convert_type(lax.shift_left(words, 16), f32)
    hi = lax.bitcast_convert_type(words & HI_MASK, f32)
    return jnp.concatenate([lo, hi], axis=1)


def _store_packed(ref, words, n, first_row=0):
    for r in range(ROW_SUB):
        ref[pl.ds(first_row * ROW_SUB + r, n, stride=ROW_SUB), :] = words[:, r * LANES:(r + 1) * LANES]


def _load_packed(ref, first_row, n):
    return jnp.concatenate([ref[pl.ds(first_row * ROW_SUB + r, n, stride=ROW_SUB), :] for r in range(ROW_SUB)],
                           axis=1)


MIX_TM = 512
MIX_SUB = 128


def _split_bf16(x, parts):
    out = []
    for _ in range(parts):
        hi = x.astype(bf16)
        out.append(hi)
        x = x - hi.astype(f32)
    return out


def _mix_kernel(o0_ref, o1_ref, o2_ref, l0_ref, l1_ref, l2_ref, pt1_ref, pt2_ref, ex_ref,
                ya_ref, gt_ref, x_ref,
                g1_ref, sc2_ref, sh2_ref, wa_ref, wb_ref, wo_ref, ln1g_ref, ln1b_ref,
                wrc_ref, br_ref, tri_ref,
                x1_ref, h2_ref, route_ref, rw_ref, cnt_ref, run_ref, xr_ref):
    step = pl.program_id(0)

    @pl.when(step == 0)
    def _():
        run_ref[...] = jnp.zeros_like(run_ref)
        xr_ref[...] = jnp.zeros_like(xr_ref)

    def back_rows(r0):
        rows = slice(r0, r0 + MIX_SUB)
        x1 = _ln(xr_ref[rows, :]) * ln1g_ref[...] + ln1b_ref[...]
        x1_ref[rows, :] = x1
        h2 = _ln(x1) * (1.0 + sc2_ref[0]) + sh2_ref[0]
        _store_packed(h2_ref, _pack_rows(h2), MIX_SUB, r0)
        h_hi, h_lo = _split_bf16(h2, 2)
        hi_both = jnp.dot(h_hi, wrc_ref[...], preferred_element_type=f32)
        return (hi_both[:, :LANES]
                + (hi_both[:, LANES:] + jnp.dot(h_lo, wrc_ref[:, :LANES], preferred_element_type=f32))
                ) + br_ref[...]

    def front_rows(r0):
        rows = slice(r0, r0 + MIX_SUB)
        os_, ls_ = [o0_ref[0, 0, rows, :].astype(f32)], [l0_ref[0, 0, rows, :]]
        for o_ref, l_ref, pt_ref in ((o1_ref, l1_ref, pt1_ref), (o2_ref, l2_ref, pt2_ref)):
            pt = pt_ref[rows, :]
            os_.append(jnp.dot(pt, o_ref[0].reshape(MIX_TM, ATT_WIDTH), preferred_element_type=f32))
            parts = [jnp.dot(pt, part, preferred_element_type=f32)
                     for part in _split_bf16(l_ref[0].reshape(MIX_TM, LANES), 3)]
            ls_.append((parts[0] + parts[1]) + parts[2])
        lm = jnp.maximum(jnp.maximum(ls_[0], ls_[1]), ls_[2])
        es = [jnp.exp(lse - lm) for lse in ls_]
        inv = 1.0 / (es[0] + es[1] + es[2])
        yb = jnp.zeros((MIX_SUB, ATT_WIDTH), f32)
        for e, o in zip(es, os_):
            w_parts = jnp.concatenate(_split_bf16(e * inv, 2), axis=1)
            yb = yb + jnp.dot(w_parts, ex_ref[...], preferred_element_type=f32) * o
        a = jnp.dot(ya_ref[rows, :], wa_ref[...], preferred_element_type=f32)
        b = jnp.dot(yb.astype(bf16), wb_ref[...], preferred_element_type=f32)
        merged = gt_ref[rows, :D_MODEL].astype(f32) * a + gt_ref[rows, D_MODEL:].astype(f32) * b
        mix = jnp.dot(merged.astype(bf16), wo_ref[...], preferred_element_type=f32)
        xr_ref[rows, :] = DN_ALPHA * x_ref[rows, :] + g1_ref[0] * mix

    logit_parts = []
    for r0 in range(0, MIX_TM, MIX_SUB):
        logit_parts.append(back_rows(r0))
        front_rows(r0)
    logits = jnp.concatenate(logit_parts, axis=0)
    lane = lax.broadcasted_iota(i32, (MIX_TM, LANES), 1)
    logits = jnp.where(lane < N_EXPERTS, logits, -jnp.inf)
    lane_f = lane.astype(f32)
    vals, idxs = [], []
    for _k in range(TOP_K):
        m = jnp.max(logits, axis=-1, keepdims=True)
        vals.append(m)
        idxs.append(jnp.min(jnp.where(logits == m, lane_f, float(LANES)), axis=-1, keepdims=True).astype(i32))
        logits = jnp.where(lane == idxs[-1], -jnp.inf, logits)
    exps = [jnp.exp(v - vals[0]) for v in vals]
    den = exps[0] + exps[1] + exps[2] + exps[3]
    wts = [e / den for e in exps]
    hits = [lane == idx for idx in idxs]
    counted = jnp.where(step > 0, 1.0, 0.0)
    onehot = jnp.zeros((MIX_TM, LANES), f32)
    for hit in hits:
        onehot = onehot + jnp.where(hit, counted, 0.0)
    prefix = jnp.dot(tri_ref[...], onehot.astype(bf16), preferred_element_type=f32) + run_ref[...]
    route = jnp.zeros((MIX_TM, LANES), i32)
    rw = jnp.zeros((MIX_TM, LANES), f32)
    for k in range(TOP_K):
        rank = jnp.sum(jnp.where(hits[k], prefix, 0.0), axis=-1, keepdims=True).astype(i32)
        route = jnp.where(lane == k, idxs[k], route)
        route = jnp.where(lane == TOP_K + k, rank, route)
        rw = jnp.where(lane == k, wts[k], rw)
    route_ref[...] = route
    rw_ref[...] = rw
    run = run_ref[...] + jnp.sum(onehot, axis=0, keepdims=True)
    run_ref[...] = run
    cnt_ref[...] = jnp.broadcast_to(run, cnt_ref.shape)


def _mix(os_, ls_, perms_t, expand, ya, gates, x2, g1, sc2, sh2, wa, wb, wo, ln1g, ln1b, wr_parts, br, tri, seq):
    t = x2.shape[0]
    nb = t // MIX_TM
    per_b = seq // MIX_TM
    cur = lambda i: jnp.minimum(i, nb - 1)
    prv = lambda i: jnp.maximum(i - 1, 0)
    row = lambda w: pl.BlockSpec((MIX_TM, w), lambda i: (cur(i), 0))
    out_row = lambda w: pl.BlockSpec((MIX_TM, w), lambda i: (prv(i), 0))
    const = lambda s: pl.BlockSpec(s, lambda i: tuple(0 for _ in s))
    mod_cur = pl.BlockSpec((1, 1, D_MODEL), lambda i: (cur(i) // per_b, 0, 0))
    mod_prv = pl.BlockSpec((1, 1, D_MODEL), lambda i: (prv(i) // per_b, 0, 0))
    grp = lambda w: [pl.BlockSpec((1, dil, MIX_TM // dil, w), lambda i: (cur(i) // per_b, 0, cur(i) % per_b, 0))
                     for _win, dil in DIL_PAIRS]
    return pl.pallas_call(
        _mix_kernel,
        grid=(nb + 1,),
        in_specs=grp(ATT_WIDTH) + grp(LANES) + [
                  const((MIX_TM, MIX_TM)), const((MIX_TM, MIX_TM)), const((2 * LANES, ATT_WIDTH)),
                  row(GM_WIDTH), row(GATE_COLS), row(D_MODEL),
                  mod_cur, mod_prv, mod_prv,
                  const((GM_WIDTH, D_MODEL)), const((ATT_WIDTH, D_MODEL)), const((D_MODEL, D_MODEL)),
                  const((1, D_MODEL)), const((1, D_MODEL)),
                  const((D_MODEL, 2 * LANES)), const((1, LANES)), const((MIX_TM, MIX_TM))],
        out_specs=[out_row(D_MODEL), pl.BlockSpec((MIX_TM * ROW_SUB, LANES), lambda i: (prv(i), 0)),
                   out_row(LANES), out_row(LANES), const((8, LANES))],
        out_shape=[jax.ShapeDtypeStruct((t, D_MODEL), f32),
                   jax.ShapeDtypeStruct((t * ROW_SUB, LANES), i32),
                   jax.ShapeDtypeStruct((t, LANES), i32),
                   jax.ShapeDtypeStruct((t, LANES), f32),
                   jax.ShapeDtypeStruct((8, LANES), f32)],
        scratch_shapes=[pltpu.VMEM((1, LANES), f32), pltpu.VMEM((MIX_TM, D_MODEL), f32)],
        compiler_params=_params(("arbitrary",)),
        name="mix",
    )(*os_, *ls_, perms_t[1], perms_t[2], expand, ya, gates, x2, g1, sc2, sh2, wa, wb, wo,
      ln1g, ln1b, wr_parts, br, tri)


DISP_TM = 512
MOE_TM = 512


def _dispatch_kernel(pends_ref, pcnt_ref, nused_ref, dest_ref, h2p_ref, xs_hbm, zbuf, sem, zsem):
    i = pl.program_id(0)
    ntile = xs_hbm.shape[0] // (MOE_TM * ROW_SUB)

    def zero_tile(first_row):
        return pltpu.make_async_copy(
            zbuf, xs_hbm.at[pl.ds(pl.multiple_of(first_row * ROW_SUB, MOE_TM * ROW_SUB), MOE_TM * ROW_SUB)], zsem)

    def for_each_zero_tile(fn):
        for e in range(N_EXPERTS):
            pl.when(pcnt_ref[e] > 0)(functools.partial(fn, lambda e=e: zero_tile(pends_ref[e] - MOE_TM)))
        for k in range(N_EXPERTS):
            tile = nused_ref[0] + k
            pl.when(tile < ntile)(functools.partial(fn, lambda tile=tile: zero_tile(tile * MOE_TM)))

    @pl.when(i == 0)
    def _():
        zbuf[...] = jnp.zeros_like(zbuf)
        for_each_zero_tile(lambda mk: mk().start())
        for_each_zero_tile(lambda mk: mk().wait())

    def row_copy(k, r):
        d = dest_ref[0, k, r]
        return pltpu.make_async_copy(h2p_ref.at[pl.ds(r * ROW_SUB, ROW_SUB)],
                                     xs_hbm.at[pl.ds(pl.multiple_of(d * ROW_SUB, ROW_SUB), ROW_SUB)], sem)

    for r in range(DISP_TM):
        for k in range(TOP_K):
            row_copy(k, r).start(priority=k % 2)
    for k in range(TOP_K):
        pltpu.make_async_copy(h2p_ref, xs_hbm.at[pl.ds(0, DISP_TM * ROW_SUB)], sem).wait()


def _dispatch(pends, pcounts, n_used, dest3, h2p, ntile):
    t = h2p.shape[0] // ROW_SUB
    grid_spec = pltpu.PrefetchScalarGridSpec(
        num_scalar_prefetch=3,
        grid=(t // DISP_TM,),
        in_specs=[pl.BlockSpec((1, TOP_K, DISP_TM), lambda i, *_: (i, 0, 0), memory_space=pltpu.SMEM),
                  pl.BlockSpec((DISP_TM * ROW_SUB, LANES), lambda i, *_: (i, 0))],
        out_specs=pl.BlockSpec(memory_space=pl.ANY),
        scratch_shapes=[pltpu.VMEM((MOE_TM * ROW_SUB, LANES), i32),
                        pltpu.SemaphoreType.DMA(()),
                        pltpu.SemaphoreType.DMA(())],
    )
    return pl.pallas_call(
        _dispatch_kernel,
        grid_spec=grid_spec,
        out_shape=jax.ShapeDtypeStruct((ntile * MOE_TM * ROW_SUB, LANES), i32),
        compiler_params=_params(("arbitrary",)),
        name="dispatch",
    )(pends, pcounts, n_used, dest3, h2p)


def _moe_kernel(te_ref, first_ref, nexte_ref, wslot_ref, nused_ref,
                xs_ref, wg_hbm, wu_hbm, wd_hbm, bg_ref, bu_ref, bd_ref,
                out_ref, wbuf, wgb, wub, wdb, sem_w):
    j = pl.program_id(0)

    def weight_copies(e, ws):
        return [pltpu.make_async_copy(w.at[e], wbuf.at[ws, k], sem_w.at[ws])
                for k, w in enumerate((wg_hbm, wu_hbm, wd_hbm))]

    @pl.when(j == 0)
    def _():
        for cp in weight_copies(te_ref[0], wslot_ref[0]):
            cp.start()

    @pl.when(first_ref[j] == 1)
    def _():
        ws = wslot_ref[j]
        for cp in weight_copies(te_ref[j], ws):
            cp.wait()
        wgb[...] = wbuf[ws, 0].astype(bf16)
        wub[...] = wbuf[ws, 1].astype(bf16)
        wdb[...] = wbuf[ws, 2].astype(bf16)
        ne = nexte_ref[j]

        @pl.when(ne >= 0)
        def _():
            for cp in weight_copies(ne, 1 - ws):
                cp.start()

    used = j < nused_ref[0]

    @pl.when(used)
    def _():
        xb = _unpack_rows(_load_packed(xs_ref, 0, MOE_TM)).astype(bf16)
        g = jnp.dot(xb, wgb[...], preferred_element_type=f32) + bg_ref[0]
        u = jnp.dot(xb, wub[...], preferred_element_type=f32) + bu_ref[0]
        g = jnp.minimum(g, SWIGLU_LIMIT)
        u = jnp.clip(u, -SWIGLU_LIMIT, SWIGLU_LIMIT)
        act = (u + 1.0) * (g * jax.nn.sigmoid(SWIGLU_ALPHA * g))
        y = jnp.dot(act.astype(bf16), wdb[...], preferred_element_type=f32) + bd_ref[0]
        _store_packed(out_ref, _pack_rows(y), MOE_TM)

    @pl.when(jnp.logical_not(used))
    def _():
        out_ref[...] = jnp.zeros_like(out_ref)


def _moe(tile_e, tile_first, next_e, wslot, n_used, xs, w_gate, b_gate, w_up, b_up, w_down, b_down):
    ntile = tile_e.shape[0]
    bspec = pl.BlockSpec((1, 1, D_MODEL), lambda j, te, *_: (te[j], 0, 0))
    hbm = pl.BlockSpec(memory_space=pl.ANY)
    grid_spec = pltpu.PrefetchScalarGridSpec(
        num_scalar_prefetch=5,
        grid=(ntile,),
        in_specs=[pl.BlockSpec((MOE_TM * ROW_SUB, LANES),
                               lambda j, te, fi, ne, ws, nu: (jnp.minimum(j, nu[0] - 1), 0)),
                  hbm, hbm, hbm, bspec, bspec, bspec],
        out_specs=pl.BlockSpec((MOE_TM * ROW_SUB, LANES), lambda j, *_: (j, 0)),
        scratch_shapes=[pltpu.VMEM((2, 3, D_MODEL, D_MODEL), f32),
                        pltpu.VMEM((D_MODEL, D_MODEL), bf16),
                        pltpu.VMEM((D_MODEL, D_MODEL), bf16),
                        pltpu.VMEM((D_MODEL, D_MODEL), bf16),
                        pltpu.SemaphoreType.DMA((2,))],
    )
    return pl.pallas_call(
        _moe_kernel,
        grid_spec=grid_spec,
        out_shape=jax.ShapeDtypeStruct((ntile * MOE_TM * ROW_SUB, LANES), i32),
        compiler_params=_params(("arbitrary",)),
        name="moe",
    )(tile_e, tile_first, next_e, wslot, n_used, xs, w_gate, w_up, w_down, b_gate, b_up, b_down)


CB_TM = DISP_TM


def _combine_kernel(dcur_ref, dnxt_ref, yb_hbm, rw_ref, x1_ref, g2_ref, lng_ref, lnb_ref, out_ref,
                    ybuf0, ybuf1, sem):
    i = pl.program_id(0)
    last = pl.num_programs(0) - 1
    ybufs = (ybuf0, ybuf1)

    def row_copy(d, k, r, s):
        return pltpu.make_async_copy(
            yb_hbm.at[pl.ds(pl.multiple_of(d * ROW_SUB, ROW_SUB), ROW_SUB)],
            ybufs[s].at[pl.ds(pl.multiple_of((k * CB_TM + r) * ROW_SUB, ROW_SUB), ROW_SUB)],
            sem.at[s])

    @pl.when(i == 0)
    def _():
        for k in range(TOP_K):
            def body(r, c, k=k):
                row_copy(dcur_ref[0, k, r], k, r, 0).start()
                return c
            lax.fori_loop(0, CB_TM, body, 0, unroll=8)

    for s in range(2):
        @pl.when(i % 2 == s)
        def _(s=s):
            pltpu.make_async_copy(yb_hbm.at[pl.ds(0, TOP_K * CB_TM * ROW_SUB)], ybufs[s], sem.at[s]).wait()

            @pl.when(i < last)
            def _():
                for k in range(TOP_K):
                    for r in range(CB_TM):
                        row_copy(dnxt_ref[0, k, r], k, r, 1 - s).start(priority=r % 2)

            parts = [_unpack_rows(_load_packed(ybufs[s], k * CB_TM, CB_TM)) * rw_ref[:, k:k + 1]
                     for k in range(TOP_K)]
            y = (parts[0] + parts[1]) + (parts[2] + parts[3])
            out_ref[...] = _ln(DN_ALPHA * x1_ref[...] + g2_ref[0] * y) * lng_ref[...] + lnb_ref[...]


def _combine(dest3, yb, rw, x1, g2, ln2g, ln2b, seq):
    t = x1.shape[0]
    nb = t // CB_TM
    per_b = seq // CB_TM
    return pl.pallas_call(
        _combine_kernel,
        grid=(nb,),
        in_specs=[pl.BlockSpec((1, TOP_K, CB_TM), lambda i: (i, 0, 0), memory_space=pltpu.SMEM),
                  pl.BlockSpec((1, TOP_K, CB_TM), lambda i: (jnp.minimum(i + 1, nb - 1), 0, 0),
                               memory_space=pltpu.SMEM),
                  pl.BlockSpec(memory_space=pl.ANY),
                  pl.BlockSpec((CB_TM, LANES), lambda i: (i, 0)),
                  pl.BlockSpec((CB_TM, D_MODEL), lambda i: (i, 0)),
                  pl.BlockSpec((1, 1, D_MODEL), lambda i: (i // per_b, 0, 0)),
                  pl.BlockSpec((1, D_MODEL), lambda i: (0, 0)),
                  pl.BlockSpec((1, D_MODEL), lambda i: (0, 0))],
        out_specs=pl.BlockSpec((CB_TM, D_MODEL), lambda i: (i, 0)),
        out_shape=jax.ShapeDtypeStruct((t, D_MODEL), f32),
        scratch_shapes=[pltpu.VMEM((TOP_K * CB_TM * ROW_SUB, LANES), i32),
                        pltpu.VMEM((TOP_K * CB_TM * ROW_SUB, LANES), i32),
                        pltpu.SemaphoreType.DMA((2,))],
        compiler_params=_params(("arbitrary",)),
        name="combine",
    )(dest3, dest3, yb, rw, x1, g2, ln2g, ln2b)


def _t5_bucket(dist):
    d = dist.astype(f32)
    large = REL_MAX_EXACT + jnp.log(jnp.maximum(d, float(REL_MAX_EXACT)) / REL_MAX_EXACT) / math.log(
        REL_MAX_DIST / REL_MAX_EXACT) * (REL_BUCKETS - REL_MAX_EXACT)
    large = jnp.minimum(large.astype(i32), REL_BUCKETS - 1)
    return jnp.where(dist < REL_MAX_EXACT, dist, large)


def _bias_indices():
    qi = jnp.arange(ATT_BLOCK)[:, None]
    ki = jnp.arange(2 * ATT_BLOCK)[None, :]
    didx = qi + ATT_BLOCK - ki
    buckets, bands = [], []
    for win, dil in DIL_PAIRS:
        buckets.append(_t5_bucket(jnp.clip(didx, 0, None) * dil))
        bands.append(((didx >= 0) & (didx <= win // dil)).astype(i32))
    return jnp.stack(buckets).astype(i32), jnp.stack(bands)


def _residue_perm(tm, dil):
    n = tm // dil
    dst = np.arange(tm)
    src = (dst % n) * dil + dst // n
    return src[:, None] == np.arange(tm)[None, :]


def kernel(x, c, w_ada, b_ada, w_in, gm_ln_g, gm_ln_b, gm_w_s, gm_b_s, w_branch_a, w_branch_b, w_out,
           rel_bias, ln1_g, ln1_b, w_router, b_router, w_gate, b_gate, w_up, b_up, w_down, b_down,
           ln2_g, ln2_b):
    batch, seq, _ = x.shape
    t = batch * seq
    l = 0
    x2 = x.reshape(t, D_MODEL)

    c8 = jnp.pad(c, ((0, 8 - batch), (0, 0)))
    mod = _adaln(c8, w_ada[l], b_ada[l][None, :])[:batch]
    sh1, sc1, g1, sh2, sc2, g2 = [m[:, None, :] for m in jnp.split(mod, 6, axis=-1)]

    perms = [jnp.asarray(_residue_perm(IN_TM, dil), bf16) for _win, dil in DIL_PAIRS]
    bs_full = jnp.repeat(gm_b_s[l].T, GM_WIDTH // GM_GROUPS, axis=1)
    ya, gates, *qkvs = _inproj(x2, sc1, sh1, w_in[l].astype(bf16), perms,
                               gm_ln_g[l][None, :], gm_ln_b[l][None, :], gm_w_s[l], bs_full, batch, seq)

    bucket, band = _bias_indices()
    bias = _relbias(rel_bias, bucket, band)
    os_, ls_ = [], []
    for g, (_win, dil) in enumerate(DIL_PAIRS):
        o, lse = _attn_group(qkvs[g], bias, g, dil, batch, seq)
        os_.append(o)
        ls_.append(lse)

    wr = jnp.pad(w_router[l], ((0, 0), (0, LANES - N_EXPERTS)))
    wr_hi = wr.astype(bf16)
    wr_parts = jnp.concatenate([wr_hi, (wr - wr_hi.astype(f32)).astype(bf16)], axis=1)
    br = jnp.pad(b_router[l], (0, LANES - N_EXPERTS))[None, :]
    tri = jnp.asarray(np.arange(MIX_TM)[None, :] < np.arange(MIX_TM)[:, None], bf16)
    perms_t = [jnp.asarray(_residue_perm(MIX_TM, dil).T, bf16) for _win, dil in DIL_PAIRS]
    expand = np.arange(LANES)[:, None] == np.arange(ATT_WIDTH)[None, :] // HEAD_DIM
    expand = jnp.asarray(np.concatenate([expand, expand], axis=0), bf16)
    x1, h2, route, rw, cnt = _mix(
        os_, ls_, perms_t, expand, ya, gates, x2, g1, sc2, sh2,
        w_branch_a[l].astype(bf16), w_branch_b[l].astype(bf16), w_out[l].astype(bf16),
        ln1_g[l][None, :], ln1_b[l][None, :], wr_parts, br, tri, seq)

    top_e = route[:, :TOP_K]
    rank = route[:, TOP_K:2 * TOP_K]
    counts = cnt[0, :N_EXPERTS].astype(i32)
    pcounts = (counts + MOE_TM - 1) // MOE_TM * MOE_TM
    experts = jnp.arange(N_EXPERTS, dtype=i32)
    upto = experts[None, :] <= experts[:, None]
    pends = jnp.sum(jnp.where(upto, pcounts[None, :], 0), axis=1)
    pstarts = pends - pcounts
    dest = jnp.sum(jnp.where(top_e[:, :, None] == experts, pstarts, 0), axis=-1) + rank
    ntile = t * TOP_K // MOE_TM + N_EXPERTS
    n_used = (pends[-1] // MOE_TM).reshape(1)
    tile_idx = jnp.minimum(jnp.arange(ntile, dtype=i32), n_used - 1)
    tile_e = jnp.sum((pends[None, :] <= (tile_idx * MOE_TM)[:, None]).astype(i32), axis=1)
    tile_first = jnp.concatenate([jnp.ones((1,), i32), (tile_e[1:] != tile_e[:-1]).astype(i32)])
    nonempty = counts > 0
    later = jnp.logical_and(experts[None, :] > experts[:, None], nonempty[None, :])
    next_nonempty = jnp.min(jnp.where(later, experts[None, :], N_EXPERTS), axis=1)
    next_nonempty = jnp.where(next_nonempty >= N_EXPERTS, -1, next_nonempty)
    expert_slot = (jnp.sum(jnp.logical_and(upto, nonempty[None, :]).astype(i32), axis=1) - 1) % 2
    of_tile = tile_e[:, None] == experts[None, :]
    tile_next = jnp.sum(jnp.where(of_tile, next_nonempty[None, :], 0), axis=1)
    tile_slot = jnp.sum(jnp.where(of_tile, expert_slot[None, :], 0), axis=1)

    dest3 = dest.reshape(t // DISP_TM, DISP_TM, TOP_K).transpose(0, 2, 1)
    xs = _dispatch(pends, pcounts, n_used, dest3, h2, ntile)
    yb = _moe(tile_e, tile_first, tile_next, tile_slot, n_used, xs,
              w_gate[l], b_gate[l][:, None, :], w_up[l], b_up[l][:, None, :],
              w_down[l], b_down[l][:, None, :])
    out = _combine(dest3, yb, rw, x1, g2, ln2_g[l][None, :], ln2_b[l][None, :], seq)
    return out.reshape(batch, seq, D_MODEL)
```

```python
import functools
import math

import numpy as np
import jax
import jax.numpy as jnp
from jax import lax
from jax.experimental import pallas as pl
from jax.experimental.pallas import tpu as pltpu

f32 = jnp.float32
bf16 = jnp.bfloat16
i32 = jnp.int32

D_MODEL = 1024
GM_WIDTH = 512
GM_GROUPS = 8
GM_CHUNK = 128
DIL_PAIRS = ((128, 1), (512, 4), (2048, 16))
N_DIL = 3
HEADS_PER_GROUP = 8
HEAD_DIM = 64
ATT_WIDTH = 512
ATT_BLOCK = 128
NEG_INF = -1e30
REL_BUCKETS = 32
REL_MAX_EXACT = 16
REL_MAX_DIST = 2048
N_EXPERTS = 32
TOP_K = 4
SWIGLU_LIMIT = 7.0
SWIGLU_ALPHA = 1.702
DEPTH = 1
DN_ALPHA = (2 * DEPTH) ** 0.25
LN_EPS = 1e-5
UV_COLS = 2 * GM_WIDTH
QKV_COLS = N_DIL * 3 * ATT_WIDTH
GATE_COLS = 2 * D_MODEL
IN_COLS = UV_COLS + QKV_COLS + GATE_COLS

LANES = 128
VMEM_LIMIT = 56 * 1024 * 1024


def _ln(x):
    mu = jnp.mean(x, axis=-1, keepdims=True)
    xc = x - mu
    var = jnp.mean(xc * xc, axis=-1, keepdims=True)
    return xc * lax.rsqrt(var + LN_EPS)


def _params(sem, vmem=VMEM_LIMIT):
    return pltpu.CompilerParams(dimension_semantics=sem, vmem_limit_bytes=vmem)


def _adaln_kernel(c_ref, w_ref, b_ref, o_ref):
    c = c_ref[...]
    s = c * jax.nn.sigmoid(c)
    o_ref[...] = jnp.dot(s, w_ref[...], preferred_element_type=f32,
                         precision=lax.Precision.HIGHEST) + b_ref[...]


def _adaln(c8, w_ada, b_ada):
    n = w_ada.shape[1] // D_MODEL
    return pl.pallas_call(
        _adaln_kernel,
        grid=(n,),
        in_specs=[pl.BlockSpec((8, D_MODEL), lambda j: (0, 0)),
                  pl.BlockSpec((D_MODEL, D_MODEL), lambda j: (0, j)),
                  pl.BlockSpec((1, D_MODEL), lambda j: (0, j))],
        out_specs=pl.BlockSpec((8, D_MODEL), lambda j: (0, j)),
        out_shape=jax.ShapeDtypeStruct((8, w_ada.shape[1]), f32),
        compiler_params=_params(("arbitrary",)),
        name="adaln",
    )(c8, w_ada, b_ada)


IN_TM = 256
IN_CW = 512
GRP_COLS = 3 * ATT_WIDTH


def _inproj_kernel(x_ref, sc_ref, sh_ref, w_ref, p1_ref, p2_ref, gm_g_ref, gm_b_ref, ws_ref, bs_ref,
                   ya_ref, gt_ref, qkv0_ref, qkv1_ref, qkv2_ref):
    xn = _ln(x_ref[...])
    h = (xn * (1.0 + sc_ref[0]) + sh_ref[0]).astype(bf16)
    hp = [h,
          jnp.dot(p1_ref[...], h, preferred_element_type=f32).astype(bf16),
          jnp.dot(p2_ref[...], h, preferred_element_type=f32).astype(bf16)]
    u_act, v_act = [jax.nn.gelu(jnp.dot(h, w_ref[:, c0:c0 + GM_WIDTH], preferred_element_type=f32))
                    for c0 in range(0, UV_COLS, GM_WIDTH)]
    row = lax.broadcasted_iota(i32, (GM_CHUNK, GM_CHUNK), 0)
    col = lax.broadcasted_iota(i32, (GM_CHUNK, GM_CHUNK), 1)
    first_half = lax.broadcasted_iota(i32, (GM_CHUNK, LANES), 1) < (GM_WIDTH // GM_GROUPS)
    ws = [jnp.where(col <= row, ws_ref[g], 0.0).astype(bf16) for g in range(GM_GROUPS)]

    def gate_chunk(r0):
        vn = (_ln(v_act[r0:r0 + GM_CHUNK, :]) * gm_g_ref[...] + gm_b_ref[...]).astype(bf16)
        for j in range(GM_WIDTH // LANES):
            sl = slice(j * LANES, (j + 1) * LANES)
            s_lo = jnp.dot(ws[2 * j], vn[:, sl], preferred_element_type=f32)
            s_hi = jnp.dot(ws[2 * j + 1], vn[:, sl], preferred_element_type=f32)
            s = jnp.where(first_half, s_lo, s_hi) + bs_ref[:, sl]
            ya_ref[r0:r0 + GM_CHUNK, sl] = (u_act[r0:r0 + GM_CHUNK, sl] * s).astype(bf16)

    pending = list(range(0, IN_TM, GM_CHUNK))
    for g, (qref, (_win, dil)) in enumerate(zip((qkv0_ref, qkv1_ref, qkv2_ref), DIL_PAIRS)):
        n = IN_TM // dil
        for q0 in range(0, GRP_COLS, IN_CW):
            c0 = UV_COLS + g * GRP_COLS + q0
            acc = jnp.dot(hp[g], w_ref[:, c0:c0 + IN_CW], preferred_element_type=f32).astype(bf16)
            for rho in range(dil):
                qref[0, rho, :, q0:q0 + IN_CW] = acc[rho * n:(rho + 1) * n, :]
        if pending:
            gate_chunk(pending.pop(0))
    while pending:
        gate_chunk(pending.pop(0))
    for g0 in range(0, GATE_COLS, IN_CW):
        c0 = UV_COLS + QKV_COLS + g0
        acc = jnp.dot(h, w_ref[:, c0:c0 + IN_CW], preferred_element_type=f32)
        gt_ref[:, g0:g0 + IN_CW] = jax.nn.sigmoid(acc).astype(bf16)


def _inproj(x2, sc1, sh1, w_in_bf, perms, gm_g, gm_b, w_s, bs_full, batch, seq):
    t = x2.shape[0]
    per_b = seq // IN_TM
    qkv_specs, qkv_shapes = [], []
    for _win, dil in DIL_PAIRS:
        n = IN_TM // dil
        qkv_specs.append(pl.BlockSpec((1, dil, n, GRP_COLS), lambda i: (i // per_b, 0, i % per_b, 0)))
        qkv_shapes.append(jax.ShapeDtypeStruct((batch, dil, seq // dil, GRP_COLS), bf16))
    return pl.pallas_call(
        _inproj_kernel,
        grid=(t // IN_TM,),
        in_specs=[pl.BlockSpec((IN_TM, D_MODEL), lambda i: (i, 0)),
                  pl.BlockSpec((1, 1, D_MODEL), lambda i: (i // per_b, 0, 0)),
                  pl.BlockSpec((1, 1, D_MODEL), lambda i: (i // per_b, 0, 0)),
                  pl.BlockSpec((D_MODEL, IN_COLS), lambda i: (0, 0)),
                  pl.BlockSpec((IN_TM, IN_TM), lambda i: (0, 0)),
                  pl.BlockSpec((IN_TM, IN_TM), lambda i: (0, 0)),
                  pl.BlockSpec((1, GM_WIDTH), lambda i: (0, 0)),
                  pl.BlockSpec((1, GM_WIDTH), lambda i: (0, 0)),
                  pl.BlockSpec((GM_GROUPS, GM_CHUNK, GM_CHUNK), lambda i: (0, 0, 0)),
                  pl.BlockSpec((GM_CHUNK, GM_WIDTH), lambda i: (0, 0))],
        out_specs=[pl.BlockSpec((IN_TM, GM_WIDTH), lambda i: (i, 0)),
                   pl.BlockSpec((IN_TM, GATE_COLS), lambda i: (i, 0))] + qkv_specs,
        out_shape=[jax.ShapeDtypeStruct((t, GM_WIDTH), bf16),
                   jax.ShapeDtypeStruct((t, GATE_COLS), bf16)] + qkv_shapes,
        compiler_params=_params(("arbitrary",)),
        name="inproj",
    )(x2, sc1, sh1, w_in_bf, perms[1], perms[2], gm_g, gm_b, w_s, bs_full)


def _relbias_kernel(tab_ref, bucket_ref, band_ref, out_ref):
    g = pl.program_id(0)
    bk = bucket_ref[0]
    band = band_ref[0] > 0
    for h in range(HEADS_PER_GROUP):
        acc = jnp.zeros((ATT_BLOCK, 2 * ATT_BLOCK), f32)
        for b in range(REL_BUCKETS):
            acc = jnp.where(bk == b, tab_ref[b, g * HEADS_PER_GROUP + h], acc)
        out_ref[0, h] = jnp.where(band, acc, NEG_INF)


def _relbias(rel_bias, bucket, band):
    return pl.pallas_call(
        _relbias_kernel,
        grid=(N_DIL,),
        in_specs=[pl.BlockSpec(memory_space=pltpu.SMEM),
                  pl.BlockSpec((1, ATT_BLOCK, 2 * ATT_BLOCK), lambda g: (g, 0, 0)),
                  pl.BlockSpec((1, ATT_BLOCK, 2 * ATT_BLOCK), lambda g: (g, 0, 0))],
        out_specs=pl.BlockSpec((1, HEADS_PER_GROUP, ATT_BLOCK, 2 * ATT_BLOCK),
                               lambda g: (g, 0, 0, 0)),
        out_shape=jax.ShapeDtypeStruct((N_DIL, HEADS_PER_GROUP, ATT_BLOCK, 2 * ATT_BLOCK), f32),
        compiler_params=_params(("arbitrary",)),
        name="relbias",
    )(rel_bias, bucket, band)


ATT_MAX_STEP_BLOCKS = 8


def _attn_kernel(nres, nblk, q_ref, kp_ref, kc_ref, vp_ref, vc_ref, bias_ref, o_ref, lse_ref):
    first = pl.program_id(2) == 0
    lane = lax.broadcasted_iota(i32, (ATT_BLOCK, LANES), 1)
    lo_half = lane < HEAD_DIM
    prev_cols = lax.broadcasted_iota(i32, (ATT_BLOCK, 2 * ATT_BLOCK), 1) < ATT_BLOCK
    no_prev = jnp.logical_and(first, prev_cols)
    nt = (((1,), (1,)), ((), ()))
    ones = jnp.ones((2 * ATT_BLOCK, LANES), bf16)
    n_slab = ATT_WIDTH // LANES
    blocks = [(res, i) for res in range(nres) for i in range(nblk)]
    logits, v_ext = [], []
    for res, i in blocks:
        cur = slice(i * ATT_BLOCK, (i + 1) * ATT_BLOCK)
        prv = slice((i - 1) * ATT_BLOCK, i * ATT_BLOCK)
        for j in range(n_slab):
            sl = slice(j * LANES, (j + 1) * LANES)
            q = q_ref[0, res, cur, sl] * (HEAD_DIM ** -0.5)
            k_prev = kp_ref[0, res, :, sl] if i == 0 else kc_ref[0, res, prv, sl]
            v_prev = vp_ref[0, res, :, sl] if i == 0 else vc_ref[0, res, prv, sl]
            k_cat = jnp.concatenate([k_prev, kc_ref[0, res, cur, sl]], axis=0)
            v_cat = jnp.concatenate([v_prev, vc_ref[0, res, cur, sl]], axis=0)
            v_ext.append(jnp.concatenate([v_cat, ones], axis=1))
            for hh in range(2):
                qm = jnp.where(lo_half if hh == 0 else jnp.logical_not(lo_half), q, 0.0).astype(bf16)
                lg_h = lax.dot_general(qm, k_cat, nt, preferred_element_type=f32) + bias_ref[0, 2 * j + hh]
                logits.append(jnp.where(no_prev, NEG_INF, lg_h) if i == 0 else lg_h)
    rows_per_block = HEADS_PER_GROUP * ATT_BLOCK
    lg = jnp.concatenate(logits, axis=0)
    m = jnp.max(lg, axis=-1, keepdims=True)
    p = jnp.exp(lg - m).astype(bf16)
    for b, (res, i) in enumerate(blocks):
        cur = slice(i * ATT_BLOCK, (i + 1) * ATT_BLOCK)
        lse_tile = jnp.zeros((ATT_BLOCK, LANES), f32)
        for j in range(n_slab):
            outs = []
            for hh in range(2):
                h = 2 * j + hh
                r0 = b * rows_per_block + h * ATT_BLOCK
                r = jnp.dot(p[r0:r0 + ATT_BLOCK], v_ext[b * n_slab + j], preferred_element_type=f32)
                den = r[:, LANES:]
                outs.append(r[:, :LANES] * (1.0 / den))
                lse_h = m[r0:r0 + ATT_BLOCK] + jnp.log(den)
                lse_tile = jnp.where(lane == h, lse_h, lse_tile)
            o_ref[0, res, cur, j * LANES:(j + 1) * LANES] = jnp.where(lo_half, outs[0], outs[1]).astype(bf16)
        lse_ref[0, res, cur, :] = lse_tile


def _attn_group(qkv_g, bias, g, dil, batch, seq):
    l = seq // dil
    nblk = min(ATT_MAX_STEP_BLOCKS, l // ATT_BLOCK)
    nres = min(dil, ATT_MAX_STEP_BLOCKS // nblk)
    tm = nblk * ATT_BLOCK
    nsteps = l // tm

    def cur(cb):
        return pl.BlockSpec((1, nres, tm, ATT_WIDTH), lambda b, r, n: (b, r, n, cb))

    def prev(cb):
        return pl.BlockSpec((1, nres, ATT_BLOCK, ATT_WIDTH),
                            lambda b, r, n: (b, r, jnp.maximum(n * nblk - 1, 0), cb))

    return pl.pallas_call(
        functools.partial(_attn_kernel, nres, nblk),
        grid=(batch, dil // nres, nsteps),
        in_specs=[cur(0), prev(1), cur(1), prev(2), cur(2),
                  pl.BlockSpec((1, HEADS_PER_GROUP, ATT_BLOCK, 2 * ATT_BLOCK),
                               lambda b, r, n: (g, 0, 0, 0))],
        out_specs=[pl.BlockSpec((1, nres, tm, ATT_WIDTH), lambda b, r, n: (b, r, n, 0)),
                   pl.BlockSpec((1, nres, tm, LANES), lambda b, r, n: (b, r, n, 0))],
        out_shape=[jax.ShapeDtypeStruct((batch, dil, l, ATT_WIDTH), bf16),
                   jax.ShapeDtypeStruct((batch, dil, l, LANES), f32)],
        compiler_params=_params(("arbitrary", "arbitrary", "arbitrary")),
        name=f"attn_g{g}",
    )(qkv_g, qkv_g, qkv_g, qkv_g, qkv_g, bias)


ROW_WORDS = D_MODEL // 2
ROW_SUB = ROW_WORDS // LANES
HI_MASK = -65536


def _pack_rows(x):
    bits = lax.bitcast_convert_type(x.astype(bf16).astype(f32), i32)
    return lax.shift_right_logical(bits[:, :ROW_WORDS], 16) | (bits[:, ROW_WORDS:] & HI_MASK)


def _unpack_rows(words):
    lo = lax.bitcast_convert_type(lax.shift_left(words, 16), f32)
    hi = lax.bitcast_convert_type(words & HI_MASK, f32)
    return jnp.concatenate([lo, hi], axis=1)


def _store_packed(ref, words, n, first_row=0):
    for r in range(ROW_SUB):
        ref[pl.ds(first_row * ROW_SUB + r, n, stride=ROW_SUB), :] = words[:, r * LANES:(r + 1) * LANES]


def _load_packed(ref, first_row, n):
    return jnp.concatenate([ref[pl.ds(first_row * ROW_SUB + r, n, stride=ROW_SUB), :] for r in range(ROW_SUB)],
                           axis=1)


MIX_TM = 256
MIX_SUB = 128


def _split_bf16(x, parts):
    out = []
    for _ in range(parts):
        hi = x.astype(bf16)
        out.append(hi)
        x = x - hi.astype(f32)
    return out


def _mix_kernel(o0_ref, o1_ref, o2_ref, l0_ref, l1_ref, l2_ref, pt1_ref, pt2_ref, ex_ref,
                ya_ref, gt_ref, x_ref,
                g1_ref, sc2_ref, sh2_ref, wa32_ref, wb32_ref, wo32_ref, ln1g_ref, ln1b_ref,
                wrc_ref, br_ref, tri_ref,
                x1_ref, h2_ref, route_ref, rw_ref, cnt_ref, run_ref, xr_ref, wa_ref, wb_ref, wo_ref):
    step = pl.program_id(0)

    @pl.when(step == 0)
    def _():
        run_ref[...] = jnp.zeros_like(run_ref)
        xr_ref[...] = jnp.zeros_like(xr_ref)
        wa_ref[...] = wa32_ref[...].astype(bf16)
        wb_ref[...] = wb32_ref[...].astype(bf16)
        wo_ref[...] = wo32_ref[...].astype(bf16)

    def back_rows(r0):
        rows = slice(r0, r0 + MIX_SUB)
        x1 = _ln(xr_ref[rows, :]) * ln1g_ref[...] + ln1b_ref[...]
        x1_ref[rows, :] = x1
        h2 = _ln(x1) * (1.0 + sc2_ref[0]) + sh2_ref[0]
        _store_packed(h2_ref, _pack_rows(h2), MIX_SUB, r0)
        h_hi, h_lo = _split_bf16(h2, 2)
        hi_both = jnp.dot(h_hi, wrc_ref[...], preferred_element_type=f32)
        return (hi_both[:, :LANES]
                + (hi_both[:, LANES:] + jnp.dot(h_lo, wrc_ref[:, :LANES], preferred_element_type=f32))
                ) + br_ref[...]

    def front_rows(r0):
        rows = slice(r0, r0 + MIX_SUB)
        os_, ls_ = [o0_ref[0, 0, rows, :].astype(f32)], [l0_ref[0, 0, rows, :]]
        for o_ref, l_ref, pt_ref in ((o1_ref, l1_ref, pt1_ref), (o2_ref, l2_ref, pt2_ref)):
            pt = pt_ref[rows, :]
            os_.append(jnp.dot(pt, o_ref[0].reshape(MIX_TM, ATT_WIDTH), preferred_element_type=f32))
            parts = [jnp.dot(pt, part, preferred_element_type=f32)
                     for part in _split_bf16(l_ref[0].reshape(MIX_TM, LANES), 3)]
            ls_.append((parts[0] + parts[1]) + parts[2])
        lm = jnp.maximum(jnp.maximum(ls_[0], ls_[1]), ls_[2])
        es = [jnp.exp(lse - lm) for lse in ls_]
        inv = 1.0 / (es[0] + es[1] + es[2])
        yb = jnp.zeros((MIX_SUB, ATT_WIDTH), f32)
        for e, o in zip(es, os_):
            w_parts = jnp.concatenate(_split_bf16(e * inv, 2), axis=1)
            yb = yb + jnp.dot(w_parts, ex_ref[...], preferred_element_type=f32) * o
        a = jnp.dot(ya_ref[rows, :], wa_ref[...], preferred_element_type=f32)
        b = jnp.dot(yb.astype(bf16), wb_ref[...], preferred_element_type=f32)
        merged = gt_ref[rows, :D_MODEL].astype(f32) * a + gt_ref[rows, D_MODEL:].astype(f32) * b
        mix = jnp.dot(merged.astype(bf16), wo_ref[...], preferred_element_type=f32)
        xr_ref[rows, :] = DN_ALPHA * x_ref[rows, :] + g1_ref[0] * mix

    logit_parts = []
    for r0 in range(0, MIX_TM, MIX_SUB):
        logit_parts.append(back_rows(r0))
        front_rows(r0)
    logits = jnp.concatenate(logit_parts, axis=0)
    lane = lax.broadcasted_iota(i32, (MIX_TM, LANES), 1)
    logits = jnp.where(lane < N_EXPERTS, logits, -jnp.inf)
    lane_f = lane.astype(f32)
    vals, idxs = [], []
    for _k in range(TOP_K):
        m = jnp.max(logits, axis=-1, keepdims=True)
        vals.append(m)
        idxs.append(jnp.min(jnp.where(logits == m, lane_f, float(LANES)), axis=-1, keepdims=True).astype(i32))
        logits = jnp.where(lane == idxs[-1], -jnp.inf, logits)
    exps = [jnp.exp(v - vals[0]) for v in vals]
    den = exps[0] + exps[1] + exps[2] + exps[3]
    wts = [e / den for e in exps]
    hits = [lane == idx for idx in idxs]
    counted = jnp.where(step > 0, 1.0, 0.0)
    onehot = jnp.zeros((MIX_TM, LANES), f32)
    for hit in hits:
        onehot = onehot + jnp.where(hit, counted, 0.0)
    prefix = jnp.dot(tri_ref[...], onehot.astype(bf16), preferred_element_type=f32) + run_ref[...]
    route = jnp.zeros((MIX_TM, LANES), i32)
    rw = jnp.zeros((MIX_TM, LANES), f32)
    for k in range(TOP_K):
        rank = jnp.sum(jnp.where(hits[k], prefix, 0.0), axis=-1, keepdims=True).astype(i32)
        route = jnp.where(lane == k, idxs[k], route)
        route = jnp.where(lane == TOP_K + k, rank, route)
        rw = jnp.where(lane == k, wts[k], rw)
    route_ref[...] = route
    rw_ref[...] = rw
    run = run_ref[...] + jnp.sum(onehot, axis=0, keepdims=True)
    run_ref[...] = run
    cnt_ref[...] = jnp.broadcast_to(run, cnt_ref.shape)


def _mix(os_, ls_, perms_t, expand, ya, gates, x2, g1, sc2, sh2, wa, wb, wo, ln1g, ln1b, wr_parts, br, tri, seq):
    t = x2.shape[0]
    nb = t // MIX_TM
    per_b = seq // MIX_TM
    cur = lambda i: jnp.minimum(i, nb - 1)
    prv = lambda i: jnp.maximum(i - 1, 0)
    row = lambda w: pl.BlockSpec((MIX_TM, w), lambda i: (cur(i), 0))
    out_row = lambda w: pl.BlockSpec((MIX_TM, w), lambda i: (prv(i), 0))
    const = lambda s: pl.BlockSpec(s, lambda i: tuple(0 for _ in s))
    mod_cur = pl.BlockSpec((1, 1, D_MODEL), lambda i: (cur(i) // per_b, 0, 0))
    mod_prv = pl.BlockSpec((1, 1, D_MODEL), lambda i: (prv(i) // per_b, 0, 0))
    grp = lambda w: [pl.BlockSpec((1, dil, MIX_TM // dil, w), lambda i: (cur(i) // per_b, 0, cur(i) % per_b, 0))
                     for _win, dil in DIL_PAIRS]
    return pl.pallas_call(
        _mix_kernel,
        grid=(nb + 1,),
        in_specs=grp(ATT_WIDTH) + grp(LANES) + [
                  const((MIX_TM, MIX_TM)), const((MIX_TM, MIX_TM)), const((2 * LANES, ATT_WIDTH)),
                  row(GM_WIDTH), row(GATE_COLS), row(D_MODEL),
                  mod_cur, mod_prv, mod_prv,
                  const((GM_WIDTH, D_MODEL)), const((ATT_WIDTH, D_MODEL)), const((D_MODEL, D_MODEL)),
                  const((1, D_MODEL)), const((1, D_MODEL)),
                  const((D_MODEL, 2 * LANES)), const((1, LANES)), const((MIX_TM, MIX_TM))],
        out_specs=[out_row(D_MODEL), pl.BlockSpec((MIX_TM * ROW_SUB, LANES), lambda i: (prv(i), 0)),
                   out_row(LANES), out_row(LANES), const((8, LANES))],
        out_shape=[jax.ShapeDtypeStruct((t, D_MODEL), f32),
                   jax.ShapeDtypeStruct((t * ROW_SUB, LANES), i32),
                   jax.ShapeDtypeStruct((t, LANES), i32),
                   jax.ShapeDtypeStruct((t, LANES), f32),
                   jax.ShapeDtypeStruct((8, LANES), f32)],
        scratch_shapes=[pltpu.VMEM((1, LANES), f32), pltpu.VMEM((MIX_TM, D_MODEL), f32),
                        pltpu.VMEM((GM_WIDTH, D_MODEL), bf16), pltpu.VMEM((ATT_WIDTH, D_MODEL), bf16),
                        pltpu.VMEM((D_MODEL, D_MODEL), bf16)],
        compiler_params=_params(("arbitrary",)),
        name="mix",
    )(*os_, *ls_, perms_t[1], perms_t[2], expand, ya, gates, x2, g1, sc2, sh2, wa, wb, wo,
      ln1g, ln1b, wr_parts, br, tri)


DISP_TM = 512
MOE_TM = 512


def _dispatch_kernel(pends_ref, pcnt_ref, nused_ref, dest_ref, h2p_ref, xs_hbm, zbuf, sem, zsem):
    i = pl.program_id(0)
    ntile = xs_hbm.shape[0] // (MOE_TM * ROW_SUB)

    def zero_tile(first_row):
        return pltpu.make_async_copy(
            zbuf, xs_hbm.at[pl.ds(pl.multiple_of(first_row * ROW_SUB, MOE_TM * ROW_SUB), MOE_TM * ROW_SUB)], zsem)

    def for_each_zero_tile(fn):
        for e in range(N_EXPERTS):
            pl.when(pcnt_ref[e] > 0)(functools.partial(fn, lambda e=e: zero_tile(pends_ref[e] - MOE_TM)))
        for k in range(N_EXPERTS):
            tile = nused_ref[0] + k
            pl.when(tile < ntile)(functools.partial(fn, lambda tile=tile: zero_tile(tile * MOE_TM)))

    @pl.when(i == 0)
    def _():
        zbuf[...] = jnp.zeros_like(zbuf)
        for_each_zero_tile(lambda mk: mk().start())
        for_each_zero_tile(lambda mk: mk().wait())

    def row_copy(k, r):
        d = dest_ref[0, k, r]
        return pltpu.make_async_copy(h2p_ref.at[pl.ds(r * ROW_SUB, ROW_SUB)],
                                     xs_hbm.at[pl.ds(pl.multiple_of(d * ROW_SUB, ROW_SUB), ROW_SUB)], sem)

    for r in range(DISP_TM):
        for k in range(TOP_K):
            row_copy(k, r).start(priority=k % 2)
    for k in range(TOP_K):
        pltpu.make_async_copy(h2p_ref, xs_hbm.at[pl.ds(0, DISP_TM * ROW_SUB)], sem).wait()


def _dispatch(pends, pcounts, n_used, dest3, h2p, ntile):
    t = h2p.shape[0] // ROW_SUB
    grid_spec = pltpu.PrefetchScalarGridSpec(
        num_scalar_prefetch=3,
        grid=(t // DISP_TM,),
        in_specs=[pl.BlockSpec((1, TOP_K, DISP_TM), lambda i, *_: (i, 0, 0), memory_space=pltpu.SMEM),
                  pl.BlockSpec((DISP_TM * ROW_SUB, LANES), lambda i, *_: (i, 0))],
        out_specs=pl.BlockSpec(memory_space=pl.ANY),
        scratch_shapes=[pltpu.VMEM((MOE_TM * ROW_SUB, LANES), i32),
                        pltpu.SemaphoreType.DMA(()),
                        pltpu.SemaphoreType.DMA(())],
    )
    return pl.pallas_call(
        _dispatch_kernel,
        grid_spec=grid_spec,
        out_shape=jax.ShapeDtypeStruct((ntile * MOE_TM * ROW_SUB, LANES), i32),
        compiler_params=_params(("arbitrary",)),
        name="dispatch",
    )(pends, pcounts, n_used, dest3, h2p)


def _moe_kernel(te_ref, first_ref, nexte_ref, wslot_ref, nused_ref,
                xs_ref, wg_hbm, wu_hbm, wd_hbm, bg_ref, bu_ref, bd_ref,
                out_ref, wbuf, wgb, wub, wdb, sem_w):
    j = pl.program_id(0)

    def weight_copies(e, ws):
        return [pltpu.make_async_copy(w.at[e], wbuf.at[ws, k], sem_w.at[ws])
                for k, w in enumerate((wg_hbm, wu_hbm, wd_hbm))]

    @pl.when(j == 0)
    def _():
        for cp in weight_copies(te_ref[0], wslot_ref[0]):
            cp.start()

    @pl.when(first_ref[j] == 1)
    def _():
        ws = wslot_ref[j]
        for cp in weight_copies(te_ref[j], ws):
            cp.wait()
        wgb[...] = wbuf[ws, 0].astype(bf16)
        wub[...] = wbuf[ws, 1].astype(bf16)
        wdb[...] = wbuf[ws, 2].astype(bf16)
        ne = nexte_ref[j]

        @pl.when(ne >= 0)
        def _():
            for cp in weight_copies(ne, 1 - ws):
                cp.start()

    used = j < nused_ref[0]

    @pl.when(used)
    def _():
        xb = _unpack_rows(_load_packed(xs_ref, 0, MOE_TM)).astype(bf16)
        g = jnp.dot(xb, wgb[...], preferred_element_type=f32) + bg_ref[0]
        u = jnp.dot(xb, wub[...], preferred_element_type=f32) + bu_ref[0]
        g = jnp.minimum(g, SWIGLU_LIMIT)
        u = jnp.clip(u, -SWIGLU_LIMIT, SWIGLU_LIMIT)
        act = (u + 1.0) * (g * jax.nn.sigmoid(SWIGLU_ALPHA * g))
        y = jnp.dot(act.astype(bf16), wdb[...], preferred_element_type=f32) + bd_ref[0]
        _store_packed(out_ref, _pack_rows(y), MOE_TM)

    @pl.when(jnp.logical_not(used))
    def _():
        out_ref[...] = jnp.zeros_like(out_ref)


def _moe(tile_e, tile_first, next_e, wslot, n_used, xs, w_gate, b_gate, w_up, b_up, w_down, b_down):
    ntile = tile_e.shape[0]
    bspec = pl.BlockSpec((1, 1, D_MODEL), lambda j, te, *_: (te[j], 0, 0))
    hbm = pl.BlockSpec(memory_space=pl.ANY)
    grid_spec = pltpu.PrefetchScalarGridSpec(
        num_scalar_prefetch=5,
        grid=(ntile,),
        in_specs=[pl.BlockSpec((MOE_TM * ROW_SUB, LANES),
                               lambda j, te, fi, ne, ws, nu: (jnp.minimum(j, nu[0] - 1), 0)),
                  hbm, hbm, hbm, bspec, bspec, bspec],
        out_specs=pl.BlockSpec((MOE_TM * ROW_SUB, LANES), lambda j, *_: (j, 0)),
        scratch_shapes=[pltpu.VMEM((2, 3, D_MODEL, D_MODEL), f32),
                        pltpu.VMEM((D_MODEL, D_MODEL), bf16),
                        pltpu.VMEM((D_MODEL, D_MODEL), bf16),
                        pltpu.VMEM((D_MODEL, D_MODEL), bf16),
                        pltpu.SemaphoreType.DMA((2,))],
    )
    return pl.pallas_call(
        _moe_kernel,
        grid_spec=grid_spec,
        out_shape=jax.ShapeDtypeStruct((ntile * MOE_TM * ROW_SUB, LANES), i32),
        compiler_params=_params(("arbitrary",)),
        name="moe",
    )(tile_e, tile_first, next_e, wslot, n_used, xs, w_gate, w_up, w_down, b_gate, b_up, b_down)


CB_TM = DISP_TM


def _combine_kernel(dcur_ref, dnxt_ref, yb_hbm, rw_ref, x1_ref, g2_ref, lng_ref, lnb_ref, out_ref,
                    ybuf0, ybuf1, sem):
    i = pl.program_id(0)
    last = pl.num_programs(0) - 1
    ybufs = (ybuf0, ybuf1)

    def row_copy(d, k, r, s):
        return pltpu.make_async_copy(
            yb_hbm.at[pl.ds(pl.multiple_of(d * ROW_SUB, ROW_SUB), ROW_SUB)],
            ybufs[s].at[pl.ds(pl.multiple_of((k * CB_TM + r) * ROW_SUB, ROW_SUB), ROW_SUB)],
            sem.at[s])

    @pl.when(i == 0)
    def _():
        for k in range(TOP_K):
            def body(r, c, k=k):
                row_copy(dcur_ref[0, k, r], k, r, 0).start()
                return c
            lax.fori_loop(0, CB_TM, body, 0, unroll=8)

    for s in range(2):
        @pl.when(i % 2 == s)
        def _(s=s):
            pltpu.make_async_copy(yb_hbm.at[pl.ds(0, TOP_K * CB_TM * ROW_SUB)], ybufs[s], sem.at[s]).wait()

            @pl.when(i < last)
            def _():
                for k in range(TOP_K):
                    for r in range(CB_TM):
                        row_copy(dnxt_ref[0, k, r], k, r, 1 - s).start(priority=r % 2)

            parts = [_unpack_rows(_load_packed(ybufs[s], k * CB_TM, CB_TM)) * rw_ref[:, k:k + 1]
                     for k in range(TOP_K)]
            y = (parts[0] + parts[1]) + (parts[2] + parts[3])
            out_ref[...] = _ln(DN_ALPHA * x1_ref[...] + g2_ref[0] * y) * lng_ref[...] + lnb_ref[...]


def _combine(dest3, yb, rw, x1, g2, ln2g, ln2b, seq):
    t = x1.shape[0]
    nb = t // CB_TM
    per_b = seq // CB_TM
    return pl.pallas_call(
        _combine_kernel,
        grid=(nb,),
        in_specs=[pl.BlockSpec((1, TOP_K, CB_TM), lambda i: (i, 0, 0), memory_space=pltpu.SMEM),
                  pl.BlockSpec((1, TOP_K, CB_TM), lambda i: (jnp.minimum(i + 1, nb - 1), 0, 0),
                               memory_space=pltpu.SMEM),
                  pl.BlockSpec(memory_space=pl.ANY),
                  pl.BlockSpec((CB_TM, LANES), lambda i: (i, 0)),
                  pl.BlockSpec((CB_TM, D_MODEL), lambda i: (i, 0)),
                  pl.BlockSpec((1, 1, D_MODEL), lambda i: (i // per_b, 0, 0)),
                  pl.BlockSpec((1, D_MODEL), lambda i: (0, 0)),
                  pl.BlockSpec((1, D_MODEL), lambda i: (0, 0))],
        out_specs=pl.BlockSpec((CB_TM, D_MODEL), lambda i: (i, 0)),
        out_shape=jax.ShapeDtypeStruct((t, D_MODEL), f32),
        scratch_shapes=[pltpu.VMEM((TOP_K * CB_TM * ROW_SUB, LANES), i32),
                        pltpu.VMEM((TOP_K * CB_TM * ROW_SUB, LANES), i32),
                        pltpu.SemaphoreType.DMA((2,))],
        compiler_params=_params(("arbitrary",)),
        name="combine",
    )(dest3, dest3, yb, rw, x1, g2, ln2g, ln2b)


def _t5_bucket(dist):
    d = dist.astype(f32)
    large = REL_MAX_EXACT + jnp.log(jnp.maximum(d, float(REL_MAX_EXACT)) / REL_MAX_EXACT) / math.log(
        REL_MAX_DIST / REL_MAX_EXACT) * (REL_BUCKETS - REL_MAX_EXACT)
    large = jnp.minimum(large.astype(i32), REL_BUCKETS - 1)
    return jnp.where(dist < REL_MAX_EXACT, dist, large)


def _bias_indices():
    qi = jnp.arange(ATT_BLOCK)[:, None]
    ki = jnp.arange(2 * ATT_BLOCK)[None, :]
    didx = qi + ATT_BLOCK - ki
    buckets, bands = [], []
    for win, dil in DIL_PAIRS:
        buckets.append(_t5_bucket(jnp.clip(didx, 0, None) * dil))
        bands.append(((didx >= 0) & (didx <= win // dil)).astype(i32))
    return jnp.stack(buckets).astype(i32), jnp.stack(bands)


def _residue_perm(tm, dil):
    n = tm // dil
    dst = np.arange(tm)
    src = (dst % n) * dil + dst // n
    return src[:, None] == np.arange(tm)[None, :]


def kernel(x, c, w_ada, b_ada, w_in, gm_ln_g, gm_ln_b, gm_w_s, gm_b_s, w_branch_a, w_branch_b, w_out,
           rel_bias, ln1_g, ln1_b, w_router, b_router, w_gate, b_gate, w_up, b_up, w_down, b_down,
           ln2_g, ln2_b):
    batch, seq, _ = x.shape
    t = batch * seq
    l = 0
    x2 = x.reshape(t, D_MODEL)

    c8 = jnp.pad(c, ((0, 8 - batch), (0, 0)))
    mod = _adaln(c8, w_ada[l], b_ada[l][None, :])[:batch]
    sh1, sc1, g1, sh2, sc2, g2 = [m[:, None, :] for m in jnp.split(mod, 6, axis=-1)]

    perms = [jnp.asarray(_residue_perm(IN_TM, dil), bf16) for _win, dil in DIL_PAIRS]
    bs_full = jnp.repeat(gm_b_s[l].T, GM_WIDTH // GM_GROUPS, axis=1)
    ya, gates, *qkvs = _inproj(x2, sc1, sh1, w_in[l].astype(bf16), perms,
                               gm_ln_g[l][None, :], gm_ln_b[l][None, :], gm_w_s[l], bs_full, batch, seq)

    bucket, band = _bias_indices()
    bias = _relbias(rel_bias, bucket, band)
    os_, ls_ = [], []
    for g, (_win, dil) in enumerate(DIL_PAIRS):
        o, lse = _attn_group(qkvs[g], bias, g, dil, batch, seq)
        os_.append(o)
        ls_.append(lse)

    wr = jnp.pad(w_router[l], ((0, 0), (0, LANES - N_EXPERTS)))
    wr_hi = wr.astype(bf16)
    wr_parts = jnp.concatenate([wr_hi, (wr - wr_hi.astype(f32)).astype(bf16)], axis=1)
    br = jnp.pad(b_router[l], (0, LANES - N_EXPERTS))[None, :]
    tri = jnp.asarray(np.arange(MIX_TM)[None, :] < np.arange(MIX_TM)[:, None], bf16)
    perms_t = [jnp.asarray(_residue_perm(MIX_TM, dil).T, bf16) for _win, dil in DIL_PAIRS]
    expand = np.arange(LANES)[:, None] == np.arange(ATT_WIDTH)[None, :] // HEAD_DIM
    expand = jnp.asarray(np.concatenate([expand, expand], axis=0), bf16)
    x1, h2, route, rw, cnt = _mix(
        os_, ls_, perms_t, expand, ya, gates, x2, g1, sc2, sh2,
        w_branch_a[l], w_branch_b[l], w_out[l],
        ln1_g[l][None, :], ln1_b[l][None, :], wr_parts, br, tri, seq)

    top_e = route[:, :TOP_K]
    rank = route[:, TOP_K:2 * TOP_K]
    counts = cnt[0, :N_EXPERTS].astype(i32)
    pcounts = (counts + MOE_TM - 1) // MOE_TM * MOE_TM
    experts = jnp.arange(N_EXPERTS, dtype=i32)
    upto = experts[None, :] <= experts[:, None]
    pends = jnp.sum(jnp.where(upto, pcounts[None, :], 0), axis=1)
    pstarts = pends - pcounts
    dest = jnp.sum(jnp.where(top_e[:, :, None] == experts, pstarts, 0), axis=-1) + rank
    ntile = t * TOP_K // MOE_TM + N_EXPERTS
    n_used = (pends[-1] // MOE_TM).reshape(1)
    tile_idx = jnp.minimum(jnp.arange(ntile, dtype=i32), n_used - 1)
    tile_e = jnp.sum((pends[None, :] <= (tile_idx * MOE_TM)[:, None]).astype(i32), axis=1)
    tile_first = jnp.concatenate([jnp.ones((1,), i32), (tile_e[1:] != tile_e[:-1]).astype(i32)])
    nonempty = counts > 0
    later = jnp.logical_and(experts[None, :] > experts[:, None], nonempty[None, :])
    next_nonempty = jnp.min(jnp.where(later, experts[None, :], N_EXPERTS), axis=1)
    next_nonempty = jnp.where(next_nonempty >= N_EXPERTS, -1, next_nonempty)
    expert_slot = (jnp.sum(jnp.logical_and(upto, nonempty[None, :]).astype(i32), axis=1) - 1) % 2
    of_tile = tile_e[:, None] == experts[None, :]
    tile_next = jnp.sum(jnp.where(of_tile, next_nonempty[None, :], 0), axis=1)
    tile_slot = jnp.sum(jnp.where(of_tile, expert_slot[None, :], 0), axis=1)

    dest3 = dest.reshape(t // DISP_TM, DISP_TM, TOP_K).transpose(0, 2, 1)
    xs = _dispatch(pends, pcounts, n_used, dest3, h2, ntile)
    yb = _moe(tile_e, tile_first, tile_next, tile_slot, n_used, xs,
              w_gate[l], b_gate[l][:, None, :], w_up[l], b_up[l][:, None, :],
              w_down[l], b_down[l][:, None, :])
    out = _combine(dest3, yb, rw, x1, g2, ln2_g[l][None, :], ln2_b[l][None, :], seq)
    return out.reshape(batch, seq, D_MODEL)
```

```python
import functools
import math

import numpy as np
import jax
import jax.numpy as jnp
from jax import lax
from jax.experimental import pallas as pl
from jax.experimental.pallas import tpu as pltpu

f32 = jnp.float32
bf16 = jnp.bfloat16
i32 = jnp.int32

D_MODEL = 1024
GM_WIDTH = 512
GM_GROUPS = 8
GM_CHUNK = 128
DIL_PAIRS = ((128, 1), (512, 4), (2048, 16))
N_DIL = 3
HEADS_PER_GROUP = 8
HEAD_DIM = 64
ATT_WIDTH = 512
ATT_BLOCK = 128
NEG_INF = -1e30
REL_BUCKETS = 32
REL_MAX_EXACT = 16
REL_MAX_DIST = 2048
N_EXPERTS = 32
TOP_K = 4
SWIGLU_LIMIT = 7.0
SWIGLU_ALPHA = 1.702
DEPTH = 1
DN_ALPHA = (2 * DEPTH) ** 0.25
LN_EPS = 1e-5
UV_COLS = 2 * GM_WIDTH
QKV_COLS = N_DIL * 3 * ATT_WIDTH
GATE_COLS = 2 * D_MODEL
IN_COLS = UV_COLS + QKV_COLS + GATE_COLS

LANES = 128
VMEM_LIMIT = 56 * 1024 * 1024


def _ln(x):
    mu = jnp.mean(x, axis=-1, keepdims=True)
    xc = x - mu
    var = jnp.mean(xc * xc, axis=-1, keepdims=True)
    return xc * lax.rsqrt(var + LN_EPS)


def _params(sem, vmem=VMEM_LIMIT):
    return pltpu.CompilerParams(dimension_semantics=sem, vmem_limit_bytes=vmem)


def _adaln_kernel(c_ref, w_ref, b_ref, o_ref):
    c = c_ref[...]
    s = c * jax.nn.sigmoid(c)
    o_ref[...] = jnp.dot(s, w_ref[...], preferred_element_type=f32,
                         precision=lax.Precision.HIGHEST) + b_ref[...]


def _adaln(c8, w_ada, b_ada):
    n = w_ada.shape[1] // D_MODEL
    return pl.pallas_call(
        _adaln_kernel,
        grid=(n,),
        in_specs=[pl.BlockSpec((8, D_MODEL), lambda j: (0, 0)),
                  pl.BlockSpec((D_MODEL, D_MODEL), lambda j: (0, j)),
                  pl.BlockSpec((1, D_MODEL), lambda j: (0, j))],
        out_specs=pl.BlockSpec((8, D_MODEL), lambda j: (0, j)),
        out_shape=jax.ShapeDtypeStruct((8, w_ada.shape[1]), f32),
        compiler_params=_params(("arbitrary",)),
        name="adaln",
    )(c8, w_ada, b_ada)


IN_TM = 256
IN_CW = 512
GRP_COLS = 3 * ATT_WIDTH


def _inproj_kernel(x_ref, sc_ref, sh_ref, w_ref, p1_ref, p2_ref, gm_g_ref, gm_b_ref, ws_ref, bs_ref,
                   ya_ref, gt_ref, qkv0_ref, qkv1_ref, qkv2_ref):
    xn = _ln(x_ref[...])
    h = (xn * (1.0 + sc_ref[0]) + sh_ref[0]).astype(bf16)
    hp = [h,
          jnp.dot(p1_ref[...], h, preferred_element_type=f32).astype(bf16),
          jnp.dot(p2_ref[...], h, preferred_element_type=f32).astype(bf16)]
    u_act, v_act = [jax.nn.gelu(jnp.dot(h, w_ref[:, c0:c0 + GM_WIDTH], preferred_element_type=f32))
                    for c0 in range(0, UV_COLS, GM_WIDTH)]
    row = lax.broadcasted_iota(i32, (GM_CHUNK, GM_CHUNK), 0)
    col = lax.broadcasted_iota(i32, (GM_CHUNK, GM_CHUNK), 1)
    first_half = lax.broadcasted_iota(i32, (GM_CHUNK, LANES), 1) < (GM_WIDTH // GM_GROUPS)
    ws = [jnp.where(col <= row, ws_ref[g], 0.0).astype(bf16) for g in range(GM_GROUPS)]

    def gate_chunk(r0):
        vn = (_ln(v_act[r0:r0 + GM_CHUNK, :]) * gm_g_ref[...] + gm_b_ref[...]).astype(bf16)
        for j in range(GM_WIDTH // LANES):
            sl = slice(j * LANES, (j + 1) * LANES)
            s_lo = jnp.dot(ws[2 * j], vn[:, sl], preferred_element_type=f32)
            s_hi = jnp.dot(ws[2 * j + 1], vn[:, sl], preferred_element_type=f32)
            s = jnp.where(first_half, s_lo, s_hi) + bs_ref[:, sl]
            ya_ref[r0:r0 + GM_CHUNK, sl] = (u_act[r0:r0 + GM_CHUNK, sl] * s).astype(bf16)

    pending = list(range(0, IN_TM, GM_CHUNK))
    for g, (qref, (_win, dil)) in enumerate(zip((qkv0_ref, qkv1_ref, qkv2_ref), DIL_PAIRS)):
        n = IN_TM // dil
        for q0 in range(0, GRP_COLS, IN_CW):
            c0 = UV_COLS + g * GRP_COLS + q0
            acc = jnp.dot(hp[g], w_ref[:, c0:c0 + IN_CW], preferred_element_type=f32).astype(bf16)
            for rho in range(dil):
                qref[0, rho, :, q0:q0 + IN_CW] = acc[rho * n:(rho + 1) * n, :]
        if pending:
            gate_chunk(pending.pop(0))
    while pending:
        gate_chunk(pending.pop(0))
    for g0 in range(0, GATE_COLS, IN_CW):
        c0 = UV_COLS + QKV_COLS + g0
        acc = jnp.dot(h, w_ref[:, c0:c0 + IN_CW], preferred_element_type=f32)
        gt_ref[:, g0:g0 + IN_CW] = jax.nn.sigmoid(acc).astype(bf16)


def _inproj(x2, sc1, sh1, w_in_bf, perms, gm_g, gm_b, w_s, bs_full, batch, seq):
    t = x2.shape[0]
    per_b = seq // IN_TM
    qkv_specs, qkv_shapes = [], []
    for _win, dil in DIL_PAIRS:
        n = IN_TM // dil
        qkv_specs.append(pl.BlockSpec((1, dil, n, GRP_COLS), lambda i: (i // per_b, 0, i % per_b, 0)))
        qkv_shapes.append(jax.ShapeDtypeStruct((batch, dil, seq // dil, GRP_COLS), bf16))
    return pl.pallas_call(
        _inproj_kernel,
        grid=(t // IN_TM,),
        in_specs=[pl.BlockSpec((IN_TM, D_MODEL), lambda i: (i, 0)),
                  pl.BlockSpec((1, 1, D_MODEL), lambda i: (i // per_b, 0, 0)),
                  pl.BlockSpec((1, 1, D_MODEL), lambda i: (i // per_b, 0, 0)),
                  pl.BlockSpec((D_MODEL, IN_COLS), lambda i: (0, 0)),
                  pl.BlockSpec((IN_TM, IN_TM), lambda i: (0, 0)),
                  pl.BlockSpec((IN_TM, IN_TM), lambda i: (0, 0)),
                  pl.BlockSpec((1, GM_WIDTH), lambda i: (0, 0)),
                  pl.BlockSpec((1, GM_WIDTH), lambda i: (0, 0)),
                  pl.BlockSpec((GM_GROUPS, GM_CHUNK, GM_CHUNK), lambda i: (0, 0, 0)),
                  pl.BlockSpec((GM_CHUNK, GM_WIDTH), lambda i: (0, 0))],
        out_specs=[pl.BlockSpec((IN_TM, GM_WIDTH), lambda i: (i, 0)),
                   pl.BlockSpec((IN_TM, GATE_COLS), lambda i: (i, 0))] + qkv_specs,
        out_shape=[jax.ShapeDtypeStruct((t, GM_WIDTH), bf16),
                   jax.ShapeDtypeStruct((t, GATE_COLS), bf16)] + qkv_shapes,
        compiler_params=_params(("arbitrary",)),
        name="inproj",
    )(x2, sc1, sh1, w_in_bf, perms[1], perms[2], gm_g, gm_b, w_s, bs_full)


def _relbias_kernel(tab_ref, bucket_ref, band_ref, out_ref):
    g = pl.program_id(0)
    bk = bucket_ref[0]
    band = band_ref[0] > 0
    for h in range(HEADS_PER_GROUP):
        acc = jnp.zeros((ATT_BLOCK, 2 * ATT_BLOCK), f32)
        for b in range(REL_BUCKETS):
            acc = jnp.where(bk == b, tab_ref[b, g * HEADS_PER_GROUP + h], acc)
        out_ref[0, h] = jnp.where(band, acc, NEG_INF)


def _relbias(rel_bias, bucket, band):
    return pl.pallas_call(
        _relbias_kernel,
        grid=(N_DIL,),
        in_specs=[pl.BlockSpec(memory_space=pltpu.SMEM),
                  pl.BlockSpec((1, ATT_BLOCK, 2 * ATT_BLOCK), lambda g: (g, 0, 0)),
                  pl.BlockSpec((1, ATT_BLOCK, 2 * ATT_BLOCK), lambda g: (g, 0, 0))],
        out_specs=pl.BlockSpec((1, HEADS_PER_GROUP, ATT_BLOCK, 2 * ATT_BLOCK),
                               lambda g: (g, 0, 0, 0)),
        out_shape=jax.ShapeDtypeStruct((N_DIL, HEADS_PER_GROUP, ATT_BLOCK, 2 * ATT_BLOCK), f32),
        compiler_params=_params(("arbitrary",)),
        name="relbias",
    )(rel_bias, bucket, band)


ATT_MAX_STEP_BLOCKS = 8


def _attn_kernel(nres, nblk, q_ref, kp_ref, kc_ref, vp_ref, vc_ref, bias_ref, o_ref, lse_ref):
    first = pl.program_id(2) == 0
    lane = lax.broadcasted_iota(i32, (ATT_BLOCK, LANES), 1)
    lo_half = lane < HEAD_DIM
    prev_cols = lax.broadcasted_iota(i32, (ATT_BLOCK, 2 * ATT_BLOCK), 1) < ATT_BLOCK
    no_prev = jnp.logical_and(first, prev_cols)
    nt = (((1,), (1,)), ((), ()))
    ones = jnp.ones((2 * ATT_BLOCK, LANES), bf16)
    n_slab = ATT_WIDTH // LANES
    blocks = [(res, i) for res in range(nres) for i in range(nblk)]
    logits, v_ext = [], []
    for res, i in blocks:
        cur = slice(i * ATT_BLOCK, (i + 1) * ATT_BLOCK)
        prv = slice((i - 1) * ATT_BLOCK, i * ATT_BLOCK)
        for j in range(n_slab):
            sl = slice(j * LANES, (j + 1) * LANES)
            q = q_ref[0, res, cur, sl] * (HEAD_DIM ** -0.5)
            k_prev = kp_ref[0, res, :, sl] if i == 0 else kc_ref[0, res, prv, sl]
            v_prev = vp_ref[0, res, :, sl] if i == 0 else vc_ref[0, res, prv, sl]
            k_cat = jnp.concatenate([k_prev, kc_ref[0, res, cur, sl]], axis=0)
            v_cat = jnp.concatenate([v_prev, vc_ref[0, res, cur, sl]], axis=0)
            v_ext.append(jnp.concatenate([v_cat, ones], axis=1))
            for hh in range(2):
                qm = jnp.where(lo_half if hh == 0 else jnp.logical_not(lo_half), q, 0.0).astype(bf16)
                lg_h = lax.dot_general(qm, k_cat, nt, preferred_element_type=f32) + bias_ref[0, 2 * j + hh]
                logits.append(jnp.where(no_prev, NEG_INF, lg_h) if i == 0 else lg_h)
    rows_per_block = HEADS_PER_GROUP * ATT_BLOCK
    lg = jnp.concatenate(logits, axis=0)
    m = jnp.max(lg, axis=-1, keepdims=True)
    p = jnp.exp(lg - m).astype(bf16)
    for b, (res, i) in enumerate(blocks):
        cur = slice(i * ATT_BLOCK, (i + 1) * ATT_BLOCK)
        lse_tile = jnp.zeros((ATT_BLOCK, LANES), f32)
        for j in range(n_slab):
            outs = []
            for hh in range(2):
                h = 2 * j + hh
                r0 = b * rows_per_block + h * ATT_BLOCK
                r = jnp.dot(p[r0:r0 + ATT_BLOCK], v_ext[b * n_slab + j], preferred_element_type=f32)
                den = r[:, LANES:]
                outs.append(r[:, :LANES] * (1.0 / den))
                lse_h = m[r0:r0 + ATT_BLOCK] + jnp.log(den)
                lse_tile = jnp.where(lane == h, lse_h, lse_tile)
            o_ref[0, res, cur, j * LANES:(j + 1) * LANES] = jnp.where(lo_half, outs[0], outs[1]).astype(bf16)
        lse_ref[0, res, cur, :] = lse_tile


def _attn_group(qkv_g, bias, g, dil, batch, seq):
    l = seq // dil
    nblk = min(ATT_MAX_STEP_BLOCKS, l // ATT_BLOCK)
    nres = min(dil, ATT_MAX_STEP_BLOCKS // nblk)
    tm = nblk * ATT_BLOCK
    nsteps = l // tm

    def cur(cb):
        return pl.BlockSpec((1, nres, tm, ATT_WIDTH), lambda b, r, n: (b, r, n, cb))

    def prev(cb):
        return pl.BlockSpec((1, nres, ATT_BLOCK, ATT_WIDTH),
                            lambda b, r, n: (b, r, jnp.maximum(n * nblk - 1, 0), cb))

    return pl.pallas_call(
        functools.partial(_attn_kernel, nres, nblk),
        grid=(batch, dil // nres, nsteps),
        in_specs=[cur(0), prev(1), cur(1), prev(2), cur(2),
                  pl.BlockSpec((1, HEADS_PER_GROUP, ATT_BLOCK, 2 * ATT_BLOCK),
                               lambda b, r, n: (g, 0, 0, 0))],
        out_specs=[pl.BlockSpec((1, nres, tm, ATT_WIDTH), lambda b, r, n: (b, r, n, 0)),
                   pl.BlockSpec((1, nres, tm, LANES), lambda b, r, n: (b, r, n, 0))],
        out_shape=[jax.ShapeDtypeStruct((batch, dil, l, ATT_WIDTH), bf16),
                   jax.ShapeDtypeStruct((batch, dil, l, LANES), f32)],
        compiler_params=_params(("arbitrary", "arbitrary", "arbitrary")),
        name=f"attn_g{g}",
    )(qkv_g, qkv_g, qkv_g, qkv_g, qkv_g, bias)


ROW_WORDS = D_MODEL // 2
ROW_SUB = ROW_WORDS // LANES
HI_MASK = -65536


def _pack_rows(x):
    bits = lax.bitcast_convert_type(x.astype(bf16).astype(f32), i32)
    return lax.shift_right_logical(bits[:, :ROW_WORDS], 16) | (bits[:, ROW_WORDS:] & HI_MASK)


def _unpack_rows(words):
    lo = lax.bitcast_convert_type(lax.shift_left(words, 16), f32)
    hi = lax.bitcast_convert_type(words & HI_MASK, f32)
    return jnp.concatenate([lo, hi], axis=1)


def _store_packed(ref, words, n, first_row=0):
    for r in range(ROW_SUB):
        ref[pl.ds(first_row * ROW_SUB + r, n, stride=ROW_SUB), :] = words[:, r * LANES:(r + 1) * LANES]


def _load_packed(ref, first_row, n):
    return jnp.concatenate([ref[pl.ds(first_row * ROW_SUB + r, n, stride=ROW_SUB), :] for r in range(ROW_SUB)],
                           axis=1)


MIX_TM = 256
MIX_SUB = 128


def _split_bf16(x, parts):
    out = []
    for _ in range(parts):
        hi = x.astype(bf16)
        out.append(hi)
        x = x - hi.astype(f32)
    return out


def _mix_kernel(o0_ref, o1_ref, o2_ref, l0_ref, l1_ref, l2_ref, pt1_ref, pt2_ref, ex_ref,
                ya_ref, gt_ref, x_ref,
                g1_ref, sc2_ref, sh2_ref, wa32_ref, wb32_ref, wo32_ref, ln1g_ref, ln1b_ref,
                wrc_ref, br_ref, tri_ref,
                x1_ref, h2_ref, route_ref, rw_ref, cnt_ref, run_ref, xr_ref, wa_ref, wb_ref, wo_ref):
    step = pl.program_id(0)

    @pl.when(step == 0)
    def _():
        run_ref[...] = jnp.zeros_like(run_ref)
        xr_ref[...] = jnp.zeros_like(xr_ref)
        wa_ref[...] = wa32_ref[...].astype(bf16)
        wb_ref[...] = wb32_ref[...].astype(bf16)
        wo_ref[...] = wo32_ref[...].astype(bf16)

    def back_norms(r0):
        rows = slice(r0, r0 + MIX_SUB)
        x1 = _ln(xr_ref[rows, :]) * ln1g_ref[...] + ln1b_ref[...]
        x1_ref[rows, :] = x1
        h2 = _ln(x1) * (1.0 + sc2_ref[0]) + sh2_ref[0]
        _store_packed(h2_ref, _pack_rows(h2), MIX_SUB, r0)
        return h2

    def back_router(h2):
        h_hi, h_lo = _split_bf16(h2, 2)
        hi_both = jnp.dot(h_hi, wrc_ref[...], preferred_element_type=f32)
        return (hi_both[:, :LANES]
                + (hi_both[:, LANES:] + jnp.dot(h_lo, wrc_ref[:, :LANES], preferred_element_type=f32))
                ) + br_ref[...]

    def front_merge(r0):
        rows = slice(r0, r0 + MIX_SUB)
        os_, ls_ = [o0_ref[0, 0, rows, :].astype(f32)], [l0_ref[0, 0, rows, :]]
        for o_ref, l_ref, pt_ref in ((o1_ref, l1_ref, pt1_ref), (o2_ref, l2_ref, pt2_ref)):
            pt = pt_ref[rows, :]
            os_.append(jnp.dot(pt, o_ref[0].reshape(MIX_TM, ATT_WIDTH), preferred_element_type=f32))
            parts = [jnp.dot(pt, part, preferred_element_type=f32)
                     for part in _split_bf16(l_ref[0].reshape(MIX_TM, LANES), 3)]
            ls_.append((parts[0] + parts[1]) + parts[2])
        lm = jnp.maximum(jnp.maximum(ls_[0], ls_[1]), ls_[2])
        es = [jnp.exp(lse - lm) for lse in ls_]
        inv = 1.0 / (es[0] + es[1] + es[2])
        yb = jnp.zeros((MIX_SUB, ATT_WIDTH), f32)
        for e, o in zip(es, os_):
            w_parts = jnp.concatenate(_split_bf16(e * inv, 2), axis=1)
            yb = yb + jnp.dot(w_parts, ex_ref[...], preferred_element_type=f32) * o
        return yb.astype(bf16)

    def front_branches(r0, yb):
        rows = slice(r0, r0 + MIX_SUB)
        a = jnp.dot(ya_ref[rows, :], wa_ref[...], preferred_element_type=f32)
        b = jnp.dot(yb, wb_ref[...], preferred_element_type=f32)
        return (gt_ref[rows, :D_MODEL].astype(f32) * a + gt_ref[rows, D_MODEL:].astype(f32) * b).astype(bf16)

    def front_out(r0, merged):
        rows = slice(r0, r0 + MIX_SUB)
        mix = jnp.dot(merged, wo_ref[...], preferred_element_type=f32)
        xr_ref[rows, :] = DN_ALPHA * x_ref[rows, :] + g1_ref[0] * mix

    subs = list(range(0, MIX_TM, MIX_SUB))
    logit_parts = []
    for r0 in subs:
        h2 = back_norms(r0)
        yb = front_merge(r0)
        logit_parts.append(back_router(h2))
        merged = front_branches(r0, yb)
        front_out(r0, merged)
    logits = jnp.concatenate(logit_parts, axis=0)
    lane = lax.broadcasted_iota(i32, (MIX_TM, LANES), 1)
    logits = jnp.where(lane < N_EXPERTS, logits, -jnp.inf)
    lane_f = lane.astype(f32)
    vals, idxs = [], []
    for _k in range(TOP_K):
        m = jnp.max(logits, axis=-1, keepdims=True)
        vals.append(m)
        idxs.append(jnp.min(jnp.where(logits == m, lane_f, float(LANES)), axis=-1, keepdims=True).astype(i32))
        logits = jnp.where(lane == idxs[-1], -jnp.inf, logits)
    exps = [jnp.exp(v - vals[0]) for v in vals]
    den = exps[0] + exps[1] + exps[2] + exps[3]
    wts = [e / den for e in exps]
    hits = [lane == idx for idx in idxs]
    counted = jnp.where(step > 0, 1.0, 0.0)
    onehot = jnp.zeros((MIX_TM, LANES), f32)
    for hit in hits:
        onehot = onehot + jnp.where(hit, counted, 0.0)
    prefix = jnp.dot(tri_ref[...], onehot.astype(bf16), preferred_element_type=f32) + run_ref[...]
    route = jnp.zeros((MIX_TM, LANES), i32)
    rw = jnp.zeros((MIX_TM, LANES), f32)
    for k in range(TOP_K):
        rank = jnp.sum(jnp.where(hits[k], prefix, 0.0), axis=-1, keepdims=True).astype(i32)
        route = jnp.where(lane == k, idxs[k], route)
        route = jnp.where(lane == TOP_K + k, rank, route)
        rw = jnp.where(lane == k, wts[k], rw)
    route_ref[...] = route
    rw_ref[...] = rw
    run = run_ref[...] + jnp.sum(onehot, axis=0, keepdims=True)
    run_ref[...] = run
    cnt_ref[...] = jnp.broadcast_to(run, cnt_ref.shape)


def _mix(os_, ls_, perms_t, expand, ya, gates, x2, g1, sc2, sh2, wa, wb, wo, ln1g, ln1b, wr_parts, br, tri, seq):
    t = x2.shape[0]
    nb = t // MIX_TM
    per_b = seq // MIX_TM
    cur = lambda i: jnp.minimum(i, nb - 1)
    prv = lambda i: jnp.maximum(i - 1, 0)
    row = lambda w: pl.BlockSpec((MIX_TM, w), lambda i: (cur(i), 0))
    out_row = lambda w: pl.BlockSpec((MIX_TM, w), lambda i: (prv(i), 0))
    const = lambda s: pl.BlockSpec(s, lambda i: tuple(0 for _ in s))
    mod_cur = pl.BlockSpec((1, 1, D_MODEL), lambda i: (cur(i) // per_b, 0, 0))
    mod_prv = pl.BlockSpec((1, 1, D_MODEL), lambda i: (prv(i) // per_b, 0, 0))
    grp = lambda w: [pl.BlockSpec((1, dil, MIX_TM // dil, w), lambda i: (cur(i) // per_b, 0, cur(i) % per_b, 0))
                     for _win, dil in DIL_PAIRS]
    return pl.pallas_call(
        _mix_kernel,
        grid=(nb + 1,),
        in_specs=grp(ATT_WIDTH) + grp(LANES) + [
                  const((MIX_TM, MIX_TM)), const((MIX_TM, MIX_TM)), const((2 * LANES, ATT_WIDTH)),
                  row(GM_WIDTH), row(GATE_COLS), row(D_MODEL),
                  mod_cur, mod_prv, mod_prv,
                  const((GM_WIDTH, D_MODEL)), const((ATT_WIDTH, D_MODEL)), const((D_MODEL, D_MODEL)),
                  const((1, D_MODEL)), const((1, D_MODEL)),
                  const((D_MODEL, 2 * LANES)), const((1, LANES)), const((MIX_TM, MIX_TM))],
        out_specs=[out_row(D_MODEL), pl.BlockSpec((MIX_TM * ROW_SUB, LANES), lambda i: (prv(i), 0)),
                   out_row(LANES), out_row(LANES), const((8, LANES))],
        out_shape=[jax.ShapeDtypeStruct((t, D_MODEL), f32),
                   jax.ShapeDtypeStruct((t * ROW_SUB, LANES), i32),
                   jax.ShapeDtypeStruct((t, LANES), i32),
                   jax.ShapeDtypeStruct((t, LANES), f32),
                   jax.ShapeDtypeStruct((8, LANES), f32)],
        scratch_shapes=[pltpu.VMEM((1, LANES), f32), pltpu.VMEM((MIX_TM, D_MODEL), f32),
                        pltpu.VMEM((GM_WIDTH, D_MODEL), bf16), pltpu.VMEM((ATT_WIDTH, D_MODEL), bf16),
                        pltpu.VMEM((D_MODEL, D_MODEL), bf16)],
        compiler_params=_params(("arbitrary",)),
        name="mix",
    )(*os_, *ls_, perms_t[1], perms_t[2], expand, ya, gates, x2, g1, sc2, sh2, wa, wb, wo,
      ln1g, ln1b, wr_parts, br, tri)


DISP_TM = 512
MOE_TM = 512


def _dispatch_kernel(pends_ref, pcnt_ref, nused_ref, dest_ref, h2p_ref, xs_hbm, zbuf, sem, zsem):
    i = pl.program_id(0)
    ntile = xs_hbm.shape[0] // (MOE_TM * ROW_SUB)

    def zero_tile(first_row):
        return pltpu.make_async_copy(
            zbuf, xs_hbm.at[pl.ds(pl.multiple_of(first_row * ROW_SUB, MOE_TM * ROW_SUB), MOE_TM * ROW_SUB)], zsem)

    def for_each_zero_tile(fn):
        for e in range(N_EXPERTS):
            pl.when(pcnt_ref[e] > 0)(functools.partial(fn, lambda e=e: zero_tile(pends_ref[e] - MOE_TM)))
        for k in range(N_EXPERTS):
            tile = nused_ref[0] + k
            pl.when(tile < ntile)(functools.partial(fn, lambda tile=tile: zero_tile(tile * MOE_TM)))

    @pl.when(i == 0)
    def _():
        zbuf[...] = jnp.zeros_like(zbuf)
        for_each_zero_tile(lambda mk: mk().start())
        for_each_zero_tile(lambda mk: mk().wait())

    def row_copy(k, r):
        d = dest_ref[0, k, r]
        return pltpu.make_async_copy(h2p_ref.at[pl.ds(r * ROW_SUB, ROW_SUB)],
                                     xs_hbm.at[pl.ds(pl.multiple_of(d * ROW_SUB, ROW_SUB), ROW_SUB)], sem)

    for r in range(DISP_TM):
        for k in range(TOP_K):
            row_copy(k, r).start(priority=k % 2)
    for k in range(TOP_K):
        pltpu.make_async_copy(h2p_ref, xs_hbm.at[pl.ds(0, DISP_TM * ROW_SUB)], sem).wait()


def _dispatch(pends, pcounts, n_used, dest3, h2p, ntile):
    t = h2p.shape[0] // ROW_SUB
    grid_spec = pltpu.PrefetchScalarGridSpec(
        num_scalar_prefetch=3,
        grid=(t // DISP_TM,),
        in_specs=[pl.BlockSpec((1, TOP_K, DISP_TM), lambda i, *_: (i, 0, 0), memory_space=pltpu.SMEM),
                  pl.BlockSpec((DISP_TM * ROW_SUB, LANES), lambda i, *_: (i, 0))],
        out_specs=pl.BlockSpec(memory_space=pl.ANY),
        scratch_shapes=[pltpu.VMEM((MOE_TM * ROW_SUB, LANES), i32),
                        pltpu.SemaphoreType.DMA(()),
                        pltpu.SemaphoreType.DMA(())],
    )
    return pl.pallas_call(
        _dispatch_kernel,
        grid_spec=grid_spec,
        out_shape=jax.ShapeDtypeStruct((ntile * MOE_TM * ROW_SUB, LANES), i32),
        compiler_params=_params(("arbitrary",)),
        name="dispatch",
    )(pends, pcounts, n_used, dest3, h2p)


def _moe_kernel(te_ref, first_ref, last_ref, nexte_ref, wslot_ref, nused_ref,
                xs_ref, wg_hbm, wu_hbm, wd_hbm, bg_ref, bu_ref, bd_ref,
                out_ref, wbuf, wg0, wu0, wd0, wg1, wu1, wd1, sem_w):
    j = pl.program_id(0)
    wb = ((wg0, wu0, wd0), (wg1, wu1, wd1))

    def weight_copies(e, ws):
        return [pltpu.make_async_copy(w.at[e], wbuf.at[ws, k], sem_w.at[ws])
                for k, w in enumerate((wg_hbm, wu_hbm, wd_hbm))]

    def cast_weight(ws, k):
        wb[ws][k][...] = wbuf[ws, k].astype(bf16)

    @pl.when(j == 0)
    def _():
        for cp in weight_copies(te_ref[0], 0):
            cp.start()
        for cp in weight_copies(te_ref[0], 0):
            cp.wait()
        for k in range(3):
            cast_weight(0, k)

    @pl.when(first_ref[j] == 1)
    def _():
        ne = nexte_ref[j]

        @pl.when(ne >= 0)
        def _():
            for cp in weight_copies(ne, 1 - wslot_ref[j]):
                cp.start()

    def expert_mlp(s, prepare_next):
        if prepare_next:
            for cp in weight_copies(nexte_ref[j], 1 - s):
                cp.wait()
        xb = _unpack_rows(_load_packed(xs_ref, 0, MOE_TM)).astype(bf16)
        if prepare_next:
            cast_weight(1 - s, 0)
        g = jnp.dot(xb, wb[s][0][...], preferred_element_type=f32) + bg_ref[0]
        if prepare_next:
            cast_weight(1 - s, 1)
        u = jnp.dot(xb, wb[s][1][...], preferred_element_type=f32) + bu_ref[0]
        g = jnp.minimum(g, SWIGLU_LIMIT)
        u = jnp.clip(u, -SWIGLU_LIMIT, SWIGLU_LIMIT)
        act = (u + 1.0) * (g * jax.nn.sigmoid(SWIGLU_ALPHA * g))
        if prepare_next:
            cast_weight(1 - s, 2)
        y = jnp.dot(act.astype(bf16), wb[s][2][...], preferred_element_type=f32) + bd_ref[0]
        _store_packed(out_ref, _pack_rows(y), MOE_TM)

    used = j < nused_ref[0]
    for s in range(2):
        for prepare_next in (False, True):
            is_last = last_ref[j] == 1
            cond = jnp.logical_and(jnp.logical_and(used, wslot_ref[j] == s),
                                   is_last if prepare_next else jnp.logical_not(is_last))
            pl.when(cond)(functools.partial(expert_mlp, s, prepare_next))

    @pl.when(jnp.logical_not(used))
    def _():
        out_ref[...] = jnp.zeros_like(out_ref)


def _moe(tile_e, tile_first, tile_last, next_e, wslot, n_used, xs, w_gate, b_gate, w_up, b_up, w_down, b_down):
    ntile = tile_e.shape[0]
    bspec = pl.BlockSpec((1, 1, D_MODEL), lambda j, te, *_: (te[j], 0, 0))
    hbm = pl.BlockSpec(memory_space=pl.ANY)
    grid_spec = pltpu.PrefetchScalarGridSpec(
        num_scalar_prefetch=6,
        grid=(ntile,),
        in_specs=[pl.BlockSpec((MOE_TM * ROW_SUB, LANES),
                               lambda j, te, fi, la, ne, ws, nu: (jnp.minimum(j, nu[0] - 1), 0)),
                  hbm, hbm, hbm, bspec, bspec, bspec],
        out_specs=pl.BlockSpec((MOE_TM * ROW_SUB, LANES), lambda j, *_: (j, 0)),
        scratch_shapes=[pltpu.VMEM((2, 3, D_MODEL, D_MODEL), f32)]
        + [pltpu.VMEM((D_MODEL, D_MODEL), bf16)] * 6
        + [pltpu.SemaphoreType.DMA((2,))],
    )
    return pl.pallas_call(
        _moe_kernel,
        grid_spec=grid_spec,
        out_shape=jax.ShapeDtypeStruct((ntile * MOE_TM * ROW_SUB, LANES), i32),
        compiler_params=_params(("arbitrary",)),
        name="moe",
    )(tile_e, tile_first, tile_last, next_e, wslot, n_used, xs, w_gate, w_up, w_down, b_gate, b_up, b_down)


CB_TM = DISP_TM


def _combine_kernel(dcur_ref, dnxt_ref, yb_hbm, rw_ref, x1_ref, g2_ref, lng_ref, lnb_ref, out_ref,
                    ybuf0, ybuf1, sem):
    i = pl.program_id(0)
    last = pl.num_programs(0) - 1
    ybufs = (ybuf0, ybuf1)

    def row_copy(d, k, r, s):
        return pltpu.make_async_copy(
            yb_hbm.at[pl.ds(pl.multiple_of(d * ROW_SUB, ROW_SUB), ROW_SUB)],
            ybufs[s].at[pl.ds(pl.multiple_of((k * CB_TM + r) * ROW_SUB, ROW_SUB), ROW_SUB)],
            sem.at[s])

    @pl.when(i == 0)
    def _():
        for k in range(TOP_K):
            def body(r, c, k=k):
                row_copy(dcur_ref[0, k, r], k, r, 0).start()
                return c
            lax.fori_loop(0, CB_TM, body, 0, unroll=8)

    for s in range(2):
        @pl.when(i % 2 == s)
        def _(s=s):
            pltpu.make_async_copy(yb_hbm.at[pl.ds(0, TOP_K * CB_TM * ROW_SUB)], ybufs[s], sem.at[s]).wait()

            @pl.when(i < last)
            def _():
                for k in range(TOP_K):
                    for r in range(CB_TM):
                        row_copy(dnxt_ref[0, k, r], k, r, 1 - s).start(priority=r % 2)

            parts = [_unpack_rows(_load_packed(ybufs[s], k * CB_TM, CB_TM)) * rw_ref[:, k:k + 1]
                     for k in range(TOP_K)]
            y = (parts[0] + parts[1]) + (parts[2] + parts[3])
            out_ref[...] = _ln(DN_ALPHA * x1_ref[...] + g2_ref[0] * y) * lng_ref[...] + lnb_ref[...]


def _combine(dest3, yb, rw, x1, g2, ln2g, ln2b, seq):
    t = x1.shape[0]
    nb = t // CB_TM
    per_b = seq // CB_TM
    return pl.pallas_call(
        _combine_kernel,
        grid=(nb,),
        in_specs=[pl.BlockSpec((1, TOP_K, CB_TM), lambda i: (i, 0, 0), memory_space=pltpu.SMEM),
                  pl.BlockSpec((1, TOP_K, CB_TM), lambda i: (jnp.minimum(i + 1, nb - 1), 0, 0),
                               memory_space=pltpu.SMEM),
                  pl.BlockSpec(memory_space=pl.ANY),
                  pl.BlockSpec((CB_TM, LANES), lambda i: (i, 0)),
                  pl.BlockSpec((CB_TM, D_MODEL), lambda i: (i, 0)),
                  pl.BlockSpec((1, 1, D_MODEL), lambda i: (i // per_b, 0, 0)),
                  pl.BlockSpec((1, D_MODEL), lambda i: (0, 0)),
                  pl.BlockSpec((1, D_MODEL), lambda i: (0, 0))],
        out_specs=pl.BlockSpec((CB_TM, D_MODEL), lambda i: (i, 0)),
        out_shape=jax.ShapeDtypeStruct((t, D_MODEL), f32),
        scratch_shapes=[pltpu.VMEM((TOP_K * CB_TM * ROW_SUB, LANES), i32),
                        pltpu.VMEM((TOP_K * CB_TM * ROW_SUB, LANES), i32),
                        pltpu.SemaphoreType.DMA((2,))],
        compiler_params=_params(("arbitrary",)),
        name="combine",
    )(dest3, dest3, yb, rw, x1, g2, ln2g, ln2b)


def _t5_bucket(dist):
    d = dist.astype(f32)
    large = REL_MAX_EXACT + jnp.log(jnp.maximum(d, float(REL_MAX_EXACT)) / REL_MAX_EXACT) / math.log(
        REL_MAX_DIST / REL_MAX_EXACT) * (REL_BUCKETS - REL_MAX_EXACT)
    large = jnp.minimum(large.astype(i32), REL_BUCKETS - 1)
    return jnp.where(dist < REL_MAX_EXACT, dist, large)


def _bias_indices():
    qi = jnp.arange(ATT_BLOCK)[:, None]
    ki = jnp.arange(2 * ATT_BLOCK)[None, :]
    didx = qi + ATT_BLOCK - ki
    buckets, bands = [], []
    for win, dil in DIL_PAIRS:
        buckets.append(_t5_bucket(jnp.clip(didx, 0, None) * dil))
        bands.append(((didx >= 0) & (didx <= win // dil)).astype(i32))
    return jnp.stack(buckets).astype(i32), jnp.stack(bands)


def _residue_perm(tm, dil):
    n = tm // dil
    dst = np.arange(tm)
    src = (dst % n) * dil + dst // n
    return src[:, None] == np.arange(tm)[None, :]


def kernel(x, c, w_ada, b_ada, w_in, gm_ln_g, gm_ln_b, gm_w_s, gm_b_s, w_branch_a, w_branch_b, w_out,
           rel_bias, ln1_g, ln1_b, w_router, b_router, w_gate, b_gate, w_up, b_up, w_down, b_down,
           ln2_g, ln2_b):
    batch, seq, _ = x.shape
    t = batch * seq
    l = 0
    x2 = x.reshape(t, D_MODEL)

    c8 = jnp.pad(c, ((0, 8 - batch), (0, 0)))
    mod = _adaln(c8, w_ada[l], b_ada[l][None, :])[:batch]
    sh1, sc1, g1, sh2, sc2, g2 = [m[:, None, :] for m in jnp.split(mod, 6, axis=-1)]

    perms = [jnp.asarray(_residue_perm(IN_TM, dil), bf16) for _win, dil in DIL_PAIRS]
    bs_full = jnp.repeat(gm_b_s[l].T, GM_WIDTH // GM_GROUPS, axis=1)
    ya, gates, *qkvs = _inproj(x2, sc1, sh1, w_in[l].astype(bf16), perms,
                               gm_ln_g[l][None, :], gm_ln_b[l][None, :], gm_w_s[l], bs_full, batch, seq)

    bucket, band = _bias_indices()
    bias = _relbias(rel_bias, bucket, band)
    os_, ls_ = [], []
    for g, (_win, dil) in enumerate(DIL_PAIRS):
        o, lse = _attn_group(qkvs[g], bias, g, dil, batch, seq)
        os_.append(o)
        ls_.append(lse)

    wr = jnp.pad(w_router[l], ((0, 0), (0, LANES - N_EXPERTS)))
    wr_hi = wr.astype(bf16)
    wr_parts = jnp.concatenate([wr_hi, (wr - wr_hi.astype(f32)).astype(bf16)], axis=1)
    br = jnp.pad(b_router[l], (0, LANES - N_EXPERTS))[None, :]
    tri = jnp.asarray(np.arange(MIX_TM)[None, :] < np.arange(MIX_TM)[:, None], bf16)
    perms_t = [jnp.asarray(_residue_perm(MIX_TM, dil).T, bf16) for _win, dil in DIL_PAIRS]
    expand = np.arange(LANES)[:, None] == np.arange(ATT_WIDTH)[None, :] // HEAD_DIM
    expand = jnp.asarray(np.concatenate([expand, expand], axis=0), bf16)
    x1, h2, route, rw, cnt = _mix(
        os_, ls_, perms_t, expand, ya, gates, x2, g1, sc2, sh2,
        w_branch_a[l], w_branch_b[l], w_out[l],
        ln1_g[l][None, :], ln1_b[l][None, :], wr_parts, br, tri, seq)

    top_e = route[:, :TOP_K]
    rank = route[:, TOP_K:2 * TOP_K]
    counts = cnt[0, :N_EXPERTS].astype(i32)
    pcounts = (counts + MOE_TM - 1) // MOE_TM * MOE_TM
    experts = jnp.arange(N_EXPERTS, dtype=i32)
    upto = experts[None, :] <= experts[:, None]
    pends = jnp.sum(jnp.where(upto, pcounts[None, :], 0), axis=1)
    pstarts = pends - pcounts
    dest = jnp.sum(jnp.where(top_e[:, :, None] == experts, pstarts, 0), axis=-1) + rank
    ntile = t * TOP_K // MOE_TM + N_EXPERTS
    n_used = (pends[-1] // MOE_TM).reshape(1)
    tile_idx = jnp.minimum(jnp.arange(ntile, dtype=i32), n_used - 1)
    tile_e = jnp.sum((pends[None, :] <= (tile_idx * MOE_TM)[:, None]).astype(i32), axis=1)
    tile_first = jnp.concatenate([jnp.ones((1,), i32), (tile_e[1:] != tile_e[:-1]).astype(i32)])
    nonempty = counts > 0
    later = jnp.logical_and(experts[None, :] > experts[:, None], nonempty[None, :])
    next_nonempty = jnp.min(jnp.where(later, experts[None, :], N_EXPERTS), axis=1)
    next_nonempty = jnp.where(next_nonempty >= N_EXPERTS, -1, next_nonempty)
    expert_slot = (jnp.sum(jnp.logical_and(upto, nonempty[None, :]).astype(i32), axis=1) - 1) % 2
    of_tile = tile_e[:, None] == experts[None, :]
    tile_next = jnp.sum(jnp.where(of_tile, next_nonempty[None, :], 0), axis=1)
    tile_slot = jnp.sum(jnp.where(of_tile, expert_slot[None, :], 0), axis=1)
    tile_last = jnp.concatenate([(tile_e[1:] != tile_e[:-1]).astype(i32), jnp.zeros((1,), i32)])
    tile_last = jnp.where(tile_next >= 0, tile_last, 0)

    dest3 = dest.reshape(t // DISP_TM, DISP_TM, TOP_K).transpose(0, 2, 1)
    xs = _dispatch(pends, pcounts, n_used, dest3, h2, ntile)
    yb = _moe(tile_e, tile_first, tile_last, tile_next, tile_slot, n_used, xs,
              w_gate[l], b_gate[l][:, None, :], w_up[l], b_up[l][:, None, :],
              w_down[l], b_down[l][:, None, :])
    out = _combine(dest3, yb, rw, x1, g2, ln2_g[l][None, :], ln2_b[l][None, :], seq)
    return out.reshape(batch, seq, D_MODEL)
```

```python
import functools
import math

import numpy as np
import jax
import jax.numpy as jnp
from jax import lax
from jax.experimental import pallas as pl
from jax.experimental.pallas import tpu as pltpu

f32 = jnp.float32
bf16 = jnp.bfloat16
i32 = jnp.int32

D_MODEL = 1024
GM_WIDTH = 512
GM_GROUPS = 8
GM_CHUNK = 128
DIL_PAIRS = ((128, 1), (512, 4), (2048, 16))
N_DIL = 3
HEADS_PER_GROUP = 8
HEAD_DIM = 64
ATT_WIDTH = 512
ATT_BLOCK = 128
NEG_INF = -1e30
REL_BUCKETS = 32
REL_MAX_EXACT = 16
REL_MAX_DIST = 2048
N_EXPERTS = 32
TOP_K = 4
SWIGLU_LIMIT = 7.0
SWIGLU_ALPHA = 1.702
DEPTH = 1
DN_ALPHA = (2 * DEPTH) ** 0.25
LN_EPS = 1e-5
UV_COLS = 2 * GM_WIDTH
QKV_COLS = N_DIL * 3 * ATT_WIDTH
GATE_COLS = 2 * D_MODEL
IN_COLS = UV_COLS + QKV_COLS + GATE_COLS

LANES = 128
VMEM_LIMIT = 56 * 1024 * 1024


def _ln(x):
    mu = jnp.mean(x, axis=-1, keepdims=True)
    xc = x - mu
    var = jnp.mean(xc * xc, axis=-1, keepdims=True)
    return xc * lax.rsqrt(var + LN_EPS)


def _params(sem, vmem=VMEM_LIMIT):
    return pltpu.CompilerParams(dimension_semantics=sem, vmem_limit_bytes=vmem)


def _adaln_kernel(c_ref, w_ref, b_ref, o_ref):
    c = c_ref[...]
    s = c * jax.nn.sigmoid(c)
    o_ref[...] = jnp.dot(s, w_ref[...], preferred_element_type=f32,
                         precision=lax.Precision.HIGHEST) + b_ref[...]


def _adaln(c8, w_ada, b_ada):
    n = w_ada.shape[1] // D_MODEL
    return pl.pallas_call(
        _adaln_kernel,
        grid=(n,),
        in_specs=[pl.BlockSpec((8, D_MODEL), lambda j: (0, 0)),
                  pl.BlockSpec((D_MODEL, D_MODEL), lambda j: (0, j)),
                  pl.BlockSpec((1, D_MODEL), lambda j: (0, j))],
        out_specs=pl.BlockSpec((8, D_MODEL), lambda j: (0, j)),
        out_shape=jax.ShapeDtypeStruct((8, w_ada.shape[1]), f32),
        compiler_params=_params(("arbitrary",)),
        name="adaln",
    )(c8, w_ada, b_ada)


IN_TM = 256
IN_CW = 512
GRP_COLS = 3 * ATT_WIDTH


def _inproj_kernel(x_ref, sc_ref, sh_ref, w_ref, p1_ref, p2_ref, gm_g_ref, gm_b_ref, ws_ref, bs_ref,
                   ya_ref, gt_ref, qkv0_ref, qkv1_ref, qkv2_ref):
    xn = _ln(x_ref[...])
    h = (xn * (1.0 + sc_ref[0]) + sh_ref[0]).astype(bf16)
    hp = [h,
          jnp.dot(p1_ref[...], h, preferred_element_type=f32).astype(bf16),
          jnp.dot(p2_ref[...], h, preferred_element_type=f32).astype(bf16)]
    u_act, v_act = [jax.nn.gelu(jnp.dot(h, w_ref[:, c0:c0 + GM_WIDTH], preferred_element_type=f32))
                    for c0 in range(0, UV_COLS, GM_WIDTH)]
    row = lax.broadcasted_iota(i32, (GM_CHUNK, GM_CHUNK), 0)
    col = lax.broadcasted_iota(i32, (GM_CHUNK, GM_CHUNK), 1)
    first_half = lax.broadcasted_iota(i32, (GM_CHUNK, LANES), 1) < (GM_WIDTH // GM_GROUPS)
    ws = [jnp.where(col <= row, ws_ref[g], 0.0).astype(bf16) for g in range(GM_GROUPS)]

    def gate_chunk(r0):
        vn = (_ln(v_act[r0:r0 + GM_CHUNK, :]) * gm_g_ref[...] + gm_b_ref[...]).astype(bf16)
        for j in range(GM_WIDTH // LANES):
            sl = slice(j * LANES, (j + 1) * LANES)
            s_lo = jnp.dot(ws[2 * j], vn[:, sl], preferred_element_type=f32)
            s_hi = jnp.dot(ws[2 * j + 1], vn[:, sl], preferred_element_type=f32)
            s = jnp.where(first_half, s_lo, s_hi) + bs_ref[:, sl]
            ya_ref[r0:r0 + GM_CHUNK, sl] = (u_act[r0:r0 + GM_CHUNK, sl] * s).astype(bf16)

    pending = list(range(0, IN_TM, GM_CHUNK))
    for g, (qref, (_win, dil)) in enumerate(zip((qkv0_ref, qkv1_ref, qkv2_ref), DIL_PAIRS)):
        n = IN_TM // dil
        for q0 in range(0, GRP_COLS, IN_CW):
            c0 = UV_COLS + g * GRP_COLS + q0
            acc = jnp.dot(hp[g], w_ref[:, c0:c0 + IN_CW], preferred_element_type=f32).astype(bf16)
            for rho in range(dil):
                qref[0, rho, :, q0:q0 + IN_CW] = acc[rho * n:(rho + 1) * n, :]
        if pending:
            gate_chunk(pending.pop(0))
    while pending:
        gate_chunk(pending.pop(0))
    for g0 in range(0, GATE_COLS, IN_CW):
        c0 = UV_COLS + QKV_COLS + g0
        acc = jnp.dot(h, w_ref[:, c0:c0 + IN_CW], preferred_element_type=f32)
        gt_ref[:, g0:g0 + IN_CW] = jax.nn.sigmoid(acc).astype(bf16)


def _inproj(x2, sc1, sh1, w_in_bf, perms, gm_g, gm_b, w_s, bs_full, batch, seq):
    t = x2.shape[0]
    per_b = seq // IN_TM
    qkv_specs, qkv_shapes = [], []
    for _win, dil in DIL_PAIRS:
        n = IN_TM // dil
        qkv_specs.append(pl.BlockSpec((1, dil, n, GRP_COLS), lambda i: (i // per_b, 0, i % per_b, 0)))
        qkv_shapes.append(jax.ShapeDtypeStruct((batch, dil, seq // dil, GRP_COLS), bf16))
    return pl.pallas_call(
        _inproj_kernel,
        grid=(t // IN_TM,),
        in_specs=[pl.BlockSpec((IN_TM, D_MODEL), lambda i: (i, 0)),
                  pl.BlockSpec((1, 1, D_MODEL), lambda i: (i // per_b, 0, 0)),
                  pl.BlockSpec((1, 1, D_MODEL), lambda i: (i // per_b, 0, 0)),
                  pl.BlockSpec((D_MODEL, IN_COLS), lambda i: (0, 0)),
                  pl.BlockSpec((IN_TM, IN_TM), lambda i: (0, 0)),
                  pl.BlockSpec((IN_TM, IN_TM), lambda i: (0, 0)),
                  pl.BlockSpec((1, GM_WIDTH), lambda i: (0, 0)),
                  pl.BlockSpec((1, GM_WIDTH), lambda i: (0, 0)),
                  pl.BlockSpec((GM_GROUPS, GM_CHUNK, GM_CHUNK), lambda i: (0, 0, 0)),
                  pl.BlockSpec((GM_CHUNK, GM_WIDTH), lambda i: (0, 0))],
        out_specs=[pl.BlockSpec((IN_TM, GM_WIDTH), lambda i: (i, 0)),
                   pl.BlockSpec((IN_TM, GATE_COLS), lambda i: (i, 0))] + qkv_specs,
        out_shape=[jax.ShapeDtypeStruct((t, GM_WIDTH), bf16),
                   jax.ShapeDtypeStruct((t, GATE_COLS), bf16)] + qkv_shapes,
        compiler_params=_params(("arbitrary",)),
        name="inproj",
    )(x2, sc1, sh1, w_in_bf, perms[1], perms[2], gm_g, gm_b, w_s, bs_full)


def _relbias_kernel(tab_ref, bucket_ref, band_ref, out_ref):
    g = pl.program_id(0)
    bk = bucket_ref[0]
    band = band_ref[0] > 0
    for h in range(HEADS_PER_GROUP):
        acc = jnp.zeros((ATT_BLOCK, 2 * ATT_BLOCK), f32)
        for b in range(REL_BUCKETS):
            acc = jnp.where(bk == b, tab_ref[b, g * HEADS_PER_GROUP + h], acc)
        out_ref[0, h] = jnp.where(band, acc, NEG_INF)


def _relbias(rel_bias, bucket, band):
    return pl.pallas_call(
        _relbias_kernel,
        grid=(N_DIL,),
        in_specs=[pl.BlockSpec(memory_space=pltpu.SMEM),
                  pl.BlockSpec((1, ATT_BLOCK, 2 * ATT_BLOCK), lambda g: (g, 0, 0)),
                  pl.BlockSpec((1, ATT_BLOCK, 2 * ATT_BLOCK), lambda g: (g, 0, 0))],
        out_specs=pl.BlockSpec((1, HEADS_PER_GROUP, ATT_BLOCK, 2 * ATT_BLOCK),
                               lambda g: (g, 0, 0, 0)),
        out_shape=jax.ShapeDtypeStruct((N_DIL, HEADS_PER_GROUP, ATT_BLOCK, 2 * ATT_BLOCK), f32),
        compiler_params=_params(("arbitrary",)),
        name="relbias",
    )(rel_bias, bucket, band)


ATT_MAX_STEP_BLOCKS = 8


def _attn_kernel(nres, nblk, q_ref, kp_ref, kc_ref, vp_ref, vc_ref, bias_ref, o_ref, lse_ref):
    first = pl.program_id(2) == 0
    lane = lax.broadcasted_iota(i32, (ATT_BLOCK, LANES), 1)
    lo_half = lane < HEAD_DIM
    prev_cols = lax.broadcasted_iota(i32, (ATT_BLOCK, 2 * ATT_BLOCK), 1) < ATT_BLOCK
    no_prev = jnp.logical_and(first, prev_cols)
    nt = (((1,), (1,)), ((), ()))
    ones = jnp.ones((2 * ATT_BLOCK, LANES), bf16)
    n_slab = ATT_WIDTH // LANES
    blocks = [(res, i) for res in range(nres) for i in range(nblk)]
    logits, v_ext = [], []
    for res, i in blocks:
        cur = slice(i * ATT_BLOCK, (i + 1) * ATT_BLOCK)
        prv = slice((i - 1) * ATT_BLOCK, i * ATT_BLOCK)
        for j in range(n_slab):
            sl = slice(j * LANES, (j + 1) * LANES)
            q = q_ref[0, res, cur, sl] * (HEAD_DIM ** -0.5)
            k_prev = kp_ref[0, res, :, sl] if i == 0 else kc_ref[0, res, prv, sl]
            v_prev = vp_ref[0, res, :, sl] if i == 0 else vc_ref[0, res, prv, sl]
            k_cat = jnp.concatenate([k_prev, kc_ref[0, res, cur, sl]], axis=0)
            v_cat = jnp.concatenate([v_prev, vc_ref[0, res, cur, sl]], axis=0)
            v_ext.append(jnp.concatenate([v_cat, ones], axis=1))
            for hh in range(2):
                qm = jnp.where(lo_half if hh == 0 else jnp.logical_not(lo_half), q, 0.0).astype(bf16)
                lg_h = lax.dot_general(qm, k_cat, nt, preferred_element_type=f32) + bias_ref[0, 2 * j + hh]
                logits.append(jnp.where(no_prev, NEG_INF, lg_h) if i == 0 else lg_h)
    rows_per_block = HEADS_PER_GROUP * ATT_BLOCK
    lg = jnp.concatenate(logits, axis=0)
    m = jnp.max(lg, axis=-1, keepdims=True)
    p = jnp.exp(lg - m).astype(bf16)
    for b, (res, i) in enumerate(blocks):
        cur = slice(i * ATT_BLOCK, (i + 1) * ATT_BLOCK)
        lse_tile = jnp.zeros((ATT_BLOCK, LANES), f32)
        for j in range(n_slab):
            outs = []
            for hh in range(2):
                h = 2 * j + hh
                r0 = b * rows_per_block + h * ATT_BLOCK
                r = jnp.dot(p[r0:r0 + ATT_BLOCK], v_ext[b * n_slab + j], preferred_element_type=f32)
                den = r[:, LANES:]
                outs.append(r[:, :LANES] * (1.0 / den))
                lse_h = m[r0:r0 + ATT_BLOCK] + jnp.log(den)
                lse_tile = jnp.where(lane == h, lse_h, lse_tile)
            o_ref[0, res, cur, j * LANES:(j + 1) * LANES] = jnp.where(lo_half, outs[0], outs[1]).astype(bf16)
        lse_ref[0, res, cur, :] = lse_tile


def _attn_group(qkv_g, bias, g, dil, batch, seq):
    l = seq // dil
    nblk = min(ATT_MAX_STEP_BLOCKS, l // ATT_BLOCK)
    nres = min(dil, ATT_MAX_STEP_BLOCKS // nblk)
    tm = nblk * ATT_BLOCK
    nsteps = l // tm

    def cur(cb):
        return pl.BlockSpec((1, nres, tm, ATT_WIDTH), lambda b, r, n: (b, r, n, cb))

    def prev(cb):
        return pl.BlockSpec((1, nres, ATT_BLOCK, ATT_WIDTH),
                            lambda b, r, n: (b, r, jnp.maximum(n * nblk - 1, 0), cb))

    return pl.pallas_call(
        functools.partial(_attn_kernel, nres, nblk),
        grid=(batch, dil // nres, nsteps),
        in_specs=[cur(0), prev(1), cur(1), prev(2), cur(2),
                  pl.BlockSpec((1, HEADS_PER_GROUP, ATT_BLOCK, 2 * ATT_BLOCK),
                               lambda b, r, n: (g, 0, 0, 0))],
        out_specs=[pl.BlockSpec((1, nres, tm, ATT_WIDTH), lambda b, r, n: (b, r, n, 0)),
                   pl.BlockSpec((1, nres, tm, LANES), lambda b, r, n: (b, r, n, 0))],
        out_shape=[jax.ShapeDtypeStruct((batch, dil, l, ATT_WIDTH), bf16),
                   jax.ShapeDtypeStruct((batch, dil, l, LANES), f32)],
        compiler_params=_params(("arbitrary", "arbitrary", "arbitrary")),
        name=f"attn_g{g}",
    )(qkv_g, qkv_g, qkv_g, qkv_g, qkv_g, bias)


ROW_WORDS = D_MODEL // 2
ROW_SUB = ROW_WORDS // LANES
HI_MASK = -65536


def _pack_rows(x):
    bits = lax.bitcast_convert_type(x.astype(bf16).astype(f32), i32)
    return lax.shift_right_logical(bits[:, :ROW_WORDS], 16) | (bits[:, ROW_WORDS:] & HI_MASK)


def _unpack_rows(words):
    lo = lax.bitcast_convert_type(lax.shift_left(words, 16), f32)
    hi = lax.bitcast_convert_type(words & HI_MASK, f32)
    return jnp.concatenate([lo, hi], axis=1)


def _store_packed(ref, words, n, first_row=0):
    for r in range(ROW_SUB):
        ref[pl.ds(first_row * ROW_SUB + r, n, stride=ROW_SUB), :] = words[:, r * LANES:(r + 1) * LANES]


def _load_packed(ref, first_row, n):
    return jnp.concatenate([ref[pl.ds(first_row * ROW_SUB + r, n, stride=ROW_SUB), :] for r in range(ROW_SUB)],
                           axis=1)


MIX_TM = 256
MIX_SUB = 128


def _split_bf16(x, parts):
    out = []
    for _ in range(parts):
        hi = x.astype(bf16)
        out.append(hi)
        x = x - hi.astype(f32)
    return out


def _mix_kernel(o0_ref, o1_ref, o2_ref, l0_ref, l1_ref, l2_ref, pt1_ref, pt2_ref, ex_ref,
                ya_ref, gt_ref, x_ref,
                g1_ref, sc2_ref, sh2_ref, wa32_ref, wb32_ref, wo32_ref, ln1g_ref, ln1b_ref,
                wrc_ref, br_ref, tri_ref,
                x1_ref, h2_ref, route_ref, rw_ref, cnt_ref, run_ref, xr_ref, wa_ref, wb_ref, wo_ref):
    step = pl.program_id(0)

    @pl.when(step == 0)
    def _():
        run_ref[...] = jnp.zeros_like(run_ref)
        xr_ref[...] = jnp.zeros_like(xr_ref)
        wa_ref[...] = wa32_ref[...].astype(bf16)
        wb_ref[...] = wb32_ref[...].astype(bf16)
        wo_ref[...] = wo32_ref[...].astype(bf16)

    def back_norms(r0):
        rows = slice(r0, r0 + MIX_SUB)
        x1 = _ln(xr_ref[rows, :]) * ln1g_ref[...] + ln1b_ref[...]
        x1_ref[rows, :] = x1
        h2 = _ln(x1) * (1.0 + sc2_ref[0]) + sh2_ref[0]
        _store_packed(h2_ref, _pack_rows(h2), MIX_SUB, r0)
        return h2

    def back_router(h2):
        h_hi, h_lo = _split_bf16(h2, 2)
        hi_both = jnp.dot(h_hi, wrc_ref[...], preferred_element_type=f32)
        return (hi_both[:, :LANES]
                + (hi_both[:, LANES:] + jnp.dot(h_lo, wrc_ref[:, :LANES], preferred_element_type=f32))
                ) + br_ref[...]

    def front_merge(r0):
        rows = slice(r0, r0 + MIX_SUB)
        os_, ls_ = [o0_ref[0, 0, rows, :].astype(f32)], [l0_ref[0, 0, rows, :]]
        for o_ref, l_ref, pt_ref in ((o1_ref, l1_ref, pt1_ref), (o2_ref, l2_ref, pt2_ref)):
            pt = pt_ref[rows, :]
            os_.append(jnp.dot(pt, o_ref[0].reshape(MIX_TM, ATT_WIDTH), preferred_element_type=f32))
            parts = [jnp.dot(pt, part, preferred_element_type=f32)
                     for part in _split_bf16(l_ref[0].reshape(MIX_TM, LANES), 3)]
            ls_.append((parts[0] + parts[1]) + parts[2])
        lm = jnp.maximum(jnp.maximum(ls_[0], ls_[1]), ls_[2])
        es = [jnp.exp(lse - lm) for lse in ls_]
        inv = 1.0 / (es[0] + es[1] + es[2])
        yb = jnp.zeros((MIX_SUB, ATT_WIDTH), f32)
        for e, o in zip(es, os_):
            w_parts = jnp.concatenate(_split_bf16(e * inv, 2), axis=1)
            yb = yb + jnp.dot(w_parts, ex_ref[...], preferred_element_type=f32) * o
        return yb.astype(bf16)

    def front_branches(r0, yb):
        rows = slice(r0, r0 + MIX_SUB)
        a = jnp.dot(ya_ref[rows, :], wa_ref[...], preferred_element_type=f32)
        b = jnp.dot(yb, wb_ref[...], preferred_element_type=f32)
        return (gt_ref[rows, :D_MODEL].astype(f32) * a + gt_ref[rows, D_MODEL:].astype(f32) * b).astype(bf16)

    def front_out(r0, merged):
        rows = slice(r0, r0 + MIX_SUB)
        mix = jnp.dot(merged, wo_ref[...], preferred_element_type=f32)
        xr_ref[rows, :] = DN_ALPHA * x_ref[rows, :] + g1_ref[0] * mix

    subs = list(range(0, MIX_TM, MIX_SUB))
    logit_parts = []
    for r0 in subs:
        h2 = back_norms(r0)
        yb = front_merge(r0)
        logit_parts.append(back_router(h2))
        merged = front_branches(r0, yb)
        front_out(r0, merged)
    logits = jnp.concatenate(logit_parts, axis=0)
    lane = lax.broadcasted_iota(i32, (MIX_TM, LANES), 1)
    logits = jnp.where(lane < N_EXPERTS, logits, -jnp.inf)
    lane_f = lane.astype(f32)
    vals, idxs = [], []
    for _k in range(TOP_K):
        m = jnp.max(logits, axis=-1, keepdims=True)
        vals.append(m)
        idxs.append(jnp.min(jnp.where(logits == m, lane_f, float(LANES)), axis=-1, keepdims=True).astype(i32))
        logits = jnp.where(lane == idxs[-1], -jnp.inf, logits)
    exps = [jnp.exp(v - vals[0]) for v in vals]
    den = exps[0] + exps[1] + exps[2] + exps[3]
    wts = [e / den for e in exps]
    hits = [lane == idx for idx in idxs]
    counted = jnp.where(step > 0, 1.0, 0.0)
    onehot = jnp.zeros((MIX_TM, LANES), f32)
    for hit in hits:
        onehot = onehot + jnp.where(hit, counted, 0.0)
    prefix = jnp.dot(tri_ref[...], onehot.astype(bf16), preferred_element_type=f32) + run_ref[...]
    route = jnp.zeros((MIX_TM, LANES), i32)
    rw = jnp.zeros((MIX_TM, LANES), f32)
    for k in range(TOP_K):
        rank = jnp.sum(jnp.where(hits[k], prefix, 0.0), axis=-1, keepdims=True).astype(i32)
        route = jnp.where(lane == k, idxs[k], route)
        route = jnp.where(lane == TOP_K + k, rank, route)
        rw = jnp.where(lane == k, wts[k], rw)
    route_ref[...] = route
    rw_ref[...] = rw
    run = run_ref[...] + jnp.sum(onehot, axis=0, keepdims=True)
    run_ref[...] = run
    cnt_ref[...] = jnp.broadcast_to(run, cnt_ref.shape)


def _mix(os_, ls_, perms_t, expand, ya, gates, x2, g1, sc2, sh2, wa, wb, wo, ln1g, ln1b, wr_parts, br, tri, seq):
    t = x2.shape[0]
    nb = t // MIX_TM
    per_b = seq // MIX_TM
    cur = lambda i: jnp.minimum(i, nb - 1)
    prv = lambda i: jnp.maximum(i - 1, 0)
    row = lambda w: pl.BlockSpec((MIX_TM, w), lambda i: (cur(i), 0))
    out_row = lambda w: pl.BlockSpec((MIX_TM, w), lambda i: (prv(i), 0))
    const = lambda s: pl.BlockSpec(s, lambda i: tuple(0 for _ in s))
    mod_cur = pl.BlockSpec((1, 1, D_MODEL), lambda i: (cur(i) // per_b, 0, 0))
    mod_prv = pl.BlockSpec((1, 1, D_MODEL), lambda i: (prv(i) // per_b, 0, 0))
    grp = lambda w: [pl.BlockSpec((1, dil, MIX_TM // dil, w), lambda i: (cur(i) // per_b, 0, cur(i) % per_b, 0))
                     for _win, dil in DIL_PAIRS]
    return pl.pallas_call(
        _mix_kernel,
        grid=(nb + 1,),
        in_specs=grp(ATT_WIDTH) + grp(LANES) + [
                  const((MIX_TM, MIX_TM)), const((MIX_TM, MIX_TM)), const((2 * LANES, ATT_WIDTH)),
                  row(GM_WIDTH), row(GATE_COLS), row(D_MODEL),
                  mod_cur, mod_prv, mod_prv,
                  const((GM_WIDTH, D_MODEL)), const((ATT_WIDTH, D_MODEL)), const((D_MODEL, D_MODEL)),
                  const((1, D_MODEL)), const((1, D_MODEL)),
                  const((D_MODEL, 2 * LANES)), const((1, LANES)), const((MIX_TM, MIX_TM))],
        out_specs=[out_row(D_MODEL), pl.BlockSpec((MIX_TM * ROW_SUB, LANES), lambda i: (prv(i), 0)),
                   out_row(LANES), out_row(LANES), const((8, LANES))],
        out_shape=[jax.ShapeDtypeStruct((t, D_MODEL), f32),
                   jax.ShapeDtypeStruct((t * ROW_SUB, LANES), i32),
                   jax.ShapeDtypeStruct((t, LANES), i32),
                   jax.ShapeDtypeStruct((t, LANES), f32),
                   jax.ShapeDtypeStruct((8, LANES), f32)],
        scratch_shapes=[pltpu.VMEM((1, LANES), f32), pltpu.VMEM((MIX_TM, D_MODEL), f32),
                        pltpu.VMEM((GM_WIDTH, D_MODEL), bf16), pltpu.VMEM((ATT_WIDTH, D_MODEL), bf16),
                        pltpu.VMEM((D_MODEL, D_MODEL), bf16)],
        compiler_params=_params(("arbitrary",)),
        name="mix",
    )(*os_, *ls_, perms_t[1], perms_t[2], expand, ya, gates, x2, g1, sc2, sh2, wa, wb, wo,
      ln1g, ln1b, wr_parts, br, tri)


DISP_TM = 512
MOE_TM = 512


def _dispatch_kernel(pends_ref, pcnt_ref, nused_ref, dest_ref, h2p_ref, xs_hbm, zbuf, sem, zsem):
    i = pl.program_id(0)
    ntile = xs_hbm.shape[0] // (MOE_TM * ROW_SUB)

    def zero_tile(first_row):
        return pltpu.make_async_copy(
            zbuf, xs_hbm.at[pl.ds(pl.multiple_of(first_row * ROW_SUB, MOE_TM * ROW_SUB), MOE_TM * ROW_SUB)], zsem)

    def for_each_zero_tile(fn):
        for e in range(N_EXPERTS):
            pl.when(pcnt_ref[e] > 0)(functools.partial(fn, lambda e=e: zero_tile(pends_ref[e] - MOE_TM)))
        for k in range(N_EXPERTS):
            tile = nused_ref[0] + k
            pl.when(tile < ntile)(functools.partial(fn, lambda tile=tile: zero_tile(tile * MOE_TM)))

    @pl.when(i == 0)
    def _():
        zbuf[...] = jnp.zeros_like(zbuf)
        for_each_zero_tile(lambda mk: mk().start())
        for_each_zero_tile(lambda mk: mk().wait())

    def row_copy(k, r):
        d = dest_ref[0, k, r]
        return pltpu.make_async_copy(h2p_ref.at[pl.ds(r * ROW_SUB, ROW_SUB)],
                                     xs_hbm.at[pl.ds(pl.multiple_of(d * ROW_SUB, ROW_SUB), ROW_SUB)], sem)

    for r in range(DISP_TM):
        for k in range(TOP_K):
            row_copy(k, r).start(priority=k % 2)
    for k in range(TOP_K):
        pltpu.make_async_copy(h2p_ref, xs_hbm.at[pl.ds(0, DISP_TM * ROW_SUB)], sem).wait()


def _dispatch(pends, pcounts, n_used, dest3, h2p, ntile):
    t = h2p.shape[0] // ROW_SUB
    grid_spec = pltpu.PrefetchScalarGridSpec(
        num_scalar_prefetch=3,
        grid=(t // DISP_TM,),
        in_specs=[pl.BlockSpec((1, TOP_K, DISP_TM), lambda i, *_: (i, 0, 0), memory_space=pltpu.SMEM),
                  pl.BlockSpec((DISP_TM * ROW_SUB, LANES), lambda i, *_: (i, 0))],
        out_specs=pl.BlockSpec(memory_space=pl.ANY),
        scratch_shapes=[pltpu.VMEM((MOE_TM * ROW_SUB, LANES), i32),
                        pltpu.SemaphoreType.DMA(()),
                        pltpu.SemaphoreType.DMA(())],
    )
    return pl.pallas_call(
        _dispatch_kernel,
        grid_spec=grid_spec,
        out_shape=jax.ShapeDtypeStruct((ntile * MOE_TM * ROW_SUB, LANES), i32),
        compiler_params=_params(("arbitrary",)),
        name="dispatch",
    )(pends, pcounts, n_used, dest3, h2p)


def _moe_kernel(te_ref, first_ref, last_ref, nexte_ref, wslot_ref, nused_ref,
                xs_ref, wg_hbm, wu_hbm, wd_hbm, bg_ref, bu_ref, bd_ref,
                out_ref, wbuf, wg0, wu0, wd0, wg1, wu1, wd1, sem_w):
    j = pl.program_id(0)
    wb = ((wg0, wu0, wd0), (wg1, wu1, wd1))

    def weight_copies(e, ws):
        return [pltpu.make_async_copy(w.at[e], wbuf.at[ws, k], sem_w.at[ws])
                for k, w in enumerate((wg_hbm, wu_hbm, wd_hbm))]

    def cast_weight(ws, k):
        wb[ws][k][...] = wbuf[ws, k].astype(bf16)

    @pl.when(j == 0)
    def _():
        for cp in weight_copies(te_ref[0], 0):
            cp.start()
        for cp in weight_copies(te_ref[0], 0):
            cp.wait()
        for k in range(3):
            cast_weight(0, k)

    @pl.when(first_ref[j] == 1)
    def _():
        ne = nexte_ref[j]

        @pl.when(ne >= 0)
        def _():
            for cp in weight_copies(ne, 1 - wslot_ref[j]):
                cp.start()

    def expert_mlp(s, prepare_next):
        if prepare_next:
            for cp in weight_copies(nexte_ref[j], 1 - s):
                cp.wait()
        xb = _unpack_rows(_load_packed(xs_ref, 0, MOE_TM)).astype(bf16)
        if prepare_next:
            cast_weight(1 - s, 0)
        g = jnp.dot(xb, wb[s][0][...], preferred_element_type=f32) + bg_ref[0]
        if prepare_next:
            cast_weight(1 - s, 1)
        u = jnp.dot(xb, wb[s][1][...], preferred_element_type=f32) + bu_ref[0]
        g = jnp.minimum(g, SWIGLU_LIMIT)
        u = jnp.clip(u, -SWIGLU_LIMIT, SWIGLU_LIMIT)
        act = (u + 1.0) * (g * jax.nn.sigmoid(SWIGLU_ALPHA * g))
        if prepare_next:
            cast_weight(1 - s, 2)
        y = jnp.dot(act.astype(bf16), wb[s][2][...], preferred_element_type=f32) + bd_ref[0]
        _store_packed(out_ref, _pack_rows(y), MOE_TM)

    used = j < nused_ref[0]
    for s in range(2):
        for prepare_next in (False, True):
            is_last = last_ref[j] == 1
            cond = jnp.logical_and(jnp.logical_and(used, wslot_ref[j] == s),
                                   is_last if prepare_next else jnp.logical_not(is_last))
            pl.when(cond)(functools.partial(expert_mlp, s, prepare_next))

    @pl.when(jnp.logical_not(used))
    def _():
        out_ref[...] = jnp.zeros_like(out_ref)


def _moe(tile_e, tile_first, tile_last, next_e, wslot, n_used, xs, w_gate, b_gate, w_up, b_up, w_down, b_down):
    ntile = tile_e.shape[0]
    bspec = pl.BlockSpec((1, 1, D_MODEL), lambda j, te, *_: (te[j], 0, 0))
    hbm = pl.BlockSpec(memory_space=pl.ANY)
    grid_spec = pltpu.PrefetchScalarGridSpec(
        num_scalar_prefetch=6,
        grid=(ntile,),
        in_specs=[pl.BlockSpec((MOE_TM * ROW_SUB, LANES),
                               lambda j, te, fi, la, ne, ws, nu: (jnp.minimum(j, nu[0] - 1), 0)),
                  hbm, hbm, hbm, bspec, bspec, bspec],
        out_specs=pl.BlockSpec((MOE_TM * ROW_SUB, LANES), lambda j, *_: (j, 0)),
        scratch_shapes=[pltpu.VMEM((2, 3, D_MODEL, D_MODEL), f32)]
        + [pltpu.VMEM((D_MODEL, D_MODEL), bf16)] * 6
        + [pltpu.SemaphoreType.DMA((2,))],
    )
    return pl.pallas_call(
        _moe_kernel,
        grid_spec=grid_spec,
        out_shape=jax.ShapeDtypeStruct((ntile * MOE_TM * ROW_SUB, LANES), i32),
        compiler_params=_params(("arbitrary",)),
        name="moe",
    )(tile_e, tile_first, tile_last, next_e, wslot, n_used, xs, w_gate, w_up, w_down, b_gate, b_up, b_down)


CB_TM = DISP_TM


def _combine_kernel(dcur_ref, dnxt_ref, yb_hbm, rw_ref, x1_ref, g2_ref, lng_ref, lnb_ref, out_ref,
                    ybuf0, ybuf1, sem):
    i = pl.program_id(0)
    last = pl.num_programs(0) - 1
    ybufs = (ybuf0, ybuf1)

    def row_copy(d, k, r, s):
        return pltpu.make_async_copy(
            yb_hbm.at[pl.ds(pl.multiple_of(d * ROW_SUB, ROW_SUB), ROW_SUB)],
            ybufs[s].at[pl.ds(pl.multiple_of((k * CB_TM + r) * ROW_SUB, ROW_SUB), ROW_SUB)],
            sem.at[s])

    @pl.when(i == 0)
    def _():
        for k in range(TOP_K):
            def body(r, c, k=k):
                row_copy(dcur_ref[0, k, r], k, r, 0).start()
                return c
            lax.fori_loop(0, CB_TM, body, 0, unroll=8)

    for s in range(2):
        @pl.when(i % 2 == s)
        def _(s=s):
            pltpu.make_async_copy(yb_hbm.at[pl.ds(0, TOP_K * CB_TM * ROW_SUB)], ybufs[s], sem.at[s]).wait()

            def finish_block():
                parts = [_unpack_rows(_load_packed(ybufs[s], k * CB_TM, CB_TM)) * rw_ref[:, k:k + 1]
                         for k in range(TOP_K)]
                y = (parts[0] + parts[1]) + (parts[2] + parts[3])
                out_ref[...] = _ln(DN_ALPHA * x1_ref[...] + g2_ref[0] * y) * lng_ref[...] + lnb_ref[...]

            @pl.when(i < last)
            def _():
                for k in range(TOP_K):
                    for r in range(CB_TM):
                        row_copy(dnxt_ref[0, k, r], k, r, 1 - s).start(priority=r % 2)
                finish_block()

            pl.when(i == last)(finish_block)


def _combine(dest3, yb, rw, x1, g2, ln2g, ln2b, seq):
    t = x1.shape[0]
    nb = t // CB_TM
    per_b = seq // CB_TM
    return pl.pallas_call(
        _combine_kernel,
        grid=(nb,),
        in_specs=[pl.BlockSpec((1, TOP_K, CB_TM), lambda i: (i, 0, 0), memory_space=pltpu.SMEM),
                  pl.BlockSpec((1, TOP_K, CB_TM), lambda i: (jnp.minimum(i + 1, nb - 1), 0, 0),
                               memory_space=pltpu.SMEM),
                  pl.BlockSpec(memory_space=pl.ANY),
                  pl.BlockSpec((CB_TM, LANES), lambda i: (i, 0)),
                  pl.BlockSpec((CB_TM, D_MODEL), lambda i: (i, 0)),
                  pl.BlockSpec((1, 1, D_MODEL), lambda i: (i // per_b, 0, 0)),
                  pl.BlockSpec((1, D_MODEL), lambda i: (0, 0)),
                  pl.BlockSpec((1, D_MODEL), lambda i: (0, 0))],
        out_specs=pl.BlockSpec((CB_TM, D_MODEL), lambda i: (i, 0)),
        out_shape=jax.ShapeDtypeStruct((t, D_MODEL), f32),
        scratch_shapes=[pltpu.VMEM((TOP_K * CB_TM * ROW_SUB, LANES), i32),
                        pltpu.VMEM((TOP_K * CB_TM * ROW_SUB, LANES), i32),
                        pltpu.SemaphoreType.DMA((2,))],
        compiler_params=_params(("arbitrary",)),
        name="combine",
    )(dest3, dest3, yb, rw, x1, g2, ln2g, ln2b)


def _t5_bucket(dist):
    d = dist.astype(f32)
    large = REL_MAX_EXACT + jnp.log(jnp.maximum(d, float(REL_MAX_EXACT)) / REL_MAX_EXACT) / math.log(
        REL_MAX_DIST / REL_MAX_EXACT) * (REL_BUCKETS - REL_MAX_EXACT)
    large = jnp.minimum(large.astype(i32), REL_BUCKETS - 1)
    return jnp.where(dist < REL_MAX_EXACT, dist, large)


def _bias_indices():
    qi = jnp.arange(ATT_BLOCK)[:, None]
    ki = jnp.arange(2 * ATT_BLOCK)[None, :]
    didx = qi + ATT_BLOCK - ki
    buckets, bands = [], []
    for win, dil in DIL_PAIRS:
        buckets.append(_t5_bucket(jnp.clip(didx, 0, None) * dil))
        bands.append(((didx >= 0) & (didx <= win // dil)).astype(i32))
    return jnp.stack(buckets).astype(i32), jnp.stack(bands)


def _residue_perm(tm, dil):
    n = tm // dil
    dst = np.arange(tm)
    src = (dst % n) * dil + dst // n
    return src[:, None] == np.arange(tm)[None, :]


def kernel(x, c, w_ada, b_ada, w_in, gm_ln_g, gm_ln_b, gm_w_s, gm_b_s, w_branch_a, w_branch_b, w_out,
           rel_bias, ln1_g, ln1_b, w_router, b_router, w_gate, b_gate, w_up, b_up, w_down, b_down,
           ln2_g, ln2_b):
    batch, seq, _ = x.shape
    t = batch * seq
    l = 0
    x2 = x.reshape(t, D_MODEL)

    c8 = jnp.pad(c, ((0, 8 - batch), (0, 0)))
    mod = _adaln(c8, w_ada[l], b_ada[l][None, :])[:batch]
    sh1, sc1, g1, sh2, sc2, g2 = [m[:, None, :] for m in jnp.split(mod, 6, axis=-1)]

    perms = [jnp.asarray(_residue_perm(IN_TM, dil), bf16) for _win, dil in DIL_PAIRS]
    bs_full = jnp.repeat(gm_b_s[l].T, GM_WIDTH // GM_GROUPS, axis=1)
    ya, gates, *qkvs = _inproj(x2, sc1, sh1, w_in[l].astype(bf16), perms,
                               gm_ln_g[l][None, :], gm_ln_b[l][None, :], gm_w_s[l], bs_full, batch, seq)

    bucket, band = _bias_indices()
    bias = _relbias(rel_bias, bucket, band)
    os_, ls_ = [], []
    for g, (_win, dil) in enumerate(DIL_PAIRS):
        o, lse = _attn_group(qkvs[g], bias, g, dil, batch, seq)
        os_.append(o)
        ls_.append(lse)

    wr = jnp.pad(w_router[l], ((0, 0), (0, LANES - N_EXPERTS)))
    wr_hi = wr.astype(bf16)
    wr_parts = jnp.concatenate([wr_hi, (wr - wr_hi.astype(f32)).astype(bf16)], axis=1)
    br = jnp.pad(b_router[l], (0, LANES - N_EXPERTS))[None, :]
    tri = jnp.asarray(np.arange(MIX_TM)[None, :] < np.arange(MIX_TM)[:, None], bf16)
    perms_t = [jnp.asarray(_residue_perm(MIX_TM, dil).T, bf16) for _win, dil in DIL_PAIRS]
    expand = np.arange(LANES)[:, None] == np.arange(ATT_WIDTH)[None, :] // HEAD_DIM
    expand = jnp.asarray(np.concatenate([expand, expand], axis=0), bf16)
    x1, h2, route, rw, cnt = _mix(
        os_, ls_, perms_t, expand, ya, gates, x2, g1, sc2, sh2,
        w_branch_a[l], w_branch_b[l], w_out[l],
        ln1_g[l][None, :], ln1_b[l][None, :], wr_parts, br, tri, seq)

    top_e = route[:, :TOP_K]
    rank = route[:, TOP_K:2 * TOP_K]
    counts = cnt[0, :N_EXPERTS].astype(i32)
    pcounts = (counts + MOE_TM - 1) // MOE_TM * MOE_TM
    experts = jnp.arange(N_EXPERTS, dtype=i32)
    upto = experts[None, :] <= experts[:, None]
    pends = jnp.sum(jnp.where(upto, pcounts[None, :], 0), axis=1)
    pstarts = pends - pcounts
    dest = jnp.sum(jnp.where(top_e[:, :, None] == experts, pstarts, 0), axis=-1) + rank
    ntile = t * TOP_K // MOE_TM + N_EXPERTS
    n_used = (pends[-1] // MOE_TM).reshape(1)
    tile_idx = jnp.minimum(jnp.arange(ntile, dtype=i32), n_used - 1)
    tile_e = jnp.sum((pends[None, :] <= (tile_idx * MOE_TM)[:, None]).astype(i32), axis=1)
    tile_first = jnp.concatenate([jnp.ones((1,), i32), (tile_e[1:] != tile_e[:-1]).astype(i32)])
    nonempty = counts > 0
    later = jnp.logical_and(experts[None, :] > experts[:, None], nonempty[None, :])
    next_nonempty = jnp.min(jnp.where(later, experts[None, :], N_EXPERTS), axis=1)
    next_nonempty = jnp.where(next_nonempty >= N_EXPERTS, -1, next_nonempty)
    expert_slot = (jnp.sum(jnp.logical_and(upto, nonempty[None, :]).astype(i32), axis=1) - 1) % 2
    of_tile = tile_e[:, None] == experts[None, :]
    tile_next = jnp.sum(jnp.where(of_tile, next_nonempty[None, :], 0), axis=1)
    tile_slot = jnp.sum(jnp.where(of_tile, expert_slot[None, :], 0), axis=1)
    tile_last = jnp.concatenate([(tile_e[1:] != tile_e[:-1]).astype(i32), jnp.zeros((1,), i32)])
    tile_last = jnp.where(tile_next >= 0, tile_last, 0)

    dest3 = dest.reshape(t // DISP_TM, DISP_TM, TOP_K).transpose(0, 2, 1)
    xs = _dispatch(pends, pcounts, n_used, dest3, h2, ntile)
    yb = _moe(tile_e, tile_first, tile_last, tile_next, tile_slot, n_used, xs,
              w_gate[l], b_gate[l][:, None, :], w_up[l], b_up[l][:, None, :],
              w_down[l], b_down[l][:, None, :])
    out = _combine(dest3, yb, rw, x1, g2, ln2_g[l][None, :], ln2_b[l][None, :], seq)
    return out.reshape(batch, seq, D_MODEL)
```

```python
import functools
import math

import numpy as np
import jax
import jax.numpy as jnp
from jax import lax
from jax.experimental import pallas as pl
from jax.experimental.pallas import tpu as pltpu

f32 = jnp.float32
bf16 = jnp.bfloat16
i32 = jnp.int32

D_MODEL = 1024
GM_WIDTH = 512
GM_GROUPS = 8
GM_CHUNK = 128
DIL_PAIRS = ((128, 1), (512, 4), (2048, 16))
N_DIL = 3
HEADS_PER_GROUP = 8
HEAD_DIM = 64
ATT_WIDTH = 512
ATT_BLOCK = 128
NEG_INF = -1e30
REL_BUCKETS = 32
REL_MAX_EXACT = 16
REL_MAX_DIST = 2048
N_EXPERTS = 32
TOP_K = 4
SWIGLU_LIMIT = 7.0
SWIGLU_ALPHA = 1.702
DEPTH = 1
DN_ALPHA = (2 * DEPTH) ** 0.25
LN_EPS = 1e-5
UV_COLS = 2 * GM_WIDTH
QKV_COLS = N_DIL * 3 * ATT_WIDTH
GATE_COLS = 2 * D_MODEL
IN_COLS = UV_COLS + QKV_COLS + GATE_COLS

LANES = 128
VMEM_LIMIT = 56 * 1024 * 1024


def _ln(x):
    mu = jnp.mean(x, axis=-1, keepdims=True)
    xc = x - mu
    var = jnp.mean(xc * xc, axis=-1, keepdims=True)
    return xc * lax.rsqrt(var + LN_EPS)


def _params(sem, vmem=VMEM_LIMIT):
    return pltpu.CompilerParams(dimension_semantics=sem, vmem_limit_bytes=vmem)


def _adaln_kernel(c_ref, w_ref, b_ref, o_ref):
    c = c_ref[...]
    s = c * jax.nn.sigmoid(c)
    o_ref[...] = jnp.dot(s, w_ref[...], preferred_element_type=f32,
                         precision=lax.Precision.HIGHEST) + b_ref[...]


def _adaln(c8, w_ada, b_ada):
    n = w_ada.shape[1] // D_MODEL
    return pl.pallas_call(
        _adaln_kernel,
        grid=(n,),
        in_specs=[pl.BlockSpec((8, D_MODEL), lambda j: (0, 0)),
                  pl.BlockSpec((D_MODEL, D_MODEL), lambda j: (0, j)),
                  pl.BlockSpec((1, D_MODEL), lambda j: (0, j))],
        out_specs=pl.BlockSpec((8, D_MODEL), lambda j: (0, j)),
        out_shape=jax.ShapeDtypeStruct((8, w_ada.shape[1]), f32),
        compiler_params=_params(("arbitrary",)),
        name="adaln",
    )(c8, w_ada, b_ada)


IN_TM = 256
IN_CW = 512
GRP_COLS = 3 * ATT_WIDTH


def _inproj_kernel(x_ref, sc_ref, sh_ref, w_ref, p1_ref, p2_ref, gm_g_ref, gm_b_ref, ws_ref, bs_ref,
                   ya_ref, gt_ref, qkv0_ref, qkv1_ref, qkv2_ref):
    xn = _ln(x_ref[...])
    h = (xn * (1.0 + sc_ref[0]) + sh_ref[0]).astype(bf16)
    hp = [h,
          jnp.dot(p1_ref[...], h, preferred_element_type=f32).astype(bf16),
          jnp.dot(p2_ref[...], h, preferred_element_type=f32).astype(bf16)]
    u_act, v_act = [jax.nn.gelu(jnp.dot(h, w_ref[:, c0:c0 + GM_WIDTH], preferred_element_type=f32))
                    for c0 in range(0, UV_COLS, GM_WIDTH)]
    row = lax.broadcasted_iota(i32, (GM_CHUNK, GM_CHUNK), 0)
    col = lax.broadcasted_iota(i32, (GM_CHUNK, GM_CHUNK), 1)
    first_half = lax.broadcasted_iota(i32, (GM_CHUNK, LANES), 1) < (GM_WIDTH // GM_GROUPS)
    ws = [jnp.where(col <= row, ws_ref[g], 0.0).astype(bf16) for g in range(GM_GROUPS)]

    def gate_chunk(r0):
        vn = (_ln(v_act[r0:r0 + GM_CHUNK, :]) * gm_g_ref[...] + gm_b_ref[...]).astype(bf16)
        for j in range(GM_WIDTH // LANES):
            sl = slice(j * LANES, (j + 1) * LANES)
            s_lo = jnp.dot(ws[2 * j], vn[:, sl], preferred_element_type=f32)
            s_hi = jnp.dot(ws[2 * j + 1], vn[:, sl], preferred_element_type=f32)
            s = jnp.where(first_half, s_lo, s_hi) + bs_ref[:, sl]
            ya_ref[r0:r0 + GM_CHUNK, sl] = (u_act[r0:r0 + GM_CHUNK, sl] * s).astype(bf16)

    pending = list(range(0, IN_TM, GM_CHUNK))
    for g, (qref, (_win, dil)) in enumerate(zip((qkv0_ref, qkv1_ref, qkv2_ref), DIL_PAIRS)):
        n = IN_TM // dil
        for q0 in range(0, GRP_COLS, IN_CW):
            c0 = UV_COLS + g * GRP_COLS + q0
            acc = jnp.dot(hp[g], w_ref[:, c0:c0 + IN_CW], preferred_element_type=f32).astype(bf16)
            for rho in range(dil):
                qref[0, rho, :, q0:q0 + IN_CW] = acc[rho * n:(rho + 1) * n, :]
        if pending:
            gate_chunk(pending.pop(0))
    while pending:
        gate_chunk(pending.pop(0))
    for g0 in range(0, GATE_COLS, IN_CW):
        c0 = UV_COLS + QKV_COLS + g0
        acc = jnp.dot(h, w_ref[:, c0:c0 + IN_CW], preferred_element_type=f32)
        gt_ref[:, g0:g0 + IN_CW] = jax.nn.sigmoid(acc).astype(bf16)


def _inproj(x2, sc1, sh1, w_in_bf, perms, gm_g, gm_b, w_s, bs_full, batch, seq):
    t = x2.shape[0]
    per_b = seq // IN_TM
    qkv_specs, qkv_shapes = [], []
    for _win, dil in DIL_PAIRS:
        n = IN_TM // dil
        qkv_specs.append(pl.BlockSpec((1, dil, n, GRP_COLS), lambda i: (i // per_b, 0, i % per_b, 0)))
        qkv_shapes.append(jax.ShapeDtypeStruct((batch, dil, seq // dil, GRP_COLS), bf16))
    return pl.pallas_call(
        _inproj_kernel,
        grid=(t // IN_TM,),
        in_specs=[pl.BlockSpec((IN_TM, D_MODEL), lambda i: (i, 0)),
                  pl.BlockSpec((1, 1, D_MODEL), lambda i: (i // per_b, 0, 0)),
                  pl.BlockSpec((1, 1, D_MODEL), lambda i: (i // per_b, 0, 0)),
                  pl.BlockSpec((D_MODEL, IN_COLS), lambda i: (0, 0)),
                  pl.BlockSpec((IN_TM, IN_TM), lambda i: (0, 0)),
                  pl.BlockSpec((IN_TM, IN_TM), lambda i: (0, 0)),
                  pl.BlockSpec((1, GM_WIDTH), lambda i: (0, 0)),
                  pl.BlockSpec((1, GM_WIDTH), lambda i: (0, 0)),
                  pl.BlockSpec((GM_GROUPS, GM_CHUNK, GM_CHUNK), lambda i: (0, 0, 0)),
                  pl.BlockSpec((GM_CHUNK, GM_WIDTH), lambda i: (0, 0))],
        out_specs=[pl.BlockSpec((IN_TM, GM_WIDTH), lambda i: (i, 0)),
                   pl.BlockSpec((IN_TM, GATE_COLS), lambda i: (i, 0))] + qkv_specs,
        out_shape=[jax.ShapeDtypeStruct((t, GM_WIDTH), bf16),
                   jax.ShapeDtypeStruct((t, GATE_COLS), bf16)] + qkv_shapes,
        compiler_params=_params(("arbitrary",)),
        name="inproj",
    )(x2, sc1, sh1, w_in_bf, perms[1], perms[2], gm_g, gm_b, w_s, bs_full)


def _relbias_kernel(tab_ref, bucket_ref, band_ref, out_ref):
    g = pl.program_id(0)
    bk = bucket_ref[0]
    band = band_ref[0] > 0
    for h in range(HEADS_PER_GROUP):
        acc = jnp.zeros((ATT_BLOCK, 2 * ATT_BLOCK), f32)
        for b in range(REL_BUCKETS):
            acc = jnp.where(bk == b, tab_ref[b, g * HEADS_PER_GROUP + h], acc)
        out_ref[0, h] = jnp.where(band, acc, NEG_INF)


def _relbias(rel_bias, bucket, band):
    return pl.pallas_call(
        _relbias_kernel,
        grid=(N_DIL,),
        in_specs=[pl.BlockSpec(memory_space=pltpu.SMEM),
                  pl.BlockSpec((1, ATT_BLOCK, 2 * ATT_BLOCK), lambda g: (g, 0, 0)),
                  pl.BlockSpec((1, ATT_BLOCK, 2 * ATT_BLOCK), lambda g: (g, 0, 0))],
        out_specs=pl.BlockSpec((1, HEADS_PER_GROUP, ATT_BLOCK, 2 * ATT_BLOCK),
                               lambda g: (g, 0, 0, 0)),
        out_shape=jax.ShapeDtypeStruct((N_DIL, HEADS_PER_GROUP, ATT_BLOCK, 2 * ATT_BLOCK), f32),
        compiler_params=_params(("arbitrary",)),
        name="relbias",
    )(rel_bias, bucket, band)


ATT_MAX_STEP_BLOCKS = 8


def _attn_kernel(nres, nblk, q_ref, kp_ref, kc_ref, vp_ref, vc_ref, bias_ref, o_ref, lse_ref):
    first = pl.program_id(2) == 0
    lane = lax.broadcasted_iota(i32, (ATT_BLOCK, LANES), 1)
    lo_half = lane < HEAD_DIM
    prev_cols = lax.broadcasted_iota(i32, (ATT_BLOCK, 2 * ATT_BLOCK), 1) < ATT_BLOCK
    no_prev = jnp.logical_and(first, prev_cols)
    nt = (((1,), (1,)), ((), ()))
    ones = jnp.ones((2 * ATT_BLOCK, LANES), bf16)
    n_slab = ATT_WIDTH // LANES
    blocks = [(res, i) for res in range(nres) for i in range(nblk)]
    logits, v_ext = [], []
    for res, i in blocks:
        cur = slice(i * ATT_BLOCK, (i + 1) * ATT_BLOCK)
        prv = slice((i - 1) * ATT_BLOCK, i * ATT_BLOCK)
        for j in range(n_slab):
            sl = slice(j * LANES, (j + 1) * LANES)
            q = q_ref[0, res, cur, sl] * (HEAD_DIM ** -0.5)
            k_prev = kp_ref[0, res, :, sl] if i == 0 else kc_ref[0, res, prv, sl]
            v_prev = vp_ref[0, res, :, sl] if i == 0 else vc_ref[0, res, prv, sl]
            k_cat = jnp.concatenate([k_prev, kc_ref[0, res, cur, sl]], axis=0)
            v_cat = jnp.concatenate([v_prev, vc_ref[0, res, cur, sl]], axis=0)
            v_ext.append(jnp.concatenate([v_cat, ones], axis=1))
            for hh in range(2):
                qm = jnp.where(lo_half if hh == 0 else jnp.logical_not(lo_half), q, 0.0).astype(bf16)
                lg_h = lax.dot_general(qm, k_cat, nt, preferred_element_type=f32) + bias_ref[0, 2 * j + hh]
                logits.append(jnp.where(no_prev, NEG_INF, lg_h) if i == 0 else lg_h)
    rows_per_block = HEADS_PER_GROUP * ATT_BLOCK
    lg = jnp.concatenate(logits, axis=0)
    m = jnp.max(lg, axis=-1, keepdims=True)
    p = jnp.exp(lg - m).astype(bf16)
    for b, (res, i) in enumerate(blocks):
        cur = slice(i * ATT_BLOCK, (i + 1) * ATT_BLOCK)
        lse_tile = jnp.zeros((ATT_BLOCK, LANES), f32)
        for j in range(n_slab):
            outs = []
            for hh in range(2):
                h = 2 * j + hh
                r0 = b * rows_per_block + h * ATT_BLOCK
                r = jnp.dot(p[r0:r0 + ATT_BLOCK], v_ext[b * n_slab + j], preferred_element_type=f32)
                den = r[:, LANES:]
                outs.append(r[:, :LANES] * (1.0 / den))
                lse_h = m[r0:r0 + ATT_BLOCK] + jnp.log(den)
                lse_tile = jnp.where(lane == h, lse_h, lse_tile)
            o_ref[0, res, cur, j * LANES:(j + 1) * LANES] = jnp.where(lo_half, outs[0], outs[1]).astype(bf16)
        lse_ref[0, res, cur, :] = lse_tile


def _attn_group(qkv_g, bias, g, dil, batch, seq):
    l = seq // dil
    nblk = min(ATT_MAX_STEP_BLOCKS, l // ATT_BLOCK)
    nres = min(dil, ATT_MAX_STEP_BLOCKS // nblk)
    tm = nblk * ATT_BLOCK
    nsteps = l // tm

    def cur(cb):
        return pl.BlockSpec((1, nres, tm, ATT_WIDTH), lambda b, r, n: (b, r, n, cb))

    def prev(cb):
        return pl.BlockSpec((1, nres, ATT_BLOCK, ATT_WIDTH),
                            lambda b, r, n: (b, r, jnp.maximum(n * nblk - 1, 0), cb))

    return pl.pallas_call(
        functools.partial(_attn_kernel, nres, nblk),
        grid=(batch, dil // nres, nsteps),
        in_specs=[cur(0), prev(1), cur(1), prev(2), cur(2),
                  pl.BlockSpec((1, HEADS_PER_GROUP, ATT_BLOCK, 2 * ATT_BLOCK),
                               lambda b, r, n: (g, 0, 0, 0))],
        out_specs=[pl.BlockSpec((1, nres, tm, ATT_WIDTH), lambda b, r, n: (b, r, n, 0)),
                   pl.BlockSpec((1, nres, tm, LANES), lambda b, r, n: (b, r, n, 0))],
        out_shape=[jax.ShapeDtypeStruct((batch, dil, l, ATT_WIDTH), bf16),
                   jax.ShapeDtypeStruct((batch, dil, l, LANES), f32)],
        compiler_params=_params(("arbitrary", "arbitrary", "arbitrary")),
        name=f"attn_g{g}",
    )(qkv_g, qkv_g, qkv_g, qkv_g, qkv_g, bias)


ROW_WORDS = D_MODEL // 2
ROW_SUB = ROW_WORDS // LANES
HI_MASK = -65536


def _pack_rows(x):
    bits = lax.bitcast_convert_type(x.astype(bf16).astype(f32), i32)
    return lax.shift_right_logical(bits[:, :ROW_WORDS], 16) | (bits[:, ROW_WORDS:] & HI_MASK)


def _unpack_rows(words):
    lo = lax.bitcast_convert_type(lax.shift_left(words, 16), f32)
    hi = lax.bitcast_convert_type(words & HI_MASK, f32)
    return jnp.concatenate([lo, hi], axis=1)


def _store_packed(ref, words, n, first_row=0):
    for r in range(ROW_SUB):
        ref[pl.ds(first_row * ROW_SUB + r, n, stride=ROW_SUB), :] = words[:, r * LANES:(r + 1) * LANES]


def _load_packed(ref, first_row, n):
    return jnp.concatenate([ref[pl.ds(first_row * ROW_SUB + r, n, stride=ROW_SUB), :] for r in range(ROW_SUB)],
                           axis=1)


MIX_TM = 256
MIX_SUB = 128


def _split_bf16(x, parts):
    out = []
    for _ in range(parts):
        hi = x.astype(bf16)
        out.append(hi)
        x = x - hi.astype(f32)
    return out


def _mix_kernel(o0_ref, o1_ref, o2_ref, l0_ref, l1_ref, l2_ref, pt1_ref, pt2_ref, ex_ref,
                ya_ref, gt_ref, x_ref,
                g1_ref, sc2_ref, sh2_ref, wa32_ref, wb32_ref, wo32_ref, ln1g_ref, ln1b_ref,
                wrc_ref, br_ref, tri_ref,
                x1_ref, h2_ref, route_ref, rw_ref, cnt_ref, run_ref, xr_ref, wa_ref, wb_ref, wo_ref):
    step = pl.program_id(0)

    @pl.when(step == 0)
    def _():
        run_ref[...] = jnp.zeros_like(run_ref)
        xr_ref[...] = jnp.zeros_like(xr_ref)
        wa_ref[...] = wa32_ref[...].astype(bf16)
        wb_ref[...] = wb32_ref[...].astype(bf16)
        wo_ref[...] = wo32_ref[...].astype(bf16)

    def back_norms(r0):
        rows = slice(r0, r0 + MIX_SUB)
        x1 = _ln(xr_ref[rows, :]) * ln1g_ref[...] + ln1b_ref[...]
        x1_ref[rows, :] = x1
        h2 = _ln(x1) * (1.0 + sc2_ref[0]) + sh2_ref[0]
        _store_packed(h2_ref, _pack_rows(h2), MIX_SUB, r0)
        return h2

    def back_router(h2):
        h_hi, h_lo = _split_bf16(h2, 2)
        hi_both = jnp.dot(h_hi, wrc_ref[...], preferred_element_type=f32)
        return (hi_both[:, :LANES]
                + (hi_both[:, LANES:] + jnp.dot(h_lo, wrc_ref[:, :LANES], preferred_element_type=f32))
                ) + br_ref[...]

    def front_merge(r0):
        rows = slice(r0, r0 + MIX_SUB)
        os_, ls_ = [o0_ref[0, 0, rows, :].astype(f32)], [l0_ref[0, 0, rows, :]]
        for o_ref, l_ref, pt_ref in ((o1_ref, l1_ref, pt1_ref), (o2_ref, l2_ref, pt2_ref)):
            pt = pt_ref[rows, :]
            os_.append(jnp.dot(pt, o_ref[0].reshape(MIX_TM, ATT_WIDTH), preferred_element_type=f32))
            parts = [jnp.dot(pt, part, preferred_element_type=f32)
                     for part in _split_bf16(l_ref[0].reshape(MIX_TM, LANES), 3)]
            ls_.append((parts[0] + parts[1]) + parts[2])
        lm = jnp.maximum(jnp.maximum(ls_[0], ls_[1]), ls_[2])
        es = [jnp.exp(lse - lm) for lse in ls_]
        inv = 1.0 / (es[0] + es[1] + es[2])
        yb = jnp.zeros((MIX_SUB, ATT_WIDTH), f32)
        for e, o in zip(es, os_):
            w_parts = jnp.concatenate(_split_bf16(e * inv, 2), axis=1)
            yb = yb + jnp.dot(w_parts, ex_ref[...], preferred_element_type=f32) * o
        return yb.astype(bf16)

    def front_branches(r0, yb):
        rows = slice(r0, r0 + MIX_SUB)
        a = jnp.dot(ya_ref[rows, :], wa_ref[...], preferred_element_type=f32)
        b = jnp.dot(yb, wb_ref[...], preferred_element_type=f32)
        return (gt_ref[rows, :D_MODEL].astype(f32) * a + gt_ref[rows, D_MODEL:].astype(f32) * b).astype(bf16)

    def front_out(r0, merged):
        rows = slice(r0, r0 + MIX_SUB)
        mix = jnp.dot(merged, wo_ref[...], preferred_element_type=f32)
        xr_ref[rows, :] = DN_ALPHA * x_ref[rows, :] + g1_ref[0] * mix

    subs = list(range(0, MIX_TM, MIX_SUB))
    logit_parts = []
    for r0 in subs:
        h2 = back_norms(r0)
        yb = front_merge(r0)
        logit_parts.append(back_router(h2))
        merged = front_branches(r0, yb)
        front_out(r0, merged)
    logits = jnp.concatenate(logit_parts, axis=0)
    lane = lax.broadcasted_iota(i32, (MIX_TM, LANES), 1)
    logits = jnp.where(lane < N_EXPERTS, logits, -jnp.inf)
    lane_f = lane.astype(f32)
    vals, idxs = [], []
    for _k in range(TOP_K):
        m = jnp.max(logits, axis=-1, keepdims=True)
        vals.append(m)
        idxs.append(jnp.min(jnp.where(logits == m, lane_f, float(LANES)), axis=-1, keepdims=True).astype(i32))
        logits = jnp.where(lane == idxs[-1], -jnp.inf, logits)
    exps = [jnp.exp(v - vals[0]) for v in vals]
    den = exps[0] + exps[1] + exps[2] + exps[3]
    wts = [e / den for e in exps]
    hits = [lane == idx for idx in idxs]
    counted = jnp.where(step > 0, 1.0, 0.0)
    onehot = jnp.zeros((MIX_TM, LANES), f32)
    for hit in hits:
        onehot = onehot + jnp.where(hit, counted, 0.0)
    prefix = jnp.dot(tri_ref[...], onehot.astype(bf16), preferred_element_type=f32) + run_ref[...]
    route = jnp.zeros((MIX_TM, LANES), i32)
    rw = jnp.zeros((MIX_TM, LANES), f32)
    for k in range(TOP_K):
        rank = jnp.sum(jnp.where(hits[k], prefix, 0.0), axis=-1, keepdims=True).astype(i32)
        route = jnp.where(lane == k, idxs[k], route)
        route = jnp.where(lane == TOP_K + k, rank, route)
        rw = jnp.where(lane == k, wts[k], rw)
    route_ref[...] = route
    rw_ref[...] = rw
    run = run_ref[...] + jnp.sum(onehot, axis=0, keepdims=True)
    run_ref[...] = run
    cnt_ref[...] = jnp.broadcast_to(run, cnt_ref.shape)


def _mix(os_, ls_, perms_t, expand, ya, gates, x2, g1, sc2, sh2, wa, wb, wo, ln1g, ln1b, wr_parts, br, tri, seq):
    t = x2.shape[0]
    nb = t // MIX_TM
    per_b = seq // MIX_TM
    cur = lambda i: jnp.minimum(i, nb - 1)
    prv = lambda i: jnp.maximum(i - 1, 0)
    row = lambda w: pl.BlockSpec((MIX_TM, w), lambda i: (cur(i), 0))
    out_row = lambda w: pl.BlockSpec((MIX_TM, w), lambda i: (prv(i), 0))
    const = lambda s: pl.BlockSpec(s, lambda i: tuple(0 for _ in s))
    mod_cur = pl.BlockSpec((1, 1, D_MODEL), lambda i: (cur(i) // per_b, 0, 0))
    mod_prv = pl.BlockSpec((1, 1, D_MODEL), lambda i: (prv(i) // per_b, 0, 0))
    grp = lambda w: [pl.BlockSpec((1, dil, MIX_TM // dil, w), lambda i: (cur(i) // per_b, 0, cur(i) % per_b, 0))
                     for _win, dil in DIL_PAIRS]
    return pl.pallas_call(
        _mix_kernel,
        grid=(nb + 1,),
        in_specs=grp(ATT_WIDTH) + grp(LANES) + [
                  const((MIX_TM, MIX_TM)), const((MIX_TM, MIX_TM)), const((2 * LANES, ATT_WIDTH)),
                  row(GM_WIDTH), row(GATE_COLS), row(D_MODEL),
                  mod_cur, mod_prv, mod_prv,
                  const((GM_WIDTH, D_MODEL)), const((ATT_WIDTH, D_MODEL)), const((D_MODEL, D_MODEL)),
                  const((1, D_MODEL)), const((1, D_MODEL)),
                  const((D_MODEL, 2 * LANES)), const((1, LANES)), const((MIX_TM, MIX_TM))],
        out_specs=[out_row(D_MODEL), pl.BlockSpec((MIX_TM * ROW_SUB, LANES), lambda i: (prv(i), 0)),
                   out_row(LANES), out_row(LANES), const((8, LANES))],
        out_shape=[jax.ShapeDtypeStruct((t, D_MODEL), f32),
                   jax.ShapeDtypeStruct((t * ROW_SUB, LANES), i32),
                   jax.ShapeDtypeStruct((t, LANES), i32),
                   jax.ShapeDtypeStruct((t, LANES), f32),
                   jax.ShapeDtypeStruct((8, LANES), f32)],
        scratch_shapes=[pltpu.VMEM((1, LANES), f32), pltpu.VMEM((MIX_TM, D_MODEL), f32),
                        pltpu.VMEM((GM_WIDTH, D_MODEL), bf16), pltpu.VMEM((ATT_WIDTH, D_MODEL), bf16),
                        pltpu.VMEM((D_MODEL, D_MODEL), bf16)],
        compiler_params=_params(("arbitrary",)),
        name="mix",
    )(*os_, *ls_, perms_t[1], perms_t[2], expand, ya, gates, x2, g1, sc2, sh2, wa, wb, wo,
      ln1g, ln1b, wr_parts, br, tri)


DISP_TM = 1024
MOE_TM = 512


def _dispatch_kernel(pends_ref, pcnt_ref, nused_ref, dest_ref, h2p_ref, xs_hbm, zbuf, sem, zsem):
    i = pl.program_id(0)
    ntile = xs_hbm.shape[0] // (MOE_TM * ROW_SUB)

    def zero_tile(first_row):
        return pltpu.make_async_copy(
            zbuf, xs_hbm.at[pl.ds(pl.multiple_of(first_row * ROW_SUB, MOE_TM * ROW_SUB), MOE_TM * ROW_SUB)], zsem)

    def for_each_zero_tile(fn):
        for e in range(N_EXPERTS):
            pl.when(pcnt_ref[e] > 0)(functools.partial(fn, lambda e=e: zero_tile(pends_ref[e] - MOE_TM)))
        for k in range(N_EXPERTS):
            tile = nused_ref[0] + k
            pl.when(tile < ntile)(functools.partial(fn, lambda tile=tile: zero_tile(tile * MOE_TM)))

    @pl.when(i == 0)
    def _():
        zbuf[...] = jnp.zeros_like(zbuf)
        for_each_zero_tile(lambda mk: mk().start())
        for_each_zero_tile(lambda mk: mk().wait())

    def row_copy(k, r):
        d = dest_ref[0, k, r]
        return pltpu.make_async_copy(h2p_ref.at[pl.ds(r * ROW_SUB, ROW_SUB)],
                                     xs_hbm.at[pl.ds(pl.multiple_of(d * ROW_SUB, ROW_SUB), ROW_SUB)], sem)

    for r in range(DISP_TM):
        for k in range(TOP_K):
            row_copy(k, r).start(priority=k % 2)
    for k in range(TOP_K):
        pltpu.make_async_copy(h2p_ref, xs_hbm.at[pl.ds(0, DISP_TM * ROW_SUB)], sem).wait()


def _dispatch(pends, pcounts, n_used, dest3, h2p, ntile):
    t = h2p.shape[0] // ROW_SUB
    grid_spec = pltpu.PrefetchScalarGridSpec(
        num_scalar_prefetch=3,
        grid=(t // DISP_TM,),
        in_specs=[pl.BlockSpec((1, TOP_K, DISP_TM), lambda i, *_: (i, 0, 0), memory_space=pltpu.SMEM),
                  pl.BlockSpec((DISP_TM * ROW_SUB, LANES), lambda i, *_: (i, 0))],
        out_specs=pl.BlockSpec(memory_space=pl.ANY),
        scratch_shapes=[pltpu.VMEM((MOE_TM * ROW_SUB, LANES), i32),
                        pltpu.SemaphoreType.DMA(()),
                        pltpu.SemaphoreType.DMA(())],
    )
    return pl.pallas_call(
        _dispatch_kernel,
        grid_spec=grid_spec,
        out_shape=jax.ShapeDtypeStruct((ntile * MOE_TM * ROW_SUB, LANES), i32),
        compiler_params=_params(("arbitrary",)),
        name="dispatch",
    )(pends, pcounts, n_used, dest3, h2p)


def _moe_kernel(te_ref, first_ref, last_ref, nexte_ref, wslot_ref, nused_ref,
                xs_ref, wg_hbm, wu_hbm, wd_hbm, bg_ref, bu_ref, bd_ref,
                out_ref, wbuf, wg0, wu0, wd0, wg1, wu1, wd1, sem_w):
    j = pl.program_id(0)
    wb = ((wg0, wu0, wd0), (wg1, wu1, wd1))

    def weight_copies(e, ws):
        return [pltpu.make_async_copy(w.at[e], wbuf.at[ws, k], sem_w.at[ws])
                for k, w in enumerate((wg_hbm, wu_hbm, wd_hbm))]

    def cast_weight(ws, k):
        wb[ws][k][...] = wbuf[ws, k].astype(bf16)

    @pl.when(j == 0)
    def _():
        for cp in weight_copies(te_ref[0], 0):
            cp.start()
        for cp in weight_copies(te_ref[0], 0):
            cp.wait()
        for k in range(3):
            cast_weight(0, k)

    @pl.when(first_ref[j] == 1)
    def _():
        ne = nexte_ref[j]

        @pl.when(ne >= 0)
        def _():
            for cp in weight_copies(ne, 1 - wslot_ref[j]):
                cp.start()

    def expert_mlp(s, prepare_next):
        if prepare_next:
            for cp in weight_copies(nexte_ref[j], 1 - s):
                cp.wait()
        xb = _unpack_rows(_load_packed(xs_ref, 0, MOE_TM)).astype(bf16)
        if prepare_next:
            cast_weight(1 - s, 0)
        g = jnp.dot(xb, wb[s][0][...], preferred_element_type=f32) + bg_ref[0]
        if prepare_next:
            cast_weight(1 - s, 1)
        u = jnp.dot(xb, wb[s][1][...], preferred_element_type=f32) + bu_ref[0]
        g = jnp.minimum(g, SWIGLU_LIMIT)
        u = jnp.clip(u, -SWIGLU_LIMIT, SWIGLU_LIMIT)
        act = (u + 1.0) * (g * jax.nn.sigmoid(SWIGLU_ALPHA * g))
        if prepare_next:
            cast_weight(1 - s, 2)
        y = jnp.dot(act.astype(bf16), wb[s][2][...], preferred_element_type=f32) + bd_ref[0]
        _store_packed(out_ref, _pack_rows(y), MOE_TM)

    used = j < nused_ref[0]
    for s in range(2):
        for prepare_next in (False, True):
            is_last = last_ref[j] == 1
            cond = jnp.logical_and(jnp.logical_and(used, wslot_ref[j] == s),
                                   is_last if prepare_next else jnp.logical_not(is_last))
            pl.when(cond)(functools.partial(expert_mlp, s, prepare_next))

    @pl.when(jnp.logical_not(used))
    def _():
        out_ref[...] = jnp.zeros_like(out_ref)


def _moe(tile_e, tile_first, tile_last, next_e, wslot, n_used, xs, w_gate, b_gate, w_up, b_up, w_down, b_down):
    ntile = tile_e.shape[0]
    bspec = pl.BlockSpec((1, 1, D_MODEL), lambda j, te, *_: (te[j], 0, 0))
    hbm = pl.BlockSpec(memory_space=pl.ANY)
    grid_spec = pltpu.PrefetchScalarGridSpec(
        num_scalar_prefetch=6,
        grid=(ntile,),
        in_specs=[pl.BlockSpec((MOE_TM * ROW_SUB, LANES),
                               lambda j, te, fi, la, ne, ws, nu: (jnp.minimum(j, nu[0] - 1), 0)),
                  hbm, hbm, hbm, bspec, bspec, bspec],
        out_specs=pl.BlockSpec((MOE_TM * ROW_SUB, LANES), lambda j, *_: (j, 0)),
        scratch_shapes=[pltpu.VMEM((2, 3, D_MODEL, D_MODEL), f32)]
        + [pltpu.VMEM((D_MODEL, D_MODEL), bf16)] * 6
        + [pltpu.SemaphoreType.DMA((2,))],
    )
    return pl.pallas_call(
        _moe_kernel,
        grid_spec=grid_spec,
        out_shape=jax.ShapeDtypeStruct((ntile * MOE_TM * ROW_SUB, LANES), i32),
        compiler_params=_params(("arbitrary",)),
        name="moe",
    )(tile_e, tile_first, tile_last, next_e, wslot, n_used, xs, w_gate, w_up, w_down, b_gate, b_up, b_down)


CB_TM = DISP_TM


def _combine_kernel(dcur_ref, dnxt_ref, yb_hbm, rw_ref, x1_ref, g2_ref, lng_ref, lnb_ref, out_ref,
                    ybuf0, ybuf1, sem):
    i = pl.program_id(0)
    last = pl.num_programs(0) - 1
    ybufs = (ybuf0, ybuf1)

    def row_copy(d, k, r, s):
        return pltpu.make_async_copy(
            yb_hbm.at[pl.ds(pl.multiple_of(d * ROW_SUB, ROW_SUB), ROW_SUB)],
            ybufs[s].at[pl.ds(pl.multiple_of((k * CB_TM + r) * ROW_SUB, ROW_SUB), ROW_SUB)],
            sem.at[s])

    @pl.when(i == 0)
    def _():
        for k in range(TOP_K):
            def body(r, c, k=k):
                row_copy(dcur_ref[0, k, r], k, r, 0).start()
                return c
            lax.fori_loop(0, CB_TM, body, 0, unroll=8)

    for s in range(2):
        @pl.when(i % 2 == s)
        def _(s=s):
            pltpu.make_async_copy(yb_hbm.at[pl.ds(0, TOP_K * CB_TM * ROW_SUB)], ybufs[s], sem.at[s]).wait()

            def finish_block():
                parts = [_unpack_rows(_load_packed(ybufs[s], k * CB_TM, CB_TM)) * rw_ref[:, k:k + 1]
                         for k in range(TOP_K)]
                y = (parts[0] + parts[1]) + (parts[2] + parts[3])
                out_ref[...] = _ln(DN_ALPHA * x1_ref[...] + g2_ref[0] * y) * lng_ref[...] + lnb_ref[...]

            @pl.when(i < last)
            def _():
                for k in range(TOP_K):
                    for r in range(CB_TM):
                        row_copy(dnxt_ref[0, k, r], k, r, 1 - s).start(priority=r % 2)
                finish_block()

            pl.when(i == last)(finish_block)


def _combine(dest3, yb, rw, x1, g2, ln2g, ln2b, seq):
    t = x1.shape[0]
    nb = t // CB_TM
    per_b = seq // CB_TM
    return pl.pallas_call(
        _combine_kernel,
        grid=(nb,),
        in_specs=[pl.BlockSpec((1, TOP_K, CB_TM), lambda i: (i, 0, 0), memory_space=pltpu.SMEM),
                  pl.BlockSpec((1, TOP_K, CB_TM), lambda i: (jnp.minimum(i + 1, nb - 1), 0, 0),
                               memory_space=pltpu.SMEM),
                  pl.BlockSpec(memory_space=pl.ANY),
                  pl.BlockSpec((CB_TM, LANES), lambda i: (i, 0)),
                  pl.BlockSpec((CB_TM, D_MODEL), lambda i: (i, 0)),
                  pl.BlockSpec((1, 1, D_MODEL), lambda i: (i // per_b, 0, 0)),
                  pl.BlockSpec((1, D_MODEL), lambda i: (0, 0)),
                  pl.BlockSpec((1, D_MODEL), lambda i: (0, 0))],
        out_specs=pl.BlockSpec((CB_TM, D_MODEL), lambda i: (i, 0)),
        out_shape=jax.ShapeDtypeStruct((t, D_MODEL), f32),
        scratch_shapes=[pltpu.VMEM((TOP_K * CB_TM * ROW_SUB, LANES), i32),
                        pltpu.VMEM((TOP_K * CB_TM * ROW_SUB, LANES), i32),
                        pltpu.SemaphoreType.DMA((2,))],
        compiler_params=_params(("arbitrary",)),
        name="combine",
    )(dest3, dest3, yb, rw, x1, g2, ln2g, ln2b)


def _t5_bucket(dist):
    d = dist.astype(f32)
    large = REL_MAX_EXACT + jnp.log(jnp.maximum(d, float(REL_MAX_EXACT)) / REL_MAX_EXACT) / math.log(
        REL_MAX_DIST / REL_MAX_EXACT) * (REL_BUCKETS - REL_MAX_EXACT)
    large = jnp.minimum(large.astype(i32), REL_BUCKETS - 1)
    return jnp.where(dist < REL_MAX_EXACT, dist, large)


def _bias_indices():
    qi = jnp.arange(ATT_BLOCK)[:, None]
    ki = jnp.arange(2 * ATT_BLOCK)[None, :]
    didx = qi + ATT_BLOCK - ki
    buckets, bands = [], []
    for win, dil in DIL_PAIRS:
        buckets.append(_t5_bucket(jnp.clip(didx, 0, None) * dil))
        bands.append(((didx >= 0) & (didx <= win // dil)).astype(i32))
    return jnp.stack(buckets).astype(i32), jnp.stack(bands)


def _residue_perm(tm, dil):
    n = tm // dil
    dst = np.arange(tm)
    src = (dst % n) * dil + dst // n
    return src[:, None] == np.arange(tm)[None, :]


def kernel(x, c, w_ada, b_ada, w_in, gm_ln_g, gm_ln_b, gm_w_s, gm_b_s, w_branch_a, w_branch_b, w_out,
           rel_bias, ln1_g, ln1_b, w_router, b_router, w_gate, b_gate, w_up, b_up, w_down, b_down,
           ln2_g, ln2_b):
    batch, seq, _ = x.shape
    t = batch * seq
    l = 0
    x2 = x.reshape(t, D_MODEL)

    c8 = jnp.pad(c, ((0, 8 - batch), (0, 0)))
    mod = _adaln(c8, w_ada[l], b_ada[l][None, :])[:batch]
    sh1, sc1, g1, sh2, sc2, g2 = [m[:, None, :] for m in jnp.split(mod, 6, axis=-1)]

    perms = [jnp.asarray(_residue_perm(IN_TM, dil), bf16) for _win, dil in DIL_PAIRS]
    bs_full = jnp.repeat(gm_b_s[l].T, GM_WIDTH // GM_GROUPS, axis=1)
    ya, gates, *qkvs = _inproj(x2, sc1, sh1, w_in[l].astype(bf16), perms,
                               gm_ln_g[l][None, :], gm_ln_b[l][None, :], gm_w_s[l], bs_full, batch, seq)

    bucket, band = _bias_indices()
    bias = _relbias(rel_bias, bucket, band)
    os_, ls_ = [], []
    for g, (_win, dil) in enumerate(DIL_PAIRS):
        o, lse = _attn_group(qkvs[g], bias, g, dil, batch, seq)
        os_.append(o)
        ls_.append(lse)

    wr = jnp.pad(w_router[l], ((0, 0), (0, LANES - N_EXPERTS)))
    wr_hi = wr.astype(bf16)
    wr_parts = jnp.concatenate([wr_hi, (wr - wr_hi.astype(f32)).astype(bf16)], axis=1)
    br = jnp.pad(b_router[l], (0, LANES - N_EXPERTS))[None, :]
    tri = jnp.asarray(np.arange(MIX_TM)[None, :] < np.arange(MIX_TM)[:, None], bf16)
    perms_t = [jnp.asarray(_residue_perm(MIX_TM, dil).T, bf16) for _win, dil in DIL_PAIRS]
    expand = np.arange(LANES)[:, None] == np.arange(ATT_WIDTH)[None, :] // HEAD_DIM
    expand = jnp.asarray(np.concatenate([expand, expand], axis=0), bf16)
    x1, h2, route, rw, cnt = _mix(
        os_, ls_, perms_t, expand, ya, gates, x2, g1, sc2, sh2,
        w_branch_a[l], w_branch_b[l], w_out[l],
        ln1_g[l][None, :], ln1_b[l][None, :], wr_parts, br, tri, seq)

    top_e = route[:, :TOP_K]
    rank = route[:, TOP_K:2 * TOP_K]
    counts = cnt[0, :N_EXPERTS].astype(i32)
    pcounts = (counts + MOE_TM - 1) // MOE_TM * MOE_TM
    experts = jnp.arange(N_EXPERTS, dtype=i32)
    upto = experts[None, :] <= experts[:, None]
    pends = jnp.sum(jnp.where(upto, pcounts[None, :], 0), axis=1)
    pstarts = pends - pcounts
    dest = jnp.sum(jnp.where(top_e[:, :, None] == experts, pstarts, 0), axis=-1) + rank
    ntile = t * TOP_K // MOE_TM + N_EXPERTS
    n_used = (pends[-1] // MOE_TM).reshape(1)
    tile_idx = jnp.minimum(jnp.arange(ntile, dtype=i32), n_used - 1)
    tile_e = jnp.sum((pends[None, :] <= (tile_idx * MOE_TM)[:, None]).astype(i32), axis=1)
    tile_first = jnp.concatenate([jnp.ones((1,), i32), (tile_e[1:] != tile_e[:-1]).astype(i32)])
    nonempty = counts > 0
    later = jnp.logical_and(experts[None, :] > experts[:, None], nonempty[None, :])
    next_nonempty = jnp.min(jnp.where(later, experts[None, :], N_EXPERTS), axis=1)
    next_nonempty = jnp.where(next_nonempty >= N_EXPERTS, -1, next_nonempty)
    expert_slot = (jnp.sum(jnp.logical_and(upto, nonempty[None, :]).astype(i32), axis=1) - 1) % 2
    of_tile = tile_e[:, None] == experts[None, :]
    tile_next = jnp.sum(jnp.where(of_tile, next_nonempty[None, :], 0), axis=1)
    tile_slot = jnp.sum(jnp.where(of_tile, expert_slot[None, :], 0), axis=1)
    tile_last = jnp.concatenate([(tile_e[1:] != tile_e[:-1]).astype(i32), jnp.zeros((1,), i32)])
    tile_last = jnp.where(tile_next >= 0, tile_last, 0)

    dest3 = dest.reshape(t // DISP_TM, DISP_TM, TOP_K).transpose(0, 2, 1)
    xs = _dispatch(pends, pcounts, n_used, dest3, h2, ntile)
    yb = _moe(tile_e, tile_first, tile_last, tile_next, tile_slot, n_used, xs,
              w_gate[l], b_gate[l][:, None, :], w_up[l], b_up[l][:, None, :],
              w_down[l], b_down[l][:, None, :])
    out = _combine(dest3, yb, rw, x1, g2, ln2_g[l][None, :], ln2_b[l][None, :], seq)
    return out.reshape(batch, seq, D_MODEL)
```
